```python
import jax, jax.numpy as jnp
from jax import lax
import numpy as np

D_MODEL = 1024
BATCH = 8
SEQ = 8192
DEPTH = 2

CHUNK = 64
GMLP_BLOCK = 128
A_HEADS = 4
A_HEAD_DIM = 128
D_A = A_HEADS * A_HEAD_DIM
B_GROUPS = 4
D_B = 512
CONV_WIDTH = 3
D_FF = 2816
N_BRANCH = 2
D_IN = 2 * D_A + 3 * D_B + N_BRANCH * D_MODEL
RMS_EPS = 1e-6
LN_EPS = 1e-5

kernel_name = "hybrid_gmlp_shortconv_gated_encoder"


def rms_norm(x, g):
    xf = x.astype(jnp.float32)
    y = xf * lax.rsqrt(jnp.mean(xf * xf, axis=-1, keepdims=True) + RMS_EPS)
    return (y * g.astype(jnp.float32)).astype(x.dtype)


def layer_norm(x, g, b):
    xf = x.astype(jnp.float32)
    mu = jnp.mean(xf, axis=-1, keepdims=True)
    var = jnp.mean(jnp.square(xf - mu), axis=-1, keepdims=True)
    y = (xf - mu) * lax.rsqrt(var + LN_EPS)
    return (y * g.astype(jnp.float32) + b.astype(jnp.float32)).astype(x.dtype)


def causal_dwconv(x, w):
    k, c = w.shape
    return lax.conv_general_dilated(
        x, w[:, None, :].astype(x.dtype), window_strides=(1,), padding=[(k - 1, 0)],
        dimension_numbers=("NWC", "WIO", "NWC"), feature_group_count=c)


def spatial_mask():
    idx = jnp.arange(GMLP_BLOCK) // CHUNK
    return idx[None, :] <= idx[:, None]


def gmlp_branch(u, v, ln_g, ln_b, w_s, b_s, mask):
    bsz, s, _ = u.shape
    u = jax.nn.gelu(u)
    v = layer_norm(jax.nn.gelu(v), ln_g, ln_b)
    vb = v.reshape(bsz, s // GMLP_BLOCK, GMLP_BLOCK, A_HEADS, A_HEAD_DIM)
    w_m = jnp.where(mask[None], w_s, jnp.zeros((), w_s.dtype))
    f = jnp.einsum("hij,bnjhd->bnihd", w_m, vb) + b_s.T[None, None, :, :, None]
    return u * f.reshape(bsz, s, D_A)


def _fwd_setup_inputs(seed: int = 0) -> dict:
    key = jax.random.key(seed)
    ks = jax.random.split(key, 20)
    n = jax.random.normal
    f32 = jnp.float32
    return {
        "x": n(ks[0], (BATCH, SEQ, D_MODEL), f32),
        "norm1_g": 1.0 + 0.02 * n(ks[1], (DEPTH, D_MODEL), f32),
        "w_in": n(ks[2], (DEPTH, D_MODEL, D_IN), f32) * D_MODEL ** -0.5,
        "b_gate": 0.02 * n(ks[3], (DEPTH, N_BRANCH * D_MODEL), f32),
        "gmlp_ln_g": 1.0 + 0.02 * n(ks[4], (DEPTH, D_A), f32),
        "gmlp_ln_b": 0.02 * n(ks[5], (DEPTH, D_A), f32),
        "w_spatial": n(ks[6], (DEPTH, A_HEADS, GMLP_BLOCK, GMLP_BLOCK), f32) * (0.5 * GMLP_BLOCK ** -0.5),
        "b_spatial": 1.0 + 0.02 * n(ks[7], (DEPTH, A_HEADS, GMLP_BLOCK), f32),
        "w_shortconv": n(ks[8], (DEPTH, CONV_WIDTH, D_B), f32) * CONV_WIDTH ** -0.5,
        "w_branch": n(ks[9], (DEPTH, N_BRANCH, D_A, D_MODEL), f32) * D_A ** -0.5,
        "w_out": n(ks[10], (DEPTH, D_MODEL, D_MODEL), f32) * D_MODEL ** -0.5,
        "norm2_g": 1.0 + 0.02 * n(ks[11], (DEPTH, D_MODEL), f32),
        "w_ffn_up": n(ks[12], (DEPTH, D_MODEL, 2 * D_FF), f32) * D_MODEL ** -0.5,
        "w_ffn_conv": n(ks[13], (DEPTH, CONV_WIDTH, D_FF), f32) * CONV_WIDTH ** -0.5,
        "b_ffn_conv": 0.02 * n(ks[14], (DEPTH, D_FF), f32),
        "w_ffn_down": n(ks[15], (DEPTH, D_FF, D_MODEL), f32) * D_FF ** -0.5,
        "final_g": 1.0 + 0.02 * n(ks[16], (D_MODEL,), f32),
    }


def _fwd_reference(x, norm1_g, w_in, b_gate, gmlp_ln_g, gmlp_ln_b, w_spatial, b_spatial,
              w_shortconv, w_branch, w_out, norm2_g, w_ffn_up, w_ffn_conv, b_ffn_conv,
              w_ffn_down, final_g):
    mask = spatial_mask()
    cuts = [D_A, 2 * D_A, 2 * D_A + D_B, 2 * D_A + 2 * D_B, 2 * D_A + 3 * D_B,
            2 * D_A + 3 * D_B + D_MODEL]
    for l in range(DEPTH):
        h = rms_norm(x, norm1_g[l])
        z = h @ w_in[l]
        u, v, bg, cg, hb, ga, gb = jnp.split(z, cuts, axis=-1)
        y_a = gmlp_branch(u, v, gmlp_ln_g[l], gmlp_ln_b[l], w_spatial[l], b_spatial[l], mask)
        y_b = bg * causal_dwconv(cg * hb, w_shortconv[l])
        g_a = jax.nn.sigmoid(ga + b_gate[l, :D_MODEL])
        g_b = jax.nn.sigmoid(gb + b_gate[l, D_MODEL:])
        merged = g_a * (y_a @ w_branch[l, 0]) + g_b * (y_b @ w_branch[l, 1])
        x = x + merged @ w_out[l]
        h = rms_norm(x, norm2_g[l])
        up = h @ w_ffn_up[l]
        gate, val = up[..., :D_FF], up[..., D_FF:]
        gate = causal_dwconv(gate, w_ffn_conv[l]) + b_ffn_conv[l]
        x = x + (jax.nn.silu(gate) * val) @ w_ffn_down[l]
    return rms_norm(x, final_g)


import jax as _jax
import jax.numpy as _jnp

TWIN_FORMAT = 'train_step'
FWD_PARAMS = ['x', 'norm1_g', 'w_in', 'b_gate', 'gmlp_ln_g', 'gmlp_ln_b', 'w_spatial', 'b_spatial', 'w_shortconv', 'w_branch', 'w_out', 'norm2_g', 'w_ffn_up', 'w_ffn_conv', 'b_ffn_conv', 'w_ffn_down', 'final_g']
TWIN_WEIGHTS = ['norm1_g', 'w_in', 'b_gate', 'gmlp_ln_g', 'gmlp_ln_b', 'w_spatial', 'b_spatial', 'w_shortconv', 'w_branch', 'w_out', 'norm2_g', 'w_ffn_up', 'w_ffn_conv', 'b_ffn_conv', 'w_ffn_down', 'final_g']
TWIN_DIFF_INPUT = 'x'
TWIN_INPUTS = ['x', 'norm1_g', 'w_in', 'b_gate', 'gmlp_ln_g', 'gmlp_ln_b', 'w_spatial', 'b_spatial', 'w_shortconv', 'w_branch', 'w_out', 'norm2_g', 'w_ffn_up', 'w_ffn_conv', 'b_ffn_conv', 'w_ffn_down', 'final_g', 'loss_target', 'm_norm1_g', 'm_w_in', 'm_b_gate', 'm_gmlp_ln_g', 'm_gmlp_ln_b', 'm_w_spatial', 'm_b_spatial', 'm_w_shortconv', 'm_w_branch', 'm_w_out', 'm_norm2_g', 'm_w_ffn_up', 'm_w_ffn_conv', 'm_b_ffn_conv', 'm_w_ffn_down', 'm_final_g', 'v_norm1_g', 'v_w_in', 'v_b_gate', 'v_gmlp_ln_g', 'v_gmlp_ln_b', 'v_w_spatial', 'v_b_spatial', 'v_w_shortconv', 'v_w_branch', 'v_w_out', 'v_norm2_g', 'v_w_ffn_up', 'v_w_ffn_conv', 'v_b_ffn_conv', 'v_w_ffn_down', 'v_final_g']
TWIN_OUTPUTS = ['loss', 'grad_x', 'grad_norm1_g', 'grad_w_in', 'grad_b_gate', 'grad_gmlp_ln_g', 'grad_gmlp_ln_b', 'grad_w_spatial', 'grad_b_spatial', 'grad_w_shortconv', 'grad_w_branch', 'grad_w_out', 'grad_norm2_g', 'grad_w_ffn_up', 'grad_w_ffn_conv', 'grad_b_ffn_conv', 'grad_w_ffn_down', 'grad_final_g', 'delta_norm1_g', 'delta_w_in', 'delta_b_gate', 'delta_gmlp_ln_g', 'delta_gmlp_ln_b', 'delta_w_spatial', 'delta_b_spatial', 'delta_w_shortconv', 'delta_w_branch', 'delta_w_out', 'delta_norm2_g', 'delta_w_ffn_up', 'delta_w_ffn_conv', 'delta_b_ffn_conv', 'delta_w_ffn_down', 'delta_final_g', 'new_m_norm1_g', 'new_m_w_in', 'new_m_b_gate', 'new_m_gmlp_ln_g', 'new_m_gmlp_ln_b', 'new_m_w_spatial', 'new_m_b_spatial', 'new_m_w_shortconv', 'new_m_w_branch', 'new_m_w_out', 'new_m_norm2_g', 'new_m_w_ffn_up', 'new_m_w_ffn_conv', 'new_m_b_ffn_conv', 'new_m_w_ffn_down', 'new_m_final_g', 'new_v_norm1_g', 'new_v_w_in', 'new_v_b_gate', 'new_v_gmlp_ln_g', 'new_v_gmlp_ln_b', 'new_v_w_spatial', 'new_v_b_spatial', 'new_v_w_shortconv', 'new_v_w_branch', 'new_v_w_out', 'new_v_norm2_g', 'new_v_w_ffn_up', 'new_v_w_ffn_conv', 'new_v_b_ffn_conv', 'new_v_w_ffn_down', 'new_v_final_g']
TWIN_LEAF_KINDS = {'loss': 'loss', 'grad_x': 'grad_x', 'grad_norm1_g': 'grad_w', 'grad_w_in': 'grad_w', 'grad_b_gate': 'grad_w', 'grad_gmlp_ln_g': 'grad_w', 'grad_gmlp_ln_b': 'grad_w', 'grad_w_spatial': 'grad_w', 'grad_b_spatial': 'grad_w', 'grad_w_shortconv': 'grad_w', 'grad_w_branch': 'grad_w', 'grad_w_out': 'grad_w', 'grad_norm2_g': 'grad_w', 'grad_w_ffn_up': 'grad_w', 'grad_w_ffn_conv': 'grad_w', 'grad_b_ffn_conv': 'grad_w', 'grad_w_ffn_down': 'grad_w', 'grad_final_g': 'grad_w', 'delta_norm1_g': 'delta_w', 'delta_w_in': 'delta_w', 'delta_b_gate': 'delta_w', 'delta_gmlp_ln_g': 'delta_w', 'delta_gmlp_ln_b': 'delta_w', 'delta_w_spatial': 'delta_w', 'delta_b_spatial': 'delta_w', 'delta_w_shortconv': 'delta_w', 'delta_w_branch': 'delta_w', 'delta_w_out': 'delta_w', 'delta_norm2_g': 'delta_w', 'delta_w_ffn_up': 'delta_w', 'delta_w_ffn_conv': 'delta_w', 'delta_b_ffn_conv': 'delta_w', 'delta_w_ffn_down': 'delta_w', 'delta_final_g': 'delta_w', 'new_m_norm1_g': 'new_m', 'new_m_w_in': 'new_m', 'new_m_b_gate': 'new_m', 'new_m_gmlp_ln_g': 'new_m', 'new_m_gmlp_ln_b': 'new_m', 'new_m_w_spatial': 'new_m', 'new_m_b_spatial': 'new_m', 'new_m_w_shortconv': 'new_m', 'new_m_w_branch': 'new_m', 'new_m_w_out': 'new_m', 'new_m_norm2_g': 'new_m', 'new_m_w_ffn_up': 'new_m', 'new_m_w_ffn_conv': 'new_m', 'new_m_b_ffn_conv': 'new_m', 'new_m_w_ffn_down': 'new_m', 'new_m_final_g': 'new_m', 'new_v_norm1_g': 'new_v', 'new_v_w_in': 'new_v', 'new_v_b_gate': 'new_v', 'new_v_gmlp_ln_g': 'new_v', 'new_v_gmlp_ln_b': 'new_v', 'new_v_w_spatial': 'new_v', 'new_v_b_spatial': 'new_v', 'new_v_w_shortconv': 'new_v', 'new_v_w_branch': 'new_v', 'new_v_w_out': 'new_v', 'new_v_norm2_g': 'new_v', 'new_v_w_ffn_up': 'new_v', 'new_v_w_ffn_conv': 'new_v', 'new_v_b_ffn_conv': 'new_v', 'new_v_w_ffn_down': 'new_v', 'new_v_final_g': 'new_v'}


def _forward(args):
    return _fwd_reference(*[args[k] for k in FWD_PARAMS])


def _output_shape():
    out = _jax.eval_shape(lambda: _forward(_fwd_setup_inputs(0)))
    return out.shape, out.dtype

N_MICROBATCH = 1
ADAM_LR = 0.001
ADAM_B1 = 0.9
ADAM_B2 = 0.999
ADAM_EPS = 1e-08
ADAM_WD = 0.01
ADAM_STEP = 10
PER_EXAMPLE_BATCH_AXIS = {'x': 0, 'loss_target': 0}
SHARED_INPUTS = []
_WEIGHT_DTYPES = {'norm1_g': _jnp.float32, 'w_in': _jnp.float32, 'b_gate': _jnp.float32, 'gmlp_ln_g': _jnp.float32, 'gmlp_ln_b': _jnp.float32, 'w_spatial': _jnp.float32, 'b_spatial': _jnp.float32, 'w_shortconv': _jnp.float32, 'w_branch': _jnp.float32, 'w_out': _jnp.float32, 'norm2_g': _jnp.float32, 'w_ffn_up': _jnp.float32, 'w_ffn_conv': _jnp.float32, 'b_ffn_conv': _jnp.float32, 'w_ffn_down': _jnp.float32, 'final_g': _jnp.float32}
MOMENT_SCALE = {'norm1_g': 2.530991e-01, 'w_in': 1.178925e-01, 'b_gate': 4.420814e-02, 'gmlp_ln_g': 5.106313e-02, 'gmlp_ln_b': 5.904875e-02, 'w_spatial': 1.035505e-01, 'b_spatial': 1.240392e-01, 'w_shortconv': 1.848784e-01, 'w_branch': 1.106867e-01, 'w_out': 1.569608e-01, 'norm2_g': 1.806465e-01, 'w_ffn_up': 7.455952e-02, 'w_ffn_conv': 7.485196e-02, 'b_ffn_conv': 7.121968e-02, 'w_ffn_down': 1.212540e-01, 'final_g': 6.406670e+01}


def _to_microbatches(a, axis):
    t = _jnp.moveaxis(a, axis, 0)
    t = t.reshape((N_MICROBATCH, t.shape[0] // N_MICROBATCH) + t.shape[1:])
    return _jnp.moveaxis(t, 1, axis + 1)


def setup_inputs(seed: int = 0) -> dict:
    inp = _fwd_setup_inputs(seed)
    key = _jax.random.fold_in(_jax.random.key(seed), 7919)
    shape, _ = _output_shape()
    out = dict(inp)
    out["loss_target"] = _jax.random.normal(_jax.random.fold_in(key, 0), shape, _jnp.float32)
    for i, name in enumerate(TWIN_WEIGHTS):
        w = inp[name].astype(_jnp.float32)
        if MOMENT_SCALE is None:
            s = _jnp.sqrt(_jnp.mean(_jnp.square(w)) + 1e-30)
        else:
            s = MOMENT_SCALE[name]
        km, kv = _jax.random.split(_jax.random.fold_in(key, i + 1))
        out[name] = w
        out["m_" + name] = s * _jax.random.normal(km, w.shape, _jnp.float32)
        out["v_" + name] = (s * s) * _jax.random.uniform(kv, w.shape, _jnp.float32, 0.5, 1.5)
    if N_MICROBATCH > 1:
        for name, axis in PER_EXAMPLE_BATCH_AXIS.items():
            out[name] = _to_microbatches(out[name], axis)
    return {'x': out['x'], 'norm1_g': out['norm1_g'], 'w_in': out['w_in'], 'b_gate': out['b_gate'], 'gmlp_ln_g': out['gmlp_ln_g'], 'gmlp_ln_b': out['gmlp_ln_b'], 'w_spatial': out['w_spatial'], 'b_spatial': out['b_spatial'], 'w_shortconv': out['w_shortconv'], 'w_branch': out['w_branch'], 'w_out': out['w_out'], 'norm2_g': out['norm2_g'], 'w_ffn_up': out['w_ffn_up'], 'w_ffn_conv': out['w_ffn_conv'], 'b_ffn_conv': out['b_ffn_conv'], 'w_ffn_down': out['w_ffn_down'], 'final_g': out['final_g'], 'loss_target': out['loss_target'], 'm_norm1_g': out['m_norm1_g'], 'm_w_in': out['m_w_in'], 'm_b_gate': out['m_b_gate'], 'm_gmlp_ln_g': out['m_gmlp_ln_g'], 'm_gmlp_ln_b': out['m_gmlp_ln_b'], 'm_w_spatial': out['m_w_spatial'], 'm_b_spatial': out['m_b_spatial'], 'm_w_shortconv': out['m_w_shortconv'], 'm_w_branch': out['m_w_branch'], 'm_w_out': out['m_w_out'], 'm_norm2_g': out['m_norm2_g'], 'm_w_ffn_up': out['m_w_ffn_up'], 'm_w_ffn_conv': out['m_w_ffn_conv'], 'm_b_ffn_conv': out['m_b_ffn_conv'], 'm_w_ffn_down': out['m_w_ffn_down'], 'm_final_g': out['m_final_g'], 'v_norm1_g': out['v_norm1_g'], 'v_w_in': out['v_w_in'], 'v_b_gate': out['v_b_gate'], 'v_gmlp_ln_g': out['v_gmlp_ln_g'], 'v_gmlp_ln_b': out['v_gmlp_ln_b'], 'v_w_spatial': out['v_w_spatial'], 'v_b_spatial': out['v_b_spatial'], 'v_w_shortconv': out['v_w_shortconv'], 'v_w_branch': out['v_w_branch'], 'v_w_out': out['v_w_out'], 'v_norm2_g': out['v_norm2_g'], 'v_w_ffn_up': out['v_w_ffn_up'], 'v_w_ffn_conv': out['v_w_ffn_conv'], 'v_b_ffn_conv': out['v_b_ffn_conv'], 'v_w_ffn_down': out['v_w_ffn_down'], 'v_final_g': out['v_final_g']}


def _loss(weights, diff, rest, loss_target):
    with _jax.named_scope("forward"):
        args = {**rest, TWIN_DIFF_INPUT: diff, **{k: w.astype(_WEIGHT_DTYPES[k]) for k, w in weights.items()}}
        y = _forward(args)
    with _jax.named_scope("loss_head"):
        err = _jnp.square(y.astype(_jnp.float32) - loss_target)
        return 0.5 * _jnp.sum(_jnp.mean(err, axis=-1)) if err.ndim else 0.5 * err


def _adamw(w, g, m, v):
    m = ADAM_B1 * m + (1.0 - ADAM_B1) * g
    v = ADAM_B2 * v + (1.0 - ADAM_B2) * _jnp.square(g)
    m_hat = m / (1.0 - ADAM_B1 ** ADAM_STEP)
    v_hat = v / (1.0 - ADAM_B2 ** ADAM_STEP)
    delta = -ADAM_LR * (m_hat / (_jnp.sqrt(v_hat) + ADAM_EPS) + ADAM_WD * w)
    return delta, m, v


def reference(x, norm1_g, w_in, b_gate, gmlp_ln_g, gmlp_ln_b, w_spatial, b_spatial, w_shortconv, w_branch, w_out, norm2_g, w_ffn_up, w_ffn_conv, b_ffn_conv, w_ffn_down, final_g, loss_target, m_norm1_g, m_w_in, m_b_gate, m_gmlp_ln_g, m_gmlp_ln_b, m_w_spatial, m_b_spatial, m_w_shortconv, m_w_branch, m_w_out, m_norm2_g, m_w_ffn_up, m_w_ffn_conv, m_b_ffn_conv, m_w_ffn_down, m_final_g, v_norm1_g, v_w_in, v_b_gate, v_gmlp_ln_g, v_gmlp_ln_b, v_w_spatial, v_b_spatial, v_w_shortconv, v_w_branch, v_w_out, v_norm2_g, v_w_ffn_up, v_w_ffn_conv, v_b_ffn_conv, v_w_ffn_down, v_final_g):
    given = dict(x=x, norm1_g=norm1_g, w_in=w_in, b_gate=b_gate, gmlp_ln_g=gmlp_ln_g, gmlp_ln_b=gmlp_ln_b, w_spatial=w_spatial, b_spatial=b_spatial, w_shortconv=w_shortconv, w_branch=w_branch, w_out=w_out, norm2_g=norm2_g, w_ffn_up=w_ffn_up, w_ffn_conv=w_ffn_conv, b_ffn_conv=b_ffn_conv, w_ffn_down=w_ffn_down, final_g=final_g, loss_target=loss_target, m_norm1_g=m_norm1_g, m_w_in=m_w_in, m_b_gate=m_b_gate, m_gmlp_ln_g=m_gmlp_ln_g, m_gmlp_ln_b=m_gmlp_ln_b, m_w_spatial=m_w_spatial, m_b_spatial=m_b_spatial, m_w_shortconv=m_w_shortconv, m_w_branch=m_w_branch, m_w_out=m_w_out, m_norm2_g=m_norm2_g, m_w_ffn_up=m_w_ffn_up, m_w_ffn_conv=m_w_ffn_conv, m_b_ffn_conv=m_b_ffn_conv, m_w_ffn_down=m_w_ffn_down, m_final_g=m_final_g, v_norm1_g=v_norm1_g, v_w_in=v_w_in, v_b_gate=v_b_gate, v_gmlp_ln_g=v_gmlp_ln_g, v_gmlp_ln_b=v_gmlp_ln_b, v_w_spatial=v_w_spatial, v_b_spatial=v_b_spatial, v_w_shortconv=v_w_shortconv, v_w_branch=v_w_branch, v_w_out=v_w_out, v_norm2_g=v_norm2_g, v_w_ffn_up=v_w_ffn_up, v_w_ffn_conv=v_w_ffn_conv, v_b_ffn_conv=v_b_ffn_conv, v_w_ffn_down=v_w_ffn_down, v_final_g=v_final_g)
    weights = {n: given[n] for n in TWIN_WEIGHTS}
    shared = {n: given[n] for n in SHARED_INPUTS}
    per_example = {n: given[n] for n in ['x']}
    grad_fn = _jax.value_and_grad(_loss, argnums=(0, 1))

    def one_microbatch(ex, loss_target):
        ex = dict(ex)
        diff = ex.pop(TWIN_DIFF_INPUT)
        return grad_fn(weights, diff, {**shared, **ex}, loss_target)

    if N_MICROBATCH == 1:
        loss, (grad_w, grad_x) = one_microbatch(per_example, given["loss_target"])
    else:
        def body(carry, xs):
            loss_sum, grad_sum = carry
            l_k, (gw_k, gx_k) = one_microbatch(xs[0], xs[1])
            with _jax.named_scope("update"):
                return (loss_sum + l_k, _jax.tree.map(_jnp.add, grad_sum, gw_k)), gx_k

        init = (_jnp.zeros((), _jnp.float32), _jax.tree.map(_jnp.zeros_like, weights))
        (loss, grad_w), grad_x = _jax.lax.scan(body, init, (per_example, given["loss_target"]))
    with _jax.named_scope("update"):
        delta_w, new_m, new_v = {}, {}, {}
        for n in TWIN_WEIGHTS:
            delta_w[n], new_m[n], new_v[n] = _adamw(weights[n], grad_w[n], given["m_" + n], given["v_" + n])
    return (loss, grad_x, *[grad_w[n] for n in TWIN_WEIGHTS], *[delta_w[n] for n in TWIN_WEIGHTS],
            *[new_m[n] for n in TWIN_WEIGHTS], *[new_v[n] for n in TWIN_WEIGHTS])
```

```python
import functools

import jax
import jax.numpy as jnp
from jax import lax
from jax.experimental import pallas as pl
from jax.experimental.pallas import tpu as pltpu

F32 = jnp.float32
BF16 = jnp.bfloat16

N_DEV = 8
DEPTH = 2
D_MODEL = 1024
D_A = 512
D_B = 512
D_FF = 2816
D_IN = 4608
N_HEADS = 4
HEAD = 128
GMLP_BLOCK = 128
CAUSAL_CHUNK = 64
OFF_U, OFF_V, OFF_BG, OFF_CG, OFF_HB, OFF_GA, OFF_GB = 0, 512, 1024, 1536, 2048, 2560, 3584
RMS_EPS = 1e-6
LN_EPS = 1e-5
ADAM_LR, ADAM_B1, ADAM_B2, ADAM_EPS, ADAM_WD, ADAM_STEP = 0.001, 0.9, 0.999, 1e-08, 0.01, 10

HALO = 16
FFN_CHUNK = 256
V7X_VMEM_BYTES = 64 << 20
VMEM_LIMIT = V7X_VMEM_BYTES - (8 << 20)
MESH_AXES = ("x", "y", "c")
GELU_C0 = 0.7978845608028654
GELU_C1 = 0.044715


def _params(*sem):
    return pltpu.CompilerParams(dimension_semantics=sem if sem else None, vmem_limit_bytes=VMEM_LIMIT)


def _dot(a, b):
    return jnp.dot(a, b, preferred_element_type=F32)


def _sigmoid(x):
    return 1.0 / (1.0 + jnp.exp(-x))


def _gelu_tanh(x):
    return jnp.tanh(GELU_C0 * (x + GELU_C1 * x * x * x))


def _shift_down(a, k, prev):
    p = prev.shape[0]
    r = pltpu.roll(a, k, 0)
    rid = lax.broadcasted_iota(jnp.int32, a.shape, 0)
    for j in range(k):
        r = jnp.where(rid == j, prev[p - k + j:p - k + j + 1, :], r)
    return r


def _shift_up(a, k, nxt):
    t = a.shape[0]
    r = pltpu.roll(a, t - k, 0)
    rid = lax.broadcasted_iota(jnp.int32, a.shape, 0)
    for j in range(k):
        r = jnp.where(rid == t - k + j, nxt[j:j + 1, :], r)
    return r


def _spatial_mask(transposed):
    ri = lax.broadcasted_iota(jnp.int32, (GMLP_BLOCK, GMLP_BLOCK), 0) // CAUSAL_CHUNK
    ci = lax.broadcasted_iota(jnp.int32, (GMLP_BLOCK, GMLP_BLOCK), 1) // CAUSAL_CHUNK
    return (ri <= ci) if transposed else (ci <= ri)


def _gmlp_forward(u, v, ln_g, ln_b, ws_ref, bs_ref, f_scr):
    tm = u.shape[0]
    tu = _gelu_tanh(u)
    tv = _gelu_tanh(v)
    gu = 0.5 * u * (1.0 + tu)
    gv = 0.5 * v * (1.0 + tv)
    mu = jnp.mean(gv, axis=-1, keepdims=True)
    cen = gv - mu
    rstd = lax.rsqrt(jnp.mean(cen * cen, axis=-1, keepdims=True) + LN_EPS)
    xh = cen * rstd
    vn = (xh * ln_g + ln_b).astype(BF16)
    mask = _spatial_mask(False)
    wm = [jnp.where(mask, ws_ref[h], 0.0).astype(BF16) for h in range(N_HEADS)]
    for b in range(tm // GMLP_BLOCK):
        rows = slice(b * GMLP_BLOCK, (b + 1) * GMLP_BLOCK)
        for h in range(N_HEADS):
            cols = slice(h * HEAD, (h + 1) * HEAD)
            f_scr[rows, cols] = _dot(wm[h], vn[rows, cols]) + bs_ref[h]
    return gu, tu, tv, xh, rstd, vn, f_scr[...]


def _norm_matmul(x, g, w, *, tm, name):
    n, d = x.shape
    c = w.shape[1]
    ch = 512

    def body(x_ref, g_ref, w_ref, h_ref, z_ref):
        xv = x_ref[...]
        r = lax.rsqrt(jnp.mean(xv * xv, axis=-1, keepdims=True) + RMS_EPS)
        h = (xv * r * g_ref[...]).astype(BF16)
        h_ref[...] = h
        for c0 in range(0, c, ch):
            z_ref[:, c0:c0 + ch] = _dot(h, w_ref[:, c0:c0 + ch]).astype(BF16)

    return pl.pallas_call(
        body, name=name, grid=(n // tm,),
        in_specs=[pl.BlockSpec((tm, d), lambda i: (i, 0)),
                  pl.BlockSpec((1, d), lambda i: (0, 0)),
                  pl.BlockSpec((d, c), lambda i: (0, 0))],
        out_specs=[pl.BlockSpec((tm, d), lambda i: (i, 0)),
                   pl.BlockSpec((tm, c), lambda i: (i, 0))],
        out_shape=[jax.ShapeDtypeStruct((n, d), BF16), jax.ShapeDtypeStruct((n, c), BF16)],
        compiler_params=_params("arbitrary"),
    )(x, g.reshape(1, d), w)


def _mix_forward(z, x, b_gate, ln_g, ln_b, w_s, b_s, w_sc, wb, w_out, *, tm, name):
    n = z.shape[0]
    hb = tm // HALO

    def body(z_ref, zp_ref, x_ref, bg_ref, lng_ref, lnb_ref, ws_ref, bs_ref, wsc_ref, wb_ref, wo_ref,
             ya_ref, yb_ref, pa_ref, pb_ref, mg_ref, x1_ref, f_scr):
        i = pl.program_id(0)
        u = z_ref[:, OFF_U:OFF_U + D_A].astype(F32)
        v = z_ref[:, OFF_V:OFF_V + D_A].astype(F32)
        gu, _, _, _, _, _, f = _gmlp_forward(u, v, lng_ref[...], lnb_ref[...], ws_ref, bs_ref, f_scr)
        ya = (gu * f).astype(BF16)
        ya_ref[...] = ya

        bgv = z_ref[:, OFF_BG:OFF_BG + D_B].astype(F32)
        q = z_ref[:, OFF_CG:OFF_CG + D_B].astype(F32) * z_ref[:, OFF_HB:OFF_HB + D_B].astype(F32)
        qp = zp_ref[:, OFF_CG:OFF_CG + D_B].astype(F32) * zp_ref[:, OFF_HB:OFF_HB + D_B].astype(F32)
        qp = jnp.where(i > 0, qp, jnp.zeros_like(qp))
        conv = wsc_ref[0:1, :] * _shift_down(q, 2, qp) + wsc_ref[1:2, :] * _shift_down(q, 1, qp) + wsc_ref[2:3, :] * q
        yb = (bgv * conv).astype(BF16)
        yb_ref[...] = yb

        pa = _dot(ya, wb_ref[0])
        pb = _dot(yb, wb_ref[1])
        pa_ref[...] = pa.astype(BF16)
        pb_ref[...] = pb.astype(BF16)
        sa = _sigmoid(z_ref[:, OFF_GA:OFF_GA + D_MODEL].astype(F32) + bg_ref[:, 0:D_MODEL])
        sb = _sigmoid(z_ref[:, OFF_GB:OFF_GB + D_MODEL].astype(F32) + bg_ref[:, D_MODEL:2 * D_MODEL])
        mg = (sa * pa + sb * pb).astype(BF16)
        mg_ref[...] = mg
        x1_ref[...] = x_ref[...] + _dot(mg, wo_ref[...])

    row = lambda w: pl.BlockSpec((tm, w), lambda i: (i, 0))
    full = lambda *s: pl.BlockSpec(s, lambda i: (0,) * len(s))
    return pl.pallas_call(
        body, name=name, grid=(n // tm,),
        in_specs=[row(D_IN),
                  pl.BlockSpec((HALO, D_IN), lambda i: (jnp.maximum(i * hb - 1, 0), 0)),
                  row(D_MODEL), full(1, 2 * D_MODEL), full(1, D_A), full(1, D_A),
                  full(N_HEADS, GMLP_BLOCK, GMLP_BLOCK), full(N_HEADS, GMLP_BLOCK, 1), full(3, D_B),
                  full(2, D_A, D_MODEL), full(D_MODEL, D_MODEL)],
        out_specs=[row(D_A), row(D_B), row(D_MODEL), row(D_MODEL), row(D_MODEL), row(D_MODEL)],
        out_shape=[jax.ShapeDtypeStruct((n, D_A), BF16), jax.ShapeDtypeStruct((n, D_B), BF16),
                   jax.ShapeDtypeStruct((n, D_MODEL), BF16), jax.ShapeDtypeStruct((n, D_MODEL), BF16),
                   jax.ShapeDtypeStruct((n, D_MODEL), BF16), jax.ShapeDtypeStruct((n, D_MODEL), F32)],
        scratch_shapes=[pltpu.VMEM((tm, D_A), F32)],
        compiler_params=_params("arbitrary"),
    )(z, z, x, b_gate.reshape(1, -1), ln_g.reshape(1, -1), ln_b.reshape(1, -1), w_s,
      b_s.reshape(N_HEADS, GMLP_BLOCK, 1), w_sc, wb, w_out)


def _ffn_forward(up, x1, w_fc, b_fc, w_down, *, tm, name):
    n = up.shape[0]
    hb = tm // HALO

    def body(up_ref, upp_ref, x1_ref, wfc_ref, bfc_ref, wd_ref, a_ref, x2_ref, acc):
        i = pl.program_id(0)
        acc[...] = x1_ref[...]
        for c0 in range(0, D_FF, FFN_CHUNK):
            cols = slice(c0, c0 + FFN_CHUNK)
            gate = up_ref[:, cols].astype(F32)
            val = up_ref[:, D_FF + c0:D_FF + c0 + FFN_CHUNK].astype(F32)
            gp = upp_ref[:, cols].astype(F32)
            gp = jnp.where(i > 0, gp, jnp.zeros_like(gp))
            gc = (wfc_ref[0:1, cols] * _shift_down(gate, 2, gp) + wfc_ref[1:2, cols] * _shift_down(gate, 1, gp)
                  + wfc_ref[2:3, cols] * gate + bfc_ref[:, cols])
            a = (gc * _sigmoid(gc) * val).astype(BF16)
            a_ref[:, cols] = a
            acc[...] += _dot(a, wd_ref[cols, :])
        x2_ref[...] = acc[...]

    return pl.pallas_call(
        body, name=name, grid=(n // tm,),
        in_specs=[pl.BlockSpec((tm, 2 * D_FF), lambda i: (i, 0)),
                  pl.BlockSpec((HALO, D_FF), lambda i: (jnp.maximum(i * hb - 1, 0), 0)),
                  pl.BlockSpec((tm, D_MODEL), lambda i: (i, 0)),
                  pl.BlockSpec((3, D_FF), lambda i: (0, 0)),
                  pl.BlockSpec((1, D_FF), lambda i: (0, 0)),
                  pl.BlockSpec((D_FF, D_MODEL), lambda i: (0, 0))],
        out_specs=[pl.BlockSpec((tm, D_FF), lambda i: (i, 0)),
                   pl.BlockSpec((tm, D_MODEL), lambda i: (i, 0))],
        out_shape=[jax.ShapeDtypeStruct((n, D_FF), BF16), jax.ShapeDtypeStruct((n, D_MODEL), F32)],
        scratch_shapes=[pltpu.VMEM((tm, D_MODEL), F32)],
        compiler_params=_params("arbitrary"),
    )(up, up, x1, w_fc, b_fc.reshape(1, -1), w_down)


def _loss_head(x, g, target, *, tm, name):
    n, d = x.shape

    def body(x_ref, g_ref, t_ref, dx_ref, loss_ref, dg_ref):
        i = pl.program_id(0)

        @pl.when(i == 0)
        def _():
            loss_ref[...] = jnp.zeros_like(loss_ref)
            dg_ref[...] = jnp.zeros_like(dg_ref)

        xv = x_ref[...]
        r = lax.rsqrt(jnp.mean(xv * xv, axis=-1, keepdims=True) + RMS_EPS)
        xh = xv * r
        gv = g_ref[...]
        e = xh * gv - t_ref[...]
        per_row = jnp.sum(e * e, axis=-1, keepdims=True) * (0.5 / d)
        loss_ref[...] += jnp.sum(per_row, axis=0, keepdims=True)
        dy = e * (1.0 / d)
        dg_ref[0:1, :] += jnp.sum(dy * xh, axis=0, keepdims=True)
        dxh = dy * gv
        dx_ref[...] = r * (dxh - xh * jnp.mean(dxh * xh, axis=-1, keepdims=True))

    return pl.pallas_call(
        body, name=name, grid=(n // tm,),
        in_specs=[pl.BlockSpec((tm, d), lambda i: (i, 0)),
                  pl.BlockSpec((1, d), lambda i: (0, 0)),
                  pl.BlockSpec((tm, d), lambda i: (i, 0))],
        out_specs=[pl.BlockSpec((tm, d), lambda i: (i, 0)),
                   pl.BlockSpec((1, 1), lambda i: (0, 0)),
                   pl.BlockSpec((8, d), lambda i: (0, 0))],
        out_shape=[jax.ShapeDtypeStruct((n, d), F32), jax.ShapeDtypeStruct((1, 1), F32),
                   jax.ShapeDtypeStruct((8, d), F32)],
        compiler_params=_params("arbitrary"),
    )(x, g.reshape(1, d), target)


def _ffn_backward(dx2, up, w_fc, b_fc, w_down_t, *, tm, name):
    n = up.shape[0]
    steps = n // tm
    hb = tm // HALO

    def body(dx_ref, dxn_ref, up_ref, upp_ref, upn_ref, wfc_ref, bfc_ref, wdt_ref, dup_ref, dwc_ref):
        i = pl.program_id(0)
        last = i == steps - 1

        @pl.when(i == 0)
        def _():
            dwc_ref[...] = jnp.zeros_like(dwc_ref)

        dxe = jnp.concatenate([dx_ref[...], dxn_ref[...]], axis=0).astype(BF16)
        for c0 in range(0, D_FF, FFN_CHUNK):
            cols = slice(c0, c0 + FFN_CHUNK)
            vcols = slice(D_FF + c0, D_FF + c0 + FFN_CHUNK)
            dae = _dot(dxe, wdt_ref[:, cols])
            da, dan = dae[:tm], dae[tm:]
            gate = up_ref[:, cols].astype(F32)
            val = up_ref[:, vcols].astype(F32)
            gp = upp_ref[:, cols].astype(F32)
            gp = jnp.where(i > 0, gp, jnp.zeros_like(gp))
            gn = upn_ref[:, cols].astype(F32)
            vn = upn_ref[:, vcols].astype(F32)
            w0, w1, w2, b = wfc_ref[0:1, cols], wfc_ref[1:2, cols], wfc_ref[2:3, cols], bfc_ref[:, cols]
            g1 = _shift_down(gate, 1, gp)
            g2 = _shift_down(gate, 2, gp)
            gc = w0 * g2 + w1 * g1 + w2 * gate + b
            s = _sigmoid(gc)
            dup_ref[:, vcols] = (da * (gc * s)).astype(BF16)
            dgc = da * val * (s * (1.0 + gc * (1.0 - s)))
            gcn = w0 * _shift_down(gn, 2, gate) + w1 * _shift_down(gn, 1, gate) + w2 * gn + b
            sn = _sigmoid(gcn)
            dgcn = dan * vn * (sn * (1.0 + gcn * (1.0 - sn)))
            dgcn = jnp.where(last, jnp.zeros_like(dgcn), dgcn)
            dgate = w2 * dgc + w1 * _shift_up(dgc, 1, dgcn) + w0 * _shift_up(dgc, 2, dgcn)
            dup_ref[:, cols] = dgate.astype(BF16)
            dwc_ref[0:1, cols] += jnp.sum(dgc * g2, axis=0, keepdims=True)
            dwc_ref[1:2, cols] += jnp.sum(dgc * g1, axis=0, keepdims=True)
            dwc_ref[2:3, cols] += jnp.sum(dgc * gate, axis=0, keepdims=True)
            dwc_ref[3:4, cols] += jnp.sum(dgc, axis=0, keepdims=True)

    nxt = lambda i: (jnp.minimum((i + 1) * hb, steps * hb - 1), 0)
    return pl.pallas_call(
        body, name=name, grid=(steps,),
        in_specs=[pl.BlockSpec((tm, D_MODEL), lambda i: (i, 0)),
                  pl.BlockSpec((HALO, D_MODEL), nxt),
                  pl.BlockSpec((tm, 2 * D_FF), lambda i: (i, 0)),
                  pl.BlockSpec((HALO, D_FF), lambda i: (jnp.maximum(i * hb - 1, 0), 0)),
                  pl.BlockSpec((HALO, 2 * D_FF), nxt),
                  pl.BlockSpec((3, D_FF), lambda i: (0, 0)),
                  pl.BlockSpec((1, D_FF), lambda i: (0, 0)),
                  pl.BlockSpec((D_MODEL, D_FF), lambda i: (0, 0))],
        out_specs=[pl.BlockSpec((tm, 2 * D_FF), lambda i: (i, 0)),
                   pl.BlockSpec((8, D_FF), lambda i: (0, 0))],
        out_shape=[jax.ShapeDtypeStruct((n, 2 * D_FF), BF16), jax.ShapeDtypeStruct((8, D_FF), F32)],
        compiler_params=_params("arbitrary"),
    )(dx2, dx2, up, up, up, w_fc, b_fc.reshape(1, -1), w_down_t)


def _matmul_norm_backward(dz, w_t, x, g, dres, *, tm, name):
    n, c = dz.shape
    d = x.shape[1]
    ch = 512

    def body(dz_ref, wt_ref, x_ref, g_ref, dres_ref, dx_ref, dg_ref):
        i = pl.program_id(0)

        @pl.when(i == 0)
        def _():
            dg_ref[...] = jnp.zeros_like(dg_ref)

        dh = _dot(dz_ref[:, 0:ch], wt_ref[0:ch, :])
        for c0 in range(ch, c, ch):
            dh += _dot(dz_ref[:, c0:c0 + ch], wt_ref[c0:c0 + ch, :])
        xv = x_ref[...]
        r = lax.rsqrt(jnp.mean(xv * xv, axis=-1, keepdims=True) + RMS_EPS)
        xh = xv * r
        dg_ref[0:1, :] += jnp.sum(dh * xh, axis=0, keepdims=True)
        dxh = dh * g_ref[...]
        dx_ref[...] = dres_ref[...] + r * (dxh - xh * jnp.mean(dxh * xh, axis=-1, keepdims=True))

    return pl.pallas_call(
        body, name=name, grid=(n // tm,),
        in_specs=[pl.BlockSpec((tm, c), lambda i: (i, 0)),
                  pl.BlockSpec((c, d), lambda i: (0, 0)),
                  pl.BlockSpec((tm, d), lambda i: (i, 0)),
                  pl.BlockSpec((1, d), lambda i: (0, 0)),
                  pl.BlockSpec((tm, d), lambda i: (i, 0))],
        out_specs=[pl.BlockSpec((tm, d), lambda i: (i, 0)),
                   pl.BlockSpec((8, d), lambda i: (0, 0))],
        out_shape=[jax.ShapeDtypeStruct((n, d), F32), jax.ShapeDtypeStruct((8, d), F32)],
        compiler_params=_params("arbitrary"),
    )(dz, w_t, x, g.reshape(1, d), dres)


def _mix_backward(dx1, z, pa, pb, b_gate, ln_g, ln_b, w_s, w_s_t, b_s, w_sc, w_out_t, wb_t, *, tm, name):
    n = z.shape[0]
    steps = n // tm
    hb = tm // HALO

    def body(dx_ref, dxn_ref, z_ref, zp_ref, zn_ref, pa_ref, pb_ref, bg_ref, lng_ref, lnb_ref, ws_ref, wst_ref,
             bs_ref, wsc_ref, wot_ref, wbt_ref,
             dz_ref, dpa_ref, dpb_ref, dbg_ref, dln_ref, dws_ref, dbs_ref, dwsc_ref, f_scr, dvn_scr):
        i = pl.program_id(0)
        last = i == steps - 1

        @pl.when(i == 0)
        def _():
            dbg_ref[...] = jnp.zeros_like(dbg_ref)
            dln_ref[...] = jnp.zeros_like(dln_ref)
            dws_ref[...] = jnp.zeros_like(dws_ref)
            dbs_ref[...] = jnp.zeros_like(dbs_ref)
            dwsc_ref[...] = jnp.zeros_like(dwsc_ref)

        dxe = jnp.concatenate([dx_ref[...], dxn_ref[...]], axis=0).astype(BF16)
        dmge = _dot(dxe, wot_ref[...])
        dmg, dmgn = dmge[:tm], dmge[tm:]

        pa_v = pa_ref[...].astype(F32)
        pb_v = pb_ref[...].astype(F32)
        sa = _sigmoid(z_ref[:, OFF_GA:OFF_GA + D_MODEL].astype(F32) + bg_ref[:, 0:D_MODEL])
        sb = _sigmoid(z_ref[:, OFF_GB:OFF_GB + D_MODEL].astype(F32) + bg_ref[:, D_MODEL:2 * D_MODEL])
        dpa = (dmg * sa).astype(BF16)
        dpb = dmg * sb
        dga = dmg * pa_v * sa * (1.0 - sa)
        dgb = dmg * pb_v * sb * (1.0 - sb)
        dpa_ref[...] = dpa
        dpb_ref[...] = dpb.astype(BF16)
        dz_ref[:, OFF_GA:OFF_GA + D_MODEL] = dga.astype(BF16)
        dz_ref[:, OFF_GB:OFF_GB + D_MODEL] = dgb.astype(BF16)
        dbg_ref[0:1, 0:D_MODEL] += jnp.sum(dga, axis=0, keepdims=True)
        dbg_ref[0:1, D_MODEL:2 * D_MODEL] += jnp.sum(dgb, axis=0, keepdims=True)

        dya = _dot(dpa, wbt_ref[0])
        u = z_ref[:, OFF_U:OFF_U + D_A].astype(F32)
        v = z_ref[:, OFF_V:OFF_V + D_A].astype(F32)
        ln_g = lng_ref[...]
        gu, tu, tv, xh, rstd, vn, f = _gmlp_forward(u, v, ln_g, lnb_ref[...], ws_ref, bs_ref, f_scr)
        dgu = dya * f
        df = dya * gu
        du = dgu * (0.5 * (1.0 + tu) + 0.5 * u * (1.0 - tu * tu) * GELU_C0 * (1.0 + 3.0 * GELU_C1 * u * u))
        dz_ref[:, OFF_U:OFF_U + D_A] = du.astype(BF16)
        df_bf = df.astype(BF16)
        mask = _spatial_mask(False)
        mask_t = _spatial_mask(True)
        wmt = [jnp.where(mask_t, wst_ref[h], 0.0).astype(BF16) for h in range(N_HEADS)]
        for b in range(tm // GMLP_BLOCK):
            rows = slice(b * GMLP_BLOCK, (b + 1) * GMLP_BLOCK)
            for h in range(N_HEADS):
                cols = slice(h * HEAD, (h + 1) * HEAD)
                dfb = df_bf[rows, cols]
                dvn_scr[rows, cols] = _dot(wmt[h], dfb)
                dws = lax.dot_general(dfb, vn[rows, cols], (((1,), (1,)), ((), ())), preferred_element_type=F32)
                dws_ref[h] += jnp.where(mask, dws, 0.0)
                dbs_ref[h] += jnp.sum(df[rows, cols], axis=1, keepdims=True)
        dvn = dvn_scr[...]
        dln_ref[0:1, :] += jnp.sum(dvn * xh, axis=0, keepdims=True)
        dln_ref[1:2, :] += jnp.sum(dvn, axis=0, keepdims=True)
        dxh = dvn * ln_g
        dgv = rstd * (dxh - jnp.mean(dxh, axis=-1, keepdims=True) - xh * jnp.mean(dxh * xh, axis=-1, keepdims=True))
        dv = dgv * (0.5 * (1.0 + tv) + 0.5 * v * (1.0 - tv * tv) * GELU_C0 * (1.0 + 3.0 * GELU_C1 * v * v))
        dz_ref[:, OFF_V:OFF_V + D_A] = dv.astype(BF16)

        sbn = _sigmoid(zn_ref[:, OFF_GB:OFF_GB + D_MODEL].astype(F32) + bg_ref[:, D_MODEL:2 * D_MODEL])
        dpbe = jnp.concatenate([dpb, dmgn * sbn], axis=0).astype(BF16)
        dybe = _dot(dpbe, wbt_ref[1])
        dyb, dybn = dybe[:tm], dybe[tm:]
        bgv = z_ref[:, OFF_BG:OFF_BG + D_B].astype(F32)
        cg = z_ref[:, OFF_CG:OFF_CG + D_B].astype(F32)
        hbv = z_ref[:, OFF_HB:OFF_HB + D_B].astype(F32)
        q = cg * hbv
        qp = zp_ref[:, OFF_CG:OFF_CG + D_B].astype(F32) * zp_ref[:, OFF_HB:OFF_HB + D_B].astype(F32)
        qp = jnp.where(i > 0, qp, jnp.zeros_like(qp))
        w0, w1, w2 = wsc_ref[0:1, :], wsc_ref[1:2, :], wsc_ref[2:3, :]
        q1 = _shift_down(q, 1, qp)
        q2 = _shift_down(q, 2, qp)
        conv = w0 * q2 + w1 * q1 + w2 * q
        dz_ref[:, OFF_BG:OFF_BG + D_B] = (dyb * conv).astype(BF16)
        dconv = dyb * bgv
        dconvn = dybn * zn_ref[:, OFF_BG:OFF_BG + D_B].astype(F32)
        dconvn = jnp.where(last, jnp.zeros_like(dconvn), dconvn)
        dwsc_ref[0:1, :] += jnp.sum(dconv * q2, axis=0, keepdims=True)
        dwsc_ref[1:2, :] += jnp.sum(dconv * q1, axis=0, keepdims=True)
        dwsc_ref[2:3, :] += jnp.sum(dconv * q, axis=0, keepdims=True)
        dq = w2 * dconv + w1 * _shift_up(dconv, 1, dconvn) + w0 * _shift_up(dconv, 2, dconvn)
        dz_ref[:, OFF_CG:OFF_CG + D_B] = (dq * hbv).astype(BF16)
        dz_ref[:, OFF_HB:OFF_HB + D_B] = (dq * cg).astype(BF16)

    row = lambda w: pl.BlockSpec((tm, w), lambda i: (i, 0))
    full = lambda *s: pl.BlockSpec(s, lambda i: (0,) * len(s))
    nxt = lambda i: (jnp.minimum((i + 1) * hb, steps * hb - 1), 0)
    prv = lambda i: (jnp.maximum(i * hb - 1, 0), 0)
    return pl.pallas_call(
        body, name=name, grid=(steps,),
        in_specs=[row(D_MODEL), pl.BlockSpec((HALO, D_MODEL), nxt),
                  row(D_IN), pl.BlockSpec((HALO, D_IN), prv), pl.BlockSpec((HALO, D_IN), nxt),
                  row(D_MODEL), row(D_MODEL),
                  full(1, 2 * D_MODEL), full(1, D_A), full(1, D_A),
                  full(N_HEADS, GMLP_BLOCK, GMLP_BLOCK), full(N_HEADS, GMLP_BLOCK, GMLP_BLOCK),
                  full(N_HEADS, GMLP_BLOCK, 1), full(3, D_B),
                  full(D_MODEL, D_MODEL), full(2, D_MODEL, D_A)],
        out_specs=[row(D_IN), row(D_MODEL), row(D_MODEL),
                   full(8, 2 * D_MODEL), full(8, D_A), full(N_HEADS, GMLP_BLOCK, GMLP_BLOCK),
                   full(N_HEADS, GMLP_BLOCK, 1), full(8, D_B)],
        out_shape=[jax.ShapeDtypeStruct((n, D_IN), BF16), jax.ShapeDtypeStruct((n, D_MODEL), BF16),
                   jax.ShapeDtypeStruct((n, D_MODEL), BF16),
                   jax.ShapeDtypeStruct((8, 2 * D_MODEL), F32), jax.ShapeDtypeStruct((8, D_A), F32),
                   jax.ShapeDtypeStruct((N_HEADS, GMLP_BLOCK, GMLP_BLOCK), F32),
                   jax.ShapeDtypeStruct((N_HEADS, GMLP_BLOCK, 1), F32), jax.ShapeDtypeStruct((8, D_B), F32)],
        scratch_shapes=[pltpu.VMEM((tm, D_A), F32), pltpu.VMEM((tm, D_A), F32)],
        compiler_params=_params("arbitrary"),
    )(dx1, dx1, z, z, z, pa, pb, b_gate.reshape(1, -1), ln_g.reshape(1, -1), ln_b.reshape(1, -1), w_s, w_s_t,
      b_s.reshape(N_HEADS, GMLP_BLOCK, 1), w_sc, w_out_t, wb_t)


def _matmul_tn(a, b, *, t1, t2, tn, name):
    n, k1 = a.shape
    k2 = b.shape[1]
    steps = n // tn

    def body(a_ref, b_ref, o_ref, acc):
        s = pl.program_id(2)

        @pl.when(s == 0)
        def _():
            acc[...] = jnp.zeros_like(acc)

        acc[...] += lax.dot_general(a_ref[...].astype(BF16), b_ref[...].astype(BF16), (((0,), (0,)), ((), ())),
                                    preferred_element_type=F32)

        @pl.when(s == steps - 1)
        def _():
            o_ref[...] = acc[...].astype(BF16)

    return pl.pallas_call(
        body, name=name, grid=(k1 // t1, k2 // t2, steps),
        in_specs=[pl.BlockSpec((tn, t1), lambda i, j, s: (s, i)),
                  pl.BlockSpec((tn, t2), lambda i, j, s: (s, j))],
        out_specs=pl.BlockSpec((t1, t2), lambda i, j, s: (i, j)),
        out_shape=jax.ShapeDtypeStruct((k1, k2), BF16),
        scratch_shapes=[pltpu.VMEM((t1, t2), F32)],
        compiler_params=_params("parallel", "parallel", "arbitrary"),
    )(a, b)


def _position():
    return lax.axis_index("x"), lax.axis_index("y"), lax.axis_index("c")


def _all_gather(arrs, *, name):
    n = len(arrs)
    mesh = pl.DeviceIdType.MESH

    def body(*refs):
        ins, outs = refs[:n], refs[n:2 * n]
        send_sems, recv_sems, local_sems = refs[2 * n:]
        x, y, c = _position()
        me, sibling = (x, y, c), (x, y, 1 - c)
        chips = [(1 - x, y), (x, 1 - y), (1 - x, 1 - y)]

        def slot(a, p):
            return outs[a].at[4 * p[0] + 2 * p[1] + p[2]]

        def copy(a, k, block, to, src=None):
            return pltpu.make_async_remote_copy(
                src_ref=slot(a, block) if src is None else src, dst_ref=slot(a, block),
                send_sem=send_sems.at[a, k], recv_sem=recv_sems.at[a, k], device_id=to, device_id_type=mesh)

        mine = [pltpu.make_async_copy(ins[a], slot(a, me), local_sems.at[a]) for a in range(n)]
        for cp in mine:
            cp.start()
        first = []
        for a in range(n):
            first.append(copy(a, 0, me, sibling, src=ins[a]))
            first += [copy(a, 1 + j, me, (*chip, c), src=ins[a]) for j, chip in enumerate(chips)]
        for cp in first:
            cp.start()
        passed = []
        for j, chip in enumerate(chips):
            for a in range(n):
                copy(a, 1 + j, (*chip, c), me).wait_recv()
                cp = copy(a, 4 + j, (*chip, c), sibling)
                cp.start()
                passed.append(cp)
        for a in range(n):
            copy(a, 0, sibling, me).wait_recv()
            for j, chip in enumerate(chips):
                copy(a, 4 + j, (*chip, 1 - c), me).wait_recv()
        for cp in first + passed:
            cp.wait_send()
        for cp in mine:
            cp.wait()

    any_spec = pl.BlockSpec(memory_space=pl.ANY)
    return pl.pallas_call(
        body, name=name,
        in_specs=[any_spec] * n, out_specs=[any_spec] * n,
        out_shape=[jax.ShapeDtypeStruct((N_DEV,) + a.shape, a.dtype) for a in arrs],
        scratch_shapes=[pltpu.SemaphoreType.DMA((n, 7)), pltpu.SemaphoreType.DMA((n, 7)),
                        pltpu.SemaphoreType.DMA((n,))],
    )(*arrs)


def _exchange(parts, *, name):
    n = len(parts)
    mesh = pl.DeviceIdType.MESH
    offsets = [(dx, dy, dc) for dx in (0, 1) for dy in (0, 1) for dc in (0, 1) if (dx, dy, dc) != (0, 0, 0)]

    def body(*refs):
        ins, outs = refs[:n], refs[n:2 * n]
        send_sems, recv_sems, local_sems = refs[2 * n:]
        x, y, c = _position()
        my_idx = 4 * x + 2 * y + c
        copies = []
        for a in range(n):
            cp = pltpu.make_async_copy(ins[a].at[my_idx], outs[a].at[my_idx], local_sems.at[a])
            cp.start()
            copies.append(cp)
        remote = []
        for k, (dx, dy, dc) in enumerate(offsets):
            px, py, pc = x ^ dx, y ^ dy, c ^ dc
            p_idx = 4 * px + 2 * py + pc
            for a in range(n):
                cp = pltpu.make_async_remote_copy(
                    src_ref=ins[a].at[p_idx], dst_ref=outs[a].at[my_idx],
                    send_sem=send_sems.at[a, k], recv_sem=recv_sems.at[a, k],
                    device_id=(px, py, pc), device_id_type=mesh)
                cp.start()
                remote.append((cp, a, k, p_idx))
        for cp, a, k, p_idx in remote:
            pltpu.make_async_remote_copy(
                src_ref=ins[a].at[p_idx], dst_ref=outs[a].at[p_idx],
                send_sem=send_sems.at[a, k], recv_sem=recv_sems.at[a, k],
                device_id=(x, y, c), device_id_type=mesh).wait_recv()
        for cp, _, _, _ in remote:
            cp.wait_send()
        for cp in copies:
            cp.wait()

    any_spec = pl.BlockSpec(memory_space=pl.ANY)
    return pl.pallas_call(
        body, name=name,
        in_specs=[any_spec] * n, out_specs=[any_spec] * n,
        out_shape=[jax.ShapeDtypeStruct(p.shape, p.dtype) for p in parts],
        scratch_shapes=[pltpu.SemaphoreType.DMA((n, 7)), pltpu.SemaphoreType.DMA((n, 7)),
                        pltpu.SemaphoreType.DMA((n,))],
    )(*parts)


def _all_reduce_small(p, *, name):
    rows, lanes = p.shape
    mesh = pl.DeviceIdType.MESH

    def body(p_ref, out_ref, buf, send_sems, recv_sems):
        x, y, c = _position()
        me, sibling = (x, y, c), (x, y, 1 - c)
        chips = [(1 - x, y), (x, 1 - y), (1 - x, 1 - y)]

        def slot(q):
            return buf.at[4 * q[0] + 2 * q[1] + q[2]]

        def copy(k, block, to, src=None):
            return pltpu.make_async_remote_copy(
                src_ref=slot(block) if src is None else src, dst_ref=slot(block),
                send_sem=send_sems.at[k], recv_sem=recv_sems.at[k], device_id=to, device_id_type=mesh)

        first = [copy(0, me, sibling, src=p_ref)]
        first += [copy(1 + j, me, (*chip, c), src=p_ref) for j, chip in enumerate(chips)]
        for cp in first:
            cp.start()
        passed = []
        for j, chip in enumerate(chips):
            copy(1 + j, (*chip, c), me).wait_recv()
            cp = copy(4 + j, (*chip, c), sibling)
            cp.start()
            passed.append(cp)
        copy(0, sibling, me).wait_recv()
        for j, chip in enumerate(chips):
            copy(4 + j, (*chip, 1 - c), me).wait_recv()
        for cp in first + passed:
            cp.wait_send()
        my_idx = 4 * x + 2 * y + c
        acc = jnp.zeros((rows, lanes), F32)
        for s in range(N_DEV):
            acc = acc + jnp.where(my_idx == s, p_ref[...], buf[s])
        out_ref[...] = acc

    return pl.pallas_call(
        body, name=name,
        in_specs=[pl.BlockSpec(memory_space=pltpu.VMEM)],
        out_specs=pl.BlockSpec(memory_space=pltpu.VMEM),
        out_shape=jax.ShapeDtypeStruct((rows, lanes), F32),
        scratch_shapes=[pltpu.VMEM((N_DEV, rows, lanes), F32),
                        pltpu.SemaphoreType.DMA((7,)), pltpu.SemaphoreType.DMA((7,))],
        compiler_params=pltpu.CompilerParams(vmem_limit_bytes=VMEM_LIMIT),
    )(p)


def _adamw_math(w, g, m, v):
    m = ADAM_B1 * m + (1.0 - ADAM_B1) * g
    v = ADAM_B2 * v + (1.0 - ADAM_B2) * (g * g)
    m_hat = m / (1.0 - ADAM_B1 ** ADAM_STEP)
    v_hat = v / (1.0 - ADAM_B2 ** ADAM_STEP)
    delta = -ADAM_LR * (m_hat / (jnp.sqrt(v_hat) + ADAM_EPS) + ADAM_WD * w)
    return delta, m, v


def _sum_adamw(recv, w, m, v, *, tr, name):
    r, c = w.shape

    def body(recv_ref, w_ref, m_ref, v_ref, g_ref, d_ref, nm_ref, nv_ref):
        g = recv_ref[0].astype(F32)
        for s in range(1, N_DEV):
            g = g + recv_ref[s].astype(F32)
        delta, nm, nv = _adamw_math(w_ref[...], g, m_ref[...], v_ref[...])
        g_ref[...] = g
        d_ref[...] = delta
        nm_ref[...] = nm
        nv_ref[...] = nv

    spec = pl.BlockSpec((tr, c), lambda i: (i, 0))
    return pl.pallas_call(
        body, name=name, grid=(r // tr,),
        in_specs=[pl.BlockSpec((N_DEV, tr, c), lambda i: (0, i, 0)), spec, spec, spec],
        out_specs=[spec] * 4,
        out_shape=[jax.ShapeDtypeStruct((r, c), F32)] * 4,
        compiler_params=_params("parallel"),
    )(recv, w, m, v)


def _adamw_small(w, g, m, v, *, name):
    def body(w_ref, g_ref, m_ref, v_ref, d_ref, nm_ref, nv_ref):
        delta, nm, nv = _adamw_math(w_ref[...], g_ref[...], m_ref[...], v_ref[...])
        d_ref[...] = delta
        nm_ref[...] = nm
        nv_ref[...] = nv

    vmem = pl.BlockSpec(memory_space=pltpu.VMEM)
    return pl.pallas_call(
        body, name=name, in_specs=[vmem] * 4, out_specs=[vmem] * 3,
        out_shape=[jax.ShapeDtypeStruct(w.shape, F32)] * 3,
    )(w, g, m, v)


def _pack(arrs):
    flat = jnp.concatenate([a.reshape(-1) for a in arrs])
    pad = (-flat.shape[0]) % 1024
    return jnp.pad(flat, (0, pad)).reshape(-1, 128)


def _unpack(packed, shapes):
    flat = packed.reshape(-1)
    out, o = [], 0
    for s in shapes:
        size = 1
        for d in s:
            size *= d
        out.append(flat[o:o + size].reshape(s))
        o += size
    return out


def _cols_gathered(g):
    return g.transpose(1, 0, 2).reshape(g.shape[1], N_DEV * g.shape[2])


def _cols_gathered_t(g):
    return g.transpose(0, 2, 1).reshape(N_DEV * g.shape[2], g.shape[1])


def _cols_to_parts(dw):
    l, k, c8 = dw.shape
    return dw.reshape(l, k, N_DEV, c8 // N_DEV).transpose(2, 0, 1, 3).reshape(N_DEV, l * k, c8 // N_DEV)


def kernel(x, norm1_g, w_in, b_gate, gmlp_ln_g, gmlp_ln_b, w_spatial, b_spatial, w_shortconv, w_branch, w_out, norm2_g, w_ffn_up, w_ffn_conv, b_ffn_conv, w_ffn_down, final_g, loss_target, m_norm1_g, m_w_in, m_b_gate, m_gmlp_ln_g, m_gmlp_ln_b, m_w_spatial, m_b_spatial, m_w_shortconv, m_w_branch, m_w_out, m_norm2_g, m_w_ffn_up, m_w_ffn_conv, m_b_ffn_conv, m_w_ffn_down, m_final_g, v_norm1_g, v_w_in, v_b_gate, v_gmlp_ln_g, v_gmlp_ln_b, v_w_spatial, v_b_spatial, v_w_shortconv, v_w_branch, v_w_out, v_norm2_g, v_w_ffn_up, v_w_ffn_conv, v_b_ffn_conv, v_w_ffn_down, v_final_g):
    n = x.shape[1]
    tm, tm_mix, tn = 512, 256, 1024
    x0 = x.reshape(n, D_MODEL)
    target = loss_target.reshape(n, D_MODEL)
    my_idx = 4 * lax.axis_index("x") + 2 * lax.axis_index("y") + lax.axis_index("c")

    conv_taps = jnp.concatenate([w_shortconv, w_ffn_conv], axis=-1)
    g_in, g_br, g_out, g_up, g_down, g_taps = _all_gather(
        [w_in.astype(BF16), w_branch.astype(BF16), w_out.astype(BF16), w_ffn_up.astype(BF16),
         w_ffn_down.astype(BF16), conv_taps], name="gather_weights")
    sc_w = D_B // N_DEV
    layers = []
    for l in range(DEPTH):
        br = g_br[:, l]
        wb = br.transpose(1, 2, 0, 3).reshape(2, D_A, D_MODEL)
        layers.append(dict(
            w_in=_cols_gathered(g_in[:, l]), w_in_t=_cols_gathered_t(g_in[:, l]),
            wb=wb, wb_t=wb.transpose(0, 2, 1),
            w_out=g_out[:, l].reshape(D_MODEL, D_MODEL), w_out_t=g_out[:, l].reshape(D_MODEL, D_MODEL).T,
            w_up=_cols_gathered(g_up[:, l]), w_up_t=_cols_gathered_t(g_up[:, l]),
            w_down=g_down[:, l].reshape(D_FF, D_MODEL), w_down_t=g_down[:, l].reshape(D_FF, D_MODEL).T,
            w_sc=_cols_gathered(g_taps[:, l, :, :sc_w]), w_fc=_cols_gathered(g_taps[:, l, :, sc_w:]),
            w_s_t=w_spatial[l].transpose(0, 2, 1)))

    saved = []
    xc = x0
    for l in range(DEPTH):
        p = layers[l]
        h, z = _norm_matmul(xc, norm1_g[l], p["w_in"], tm=tm, name=f"fwd_in_{l}")
        ya, yb, pa, pb, mg, x1 = _mix_forward(z, xc, b_gate[l], gmlp_ln_g[l], gmlp_ln_b[l], w_spatial[l], b_spatial[l],
                                              p["w_sc"], p["wb"], p["w_out"], tm=tm_mix, name=f"fwd_mix_{l}")
        h2, up = _norm_matmul(x1, norm2_g[l], p["w_up"], tm=tm, name=f"fwd_up_{l}")
        a, x2 = _ffn_forward(up, x1, p["w_fc"], b_ffn_conv[l], p["w_down"], tm=tm, name=f"fwd_ffn_{l}")
        saved.append(dict(x=xc, h=h, z=z, ya=ya, yb=yb, pa=pa, pb=pb, mg=mg, x1=x1, h2=h2, up=up, a=a))
        xc = x2
    dx, loss_part, dgf = _loss_head(xc, final_g, target, tm=tm, name="loss_head")

    dw_in, dw_br, dw_out, dw_up, dw_down = [None] * DEPTH, [None] * DEPTH, [None] * DEPTH, [None] * DEPTH, [None] * DEPTH
    small = [None] * DEPTH
    for l in reversed(range(DEPTH)):
        p, s = layers[l], saved[l]
        dup, dwc = _ffn_backward(dx, s["up"], p["w_fc"], b_ffn_conv[l], p["w_down_t"], tm=tm, name=f"bwd_ffn_{l}")
        dw_down[l] = _matmul_tn(s["a"], dx, t1=D_FF // 2, t2=D_MODEL, tn=tn, name=f"dw_down_{l}")
        dx1, dg2 = _matmul_norm_backward(dup, p["w_up_t"], s["x1"], norm2_g[l], dx, tm=tm, name=f"bwd_up_{l}")
        dw_up[l] = _matmul_tn(s["h2"], dup, t1=D_MODEL, t2=2 * D_FF // 4, tn=tn, name=f"dw_up_{l}")
        dz, dpa, dpb, dbg, dln, dws, dbs, dwsc = _mix_backward(
            dx1, s["z"], s["pa"], s["pb"], b_gate[l], gmlp_ln_g[l], gmlp_ln_b[l], w_spatial[l], p["w_s_t"],
            b_spatial[l], p["w_sc"], p["w_out_t"], p["wb_t"], tm=tm_mix, name=f"bwd_mix_{l}")
        dw_out[l] = _matmul_tn(s["mg"], dx1, t1=D_MODEL, t2=D_MODEL, tn=tn, name=f"dw_out_{l}")
        dw_br[l] = jnp.stack([
            _matmul_tn(s["ya"], dpa, t1=D_A, t2=D_MODEL, tn=tn, name=f"dw_branch_a_{l}"),
            _matmul_tn(s["yb"], dpb, t1=D_B, t2=D_MODEL, tn=tn, name=f"dw_branch_b_{l}")])
        dx0, dg1 = _matmul_norm_backward(dz, p["w_in_t"], s["x"], norm1_g[l], dx1, tm=tm, name=f"bwd_in_{l}")
        dw_in[l] = _matmul_tn(s["h"], dz, t1=D_MODEL, t2=D_IN // 4, tn=tn, name=f"dw_in_{l}")
        small[l] = dict(norm1_g=dg1[0], b_gate=dbg[0], gmlp_ln_g=dln[0], gmlp_ln_b=dln[1], w_spatial=dws,
                        b_spatial=dbs.reshape(N_HEADS, GMLP_BLOCK), w_shortconv=dwsc[0:3], norm2_g=dg2[0],
                        w_ffn_conv=dwc[0:3], b_ffn_conv=dwc[3])
        dx = dx0
    grad_x = dx.reshape(x.shape)

    parts = [_cols_to_parts(jnp.stack(dw_in)),
             _cols_to_parts(jnp.stack(dw_br).reshape(DEPTH, 2 * D_A, D_MODEL)),
             jnp.stack(dw_out).reshape(DEPTH, N_DEV, D_MODEL // N_DEV, D_MODEL).transpose(1, 0, 2, 3).reshape(N_DEV, -1, D_MODEL),
             _cols_to_parts(jnp.stack(dw_up)),
             jnp.stack(dw_down).reshape(DEPTH, N_DEV, D_FF // N_DEV, D_MODEL).transpose(1, 0, 2, 3).reshape(N_DEV, -1, D_MODEL)]
    recv = _exchange(parts, name="exchange_weight_grads")
    big_names = ["w_in", "w_branch", "w_out", "w_ffn_up", "w_ffn_down"]
    big_w = dict(w_in=(w_in, m_w_in, v_w_in), w_branch=(w_branch, m_w_branch, v_w_branch), w_out=(w_out, m_w_out, v_w_out),
                 w_ffn_up=(w_ffn_up, m_w_ffn_up, v_w_ffn_up), w_ffn_down=(w_ffn_down, m_w_ffn_down, v_w_ffn_down))
    row_tiles = dict(w_in=256, w_branch=512, w_out=64, w_ffn_up=256, w_ffn_down=176)
    results = {}
    for name, rc in zip(big_names, recv):
        w, m, v = big_w[name]
        r2 = rc.shape[1]
        flat = lambda t: t.reshape(r2, t.shape[-1])
        outs = _sum_adamw(rc, flat(w), flat(m), flat(v), tr=row_tiles[name], name=f"adamw_{name}")
        results[name] = tuple(o.reshape(w.shape) for o in outs)

    small_names = ["norm1_g", "b_gate", "gmlp_ln_g", "gmlp_ln_b", "w_spatial", "b_spatial", "w_shortconv", "norm2_g",
                   "w_ffn_conv", "b_ffn_conv"]
    stacked = [jnp.stack([small[l][k] for l in range(DEPTH)]) for k in small_names] + [dgf[0]]
    shapes = [a.shape for a in stacked]
    reduced = _unpack(_all_reduce_small(_pack(stacked), name="all_reduce_small_grads"), shapes)
    g_small = dict(zip(small_names + ["final_g"], reduced))
    g_small["w_shortconv"] = lax.dynamic_slice_in_dim(g_small["w_shortconv"], my_idx * sc_w, sc_w, axis=2)
    fc_w = D_FF // N_DEV
    g_small["w_ffn_conv"] = lax.dynamic_slice_in_dim(g_small["w_ffn_conv"], my_idx * fc_w, fc_w, axis=2)
    small_w = dict(norm1_g=(norm1_g, m_norm1_g, v_norm1_g), b_gate=(b_gate, m_b_gate, v_b_gate),
                   gmlp_ln_g=(gmlp_ln_g, m_gmlp_ln_g, v_gmlp_ln_g), gmlp_ln_b=(gmlp_ln_b, m_gmlp_ln_b, v_gmlp_ln_b),
                   w_spatial=(w_spatial, m_w_spatial, v_w_spatial), b_spatial=(b_spatial, m_b_spatial, v_b_spatial),
                   w_shortconv=(w_shortconv, m_w_shortconv, v_w_shortconv), norm2_g=(norm2_g, m_norm2_g, v_norm2_g),
                   w_ffn_conv=(w_ffn_conv, m_w_ffn_conv, v_w_ffn_conv), b_ffn_conv=(b_ffn_conv, m_b_ffn_conv, v_b_ffn_conv),
                   final_g=(final_g, m_final_g, v_final_g))
    order = small_names + ["final_g"]
    local_shapes = [small_w[k][0].shape for k in order]
    packed = [_pack([small_w[k][j] for k in order]) for j in range(3)]
    d_s, m_s, v_s = _adamw_small(packed[0], _pack([g_small[k] for k in order]), packed[1], packed[2], name="adamw_small")
    d_s, m_s, v_s = _unpack(d_s, local_shapes), _unpack(m_s, local_shapes), _unpack(v_s, local_shapes)
    for j, k in enumerate(order):
        results[k] = (g_small[k], d_s[j], m_s[j], v_s[j])

    loss = lax.psum(loss_part[0, 0], MESH_AXES)
    names = ["norm1_g", "w_in", "b_gate", "gmlp_ln_g", "gmlp_ln_b", "w_spatial", "b_spatial", "w_shortconv", "w_branch",
             "w_out", "norm2_g", "w_ffn_up", "w_ffn_conv", "b_ffn_conv", "w_ffn_down", "final_g"]
    return (loss, grad_x, *[results[k][0] for k in names], *[results[k][1] for k in names],
            *[results[k][2] for k in names], *[results[k][3] for k in names])
```

```python
import math

import jax
import jax.numpy as jnp
from jax import lax
from jax.experimental import pallas as pl
from jax.experimental.pallas import tpu as pltpu

F32 = jnp.float32
BF16 = jnp.bfloat16

N_DEV = 8
DEPTH = 2
D_MODEL = 1024
D_A = 512
D_B = 512
D_FF = 2816
D_IN = 4608
N_HEADS = 4
HEAD = 128
GMLP_BLOCK = 128
CAUSAL_CHUNK = 64
OFF_U, OFF_V, OFF_BG, OFF_CG, OFF_HB, OFF_GA, OFF_GB = 0, 512, 1024, 1536, 2048, 2560, 3584
RMS_EPS = 1e-6
LN_EPS = 1e-5
ADAM_LR, ADAM_B1, ADAM_B2, ADAM_EPS, ADAM_WD, ADAM_STEP = 0.001, 0.9, 0.999, 1e-08, 0.01, 10

SUBLANES = 8
HALO = 16
FFN_CHUNK = 256
V7X_VMEM_BYTES = 64 << 20
VMEM_LIMIT = V7X_VMEM_BYTES - (8 << 20)
MESH_AXES = ("x", "y", "c")
MESH = pl.DeviceIdType.MESH
GELU_C0 = 0.7978845608028654
GELU_C1 = 0.044715
NT = (((1,), (1,)), ((), ()))
TN = (((0,), (0,)), ((), ()))


def _dot(a, b):
    return jnp.dot(a, b, preferred_element_type=F32)


def _dot_nt(a, b):
    return lax.dot_general(a, b, NT, preferred_element_type=F32)


def _sigmoid(x):
    return 1.0 / (1.0 + jnp.exp(-x))


def _gelu_tanh(x):
    return jnp.tanh(GELU_C0 * (x + GELU_C1 * x * x * x))


def _gelu_grad(x, t):
    return 0.5 * (1.0 + t) + 0.5 * x * (1.0 - t * t) * GELU_C0 * (1.0 + 3.0 * GELU_C1 * x * x)


def _shift_down(a, k, prev):
    p = prev.shape[0]
    r = pltpu.roll(a, k, 0)
    head = r[0:SUBLANES]
    rid = lax.broadcasted_iota(jnp.int32, head.shape, 0)
    for j in range(k):
        head = jnp.where(rid == j, prev[p - k + j:p - k + j + 1, :], head)
    return jnp.concatenate([head, r[SUBLANES:]], axis=0)


def _shift_up(a, k, nxt):
    t = a.shape[0]
    r = pltpu.roll(a, t - k, 0)
    tail = r[t - SUBLANES:t]
    rid = lax.broadcasted_iota(jnp.int32, tail.shape, 0)
    for j in range(k):
        tail = jnp.where(rid == SUBLANES - k + j, nxt[j:j + 1, :], tail)
    return jnp.concatenate([r[0:t - SUBLANES], tail], axis=0)


def _spatial_mask(transposed):
    ri = lax.broadcasted_iota(jnp.int32, (GMLP_BLOCK, GMLP_BLOCK), 0) // CAUSAL_CHUNK
    ci = lax.broadcasted_iota(jnp.int32, (GMLP_BLOCK, GMLP_BLOCK), 1) // CAUSAL_CHUNK
    return (ri <= ci) if transposed else (ci <= ri)


def _gmlp_forward(u, v, ln_g, ln_b, ws_ref, bs_ref, f_scr):
    tm = u.shape[0]
    tu = _gelu_tanh(u)
    tv = _gelu_tanh(v)
    gu = 0.5 * u * (1.0 + tu)
    gv = 0.5 * v * (1.0 + tv)
    mu = jnp.mean(gv, axis=-1, keepdims=True)
    cen = gv - mu
    rstd = lax.rsqrt(jnp.mean(cen * cen, axis=-1, keepdims=True) + LN_EPS)
    xh = cen * rstd
    vn = (xh * ln_g + ln_b).astype(BF16)
    mask = _spatial_mask(False)
    wm = [jnp.where(mask, ws_ref[h], 0.0).astype(BF16) for h in range(N_HEADS)]
    for b in range(tm // GMLP_BLOCK):
        rows = slice(b * GMLP_BLOCK, (b + 1) * GMLP_BLOCK)
        for h in range(N_HEADS):
            cols = slice(h * HEAD, (h + 1) * HEAD)
            f_scr[rows, cols] = _dot(wm[h], vn[rows, cols]) + bs_ref[h]
    return gu, tu, tv, xh, rstd, vn, f_scr[...]


def _position():
    return lax.axis_index("x"), lax.axis_index("y"), lax.axis_index("c")


class _Gather:
    def __init__(self, arrays):
        self.arrays = list(arrays)
        self.out_shape = [jax.ShapeDtypeStruct((N_DEV,) + a.shape, a.dtype) for a in self.arrays]

    def _plan(self, ins, outs, sems):
        send_sems, recv_sems, local_sems = sems
        x, y, c = _position()
        me, sibling = (x, y, c), (x, y, 1 - c)
        chips = [(1 - x, y), (x, 1 - y), (1 - x, 1 - y)]

        def slot(a, p):
            return outs[a].at[4 * p[0] + 2 * p[1] + p[2]]

        def copy(a, k, block, to, src=None):
            return pltpu.make_async_remote_copy(
                src_ref=slot(a, block) if src is None else src, dst_ref=slot(a, block),
                send_sem=send_sems.at[a, k], recv_sem=recv_sems.at[a, k], device_id=to, device_id_type=MESH)

        n = len(self.arrays)

        def mine():
            return [pltpu.make_async_copy(ins[a], slot(a, me), local_sems.at[a]) for a in range(n)]

        def first():
            out = []
            for a in range(n):
                out.append(copy(a, 0, me, sibling, src=ins[a]))
                out += [copy(a, 1 + j, me, (*chip, c), src=ins[a]) for j, chip in enumerate(chips)]
            return out

        def arrivals():
            return [copy(a, 1 + j, (*chip, c), me) for j, chip in enumerate(chips) for a in range(n)]

        def relays():
            return [copy(a, 4 + j, (*chip, c), sibling) for j, chip in enumerate(chips) for a in range(n)]

        def from_sibling():
            out = [copy(a, 0, sibling, me) for a in range(n)]
            return out + [copy(a, 4 + j, (*chip, 1 - c), me) for j, chip in enumerate(chips) for a in range(n)]

        return mine, first, arrivals, relays, from_sibling

    def start(self, ins, outs, sems):
        mine, first, _, _, _ = self._plan(ins, outs, sems)
        for cp in mine() + first():
            cp.start()

    def relay(self, ins, outs, sems):
        _, _, arrivals, relays, _ = self._plan(ins, outs, sems)
        for arrived, onward in zip(arrivals(), relays()):
            arrived.wait_recv()
            onward.start()

    def finish(self, ins, outs, sems):
        mine, first, _, relays, from_sibling = self._plan(ins, outs, sems)
        for cp in from_sibling():
            cp.wait_recv()
        for cp in first() + relays():
            cp.wait_send()
        for cp in mine():
            cp.wait()


class _Exchange:
    def __init__(self, arrays):
        self.arrays = list(arrays)
        self.out_shape = [jax.ShapeDtypeStruct(a.shape, a.dtype) for a in self.arrays]

    def _plan(self, ins, outs, sems):
        send_sems, recv_sems, local_sems = sems
        x, y, c = _position()
        my_idx = 4 * x + 2 * y + c
        n = len(self.arrays)
        offsets = [(dx, dy, dc) for dx in (0, 1) for dy in (0, 1) for dc in (0, 1) if (dx, dy, dc) != (0, 0, 0)]

        def mine():
            return [pltpu.make_async_copy(ins[a].at[my_idx], outs[a].at[my_idx], local_sems.at[a]) for a in range(n)]

        def remote(arriving):
            out = []
            for k, (dx, dy, dc) in enumerate(offsets):
                px, py, pc = x ^ dx, y ^ dy, c ^ dc
                p_idx = 4 * px + 2 * py + pc
                for a in range(n):
                    out.append(pltpu.make_async_remote_copy(
                        src_ref=ins[a].at[p_idx], dst_ref=outs[a].at[p_idx if arriving else my_idx],
                        send_sem=send_sems.at[a, k], recv_sem=recv_sems.at[a, k],
                        device_id=(px, py, pc), device_id_type=MESH))
            return out

        return mine, remote

    def start(self, ins, outs, sems):
        mine, remote = self._plan(ins, outs, sems)
        for cp in mine() + remote(False):
            cp.start()

    def relay(self, ins, outs, sems):
        pass

    def finish(self, ins, outs, sems):
        mine, remote = self._plan(ins, outs, sems)
        for cp in remote(True):
            cp.wait_recv()
        for cp in remote(False):
            cp.wait_send()
        for cp in mine():
            cp.wait()


def _call(body, *, name, grid, in_specs, out_specs, out_shape, args, scratch_shapes=(), carry=None):
    n_in, n_out, n_scr = len(in_specs), len(out_specs), len(scratch_shapes)
    params = pltpu.CompilerParams(dimension_semantics=("arbitrary",) * len(grid), vmem_limit_bytes=VMEM_LIMIT)
    if carry is None:
        outs = pl.pallas_call(body, name=name, grid=grid, in_specs=in_specs, out_specs=out_specs, out_shape=out_shape,
                              scratch_shapes=list(scratch_shapes), compiler_params=params)(*args)
        return outs, None
    m = len(carry.arrays)
    total = math.prod(grid)

    def wrapped(*refs):
        ins, refs = refs[:n_in], refs[n_in:]
        c_ins, refs = refs[:m], refs[m:]
        outs, refs = refs[:n_out], refs[n_out:]
        c_outs, refs = refs[:m], refs[m:]
        scr, sems = refs[:n_scr], refs[n_scr:]
        flat = pl.program_id(0)
        for d in range(1, len(grid)):
            flat = flat * grid[d] + pl.program_id(d)

        @pl.when(flat == 0)
        def _():
            carry.start(c_ins, c_outs, sems)

        body(*ins, *outs, *scr)

        @pl.when(flat == total - 2)
        def _():
            carry.relay(c_ins, c_outs, sems)

        @pl.when(flat == total - 1)
        def _():
            carry.finish(c_ins, c_outs, sems)

    any_spec = pl.BlockSpec(memory_space=pl.ANY)
    sem_shapes = [pltpu.SemaphoreType.DMA((m, 7)), pltpu.SemaphoreType.DMA((m, 7)), pltpu.SemaphoreType.DMA((m,))]
    outs = pl.pallas_call(
        wrapped, name=name, grid=grid,
        in_specs=list(in_specs) + [any_spec] * m, out_specs=list(out_specs) + [any_spec] * m,
        out_shape=list(out_shape) + carry.out_shape,
        scratch_shapes=list(scratch_shapes) + sem_shapes, compiler_params=params)(*args, *carry.arrays)
    return outs[:n_out], outs[n_out:]


def _gather_now(arrays, *, name):
    carry = _Gather(arrays)
    m = len(arrays)

    def body(*refs):
        ins, outs, sems = refs[:m], refs[m:2 * m], refs[2 * m:]
        carry.start(ins, outs, sems)
        carry.relay(ins, outs, sems)
        carry.finish(ins, outs, sems)

    any_spec = pl.BlockSpec(memory_space=pl.ANY)
    return pl.pallas_call(
        body, name=name, in_specs=[any_spec] * m, out_specs=[any_spec] * m, out_shape=carry.out_shape,
        scratch_shapes=[pltpu.SemaphoreType.DMA((m, 7)), pltpu.SemaphoreType.DMA((m, 7)),
                        pltpu.SemaphoreType.DMA((m,))],
    )(*arrays)


def _all_reduce_small(p, *, name):
    rows, lanes = p.shape

    def body(p_ref, out_ref, buf, send_sems, recv_sems):
        x, y, c = _position()
        me, sibling = (x, y, c), (x, y, 1 - c)
        chips = [(1 - x, y), (x, 1 - y), (1 - x, 1 - y)]

        def slot(q):
            return buf.at[4 * q[0] + 2 * q[1] + q[2]]

        def copy(k, block, to, src=None):
            return pltpu.make_async_remote_copy(
                src_ref=slot(block) if src is None else src, dst_ref=slot(block),
                send_sem=send_sems.at[k], recv_sem=recv_sems.at[k], device_id=to, device_id_type=MESH)

        first = [copy(0, me, sibling, src=p_ref)]
        first += [copy(1 + j, me, (*chip, c), src=p_ref) for j, chip in enumerate(chips)]
        for cp in first:
            cp.start()
        passed = []
        for j, chip in enumerate(chips):
            copy(1 + j, (*chip, c), me).wait_recv()
            cp = copy(4 + j, (*chip, c), sibling)
            cp.start()
            passed.append(cp)
        copy(0, sibling, me).wait_recv()
        for j, chip in enumerate(chips):
            copy(4 + j, (*chip, 1 - c), me).wait_recv()
        for cp in first + passed:
            cp.wait_send()
        my_idx = 4 * x + 2 * y + c
        acc = jnp.zeros((rows, lanes), F32)
        for s in range(N_DEV):
            acc = acc + jnp.where(my_idx == s, p_ref[...], buf[s])
        out_ref[...] = acc

    return pl.pallas_call(
        body, name=name,
        in_specs=[pl.BlockSpec(memory_space=pltpu.VMEM)],
        out_specs=pl.BlockSpec(memory_space=pltpu.VMEM),
        out_shape=jax.ShapeDtypeStruct((rows, lanes), F32),
        scratch_shapes=[pltpu.VMEM((N_DEV, rows, lanes), F32),
                        pltpu.SemaphoreType.DMA((7,)), pltpu.SemaphoreType.DMA((7,))],
        compiler_params=pltpu.CompilerParams(vmem_limit_bytes=VMEM_LIMIT),
    )(p)


def _norm_matmul(x, g, w_t, *, tm, name, carry=None):
    n, d = x.shape
    c = w_t.shape[0]
    ch = 512

    def body(x_ref, g_ref, wt_ref, h_ref, z_ref):
        xv = x_ref[...]
        r = lax.rsqrt(jnp.mean(xv * xv, axis=-1, keepdims=True) + RMS_EPS)
        h = (xv * r * g_ref[...]).astype(BF16)
        h_ref[...] = h
        for c0 in range(0, c, ch):
            z_ref[:, c0:c0 + ch] = _dot_nt(h, wt_ref[c0:c0 + ch, :]).astype(BF16)

    return _call(
        body, name=name, grid=(n // tm,), carry=carry,
        in_specs=[pl.BlockSpec((tm, d), lambda i: (i, 0)),
                  pl.BlockSpec((1, d), lambda i: (0, 0)),
                  pl.BlockSpec((c, d), lambda i: (0, 0))],
        out_specs=[pl.BlockSpec((tm, d), lambda i: (i, 0)),
                   pl.BlockSpec((tm, c), lambda i: (i, 0))],
        out_shape=[jax.ShapeDtypeStruct((n, d), BF16), jax.ShapeDtypeStruct((n, c), BF16)],
        args=(x, g.reshape(1, d), w_t))


def _mix_forward(z, x, b_gate, ln_g, ln_b, w_s, b_s, w_sc, wb, w_out, *, tm, name, carry=None):
    n = z.shape[0]
    hb = tm // HALO

    def body(z_ref, zp_ref, x_ref, bg_ref, lng_ref, lnb_ref, ws_ref, bs_ref, wsc_ref, wb_ref, wo_ref,
             ya_ref, yb_ref, cv_ref, pa_ref, pb_ref, mg_ref, x1_ref, f_scr):
        i = pl.program_id(0)
        u = z_ref[:, OFF_U:OFF_U + D_A].astype(F32)
        v = z_ref[:, OFF_V:OFF_V + D_A].astype(F32)
        gu, _, _, _, _, _, f = _gmlp_forward(u, v, lng_ref[...], lnb_ref[...], ws_ref, bs_ref, f_scr)
        ya = (gu * f).astype(BF16)
        ya_ref[...] = ya

        bgv = z_ref[:, OFF_BG:OFF_BG + D_B].astype(F32)
        q = z_ref[:, OFF_CG:OFF_CG + D_B].astype(F32) * z_ref[:, OFF_HB:OFF_HB + D_B].astype(F32)
        qp = zp_ref[:, OFF_CG:OFF_CG + D_B].astype(F32) * zp_ref[:, OFF_HB:OFF_HB + D_B].astype(F32)
        qp = jnp.where(i > 0, qp, jnp.zeros_like(qp))
        conv = wsc_ref[0:1, :] * _shift_down(q, 2, qp) + wsc_ref[1:2, :] * _shift_down(q, 1, qp) + wsc_ref[2:3, :] * q
        cv_ref[...] = conv.astype(BF16)
        yb = (bgv * conv).astype(BF16)
        yb_ref[...] = yb

        pa = _dot(ya, wb_ref[0])
        pb = _dot(yb, wb_ref[1])
        pa_ref[...] = pa.astype(BF16)
        pb_ref[...] = pb.astype(BF16)
        sa = _sigmoid(z_ref[:, OFF_GA:OFF_GA + D_MODEL].astype(F32) + bg_ref[:, 0:D_MODEL])
        sb = _sigmoid(z_ref[:, OFF_GB:OFF_GB + D_MODEL].astype(F32) + bg_ref[:, D_MODEL:2 * D_MODEL])
        mg = (sa * pa + sb * pb).astype(BF16)
        mg_ref[...] = mg
        x1_ref[...] = x_ref[...] + _dot(mg, wo_ref[...])

    row = lambda w: pl.BlockSpec((tm, w), lambda i: (i, 0))
    full = lambda *s: pl.BlockSpec(s, lambda i: (0,) * len(s))
    bf = lambda w: jax.ShapeDtypeStruct((n, w), BF16)
    return _call(
        body, name=name, grid=(n // tm,), carry=carry,
        in_specs=[row(D_IN),
                  pl.BlockSpec((HALO, D_IN), lambda i: (jnp.maximum(i * hb - 1, 0), 0)),
                  row(D_MODEL), full(1, 2 * D_MODEL), full(1, D_A), full(1, D_A),
                  full(N_HEADS, GMLP_BLOCK, GMLP_BLOCK), full(N_HEADS, GMLP_BLOCK, 1), full(3, D_B),
                  full(2, D_A, D_MODEL), full(D_MODEL, D_MODEL)],
        out_specs=[row(D_A), row(D_B), row(D_B), row(D_MODEL), row(D_MODEL), row(D_MODEL), row(D_MODEL)],
        out_shape=[bf(D_A), bf(D_B), bf(D_B), bf(D_MODEL), bf(D_MODEL), bf(D_MODEL),
                   jax.ShapeDtypeStruct((n, D_MODEL), F32)],
        scratch_shapes=[pltpu.VMEM((tm, D_A), F32)],
        args=(z, z, x, b_gate.reshape(1, -1), ln_g.reshape(1, -1), ln_b.reshape(1, -1), w_s,
              b_s.reshape(N_HEADS, GMLP_BLOCK, 1), w_sc, wb, w_out))


def _ffn_forward(up, x1, w_fc, b_fc, w_down, *, tm, name, carry=None):
    n = up.shape[0]
    hb = tm // HALO

    def body(up_ref, upp_ref, x1_ref, wfc_ref, bfc_ref, wd_ref, gc_ref, a_ref, x2_ref, acc):
        i = pl.program_id(0)
        acc[...] = x1_ref[...]
        for c0 in range(0, D_FF, FFN_CHUNK):
            cols = slice(c0, c0 + FFN_CHUNK)
            gate = up_ref[:, cols].astype(F32)
            val = up_ref[:, D_FF + c0:D_FF + c0 + FFN_CHUNK].astype(F32)
            gp = upp_ref[:, cols].astype(F32)
            gp = jnp.where(i > 0, gp, jnp.zeros_like(gp))
            gc = (wfc_ref[0:1, cols] * _shift_down(gate, 2, gp) + wfc_ref[1:2, cols] * _shift_down(gate, 1, gp)
                  + wfc_ref[2:3, cols] * gate + bfc_ref[:, cols])
            gc_ref[:, cols] = gc.astype(BF16)
            a = (gc * _sigmoid(gc) * val).astype(BF16)
            a_ref[:, cols] = a
            acc[...] += _dot(a, wd_ref[cols, :])
        x2_ref[...] = acc[...]

    return _call(
        body, name=name, grid=(n // tm,), carry=carry,
        in_specs=[pl.BlockSpec((tm, 2 * D_FF), lambda i: (i, 0)),
                  pl.BlockSpec((HALO, D_FF), lambda i: (jnp.maximum(i * hb - 1, 0), 0)),
                  pl.BlockSpec((tm, D_MODEL), lambda i: (i, 0)),
                  pl.BlockSpec((3, D_FF), lambda i: (0, 0)),
                  pl.BlockSpec((1, D_FF), lambda i: (0, 0)),
                  pl.BlockSpec((D_FF, D_MODEL), lambda i: (0, 0))],
        out_specs=[pl.BlockSpec((tm, D_FF), lambda i: (i, 0)),
                   pl.BlockSpec((tm, D_FF), lambda i: (i, 0)),
                   pl.BlockSpec((tm, D_MODEL), lambda i: (i, 0))],
        out_shape=[jax.ShapeDtypeStruct((n, D_FF), BF16), jax.ShapeDtypeStruct((n, D_FF), BF16),
                   jax.ShapeDtypeStruct((n, D_MODEL), F32)],
        scratch_shapes=[pltpu.VMEM((tm, D_MODEL), F32)],
        args=(up, up, x1, w_fc, b_fc.reshape(1, -1), w_down))


def _loss_head(x, g, target, *, tm, name):
    n, d = x.shape

    def body(x_ref, g_ref, t_ref, dx_ref, loss_ref, dg_ref):
        i = pl.program_id(0)

        @pl.when(i == 0)
        def _():
            loss_ref[...] = jnp.zeros_like(loss_ref)
            dg_ref[...] = jnp.zeros_like(dg_ref)

        xv = x_ref[...]
        r = lax.rsqrt(jnp.mean(xv * xv, axis=-1, keepdims=True) + RMS_EPS)
        xh = xv * r
        gv = g_ref[...]
        e = xh * gv - t_ref[...]
        per_row = jnp.sum(e * e, axis=-1, keepdims=True) * (0.5 / d)
        loss_ref[...] += jnp.sum(per_row, axis=0, keepdims=True)
        dy = e * (1.0 / d)
        dg_ref[0:1, :] += jnp.sum(dy * xh, axis=0, keepdims=True)
        dxh = dy * gv
        dx_ref[...] = r * (dxh - xh * jnp.mean(dxh * xh, axis=-1, keepdims=True))

    outs, _ = _call(
        body, name=name, grid=(n // tm,),
        in_specs=[pl.BlockSpec((tm, d), lambda i: (i, 0)),
                  pl.BlockSpec((1, d), lambda i: (0, 0)),
                  pl.BlockSpec((tm, d), lambda i: (i, 0))],
        out_specs=[pl.BlockSpec((tm, d), lambda i: (i, 0)),
                   pl.BlockSpec((1, 1), lambda i: (0, 0)),
                   pl.BlockSpec((8, d), lambda i: (0, 0))],
        out_shape=[jax.ShapeDtypeStruct((n, d), F32), jax.ShapeDtypeStruct((1, 1), F32),
                   jax.ShapeDtypeStruct((8, d), F32)],
        args=(x, g.reshape(1, d), target))
    return outs


def _ffn_backward(dx2, up, gc, w_fc, w_down, *, tm, name, carry=None):
    n = up.shape[0]
    steps = n // tm
    hb = tm // HALO

    def body(dx_ref, dxn_ref, up_ref, upn_ref, gc_ref, gcn_ref, wfc_ref, wd_ref, dup_ref, dwc_ref):
        i = pl.program_id(0)
        last = i == steps - 1

        @pl.when(i == 0)
        def _():
            dwc_ref[...] = jnp.zeros_like(dwc_ref)

        dxe = jnp.concatenate([dx_ref[...], dxn_ref[...]], axis=0).astype(BF16)
        for c0 in range(0, D_FF, FFN_CHUNK):
            cols = slice(c0, c0 + FFN_CHUNK)
            vcols = slice(D_FF + c0, D_FF + c0 + FFN_CHUNK)
            dae = _dot_nt(dxe, wd_ref[cols, :])
            da, dan = dae[:tm], dae[tm:]
            gate = up_ref[:, cols].astype(F32)
            val = up_ref[:, vcols].astype(F32)
            gcv = gc_ref[:, cols].astype(F32)
            s = _sigmoid(gcv)
            dup_ref[:, vcols] = (da * (gcv * s)).astype(BF16)
            dgc = da * val * (s * (1.0 + gcv * (1.0 - s)))
            gcn = gcn_ref[:, cols].astype(F32)
            sn = _sigmoid(gcn)
            dgcn = dan * upn_ref[:, vcols].astype(F32) * (sn * (1.0 + gcn * (1.0 - sn)))
            dgcn = jnp.where(last, jnp.zeros_like(dgcn), dgcn)
            up1 = _shift_up(dgc, 1, dgcn)
            up2 = _shift_up(dgc, 2, dgcn)
            dgate = wfc_ref[2:3, cols] * dgc + wfc_ref[1:2, cols] * up1 + wfc_ref[0:1, cols] * up2
            dup_ref[:, cols] = dgate.astype(BF16)
            dwc_ref[0:1, cols] += jnp.sum(gate * up2, axis=0, keepdims=True)
            dwc_ref[1:2, cols] += jnp.sum(gate * up1, axis=0, keepdims=True)
            dwc_ref[2:3, cols] += jnp.sum(gate * dgc, axis=0, keepdims=True)
            dwc_ref[3:4, cols] += jnp.sum(dgc, axis=0, keepdims=True)

    nxt = lambda i: (jnp.minimum((i + 1) * hb, steps * hb - 1), 0)
    return _call(
        body, name=name, grid=(steps,), carry=carry,
        in_specs=[pl.BlockSpec((tm, D_MODEL), lambda i: (i, 0)),
                  pl.BlockSpec((HALO, D_MODEL), nxt),
                  pl.BlockSpec((tm, 2 * D_FF), lambda i: (i, 0)),
                  pl.BlockSpec((HALO, 2 * D_FF), nxt),
                  pl.BlockSpec((tm, D_FF), lambda i: (i, 0)),
                  pl.BlockSpec((HALO, D_FF), nxt),
                  pl.BlockSpec((3, D_FF), lambda i: (0, 0)),
                  pl.BlockSpec((D_FF, D_MODEL), lambda i: (0, 0))],
        out_specs=[pl.BlockSpec((tm, 2 * D_FF), lambda i: (i, 0)),
                   pl.BlockSpec((8, D_FF), lambda i: (0, 0))],
        out_shape=[jax.ShapeDtypeStruct((n, 2 * D_FF), BF16), jax.ShapeDtypeStruct((8, D_FF), F32)],
        args=(dx2, dx2, up, up, gc, gc, w_fc, w_down))


def _matmul_norm_backward(dz, w_t, x, g, dres, *, tm, name, carry=None):
    n, c = dz.shape
    d = x.shape[1]
    ch = 512

    def body(dz_ref, wt_ref, x_ref, g_ref, dres_ref, dx_ref, dg_ref):
        i = pl.program_id(0)

        @pl.when(i == 0)
        def _():
            dg_ref[...] = jnp.zeros_like(dg_ref)

        dh = _dot(dz_ref[:, 0:ch], wt_ref[0:ch, :])
        for c0 in range(ch, c, ch):
            dh += _dot(dz_ref[:, c0:c0 + ch], wt_ref[c0:c0 + ch, :])
        xv = x_ref[...]
        r = lax.rsqrt(jnp.mean(xv * xv, axis=-1, keepdims=True) + RMS_EPS)
        xh = xv * r
        dg_ref[0:1, :] += jnp.sum(dh * xh, axis=0, keepdims=True)
        dxh = dh * g_ref[...]
        dx_ref[...] = dres_ref[...] + r * (dxh - xh * jnp.mean(dxh * xh, axis=-1, keepdims=True))

    return _call(
        body, name=name, grid=(n // tm,), carry=carry,
        in_specs=[pl.BlockSpec((tm, c), lambda i: (i, 0)),
                  pl.BlockSpec((c, d), lambda i: (0, 0)),
                  pl.BlockSpec((tm, d), lambda i: (i, 0)),
                  pl.BlockSpec((1, d), lambda i: (0, 0)),
                  pl.BlockSpec((tm, d), lambda i: (i, 0))],
        out_specs=[pl.BlockSpec((tm, d), lambda i: (i, 0)),
                   pl.BlockSpec((8, d), lambda i: (0, 0))],
        out_shape=[jax.ShapeDtypeStruct((n, d), F32), jax.ShapeDtypeStruct((8, d), F32)],
        args=(dz, w_t, x, g.reshape(1, d), dres))


def _mix_backward(dx1, z, conv, pa, pb, b_gate, ln_g, ln_b, w_s, w_s_t, b_s, w_sc, w_out, wb, *, tm, name, carry=None):
    n = z.shape[0]
    steps = n // tm
    hb = tm // HALO

    def body(dx_ref, dxn_ref, z_ref, zn_ref, cv_ref, pa_ref, pb_ref, bg_ref, lng_ref, lnb_ref, ws_ref, wst_ref,
             bs_ref, wsc_ref, wo_ref, wb_ref,
             dz_ref, dpa_ref, dpb_ref, dbg_ref, dln_ref, dws_ref, dbs_ref, dwsc_ref, f_scr, dvn_scr):
        i = pl.program_id(0)
        last = i == steps - 1

        @pl.when(i == 0)
        def _():
            dbg_ref[...] = jnp.zeros_like(dbg_ref)
            dln_ref[...] = jnp.zeros_like(dln_ref)
            dws_ref[...] = jnp.zeros_like(dws_ref)
            dbs_ref[...] = jnp.zeros_like(dbs_ref)
            dwsc_ref[...] = jnp.zeros_like(dwsc_ref)

        dxe = jnp.concatenate([dx_ref[...], dxn_ref[...]], axis=0).astype(BF16)
        dmge = _dot_nt(dxe, wo_ref[...])
        dmg, dmgn = dmge[:tm], dmge[tm:]

        pa_v = pa_ref[...].astype(F32)
        pb_v = pb_ref[...].astype(F32)
        sa = _sigmoid(z_ref[:, OFF_GA:OFF_GA + D_MODEL].astype(F32) + bg_ref[:, 0:D_MODEL])
        sb = _sigmoid(z_ref[:, OFF_GB:OFF_GB + D_MODEL].astype(F32) + bg_ref[:, D_MODEL:2 * D_MODEL])
        dpa = (dmg * sa).astype(BF16)
        dpb = dmg * sb
        dga = dmg * pa_v * sa * (1.0 - sa)
        dgb = dmg * pb_v * sb * (1.0 - sb)
        dpa_ref[...] = dpa
        dpb_ref[...] = dpb.astype(BF16)
        dz_ref[:, OFF_GA:OFF_GA + D_MODEL] = dga.astype(BF16)
        dz_ref[:, OFF_GB:OFF_GB + D_MODEL] = dgb.astype(BF16)
        dbg_ref[0:1, 0:D_MODEL] += jnp.sum(dga, axis=0, keepdims=True)
        dbg_ref[0:1, D_MODEL:2 * D_MODEL] += jnp.sum(dgb, axis=0, keepdims=True)

        dya = _dot_nt(dpa, wb_ref[0])
        u = z_ref[:, OFF_U:OFF_U + D_A].astype(F32)
        v = z_ref[:, OFF_V:OFF_V + D_A].astype(F32)
        ln_g = lng_ref[...]
        gu, tu, tv, xh, rstd, vn, f = _gmlp_forward(u, v, ln_g, lnb_ref[...], ws_ref, bs_ref, f_scr)
        dgu = dya * f
        df = dya * gu
        dz_ref[:, OFF_U:OFF_U + D_A] = (dgu * _gelu_grad(u, tu)).astype(BF16)
        df_bf = df.astype(BF16)
        mask = _spatial_mask(False)
        mask_t = _spatial_mask(True)
        wmt = [jnp.where(mask_t, wst_ref[h], 0.0).astype(BF16) for h in range(N_HEADS)]
        for b in range(tm // GMLP_BLOCK):
            rows = slice(b * GMLP_BLOCK, (b + 1) * GMLP_BLOCK)
            for h in range(N_HEADS):
                cols = slice(h * HEAD, (h + 1) * HEAD)
                dfb = df_bf[rows, cols]
                dvn_scr[rows, cols] = _dot(wmt[h], dfb)
                dws_ref[h] += jnp.where(mask, _dot_nt(dfb, vn[rows, cols]), 0.0)
                dbs_ref[h] += jnp.sum(df[rows, cols], axis=1, keepdims=True)
        dvn = dvn_scr[...]
        dln_ref[0:1, :] += jnp.sum(dvn * xh, axis=0, keepdims=True)
        dln_ref[1:2, :] += jnp.sum(dvn, axis=0, keepdims=True)
        dxh = dvn * ln_g
        dgv = rstd * (dxh - jnp.mean(dxh, axis=-1, keepdims=True) - xh * jnp.mean(dxh * xh, axis=-1, keepdims=True))
        dz_ref[:, OFF_V:OFF_V + D_A] = (dgv * _gelu_grad(v, tv)).astype(BF16)

        sbn = _sigmoid(zn_ref[:, OFF_GB:OFF_GB + D_MODEL].astype(F32) + bg_ref[:, D_MODEL:2 * D_MODEL])
        dpbe = jnp.concatenate([dpb, dmgn * sbn], axis=0).astype(BF16)
        dybe = _dot_nt(dpbe, wb_ref[1])
        dyb, dybn = dybe[:tm], dybe[tm:]
        bgv = z_ref[:, OFF_BG:OFF_BG + D_B].astype(F32)
        cg = z_ref[:, OFF_CG:OFF_CG + D_B].astype(F32)
        hbv = z_ref[:, OFF_HB:OFF_HB + D_B].astype(F32)
        q = cg * hbv
        dz_ref[:, OFF_BG:OFF_BG + D_B] = (dyb * cv_ref[...].astype(F32)).astype(BF16)
        dconv = dyb * bgv
        dconvn = dybn * zn_ref[:, OFF_BG:OFF_BG + D_B].astype(F32)
        dconvn = jnp.where(last, jnp.zeros_like(dconvn), dconvn)
        up1 = _shift_up(dconv, 1, dconvn)
        up2 = _shift_up(dconv, 2, dconvn)
        dwsc_ref[0:1, :] += jnp.sum(q * up2, axis=0, keepdims=True)
        dwsc_ref[1:2, :] += jnp.sum(q * up1, axis=0, keepdims=True)
        dwsc_ref[2:3, :] += jnp.sum(q * dconv, axis=0, keepdims=True)
        dq = wsc_ref[2:3, :] * dconv + wsc_ref[1:2, :] * up1 + wsc_ref[0:1, :] * up2
        dz_ref[:, OFF_CG:OFF_CG + D_B] = (dq * hbv).astype(BF16)
        dz_ref[:, OFF_HB:OFF_HB + D_B] = (dq * cg).astype(BF16)

    row = lambda w: pl.BlockSpec((tm, w), lambda i: (i, 0))
    full = lambda *s: pl.BlockSpec(s, lambda i: (0,) * len(s))
    nxt = lambda i: (jnp.minimum((i + 1) * hb, steps * hb - 1), 0)
    return _call(
        body, name=name, grid=(steps,), carry=carry,
        in_specs=[row(D_MODEL), pl.BlockSpec((HALO, D_MODEL), nxt),
                  row(D_IN), pl.BlockSpec((HALO, D_IN), nxt),
                  row(D_B), row(D_MODEL), row(D_MODEL),
                  full(1, 2 * D_MODEL), full(1, D_A), full(1, D_A),
                  full(N_HEADS, GMLP_BLOCK, GMLP_BLOCK), full(N_HEADS, GMLP_BLOCK, GMLP_BLOCK),
                  full(N_HEADS, GMLP_BLOCK, 1), full(3, D_B),
                  full(D_MODEL, D_MODEL), full(2, D_A, D_MODEL)],
        out_specs=[row(D_IN), row(D_MODEL), row(D_MODEL),
                   full(8, 2 * D_MODEL), full(8, D_A), full(N_HEADS, GMLP_BLOCK, GMLP_BLOCK),
                   full(N_HEADS, GMLP_BLOCK, 1), full(8, D_B)],
        out_shape=[jax.ShapeDtypeStruct((n, D_IN), BF16), jax.ShapeDtypeStruct((n, D_MODEL), BF16),
                   jax.ShapeDtypeStruct((n, D_MODEL), BF16),
                   jax.ShapeDtypeStruct((8, 2 * D_MODEL), F32), jax.ShapeDtypeStruct((8, D_A), F32),
                   jax.ShapeDtypeStruct((N_HEADS, GMLP_BLOCK, GMLP_BLOCK), F32),
                   jax.ShapeDtypeStruct((N_HEADS, GMLP_BLOCK, 1), F32), jax.ShapeDtypeStruct((8, D_B), F32)],
        scratch_shapes=[pltpu.VMEM((tm, D_A), F32), pltpu.VMEM((tm, D_A), F32)],
        args=(dx1, dx1, z, z, conv, pa, pb, b_gate.reshape(1, -1), ln_g.reshape(1, -1), ln_b.reshape(1, -1), w_s, w_s_t,
              b_s.reshape(N_HEADS, GMLP_BLOCK, 1), w_sc, w_out, wb))


def _matmul_tn(a, b, *, t1, tn, name, carry=None):
    n, k1 = a.shape
    k2 = b.shape[1]
    steps = n // tn

    def body(a_ref, b_ref, o_ref, acc):
        s = pl.program_id(1)

        @pl.when(s == 0)
        def _():
            acc[...] = jnp.zeros_like(acc)

        acc[...] += lax.dot_general(a_ref[...].astype(BF16), b_ref[...].astype(BF16), TN, preferred_element_type=F32)

        @pl.when(s == steps - 1)
        def _():
            o_ref[...] = acc[...].astype(BF16)

    outs, carried = _call(
        body, name=name, grid=(k1 // t1, steps), carry=carry,
        in_specs=[pl.BlockSpec((tn, t1), lambda i, s: (s, i)),
                  pl.BlockSpec((tn, k2), lambda i, s: (s, 0))],
        out_specs=[pl.BlockSpec((t1, k2), lambda i, s: (i, 0))],
        out_shape=[jax.ShapeDtypeStruct((k1, k2), BF16)],
        scratch_shapes=[pltpu.VMEM((t1, k2), F32)],
        args=(a, b))
    return outs[0], carried


def _adamw_math(w, g, m, v):
    m = ADAM_B1 * m + (1.0 - ADAM_B1) * g
    v = ADAM_B2 * v + (1.0 - ADAM_B2) * (g * g)
    m_hat = m / (1.0 - ADAM_B1 ** ADAM_STEP)
    v_hat = v / (1.0 - ADAM_B2 ** ADAM_STEP)
    delta = -ADAM_LR * (m_hat / (jnp.sqrt(v_hat) + ADAM_EPS) + ADAM_WD * w)
    return delta, m, v


def _sum_parts(recvs, *, tr, name):
    _, r, c = recvs[0].shape

    def body(*refs):
        recv_refs, g_ref = refs[:DEPTH], refs[DEPTH]
        layer = pl.program_id(0)
        for l in range(DEPTH):
            @pl.when(layer == l)
            def _(l=l):
                g = recv_refs[l][0].astype(F32)
                for s in range(1, N_DEV):
                    g = g + recv_refs[l][s].astype(F32)
                g_ref[0] = g

    outs, _ = _call(
        body, name=name, grid=(DEPTH, r // tr),
        in_specs=[pl.BlockSpec((N_DEV, tr, c), lambda l, i: (0, i, 0))] * DEPTH,
        out_specs=[pl.BlockSpec((1, tr, c), lambda l, i: (l, i, 0))],
        out_shape=[jax.ShapeDtypeStruct((DEPTH, r, c), F32)],
        args=tuple(recvs))
    return outs[0]


def _adamw(w, g, m, v, *, tr, name):
    r, c = w.shape

    def body(w_ref, g_ref, m_ref, v_ref, d_ref, nm_ref, nv_ref):
        delta, nm, nv = _adamw_math(w_ref[...], g_ref[...], m_ref[...], v_ref[...])
        d_ref[...] = delta
        nm_ref[...] = nm
        nv_ref[...] = nv

    spec = pl.BlockSpec((tr, c), lambda i: (i, 0))
    outs, _ = _call(body, name=name, grid=(r // tr,), in_specs=[spec] * 4, out_specs=[spec] * 3,
                    out_shape=[jax.ShapeDtypeStruct((r, c), F32)] * 3, args=(w, g, m, v))
    return outs


def _sum_adamw(recvs, w, m, v, *, tr, name):
    _, r, c = recvs[0].shape

    def body(*refs):
        recv_refs = refs[:DEPTH]
        w_ref, m_ref, v_ref, g_ref, d_ref, nm_ref, nv_ref = refs[DEPTH:]
        layer = pl.program_id(0)
        for l in range(DEPTH):
            @pl.when(layer == l)
            def _(l=l):
                g = recv_refs[l][0].astype(F32)
                for s in range(1, N_DEV):
                    g = g + recv_refs[l][s].astype(F32)
                delta, nm, nv = _adamw_math(w_ref[0], g, m_ref[0], v_ref[0])
                g_ref[0] = g
                d_ref[0] = delta
                nm_ref[0] = nm
                nv_ref[0] = nv

    spec = pl.BlockSpec((1, tr, c), lambda l, i: (l, i, 0))
    outs, _ = _call(
        body, name=name, grid=(DEPTH, r // tr),
        in_specs=[pl.BlockSpec((N_DEV, tr, c), lambda l, i: (0, i, 0))] * DEPTH + [spec] * 3,
        out_specs=[spec] * 4, out_shape=[jax.ShapeDtypeStruct((DEPTH, r, c), F32)] * 4,
        args=tuple(recvs) + (w, m, v))
    return outs


def _adamw_small(w, g, m, v, *, name):
    def body(w_ref, g_ref, m_ref, v_ref, d_ref, nm_ref, nv_ref):
        delta, nm, nv = _adamw_math(w_ref[...], g_ref[...], m_ref[...], v_ref[...])
        d_ref[...] = delta
        nm_ref[...] = nm
        nv_ref[...] = nv

    vmem = pl.BlockSpec(memory_space=pltpu.VMEM)
    return pl.pallas_call(
        body, name=name, in_specs=[vmem] * 4, out_specs=[vmem] * 3,
        out_shape=[jax.ShapeDtypeStruct(w.shape, F32)] * 3,
    )(w, g, m, v)


def _pack(arrs):
    flat = jnp.concatenate([a.reshape(-1) for a in arrs])
    pad = (-flat.shape[0]) % 1024
    return jnp.pad(flat, (0, pad)).reshape(-1, 128)


def _unpack(packed, shapes):
    flat = packed.reshape(-1)
    out, o = [], 0
    for s in shapes:
        size = math.prod(s)
        out.append(flat[o:o + size].reshape(s))
        o += size
    return out


def _rows(gathered):
    return gathered.reshape(N_DEV * gathered.shape[1], gathered.shape[2])


def _parts(full):
    return full.reshape(N_DEV, full.shape[0] // N_DEV, full.shape[1])


def kernel(x, norm1_g, w_in, b_gate, gmlp_ln_g, gmlp_ln_b, w_spatial, b_spatial, w_shortconv, w_branch, w_out, norm2_g, w_ffn_up, w_ffn_conv, b_ffn_conv, w_ffn_down, final_g, loss_target, m_norm1_g, m_w_in, m_b_gate, m_gmlp_ln_g, m_gmlp_ln_b, m_w_spatial, m_b_spatial, m_w_shortconv, m_w_branch, m_w_out, m_norm2_g, m_w_ffn_up, m_w_ffn_conv, m_b_ffn_conv, m_w_ffn_down, m_final_g, v_norm1_g, v_w_in, v_b_gate, v_gmlp_ln_g, v_gmlp_ln_b, v_w_spatial, v_b_spatial, v_w_shortconv, v_w_branch, v_w_out, v_norm2_g, v_w_ffn_up, v_w_ffn_conv, v_b_ffn_conv, v_w_ffn_down, v_final_g):
    n = x.shape[1]
    tm, tm_mix, tn = 512, 256, 1024
    x0 = x.reshape(n, D_MODEL)
    target = loss_target.reshape(n, D_MODEL)
    my_idx = 4 * lax.axis_index("x") + 2 * lax.axis_index("y") + lax.axis_index("c")
    sc_w, fc_w = D_B // N_DEV, D_FF // N_DEV

    sh_in = [w_in[l].T.astype(BF16) for l in range(DEPTH)]
    sh_up = [w_ffn_up[l].T.astype(BF16) for l in range(DEPTH)]
    sh_br = [w_branch[l].astype(BF16) for l in range(DEPTH)]
    sh_out = [w_out[l].astype(BF16) for l in range(DEPTH)]
    sh_down = [w_ffn_down[l].astype(BF16) for l in range(DEPTH)]
    taps = jnp.concatenate([w_shortconv, w_ffn_conv], axis=-1)

    def branch_weights(g):
        return g.transpose(1, 2, 0, 3).reshape(2, D_A, D_MODEL)

    g_in0, g_taps = _gather_now([sh_in[0], taps], name="gather_first")
    w_sc = [g_taps[:, l, :, :sc_w].transpose(1, 0, 2).reshape(3, D_B) for l in range(DEPTH)]
    w_fc = [g_taps[:, l, :, sc_w:].transpose(1, 0, 2).reshape(3, D_FF) for l in range(DEPTH)]
    w_s_t = [w_spatial[l].transpose(0, 2, 1) for l in range(DEPTH)]
    weights = [dict(), dict()]
    weights[0]["in_t"] = _rows(g_in0)
    saved = []
    xc = x0
    for l in range(DEPTH):
        p = weights[l]
        carry = _Gather([sh_br[0], sh_out[0], sh_up[0]]) if l == 0 else None
        (h, z), got = _norm_matmul(xc, norm1_g[l], p["in_t"], tm=tm, name=f"fwd_in_{l}", carry=carry)
        if l == 0:
            p["wb"], p["out"], p["up_t"] = branch_weights(got[0]), _rows(got[1]), _rows(got[2])
        carry = _Gather([sh_down[0], sh_in[1]]) if l == 0 else None
        (ya, yb, conv, pa, pb, mg, x1), got = _mix_forward(
            z, xc, b_gate[l], gmlp_ln_g[l], gmlp_ln_b[l], w_spatial[l], b_spatial[l], w_sc[l], p["wb"], p["out"],
            tm=tm_mix, name=f"fwd_mix_{l}", carry=carry)
        if l == 0:
            p["down"], weights[1]["in_t"] = _rows(got[0]), _rows(got[1])
        carry = _Gather([sh_br[1], sh_out[1], sh_up[1]]) if l == 0 else None
        (h2, up), got = _norm_matmul(x1, norm2_g[l], p["up_t"], tm=tm, name=f"fwd_up_{l}", carry=carry)
        if l == 0:
            weights[1]["wb"], weights[1]["out"], weights[1]["up_t"] = branch_weights(got[0]), _rows(got[1]), _rows(got[2])
        carry = _Gather([sh_down[1]]) if l == 0 else None
        (gc, a, x2), got = _ffn_forward(up, x1, w_fc[l], b_ffn_conv[l], p["down"], tm=tm, name=f"fwd_ffn_{l}", carry=carry)
        if l == 0:
            weights[1]["down"] = _rows(got[0])
        saved.append(dict(x=xc, h=h, z=z, ya=ya, yb=yb, conv=conv, pa=pa, pb=pb, mg=mg, x1=x1, h2=h2, up=up, gc=gc, a=a))
        xc = x2
    dx, loss_part, dgf = _loss_head(xc, final_g, target, tm=tm, name="loss_head")

    recv = [dict(), dict()]
    small = [None] * DEPTH
    pending_in = None
    for l in reversed(range(DEPTH)):
        p, s = weights[l], saved[l]
        carry = _Exchange([pending_in]) if pending_in is not None else None
        (dup, dwc), got = _ffn_backward(dx, s["up"], s["gc"], w_fc[l], p["down"], tm=tm, name=f"bwd_ffn_{l}", carry=carry)
        if got is not None:
            recv[l + 1]["in_t"] = got[0]
        dw_down, _ = _matmul_tn(s["a"], dx, t1=D_FF // 2, tn=tn, name=f"dw_down_{l}")
        dw_up_t, got = _matmul_tn(dup, s["h2"], t1=2 * D_FF // 4, tn=tn, name=f"dw_up_{l}", carry=_Exchange([_parts(dw_down)]))
        recv[l]["down"] = got[0]
        (dx1, dg2), _ = _matmul_norm_backward(dup, p["up_t"], s["x1"], norm2_g[l], dx, tm=tm, name=f"bwd_up_{l}")
        (dz, dpa, dpb, dbg, dln, dws, dbs, dwsc), got = _mix_backward(
            dx1, s["z"], s["conv"], s["pa"], s["pb"], b_gate[l], gmlp_ln_g[l], gmlp_ln_b[l], w_spatial[l], w_s_t[l],
            b_spatial[l], w_sc[l], p["out"], p["wb"], tm=tm_mix, name=f"bwd_mix_{l}", carry=_Exchange([_parts(dw_up_t)]))
        recv[l]["up_t"] = got[0]
        dw_out, _ = _matmul_tn(s["mg"], dx1, t1=D_MODEL, tn=tn, name=f"dw_out_{l}")
        dw_bra_t, _ = _matmul_tn(dpa, s["ya"], t1=D_MODEL, tn=tn, name=f"dw_branch_a_{l}")
        dw_brb_t, _ = _matmul_tn(dpb, s["yb"], t1=D_MODEL, tn=tn, name=f"dw_branch_b_{l}")
        dw_in_t, got = _matmul_tn(dz, s["h"], t1=D_IN // 4, tn=tn, name=f"dw_in_{l}",
                                  carry=_Exchange([_parts(dw_out), _parts(dw_bra_t), _parts(dw_brb_t)]))
        recv[l]["out"], recv[l]["bra_t"], recv[l]["brb_t"] = got
        carry = _Exchange([_parts(dw_in_t)]) if l == 0 else None
        (dx0, dg1), got = _matmul_norm_backward(dz, p["in_t"], s["x"], norm1_g[l], dx1, tm=tm, name=f"bwd_in_{l}", carry=carry)
        if l == 0:
            recv[0]["in_t"] = got[0]
        else:
            pending_in = _parts(dw_in_t)
        small[l] = dict(norm1_g=dg1[0], b_gate=dbg[0], gmlp_ln_g=dln[0], gmlp_ln_b=dln[1], w_spatial=dws,
                        b_spatial=dbs.reshape(N_HEADS, GMLP_BLOCK), w_shortconv=dwsc[0:3], norm2_g=dg2[0],
                        w_ffn_conv=dwc[0:3], b_ffn_conv=dwc[3])
        dx = dx0
    grad_x = dx.reshape(x.shape)

    results = {}
    both = lambda key: [recv[l][key] for l in range(DEPTH)]
    g_in = _sum_parts(both("in_t"), tr=192, name="sum_w_in").transpose(0, 2, 1)
    g_up = _sum_parts(both("up_t"), tr=176, name="sum_w_ffn_up").transpose(0, 2, 1)
    g_bra = _sum_parts(both("bra_t"), tr=128, name="sum_w_branch_a").transpose(0, 2, 1)
    g_brb = _sum_parts(both("brb_t"), tr=128, name="sum_w_branch_b").transpose(0, 2, 1)
    g_br = jnp.stack([g_bra, g_brb], axis=1)
    for key, g, (w, m, v), tr in [("w_in", g_in, (w_in, m_w_in, v_w_in), 256),
                                  ("w_ffn_up", g_up, (w_ffn_up, m_w_ffn_up, v_w_ffn_up), 256),
                                  ("w_branch", g_br, (w_branch, m_w_branch, v_w_branch), 512)]:
        flat = lambda t: t.reshape(-1, t.shape[-1])
        outs = _adamw(flat(w), flat(g), flat(m), flat(v), tr=tr, name=f"adamw_{key}")
        results[key] = (g,) + tuple(o.reshape(w.shape) for o in outs)
    results["w_out"] = tuple(_sum_adamw(both("out"), w_out, m_w_out, v_w_out, tr=128, name="adamw_w_out"))
    results["w_ffn_down"] = tuple(_sum_adamw(both("down"), w_ffn_down, m_w_ffn_down, v_w_ffn_down, tr=176,
                                             name="adamw_w_ffn_down"))

    small_names = ["norm1_g", "b_gate", "gmlp_ln_g", "gmlp_ln_b", "w_spatial", "b_spatial", "w_shortconv", "norm2_g",
                   "w_ffn_conv", "b_ffn_conv"]
    stacked = [jnp.stack([small[l][k] for l in range(DEPTH)]) for k in small_names] + [dgf[0]]
    shapes = [a.shape for a in stacked]
    reduced = _unpack(_all_reduce_small(_pack(stacked), name="all_reduce_small_grads"), shapes)
    g_small = dict(zip(small_names + ["final_g"], reduced))
    g_small["w_shortconv"] = lax.dynamic_slice_in_dim(g_small["w_shortconv"], my_idx * sc_w, sc_w, axis=2)
    g_small["w_ffn_conv"] = lax.dynamic_slice_in_dim(g_small["w_ffn_conv"], my_idx * fc_w, fc_w, axis=2)
    small_w = dict(norm1_g=(norm1_g, m_norm1_g, v_norm1_g), b_gate=(b_gate, m_b_gate, v_b_gate),
                   gmlp_ln_g=(gmlp_ln_g, m_gmlp_ln_g, v_gmlp_ln_g), gmlp_ln_b=(gmlp_ln_b, m_gmlp_ln_b, v_gmlp_ln_b),
                   w_spatial=(w_spatial, m_w_spatial, v_w_spatial), b_spatial=(b_spatial, m_b_spatial, v_b_spatial),
                   w_shortconv=(w_shortconv, m_w_shortconv, v_w_shortconv), norm2_g=(norm2_g, m_norm2_g, v_norm2_g),
                   w_ffn_conv=(w_ffn_conv, m_w_ffn_conv, v_w_ffn_conv), b_ffn_conv=(b_ffn_conv, m_b_ffn_conv, v_b_ffn_conv),
                   final_g=(final_g, m_final_g, v_final_g))
    order = small_names + ["final_g"]
    local_shapes = [small_w[k][0].shape for k in order]
    packed = [_pack([small_w[k][j] for k in order]) for j in range(3)]
    d_s, m_s, v_s = _adamw_small(packed[0], _pack([g_small[k] for k in order]), packed[1], packed[2], name="adamw_small")
    d_s, m_s, v_s = _unpack(d_s, local_shapes), _unpack(m_s, local_shapes), _unpack(v_s, local_shapes)
    for j, k in enumerate(order):
        results[k] = (g_small[k], d_s[j], m_s[j], v_s[j])

    loss = lax.psum(loss_part[0, 0], MESH_AXES)
    names = ["norm1_g", "w_in", "b_gate", "gmlp_ln_g", "gmlp_ln_b", "w_spatial", "b_spatial", "w_shortconv", "w_branch",
             "w_out", "norm2_g", "w_ffn_up", "w_ffn_conv", "b_ffn_conv", "w_ffn_down", "final_g"]
    return (loss, grad_x, *[results[k][0] for k in names], *[results[k][1] for k in names],
            *[results[k][2] for k in names], *[results[k][3] for k in names])
```

```python
import math

import jax
import jax.numpy as jnp
from jax import lax
from jax.experimental import pallas as pl
from jax.experimental.pallas import tpu as pltpu

F32 = jnp.float32
BF16 = jnp.bfloat16

N_DEV = 8
DEPTH = 2
D_MODEL = 1024
D_A = 512
D_B = 512
D_FF = 2816
D_IN = 4608
N_HEADS = 4
HEAD = 128
GMLP_BLOCK = 128
CAUSAL_CHUNK = 64
OFF_U, OFF_V, OFF_BG, OFF_CG, OFF_HB, OFF_GA, OFF_GB = 0, 512, 1024, 1536, 2048, 2560, 3584
RMS_EPS = 1e-6
LN_EPS = 1e-5
ADAM_LR, ADAM_B1, ADAM_B2, ADAM_EPS, ADAM_WD, ADAM_STEP = 0.001, 0.9, 0.999, 1e-08, 0.01, 10

SUBLANES = 8
HALO = 16
FFN_CHUNK = 256
V7X_VMEM_BYTES = 64 << 20
VMEM_LIMIT = V7X_VMEM_BYTES - (8 << 20)
MESH_AXES = ("x", "y", "c")
MESH = pl.DeviceIdType.MESH
GELU_C0 = 0.7978845608028654
GELU_C1 = 0.044715
NT = (((1,), (1,)), ((), ()))
TN = (((0,), (0,)), ((), ()))


def _dot(a, b):
    return jnp.dot(a, b, preferred_element_type=F32)


def _dot_nt(a, b):
    return lax.dot_general(a, b, NT, preferred_element_type=F32)


def _sigmoid(x):
    return 1.0 / (1.0 + jnp.exp(-x))


def _gelu_tanh(x):
    return jnp.tanh(GELU_C0 * (x + GELU_C1 * x * x * x))


def _gelu_grad(x, t):
    return 0.5 * (1.0 + t) + 0.5 * x * (1.0 - t * t) * GELU_C0 * (1.0 + 3.0 * GELU_C1 * x * x)


def _shift_down(a, k, prev):
    p = prev.shape[0]
    r = pltpu.roll(a, k, 0)
    head = r[0:SUBLANES]
    rid = lax.broadcasted_iota(jnp.int32, head.shape, 0)
    for j in range(k):
        head = jnp.where(rid == j, prev[p - k + j:p - k + j + 1, :], head)
    return jnp.concatenate([head, r[SUBLANES:]], axis=0)


def _shift_up(a, k, nxt):
    t = a.shape[0]
    r = pltpu.roll(a, t - k, 0)
    tail = r[t - SUBLANES:t]
    rid = lax.broadcasted_iota(jnp.int32, tail.shape, 0)
    for j in range(k):
        tail = jnp.where(rid == SUBLANES - k + j, nxt[j:j + 1, :], tail)
    return jnp.concatenate([r[0:t - SUBLANES], tail], axis=0)


def _spatial_mask(transposed):
    ri = lax.broadcasted_iota(jnp.int32, (GMLP_BLOCK, GMLP_BLOCK), 0) // CAUSAL_CHUNK
    ci = lax.broadcasted_iota(jnp.int32, (GMLP_BLOCK, GMLP_BLOCK), 1) // CAUSAL_CHUNK
    return (ri <= ci) if transposed else (ci <= ri)


def _gmlp_forward(u, v, ln_g, ln_b, ws_ref, bs_ref, f_scr):
    tm = u.shape[0]
    tu = _gelu_tanh(u)
    tv = _gelu_tanh(v)
    gu = 0.5 * u * (1.0 + tu)
    gv = 0.5 * v * (1.0 + tv)
    mu = jnp.mean(gv, axis=-1, keepdims=True)
    cen = gv - mu
    rstd = lax.rsqrt(jnp.mean(cen * cen, axis=-1, keepdims=True) + LN_EPS)
    xh = cen * rstd
    vn = (xh * ln_g + ln_b).astype(BF16)
    mask = _spatial_mask(False)
    wm = [jnp.where(mask, ws_ref[h], 0.0).astype(BF16) for h in range(N_HEADS)]
    for b in range(tm // GMLP_BLOCK):
        rows = slice(b * GMLP_BLOCK, (b + 1) * GMLP_BLOCK)
        for h in range(N_HEADS):
            cols = slice(h * HEAD, (h + 1) * HEAD)
            f_scr[rows, cols] = _dot(wm[h], vn[rows, cols]) + bs_ref[h]
    return gu, tu, tv, xh, rstd, vn, f_scr[...]


def _position():
    return lax.axis_index("x"), lax.axis_index("y"), lax.axis_index("c")


class _Gather:
    def __init__(self, arrays):
        self.arrays = list(arrays)
        self.out_shape = [jax.ShapeDtypeStruct((N_DEV,) + a.shape, a.dtype) for a in self.arrays]

    def _plan(self, ins, outs, sems):
        send_sems, recv_sems, local_sems = sems
        x, y, c = _position()
        me, sibling = (x, y, c), (x, y, 1 - c)
        chips = [(1 - x, y), (x, 1 - y), (1 - x, 1 - y)]

        def slot(a, p):
            return outs[a].at[4 * p[0] + 2 * p[1] + p[2]]

        def copy(a, k, block, to, src=None):
            return pltpu.make_async_remote_copy(
                src_ref=slot(a, block) if src is None else src, dst_ref=slot(a, block),
                send_sem=send_sems.at[a, k], recv_sem=recv_sems.at[a, k], device_id=to, device_id_type=MESH)

        n = len(self.arrays)

        def mine():
            return [pltpu.make_async_copy(ins[a], slot(a, me), local_sems.at[a]) for a in range(n)]

        def first():
            out = []
            for a in range(n):
                out.append(copy(a, 0, me, sibling, src=ins[a]))
                out += [copy(a, 1 + j, me, (*chip, c), src=ins[a]) for j, chip in enumerate(chips)]
            return out

        def arrivals():
            return [copy(a, 1 + j, (*chip, c), me) for j, chip in enumerate(chips) for a in range(n)]

        def relays():
            return [copy(a, 4 + j, (*chip, c), sibling) for j, chip in enumerate(chips) for a in range(n)]

        def from_sibling():
            out = [copy(a, 0, sibling, me) for a in range(n)]
            return out + [copy(a, 4 + j, (*chip, 1 - c), me) for j, chip in enumerate(chips) for a in range(n)]

        return mine, first, arrivals, relays, from_sibling

    def start(self, ins, outs, sems):
        mine, first, _, _, _ = self._plan(ins, outs, sems)
        for cp in mine() + first():
            cp.start()

    def relay(self, ins, outs, sems):
        _, _, arrivals, relays, _ = self._plan(ins, outs, sems)
        for arrived, onward in zip(arrivals(), relays()):
            arrived.wait_recv()
            onward.start()

    def finish(self, ins, outs, sems):
        mine, first, _, relays, from_sibling = self._plan(ins, outs, sems)
        for cp in from_sibling():
            cp.wait_recv()
        for cp in first() + relays():
            cp.wait_send()
        for cp in mine():
            cp.wait()


class _Exchange:
    def __init__(self, arrays):
        self.arrays = list(arrays)
        self.out_shape = [jax.ShapeDtypeStruct(a.shape, a.dtype) for a in self.arrays]

    def _plan(self, ins, outs, sems):
        send_sems, recv_sems, local_sems = sems
        x, y, c = _position()
        my_idx = 4 * x + 2 * y + c
        n = len(self.arrays)
        offsets = [(dx, dy, dc) for dx in (0, 1) for dy in (0, 1) for dc in (0, 1) if (dx, dy, dc) != (0, 0, 0)]

        def mine():
            return [pltpu.make_async_copy(ins[a].at[my_idx], outs[a].at[my_idx], local_sems.at[a]) for a in range(n)]

        def remote(arriving):
            out = []
            for k, (dx, dy, dc) in enumerate(offsets):
                px, py, pc = x ^ dx, y ^ dy, c ^ dc
                p_idx = 4 * px + 2 * py + pc
                for a in range(n):
                    out.append(pltpu.make_async_remote_copy(
                        src_ref=ins[a].at[p_idx], dst_ref=outs[a].at[p_idx if arriving else my_idx],
                        send_sem=send_sems.at[a, k], recv_sem=recv_sems.at[a, k],
                        device_id=(px, py, pc), device_id_type=MESH))
            return out

        return mine, remote

    def start(self, ins, outs, sems):
        mine, remote = self._plan(ins, outs, sems)
        for cp in mine() + remote(False):
            cp.start()

    def relay(self, ins, outs, sems):
        pass

    def finish(self, ins, outs, sems):
        mine, remote = self._plan(ins, outs, sems)
        for cp in remote(True):
            cp.wait_recv()
        for cp in remote(False):
            cp.wait_send()
        for cp in mine():
            cp.wait()


def _call(body, *, name, grid, in_specs, out_specs, out_shape, args, scratch_shapes=(), carry=None):
    n_in, n_out, n_scr = len(in_specs), len(out_specs), len(scratch_shapes)
    params = pltpu.CompilerParams(dimension_semantics=("arbitrary",) * len(grid), vmem_limit_bytes=VMEM_LIMIT)
    if carry is None:
        outs = pl.pallas_call(body, name=name, grid=grid, in_specs=in_specs, out_specs=out_specs, out_shape=out_shape,
                              scratch_shapes=list(scratch_shapes), compiler_params=params)(*args)
        return outs, None
    m = len(carry.arrays)
    total = math.prod(grid)

    def wrapped(*refs):
        ins, refs = refs[:n_in], refs[n_in:]
        c_ins, refs = refs[:m], refs[m:]
        outs, refs = refs[:n_out], refs[n_out:]
        c_outs, refs = refs[:m], refs[m:]
        scr, sems = refs[:n_scr], refs[n_scr:]
        flat = pl.program_id(0)
        for d in range(1, len(grid)):
            flat = flat * grid[d] + pl.program_id(d)

        @pl.when(flat == 0)
        def _():
            carry.start(c_ins, c_outs, sems)

        body(*ins, *outs, *scr)

        @pl.when(flat == total - 2)
        def _():
            carry.relay(c_ins, c_outs, sems)

        @pl.when(flat == total - 1)
        def _():
            carry.finish(c_ins, c_outs, sems)

    any_spec = pl.BlockSpec(memory_space=pl.ANY)
    sem_shapes = [pltpu.SemaphoreType.DMA((m, 7)), pltpu.SemaphoreType.DMA((m, 7)), pltpu.SemaphoreType.DMA((m,))]
    outs = pl.pallas_call(
        wrapped, name=name, grid=grid,
        in_specs=list(in_specs) + [any_spec] * m, out_specs=list(out_specs) + [any_spec] * m,
        out_shape=list(out_shape) + carry.out_shape,
        scratch_shapes=list(scratch_shapes) + sem_shapes, compiler_params=params)(*args, *carry.arrays)
    return outs[:n_out], outs[n_out:]


def _gather_now(arrays, *, name):
    carry = _Gather(arrays)
    m = len(arrays)

    def body(*refs):
        ins, outs, sems = refs[:m], refs[m:2 * m], refs[2 * m:]
        carry.start(ins, outs, sems)
        carry.relay(ins, outs, sems)
        carry.finish(ins, outs, sems)

    any_spec = pl.BlockSpec(memory_space=pl.ANY)
    return pl.pallas_call(
        body, name=name, in_specs=[any_spec] * m, out_specs=[any_spec] * m, out_shape=carry.out_shape,
        scratch_shapes=[pltpu.SemaphoreType.DMA((m, 7)), pltpu.SemaphoreType.DMA((m, 7)),
                        pltpu.SemaphoreType.DMA((m,))],
    )(*arrays)


def _all_reduce_small(p, *, name):
    rows, lanes = p.shape

    def body(p_ref, out_ref, buf, send_sems, recv_sems):
        x, y, c = _position()
        me, sibling = (x, y, c), (x, y, 1 - c)
        chips = [(1 - x, y), (x, 1 - y), (1 - x, 1 - y)]

        def slot(q):
            return buf.at[4 * q[0] + 2 * q[1] + q[2]]

        def copy(k, block, to, src=None):
            return pltpu.make_async_remote_copy(
                src_ref=slot(block) if src is None else src, dst_ref=slot(block),
                send_sem=send_sems.at[k], recv_sem=recv_sems.at[k], device_id=to, device_id_type=MESH)

        first = [copy(0, me, sibling, src=p_ref)]
        first += [copy(1 + j, me, (*chip, c), src=p_ref) for j, chip in enumerate(chips)]
        for cp in first:
            cp.start()
        passed = []
        for j, chip in enumerate(chips):
            copy(1 + j, (*chip, c), me).wait_recv()
            cp = copy(4 + j, (*chip, c), sibling)
            cp.start()
            passed.append(cp)
        copy(0, sibling, me).wait_recv()
        for j, chip in enumerate(chips):
            copy(4 + j, (*chip, 1 - c), me).wait_recv()
        for cp in first + passed:
            cp.wait_send()
        my_idx = 4 * x + 2 * y + c
        acc = jnp.zeros((rows, lanes), F32)
        for s in range(N_DEV):
            acc = acc + jnp.where(my_idx == s, p_ref[...], buf[s])
        out_ref[...] = acc

    return pl.pallas_call(
        body, name=name,
        in_specs=[pl.BlockSpec(memory_space=pltpu.VMEM)],
        out_specs=pl.BlockSpec(memory_space=pltpu.VMEM),
        out_shape=jax.ShapeDtypeStruct((rows, lanes), F32),
        scratch_shapes=[pltpu.VMEM((N_DEV, rows, lanes), F32),
                        pltpu.SemaphoreType.DMA((7,)), pltpu.SemaphoreType.DMA((7,))],
        compiler_params=pltpu.CompilerParams(vmem_limit_bytes=VMEM_LIMIT),
    )(p)


def _norm_matmul(x, g, w_t, *, tm, name, carry=None):
    n, d = x.shape
    c = w_t.shape[0]
    ch = 512

    def body(x_ref, g_ref, wt_ref, h_ref, z_ref):
        xv = x_ref[...]
        r = lax.rsqrt(jnp.mean(xv * xv, axis=-1, keepdims=True) + RMS_EPS)
        h = (xv * r * g_ref[...]).astype(BF16)
        h_ref[...] = h
        for c0 in range(0, c, ch):
            z_ref[:, c0:c0 + ch] = _dot_nt(h, wt_ref[c0:c0 + ch, :]).astype(BF16)

    return _call(
        body, name=name, grid=(n // tm,), carry=carry,
        in_specs=[pl.BlockSpec((tm, d), lambda i: (i, 0)),
                  pl.BlockSpec((1, d), lambda i: (0, 0)),
                  pl.BlockSpec((c, d), lambda i: (0, 0))],
        out_specs=[pl.BlockSpec((tm, d), lambda i: (i, 0)),
                   pl.BlockSpec((tm, c), lambda i: (i, 0))],
        out_shape=[jax.ShapeDtypeStruct((n, d), BF16), jax.ShapeDtypeStruct((n, c), BF16)],
        args=(x, g.reshape(1, d), w_t))


def _mix_forward(z, x, b_gate, ln_g, ln_b, w_s, b_s, w_sc, wb, w_out, *, tm, name, carry=None):
    n = z.shape[0]
    hb = tm // HALO

    def body(z_ref, zp_ref, x_ref, bg_ref, lng_ref, lnb_ref, ws_ref, bs_ref, wsc_ref, wb_ref, wo_ref,
             ya_ref, yb_ref, cv_ref, pa_ref, pb_ref, mg_ref, x1_ref, f_scr):
        i = pl.program_id(0)
        u = z_ref[:, OFF_U:OFF_U + D_A].astype(F32)
        v = z_ref[:, OFF_V:OFF_V + D_A].astype(F32)
        gu, _, _, _, _, _, f = _gmlp_forward(u, v, lng_ref[...], lnb_ref[...], ws_ref, bs_ref, f_scr)
        ya = (gu * f).astype(BF16)
        ya_ref[...] = ya

        bgv = z_ref[:, OFF_BG:OFF_BG + D_B].astype(F32)
        q = z_ref[:, OFF_CG:OFF_CG + D_B].astype(F32) * z_ref[:, OFF_HB:OFF_HB + D_B].astype(F32)
        qp = zp_ref[:, OFF_CG:OFF_CG + D_B].astype(F32) * zp_ref[:, OFF_HB:OFF_HB + D_B].astype(F32)
        qp = jnp.where(i > 0, qp, jnp.zeros_like(qp))
        conv = wsc_ref[0:1, :] * _shift_down(q, 2, qp) + wsc_ref[1:2, :] * _shift_down(q, 1, qp) + wsc_ref[2:3, :] * q
        cv_ref[...] = conv.astype(BF16)
        yb = (bgv * conv).astype(BF16)
        yb_ref[...] = yb

        pa = _dot(ya, wb_ref[0])
        pb = _dot(yb, wb_ref[1])
        pa_ref[...] = pa.astype(BF16)
        pb_ref[...] = pb.astype(BF16)
        sa = _sigmoid(z_ref[:, OFF_GA:OFF_GA + D_MODEL].astype(F32) + bg_ref[:, 0:D_MODEL])
        sb = _sigmoid(z_ref[:, OFF_GB:OFF_GB + D_MODEL].astype(F32) + bg_ref[:, D_MODEL:2 * D_MODEL])
        mg = (sa * pa + sb * pb).astype(BF16)
        mg_ref[...] = mg
        x1_ref[...] = x_ref[...] + _dot(mg, wo_ref[...])

    row = lambda w: pl.BlockSpec((tm, w), lambda i: (i, 0))
    full = lambda *s: pl.BlockSpec(s, lambda i: (0,) * len(s))
    bf = lambda w: jax.ShapeDtypeStruct((n, w), BF16)
    return _call(
        body, name=name, grid=(n // tm,), carry=carry,
        in_specs=[row(D_IN),
                  pl.BlockSpec((HALO, D_IN), lambda i: (jnp.maximum(i * hb - 1, 0), 0)),
                  row(D_MODEL), full(1, 2 * D_MODEL), full(1, D_A), full(1, D_A),
                  full(N_HEADS, GMLP_BLOCK, GMLP_BLOCK), full(N_HEADS, GMLP_BLOCK, 1), full(3, D_B),
                  full(2, D_A, D_MODEL), full(D_MODEL, D_MODEL)],
        out_specs=[row(D_A), row(D_B), row(D_B), row(D_MODEL), row(D_MODEL), row(D_MODEL), row(D_MODEL)],
        out_shape=[bf(D_A), bf(D_B), bf(D_B), bf(D_MODEL), bf(D_MODEL), bf(D_MODEL),
                   jax.ShapeDtypeStruct((n, D_MODEL), F32)],
        scratch_shapes=[pltpu.VMEM((tm, D_A), F32)],
        args=(z, z, x, b_gate.reshape(1, -1), ln_g.reshape(1, -1), ln_b.reshape(1, -1), w_s,
              b_s.reshape(N_HEADS, GMLP_BLOCK, 1), w_sc, wb, w_out))


def _ffn_forward(up, x1, w_fc, b_fc, w_down, *, tm, name, carry=None):
    n = up.shape[0]
    hb = tm // HALO

    def body(up_ref, upp_ref, x1_ref, wfc_ref, bfc_ref, wd_ref, gc_ref, a_ref, x2_ref, acc):
        i = pl.program_id(0)
        acc[...] = x1_ref[...]
        for c0 in range(0, D_FF, FFN_CHUNK):
            cols = slice(c0, c0 + FFN_CHUNK)
            gate = up_ref[:, cols].astype(F32)
            val = up_ref[:, D_FF + c0:D_FF + c0 + FFN_CHUNK].astype(F32)
            gp = upp_ref[:, cols].astype(F32)
            gp = jnp.where(i > 0, gp, jnp.zeros_like(gp))
            gc = (wfc_ref[0:1, cols] * _shift_down(gate, 2, gp) + wfc_ref[1:2, cols] * _shift_down(gate, 1, gp)
                  + wfc_ref[2:3, cols] * gate + bfc_ref[:, cols])
            gc_ref[:, cols] = gc.astype(BF16)
            a = (gc * _sigmoid(gc) * val).astype(BF16)
            a_ref[:, cols] = a
            acc[...] += _dot(a, wd_ref[cols, :])
        x2_ref[...] = acc[...]

    return _call(
        body, name=name, grid=(n // tm,), carry=carry,
        in_specs=[pl.BlockSpec((tm, 2 * D_FF), lambda i: (i, 0)),
                  pl.BlockSpec((HALO, D_FF), lambda i: (jnp.maximum(i * hb - 1, 0), 0)),
                  pl.BlockSpec((tm, D_MODEL), lambda i: (i, 0)),
                  pl.BlockSpec((3, D_FF), lambda i: (0, 0)),
                  pl.BlockSpec((1, D_FF), lambda i: (0, 0)),
                  pl.BlockSpec((D_FF, D_MODEL), lambda i: (0, 0))],
        out_specs=[pl.BlockSpec((tm, D_FF), lambda i: (i, 0)),
                   pl.BlockSpec((tm, D_FF), lambda i: (i, 0)),
                   pl.BlockSpec((tm, D_MODEL), lambda i: (i, 0))],
        out_shape=[jax.ShapeDtypeStruct((n, D_FF), BF16), jax.ShapeDtypeStruct((n, D_FF), BF16),
                   jax.ShapeDtypeStruct((n, D_MODEL), F32)],
        scratch_shapes=[pltpu.VMEM((tm, D_MODEL), F32)],
        args=(up, up, x1, w_fc, b_fc.reshape(1, -1), w_down))


def _loss_head(x, g, target, *, tm, name):
    n, d = x.shape

    def body(x_ref, g_ref, t_ref, dx_ref, loss_ref, dg_ref):
        i = pl.program_id(0)

        @pl.when(i == 0)
        def _():
            loss_ref[...] = jnp.zeros_like(loss_ref)
            dg_ref[...] = jnp.zeros_like(dg_ref)

        xv = x_ref[...]
        r = lax.rsqrt(jnp.mean(xv * xv, axis=-1, keepdims=True) + RMS_EPS)
        xh = xv * r
        gv = g_ref[...]
        e = xh * gv - t_ref[...]
        per_row = jnp.sum(e * e, axis=-1, keepdims=True) * (0.5 / d)
        loss_ref[...] += jnp.sum(per_row, axis=0, keepdims=True)
        dy = e * (1.0 / d)
        dg_ref[0:1, :] += jnp.sum(dy * xh, axis=0, keepdims=True)
        dxh = dy * gv
        dx_ref[...] = r * (dxh - xh * jnp.mean(dxh * xh, axis=-1, keepdims=True))

    outs, _ = _call(
        body, name=name, grid=(n // tm,),
        in_specs=[pl.BlockSpec((tm, d), lambda i: (i, 0)),
                  pl.BlockSpec((1, d), lambda i: (0, 0)),
                  pl.BlockSpec((tm, d), lambda i: (i, 0))],
        out_specs=[pl.BlockSpec((tm, d), lambda i: (i, 0)),
                   pl.BlockSpec((1, 1), lambda i: (0, 0)),
                   pl.BlockSpec((8, d), lambda i: (0, 0))],
        out_shape=[jax.ShapeDtypeStruct((n, d), F32), jax.ShapeDtypeStruct((1, 1), F32),
                   jax.ShapeDtypeStruct((8, d), F32)],
        args=(x, g.reshape(1, d), target))
    return outs


def _ffn_backward(dx2, up, gc, w_fc, w_down, *, tm, name, carry=None):
    n = up.shape[0]
    steps = n // tm
    hb = tm // HALO

    def body(dx_ref, dxn_ref, up_ref, upn_ref, gc_ref, gcn_ref, wfc_ref, wd_ref, dup_ref, dwc_ref):
        i = pl.program_id(0)
        last = i == steps - 1

        @pl.when(i == 0)
        def _():
            dwc_ref[...] = jnp.zeros_like(dwc_ref)

        dxe = jnp.concatenate([dx_ref[...], dxn_ref[...]], axis=0).astype(BF16)
        for c0 in range(0, D_FF, FFN_CHUNK):
            cols = slice(c0, c0 + FFN_CHUNK)
            vcols = slice(D_FF + c0, D_FF + c0 + FFN_CHUNK)
            dae = _dot_nt(dxe, wd_ref[cols, :])
            da, dan = dae[:tm], dae[tm:]
            gate = up_ref[:, cols].astype(F32)
            val = up_ref[:, vcols].astype(F32)
            gcv = gc_ref[:, cols].astype(F32)
            s = _sigmoid(gcv)
            dup_ref[:, vcols] = (da * (gcv * s)).astype(BF16)
            dgc = da * val * (s * (1.0 + gcv * (1.0 - s)))
            gcn = gcn_ref[:, cols].astype(F32)
            sn = _sigmoid(gcn)
            dgcn = dan * upn_ref[:, vcols].astype(F32) * (sn * (1.0 + gcn * (1.0 - sn)))
            dgcn = jnp.where(last, jnp.zeros_like(dgcn), dgcn)
            up1 = _shift_up(dgc, 1, dgcn)
            up2 = _shift_up(dgc, 2, dgcn)
            dgate = wfc_ref[2:3, cols] * dgc + wfc_ref[1:2, cols] * up1 + wfc_ref[0:1, cols] * up2
            dup_ref[:, cols] = dgate.astype(BF16)
            dwc_ref[0:1, cols] += jnp.sum(gate * up2, axis=0, keepdims=True)
            dwc_ref[1:2, cols] += jnp.sum(gate * up1, axis=0, keepdims=True)
            dwc_ref[2:3, cols] += jnp.sum(gate * dgc, axis=0, keepdims=True)
            dwc_ref[3:4, cols] += jnp.sum(dgc, axis=0, keepdims=True)

    nxt = lambda i: (jnp.minimum((i + 1) * hb, steps * hb - 1), 0)
    return _call(
        body, name=name, grid=(steps,), carry=carry,
        in_specs=[pl.BlockSpec((tm, D_MODEL), lambda i: (i, 0)),
                  pl.BlockSpec((HALO, D_MODEL), nxt),
                  pl.BlockSpec((tm, 2 * D_FF), lambda i: (i, 0)),
                  pl.BlockSpec((HALO, 2 * D_FF), nxt),
                  pl.BlockSpec((tm, D_FF), lambda i: (i, 0)),
                  pl.BlockSpec((HALO, D_FF), nxt),
                  pl.BlockSpec((3, D_FF), lambda i: (0, 0)),
                  pl.BlockSpec((D_FF, D_MODEL), lambda i: (0, 0))],
        out_specs=[pl.BlockSpec((tm, 2 * D_FF), lambda i: (i, 0)),
                   pl.BlockSpec((8, D_FF), lambda i: (0, 0))],
        out_shape=[jax.ShapeDtypeStruct((n, 2 * D_FF), BF16), jax.ShapeDtypeStruct((8, D_FF), F32)],
        args=(dx2, dx2, up, up, gc, gc, w_fc, w_down))


def _matmul_norm_backward(dz, w_t, x, g, dres, *, tm, name, carry=None):
    n, c = dz.shape
    d = x.shape[1]
    ch = 512

    def body(dz_ref, wt_ref, x_ref, g_ref, dres_ref, dx_ref, dg_ref):
        i = pl.program_id(0)

        @pl.when(i == 0)
        def _():
            dg_ref[...] = jnp.zeros_like(dg_ref)

        dh = _dot(dz_ref[:, 0:ch], wt_ref[0:ch, :])
        for c0 in range(ch, c, ch):
            dh += _dot(dz_ref[:, c0:c0 + ch], wt_ref[c0:c0 + ch, :])
        xv = x_ref[...]
        r = lax.rsqrt(jnp.mean(xv * xv, axis=-1, keepdims=True) + RMS_EPS)
        xh = xv * r
        dg_ref[0:1, :] += jnp.sum(dh * xh, axis=0, keepdims=True)
        dxh = dh * g_ref[...]
        dx_ref[...] = dres_ref[...] + r * (dxh - xh * jnp.mean(dxh * xh, axis=-1, keepdims=True))

    return _call(
        body, name=name, grid=(n // tm,), carry=carry,
        in_specs=[pl.BlockSpec((tm, c), lambda i: (i, 0)),
                  pl.BlockSpec((c, d), lambda i: (0, 0)),
                  pl.BlockSpec((tm, d), lambda i: (i, 0)),
                  pl.BlockSpec((1, d), lambda i: (0, 0)),
                  pl.BlockSpec((tm, d), lambda i: (i, 0))],
        out_specs=[pl.BlockSpec((tm, d), lambda i: (i, 0)),
                   pl.BlockSpec((8, d), lambda i: (0, 0))],
        out_shape=[jax.ShapeDtypeStruct((n, d), F32), jax.ShapeDtypeStruct((8, d), F32)],
        args=(dz, w_t, x, g.reshape(1, d), dres))


def _mix_backward(dx1, z, conv, pa, pb, b_gate, ln_g, ln_b, w_s, w_s_t, b_s, w_sc, w_out, wb, *, tm, name, carry=None):
    n = z.shape[0]
    steps = n // tm
    hb = tm // HALO

    def body(dx_ref, dxn_ref, z_ref, zn_ref, cv_ref, pa_ref, pb_ref, bg_ref, lng_ref, lnb_ref, ws_ref, wst_ref,
             bs_ref, wsc_ref, wo_ref, wb_ref,
             dz_ref, dpa_ref, dpb_ref, dbg_ref, dln_ref, dws_ref, dbs_ref, dwsc_ref, f_scr, dvn_scr):
        i = pl.program_id(0)
        last = i == steps - 1

        @pl.when(i == 0)
        def _():
            dbg_ref[...] = jnp.zeros_like(dbg_ref)
            dln_ref[...] = jnp.zeros_like(dln_ref)
            dws_ref[...] = jnp.zeros_like(dws_ref)
            dbs_ref[...] = jnp.zeros_like(dbs_ref)
            dwsc_ref[...] = jnp.zeros_like(dwsc_ref)

        dxe = jnp.concatenate([dx_ref[...], dxn_ref[...]], axis=0).astype(BF16)
        dmge = _dot_nt(dxe, wo_ref[...])
        dmg, dmgn = dmge[:tm], dmge[tm:]

        pa_v = pa_ref[...].astype(F32)
        pb_v = pb_ref[...].astype(F32)
        sa = _sigmoid(z_ref[:, OFF_GA:OFF_GA + D_MODEL].astype(F32) + bg_ref[:, 0:D_MODEL])
        sb = _sigmoid(z_ref[:, OFF_GB:OFF_GB + D_MODEL].astype(F32) + bg_ref[:, D_MODEL:2 * D_MODEL])
        dpa = (dmg * sa).astype(BF16)
        dpb = dmg * sb
        dga = dmg * pa_v * sa * (1.0 - sa)
        dgb = dmg * pb_v * sb * (1.0 - sb)
        dpa_ref[...] = dpa
        dpb_ref[...] = dpb.astype(BF16)
        dz_ref[:, OFF_GA:OFF_GA + D_MODEL] = dga.astype(BF16)
        dz_ref[:, OFF_GB:OFF_GB + D_MODEL] = dgb.astype(BF16)
        dbg_ref[0:1, 0:D_MODEL] += jnp.sum(dga, axis=0, keepdims=True)
        dbg_ref[0:1, D_MODEL:2 * D_MODEL] += jnp.sum(dgb, axis=0, keepdims=True)

        dya = _dot_nt(dpa, wb_ref[0])
        u = z_ref[:, OFF_U:OFF_U + D_A].astype(F32)
        v = z_ref[:, OFF_V:OFF_V + D_A].astype(F32)
        ln_g = lng_ref[...]
        gu, tu, tv, xh, rstd, vn, f = _gmlp_forward(u, v, ln_g, lnb_ref[...], ws_ref, bs_ref, f_scr)
        dgu = dya * f
        df = dya * gu
        dz_ref[:, OFF_U:OFF_U + D_A] = (dgu * _gelu_grad(u, tu)).astype(BF16)
        df_bf = df.astype(BF16)
        mask = _spatial_mask(False)
        mask_t = _spatial_mask(True)
        wmt = [jnp.where(mask_t, wst_ref[h], 0.0).astype(BF16) for h in range(N_HEADS)]
        for b in range(tm // GMLP_BLOCK):
            rows = slice(b * GMLP_BLOCK, (b + 1) * GMLP_BLOCK)
            for h in range(N_HEADS):
                cols = slice(h * HEAD, (h + 1) * HEAD)
                dfb = df_bf[rows, cols]
                dvn_scr[rows, cols] = _dot(wmt[h], dfb)
                dws_ref[h] += jnp.where(mask, _dot_nt(dfb, vn[rows, cols]), 0.0)
                dbs_ref[h] += jnp.sum(df[rows, cols], axis=1, keepdims=True)
        dvn = dvn_scr[...]
        dln_ref[0:1, :] += jnp.sum(dvn * xh, axis=0, keepdims=True)
        dln_ref[1:2, :] += jnp.sum(dvn, axis=0, keepdims=True)
        dxh = dvn * ln_g
        dgv = rstd * (dxh - jnp.mean(dxh, axis=-1, keepdims=True) - xh * jnp.mean(dxh * xh, axis=-1, keepdims=True))
        dz_ref[:, OFF_V:OFF_V + D_A] = (dgv * _gelu_grad(v, tv)).astype(BF16)

        sbn = _sigmoid(zn_ref[:, OFF_GB:OFF_GB + D_MODEL].astype(F32) + bg_ref[:, D_MODEL:2 * D_MODEL])
        dpbe = jnp.concatenate([dpb, dmgn * sbn], axis=0).astype(BF16)
        dybe = _dot_nt(dpbe, wb_ref[1])
        dyb, dybn = dybe[:tm], dybe[tm:]
        bgv = z_ref[:, OFF_BG:OFF_BG + D_B].astype(F32)
        cg = z_ref[:, OFF_CG:OFF_CG + D_B].astype(F32)
        hbv = z_ref[:, OFF_HB:OFF_HB + D_B].astype(F32)
        q = cg * hbv
        dz_ref[:, OFF_BG:OFF_BG + D_B] = (dyb * cv_ref[...].astype(F32)).astype(BF16)
        dconv = dyb * bgv
        dconvn = dybn * zn_ref[:, OFF_BG:OFF_BG + D_B].astype(F32)
        dconvn = jnp.where(last, jnp.zeros_like(dconvn), dconvn)
        up1 = _shift_up(dconv, 1, dconvn)
        up2 = _shift_up(dconv, 2, dconvn)
        dwsc_ref[0:1, :] += jnp.sum(q * up2, axis=0, keepdims=True)
        dwsc_ref[1:2, :] += jnp.sum(q * up1, axis=0, keepdims=True)
        dwsc_ref[2:3, :] += jnp.sum(q * dconv, axis=0, keepdims=True)
        dq = wsc_ref[2:3, :] * dconv + wsc_ref[1:2, :] * up1 + wsc_ref[0:1, :] * up2
        dz_ref[:, OFF_CG:OFF_CG + D_B] = (dq * hbv).astype(BF16)
        dz_ref[:, OFF_HB:OFF_HB + D_B] = (dq * cg).astype(BF16)

    row = lambda w: pl.BlockSpec((tm, w), lambda i: (i, 0))
    full = lambda *s: pl.BlockSpec(s, lambda i: (0,) * len(s))
    nxt = lambda i: (jnp.minimum((i + 1) * hb, steps * hb - 1), 0)
    return _call(
        body, name=name, grid=(steps,), carry=carry,
        in_specs=[row(D_MODEL), pl.BlockSpec((HALO, D_MODEL), nxt),
                  row(D_IN), pl.BlockSpec((HALO, D_IN), nxt),
                  row(D_B), row(D_MODEL), row(D_MODEL),
                  full(1, 2 * D_MODEL), full(1, D_A), full(1, D_A),
                  full(N_HEADS, GMLP_BLOCK, GMLP_BLOCK), full(N_HEADS, GMLP_BLOCK, GMLP_BLOCK),
                  full(N_HEADS, GMLP_BLOCK, 1), full(3, D_B),
                  full(D_MODEL, D_MODEL), full(2, D_A, D_MODEL)],
        out_specs=[row(D_IN), row(D_MODEL), row(D_MODEL),
                   full(8, 2 * D_MODEL), full(8, D_A), full(N_HEADS, GMLP_BLOCK, GMLP_BLOCK),
                   full(N_HEADS, GMLP_BLOCK, 1), full(8, D_B)],
        out_shape=[jax.ShapeDtypeStruct((n, D_IN), BF16), jax.ShapeDtypeStruct((n, D_MODEL), BF16),
                   jax.ShapeDtypeStruct((n, D_MODEL), BF16),
                   jax.ShapeDtypeStruct((8, 2 * D_MODEL), F32), jax.ShapeDtypeStruct((8, D_A), F32),
                   jax.ShapeDtypeStruct((N_HEADS, GMLP_BLOCK, GMLP_BLOCK), F32),
                   jax.ShapeDtypeStruct((N_HEADS, GMLP_BLOCK, 1), F32), jax.ShapeDtypeStruct((8, D_B), F32)],
        scratch_shapes=[pltpu.VMEM((tm, D_A), F32), pltpu.VMEM((tm, D_A), F32)],
        args=(dx1, dx1, z, z, conv, pa, pb, b_gate.reshape(1, -1), ln_g.reshape(1, -1), ln_b.reshape(1, -1), w_s, w_s_t,
              b_s.reshape(N_HEADS, GMLP_BLOCK, 1), w_sc, w_out, wb))


def _matmul_tn(a, b, *, t1, tn, name, carry=None):
    n, k1 = a.shape
    k2 = b.shape[1]
    steps = n // tn

    def body(a_ref, b_ref, o_ref, acc):
        s = pl.program_id(1)

        @pl.when(s == 0)
        def _():
            acc[...] = jnp.zeros_like(acc)

        acc[...] += lax.dot_general(a_ref[...].astype(BF16), b_ref[...].astype(BF16), TN, preferred_element_type=F32)

        @pl.when(s == steps - 1)
        def _():
            o_ref[...] = acc[...].astype(BF16)

    outs, carried = _call(
        body, name=name, grid=(k1 // t1, steps), carry=carry,
        in_specs=[pl.BlockSpec((tn, t1), lambda i, s: (s, i)),
                  pl.BlockSpec((tn, k2), lambda i, s: (s, 0))],
        out_specs=[pl.BlockSpec((t1, k2), lambda i, s: (i, 0))],
        out_shape=[jax.ShapeDtypeStruct((k1, k2), BF16)],
        scratch_shapes=[pltpu.VMEM((t1, k2), F32)],
        args=(a, b))
    return outs[0], carried


def _adamw_math(w, g, m, v):
    m = ADAM_B1 * m + (1.0 - ADAM_B1) * g
    v = ADAM_B2 * v + (1.0 - ADAM_B2) * (g * g)
    m_hat = m / (1.0 - ADAM_B1 ** ADAM_STEP)
    v_hat = v / (1.0 - ADAM_B2 ** ADAM_STEP)
    delta = -ADAM_LR * (m_hat / (jnp.sqrt(v_hat) + ADAM_EPS) + ADAM_WD * w)
    return delta, m, v


def _sum_parts(recvs, *, tr, name):
    _, r, c = recvs[0].shape

    def body(*refs):
        recv_refs, g_ref = refs[:DEPTH], refs[DEPTH]
        layer = pl.program_id(0)
        for l in range(DEPTH):
            @pl.when(layer == l)
            def _(l=l):
                g = recv_refs[l][0].astype(F32)
                for s in range(1, N_DEV):
                    g = g + recv_refs[l][s].astype(F32)
                g_ref[0] = g

    outs, _ = _call(
        body, name=name, grid=(DEPTH, r // tr),
        in_specs=[pl.BlockSpec((N_DEV, tr, c), lambda l, i: (0, i, 0))] * DEPTH,
        out_specs=[pl.BlockSpec((1, tr, c), lambda l, i: (l, i, 0))],
        out_shape=[jax.ShapeDtypeStruct((DEPTH, r, c), F32)],
        args=tuple(recvs))
    return outs[0]


def _adamw(w, g, m, v, *, tr, name):
    r, c = w.shape

    def body(w_ref, g_ref, m_ref, v_ref, d_ref, nm_ref, nv_ref):
        delta, nm, nv = _adamw_math(w_ref[...], g_ref[...], m_ref[...], v_ref[...])
        d_ref[...] = delta
        nm_ref[...] = nm
        nv_ref[...] = nv

    spec = pl.BlockSpec((tr, c), lambda i: (i, 0))
    outs, _ = _call(body, name=name, grid=(r // tr,), in_specs=[spec] * 4, out_specs=[spec] * 3,
                    out_shape=[jax.ShapeDtypeStruct((r, c), F32)] * 3, args=(w, g, m, v))
    return outs


def _sum_adamw(recvs, w, m, v, *, tr, name):
    _, r, c = recvs[0].shape

    def body(*refs):
        recv_refs = refs[:DEPTH]
        w_ref, m_ref, v_ref, g_ref, d_ref, nm_ref, nv_ref = refs[DEPTH:]
        layer = pl.program_id(0)
        for l in range(DEPTH):
            @pl.when(layer == l)
            def _(l=l):
                g = recv_refs[l][0].astype(F32)
                for s in range(1, N_DEV):
                    g = g + recv_refs[l][s].astype(F32)
                delta, nm, nv = _adamw_math(w_ref[0], g, m_ref[0], v_ref[0])
                g_ref[0] = g
                d_ref[0] = delta
                nm_ref[0] = nm
                nv_ref[0] = nv

    spec = pl.BlockSpec((1, tr, c), lambda l, i: (l, i, 0))
    outs, _ = _call(
        body, name=name, grid=(DEPTH, r // tr),
        in_specs=[pl.BlockSpec((N_DEV, tr, c), lambda l, i: (0, i, 0))] * DEPTH + [spec] * 3,
        out_specs=[spec] * 4, out_shape=[jax.ShapeDtypeStruct((DEPTH, r, c), F32)] * 4,
        args=tuple(recvs) + (w, m, v))
    return outs


def _adamw_small(w, g, m, v, *, name):
    def body(w_ref, g_ref, m_ref, v_ref, d_ref, nm_ref, nv_ref):
        delta, nm, nv = _adamw_math(w_ref[...], g_ref[...], m_ref[...], v_ref[...])
        d_ref[...] = delta
        nm_ref[...] = nm
        nv_ref[...] = nv

    vmem = pl.BlockSpec(memory_space=pltpu.VMEM)
    return pl.pallas_call(
        body, name=name, in_specs=[vmem] * 4, out_specs=[vmem] * 3,
        out_shape=[jax.ShapeDtypeStruct(w.shape, F32)] * 3,
    )(w, g, m, v)


def _pack(arrs):
    flat = jnp.concatenate([a.reshape(-1) for a in arrs])
    pad = (-flat.shape[0]) % 1024
    return jnp.pad(flat, (0, pad)).reshape(-1, 128)


def _unpack(packed, shapes):
    flat = packed.reshape(-1)
    out, o = [], 0
    for s in shapes:
        size = math.prod(s)
        out.append(flat[o:o + size].reshape(s))
        o += size
    return out


def _rows(gathered):
    return gathered.reshape(N_DEV * gathered.shape[1], gathered.shape[2])


def _parts(full):
    return full.reshape(N_DEV, full.shape[0] // N_DEV, full.shape[1])


def kernel(x, norm1_g, w_in, b_gate, gmlp_ln_g, gmlp_ln_b, w_spatial, b_spatial, w_shortconv, w_branch, w_out, norm2_g, w_ffn_up, w_ffn_conv, b_ffn_conv, w_ffn_down, final_g, loss_target, m_norm1_g, m_w_in, m_b_gate, m_gmlp_ln_g, m_gmlp_ln_b, m_w_spatial, m_b_spatial, m_w_shortconv, m_w_branch, m_w_out, m_norm2_g, m_w_ffn_up, m_w_ffn_conv, m_b_ffn_conv, m_w_ffn_down, m_final_g, v_norm1_g, v_w_in, v_b_gate, v_gmlp_ln_g, v_gmlp_ln_b, v_w_spatial, v_b_spatial, v_w_shortconv, v_w_branch, v_w_out, v_norm2_g, v_w_ffn_up, v_w_ffn_conv, v_b_ffn_conv, v_w_ffn_down, v_final_g):
    n = x.shape[1]
    tm, tm_mix, tn = 512, 512, 1024
    x0 = x.reshape(n, D_MODEL)
    target = loss_target.reshape(n, D_MODEL)
    my_idx = 4 * lax.axis_index("x") + 2 * lax.axis_index("y") + lax.axis_index("c")
    sc_w, fc_w = D_B // N_DEV, D_FF // N_DEV

    sh_in = [w_in[l].T.astype(BF16) for l in range(DEPTH)]
    sh_up = [w_ffn_up[l].T.astype(BF16) for l in range(DEPTH)]
    sh_br = [w_branch[l].astype(BF16) for l in range(DEPTH)]
    sh_out = [w_out[l].astype(BF16) for l in range(DEPTH)]
    sh_down = [w_ffn_down[l].astype(BF16) for l in range(DEPTH)]
    taps = jnp.concatenate([w_shortconv, w_ffn_conv], axis=-1)

    def branch_weights(g):
        return g.transpose(1, 2, 0, 3).reshape(2, D_A, D_MODEL)

    g_in0, g_taps = _gather_now([sh_in[0], taps], name="gather_first")
    w_sc = [g_taps[:, l, :, :sc_w].transpose(1, 0, 2).reshape(3, D_B) for l in range(DEPTH)]
    w_fc = [g_taps[:, l, :, sc_w:].transpose(1, 0, 2).reshape(3, D_FF) for l in range(DEPTH)]
    w_s_t = [w_spatial[l].transpose(0, 2, 1) for l in range(DEPTH)]
    weights = [dict(), dict()]
    weights[0]["in_t"] = _rows(g_in0)
    saved = []
    xc = x0
    for l in range(DEPTH):
        p = weights[l]
        carry = _Gather([sh_br[0], sh_out[0], sh_up[0]]) if l == 0 else None
        (h, z), got = _norm_matmul(xc, norm1_g[l], p["in_t"], tm=tm, name=f"fwd_in_{l}", carry=carry)
        if l == 0:
            p["wb"], p["out"], p["up_t"] = branch_weights(got[0]), _rows(got[1]), _rows(got[2])
        carry = _Gather([sh_down[0], sh_in[1]]) if l == 0 else None
        (ya, yb, conv, pa, pb, mg, x1), got = _mix_forward(
            z, xc, b_gate[l], gmlp_ln_g[l], gmlp_ln_b[l], w_spatial[l], b_spatial[l], w_sc[l], p["wb"], p["out"],
            tm=tm_mix, name=f"fwd_mix_{l}", carry=carry)
        if l == 0:
            p["down"], weights[1]["in_t"] = _rows(got[0]), _rows(got[1])
        carry = _Gather([sh_br[1], sh_out[1], sh_up[1]]) if l == 0 else None
        (h2, up), got = _norm_matmul(x1, norm2_g[l], p["up_t"], tm=tm, name=f"fwd_up_{l}", carry=carry)
        if l == 0:
            weights[1]["wb"], weights[1]["out"], weights[1]["up_t"] = branch_weights(got[0]), _rows(got[1]), _rows(got[2])
        carry = _Gather([sh_down[1]]) if l == 0 else None
        (gc, a, x2), got = _ffn_forward(up, x1, w_fc[l], b_ffn_conv[l], p["down"], tm=tm, name=f"fwd_ffn_{l}", carry=carry)
        if l == 0:
            weights[1]["down"] = _rows(got[0])
        saved.append(dict(x=xc, h=h, z=z, ya=ya, yb=yb, conv=conv, pa=pa, pb=pb, mg=mg, x1=x1, h2=h2, up=up, gc=gc, a=a))
        xc = x2
    dx, loss_part, dgf = _loss_head(xc, final_g, target, tm=tm, name="loss_head")

    recv = [dict(), dict()]
    small = [None] * DEPTH
    pending_in = None
    for l in reversed(range(DEPTH)):
        p, s = weights[l], saved[l]
        carry = _Exchange([pending_in]) if pending_in is not None else None
        (dup, dwc), got = _ffn_backward(dx, s["up"], s["gc"], w_fc[l], p["down"], tm=tm, name=f"bwd_ffn_{l}", carry=carry)
        if got is not None:
            recv[l + 1]["in_t"] = got[0]
        dw_down, _ = _matmul_tn(s["a"], dx, t1=D_FF // 2, tn=tn, name=f"dw_down_{l}")
        dw_up_t, got = _matmul_tn(dup, s["h2"], t1=2 * D_FF // 4, tn=tn, name=f"dw_up_{l}", carry=_Exchange([_parts(dw_down)]))
        recv[l]["down"] = got[0]
        (dx1, dg2), _ = _matmul_norm_backward(dup, p["up_t"], s["x1"], norm2_g[l], dx, tm=tm, name=f"bwd_up_{l}")
        (dz, dpa, dpb, dbg, dln, dws, dbs, dwsc), got = _mix_backward(
            dx1, s["z"], s["conv"], s["pa"], s["pb"], b_gate[l], gmlp_ln_g[l], gmlp_ln_b[l], w_spatial[l], w_s_t[l],
            b_spatial[l], w_sc[l], p["out"], p["wb"], tm=tm_mix, name=f"bwd_mix_{l}", carry=_Exchange([_parts(dw_up_t)]))
        recv[l]["up_t"] = got[0]
        dw_out, _ = _matmul_tn(s["mg"], dx1, t1=D_MODEL, tn=tn, name=f"dw_out_{l}")
        dw_bra_t, _ = _matmul_tn(dpa, s["ya"], t1=D_MODEL, tn=tn, name=f"dw_branch_a_{l}")
        dw_brb_t, _ = _matmul_tn(dpb, s["yb"], t1=D_MODEL, tn=tn, name=f"dw_branch_b_{l}")
        dw_in_t, got = _matmul_tn(dz, s["h"], t1=D_IN // 4, tn=tn, name=f"dw_in_{l}",
                                  carry=_Exchange([_parts(dw_out), _parts(dw_bra_t), _parts(dw_brb_t)]))
        recv[l]["out"], recv[l]["bra_t"], recv[l]["brb_t"] = got
        carry = _Exchange([_parts(dw_in_t)]) if l == 0 else None
        (dx0, dg1), got = _matmul_norm_backward(dz, p["in_t"], s["x"], norm1_g[l], dx1, tm=tm, name=f"bwd_in_{l}", carry=carry)
        if l == 0:
            recv[0]["in_t"] = got[0]
        else:
            pending_in = _parts(dw_in_t)
        small[l] = dict(norm1_g=dg1[0], b_gate=dbg[0], gmlp_ln_g=dln[0], gmlp_ln_b=dln[1], w_spatial=dws,
                        b_spatial=dbs.reshape(N_HEADS, GMLP_BLOCK), w_shortconv=dwsc[0:3], norm2_g=dg2[0],
                        w_ffn_conv=dwc[0:3], b_ffn_conv=dwc[3])
        dx = dx0
    grad_x = dx.reshape(x.shape)

    results = {}
    both = lambda key: [recv[l][key] for l in range(DEPTH)]
    swap = lambda t: t.transpose(0, 2, 1)
    for key, slab, (w, m, v), tr in [("w_in", "in_t", (w_in, m_w_in, v_w_in), 192),
                                     ("w_ffn_up", "up_t", (w_ffn_up, m_w_ffn_up, v_w_ffn_up), 176)]:
        outs = _sum_adamw(both(slab), swap(w), swap(m), swap(v), tr=tr, name=f"adamw_{key}")
        results[key] = tuple(swap(o) for o in outs)
    g_bra = _sum_parts(both("bra_t"), tr=128, name="sum_w_branch_a").transpose(0, 2, 1)
    g_brb = _sum_parts(both("brb_t"), tr=128, name="sum_w_branch_b").transpose(0, 2, 1)
    g_br = jnp.stack([g_bra, g_brb], axis=1)
    flat = lambda t: t.reshape(-1, t.shape[-1])
    outs = _adamw(flat(w_branch), flat(g_br), flat(m_w_branch), flat(v_w_branch), tr=512, name="adamw_w_branch")
    results["w_branch"] = (g_br,) + tuple(o.reshape(w_branch.shape) for o in outs)
    results["w_out"] = tuple(_sum_adamw(both("out"), w_out, m_w_out, v_w_out, tr=128, name="adamw_w_out"))
    results["w_ffn_down"] = tuple(_sum_adamw(both("down"), w_ffn_down, m_w_ffn_down, v_w_ffn_down, tr=176,
                                             name="adamw_w_ffn_down"))

    small_names = ["norm1_g", "b_gate", "gmlp_ln_g", "gmlp_ln_b", "w_spatial", "b_spatial", "w_shortconv", "norm2_g",
                   "w_ffn_conv", "b_ffn_conv"]
    stacked = [jnp.stack([small[l][k] for l in range(DEPTH)]) for k in small_names] + [dgf[0]]
    shapes = [a.shape for a in stacked]
    reduced = _unpack(_all_reduce_small(_pack(stacked), name="all_reduce_small_grads"), shapes)
    g_small = dict(zip(small_names + ["final_g"], reduced))
    g_small["w_shortconv"] = lax.dynamic_slice_in_dim(g_small["w_shortconv"], my_idx * sc_w, sc_w, axis=2)
    g_small["w_ffn_conv"] = lax.dynamic_slice_in_dim(g_small["w_ffn_conv"], my_idx * fc_w, fc_w, axis=2)
    small_w = dict(norm1_g=(norm1_g, m_norm1_g, v_norm1_g), b_gate=(b_gate, m_b_gate, v_b_gate),
                   gmlp_ln_g=(gmlp_ln_g, m_gmlp_ln_g, v_gmlp_ln_g), gmlp_ln_b=(gmlp_ln_b, m_gmlp_ln_b, v_gmlp_ln_b),
                   w_spatial=(w_spatial, m_w_spatial, v_w_spatial), b_spatial=(b_spatial, m_b_spatial, v_b_spatial),
                   w_shortconv=(w_shortconv, m_w_shortconv, v_w_shortconv), norm2_g=(norm2_g, m_norm2_g, v_norm2_g),
                   w_ffn_conv=(w_ffn_conv, m_w_ffn_conv, v_w_ffn_conv), b_ffn_conv=(b_ffn_conv, m_b_ffn_conv, v_b_ffn_conv),
                   final_g=(final_g, m_final_g, v_final_g))
    order = small_names + ["final_g"]
    local_shapes = [small_w[k][0].shape for k in order]
    packed = [_pack([small_w[k][j] for k in order]) for j in range(3)]
    d_s, m_s, v_s = _adamw_small(packed[0], _pack([g_small[k] for k in order]), packed[1], packed[2], name="adamw_small")
    d_s, m_s, v_s = _unpack(d_s, local_shapes), _unpack(m_s, local_shapes), _unpack(v_s, local_shapes)
    for j, k in enumerate(order):
        results[k] = (g_small[k], d_s[j], m_s[j], v_s[j])

    loss = lax.psum(loss_part[0, 0], MESH_AXES)
    names = ["norm1_g", "w_in", "b_gate", "gmlp_ln_g", "gmlp_ln_b", "w_spatial", "b_spatial", "w_shortconv", "w_branch",
             "w_out", "norm2_g", "w_ffn_up", "w_ffn_conv", "b_ffn_conv", "w_ffn_down", "final_g"]
    return (loss, grad_x, *[results[k][0] for k in names], *[results[k][1] for k in names],
            *[results[k][2] for k in names], *[results[k][3] for k in names])
```

```python
import math

import jax
import jax.numpy as jnp
from jax import lax
from jax.experimental import pallas as pl
from jax.experimental.pallas import tpu as pltpu

F32 = jnp.float32
BF16 = jnp.bfloat16

N_DEV = 8
DEPTH = 2
D_MODEL = 1024
D_A = 512
D_B = 512
D_FF = 2816
D_IN = 4608
N_HEADS = 4
HEAD = 128
GMLP_BLOCK = 128
CAUSAL_CHUNK = 64
OFF_U, OFF_V, OFF_BG, OFF_CG, OFF_HB, OFF_GA, OFF_GB = 0, 512, 1024, 1536, 2048, 2560, 3584
RMS_EPS = 1e-6
LN_EPS = 1e-5
ADAM_LR, ADAM_B1, ADAM_B2, ADAM_EPS, ADAM_WD, ADAM_STEP = 0.001, 0.9, 0.999, 1e-08, 0.01, 10

SUBLANES = 8
HALO = 16
FFN_CHUNK = 256
V7X_VMEM_BYTES = 64 << 20
VMEM_LIMIT = V7X_VMEM_BYTES - (8 << 20)
MESH_AXES = ("x", "y", "c")
MESH = pl.DeviceIdType.MESH
GELU_C0 = 0.7978845608028654
GELU_C1 = 0.044715
NT = (((1,), (1,)), ((), ()))
TN = (((0,), (0,)), ((), ()))


def _dot(a, b):
    return jnp.dot(a, b, preferred_element_type=F32)


def _dot_nt(a, b):
    return lax.dot_general(a, b, NT, preferred_element_type=F32)


def _sigmoid(x):
    return 1.0 / (1.0 + jnp.exp(-x))


def _gelu_tanh(x):
    return jnp.tanh(GELU_C0 * (x + GELU_C1 * x * x * x))


def _gelu_grad(x, t):
    return 0.5 * (1.0 + t) + 0.5 * x * (1.0 - t * t) * GELU_C0 * (1.0 + 3.0 * GELU_C1 * x * x)


def _sublane_tile(dtype):
    return SUBLANES * (4 // jnp.dtype(dtype).itemsize)


def _shift_down(a, k, prev):
    p = prev.shape[0]
    r = pltpu.roll(a, k, 0)
    sub = _sublane_tile(a.dtype)
    head = r[0:sub]
    rid = lax.broadcasted_iota(jnp.int32, head.shape, 0)
    for j in range(k):
        head = jnp.where(rid == j, prev[p - k + j:p - k + j + 1, :], head)
    return jnp.concatenate([head, r[sub:]], axis=0)


def _shift_up(a, k, nxt):
    t = a.shape[0]
    r = pltpu.roll(a, t - k, 0)
    sub = _sublane_tile(a.dtype)
    tail = r[t - sub:t]
    rid = lax.broadcasted_iota(jnp.int32, tail.shape, 0)
    for j in range(k):
        tail = jnp.where(rid == sub - k + j, nxt[j:j + 1, :], tail)
    return jnp.concatenate([r[0:t - sub], tail], axis=0)


def _column_sums(p):
    if p.dtype.itemsize < 4:
        t = p.shape[0]
        p = p[:t // 2] + p[t // 2:]
        p = p[:t // 4] + p[t // 4:]
    return jnp.sum(p.astype(F32), axis=0, keepdims=True)


def _spatial_mask(transposed):
    ri = lax.broadcasted_iota(jnp.int32, (GMLP_BLOCK, GMLP_BLOCK), 0) // CAUSAL_CHUNK
    ci = lax.broadcasted_iota(jnp.int32, (GMLP_BLOCK, GMLP_BLOCK), 1) // CAUSAL_CHUNK
    return (ri <= ci) if transposed else (ci <= ri)


def _gmlp_forward(u, v, ln_g, ln_b, ws_ref, bs_ref, f_scr):
    tm = u.shape[0]
    tu = _gelu_tanh(u)
    tv = _gelu_tanh(v)
    gu = 0.5 * u * (1.0 + tu)
    gv = 0.5 * v * (1.0 + tv)
    mu = jnp.mean(gv, axis=-1, keepdims=True)
    cen = gv - mu
    rstd = lax.rsqrt(jnp.mean(cen * cen, axis=-1, keepdims=True) + LN_EPS)
    xh = cen * rstd
    vn = (xh * ln_g + ln_b).astype(BF16)
    mask = _spatial_mask(False)
    wm = [jnp.where(mask, ws_ref[h], 0.0).astype(BF16) for h in range(N_HEADS)]
    for b in range(tm // GMLP_BLOCK):
        rows = slice(b * GMLP_BLOCK, (b + 1) * GMLP_BLOCK)
        for h in range(N_HEADS):
            cols = slice(h * HEAD, (h + 1) * HEAD)
            f_scr[rows, cols] = (_dot(wm[h], vn[rows, cols]) + bs_ref[h]).astype(f_scr.dtype)
    return gu, tu, tv, xh, rstd, vn, f_scr[...]


def _position():
    return lax.axis_index("x"), lax.axis_index("y"), lax.axis_index("c")


class _Gather:
    def __init__(self, arrays):
        self.arrays = list(arrays)
        self.out_shape = [jax.ShapeDtypeStruct((N_DEV,) + a.shape, a.dtype) for a in self.arrays]

    def _plan(self, ins, outs, sems):
        send_sems, recv_sems, local_sems = sems
        x, y, c = _position()
        me, sibling = (x, y, c), (x, y, 1 - c)
        chips = [(1 - x, y), (x, 1 - y), (1 - x, 1 - y)]

        def slot(a, p):
            return outs[a].at[4 * p[0] + 2 * p[1] + p[2]]

        def copy(a, k, block, to, src=None):
            return pltpu.make_async_remote_copy(
                src_ref=slot(a, block) if src is None else src, dst_ref=slot(a, block),
                send_sem=send_sems.at[a, k], recv_sem=recv_sems.at[a, k], device_id=to, device_id_type=MESH)

        n = len(self.arrays)

        def mine():
            return [pltpu.make_async_copy(ins[a], slot(a, me), local_sems.at[a]) for a in range(n)]

        def first():
            out = []
            for a in range(n):
                out.append(copy(a, 0, me, sibling, src=ins[a]))
                out += [copy(a, 1 + j, me, (*chip, c), src=ins[a]) for j, chip in enumerate(chips)]
            return out

        def arrivals():
            return [copy(a, 1 + j, (*chip, c), me) for j, chip in enumerate(chips) for a in range(n)]

        def relays():
            return [copy(a, 4 + j, (*chip, c), sibling) for j, chip in enumerate(chips) for a in range(n)]

        def from_sibling():
            out = [copy(a, 0, sibling, me) for a in range(n)]
            return out + [copy(a, 4 + j, (*chip, 1 - c), me) for j, chip in enumerate(chips) for a in range(n)]

        return mine, first, arrivals, relays, from_sibling

    def start(self, ins, outs, sems):
        mine, first, _, _, _ = self._plan(ins, outs, sems)
        for cp in mine() + first():
            cp.start()

    def relay(self, ins, outs, sems):
        _, _, arrivals, relays, _ = self._plan(ins, outs, sems)
        for arrived, onward in zip(arrivals(), relays()):
            arrived.wait_recv()
            onward.start()

    def finish(self, ins, outs, sems):
        mine, first, _, relays, from_sibling = self._plan(ins, outs, sems)
        for cp in from_sibling():
            cp.wait_recv()
        for cp in first() + relays():
            cp.wait_send()
        for cp in mine():
            cp.wait()


class _Exchange:
    def __init__(self, arrays):
        self.arrays = list(arrays)
        self.out_shape = [jax.ShapeDtypeStruct(a.shape, a.dtype) for a in self.arrays]

    def _plan(self, ins, outs, sems):
        send_sems, recv_sems, local_sems = sems
        x, y, c = _position()
        my_idx = 4 * x + 2 * y + c
        n = len(self.arrays)
        offsets = [(dx, dy, dc) for dx in (0, 1) for dy in (0, 1) for dc in (0, 1) if (dx, dy, dc) != (0, 0, 0)]

        def mine():
            return [pltpu.make_async_copy(ins[a].at[my_idx], outs[a].at[my_idx], local_sems.at[a]) for a in range(n)]

        def remote(arriving):
            out = []
            for k, (dx, dy, dc) in enumerate(offsets):
                px, py, pc = x ^ dx, y ^ dy, c ^ dc
                p_idx = 4 * px + 2 * py + pc
                for a in range(n):
                    out.append(pltpu.make_async_remote_copy(
                        src_ref=ins[a].at[p_idx], dst_ref=outs[a].at[p_idx if arriving else my_idx],
                        send_sem=send_sems.at[a, k], recv_sem=recv_sems.at[a, k],
                        device_id=(px, py, pc), device_id_type=MESH))
            return out

        return mine, remote

    def start(self, ins, outs, sems):
        mine, remote = self._plan(ins, outs, sems)
        for cp in mine() + remote(False):
            cp.start()

    def relay(self, ins, outs, sems):
        pass

    def finish(self, ins, outs, sems):
        mine, remote = self._plan(ins, outs, sems)
        for cp in remote(True):
            cp.wait_recv()
        for cp in remote(False):
            cp.wait_send()
        for cp in mine():
            cp.wait()


def _call(body, *, name, grid, in_specs, out_specs, out_shape, args, scratch_shapes=(), carry=None):
    n_in, n_out, n_scr = len(in_specs), len(out_specs), len(scratch_shapes)
    params = pltpu.CompilerParams(dimension_semantics=("arbitrary",) * len(grid), vmem_limit_bytes=VMEM_LIMIT)
    if carry is None:
        outs = pl.pallas_call(body, name=name, grid=grid, in_specs=in_specs, out_specs=out_specs, out_shape=out_shape,
                              scratch_shapes=list(scratch_shapes), compiler_params=params)(*args)
        return outs, None
    m = len(carry.arrays)
    total = math.prod(grid)

    def wrapped(*refs):
        ins, refs = refs[:n_in], refs[n_in:]
        c_ins, refs = refs[:m], refs[m:]
        outs, refs = refs[:n_out], refs[n_out:]
        c_outs, refs = refs[:m], refs[m:]
        scr, sems = refs[:n_scr], refs[n_scr:]
        flat = pl.program_id(0)
        for d in range(1, len(grid)):
            flat = flat * grid[d] + pl.program_id(d)

        @pl.when(flat == 0)
        def _():
            carry.start(c_ins, c_outs, sems)

        body(*ins, *outs, *scr)

        @pl.when(flat == total - 2)
        def _():
            carry.relay(c_ins, c_outs, sems)

        @pl.when(flat == total - 1)
        def _():
            carry.finish(c_ins, c_outs, sems)

    any_spec = pl.BlockSpec(memory_space=pl.ANY)
    sem_shapes = [pltpu.SemaphoreType.DMA((m, 7)), pltpu.SemaphoreType.DMA((m, 7)), pltpu.SemaphoreType.DMA((m,))]
    outs = pl.pallas_call(
        wrapped, name=name, grid=grid,
        in_specs=list(in_specs) + [any_spec] * m, out_specs=list(out_specs) + [any_spec] * m,
        out_shape=list(out_shape) + carry.out_shape,
        scratch_shapes=list(scratch_shapes) + sem_shapes, compiler_params=params)(*args, *carry.arrays)
    return outs[:n_out], outs[n_out:]


def _gather_now(arrays, *, name):
    carry = _Gather(arrays)
    m = len(arrays)

    def body(*refs):
        ins, outs, sems = refs[:m], refs[m:2 * m], refs[2 * m:]
        carry.start(ins, outs, sems)
        carry.relay(ins, outs, sems)
        carry.finish(ins, outs, sems)

    any_spec = pl.BlockSpec(memory_space=pl.ANY)
    return pl.pallas_call(
        body, name=name, in_specs=[any_spec] * m, out_specs=[any_spec] * m, out_shape=carry.out_shape,
        scratch_shapes=[pltpu.SemaphoreType.DMA((m, 7)), pltpu.SemaphoreType.DMA((m, 7)),
                        pltpu.SemaphoreType.DMA((m,))],
    )(*arrays)


def _all_reduce_small(p, *, name):
    rows, lanes = p.shape

    def body(p_ref, out_ref, buf, send_sems, recv_sems):
        x, y, c = _position()
        me, sibling = (x, y, c), (x, y, 1 - c)
        chips = [(1 - x, y), (x, 1 - y), (1 - x, 1 - y)]

        def slot(q):
            return buf.at[4 * q[0] + 2 * q[1] + q[2]]

        def copy(k, block, to, src=None):
            return pltpu.make_async_remote_copy(
                src_ref=slot(block) if src is None else src, dst_ref=slot(block),
                send_sem=send_sems.at[k], recv_sem=recv_sems.at[k], device_id=to, device_id_type=MESH)

        first = [copy(0, me, sibling, src=p_ref)]
        first += [copy(1 + j, me, (*chip, c), src=p_ref) for j, chip in enumerate(chips)]
        for cp in first:
            cp.start()
        passed = []
        for j, chip in enumerate(chips):
            copy(1 + j, (*chip, c), me).wait_recv()
            cp = copy(4 + j, (*chip, c), sibling)
            cp.start()
            passed.append(cp)
        copy(0, sibling, me).wait_recv()
        for j, chip in enumerate(chips):
            copy(4 + j, (*chip, 1 - c), me).wait_recv()
        for cp in first + passed:
            cp.wait_send()
        my_idx = 4 * x + 2 * y + c
        acc = jnp.zeros((rows, lanes), F32)
        for s in range(N_DEV):
            acc = acc + jnp.where(my_idx == s, p_ref[...], buf[s])
        out_ref[...] = acc

    return pl.pallas_call(
        body, name=name,
        in_specs=[pl.BlockSpec(memory_space=pltpu.VMEM)],
        out_specs=pl.BlockSpec(memory_space=pltpu.VMEM),
        out_shape=jax.ShapeDtypeStruct((rows, lanes), F32),
        scratch_shapes=[pltpu.VMEM((N_DEV, rows, lanes), F32),
                        pltpu.SemaphoreType.DMA((7,)), pltpu.SemaphoreType.DMA((7,))],
        compiler_params=pltpu.CompilerParams(vmem_limit_bytes=VMEM_LIMIT),
    )(p)


def _norm_matmul(x, g, w_t, *, tm, name, carry=None):
    n, d = x.shape
    c = w_t.shape[0]
    ch = 512

    def body(x_ref, g_ref, wt_ref, h_ref, z_ref):
        xv = x_ref[...]
        r = lax.rsqrt(jnp.mean(xv * xv, axis=-1, keepdims=True) + RMS_EPS)
        h = (xv * r * g_ref[...]).astype(BF16)
        h_ref[...] = h
        for c0 in range(0, c, ch):
            z_ref[:, c0:c0 + ch] = _dot_nt(h, wt_ref[c0:c0 + ch, :]).astype(BF16)

    return _call(
        body, name=name, grid=(n // tm,), carry=carry,
        in_specs=[pl.BlockSpec((tm, d), lambda i: (i, 0)),
                  pl.BlockSpec((1, d), lambda i: (0, 0)),
                  pl.BlockSpec((c, d), lambda i: (0, 0))],
        out_specs=[pl.BlockSpec((tm, d), lambda i: (i, 0)),
                   pl.BlockSpec((tm, c), lambda i: (i, 0))],
        out_shape=[jax.ShapeDtypeStruct((n, d), BF16), jax.ShapeDtypeStruct((n, c), BF16)],
        args=(x, g.reshape(1, d), w_t))


def _mix_forward(z, x, b_gate, ln_g, ln_b, w_s, b_s, w_sc, wb, w_out, *, tm, name, carry=None):
    n = z.shape[0]
    hb = tm // HALO

    def body(z_ref, zp_ref, x_ref, bg_ref, lng_ref, lnb_ref, ws_ref, bs_ref, wsc_ref, wb_ref, wo_ref,
             ya_ref, yb_ref, cv_ref, pa_ref, pb_ref, mg_ref, x1_ref, f_scr):
        i = pl.program_id(0)
        u = z_ref[:, OFF_U:OFF_U + D_A]
        v = z_ref[:, OFF_V:OFF_V + D_A].astype(F32)
        gu, _, _, _, _, _, f = _gmlp_forward(u, v, lng_ref[...], lnb_ref[...], ws_ref, bs_ref, f_scr)
        ya = gu * f
        ya_ref[...] = ya

        q = z_ref[:, OFF_CG:OFF_CG + D_B] * z_ref[:, OFF_HB:OFF_HB + D_B]
        qp = zp_ref[:, OFF_CG:OFF_CG + D_B] * zp_ref[:, OFF_HB:OFF_HB + D_B]
        qp = jnp.where(i > 0, qp, jnp.zeros_like(qp))
        w = wsc_ref[...].astype(BF16)
        conv = w[0:1] * _shift_down(q, 2, qp) + w[1:2] * _shift_down(q, 1, qp) + w[2:3] * q
        cv_ref[...] = conv
        yb = z_ref[:, OFF_BG:OFF_BG + D_B] * conv
        yb_ref[...] = yb

        pa = _dot(ya, wb_ref[0]).astype(BF16)
        pb = _dot(yb, wb_ref[1]).astype(BF16)
        pa_ref[...] = pa
        pb_ref[...] = pb
        bg = bg_ref[...].astype(BF16)
        sa = _sigmoid(z_ref[:, OFF_GA:OFF_GA + D_MODEL] + bg[:, 0:D_MODEL])
        sb = _sigmoid(z_ref[:, OFF_GB:OFF_GB + D_MODEL] + bg[:, D_MODEL:2 * D_MODEL])
        mg = sa * pa + sb * pb
        mg_ref[...] = mg
        x1_ref[...] = x_ref[...] + _dot(mg, wo_ref[...])

    row = lambda w: pl.BlockSpec((tm, w), lambda i: (i, 0))
    full = lambda *s: pl.BlockSpec(s, lambda i: (0,) * len(s))
    bf = lambda w: jax.ShapeDtypeStruct((n, w), BF16)
    return _call(
        body, name=name, grid=(n // tm,), carry=carry,
        in_specs=[row(D_IN),
                  pl.BlockSpec((HALO, D_IN), lambda i: (jnp.maximum(i * hb - 1, 0), 0)),
                  row(D_MODEL), full(1, 2 * D_MODEL), full(1, D_A), full(1, D_A),
                  full(N_HEADS, GMLP_BLOCK, GMLP_BLOCK), full(N_HEADS, GMLP_BLOCK, 1), full(3, D_B),
                  full(2, D_A, D_MODEL), full(D_MODEL, D_MODEL)],
        out_specs=[row(D_A), row(D_B), row(D_B), row(D_MODEL), row(D_MODEL), row(D_MODEL), row(D_MODEL)],
        out_shape=[bf(D_A), bf(D_B), bf(D_B), bf(D_MODEL), bf(D_MODEL), bf(D_MODEL),
                   jax.ShapeDtypeStruct((n, D_MODEL), F32)],
        scratch_shapes=[pltpu.VMEM((tm, D_A), BF16)],
        args=(z, z, x, b_gate.reshape(1, -1), ln_g.reshape(1, -1), ln_b.reshape(1, -1), w_s,
              b_s.reshape(N_HEADS, GMLP_BLOCK, 1), w_sc, wb, w_out))


def _ffn_forward(up, x1, w_fc, b_fc, w_down, *, tm, name, carry=None):
    n = up.shape[0]
    hb = tm // HALO

    def body(up_ref, upp_ref, x1_ref, wfc_ref, bfc_ref, wd_ref, gc_ref, a_ref, x2_ref, acc):
        i = pl.program_id(0)
        acc[...] = x1_ref[...]
        for c0 in range(0, D_FF, FFN_CHUNK):
            cols = slice(c0, c0 + FFN_CHUNK)
            gate = up_ref[:, cols]
            val = up_ref[:, D_FF + c0:D_FF + c0 + FFN_CHUNK]
            gp = upp_ref[:, cols]
            gp = jnp.where(i > 0, gp, jnp.zeros_like(gp))
            w = wfc_ref[:, cols].astype(BF16)
            gc = (w[0:1] * _shift_down(gate, 2, gp) + w[1:2] * _shift_down(gate, 1, gp) + w[2:3] * gate
                  + bfc_ref[:, cols].astype(BF16))
            gc_ref[:, cols] = gc
            a = gc * _sigmoid(gc) * val
            a_ref[:, cols] = a
            acc[...] += _dot(a, wd_ref[cols, :])
        x2_ref[...] = acc[...]

    return _call(
        body, name=name, grid=(n // tm,), carry=carry,
        in_specs=[pl.BlockSpec((tm, 2 * D_FF), lambda i: (i, 0)),
                  pl.BlockSpec((HALO, D_FF), lambda i: (jnp.maximum(i * hb - 1, 0), 0)),
                  pl.BlockSpec((tm, D_MODEL), lambda i: (i, 0)),
                  pl.BlockSpec((3, D_FF), lambda i: (0, 0)),
                  pl.BlockSpec((1, D_FF), lambda i: (0, 0)),
                  pl.BlockSpec((D_FF, D_MODEL), lambda i: (0, 0))],
        out_specs=[pl.BlockSpec((tm, D_FF), lambda i: (i, 0)),
                   pl.BlockSpec((tm, D_FF), lambda i: (i, 0)),
                   pl.BlockSpec((tm, D_MODEL), lambda i: (i, 0))],
        out_shape=[jax.ShapeDtypeStruct((n, D_FF), BF16), jax.ShapeDtypeStruct((n, D_FF), BF16),
                   jax.ShapeDtypeStruct((n, D_MODEL), F32)],
        scratch_shapes=[pltpu.VMEM((tm, D_MODEL), F32)],
        args=(up, up, x1, w_fc, b_fc.reshape(1, -1), w_down))


def _loss_head(x, g, target, *, tm, name):
    n, d = x.shape

    def body(x_ref, g_ref, t_ref, dx_ref, loss_ref, dg_ref):
        i = pl.program_id(0)

        @pl.when(i == 0)
        def _():
            loss_ref[...] = jnp.zeros_like(loss_ref)
            dg_ref[...] = jnp.zeros_like(dg_ref)

        xv = x_ref[...]
        r = lax.rsqrt(jnp.mean(xv * xv, axis=-1, keepdims=True) + RMS_EPS)
        xh = xv * r
        gv = g_ref[...]
        e = xh * gv - t_ref[...]
        per_row = jnp.sum(e * e, axis=-1, keepdims=True) * (0.5 / d)
        loss_ref[...] += jnp.sum(per_row, axis=0, keepdims=True)
        dy = e * (1.0 / d)
        dg_ref[0:1, :] += jnp.sum(dy * xh, axis=0, keepdims=True)
        dxh = dy * gv
        dx_ref[...] = r * (dxh - xh * jnp.mean(dxh * xh, axis=-1, keepdims=True))

    outs, _ = _call(
        body, name=name, grid=(n // tm,),
        in_specs=[pl.BlockSpec((tm, d), lambda i: (i, 0)),
                  pl.BlockSpec((1, d), lambda i: (0, 0)),
                  pl.BlockSpec((tm, d), lambda i: (i, 0))],
        out_specs=[pl.BlockSpec((tm, d), lambda i: (i, 0)),
                   pl.BlockSpec((1, 1), lambda i: (0, 0)),
                   pl.BlockSpec((8, d), lambda i: (0, 0))],
        out_shape=[jax.ShapeDtypeStruct((n, d), F32), jax.ShapeDtypeStruct((1, 1), F32),
                   jax.ShapeDtypeStruct((8, d), F32)],
        args=(x, g.reshape(1, d), target))
    return outs


def _ffn_backward(dx2, up, gc, w_fc, w_down, *, tm, name, carry=None):
    n = up.shape[0]
    steps = n // tm
    hb = tm // HALO

    def body(dx_ref, dxn_ref, up_ref, upn_ref, gc_ref, gcn_ref, wfc_ref, wd_ref, dup_ref, dwc_ref):
        i = pl.program_id(0)
        last = i == steps - 1

        @pl.when(i == 0)
        def _():
            dwc_ref[...] = jnp.zeros_like(dwc_ref)

        dxe = jnp.concatenate([dx_ref[...], dxn_ref[...]], axis=0).astype(BF16)
        for c0 in range(0, D_FF, FFN_CHUNK):
            cols = slice(c0, c0 + FFN_CHUNK)
            vcols = slice(D_FF + c0, D_FF + c0 + FFN_CHUNK)
            dae = _dot_nt(dxe, wd_ref[cols, :])
            da, dan = dae[:tm], dae[tm:]
            gate = up_ref[:, cols]
            val = up_ref[:, vcols]
            gcv = gc_ref[:, cols]
            s = _sigmoid(gcv)
            dab = da.astype(BF16)
            dup_ref[:, vcols] = dab * (gcv * s)
            dgc = dab * val * (s * (1.0 + gcv * (1.0 - s)))
            gcn = gcn_ref[:, cols]
            sn = _sigmoid(gcn)
            dgcn = dan.astype(BF16) * upn_ref[:, vcols] * (sn * (1.0 + gcn * (1.0 - sn)))
            dgcn = jnp.where(last, jnp.zeros_like(dgcn), dgcn)
            up1 = _shift_up(dgc, 1, dgcn)
            up2 = _shift_up(dgc, 2, dgcn)
            w = wfc_ref[:, cols].astype(BF16)
            dup_ref[:, cols] = w[2:3] * dgc + w[1:2] * up1 + w[0:1] * up2
            dwc_ref[0:1, cols] += _column_sums(gate * up2)
            dwc_ref[1:2, cols] += _column_sums(gate * up1)
            dwc_ref[2:3, cols] += _column_sums(gate * dgc)
            dwc_ref[3:4, cols] += _column_sums(dgc)

    nxt = lambda i: (jnp.minimum((i + 1) * hb, steps * hb - 1), 0)
    return _call(
        body, name=name, grid=(steps,), carry=carry,
        in_specs=[pl.BlockSpec((tm, D_MODEL), lambda i: (i, 0)),
                  pl.BlockSpec((HALO, D_MODEL), nxt),
                  pl.BlockSpec((tm, 2 * D_FF), lambda i: (i, 0)),
                  pl.BlockSpec((HALO, 2 * D_FF), nxt),
                  pl.BlockSpec((tm, D_FF), lambda i: (i, 0)),
                  pl.BlockSpec((HALO, D_FF), nxt),
                  pl.BlockSpec((3, D_FF), lambda i: (0, 0)),
                  pl.BlockSpec((D_FF, D_MODEL), lambda i: (0, 0))],
        out_specs=[pl.BlockSpec((tm, 2 * D_FF), lambda i: (i, 0)),
                   pl.BlockSpec((8, D_FF), lambda i: (0, 0))],
        out_shape=[jax.ShapeDtypeStruct((n, 2 * D_FF), BF16), jax.ShapeDtypeStruct((8, D_FF), F32)],
        args=(dx2, dx2, up, up, gc, gc, w_fc, w_down))


def _matmul_norm_backward(dz, w_t, x, g, dres, *, tm, name, carry=None):
    n, c = dz.shape
    d = x.shape[1]
    ch = 512

    def body(dz_ref, wt_ref, x_ref, g_ref, dres_ref, dx_ref, dg_ref):
        i = pl.program_id(0)

        @pl.when(i == 0)
        def _():
            dg_ref[...] = jnp.zeros_like(dg_ref)

        dh = _dot(dz_ref[:, 0:ch], wt_ref[0:ch, :])
        for c0 in range(ch, c, ch):
            dh += _dot(dz_ref[:, c0:c0 + ch], wt_ref[c0:c0 + ch, :])
        xv = x_ref[...]
        r = lax.rsqrt(jnp.mean(xv * xv, axis=-1, keepdims=True) + RMS_EPS)
        xh = xv * r
        dg_ref[0:1, :] += jnp.sum(dh * xh, axis=0, keepdims=True)
        dxh = dh * g_ref[...]
        dx_ref[...] = dres_ref[...] + r * (dxh - xh * jnp.mean(dxh * xh, axis=-1, keepdims=True))

    return _call(
        body, name=name, grid=(n // tm,), carry=carry,
        in_specs=[pl.BlockSpec((tm, c), lambda i: (i, 0)),
                  pl.BlockSpec((c, d), lambda i: (0, 0)),
                  pl.BlockSpec((tm, d), lambda i: (i, 0)),
                  pl.BlockSpec((1, d), lambda i: (0, 0)),
                  pl.BlockSpec((tm, d), lambda i: (i, 0))],
        out_specs=[pl.BlockSpec((tm, d), lambda i: (i, 0)),
                   pl.BlockSpec((8, d), lambda i: (0, 0))],
        out_shape=[jax.ShapeDtypeStruct((n, d), F32), jax.ShapeDtypeStruct((8, d), F32)],
        args=(dz, w_t, x, g.reshape(1, d), dres))


def _mix_backward(dx1, z, conv, pa, pb, b_gate, ln_g, ln_b, w_s, w_s_t, b_s, w_sc, w_out, wb, *, tm, name, carry=None):
    n = z.shape[0]
    steps = n // tm
    hb = tm // HALO

    def body(dx_ref, dxn_ref, z_ref, zn_ref, cv_ref, pa_ref, pb_ref, bg_ref, lng_ref, lnb_ref, ws_ref, wst_ref,
             bs_ref, wsc_ref, wo_ref, wb_ref,
             dz_ref, dpa_ref, dpb_ref, dbg_ref, dln_ref, dws_ref, dbs_ref, dwsc_ref, f_scr, dvn_scr):
        i = pl.program_id(0)
        last = i == steps - 1

        @pl.when(i == 0)
        def _():
            dbg_ref[...] = jnp.zeros_like(dbg_ref)
            dln_ref[...] = jnp.zeros_like(dln_ref)
            dws_ref[...] = jnp.zeros_like(dws_ref)
            dbs_ref[...] = jnp.zeros_like(dbs_ref)
            dwsc_ref[...] = jnp.zeros_like(dwsc_ref)

        dxe = jnp.concatenate([dx_ref[...], dxn_ref[...]], axis=0).astype(BF16)
        dmge = _dot_nt(dxe, wo_ref[...])
        dmg, dmgn = dmge[:tm].astype(BF16), dmge[tm:].astype(BF16)

        pa_v = pa_ref[...]
        pb_v = pb_ref[...]
        bg = bg_ref[...].astype(BF16)
        sa = _sigmoid(z_ref[:, OFF_GA:OFF_GA + D_MODEL] + bg[:, 0:D_MODEL])
        sb = _sigmoid(z_ref[:, OFF_GB:OFF_GB + D_MODEL] + bg[:, D_MODEL:2 * D_MODEL])
        dpa = dmg * sa
        dpb = dmg * sb
        dga = dmg * pa_v * sa * (1.0 - sa)
        dgb = dmg * pb_v * sb * (1.0 - sb)
        dpa_ref[...] = dpa
        dpb_ref[...] = dpb
        dz_ref[:, OFF_GA:OFF_GA + D_MODEL] = dga
        dz_ref[:, OFF_GB:OFF_GB + D_MODEL] = dgb
        dbg_ref[0:1, 0:D_MODEL] += _column_sums(dga)
        dbg_ref[0:1, D_MODEL:2 * D_MODEL] += _column_sums(dgb)

        dya = _dot_nt(dpa, wb_ref[0]).astype(BF16)
        u = z_ref[:, OFF_U:OFF_U + D_A]
        v = z_ref[:, OFF_V:OFF_V + D_A].astype(F32)
        ln_g = lng_ref[...]
        gu, tu, tv, xh, rstd, vn, f = _gmlp_forward(u, v, ln_g, lnb_ref[...], ws_ref, bs_ref, f_scr)
        dgu = dya * f
        df_bf = dya * gu
        dz_ref[:, OFF_U:OFF_U + D_A] = dgu * _gelu_grad(u, tu)
        mask = _spatial_mask(False)
        mask_t = _spatial_mask(True)
        wmt = [jnp.where(mask_t, wst_ref[h], 0.0).astype(BF16) for h in range(N_HEADS)]
        for b in range(tm // GMLP_BLOCK):
            rows = slice(b * GMLP_BLOCK, (b + 1) * GMLP_BLOCK)
            for h in range(N_HEADS):
                cols = slice(h * HEAD, (h + 1) * HEAD)
                dfb = df_bf[rows, cols]
                dvn_scr[rows, cols] = _dot(wmt[h], dfb)
                dws_ref[h] += jnp.where(mask, _dot_nt(dfb, vn[rows, cols]), 0.0)
                dbs_ref[h] += jnp.sum(dfb.astype(F32), axis=1, keepdims=True)
        dvn = dvn_scr[...]
        dln_ref[0:1, :] += jnp.sum(dvn * xh, axis=0, keepdims=True)
        dln_ref[1:2, :] += jnp.sum(dvn, axis=0, keepdims=True)
        dxh = dvn * ln_g
        dgv = rstd * (dxh - jnp.mean(dxh, axis=-1, keepdims=True) - xh * jnp.mean(dxh * xh, axis=-1, keepdims=True))
        dz_ref[:, OFF_V:OFF_V + D_A] = (dgv * _gelu_grad(v, tv)).astype(BF16)

        sbn = _sigmoid(zn_ref[:, OFF_GB:OFF_GB + D_MODEL] + bg[:, D_MODEL:2 * D_MODEL])
        dpbe = jnp.concatenate([dpb, dmgn * sbn], axis=0)
        dybe = _dot_nt(dpbe, wb_ref[1])
        dyb, dybn = dybe[:tm].astype(BF16), dybe[tm:].astype(BF16)
        bgv = z_ref[:, OFF_BG:OFF_BG + D_B]
        cg = z_ref[:, OFF_CG:OFF_CG + D_B]
        hbv = z_ref[:, OFF_HB:OFF_HB + D_B]
        q = cg * hbv
        dz_ref[:, OFF_BG:OFF_BG + D_B] = dyb * cv_ref[...]
        dconv = dyb * bgv
        dconvn = dybn * zn_ref[:, OFF_BG:OFF_BG + D_B]
        dconvn = jnp.where(last, jnp.zeros_like(dconvn), dconvn)
        up1 = _shift_up(dconv, 1, dconvn)
        up2 = _shift_up(dconv, 2, dconvn)
        dwsc_ref[0:1, :] += _column_sums(q * up2)
        dwsc_ref[1:2, :] += _column_sums(q * up1)
        dwsc_ref[2:3, :] += _column_sums(q * dconv)
        w = wsc_ref[...].astype(BF16)
        dq = w[2:3] * dconv + w[1:2] * up1 + w[0:1] * up2
        dz_ref[:, OFF_CG:OFF_CG + D_B] = dq * hbv
        dz_ref[:, OFF_HB:OFF_HB + D_B] = dq * cg

    row = lambda w: pl.BlockSpec((tm, w), lambda i: (i, 0))
    full = lambda *s: pl.BlockSpec(s, lambda i: (0,) * len(s))
    nxt = lambda i: (jnp.minimum((i + 1) * hb, steps * hb - 1), 0)
    return _call(
        body, name=name, grid=(steps,), carry=carry,
        in_specs=[row(D_MODEL), pl.BlockSpec((HALO, D_MODEL), nxt),
                  row(D_IN), pl.BlockSpec((HALO, D_IN), nxt),
                  row(D_B), row(D_MODEL), row(D_MODEL),
                  full(1, 2 * D_MODEL), full(1, D_A), full(1, D_A),
                  full(N_HEADS, GMLP_BLOCK, GMLP_BLOCK), full(N_HEADS, GMLP_BLOCK, GMLP_BLOCK),
                  full(N_HEADS, GMLP_BLOCK, 1), full(3, D_B),
                  full(D_MODEL, D_MODEL), full(2, D_A, D_MODEL)],
        out_specs=[row(D_IN), row(D_MODEL), row(D_MODEL),
                   full(8, 2 * D_MODEL), full(8, D_A), full(N_HEADS, GMLP_BLOCK, GMLP_BLOCK),
                   full(N_HEADS, GMLP_BLOCK, 1), full(8, D_B)],
        out_shape=[jax.ShapeDtypeStruct((n, D_IN), BF16), jax.ShapeDtypeStruct((n, D_MODEL), BF16),
                   jax.ShapeDtypeStruct((n, D_MODEL), BF16),
                   jax.ShapeDtypeStruct((8, 2 * D_MODEL), F32), jax.ShapeDtypeStruct((8, D_A), F32),
                   jax.ShapeDtypeStruct((N_HEADS, GMLP_BLOCK, GMLP_BLOCK), F32),
                   jax.ShapeDtypeStruct((N_HEADS, GMLP_BLOCK, 1), F32), jax.ShapeDtypeStruct((8, D_B), F32)],
        scratch_shapes=[pltpu.VMEM((tm, D_A), BF16), pltpu.VMEM((tm, D_A), F32)],
        args=(dx1, dx1, z, z, conv, pa, pb, b_gate.reshape(1, -1), ln_g.reshape(1, -1), ln_b.reshape(1, -1), w_s, w_s_t,
              b_s.reshape(N_HEADS, GMLP_BLOCK, 1), w_sc, w_out, wb))


def _matmul_tn(a, b, *, t1, tn, name, carry=None):
    n, k1 = a.shape
    k2 = b.shape[1]
    steps = n // tn

    def body(a_ref, b_ref, o_ref, acc):
        s = pl.program_id(1)

        @pl.when(s == 0)
        def _():
            acc[...] = jnp.zeros_like(acc)

        acc[...] += lax.dot_general(a_ref[...].astype(BF16), b_ref[...].astype(BF16), TN, preferred_element_type=F32)

        @pl.when(s == steps - 1)
        def _():
            o_ref[...] = acc[...].astype(BF16)

    outs, carried = _call(
        body, name=name, grid=(k1 // t1, steps), carry=carry,
        in_specs=[pl.BlockSpec((tn, t1), lambda i, s: (s, i)),
                  pl.BlockSpec((tn, k2), lambda i, s: (s, 0))],
        out_specs=[pl.BlockSpec((t1, k2), lambda i, s: (i, 0))],
        out_shape=[jax.ShapeDtypeStruct((k1, k2), BF16)],
        scratch_shapes=[pltpu.VMEM((t1, k2), F32)],
        args=(a, b))
    return outs[0], carried


def _adamw_math(w, g, m, v):
    m = ADAM_B1 * m + (1.0 - ADAM_B1) * g
    v = ADAM_B2 * v + (1.0 - ADAM_B2) * (g * g)
    m_hat = m / (1.0 - ADAM_B1 ** ADAM_STEP)
    v_hat = v / (1.0 - ADAM_B2 ** ADAM_STEP)
    delta = -ADAM_LR * (m_hat / (jnp.sqrt(v_hat) + ADAM_EPS) + ADAM_WD * w)
    return delta, m, v


def _sum_parts(recvs, *, tr, name):
    _, r, c = recvs[0].shape

    def body(*refs):
        recv_refs, g_ref = refs[:DEPTH], refs[DEPTH]
        layer = pl.program_id(0)
        for l in range(DEPTH):
            @pl.when(layer == l)
            def _(l=l):
                g = recv_refs[l][0].astype(F32)
                for s in range(1, N_DEV):
                    g = g + recv_refs[l][s].astype(F32)
                g_ref[0] = g

    outs, _ = _call(
        body, name=name, grid=(DEPTH, r // tr),
        in_specs=[pl.BlockSpec((N_DEV, tr, c), lambda l, i: (0, i, 0))] * DEPTH,
        out_specs=[pl.BlockSpec((1, tr, c), lambda l, i: (l, i, 0))],
        out_shape=[jax.ShapeDtypeStruct((DEPTH, r, c), F32)],
        args=tuple(recvs))
    return outs[0]


def _adamw(w, g, m, v, *, tr, name):
    r, c = w.shape

    def body(w_ref, g_ref, m_ref, v_ref, d_ref, nm_ref, nv_ref):
        delta, nm, nv = _adamw_math(w_ref[...], g_ref[...], m_ref[...], v_ref[...])
        d_ref[...] = delta
        nm_ref[...] = nm
        nv_ref[...] = nv

    spec = pl.BlockSpec((tr, c), lambda i: (i, 0))
    outs, _ = _call(body, name=name, grid=(r // tr,), in_specs=[spec] * 4, out_specs=[spec] * 3,
                    out_shape=[jax.ShapeDtypeStruct((r, c), F32)] * 3, args=(w, g, m, v))
    return outs


def _sum_adamw(recvs, w, m, v, *, tr, name):
    _, r, c = recvs[0].shape

    def body(*refs):
        recv_refs = refs[:DEPTH]
        w_ref, m_ref, v_ref, g_ref, d_ref, nm_ref, nv_ref = refs[DEPTH:]
        layer = pl.program_id(0)
        for l in range(DEPTH):
            @pl.when(layer == l)
            def _(l=l):
                g = recv_refs[l][0].astype(F32)
                for s in range(1, N_DEV):
                    g = g + recv_refs[l][s].astype(F32)
                delta, nm, nv = _adamw_math(w_ref[0], g, m_ref[0], v_ref[0])
                g_ref[0] = g
                d_ref[0] = delta
                nm_ref[0] = nm
                nv_ref[0] = nv

    spec = pl.BlockSpec((1, tr, c), lambda l, i: (l, i, 0))
    outs, _ = _call(
        body, name=name, grid=(DEPTH, r // tr),
        in_specs=[pl.BlockSpec((N_DEV, tr, c), lambda l, i: (0, i, 0))] * DEPTH + [spec] * 3,
        out_specs=[spec] * 4, out_shape=[jax.ShapeDtypeStruct((DEPTH, r, c), F32)] * 4,
        args=tuple(recvs) + (w, m, v))
    return outs


def _adamw_small(w, g, m, v, *, name):
    def body(w_ref, g_ref, m_ref, v_ref, d_ref, nm_ref, nv_ref):
        delta, nm, nv = _adamw_math(w_ref[...], g_ref[...], m_ref[...], v_ref[...])
        d_ref[...] = delta
        nm_ref[...] = nm
        nv_ref[...] = nv

    vmem = pl.BlockSpec(memory_space=pltpu.VMEM)
    return pl.pallas_call(
        body, name=name, in_specs=[vmem] * 4, out_specs=[vmem] * 3,
        out_shape=[jax.ShapeDtypeStruct(w.shape, F32)] * 3,
    )(w, g, m, v)


def _pack(arrs):
    flat = jnp.concatenate([a.reshape(-1) for a in arrs])
    pad = (-flat.shape[0]) % 1024
    return jnp.pad(flat, (0, pad)).reshape(-1, 128)


def _unpack(packed, shapes):
    flat = packed.reshape(-1)
    out, o = [], 0
    for s in shapes:
        size = math.prod(s)
        out.append(flat[o:o + size].reshape(s))
        o += size
    return out


def _rows(gathered):
    return gathered.reshape(N_DEV * gathered.shape[1], gathered.shape[2])


def _parts(full):
    return full.reshape(N_DEV, full.shape[0] // N_DEV, full.shape[1])


def kernel(x, norm1_g, w_in, b_gate, gmlp_ln_g, gmlp_ln_b, w_spatial, b_spatial, w_shortconv, w_branch, w_out, norm2_g, w_ffn_up, w_ffn_conv, b_ffn_conv, w_ffn_down, final_g, loss_target, m_norm1_g, m_w_in, m_b_gate, m_gmlp_ln_g, m_gmlp_ln_b, m_w_spatial, m_b_spatial, m_w_shortconv, m_w_branch, m_w_out, m_norm2_g, m_w_ffn_up, m_w_ffn_conv, m_b_ffn_conv, m_w_ffn_down, m_final_g, v_norm1_g, v_w_in, v_b_gate, v_gmlp_ln_g, v_gmlp_ln_b, v_w_spatial, v_b_spatial, v_w_shortconv, v_w_branch, v_w_out, v_norm2_g, v_w_ffn_up, v_w_ffn_conv, v_b_ffn_conv, v_w_ffn_down, v_final_g):
    n = x.shape[1]
    tm, tm_mix, tn = 512, 512, 1024
    x0 = x.reshape(n, D_MODEL)
    target = loss_target.reshape(n, D_MODEL)
    my_idx = 4 * lax.axis_index("x") + 2 * lax.axis_index("y") + lax.axis_index("c")
    sc_w, fc_w = D_B // N_DEV, D_FF // N_DEV

    sh_in = [w_in[l].T.astype(BF16) for l in range(DEPTH)]
    sh_up = [w_ffn_up[l].T.astype(BF16) for l in range(DEPTH)]
    sh_br = [w_branch[l].astype(BF16) for l in range(DEPTH)]
    sh_out = [w_out[l].astype(BF16) for l in range(DEPTH)]
    sh_down = [w_ffn_down[l].astype(BF16) for l in range(DEPTH)]
    taps = jnp.concatenate([w_shortconv, w_ffn_conv], axis=-1)

    def branch_weights(g):
        return g.transpose(1, 2, 0, 3).reshape(2, D_A, D_MODEL)

    g_in0, g_taps = _gather_now([sh_in[0], taps], name="gather_first")
    w_sc = [g_taps[:, l, :, :sc_w].transpose(1, 0, 2).reshape(3, D_B) for l in range(DEPTH)]
    w_fc = [g_taps[:, l, :, sc_w:].transpose(1, 0, 2).reshape(3, D_FF) for l in range(DEPTH)]
    w_s_t = [w_spatial[l].transpose(0, 2, 1) for l in range(DEPTH)]
    weights = [dict(), dict()]
    weights[0]["in_t"] = _rows(g_in0)
    saved = []
    xc = x0
    for l in range(DEPTH):
        p = weights[l]
        carry = _Gather([sh_br[0], sh_out[0]] if l == 0 else [sh_down[1]])
        (h, z), got = _norm_matmul(xc, norm1_g[l], p["in_t"], tm=tm, name=f"fwd_in_{l}", carry=carry)
        if l == 0:
            p["wb"], p["out"] = branch_weights(got[0]), _rows(got[1])
        else:
            p["down"] = _rows(got[0])
        carry = _Gather([sh_up[0]]) if l == 0 else None
        (ya, yb, conv, pa, pb, mg, x1), got = _mix_forward(
            z, xc, b_gate[l], gmlp_ln_g[l], gmlp_ln_b[l], w_spatial[l], b_spatial[l], w_sc[l], p["wb"], p["out"],
            tm=tm_mix, name=f"fwd_mix_{l}", carry=carry)
        if l == 0:
            p["up_t"] = _rows(got[0])
        carry = _Gather([sh_down[0], sh_in[1]]) if l == 0 else None
        (h2, up), got = _norm_matmul(x1, norm2_g[l], p["up_t"], tm=tm, name=f"fwd_up_{l}", carry=carry)
        if l == 0:
            p["down"], weights[1]["in_t"] = _rows(got[0]), _rows(got[1])
        carry = _Gather([sh_br[1], sh_out[1], sh_up[1]]) if l == 0 else None
        (gc, a, x2), got = _ffn_forward(up, x1, w_fc[l], b_ffn_conv[l], p["down"], tm=tm, name=f"fwd_ffn_{l}", carry=carry)
        if l == 0:
            weights[1]["wb"], weights[1]["out"], weights[1]["up_t"] = branch_weights(got[0]), _rows(got[1]), _rows(got[2])
        saved.append(dict(x=xc, h=h, z=z, ya=ya, yb=yb, conv=conv, pa=pa, pb=pb, mg=mg, x1=x1, h2=h2, up=up, gc=gc, a=a))
        xc = x2
    dx, loss_part, dgf = _loss_head(xc, final_g, target, tm=tm, name="loss_head")

    recv = [dict(), dict()]
    small = [None] * DEPTH
    pending_in = None
    for l in reversed(range(DEPTH)):
        p, s = weights[l], saved[l]
        carry = _Exchange([pending_in]) if pending_in is not None else None
        (dup, dwc), got = _ffn_backward(dx, s["up"], s["gc"], w_fc[l], p["down"], tm=tm, name=f"bwd_ffn_{l}", carry=carry)
        if got is not None:
            recv[l + 1]["in_t"] = got[0]
        dw_down, _ = _matmul_tn(s["a"], dx, t1=D_FF // 2, tn=tn, name=f"dw_down_{l}")
        dw_up_t, got = _matmul_tn(dup, s["h2"], t1=2 * D_FF // 4, tn=tn, name=f"dw_up_{l}", carry=_Exchange([_parts(dw_down)]))
        recv[l]["down"] = got[0]
        (dx1, dg2), _ = _matmul_norm_backward(dup, p["up_t"], s["x1"], norm2_g[l], dx, tm=tm, name=f"bwd_up_{l}")
        (dz, dpa, dpb, dbg, dln, dws, dbs, dwsc), got = _mix_backward(
            dx1, s["z"], s["conv"], s["pa"], s["pb"], b_gate[l], gmlp_ln_g[l], gmlp_ln_b[l], w_spatial[l], w_s_t[l],
            b_spatial[l], w_sc[l], p["out"], p["wb"], tm=tm_mix, name=f"bwd_mix_{l}", carry=_Exchange([_parts(dw_up_t)]))
        recv[l]["up_t"] = got[0]
        dw_out, _ = _matmul_tn(s["mg"], dx1, t1=D_MODEL, tn=tn, name=f"dw_out_{l}")
        dw_bra_t, _ = _matmul_tn(dpa, s["ya"], t1=D_MODEL, tn=tn, name=f"dw_branch_a_{l}")
        dw_brb_t, _ = _matmul_tn(dpb, s["yb"], t1=D_MODEL, tn=tn, name=f"dw_branch_b_{l}")
        dw_in_t, got = _matmul_tn(dz, s["h"], t1=D_IN // 4, tn=tn, name=f"dw_in_{l}",
                                  carry=_Exchange([_parts(dw_out), _parts(dw_bra_t), _parts(dw_brb_t)]))
        recv[l]["out"], recv[l]["bra_t"], recv[l]["brb_t"] = got
        carry = _Exchange([_parts(dw_in_t)]) if l == 0 else None
        (dx0, dg1), got = _matmul_norm_backward(dz, p["in_t"], s["x"], norm1_g[l], dx1, tm=tm, name=f"bwd_in_{l}", carry=carry)
        if l == 0:
            recv[0]["in_t"] = got[0]
        else:
            pending_in = _parts(dw_in_t)
        small[l] = dict(norm1_g=dg1[0], b_gate=dbg[0], gmlp_ln_g=dln[0], gmlp_ln_b=dln[1], w_spatial=dws,
                        b_spatial=dbs.reshape(N_HEADS, GMLP_BLOCK), w_shortconv=dwsc[0:3], norm2_g=dg2[0],
                        w_ffn_conv=dwc[0:3], b_ffn_conv=dwc[3])
        dx = dx0
    grad_x = dx.reshape(x.shape)

    results = {}
    both = lambda key: [recv[l][key] for l in range(DEPTH)]
    swap = lambda t: t.transpose(0, 2, 1)
    for key, slab, (w, m, v), tr in [("w_in", "in_t", (w_in, m_w_in, v_w_in), 192),
                                     ("w_ffn_up", "up_t", (w_ffn_up, m_w_ffn_up, v_w_ffn_up), 176)]:
        outs = _sum_adamw(both(slab), swap(w), swap(m), swap(v), tr=tr, name=f"adamw_{key}")
        results[key] = tuple(swap(o) for o in outs)
    g_bra = _sum_parts(both("bra_t"), tr=128, name="sum_w_branch_a").transpose(0, 2, 1)
    g_brb = _sum_parts(both("brb_t"), tr=128, name="sum_w_branch_b").transpose(0, 2, 1)
    g_br = jnp.stack([g_bra, g_brb], axis=1)
    flat = lambda t: t.reshape(-1, t.shape[-1])
    outs = _adamw(flat(w_branch), flat(g_br), flat(m_w_branch), flat(v_w_branch), tr=512, name="adamw_w_branch")
    results["w_branch"] = (g_br,) + tuple(o.reshape(w_branch.shape) for o in outs)
    results["w_out"] = tuple(_sum_adamw(both("out"), w_out, m_w_out, v_w_out, tr=128, name="adamw_w_out"))
    results["w_ffn_down"] = tuple(_sum_adamw(both("down"), w_ffn_down, m_w_ffn_down, v_w_ffn_down, tr=176,
                                             name="adamw_w_ffn_down"))

    small_names = ["norm1_g", "b_gate", "gmlp_ln_g", "gmlp_ln_b", "w_spatial", "b_spatial", "w_shortconv", "norm2_g",
                   "w_ffn_conv", "b_ffn_conv"]
    stacked = [jnp.stack([small[l][k] for l in range(DEPTH)]) for k in small_names] + [dgf[0]]
    shapes = [a.shape for a in stacked]
    reduced = _unpack(_all_reduce_small(_pack(stacked), name="all_reduce_small_grads"), shapes)
    g_small = dict(zip(small_names + ["final_g"], reduced))
    g_small["w_shortconv"] = lax.dynamic_slice_in_dim(g_small["w_shortconv"], my_idx * sc_w, sc_w, axis=2)
    g_small["w_ffn_conv"] = lax.dynamic_slice_in_dim(g_small["w_ffn_conv"], my_idx * fc_w, fc_w, axis=2)
    small_w = dict(norm1_g=(norm1_g, m_norm1_g, v_norm1_g), b_gate=(b_gate, m_b_gate, v_b_gate),
                   gmlp_ln_g=(gmlp_ln_g, m_gmlp_ln_g, v_gmlp_ln_g), gmlp_ln_b=(gmlp_ln_b, m_gmlp_ln_b, v_gmlp_ln_b),
                   w_spatial=(w_spatial, m_w_spatial, v_w_spatial), b_spatial=(b_spatial, m_b_spatial, v_b_spatial),
                   w_shortconv=(w_shortconv, m_w_shortconv, v_w_shortconv), norm2_g=(norm2_g, m_norm2_g, v_norm2_g),
                   w_ffn_conv=(w_ffn_conv, m_w_ffn_conv, v_w_ffn_conv), b_ffn_conv=(b_ffn_conv, m_b_ffn_conv, v_b_ffn_conv),
                   final_g=(final_g, m_final_g, v_final_g))
    order = small_names + ["final_g"]
    local_shapes = [small_w[k][0].shape for k in order]
    packed = [_pack([small_w[k][j] for k in order]) for j in range(3)]
    d_s, m_s, v_s = _adamw_small(packed[0], _pack([g_small[k] for k in order]), packed[1], packed[2], name="adamw_small")
    d_s, m_s, v_s = _unpack(d_s, local_shapes), _unpack(m_s, local_shapes), _unpack(v_s, local_shapes)
    for j, k in enumerate(order):
        results[k] = (g_small[k], d_s[j], m_s[j], v_s[j])

    loss = lax.psum(loss_part[0, 0], MESH_AXES)
    names = ["norm1_g", "w_in", "b_gate", "gmlp_ln_g", "gmlp_ln_b", "w_spatial", "b_spatial", "w_shortconv", "w_branch",
             "w_out", "norm2_g", "w_ffn_up", "w_ffn_conv", "b_ffn_conv", "w_ffn_down", "final_g"]
    return (loss, grad_x, *[results[k][0] for k in names], *[results[k][1] for k in names],
            *[results[k][2] for k in names], *[results[k][3] for k in names])
```

```python
import math

import jax
import jax.numpy as jnp
from jax import lax
from jax.experimental import pallas as pl
from jax.experimental.pallas import tpu as pltpu

F32 = jnp.float32
BF16 = jnp.bfloat16

N_DEV = 8
DEPTH = 2
D_MODEL = 1024
D_A = 512
D_B = 512
D_FF = 2816
D_IN = 4608
N_HEADS = 4
HEAD = 128
GMLP_BLOCK = 128
CAUSAL_CHUNK = 64
OFF_U, OFF_V, OFF_BG, OFF_CG, OFF_HB, OFF_GA, OFF_GB = 0, 512, 1024, 1536, 2048, 2560, 3584
RMS_EPS = 1e-6
LN_EPS = 1e-5
ADAM_LR, ADAM_B1, ADAM_B2, ADAM_EPS, ADAM_WD, ADAM_STEP = 0.001, 0.9, 0.999, 1e-08, 0.01, 10

SUBLANES = 8
HALO = 16
FFN_CHUNK = 256
V7X_VMEM_BYTES = 64 << 20
VMEM_LIMIT = V7X_VMEM_BYTES - (8 << 20)
MESH = pl.DeviceIdType.MESH
GELU_C0 = 0.7978845608028654
GELU_C1 = 0.044715
NT = (((1,), (1,)), ((), ()))
TN = (((0,), (0,)), ((), ()))


def _dot(a, b):
    return jnp.dot(a, b, preferred_element_type=F32)


def _dot_nt(a, b):
    return lax.dot_general(a, b, NT, preferred_element_type=F32)


def _sigmoid(x):
    return 1.0 / (1.0 + jnp.exp(-x))


def _gelu_tanh(x):
    return jnp.tanh(GELU_C0 * (x + GELU_C1 * x * x * x))


def _gelu_grad(x, t):
    return 0.5 * (1.0 + t) + 0.5 * x * (1.0 - t * t) * GELU_C0 * (1.0 + 3.0 * GELU_C1 * x * x)


def _sublane_tile(dtype):
    return SUBLANES * (4 // jnp.dtype(dtype).itemsize)


def _shift_down(a, k, prev):
    p = prev.shape[0]
    r = pltpu.roll(a, k, 0)
    sub = _sublane_tile(a.dtype)
    head = r[0:sub]
    rid = lax.broadcasted_iota(jnp.int32, head.shape, 0)
    for j in range(k):
        head = jnp.where(rid == j, prev[p - k + j:p - k + j + 1, :], head)
    return jnp.concatenate([head, r[sub:]], axis=0)


def _shift_up(a, k, nxt):
    t = a.shape[0]
    r = pltpu.roll(a, t - k, 0)
    sub = _sublane_tile(a.dtype)
    tail = r[t - sub:t]
    rid = lax.broadcasted_iota(jnp.int32, tail.shape, 0)
    for j in range(k):
        tail = jnp.where(rid == sub - k + j, nxt[j:j + 1, :], tail)
    return jnp.concatenate([r[0:t - sub], tail], axis=0)


def _column_sums(p):
    if p.dtype.itemsize < 4:
        t = p.shape[0]
        p = p[:t // 2] + p[t // 2:]
        p = p[:t // 4] + p[t // 4:]
    return jnp.sum(p.astype(F32), axis=0, keepdims=True)


def _spatial_mask(transposed):
    ri = lax.broadcasted_iota(jnp.int32, (GMLP_BLOCK, GMLP_BLOCK), 0) // CAUSAL_CHUNK
    ci = lax.broadcasted_iota(jnp.int32, (GMLP_BLOCK, GMLP_BLOCK), 1) // CAUSAL_CHUNK
    return (ri <= ci) if transposed else (ci <= ri)


def _gmlp_forward(u, v, ln_g, ln_b, ws_ref, bs_ref, f_scr):
    tm = u.shape[0]
    tu = _gelu_tanh(u)
    tv = _gelu_tanh(v)
    gu = 0.5 * u * (1.0 + tu)
    gv = 0.5 * v * (1.0 + tv)
    mu = jnp.mean(gv, axis=-1, keepdims=True)
    cen = gv - mu
    rstd = lax.rsqrt(jnp.mean(cen * cen, axis=-1, keepdims=True) + LN_EPS)
    xh = cen * rstd
    vn = (xh * ln_g + ln_b).astype(BF16)
    mask = _spatial_mask(False)
    wm = [jnp.where(mask, ws_ref[h], 0.0).astype(BF16) for h in range(N_HEADS)]
    for b in range(tm // GMLP_BLOCK):
        rows = slice(b * GMLP_BLOCK, (b + 1) * GMLP_BLOCK)
        for h in range(N_HEADS):
            cols = slice(h * HEAD, (h + 1) * HEAD)
            f_scr[rows, cols] = (_dot(wm[h], vn[rows, cols]) + bs_ref[h]).astype(f_scr.dtype)
    return gu, tu, tv, xh, rstd, vn, f_scr[...]


def _position():
    return lax.axis_index("x"), lax.axis_index("y"), lax.axis_index("c")


class _Gather:
    def __init__(self, arrays):
        self.arrays = list(arrays)
        self.out_shape = [jax.ShapeDtypeStruct((N_DEV,) + a.shape, a.dtype) for a in self.arrays]

    def _plan(self, ins, outs, sems):
        send_sems, recv_sems, local_sems = sems
        x, y, c = _position()
        me, sibling = (x, y, c), (x, y, 1 - c)
        chips = [(1 - x, y), (x, 1 - y), (1 - x, 1 - y)]

        def slot(a, p):
            return outs[a].at[4 * p[0] + 2 * p[1] + p[2]]

        def copy(a, k, block, to, src=None):
            return pltpu.make_async_remote_copy(
                src_ref=slot(a, block) if src is None else src, dst_ref=slot(a, block),
                send_sem=send_sems.at[a, k], recv_sem=recv_sems.at[a, k], device_id=to, device_id_type=MESH)

        n = len(self.arrays)

        def mine():
            return [pltpu.make_async_copy(ins[a], slot(a, me), local_sems.at[a]) for a in range(n)]

        def first():
            out = []
            for a in range(n):
                out.append(copy(a, 0, me, sibling, src=ins[a]))
                out += [copy(a, 1 + j, me, (*chip, c), src=ins[a]) for j, chip in enumerate(chips)]
            return out

        def arrivals():
            return [copy(a, 1 + j, (*chip, c), me) for j, chip in enumerate(chips) for a in range(n)]

        def relays():
            return [copy(a, 4 + j, (*chip, c), sibling) for j, chip in enumerate(chips) for a in range(n)]

        def from_sibling():
            out = [copy(a, 0, sibling, me) for a in range(n)]
            return out + [copy(a, 4 + j, (*chip, 1 - c), me) for j, chip in enumerate(chips) for a in range(n)]

        return mine, first, arrivals, relays, from_sibling

    def start(self, ins, outs, sems):
        mine, first, _, _, _ = self._plan(ins, outs, sems)
        for cp in mine() + first():
            cp.start()

    def relay(self, ins, outs, sems):
        _, _, arrivals, relays, _ = self._plan(ins, outs, sems)
        for arrived, onward in zip(arrivals(), relays()):
            arrived.wait_recv()
            onward.start()

    def finish(self, ins, outs, sems):
        mine, first, _, relays, from_sibling = self._plan(ins, outs, sems)
        for cp in from_sibling():
            cp.wait_recv()
        for cp in first() + relays():
            cp.wait_send()
        for cp in mine():
            cp.wait()


class _Exchange:
    def __init__(self, arrays):
        self.arrays = list(arrays)
        self.out_shape = [jax.ShapeDtypeStruct(a.shape, a.dtype) for a in self.arrays]

    def _plan(self, ins, outs, sems):
        send_sems, recv_sems, local_sems = sems
        x, y, c = _position()
        my_idx = 4 * x + 2 * y + c
        n = len(self.arrays)
        offsets = [(dx, dy, dc) for dx in (0, 1) for dy in (0, 1) for dc in (0, 1) if (dx, dy, dc) != (0, 0, 0)]

        def mine():
            return [pltpu.make_async_copy(ins[a].at[my_idx], outs[a].at[my_idx], local_sems.at[a]) for a in range(n)]

        def remote(arriving):
            out = []
            for k, (dx, dy, dc) in enumerate(offsets):
                px, py, pc = x ^ dx, y ^ dy, c ^ dc
                p_idx = 4 * px + 2 * py + pc
                for a in range(n):
                    out.append(pltpu.make_async_remote_copy(
                        src_ref=ins[a].at[p_idx], dst_ref=outs[a].at[p_idx if arriving else my_idx],
                        send_sem=send_sems.at[a, k], recv_sem=recv_sems.at[a, k],
                        device_id=(px, py, pc), device_id_type=MESH))
            return out

        return mine, remote

    def start(self, ins, outs, sems):
        mine, remote = self._plan(ins, outs, sems)
        for cp in mine() + remote(False):
            cp.start()

    def relay(self, ins, outs, sems):
        pass

    def finish(self, ins, outs, sems):
        mine, remote = self._plan(ins, outs, sems)
        for cp in remote(True):
            cp.wait_recv()
        for cp in remote(False):
            cp.wait_send()
        for cp in mine():
            cp.wait()


def _call(body, *, name, grid, in_specs, out_specs, out_shape, args, scratch_shapes=(), carry=None):
    n_in, n_out, n_scr = len(in_specs), len(out_specs), len(scratch_shapes)
    params = pltpu.CompilerParams(dimension_semantics=("arbitrary",) * len(grid), vmem_limit_bytes=VMEM_LIMIT)
    if carry is None:
        outs = pl.pallas_call(body, name=name, grid=grid, in_specs=in_specs, out_specs=out_specs, out_shape=out_shape,
                              scratch_shapes=list(scratch_shapes), compiler_params=params)(*args)
        return outs, None
    m = len(carry.arrays)
    total = math.prod(grid)

    def wrapped(*refs):
        ins, refs = refs[:n_in], refs[n_in:]
        c_ins, refs = refs[:m], refs[m:]
        outs, refs = refs[:n_out], refs[n_out:]
        c_outs, refs = refs[:m], refs[m:]
        scr, sems = refs[:n_scr], refs[n_scr:]
        flat = pl.program_id(0)
        for d in range(1, len(grid)):
            flat = flat * grid[d] + pl.program_id(d)

        @pl.when(flat == 0)
        def _():
            carry.start(c_ins, c_outs, sems)

        body(*ins, *outs, *scr)

        @pl.when(flat == total - 2)
        def _():
            carry.relay(c_ins, c_outs, sems)

        @pl.when(flat == total - 1)
        def _():
            carry.finish(c_ins, c_outs, sems)

    any_spec = pl.BlockSpec(memory_space=pl.ANY)
    sem_shapes = [pltpu.SemaphoreType.DMA((m, 7)), pltpu.SemaphoreType.DMA((m, 7)), pltpu.SemaphoreType.DMA((m,))]
    outs = pl.pallas_call(
        wrapped, name=name, grid=grid,
        in_specs=list(in_specs) + [any_spec] * m, out_specs=list(out_specs) + [any_spec] * m,
        out_shape=list(out_shape) + carry.out_shape,
        scratch_shapes=list(scratch_shapes) + sem_shapes, compiler_params=params)(*args, *carry.arrays)
    return outs[:n_out], outs[n_out:]


def _gather_now(arrays, *, name):
    carry = _Gather(arrays)
    m = len(arrays)

    def body(*refs):
        ins, outs, sems = refs[:m], refs[m:2 * m], refs[2 * m:]
        carry.start(ins, outs, sems)
        carry.relay(ins, outs, sems)
        carry.finish(ins, outs, sems)

    any_spec = pl.BlockSpec(memory_space=pl.ANY)
    return pl.pallas_call(
        body, name=name, in_specs=[any_spec] * m, out_specs=[any_spec] * m, out_shape=carry.out_shape,
        scratch_shapes=[pltpu.SemaphoreType.DMA((m, 7)), pltpu.SemaphoreType.DMA((m, 7)),
                        pltpu.SemaphoreType.DMA((m,))],
    )(*arrays)


def _all_reduce_small(p, *, name):
    rows, lanes = p.shape

    def body(p_ref, out_ref, buf, send_sems, recv_sems):
        x, y, c = _position()
        me, sibling = (x, y, c), (x, y, 1 - c)
        chips = [(1 - x, y), (x, 1 - y), (1 - x, 1 - y)]

        def slot(q):
            return buf.at[4 * q[0] + 2 * q[1] + q[2]]

        def copy(k, block, to, src=None):
            return pltpu.make_async_remote_copy(
                src_ref=slot(block) if src is None else src, dst_ref=slot(block),
                send_sem=send_sems.at[k], recv_sem=recv_sems.at[k], device_id=to, device_id_type=MESH)

        first = [copy(0, me, sibling, src=p_ref)]
        first += [copy(1 + j, me, (*chip, c), src=p_ref) for j, chip in enumerate(chips)]
        for cp in first:
            cp.start()
        passed = []
        for j, chip in enumerate(chips):
            copy(1 + j, (*chip, c), me).wait_recv()
            cp = copy(4 + j, (*chip, c), sibling)
            cp.start()
            passed.append(cp)
        copy(0, sibling, me).wait_recv()
        for j, chip in enumerate(chips):
            copy(4 + j, (*chip, 1 - c), me).wait_recv()
        for cp in first + passed:
            cp.wait_send()
        my_idx = 4 * x + 2 * y + c
        acc = jnp.zeros((rows, lanes), F32)
        for s in range(N_DEV):
            acc = acc + jnp.where(my_idx == s, p_ref[...], buf[s])
        out_ref[...] = acc

    return pl.pallas_call(
        body, name=name,
        in_specs=[pl.BlockSpec(memory_space=pltpu.VMEM)],
        out_specs=pl.BlockSpec(memory_space=pltpu.VMEM),
        out_shape=jax.ShapeDtypeStruct((rows, lanes), F32),
        scratch_shapes=[pltpu.VMEM((N_DEV, rows, lanes), F32),
                        pltpu.SemaphoreType.DMA((7,)), pltpu.SemaphoreType.DMA((7,))],
        compiler_params=pltpu.CompilerParams(vmem_limit_bytes=VMEM_LIMIT),
    )(p)


def _norm_matmul(x, g, w_t, *, tm, name, carry=None):
    n, d = x.shape
    c = w_t.shape[0]
    ch = 512

    def body(x_ref, g_ref, wt_ref, h_ref, z_ref):
        xv = x_ref[...]
        r = lax.rsqrt(jnp.mean(xv * xv, axis=-1, keepdims=True) + RMS_EPS)
        h = (xv * r * g_ref[...]).astype(BF16)
        h_ref[...] = h
        for c0 in range(0, c, ch):
            z_ref[:, c0:c0 + ch] = _dot_nt(h, wt_ref[c0:c0 + ch, :]).astype(BF16)

    return _call(
        body, name=name, grid=(n // tm,), carry=carry,
        in_specs=[pl.BlockSpec((tm, d), lambda i: (i, 0)),
                  pl.BlockSpec((1, d), lambda i: (0, 0)),
                  pl.BlockSpec((c, d), lambda i: (0, 0))],
        out_specs=[pl.BlockSpec((tm, d), lambda i: (i, 0)),
                   pl.BlockSpec((tm, c), lambda i: (i, 0))],
        out_shape=[jax.ShapeDtypeStruct((n, d), BF16), jax.ShapeDtypeStruct((n, c), BF16)],
        args=(x, g.reshape(1, d), w_t))


def _mix_forward(z, x, b_gate, ln_g, ln_b, w_s, b_s, w_sc, wb, w_out, *, tm, name, carry=None):
    n = z.shape[0]
    hb = tm // HALO

    def body(z_ref, zp_ref, x_ref, bg_ref, lng_ref, lnb_ref, ws_ref, bs_ref, wsc_ref, wb_ref, wo_ref,
             ya_ref, yb_ref, cv_ref, pa_ref, pb_ref, mg_ref, x1_ref, f_scr):
        i = pl.program_id(0)
        u = z_ref[:, OFF_U:OFF_U + D_A]
        v = z_ref[:, OFF_V:OFF_V + D_A].astype(F32)
        gu, _, _, _, _, _, f = _gmlp_forward(u, v, lng_ref[...], lnb_ref[...], ws_ref, bs_ref, f_scr)
        ya = gu * f
        ya_ref[...] = ya

        q = z_ref[:, OFF_CG:OFF_CG + D_B] * z_ref[:, OFF_HB:OFF_HB + D_B]
        qp = zp_ref[:, OFF_CG:OFF_CG + D_B] * zp_ref[:, OFF_HB:OFF_HB + D_B]
        qp = jnp.where(i > 0, qp, jnp.zeros_like(qp))
        w = wsc_ref[...].astype(BF16)
        conv = w[0:1] * _shift_down(q, 2, qp) + w[1:2] * _shift_down(q, 1, qp) + w[2:3] * q
        cv_ref[...] = conv
        yb = z_ref[:, OFF_BG:OFF_BG + D_B] * conv
        yb_ref[...] = yb

        pa = _dot(ya, wb_ref[0]).astype(BF16)
        pb = _dot(yb, wb_ref[1]).astype(BF16)
        pa_ref[...] = pa
        pb_ref[...] = pb
        bg = bg_ref[...].astype(BF16)
        sa = _sigmoid(z_ref[:, OFF_GA:OFF_GA + D_MODEL] + bg[:, 0:D_MODEL])
        sb = _sigmoid(z_ref[:, OFF_GB:OFF_GB + D_MODEL] + bg[:, D_MODEL:2 * D_MODEL])
        mg = sa * pa + sb * pb
        mg_ref[...] = mg
        x1_ref[...] = x_ref[...] + _dot(mg, wo_ref[...])

    row = lambda w: pl.BlockSpec((tm, w), lambda i: (i, 0))
    full = lambda *s: pl.BlockSpec(s, lambda i: (0,) * len(s))
    bf = lambda w: jax.ShapeDtypeStruct((n, w), BF16)
    return _call(
        body, name=name, grid=(n // tm,), carry=carry,
        in_specs=[row(D_IN),
                  pl.BlockSpec((HALO, D_IN), lambda i: (jnp.maximum(i * hb - 1, 0), 0)),
                  row(D_MODEL), full(1, 2 * D_MODEL), full(1, D_A), full(1, D_A),
                  full(N_HEADS, GMLP_BLOCK, GMLP_BLOCK), full(N_HEADS, GMLP_BLOCK, 1), full(3, D_B),
                  full(2, D_A, D_MODEL), full(D_MODEL, D_MODEL)],
        out_specs=[row(D_A), row(D_B), row(D_B), row(D_MODEL), row(D_MODEL), row(D_MODEL), row(D_MODEL)],
        out_shape=[bf(D_A), bf(D_B), bf(D_B), bf(D_MODEL), bf(D_MODEL), bf(D_MODEL),
                   jax.ShapeDtypeStruct((n, D_MODEL), F32)],
        scratch_shapes=[pltpu.VMEM((tm, D_A), BF16)],
        args=(z, z, x, b_gate.reshape(1, -1), ln_g.reshape(1, -1), ln_b.reshape(1, -1), w_s,
              b_s.reshape(N_HEADS, GMLP_BLOCK, 1), w_sc, wb, w_out))


def _ffn_forward(up, x1, w_fc, b_fc, w_down, *, tm, name, carry=None):
    n = up.shape[0]
    hb = tm // HALO

    def body(up_ref, upp_ref, x1_ref, wfc_ref, bfc_ref, wd_ref, gc_ref, a_ref, x2_ref, acc):
        i = pl.program_id(0)
        acc[...] = x1_ref[...]
        for c0 in range(0, D_FF, FFN_CHUNK):
            cols = slice(c0, c0 + FFN_CHUNK)
            gate = up_ref[:, cols]
            val = up_ref[:, D_FF + c0:D_FF + c0 + FFN_CHUNK]
            gp = upp_ref[:, cols]
            gp = jnp.where(i > 0, gp, jnp.zeros_like(gp))
            w = wfc_ref[:, cols].astype(BF16)
            gc = (w[0:1] * _shift_down(gate, 2, gp) + w[1:2] * _shift_down(gate, 1, gp) + w[2:3] * gate
                  + bfc_ref[:, cols].astype(BF16))
            gc_ref[:, cols] = gc
            a = gc * _sigmoid(gc) * val
            a_ref[:, cols] = a
            acc[...] += _dot(a, wd_ref[cols, :])
        x2_ref[...] = acc[...]

    return _call(
        body, name=name, grid=(n // tm,), carry=carry,
        in_specs=[pl.BlockSpec((tm, 2 * D_FF), lambda i: (i, 0)),
                  pl.BlockSpec((HALO, D_FF), lambda i: (jnp.maximum(i * hb - 1, 0), 0)),
                  pl.BlockSpec((tm, D_MODEL), lambda i: (i, 0)),
                  pl.BlockSpec((3, D_FF), lambda i: (0, 0)),
                  pl.BlockSpec((1, D_FF), lambda i: (0, 0)),
                  pl.BlockSpec((D_FF, D_MODEL), lambda i: (0, 0))],
        out_specs=[pl.BlockSpec((tm, D_FF), lambda i: (i, 0)),
                   pl.BlockSpec((tm, D_FF), lambda i: (i, 0)),
                   pl.BlockSpec((tm, D_MODEL), lambda i: (i, 0))],
        out_shape=[jax.ShapeDtypeStruct((n, D_FF), BF16), jax.ShapeDtypeStruct((n, D_FF), BF16),
                   jax.ShapeDtypeStruct((n, D_MODEL), F32)],
        scratch_shapes=[pltpu.VMEM((tm, D_MODEL), F32)],
        args=(up, up, x1, w_fc, b_fc.reshape(1, -1), w_down))


def _loss_head(x, g, target, *, tm, name):
    n, d = x.shape

    def body(x_ref, g_ref, t_ref, dx_ref, loss_ref, dg_ref):
        i = pl.program_id(0)

        @pl.when(i == 0)
        def _():
            loss_ref[...] = jnp.zeros_like(loss_ref)
            dg_ref[...] = jnp.zeros_like(dg_ref)

        xv = x_ref[...]
        r = lax.rsqrt(jnp.mean(xv * xv, axis=-1, keepdims=True) + RMS_EPS)
        xh = xv * r
        gv = g_ref[...]
        e = xh * gv - t_ref[...]
        per_row = jnp.sum(e * e, axis=-1, keepdims=True) * (0.5 / d)
        loss_ref[...] += jnp.sum(per_row, axis=0, keepdims=True)
        dy = e * (1.0 / d)
        dg_ref[0:1, :] += jnp.sum(dy * xh, axis=0, keepdims=True)
        dxh = dy * gv
        dx_ref[...] = r * (dxh - xh * jnp.mean(dxh * xh, axis=-1, keepdims=True))

    outs, _ = _call(
        body, name=name, grid=(n // tm,),
        in_specs=[pl.BlockSpec((tm, d), lambda i: (i, 0)),
                  pl.BlockSpec((1, d), lambda i: (0, 0)),
                  pl.BlockSpec((tm, d), lambda i: (i, 0))],
        out_specs=[pl.BlockSpec((tm, d), lambda i: (i, 0)),
                   pl.BlockSpec((1, 1), lambda i: (0, 0)),
                   pl.BlockSpec((8, d), lambda i: (0, 0))],
        out_shape=[jax.ShapeDtypeStruct((n, d), F32), jax.ShapeDtypeStruct((1, 1), F32),
                   jax.ShapeDtypeStruct((8, d), F32)],
        args=(x, g.reshape(1, d), target))
    return outs


def _ffn_backward(dx2, up, gc, w_fc, w_down, *, tm, name, carry=None):
    n = up.shape[0]
    steps = n // tm
    hb = tm // HALO

    def body(dx_ref, dxn_ref, up_ref, upn_ref, gc_ref, gcn_ref, wfc_ref, wd_ref, dup_ref, dwc_ref):
        i = pl.program_id(0)
        last = i == steps - 1

        @pl.when(i == 0)
        def _():
            dwc_ref[...] = jnp.zeros_like(dwc_ref)

        dxe = jnp.concatenate([dx_ref[...], dxn_ref[...]], axis=0).astype(BF16)
        for c0 in range(0, D_FF, FFN_CHUNK):
            cols = slice(c0, c0 + FFN_CHUNK)
            vcols = slice(D_FF + c0, D_FF + c0 + FFN_CHUNK)
            dae = _dot_nt(dxe, wd_ref[cols, :])
            da, dan = dae[:tm], dae[tm:]
            gate = up_ref[:, cols]
            val = up_ref[:, vcols]
            gcv = gc_ref[:, cols]
            s = _sigmoid(gcv)
            dab = da.astype(BF16)
            dup_ref[:, vcols] = dab * (gcv * s)
            dgc = dab * val * (s * (1.0 + gcv * (1.0 - s)))
            gcn = gcn_ref[:, cols]
            sn = _sigmoid(gcn)
            dgcn = dan.astype(BF16) * upn_ref[:, vcols] * (sn * (1.0 + gcn * (1.0 - sn)))
            dgcn = jnp.where(last, jnp.zeros_like(dgcn), dgcn)
            up1 = _shift_up(dgc, 1, dgcn)
            up2 = _shift_up(dgc, 2, dgcn)
            w = wfc_ref[:, cols].astype(BF16)
            dup_ref[:, cols] = w[2:3] * dgc + w[1:2] * up1 + w[0:1] * up2
            dwc_ref[0:1, cols] += _column_sums(gate * up2)
            dwc_ref[1:2, cols] += _column_sums(gate * up1)
            dwc_ref[2:3, cols] += _column_sums(gate * dgc)
            dwc_ref[3:4, cols] += _column_sums(dgc)

    nxt = lambda i: (jnp.minimum((i + 1) * hb, steps * hb - 1), 0)
    return _call(
        body, name=name, grid=(steps,), carry=carry,
        in_specs=[pl.BlockSpec((tm, D_MODEL), lambda i: (i, 0)),
                  pl.BlockSpec((HALO, D_MODEL), nxt),
                  pl.BlockSpec((tm, 2 * D_FF), lambda i: (i, 0)),
                  pl.BlockSpec((HALO, 2 * D_FF), nxt),
                  pl.BlockSpec((tm, D_FF), lambda i: (i, 0)),
                  pl.BlockSpec((HALO, D_FF), nxt),
                  pl.BlockSpec((3, D_FF), lambda i: (0, 0)),
                  pl.BlockSpec((D_FF, D_MODEL), lambda i: (0, 0))],
        out_specs=[pl.BlockSpec((tm, 2 * D_FF), lambda i: (i, 0)),
                   pl.BlockSpec((8, D_FF), lambda i: (0, 0))],
        out_shape=[jax.ShapeDtypeStruct((n, 2 * D_FF), BF16), jax.ShapeDtypeStruct((8, D_FF), F32)],
        args=(dx2, dx2, up, up, gc, gc, w_fc, w_down))


def _matmul_norm_backward(dz, w_t, x, g, dres, *, tm, name, carry=None):
    n, c = dz.shape
    d = x.shape[1]
    ch = 512

    def body(dz_ref, wt_ref, x_ref, g_ref, dres_ref, dx_ref, dg_ref):
        i = pl.program_id(0)

        @pl.when(i == 0)
        def _():
            dg_ref[...] = jnp.zeros_like(dg_ref)

        dh = _dot(dz_ref[:, 0:ch], wt_ref[0:ch, :])
        for c0 in range(ch, c, ch):
            dh += _dot(dz_ref[:, c0:c0 + ch], wt_ref[c0:c0 + ch, :])
        xv = x_ref[...]
        r = lax.rsqrt(jnp.mean(xv * xv, axis=-1, keepdims=True) + RMS_EPS)
        xh = xv * r
        dg_ref[0:1, :] += jnp.sum(dh * xh, axis=0, keepdims=True)
        dxh = dh * g_ref[...]
        dx_ref[...] = dres_ref[...] + r * (dxh - xh * jnp.mean(dxh * xh, axis=-1, keepdims=True))

    return _call(
        body, name=name, grid=(n // tm,), carry=carry,
        in_specs=[pl.BlockSpec((tm, c), lambda i: (i, 0)),
                  pl.BlockSpec((c, d), lambda i: (0, 0)),
                  pl.BlockSpec((tm, d), lambda i: (i, 0)),
                  pl.BlockSpec((1, d), lambda i: (0, 0)),
                  pl.BlockSpec((tm, d), lambda i: (i, 0))],
        out_specs=[pl.BlockSpec((tm, d), lambda i: (i, 0)),
                   pl.BlockSpec((8, d), lambda i: (0, 0))],
        out_shape=[jax.ShapeDtypeStruct((n, d), F32), jax.ShapeDtypeStruct((8, d), F32)],
        args=(dz, w_t, x, g.reshape(1, d), dres))


def _mix_backward(dx1, z, conv, pa, pb, b_gate, ln_g, ln_b, w_s, w_s_t, b_s, w_sc, w_out, wb, *, tm, name, carry=None):
    n = z.shape[0]
    steps = n // tm
    hb = tm // HALO

    def body(dx_ref, dxn_ref, z_ref, zn_ref, cv_ref, pa_ref, pb_ref, bg_ref, lng_ref, lnb_ref, ws_ref, wst_ref,
             bs_ref, wsc_ref, wo_ref, wb_ref,
             dz_ref, dpa_ref, dpb_ref, dbg_ref, dln_ref, dws_ref, dbs_ref, dwsc_ref, f_scr, dvn_scr):
        i = pl.program_id(0)
        last = i == steps - 1

        @pl.when(i == 0)
        def _():
            dbg_ref[...] = jnp.zeros_like(dbg_ref)
            dln_ref[...] = jnp.zeros_like(dln_ref)
            dws_ref[...] = jnp.zeros_like(dws_ref)
            dbs_ref[...] = jnp.zeros_like(dbs_ref)
            dwsc_ref[...] = jnp.zeros_like(dwsc_ref)

        dxe = jnp.concatenate([dx_ref[...], dxn_ref[...]], axis=0).astype(BF16)
        dmge = _dot_nt(dxe, wo_ref[...])
        dmg, dmgn = dmge[:tm].astype(BF16), dmge[tm:].astype(BF16)

        pa_v = pa_ref[...]
        pb_v = pb_ref[...]
        bg = bg_ref[...].astype(BF16)
        sa = _sigmoid(z_ref[:, OFF_GA:OFF_GA + D_MODEL] + bg[:, 0:D_MODEL])
        sb = _sigmoid(z_ref[:, OFF_GB:OFF_GB + D_MODEL] + bg[:, D_MODEL:2 * D_MODEL])
        dpa = dmg * sa
        dpb = dmg * sb
        dga = dmg * pa_v * sa * (1.0 - sa)
        dgb = dmg * pb_v * sb * (1.0 - sb)
        dpa_ref[...] = dpa
        dpb_ref[...] = dpb
        dz_ref[:, OFF_GA:OFF_GA + D_MODEL] = dga
        dz_ref[:, OFF_GB:OFF_GB + D_MODEL] = dgb
        dbg_ref[0:1, 0:D_MODEL] += _column_sums(dga)
        dbg_ref[0:1, D_MODEL:2 * D_MODEL] += _column_sums(dgb)

        dya = _dot_nt(dpa, wb_ref[0]).astype(BF16)
        u = z_ref[:, OFF_U:OFF_U + D_A]
        v = z_ref[:, OFF_V:OFF_V + D_A].astype(F32)
        ln_g = lng_ref[...]
        gu, tu, tv, xh, rstd, vn, f = _gmlp_forward(u, v, ln_g, lnb_ref[...], ws_ref, bs_ref, f_scr)
        dgu = dya * f
        df_bf = dya * gu
        dz_ref[:, OFF_U:OFF_U + D_A] = dgu * _gelu_grad(u, tu)
        mask = _spatial_mask(False)
        mask_t = _spatial_mask(True)
        wmt = [jnp.where(mask_t, wst_ref[h], 0.0).astype(BF16) for h in range(N_HEADS)]
        for b in range(tm // GMLP_BLOCK):
            rows = slice(b * GMLP_BLOCK, (b + 1) * GMLP_BLOCK)
            for h in range(N_HEADS):
                cols = slice(h * HEAD, (h + 1) * HEAD)
                dfb = df_bf[rows, cols]
                dvn_scr[rows, cols] = _dot(wmt[h], dfb)
                dws_ref[h] += jnp.where(mask, _dot_nt(dfb, vn[rows, cols]), 0.0)
                dbs_ref[h] += jnp.sum(dfb.astype(F32), axis=1, keepdims=True)
        dvn = dvn_scr[...]
        dln_ref[0:1, :] += jnp.sum(dvn * xh, axis=0, keepdims=True)
        dln_ref[1:2, :] += jnp.sum(dvn, axis=0, keepdims=True)
        dxh = dvn * ln_g
        dgv = rstd * (dxh - jnp.mean(dxh, axis=-1, keepdims=True) - xh * jnp.mean(dxh * xh, axis=-1, keepdims=True))
        dz_ref[:, OFF_V:OFF_V + D_A] = (dgv * _gelu_grad(v, tv)).astype(BF16)

        sbn = _sigmoid(zn_ref[:, OFF_GB:OFF_GB + D_MODEL] + bg[:, D_MODEL:2 * D_MODEL])
        dpbe = jnp.concatenate([dpb, dmgn * sbn], axis=0)
        dybe = _dot_nt(dpbe, wb_ref[1])
        dyb, dybn = dybe[:tm].astype(BF16), dybe[tm:].astype(BF16)
        bgv = z_ref[:, OFF_BG:OFF_BG + D_B]
        cg = z_ref[:, OFF_CG:OFF_CG + D_B]
        hbv = z_ref[:, OFF_HB:OFF_HB + D_B]
        q = cg * hbv
        dz_ref[:, OFF_BG:OFF_BG + D_B] = dyb * cv_ref[...]
        dconv = dyb * bgv
        dconvn = dybn * zn_ref[:, OFF_BG:OFF_BG + D_B]
        dconvn = jnp.where(last, jnp.zeros_like(dconvn), dconvn)
        up1 = _shift_up(dconv, 1, dconvn)
        up2 = _shift_up(dconv, 2, dconvn)
        dwsc_ref[0:1, :] += _column_sums(q * up2)
        dwsc_ref[1:2, :] += _column_sums(q * up1)
        dwsc_ref[2:3, :] += _column_sums(q * dconv)
        w = wsc_ref[...].astype(BF16)
        dq = w[2:3] * dconv + w[1:2] * up1 + w[0:1] * up2
        dz_ref[:, OFF_CG:OFF_CG + D_B] = dq * hbv
        dz_ref[:, OFF_HB:OFF_HB + D_B] = dq * cg

    row = lambda w: pl.BlockSpec((tm, w), lambda i: (i, 0))
    full = lambda *s: pl.BlockSpec(s, lambda i: (0,) * len(s))
    nxt = lambda i: (jnp.minimum((i + 1) * hb, steps * hb - 1), 0)
    return _call(
        body, name=name, grid=(steps,), carry=carry,
        in_specs=[row(D_MODEL), pl.BlockSpec((HALO, D_MODEL), nxt),
                  row(D_IN), pl.BlockSpec((HALO, D_IN), nxt),
                  row(D_B), row(D_MODEL), row(D_MODEL),
                  full(1, 2 * D_MODEL), full(1, D_A), full(1, D_A),
                  full(N_HEADS, GMLP_BLOCK, GMLP_BLOCK), full(N_HEADS, GMLP_BLOCK, GMLP_BLOCK),
                  full(N_HEADS, GMLP_BLOCK, 1), full(3, D_B),
                  full(D_MODEL, D_MODEL), full(2, D_A, D_MODEL)],
        out_specs=[row(D_IN), row(D_MODEL), row(D_MODEL),
                   full(8, 2 * D_MODEL), full(8, D_A), full(N_HEADS, GMLP_BLOCK, GMLP_BLOCK),
                   full(N_HEADS, GMLP_BLOCK, 1), full(8, D_B)],
        out_shape=[jax.ShapeDtypeStruct((n, D_IN), BF16), jax.ShapeDtypeStruct((n, D_MODEL), BF16),
                   jax.ShapeDtypeStruct((n, D_MODEL), BF16),
                   jax.ShapeDtypeStruct((8, 2 * D_MODEL), F32), jax.ShapeDtypeStruct((8, D_A), F32),
                   jax.ShapeDtypeStruct((N_HEADS, GMLP_BLOCK, GMLP_BLOCK), F32),
                   jax.ShapeDtypeStruct((N_HEADS, GMLP_BLOCK, 1), F32), jax.ShapeDtypeStruct((8, D_B), F32)],
        scratch_shapes=[pltpu.VMEM((tm, D_A), BF16), pltpu.VMEM((tm, D_A), F32)],
        args=(dx1, dx1, z, z, conv, pa, pb, b_gate.reshape(1, -1), ln_g.reshape(1, -1), ln_b.reshape(1, -1), w_s, w_s_t,
              b_s.reshape(N_HEADS, GMLP_BLOCK, 1), w_sc, w_out, wb))


def _matmul_tn(a, b, *, t1, tn, name, carry=None):
    n, k1 = a.shape
    k2 = b.shape[1]
    steps = n // tn

    def body(a_ref, b_ref, o_ref, acc):
        s = pl.program_id(1)

        @pl.when(s == 0)
        def _():
            acc[...] = jnp.zeros_like(acc)

        acc[...] += lax.dot_general(a_ref[...].astype(BF16), b_ref[...].astype(BF16), TN, preferred_element_type=F32)

        @pl.when(s == steps - 1)
        def _():
            o_ref[...] = acc[...].astype(BF16)

    outs, carried = _call(
        body, name=name, grid=(k1 // t1, steps), carry=carry,
        in_specs=[pl.BlockSpec((tn, t1), lambda i, s: (s, i)),
                  pl.BlockSpec((tn, k2), lambda i, s: (s, 0))],
        out_specs=[pl.BlockSpec((t1, k2), lambda i, s: (i, 0))],
        out_shape=[jax.ShapeDtypeStruct((k1, k2), BF16)],
        scratch_shapes=[pltpu.VMEM((t1, k2), F32)],
        args=(a, b))
    return outs[0], carried


def _adamw_math(w, g, m, v):
    m = ADAM_B1 * m + (1.0 - ADAM_B1) * g
    v = ADAM_B2 * v + (1.0 - ADAM_B2) * (g * g)
    m_hat = m / (1.0 - ADAM_B1 ** ADAM_STEP)
    v_hat = v / (1.0 - ADAM_B2 ** ADAM_STEP)
    delta = -ADAM_LR * (m_hat / (jnp.sqrt(v_hat) + ADAM_EPS) + ADAM_WD * w)
    return delta, m, v


def _sum_parts(recvs, *, tr, name):
    _, r, c = recvs[0].shape

    def body(*refs):
        recv_refs, g_ref = refs[:DEPTH], refs[DEPTH]
        layer = pl.program_id(0)
        for l in range(DEPTH):
            @pl.when(layer == l)
            def _(l=l):
                g = recv_refs[l][0].astype(F32)
                for s in range(1, N_DEV):
                    g = g + recv_refs[l][s].astype(F32)
                g_ref[0] = g

    outs, _ = _call(
        body, name=name, grid=(DEPTH, r // tr),
        in_specs=[pl.BlockSpec((N_DEV, tr, c), lambda l, i: (0, i, 0))] * DEPTH,
        out_specs=[pl.BlockSpec((1, tr, c), lambda l, i: (l, i, 0))],
        out_shape=[jax.ShapeDtypeStruct((DEPTH, r, c), F32)],
        args=tuple(recvs))
    return outs[0]


def _adamw(w, g, m, v, *, tr, name):
    r, c = w.shape

    def body(w_ref, g_ref, m_ref, v_ref, d_ref, nm_ref, nv_ref):
        delta, nm, nv = _adamw_math(w_ref[...], g_ref[...], m_ref[...], v_ref[...])
        d_ref[...] = delta
        nm_ref[...] = nm
        nv_ref[...] = nv

    spec = pl.BlockSpec((tr, c), lambda i: (i, 0))
    outs, _ = _call(body, name=name, grid=(r // tr,), in_specs=[spec] * 4, out_specs=[spec] * 3,
                    out_shape=[jax.ShapeDtypeStruct((r, c), F32)] * 3, args=(w, g, m, v))
    return outs


def _sum_adamw(recvs, w, m, v, *, tr, name):
    _, r, c = recvs[0].shape

    def body(*refs):
        recv_refs = refs[:DEPTH]
        w_ref, m_ref, v_ref, g_ref, d_ref, nm_ref, nv_ref = refs[DEPTH:]
        layer = pl.program_id(0)
        for l in range(DEPTH):
            @pl.when(layer == l)
            def _(l=l):
                g = recv_refs[l][0].astype(F32)
                for s in range(1, N_DEV):
                    g = g + recv_refs[l][s].astype(F32)
                delta, nm, nv = _adamw_math(w_ref[0], g, m_ref[0], v_ref[0])
                g_ref[0] = g
                d_ref[0] = delta
                nm_ref[0] = nm
                nv_ref[0] = nv

    spec = pl.BlockSpec((1, tr, c), lambda l, i: (l, i, 0))
    outs, _ = _call(
        body, name=name, grid=(DEPTH, r // tr),
        in_specs=[pl.BlockSpec((N_DEV, tr, c), lambda l, i: (0, i, 0))] * DEPTH + [spec] * 3,
        out_specs=[spec] * 4, out_shape=[jax.ShapeDtypeStruct((DEPTH, r, c), F32)] * 4,
        args=tuple(recvs) + (w, m, v))
    return outs


def _adamw_small(w, g, m, v, *, name):
    def body(w_ref, g_ref, m_ref, v_ref, d_ref, nm_ref, nv_ref):
        delta, nm, nv = _adamw_math(w_ref[...], g_ref[...], m_ref[...], v_ref[...])
        d_ref[...] = delta
        nm_ref[...] = nm
        nv_ref[...] = nv

    vmem = pl.BlockSpec(memory_space=pltpu.VMEM)
    return pl.pallas_call(
        body, name=name, in_specs=[vmem] * 4, out_specs=[vmem] * 3,
        out_shape=[jax.ShapeDtypeStruct(w.shape, F32)] * 3,
    )(w, g, m, v)


def _pack(arrs):
    flat = jnp.concatenate([a.reshape(-1) for a in arrs])
    pad = (-flat.shape[0]) % 1024
    return jnp.pad(flat, (0, pad)).reshape(-1, 128)


def _unpack(packed, shapes):
    flat = packed.reshape(-1)
    out, o = [], 0
    for s in shapes:
        size = math.prod(s)
        out.append(flat[o:o + size].reshape(s))
        o += size
    return out


def _rows(gathered):
    return gathered.reshape(N_DEV * gathered.shape[1], gathered.shape[2])


def _parts(full):
    return full.reshape(N_DEV, full.shape[0] // N_DEV, full.shape[1])


def kernel(x, norm1_g, w_in, b_gate, gmlp_ln_g, gmlp_ln_b, w_spatial, b_spatial, w_shortconv, w_branch, w_out, norm2_g, w_ffn_up, w_ffn_conv, b_ffn_conv, w_ffn_down, final_g, loss_target, m_norm1_g, m_w_in, m_b_gate, m_gmlp_ln_g, m_gmlp_ln_b, m_w_spatial, m_b_spatial, m_w_shortconv, m_w_branch, m_w_out, m_norm2_g, m_w_ffn_up, m_w_ffn_conv, m_b_ffn_conv, m_w_ffn_down, m_final_g, v_norm1_g, v_w_in, v_b_gate, v_gmlp_ln_g, v_gmlp_ln_b, v_w_spatial, v_b_spatial, v_w_shortconv, v_w_branch, v_w_out, v_norm2_g, v_w_ffn_up, v_w_ffn_conv, v_b_ffn_conv, v_w_ffn_down, v_final_g):
    n = x.shape[1]
    tm, tm_mix, tn = 512, 512, 1024
    x0 = x.reshape(n, D_MODEL)
    target = loss_target.reshape(n, D_MODEL)
    my_idx = 4 * lax.axis_index("x") + 2 * lax.axis_index("y") + lax.axis_index("c")
    sc_w, fc_w = D_B // N_DEV, D_FF // N_DEV

    sh_in = [w_in[l].T.astype(BF16) for l in range(DEPTH)]
    sh_up = [w_ffn_up[l].T.astype(BF16) for l in range(DEPTH)]
    sh_br = [w_branch[l].astype(BF16) for l in range(DEPTH)]
    sh_out = [w_out[l].astype(BF16) for l in range(DEPTH)]
    sh_down = [w_ffn_down[l].astype(BF16) for l in range(DEPTH)]
    taps = jnp.concatenate([w_shortconv, w_ffn_conv], axis=-1)

    def branch_weights(g):
        return g.transpose(1, 2, 0, 3).reshape(2, D_A, D_MODEL)

    g_in0, g_taps = _gather_now([sh_in[0], taps], name="gather_first")
    w_sc = [g_taps[:, l, :, :sc_w].transpose(1, 0, 2).reshape(3, D_B) for l in range(DEPTH)]
    w_fc = [g_taps[:, l, :, sc_w:].transpose(1, 0, 2).reshape(3, D_FF) for l in range(DEPTH)]
    w_s_t = [w_spatial[l].transpose(0, 2, 1) for l in range(DEPTH)]
    weights = [dict(), dict()]
    weights[0]["in_t"] = _rows(g_in0)
    saved = []
    xc = x0
    for l in range(DEPTH):
        p = weights[l]
        carry = _Gather([sh_br[0], sh_out[0]] if l == 0 else [sh_down[1]])
        (h, z), got = _norm_matmul(xc, norm1_g[l], p["in_t"], tm=tm, name=f"fwd_in_{l}", carry=carry)
        if l == 0:
            p["wb"], p["out"] = branch_weights(got[0]), _rows(got[1])
        else:
            p["down"] = _rows(got[0])
        carry = _Gather([sh_up[0]]) if l == 0 else None
        (ya, yb, conv, pa, pb, mg, x1), got = _mix_forward(
            z, xc, b_gate[l], gmlp_ln_g[l], gmlp_ln_b[l], w_spatial[l], b_spatial[l], w_sc[l], p["wb"], p["out"],
            tm=tm_mix, name=f"fwd_mix_{l}", carry=carry)
        if l == 0:
            p["up_t"] = _rows(got[0])
        carry = _Gather([sh_down[0], sh_in[1]]) if l == 0 else None
        (h2, up), got = _norm_matmul(x1, norm2_g[l], p["up_t"], tm=tm, name=f"fwd_up_{l}", carry=carry)
        if l == 0:
            p["down"], weights[1]["in_t"] = _rows(got[0]), _rows(got[1])
        carry = _Gather([sh_br[1], sh_out[1], sh_up[1]]) if l == 0 else None
        (gc, a, x2), got = _ffn_forward(up, x1, w_fc[l], b_ffn_conv[l], p["down"], tm=tm, name=f"fwd_ffn_{l}", carry=carry)
        if l == 0:
            weights[1]["wb"], weights[1]["out"], weights[1]["up_t"] = branch_weights(got[0]), _rows(got[1]), _rows(got[2])
        saved.append(dict(x=xc, h=h, z=z, ya=ya, yb=yb, conv=conv, pa=pa, pb=pb, mg=mg, x1=x1, h2=h2, up=up, gc=gc, a=a))
        xc = x2
    dx, loss_part, dgf = _loss_head(xc, final_g, target, tm=tm, name="loss_head")

    recv = [dict(), dict()]
    small = [None] * DEPTH
    pending_in = None
    for l in reversed(range(DEPTH)):
        p, s = weights[l], saved[l]
        carry = _Exchange([pending_in]) if pending_in is not None else None
        (dup, dwc), got = _ffn_backward(dx, s["up"], s["gc"], w_fc[l], p["down"], tm=tm, name=f"bwd_ffn_{l}", carry=carry)
        if got is not None:
            recv[l + 1]["in_t"] = got[0]
        dw_down, _ = _matmul_tn(s["a"], dx, t1=D_FF // 2, tn=tn, name=f"dw_down_{l}")
        dw_up_t, got = _matmul_tn(dup, s["h2"], t1=2 * D_FF // 4, tn=tn, name=f"dw_up_{l}", carry=_Exchange([_parts(dw_down)]))
        recv[l]["down"] = got[0]
        (dx1, dg2), got = _matmul_norm_backward(dup, p["up_t"], s["x1"], norm2_g[l], dx, tm=tm, name=f"bwd_up_{l}",
                                                carry=_Exchange([_parts(dw_up_t)]))
        recv[l]["up_t"] = got[0]
        (dz, dpa, dpb, dbg, dln, dws, dbs, dwsc), _ = _mix_backward(
            dx1, s["z"], s["conv"], s["pa"], s["pb"], b_gate[l], gmlp_ln_g[l], gmlp_ln_b[l], w_spatial[l], w_s_t[l],
            b_spatial[l], w_sc[l], p["out"], p["wb"], tm=tm_mix, name=f"bwd_mix_{l}")
        dw_out, _ = _matmul_tn(s["mg"], dx1, t1=D_MODEL, tn=tn, name=f"dw_out_{l}")
        dw_bra_t, _ = _matmul_tn(dpa, s["ya"], t1=D_MODEL, tn=tn, name=f"dw_branch_a_{l}")
        dw_brb_t, _ = _matmul_tn(dpb, s["yb"], t1=D_MODEL, tn=tn, name=f"dw_branch_b_{l}")
        dw_in_t, got = _matmul_tn(dz, s["h"], t1=D_IN // 4, tn=tn, name=f"dw_in_{l}",
                                  carry=_Exchange([_parts(dw_out), _parts(dw_bra_t), _parts(dw_brb_t)]))
        recv[l]["out"], recv[l]["bra_t"], recv[l]["brb_t"] = got
        carry = _Exchange([_parts(dw_in_t)]) if l == 0 else None
        (dx0, dg1), got = _matmul_norm_backward(dz, p["in_t"], s["x"], norm1_g[l], dx1, tm=tm, name=f"bwd_in_{l}", carry=carry)
        if l == 0:
            recv[0]["in_t"] = got[0]
        else:
            pending_in = _parts(dw_in_t)
        small[l] = dict(norm1_g=dg1[0], b_gate=dbg[0], gmlp_ln_g=dln[0], gmlp_ln_b=dln[1], w_spatial=dws,
                        b_spatial=dbs.reshape(N_HEADS, GMLP_BLOCK), w_shortconv=dwsc[0:3], norm2_g=dg2[0],
                        w_ffn_conv=dwc[0:3], b_ffn_conv=dwc[3])
        dx = dx0
    grad_x = dx.reshape(x.shape)

    results = {}
    both = lambda key: [recv[l][key] for l in range(DEPTH)]
    swap = lambda t: t.transpose(0, 2, 1)
    for key, slab, (w, m, v), tr in [("w_in", "in_t", (w_in, m_w_in, v_w_in), 192),
                                     ("w_ffn_up", "up_t", (w_ffn_up, m_w_ffn_up, v_w_ffn_up), 176)]:
        outs = _sum_adamw(both(slab), swap(w), swap(m), swap(v), tr=tr, name=f"adamw_{key}")
        results[key] = tuple(swap(o) for o in outs)
    g_bra = _sum_parts(both("bra_t"), tr=128, name="sum_w_branch_a").transpose(0, 2, 1)
    g_brb = _sum_parts(both("brb_t"), tr=128, name="sum_w_branch_b").transpose(0, 2, 1)
    g_br = jnp.stack([g_bra, g_brb], axis=1)
    flat = lambda t: t.reshape(-1, t.shape[-1])
    outs = _adamw(flat(w_branch), flat(g_br), flat(m_w_branch), flat(v_w_branch), tr=512, name="adamw_w_branch")
    results["w_branch"] = (g_br,) + tuple(o.reshape(w_branch.shape) for o in outs)
    results["w_out"] = tuple(_sum_adamw(both("out"), w_out, m_w_out, v_w_out, tr=128, name="adamw_w_out"))
    results["w_ffn_down"] = tuple(_sum_adamw(both("down"), w_ffn_down, m_w_ffn_down, v_w_ffn_down, tr=176,
                                             name="adamw_w_ffn_down"))

    small_names = ["norm1_g", "b_gate", "gmlp_ln_g", "gmlp_ln_b", "w_spatial", "b_spatial", "w_shortconv", "norm2_g",
                   "w_ffn_conv", "b_ffn_conv"]
    stacked = [jnp.stack([small[l][k] for l in range(DEPTH)]) for k in small_names] + [dgf[0], loss_part.reshape(1)]
    shapes = [a.shape for a in stacked]
    reduced = _unpack(_all_reduce_small(_pack(stacked), name="all_reduce_small_grads"), shapes)
    loss = reduced.pop()[0]
    g_small = dict(zip(small_names + ["final_g"], reduced))
    g_small["w_shortconv"] = lax.dynamic_slice_in_dim(g_small["w_shortconv"], my_idx * sc_w, sc_w, axis=2)
    g_small["w_ffn_conv"] = lax.dynamic_slice_in_dim(g_small["w_ffn_conv"], my_idx * fc_w, fc_w, axis=2)
    small_w = dict(norm1_g=(norm1_g, m_norm1_g, v_norm1_g), b_gate=(b_gate, m_b_gate, v_b_gate),
                   gmlp_ln_g=(gmlp_ln_g, m_gmlp_ln_g, v_gmlp_ln_g), gmlp_ln_b=(gmlp_ln_b, m_gmlp_ln_b, v_gmlp_ln_b),
                   w_spatial=(w_spatial, m_w_spatial, v_w_spatial), b_spatial=(b_spatial, m_b_spatial, v_b_spatial),
                   w_shortconv=(w_shortconv, m_w_shortconv, v_w_shortconv), norm2_g=(norm2_g, m_norm2_g, v_norm2_g),
                   w_ffn_conv=(w_ffn_conv, m_w_ffn_conv, v_w_ffn_conv), b_ffn_conv=(b_ffn_conv, m_b_ffn_conv, v_b_ffn_conv),
                   final_g=(final_g, m_final_g, v_final_g))
    order = small_names + ["final_g"]
    local_shapes = [small_w[k][0].shape for k in order]
    packed = [_pack([small_w[k][j] for k in order]) for j in range(3)]
    d_s, m_s, v_s = _adamw_small(packed[0], _pack([g_small[k] for k in order]), packed[1], packed[2], name="adamw_small")
    d_s, m_s, v_s = _unpack(d_s, local_shapes), _unpack(m_s, local_shapes), _unpack(v_s, local_shapes)
    for j, k in enumerate(order):
        results[k] = (g_small[k], d_s[j], m_s[j], v_s[j])

    names = ["norm1_g", "w_in", "b_gate", "gmlp_ln_g", "gmlp_ln_b", "w_spatial", "b_spatial", "w_shortconv", "w_branch",
             "w_out", "norm2_g", "w_ffn_up", "w_ffn_conv", "b_ffn_conv", "w_ffn_down", "final_g"]
    return (loss, grad_x, *[results[k][0] for k in names], *[results[k][1] for k in names],
            *[results[k][2] for k in names], *[results[k][3] for k in names])
```

```python
import math

import jax
import jax.numpy as jnp
from jax import lax
from jax.experimental import pallas as pl
from jax.experimental.pallas import tpu as pltpu

F32 = jnp.float32
BF16 = jnp.bfloat16

N_DEV = 8
DEPTH = 2
D_MODEL = 1024
D_A = 512
D_B = 512
D_FF = 2816
D_IN = 4608
N_HEADS = 4
HEAD = 128
GMLP_BLOCK = 128
CAUSAL_CHUNK = 64
OFF_U, OFF_V, OFF_BG, OFF_CG, OFF_HB, OFF_GA, OFF_GB = 0, 512, 1024, 1536, 2048, 2560, 3584
RMS_EPS = 1e-6
LN_EPS = 1e-5
ADAM_LR, ADAM_B1, ADAM_B2, ADAM_EPS, ADAM_WD, ADAM_STEP = 0.001, 0.9, 0.999, 1e-08, 0.01, 10

SUBLANES = 8
HALO = 16
FFN_CHUNK = 256
V7X_VMEM_BYTES = 64 << 20
VMEM_LIMIT = V7X_VMEM_BYTES - (8 << 20)
MESH = pl.DeviceIdType.MESH
GELU_C0 = 0.7978845608028654
GELU_C1 = 0.044715
NT = (((1,), (1,)), ((), ()))
TN = (((0,), (0,)), ((), ()))


def _dot(a, b):
    return jnp.dot(a, b, preferred_element_type=F32)


def _dot_nt(a, b):
    return lax.dot_general(a, b, NT, preferred_element_type=F32)


def _sigmoid(x):
    return 1.0 / (1.0 + jnp.exp(-x))


def _gelu_tanh(x):
    return jnp.tanh(GELU_C0 * (x + GELU_C1 * x * x * x))


def _gelu_grad(x, t):
    return 0.5 * (1.0 + t) + 0.5 * x * (1.0 - t * t) * GELU_C0 * (1.0 + 3.0 * GELU_C1 * x * x)


def _sublane_tile(dtype):
    return SUBLANES * (4 // jnp.dtype(dtype).itemsize)


def _shift_down(a, k, prev):
    p = prev.shape[0]
    r = pltpu.roll(a, k, 0)
    sub = _sublane_tile(a.dtype)
    head = r[0:sub]
    rid = lax.broadcasted_iota(jnp.int32, head.shape, 0)
    for j in range(k):
        head = jnp.where(rid == j, prev[p - k + j:p - k + j + 1, :], head)
    return jnp.concatenate([head, r[sub:]], axis=0)


def _shift_up(a, k, nxt):
    t = a.shape[0]
    r = pltpu.roll(a, t - k, 0)
    sub = _sublane_tile(a.dtype)
    tail = r[t - sub:t]
    rid = lax.broadcasted_iota(jnp.int32, tail.shape, 0)
    for j in range(k):
        tail = jnp.where(rid == sub - k + j, nxt[j:j + 1, :], tail)
    return jnp.concatenate([r[0:t - sub], tail], axis=0)


def _column_sums(p):
    if p.dtype.itemsize < 4:
        t = p.shape[0]
        p = p[:t // 2] + p[t // 2:]
        p = p[:t // 4] + p[t // 4:]
    return jnp.sum(p.astype(F32), axis=0, keepdims=True)


def _spatial_mask(transposed):
    ri = lax.broadcasted_iota(jnp.int32, (GMLP_BLOCK, GMLP_BLOCK), 0) // CAUSAL_CHUNK
    ci = lax.broadcasted_iota(jnp.int32, (GMLP_BLOCK, GMLP_BLOCK), 1) // CAUSAL_CHUNK
    return (ri <= ci) if transposed else (ci <= ri)


def _gmlp_forward(u, v, ln_g, ln_b, ws_ref, bs_ref, f_scr):
    tm = u.shape[0]
    tu = _gelu_tanh(u)
    tv = _gelu_tanh(v)
    gu = 0.5 * u * (1.0 + tu)
    gv = 0.5 * v * (1.0 + tv)
    mu = jnp.mean(gv, axis=-1, keepdims=True)
    cen = gv - mu
    rstd = lax.rsqrt(jnp.mean(cen * cen, axis=-1, keepdims=True) + LN_EPS)
    xh = cen * rstd
    vn = (xh * ln_g + ln_b).astype(BF16)
    mask = _spatial_mask(False)
    wm = [jnp.where(mask, ws_ref[h], 0.0).astype(BF16) for h in range(N_HEADS)]
    for b in range(tm // GMLP_BLOCK):
        rows = slice(b * GMLP_BLOCK, (b + 1) * GMLP_BLOCK)
        for h in range(N_HEADS):
            cols = slice(h * HEAD, (h + 1) * HEAD)
            f_scr[rows, cols] = (_dot(wm[h], vn[rows, cols]) + bs_ref[h]).astype(f_scr.dtype)
    return gu, tu, tv, xh, rstd, vn, f_scr[...]


def _position():
    return lax.axis_index("x"), lax.axis_index("y"), lax.axis_index("c")


class _Gather:
    def __init__(self, arrays):
        self.arrays = list(arrays)
        self.out_shape = [jax.ShapeDtypeStruct((N_DEV,) + a.shape, a.dtype) for a in self.arrays]

    def _plan(self, ins, outs, sems):
        send_sems, recv_sems, local_sems = sems
        x, y, c = _position()
        me, sibling = (x, y, c), (x, y, 1 - c)
        chips = [(1 - x, y), (x, 1 - y), (1 - x, 1 - y)]

        def slot(a, p):
            return outs[a].at[4 * p[0] + 2 * p[1] + p[2]]

        def copy(a, k, block, to, src=None):
            return pltpu.make_async_remote_copy(
                src_ref=slot(a, block) if src is None else src, dst_ref=slot(a, block),
                send_sem=send_sems.at[a, k], recv_sem=recv_sems.at[a, k], device_id=to, device_id_type=MESH)

        n = len(self.arrays)

        def mine():
            return [pltpu.make_async_copy(ins[a], slot(a, me), local_sems.at[a]) for a in range(n)]

        def first():
            out = []
            for a in range(n):
                out.append(copy(a, 0, me, sibling, src=ins[a]))
                out += [copy(a, 1 + j, me, (*chip, c), src=ins[a]) for j, chip in enumerate(chips)]
            return out

        def arrivals():
            return [copy(a, 1 + j, (*chip, c), me) for j, chip in enumerate(chips) for a in range(n)]

        def relays():
            return [copy(a, 4 + j, (*chip, c), sibling) for j, chip in enumerate(chips) for a in range(n)]

        def from_sibling():
            out = [copy(a, 0, sibling, me) for a in range(n)]
            return out + [copy(a, 4 + j, (*chip, 1 - c), me) for j, chip in enumerate(chips) for a in range(n)]

        return mine, first, arrivals, relays, from_sibling

    def start(self, ins, outs, sems):
        mine, first, _, _, _ = self._plan(ins, outs, sems)
        for cp in mine() + first():
            cp.start()

    def relay(self, ins, outs, sems):
        _, _, arrivals, relays, _ = self._plan(ins, outs, sems)
        for arrived, onward in zip(arrivals(), relays()):
            arrived.wait_recv()
            onward.start()

    def finish(self, ins, outs, sems):
        mine, first, _, relays, from_sibling = self._plan(ins, outs, sems)
        for cp in from_sibling():
            cp.wait_recv()
        for cp in first() + relays():
            cp.wait_send()
        for cp in mine():
            cp.wait()


class _Exchange:
    def __init__(self, arrays):
        self.arrays = list(arrays)
        self.out_shape = [jax.ShapeDtypeStruct(a.shape, a.dtype) for a in self.arrays]

    def _plan(self, ins, outs, sems):
        send_sems, recv_sems, local_sems = sems
        x, y, c = _position()
        my_idx = 4 * x + 2 * y + c
        n = len(self.arrays)
        offsets = [(dx, dy, dc) for dx in (0, 1) for dy in (0, 1) for dc in (0, 1) if (dx, dy, dc) != (0, 0, 0)]

        def mine():
            return [pltpu.make_async_copy(ins[a].at[my_idx], outs[a].at[my_idx], local_sems.at[a]) for a in range(n)]

        def remote(arriving):
            out = []
            for k, (dx, dy, dc) in enumerate(offsets):
                px, py, pc = x ^ dx, y ^ dy, c ^ dc
                p_idx = 4 * px + 2 * py + pc
                for a in range(n):
                    out.append(pltpu.make_async_remote_copy(
                        src_ref=ins[a].at[p_idx], dst_ref=outs[a].at[p_idx if arriving else my_idx],
                        send_sem=send_sems.at[a, k], recv_sem=recv_sems.at[a, k],
                        device_id=(px, py, pc), device_id_type=MESH))
            return out

        return mine, remote

    def start(self, ins, outs, sems):
        mine, remote = self._plan(ins, outs, sems)
        for cp in mine() + remote(False):
            cp.start()

    def relay(self, ins, outs, sems):
        pass

    def finish(self, ins, outs, sems):
        mine, remote = self._plan(ins, outs, sems)
        for cp in remote(True):
            cp.wait_recv()
        for cp in remote(False):
            cp.wait_send()
        for cp in mine():
            cp.wait()


def _call(body, *, name, grid, in_specs, out_specs, out_shape, args, scratch_shapes=(), carry=None):
    n_in, n_out, n_scr = len(in_specs), len(out_specs), len(scratch_shapes)
    params = pltpu.CompilerParams(dimension_semantics=("arbitrary",) * len(grid), vmem_limit_bytes=VMEM_LIMIT)
    if carry is None:
        outs = pl.pallas_call(body, name=name, grid=grid, in_specs=in_specs, out_specs=out_specs, out_shape=out_shape,
                              scratch_shapes=list(scratch_shapes), compiler_params=params)(*args)
        return outs, None
    m = len(carry.arrays)
    total = math.prod(grid)

    def wrapped(*refs):
        ins, refs = refs[:n_in], refs[n_in:]
        c_ins, refs = refs[:m], refs[m:]
        outs, refs = refs[:n_out], refs[n_out:]
        c_outs, refs = refs[:m], refs[m:]
        scr, sems = refs[:n_scr], refs[n_scr:]
        flat = pl.program_id(0)
        for d in range(1, len(grid)):
            flat = flat * grid[d] + pl.program_id(d)

        @pl.when(flat == 0)
        def _():
            carry.start(c_ins, c_outs, sems)

        body(*ins, *outs, *scr)

        @pl.when(flat == total - 2)
        def _():
            carry.relay(c_ins, c_outs, sems)

        @pl.when(flat == total - 1)
        def _():
            carry.finish(c_ins, c_outs, sems)

    any_spec = pl.BlockSpec(memory_space=pl.ANY)
    sem_shapes = [pltpu.SemaphoreType.DMA((m, 7)), pltpu.SemaphoreType.DMA((m, 7)), pltpu.SemaphoreType.DMA((m,))]
    outs = pl.pallas_call(
        wrapped, name=name, grid=grid,
        in_specs=list(in_specs) + [any_spec] * m, out_specs=list(out_specs) + [any_spec] * m,
        out_shape=list(out_shape) + carry.out_shape,
        scratch_shapes=list(scratch_shapes) + sem_shapes, compiler_params=params)(*args, *carry.arrays)
    return outs[:n_out], outs[n_out:]


def _gather_now(arrays, *, name):
    carry = _Gather(arrays)
    m = len(arrays)

    def body(*refs):
        ins, outs, sems = refs[:m], refs[m:2 * m], refs[2 * m:]
        carry.start(ins, outs, sems)
        carry.relay(ins, outs, sems)
        carry.finish(ins, outs, sems)

    any_spec = pl.BlockSpec(memory_space=pl.ANY)
    return pl.pallas_call(
        body, name=name, in_specs=[any_spec] * m, out_specs=[any_spec] * m, out_shape=carry.out_shape,
        scratch_shapes=[pltpu.SemaphoreType.DMA((m, 7)), pltpu.SemaphoreType.DMA((m, 7)),
                        pltpu.SemaphoreType.DMA((m,))],
    )(*arrays)


def _all_reduce_small(p, *, name):
    rows, lanes = p.shape

    def body(p_ref, out_ref, buf, send_sems, recv_sems):
        x, y, c = _position()
        me, sibling = (x, y, c), (x, y, 1 - c)
        chips = [(1 - x, y), (x, 1 - y), (1 - x, 1 - y)]

        def slot(q):
            return buf.at[4 * q[0] + 2 * q[1] + q[2]]

        def copy(k, block, to, src=None):
            return pltpu.make_async_remote_copy(
                src_ref=slot(block) if src is None else src, dst_ref=slot(block),
                send_sem=send_sems.at[k], recv_sem=recv_sems.at[k], device_id=to, device_id_type=MESH)

        first = [copy(0, me, sibling, src=p_ref)]
        first += [copy(1 + j, me, (*chip, c), src=p_ref) for j, chip in enumerate(chips)]
        for cp in first:
            cp.start()
        passed = []
        for j, chip in enumerate(chips):
            copy(1 + j, (*chip, c), me).wait_recv()
            cp = copy(4 + j, (*chip, c), sibling)
            cp.start()
            passed.append(cp)
        copy(0, sibling, me).wait_recv()
        for j, chip in enumerate(chips):
            copy(4 + j, (*chip, 1 - c), me).wait_recv()
        for cp in first + passed:
            cp.wait_send()
        my_idx = 4 * x + 2 * y + c
        acc = jnp.zeros((rows, lanes), F32)
        for s in range(N_DEV):
            acc = acc + jnp.where(my_idx == s, p_ref[...], buf[s])
        out_ref[...] = acc

    return pl.pallas_call(
        body, name=name,
        in_specs=[pl.BlockSpec(memory_space=pltpu.VMEM)],
        out_specs=pl.BlockSpec(memory_space=pltpu.VMEM),
        out_shape=jax.ShapeDtypeStruct((rows, lanes), F32),
        scratch_shapes=[pltpu.VMEM((N_DEV, rows, lanes), F32),
                        pltpu.SemaphoreType.DMA((7,)), pltpu.SemaphoreType.DMA((7,))],
        compiler_params=pltpu.CompilerParams(vmem_limit_bytes=VMEM_LIMIT),
    )(p)


def _norm_matmul(x, g, w_t, *, tm, name, carry=None):
    n, d = x.shape
    c = w_t.shape[0]
    ch = 512

    def body(x_ref, g_ref, wt_ref, h_ref, z_ref):
        xv = x_ref[...]
        r = lax.rsqrt(jnp.mean(xv * xv, axis=-1, keepdims=True) + RMS_EPS)
        h = (xv * r * g_ref[...]).astype(BF16)
        h_ref[...] = h
        for c0 in range(0, c, ch):
            z_ref[:, c0:c0 + ch] = _dot_nt(h, wt_ref[c0:c0 + ch, :]).astype(BF16)

    return _call(
        body, name=name, grid=(n // tm,), carry=carry,
        in_specs=[pl.BlockSpec((tm, d), lambda i: (i, 0)),
                  pl.BlockSpec((1, d), lambda i: (0, 0)),
                  pl.BlockSpec((c, d), lambda i: (0, 0))],
        out_specs=[pl.BlockSpec((tm, d), lambda i: (i, 0)),
                   pl.BlockSpec((tm, c), lambda i: (i, 0))],
        out_shape=[jax.ShapeDtypeStruct((n, d), BF16), jax.ShapeDtypeStruct((n, c), BF16)],
        args=(x, g.reshape(1, d), w_t))


def _mix_forward(z, x, b_gate, ln_g, ln_b, w_s, b_s, w_sc, wb, w_out, *, tm, name, carry=None):
    n = z.shape[0]
    hb = tm // HALO

    def body(z_ref, zp_ref, x_ref, bg_ref, lng_ref, lnb_ref, ws_ref, bs_ref, wsc_ref, wb_ref, wo_ref,
             ya_ref, yb_ref, cv_ref, pa_ref, pb_ref, mg_ref, x1_ref, f_scr):
        i = pl.program_id(0)
        u = z_ref[:, OFF_U:OFF_U + D_A]
        v = z_ref[:, OFF_V:OFF_V + D_A].astype(F32)
        gu, _, _, _, _, _, f = _gmlp_forward(u, v, lng_ref[...], lnb_ref[...], ws_ref, bs_ref, f_scr)
        ya = gu * f
        ya_ref[...] = ya

        q = z_ref[:, OFF_CG:OFF_CG + D_B] * z_ref[:, OFF_HB:OFF_HB + D_B]
        qp = zp_ref[:, OFF_CG:OFF_CG + D_B] * zp_ref[:, OFF_HB:OFF_HB + D_B]
        qp = jnp.where(i > 0, qp, jnp.zeros_like(qp))
        w = wsc_ref[...].astype(BF16)
        conv = w[0:1] * _shift_down(q, 2, qp) + w[1:2] * _shift_down(q, 1, qp) + w[2:3] * q
        cv_ref[...] = conv
        yb = z_ref[:, OFF_BG:OFF_BG + D_B] * conv
        yb_ref[...] = yb

        pa = _dot(ya, wb_ref[0]).astype(BF16)
        pb = _dot(yb, wb_ref[1]).astype(BF16)
        pa_ref[...] = pa
        pb_ref[...] = pb
        bg = bg_ref[...].astype(BF16)
        sa = _sigmoid(z_ref[:, OFF_GA:OFF_GA + D_MODEL] + bg[:, 0:D_MODEL])
        sb = _sigmoid(z_ref[:, OFF_GB:OFF_GB + D_MODEL] + bg[:, D_MODEL:2 * D_MODEL])
        mg = sa * pa + sb * pb
        mg_ref[...] = mg
        x1_ref[...] = x_ref[...] + _dot(mg, wo_ref[...])

    row = lambda w: pl.BlockSpec((tm, w), lambda i: (i, 0))
    full = lambda *s: pl.BlockSpec(s, lambda i: (0,) * len(s))
    bf = lambda w: jax.ShapeDtypeStruct((n, w), BF16)
    return _call(
        body, name=name, grid=(n // tm,), carry=carry,
        in_specs=[row(D_IN),
                  pl.BlockSpec((HALO, D_IN), lambda i: (jnp.maximum(i * hb - 1, 0), 0)),
                  row(D_MODEL), full(1, 2 * D_MODEL), full(1, D_A), full(1, D_A),
                  full(N_HEADS, GMLP_BLOCK, GMLP_BLOCK), full(N_HEADS, GMLP_BLOCK, 1), full(3, D_B),
                  full(2, D_A, D_MODEL), full(D_MODEL, D_MODEL)],
        out_specs=[row(D_A), row(D_B), row(D_B), row(D_MODEL), row(D_MODEL), row(D_MODEL), row(D_MODEL)],
        out_shape=[bf(D_A), bf(D_B), bf(D_B), bf(D_MODEL), bf(D_MODEL), bf(D_MODEL),
                   jax.ShapeDtypeStruct((n, D_MODEL), F32)],
        scratch_shapes=[pltpu.VMEM((tm, D_A), BF16)],
        args=(z, z, x, b_gate.reshape(1, -1), ln_g.reshape(1, -1), ln_b.reshape(1, -1), w_s,
              b_s.reshape(N_HEADS, GMLP_BLOCK, 1), w_sc, wb, w_out))


def _loss_tile(xv, gv, tv):
    d = xv.shape[-1]
    r = lax.rsqrt(jnp.mean(xv * xv, axis=-1, keepdims=True) + RMS_EPS)
    xh = xv * r
    e = xh * gv - tv
    per_row = jnp.sum(e * e, axis=-1, keepdims=True) * (0.5 / d)
    dy = e * (1.0 / d)
    dxh = dy * gv
    dx = r * (dxh - xh * jnp.mean(dxh * xh, axis=-1, keepdims=True))
    return dx, jnp.sum(per_row, axis=0, keepdims=True), jnp.sum(dy * xh, axis=0, keepdims=True)


def _ffn_forward(up, x1, w_fc, b_fc, w_down, *, tm, name, carry=None, head=None):
    n = up.shape[0]
    hb = tm // HALO
    n_in = 6 if head is None else 8

    def body(*refs):
        up_ref, upp_ref, x1_ref, wfc_ref, bfc_ref, wd_ref = refs[:6]
        gc_ref, a_ref, out_ref = refs[n_in:n_in + 3]
        acc = refs[-1]
        i = pl.program_id(0)
        acc[...] = x1_ref[...]
        for c0 in range(0, D_FF, FFN_CHUNK):
            cols = slice(c0, c0 + FFN_CHUNK)
            gate = up_ref[:, cols]
            val = up_ref[:, D_FF + c0:D_FF + c0 + FFN_CHUNK]
            gp = upp_ref[:, cols]
            gp = jnp.where(i > 0, gp, jnp.zeros_like(gp))
            w = wfc_ref[:, cols].astype(BF16)
            gc = (w[0:1] * _shift_down(gate, 2, gp) + w[1:2] * _shift_down(gate, 1, gp) + w[2:3] * gate
                  + bfc_ref[:, cols].astype(BF16))
            gc_ref[:, cols] = gc
            a = gc * _sigmoid(gc) * val
            a_ref[:, cols] = a
            acc[...] += _dot(a, wd_ref[cols, :])
        if head is None:
            out_ref[...] = acc[...]
        else:
            g_ref, t_ref = refs[6:8]
            loss_ref, dg_ref = refs[n_in + 3:n_in + 5]

            @pl.when(i == 0)
            def _():
                loss_ref[...] = jnp.zeros_like(loss_ref)
                dg_ref[...] = jnp.zeros_like(dg_ref)

            dx, loss, dg = _loss_tile(acc[...], g_ref[...], t_ref[...])
            out_ref[...] = dx
            loss_ref[...] += loss
            dg_ref[0:1, :] += dg

    row = lambda w: pl.BlockSpec((tm, w), lambda i: (i, 0))
    full = lambda r, c: pl.BlockSpec((r, c), lambda i: (0, 0))
    in_specs = [row(2 * D_FF), pl.BlockSpec((HALO, D_FF), lambda i: (jnp.maximum(i * hb - 1, 0), 0)), row(D_MODEL),
                full(3, D_FF), full(1, D_FF), full(D_FF, D_MODEL)]
    out_specs = [row(D_FF), row(D_FF), row(D_MODEL)]
    out_shape = [jax.ShapeDtypeStruct((n, D_FF), BF16), jax.ShapeDtypeStruct((n, D_FF), BF16),
                 jax.ShapeDtypeStruct((n, D_MODEL), F32)]
    args = (up, up, x1, w_fc, b_fc.reshape(1, -1), w_down)
    if head is not None:
        in_specs += [full(1, D_MODEL), row(D_MODEL)]
        out_specs += [full(1, 1), full(8, D_MODEL)]
        out_shape += [jax.ShapeDtypeStruct((1, 1), F32), jax.ShapeDtypeStruct((8, D_MODEL), F32)]
        args += (head[0].reshape(1, -1), head[1])
    return _call(body, name=name, grid=(n // tm,), carry=carry, in_specs=in_specs, out_specs=out_specs,
                 out_shape=out_shape, scratch_shapes=[pltpu.VMEM((tm, D_MODEL), F32)], args=args)


def _ffn_backward(dx2, up, gc, w_fc, w_down, *, tm, name, carry=None):
    n = up.shape[0]
    steps = n // tm
    hb = tm // HALO

    def body(dx_ref, dxn_ref, up_ref, upn_ref, gc_ref, gcn_ref, wfc_ref, wd_ref, dup_ref, dwc_ref):
        i = pl.program_id(0)
        last = i == steps - 1

        @pl.when(i == 0)
        def _():
            dwc_ref[...] = jnp.zeros_like(dwc_ref)

        dxe = jnp.concatenate([dx_ref[...], dxn_ref[...]], axis=0).astype(BF16)
        for c0 in range(0, D_FF, FFN_CHUNK):
            cols = slice(c0, c0 + FFN_CHUNK)
            vcols = slice(D_FF + c0, D_FF + c0 + FFN_CHUNK)
            dae = _dot_nt(dxe, wd_ref[cols, :])
            da, dan = dae[:tm], dae[tm:]
            gate = up_ref[:, cols]
            val = up_ref[:, vcols]
            gcv = gc_ref[:, cols]
            s = _sigmoid(gcv)
            dab = da.astype(BF16)
            dup_ref[:, vcols] = dab * (gcv * s)
            dgc = dab * val * (s * (1.0 + gcv * (1.0 - s)))
            gcn = gcn_ref[:, cols]
            sn = _sigmoid(gcn)
            dgcn = dan.astype(BF16) * upn_ref[:, vcols] * (sn * (1.0 + gcn * (1.0 - sn)))
            dgcn = jnp.where(last, jnp.zeros_like(dgcn), dgcn)
            up1 = _shift_up(dgc, 1, dgcn)
            up2 = _shift_up(dgc, 2, dgcn)
            w = wfc_ref[:, cols].astype(BF16)
            dup_ref[:, cols] = w[2:3] * dgc + w[1:2] * up1 + w[0:1] * up2
            dwc_ref[0:1, cols] += _column_sums(gate * up2)
            dwc_ref[1:2, cols] += _column_sums(gate * up1)
            dwc_ref[2:3, cols] += _column_sums(gate * dgc)
            dwc_ref[3:4, cols] += _column_sums(dgc)

    nxt = lambda i: (jnp.minimum((i + 1) * hb, steps * hb - 1), 0)
    return _call(
        body, name=name, grid=(steps,), carry=carry,
        in_specs=[pl.BlockSpec((tm, D_MODEL), lambda i: (i, 0)),
                  pl.BlockSpec((HALO, D_MODEL), nxt),
                  pl.BlockSpec((tm, 2 * D_FF), lambda i: (i, 0)),
                  pl.BlockSpec((HALO, 2 * D_FF), nxt),
                  pl.BlockSpec((tm, D_FF), lambda i: (i, 0)),
                  pl.BlockSpec((HALO, D_FF), nxt),
                  pl.BlockSpec((3, D_FF), lambda i: (0, 0)),
                  pl.BlockSpec((D_FF, D_MODEL), lambda i: (0, 0))],
        out_specs=[pl.BlockSpec((tm, 2 * D_FF), lambda i: (i, 0)),
                   pl.BlockSpec((8, D_FF), lambda i: (0, 0))],
        out_shape=[jax.ShapeDtypeStruct((n, 2 * D_FF), BF16), jax.ShapeDtypeStruct((8, D_FF), F32)],
        args=(dx2, dx2, up, up, gc, gc, w_fc, w_down))


def _matmul_norm_backward(dz, w_t, x, g, dres, *, tm, name, carry=None):
    n, c = dz.shape
    d = x.shape[1]
    ch = 512

    def body(dz_ref, wt_ref, x_ref, g_ref, dres_ref, dx_ref, dg_ref):
        i = pl.program_id(0)

        @pl.when(i == 0)
        def _():
            dg_ref[...] = jnp.zeros_like(dg_ref)

        dh = _dot(dz_ref[:, 0:ch], wt_ref[0:ch, :])
        for c0 in range(ch, c, ch):
            dh += _dot(dz_ref[:, c0:c0 + ch], wt_ref[c0:c0 + ch, :])
        xv = x_ref[...]
        r = lax.rsqrt(jnp.mean(xv * xv, axis=-1, keepdims=True) + RMS_EPS)
        xh = xv * r
        dg_ref[0:1, :] += jnp.sum(dh * xh, axis=0, keepdims=True)
        dxh = dh * g_ref[...]
        dx_ref[...] = dres_ref[...] + r * (dxh - xh * jnp.mean(dxh * xh, axis=-1, keepdims=True))

    return _call(
        body, name=name, grid=(n // tm,), carry=carry,
        in_specs=[pl.BlockSpec((tm, c), lambda i: (i, 0)),
                  pl.BlockSpec((c, d), lambda i: (0, 0)),
                  pl.BlockSpec((tm, d), lambda i: (i, 0)),
                  pl.BlockSpec((1, d), lambda i: (0, 0)),
                  pl.BlockSpec((tm, d), lambda i: (i, 0))],
        out_specs=[pl.BlockSpec((tm, d), lambda i: (i, 0)),
                   pl.BlockSpec((8, d), lambda i: (0, 0))],
        out_shape=[jax.ShapeDtypeStruct((n, d), F32), jax.ShapeDtypeStruct((8, d), F32)],
        args=(dz, w_t, x, g.reshape(1, d), dres))


def _mix_backward(dx1, z, conv, pa, pb, b_gate, ln_g, ln_b, w_s, w_s_t, b_s, w_sc, w_out, wb, *, tm, name, carry=None):
    n = z.shape[0]
    steps = n // tm
    hb = tm // HALO

    def body(dx_ref, dxn_ref, z_ref, zn_ref, cv_ref, pa_ref, pb_ref, bg_ref, lng_ref, lnb_ref, ws_ref, wst_ref,
             bs_ref, wsc_ref, wo_ref, wb_ref,
             dz_ref, dpa_ref, dpb_ref, dbg_ref, dln_ref, dws_ref, dbs_ref, dwsc_ref, f_scr, dvn_scr):
        i = pl.program_id(0)
        last = i == steps - 1

        @pl.when(i == 0)
        def _():
            dbg_ref[...] = jnp.zeros_like(dbg_ref)
            dln_ref[...] = jnp.zeros_like(dln_ref)
            dws_ref[...] = jnp.zeros_like(dws_ref)
            dbs_ref[...] = jnp.zeros_like(dbs_ref)
            dwsc_ref[...] = jnp.zeros_like(dwsc_ref)

        dxe = jnp.concatenate([dx_ref[...], dxn_ref[...]], axis=0).astype(BF16)
        dmge = _dot_nt(dxe, wo_ref[...])
        dmg, dmgn = dmge[:tm].astype(BF16), dmge[tm:].astype(BF16)

        pa_v = pa_ref[...]
        pb_v = pb_ref[...]
        bg = bg_ref[...].astype(BF16)
        sa = _sigmoid(z_ref[:, OFF_GA:OFF_GA + D_MODEL] + bg[:, 0:D_MODEL])
        sb = _sigmoid(z_ref[:, OFF_GB:OFF_GB + D_MODEL] + bg[:, D_MODEL:2 * D_MODEL])
        dpa = dmg * sa
        dpb = dmg * sb
        dga = dmg * pa_v * sa * (1.0 - sa)
        dgb = dmg * pb_v * sb * (1.0 - sb)
        dpa_ref[...] = dpa
        dpb_ref[...] = dpb
        dz_ref[:, OFF_GA:OFF_GA + D_MODEL] = dga
        dz_ref[:, OFF_GB:OFF_GB + D_MODEL] = dgb
        dbg_ref[0:1, 0:D_MODEL] += _column_sums(dga)
        dbg_ref[0:1, D_MODEL:2 * D_MODEL] += _column_sums(dgb)

        dya = _dot_nt(dpa, wb_ref[0]).astype(BF16)
        u = z_ref[:, OFF_U:OFF_U + D_A]
        v = z_ref[:, OFF_V:OFF_V + D_A].astype(F32)
        ln_g = lng_ref[...]
        gu, tu, tv, xh, rstd, vn, f = _gmlp_forward(u, v, ln_g, lnb_ref[...], ws_ref, bs_ref, f_scr)
        dgu = dya * f
        df_bf = dya * gu
        dz_ref[:, OFF_U:OFF_U + D_A] = dgu * _gelu_grad(u, tu)
        mask = _spatial_mask(False)
        mask_t = _spatial_mask(True)
        wmt = [jnp.where(mask_t, wst_ref[h], 0.0).astype(BF16) for h in range(N_HEADS)]
        for b in range(tm // GMLP_BLOCK):
            rows = slice(b * GMLP_BLOCK, (b + 1) * GMLP_BLOCK)
            for h in range(N_HEADS):
                cols = slice(h * HEAD, (h + 1) * HEAD)
                dfb = df_bf[rows, cols]
                dvn_scr[rows, cols] = _dot(wmt[h], dfb)
                dws_ref[h] += jnp.where(mask, _dot_nt(dfb, vn[rows, cols]), 0.0)
                dbs_ref[h] += jnp.sum(dfb.astype(F32), axis=1, keepdims=True)
        dvn = dvn_scr[...]
        dln_ref[0:1, :] += jnp.sum(dvn * xh, axis=0, keepdims=True)
        dln_ref[1:2, :] += jnp.sum(dvn, axis=0, keepdims=True)
        dxh = dvn * ln_g
        dgv = rstd * (dxh - jnp.mean(dxh, axis=-1, keepdims=True) - xh * jnp.mean(dxh * xh, axis=-1, keepdims=True))
        dz_ref[:, OFF_V:OFF_V + D_A] = (dgv * _gelu_grad(v, tv)).astype(BF16)

        sbn = _sigmoid(zn_ref[:, OFF_GB:OFF_GB + D_MODEL] + bg[:, D_MODEL:2 * D_MODEL])
        dpbe = jnp.concatenate([dpb, dmgn * sbn], axis=0)
        dybe = _dot_nt(dpbe, wb_ref[1])
        dyb, dybn = dybe[:tm].astype(BF16), dybe[tm:].astype(BF16)
        bgv = z_ref[:, OFF_BG:OFF_BG + D_B]
        cg = z_ref[:, OFF_CG:OFF_CG + D_B]
        hbv = z_ref[:, OFF_HB:OFF_HB + D_B]
        q = cg * hbv
        dz_ref[:, OFF_BG:OFF_BG + D_B] = dyb * cv_ref[...]
        dconv = dyb * bgv
        dconvn = dybn * zn_ref[:, OFF_BG:OFF_BG + D_B]
        dconvn = jnp.where(last, jnp.zeros_like(dconvn), dconvn)
        up1 = _shift_up(dconv, 1, dconvn)
        up2 = _shift_up(dconv, 2, dconvn)
        dwsc_ref[0:1, :] += _column_sums(q * up2)
        dwsc_ref[1:2, :] += _column_sums(q * up1)
        dwsc_ref[2:3, :] += _column_sums(q * dconv)
        w = wsc_ref[...].astype(BF16)
        dq = w[2:3] * dconv + w[1:2] * up1 + w[0:1] * up2
        dz_ref[:, OFF_CG:OFF_CG + D_B] = dq * hbv
        dz_ref[:, OFF_HB:OFF_HB + D_B] = dq * cg

    row = lambda w: pl.BlockSpec((tm, w), lambda i: (i, 0))
    full = lambda *s: pl.BlockSpec(s, lambda i: (0,) * len(s))
    nxt = lambda i: (jnp.minimum((i + 1) * hb, steps * hb - 1), 0)
    return _call(
        body, name=name, grid=(steps,), carry=carry,
        in_specs=[row(D_MODEL), pl.BlockSpec((HALO, D_MODEL), nxt),
                  row(D_IN), pl.BlockSpec((HALO, D_IN), nxt),
                  row(D_B), row(D_MODEL), row(D_MODEL),
                  full(1, 2 * D_MODEL), full(1, D_A), full(1, D_A),
                  full(N_HEADS, GMLP_BLOCK, GMLP_BLOCK), full(N_HEADS, GMLP_BLOCK, GMLP_BLOCK),
                  full(N_HEADS, GMLP_BLOCK, 1), full(3, D_B),
                  full(D_MODEL, D_MODEL), full(2, D_A, D_MODEL)],
        out_specs=[row(D_IN), row(D_MODEL), row(D_MODEL),
                   full(8, 2 * D_MODEL), full(8, D_A), full(N_HEADS, GMLP_BLOCK, GMLP_BLOCK),
                   full(N_HEADS, GMLP_BLOCK, 1), full(8, D_B)],
        out_shape=[jax.ShapeDtypeStruct((n, D_IN), BF16), jax.ShapeDtypeStruct((n, D_MODEL), BF16),
                   jax.ShapeDtypeStruct((n, D_MODEL), BF16),
                   jax.ShapeDtypeStruct((8, 2 * D_MODEL), F32), jax.ShapeDtypeStruct((8, D_A), F32),
                   jax.ShapeDtypeStruct((N_HEADS, GMLP_BLOCK, GMLP_BLOCK), F32),
                   jax.ShapeDtypeStruct((N_HEADS, GMLP_BLOCK, 1), F32), jax.ShapeDtypeStruct((8, D_B), F32)],
        scratch_shapes=[pltpu.VMEM((tm, D_A), BF16), pltpu.VMEM((tm, D_A), F32)],
        args=(dx1, dx1, z, z, conv, pa, pb, b_gate.reshape(1, -1), ln_g.reshape(1, -1), ln_b.reshape(1, -1), w_s, w_s_t,
              b_s.reshape(N_HEADS, GMLP_BLOCK, 1), w_sc, w_out, wb))


def _matmul_tn(a, b, *, t1, tn, name, carry=None, pieces=1):
    n, k1 = a.shape
    k2 = b.shape[1]
    steps = n // tn
    w = k2 // pieces

    def body(a_ref, b_ref, *rest):
        o_refs, acc = rest[:pieces], rest[pieces]
        s = pl.program_id(1)

        @pl.when(s == 0)
        def _():
            acc[...] = jnp.zeros_like(acc)

        acc[...] += lax.dot_general(a_ref[...].astype(BF16), b_ref[...].astype(BF16), TN, preferred_element_type=F32)

        @pl.when(s == steps - 1)
        def _():
            for c, o_ref in enumerate(o_refs):
                o_ref[...] = acc[:, c * w:(c + 1) * w].astype(BF16)

    outs, carried = _call(
        body, name=name, grid=(k1 // t1, steps), carry=carry,
        in_specs=[pl.BlockSpec((tn, t1), lambda i, s: (s, i)),
                  pl.BlockSpec((tn, k2), lambda i, s: (s, 0))],
        out_specs=[pl.BlockSpec((t1, w), lambda i, s: (i, 0))] * pieces,
        out_shape=[jax.ShapeDtypeStruct((k1, w), BF16)] * pieces,
        scratch_shapes=[pltpu.VMEM((t1, k2), F32)],
        args=(a, b))
    return (outs[0] if pieces == 1 else list(outs)), carried


def _adamw_math(w, g, m, v):
    m = ADAM_B1 * m + (1.0 - ADAM_B1) * g
    v = ADAM_B2 * v + (1.0 - ADAM_B2) * (g * g)
    m_hat = m / (1.0 - ADAM_B1 ** ADAM_STEP)
    v_hat = v / (1.0 - ADAM_B2 ** ADAM_STEP)
    delta = -ADAM_LR * (m_hat / (jnp.sqrt(v_hat) + ADAM_EPS) + ADAM_WD * w)
    return delta, m, v


def _sum_parts(recvs, *, tr, name):
    _, r, c = recvs[0].shape

    def body(*refs):
        recv_refs, g_ref = refs[:DEPTH], refs[DEPTH]
        layer = pl.program_id(0)
        for l in range(DEPTH):
            @pl.when(layer == l)
            def _(l=l):
                g = recv_refs[l][0].astype(F32)
                for s in range(1, N_DEV):
                    g = g + recv_refs[l][s].astype(F32)
                g_ref[0] = g

    outs, _ = _call(
        body, name=name, grid=(DEPTH, r // tr),
        in_specs=[pl.BlockSpec((N_DEV, tr, c), lambda l, i: (0, i, 0))] * DEPTH,
        out_specs=[pl.BlockSpec((1, tr, c), lambda l, i: (l, i, 0))],
        out_shape=[jax.ShapeDtypeStruct((DEPTH, r, c), F32)],
        args=tuple(recvs))
    return outs[0]


def _adamw(w, g, m, v, *, tr, name):
    r, c = w.shape

    def body(w_ref, g_ref, m_ref, v_ref, d_ref, nm_ref, nv_ref):
        delta, nm, nv = _adamw_math(w_ref[...], g_ref[...], m_ref[...], v_ref[...])
        d_ref[...] = delta
        nm_ref[...] = nm
        nv_ref[...] = nv

    spec = pl.BlockSpec((tr, c), lambda i: (i, 0))
    outs, _ = _call(body, name=name, grid=(r // tr,), in_specs=[spec] * 4, out_specs=[spec] * 3,
                    out_shape=[jax.ShapeDtypeStruct((r, c), F32)] * 3, args=(w, g, m, v))
    return outs


def _sum_adamw(recvs, w, m, v, *, tr, name):
    _, r, c = w.shape
    blocks = len(recvs[0])
    flat = [piece for layer in recvs for piece in layer]

    def body(*refs):
        recv_refs = refs[:len(flat)]
        w_ref, m_ref, v_ref, g_ref, d_ref, nm_ref, nv_ref = refs[len(flat):]
        layer = pl.program_id(0)
        for l in range(DEPTH):
            @pl.when(layer == l)
            def _(l=l):
                cols = []
                for piece in recv_refs[l * blocks:(l + 1) * blocks]:
                    part = piece[0].astype(F32)
                    for s in range(1, N_DEV):
                        part = part + piece[s].astype(F32)
                    cols.append(part)
                g = cols[0] if blocks == 1 else jnp.concatenate(cols, axis=-1)
                delta, nm, nv = _adamw_math(w_ref[0], g, m_ref[0], v_ref[0])
                g_ref[0] = g
                d_ref[0] = delta
                nm_ref[0] = nm
                nv_ref[0] = nv

    spec = pl.BlockSpec((1, tr, c), lambda l, i: (l, i, 0))
    outs, _ = _call(
        body, name=name, grid=(DEPTH, r // tr),
        in_specs=[pl.BlockSpec((N_DEV, tr, c // blocks), lambda l, i: (0, i, 0))] * len(flat) + [spec] * 3,
        out_specs=[spec] * 4, out_shape=[jax.ShapeDtypeStruct((DEPTH, r, c), F32)] * 4,
        args=tuple(flat) + (w, m, v))
    return outs


def _adamw_small(w, g, m, v, *, name):
    def body(w_ref, g_ref, m_ref, v_ref, d_ref, nm_ref, nv_ref):
        delta, nm, nv = _adamw_math(w_ref[...], g_ref[...], m_ref[...], v_ref[...])
        d_ref[...] = delta
        nm_ref[...] = nm
        nv_ref[...] = nv

    vmem = pl.BlockSpec(memory_space=pltpu.VMEM)
    return pl.pallas_call(
        body, name=name, in_specs=[vmem] * 4, out_specs=[vmem] * 3,
        out_shape=[jax.ShapeDtypeStruct(w.shape, F32)] * 3,
    )(w, g, m, v)


def _pack(arrs):
    flat = jnp.concatenate([a.reshape(-1) for a in arrs])
    pad = (-flat.shape[0]) % 1024
    return jnp.pad(flat, (0, pad)).reshape(-1, 128)


def _unpack(packed, shapes):
    flat = packed.reshape(-1)
    out, o = [], 0
    for s in shapes:
        size = math.prod(s)
        out.append(flat[o:o + size].reshape(s))
        o += size
    return out


def _rows(gathered):
    return gathered.reshape(N_DEV * gathered.shape[1], gathered.shape[2])


def _parts(full):
    return full.reshape(N_DEV, full.shape[0] // N_DEV, full.shape[1])


def kernel(x, norm1_g, w_in, b_gate, gmlp_ln_g, gmlp_ln_b, w_spatial, b_spatial, w_shortconv, w_branch, w_out, norm2_g, w_ffn_up, w_ffn_conv, b_ffn_conv, w_ffn_down, final_g, loss_target, m_norm1_g, m_w_in, m_b_gate, m_gmlp_ln_g, m_gmlp_ln_b, m_w_spatial, m_b_spatial, m_w_shortconv, m_w_branch, m_w_out, m_norm2_g, m_w_ffn_up, m_w_ffn_conv, m_b_ffn_conv, m_w_ffn_down, m_final_g, v_norm1_g, v_w_in, v_b_gate, v_gmlp_ln_g, v_gmlp_ln_b, v_w_spatial, v_b_spatial, v_w_shortconv, v_w_branch, v_w_out, v_norm2_g, v_w_ffn_up, v_w_ffn_conv, v_b_ffn_conv, v_w_ffn_down, v_final_g):
    n = x.shape[1]
    tm, tm_mix, tn = 512, 512, 1024
    x0 = x.reshape(n, D_MODEL)
    target = loss_target.reshape(n, D_MODEL)
    my_idx = 4 * lax.axis_index("x") + 2 * lax.axis_index("y") + lax.axis_index("c")
    sc_w, fc_w = D_B // N_DEV, D_FF // N_DEV

    sh_in = [w_in[l].T.astype(BF16) for l in range(DEPTH)]
    sh_up = [w_ffn_up[l].T.astype(BF16) for l in range(DEPTH)]
    sh_br = [w_branch[l].astype(BF16) for l in range(DEPTH)]
    sh_out = [w_out[l].astype(BF16) for l in range(DEPTH)]
    sh_down = [w_ffn_down[l].astype(BF16) for l in range(DEPTH)]
    taps = jnp.concatenate([w_shortconv, w_ffn_conv], axis=-1)

    def branch_weights(g):
        return g.transpose(1, 2, 0, 3).reshape(2, D_A, D_MODEL)

    g_in0, g_taps = _gather_now([sh_in[0], taps], name="gather_first")
    w_sc = [g_taps[:, l, :, :sc_w].transpose(1, 0, 2).reshape(3, D_B) for l in range(DEPTH)]
    w_fc = [g_taps[:, l, :, sc_w:].transpose(1, 0, 2).reshape(3, D_FF) for l in range(DEPTH)]
    w_s_t = [w_spatial[l].transpose(0, 2, 1) for l in range(DEPTH)]
    weights = [dict(), dict()]
    weights[0]["in_t"] = _rows(g_in0)
    saved = []
    xc = x0
    for l in range(DEPTH):
        p = weights[l]
        carry = _Gather([sh_br[0], sh_out[0]] if l == 0 else [sh_up[1]])
        (h, z), got = _norm_matmul(xc, norm1_g[l], p["in_t"], tm=tm, name=f"fwd_in_{l}", carry=carry)
        if l == 0:
            p["wb"], p["out"] = branch_weights(got[0]), _rows(got[1])
        else:
            p["up_t"] = _rows(got[0])
        carry = _Gather([sh_up[0]]) if l == 0 else None
        (ya, yb, conv, pa, pb, mg, x1), got = _mix_forward(
            z, xc, b_gate[l], gmlp_ln_g[l], gmlp_ln_b[l], w_spatial[l], b_spatial[l], w_sc[l], p["wb"], p["out"],
            tm=tm_mix, name=f"fwd_mix_{l}", carry=carry)
        if l == 0:
            p["up_t"] = _rows(got[0])
        carry = _Gather([sh_down[0], sh_in[1]] if l == 0 else [sh_down[1]])
        (h2, up), got = _norm_matmul(x1, norm2_g[l], p["up_t"], tm=tm, name=f"fwd_up_{l}", carry=carry)
        if l == 0:
            p["down"], weights[1]["in_t"] = _rows(got[0]), _rows(got[1])
        else:
            p["down"] = _rows(got[0])
        carry = _Gather([sh_br[1], sh_out[1]]) if l == 0 else None
        head = (final_g, target) if l == DEPTH - 1 else None
        outs, got = _ffn_forward(up, x1, w_fc[l], b_ffn_conv[l], p["down"], tm=tm, name=f"fwd_ffn_{l}", carry=carry, head=head)
        if l == 0:
            weights[1]["wb"], weights[1]["out"] = branch_weights(got[0]), _rows(got[1])
        gc, a = outs[0], outs[1]
        saved.append(dict(x=xc, h=h, z=z, ya=ya, yb=yb, conv=conv, pa=pa, pb=pb, mg=mg, x1=x1, h2=h2, up=up, gc=gc, a=a))
        xc = outs[2]
    dx, loss_part, dgf = outs[2], outs[3], outs[4]

    recv = [dict(), dict()]
    small = [None] * DEPTH
    pending_in = None
    for l in reversed(range(DEPTH)):
        p, s = weights[l], saved[l]
        carry = _Exchange([pending_in]) if pending_in is not None else None
        (dup, dwc), got = _ffn_backward(dx, s["up"], s["gc"], w_fc[l], p["down"], tm=tm, name=f"bwd_ffn_{l}", carry=carry)
        if got is not None:
            recv[l + 1]["in_t"] = got[0]
        dw_down, _ = _matmul_tn(s["a"], dx, t1=D_FF // 2, tn=tn, name=f"dw_down_{l}")
        dw_up_t, got = _matmul_tn(dup, s["h2"], t1=2 * D_FF // 4, tn=tn, name=f"dw_up_{l}", pieces=2,
                                  carry=_Exchange([_parts(dw_down)]))
        recv[l]["down"] = got[0]
        (dx1, dg2), got_left = _matmul_norm_backward(dup, p["up_t"], s["x1"], norm2_g[l], dx, tm=tm, name=f"bwd_up_{l}",
                                                     carry=_Exchange([_parts(dw_up_t[0])]))
        (dz, dpa, dpb, dbg, dln, dws, dbs, dwsc), got_right = _mix_backward(
            dx1, s["z"], s["conv"], s["pa"], s["pb"], b_gate[l], gmlp_ln_g[l], gmlp_ln_b[l], w_spatial[l], w_s_t[l],
            b_spatial[l], w_sc[l], p["out"], p["wb"], tm=tm_mix, name=f"bwd_mix_{l}",
            carry=_Exchange([_parts(dw_up_t[1])]))
        recv[l]["up_t"] = [got_left[0], got_right[0]]
        dw_out, _ = _matmul_tn(s["mg"], dx1, t1=D_MODEL, tn=tn, name=f"dw_out_{l}")
        dw_bra_t, _ = _matmul_tn(dpa, s["ya"], t1=D_MODEL, tn=tn, name=f"dw_branch_a_{l}")
        dw_brb_t, _ = _matmul_tn(dpb, s["yb"], t1=D_MODEL, tn=tn, name=f"dw_branch_b_{l}")
        dw_in_t, got = _matmul_tn(dz, s["h"], t1=D_IN // 4, tn=tn, name=f"dw_in_{l}",
                                  carry=_Exchange([_parts(dw_out), _parts(dw_bra_t), _parts(dw_brb_t)]))
        recv[l]["out"], recv[l]["bra_t"], recv[l]["brb_t"] = got
        carry = _Exchange([_parts(dw_in_t)]) if l == 0 else None
        (dx0, dg1), got = _matmul_norm_backward(dz, p["in_t"], s["x"], norm1_g[l], dx1, tm=tm, name=f"bwd_in_{l}", carry=carry)
        if l == 0:
            recv[0]["in_t"] = got[0]
        else:
            pending_in = _parts(dw_in_t)
        small[l] = dict(norm1_g=dg1[0], b_gate=dbg[0], gmlp_ln_g=dln[0], gmlp_ln_b=dln[1], w_spatial=dws,
                        b_spatial=dbs.reshape(N_HEADS, GMLP_BLOCK), w_shortconv=dwsc[0:3], norm2_g=dg2[0],
                        w_ffn_conv=dwc[0:3], b_ffn_conv=dwc[3])
        dx = dx0
    grad_x = dx.reshape(x.shape)

    results = {}
    both = lambda key: [recv[l][key] for l in range(DEPTH)]
    blocks = lambda key: [r if isinstance(r, list) else [r] for r in both(key)]
    swap = lambda t: t.transpose(0, 2, 1)
    for key, slab, (w, m, v), tr in [("w_in", "in_t", (w_in, m_w_in, v_w_in), 192),
                                     ("w_ffn_up", "up_t", (w_ffn_up, m_w_ffn_up, v_w_ffn_up), 176)]:
        outs = _sum_adamw(blocks(slab), swap(w), swap(m), swap(v), tr=tr, name=f"adamw_{key}")
        results[key] = tuple(swap(o) for o in outs)
    g_bra = _sum_parts(both("bra_t"), tr=128, name="sum_w_branch_a").transpose(0, 2, 1)
    g_brb = _sum_parts(both("brb_t"), tr=128, name="sum_w_branch_b").transpose(0, 2, 1)
    g_br = jnp.stack([g_bra, g_brb], axis=1)
    flat = lambda t: t.reshape(-1, t.shape[-1])
    outs = _adamw(flat(w_branch), flat(g_br), flat(m_w_branch), flat(v_w_branch), tr=512, name="adamw_w_branch")
    results["w_branch"] = (g_br,) + tuple(o.reshape(w_branch.shape) for o in outs)
    results["w_out"] = tuple(_sum_adamw(blocks("out"), w_out, m_w_out, v_w_out, tr=128, name="adamw_w_out"))
    results["w_ffn_down"] = tuple(_sum_adamw(blocks("down"), w_ffn_down, m_w_ffn_down, v_w_ffn_down, tr=176,
                                             name="adamw_w_ffn_down"))

    small_names = ["norm1_g", "b_gate", "gmlp_ln_g", "gmlp_ln_b", "w_spatial", "b_spatial", "w_shortconv", "norm2_g",
                   "w_ffn_conv", "b_ffn_conv"]
    stacked = [jnp.stack([small[l][k] for l in range(DEPTH)]) for k in small_names] + [dgf[0], loss_part.reshape(1)]
    shapes = [a.shape for a in stacked]
    reduced = _unpack(_all_reduce_small(_pack(stacked), name="all_reduce_small_grads"), shapes)
    loss = reduced.pop()[0]
    g_small = dict(zip(small_names + ["final_g"], reduced))
    g_small["w_shortconv"] = lax.dynamic_slice_in_dim(g_small["w_shortconv"], my_idx * sc_w, sc_w, axis=2)
    g_small["w_ffn_conv"] = lax.dynamic_slice_in_dim(g_small["w_ffn_conv"], my_idx * fc_w, fc_w, axis=2)
    small_w = dict(norm1_g=(norm1_g, m_norm1_g, v_norm1_g), b_gate=(b_gate, m_b_gate, v_b_gate),
                   gmlp_ln_g=(gmlp_ln_g, m_gmlp_ln_g, v_gmlp_ln_g), gmlp_ln_b=(gmlp_ln_b, m_gmlp_ln_b, v_gmlp_ln_b),
                   w_spatial=(w_spatial, m_w_spatial, v_w_spatial), b_spatial=(b_spatial, m_b_spatial, v_b_spatial),
                   w_shortconv=(w_shortconv, m_w_shortconv, v_w_shortconv), norm2_g=(norm2_g, m_norm2_g, v_norm2_g),
                   w_ffn_conv=(w_ffn_conv, m_w_ffn_conv, v_w_ffn_conv), b_ffn_conv=(b_ffn_conv, m_b_ffn_conv, v_b_ffn_conv),
                   final_g=(final_g, m_final_g, v_final_g))
    order = small_names + ["final_g"]
    local_shapes = [small_w[k][0].shape for k in order]
    packed = [_pack([small_w[k][j] for k in order]) for j in range(3)]
    d_s, m_s, v_s = _adamw_small(packed[0], _pack([g_small[k] for k in order]), packed[1], packed[2], name="adamw_small")
    d_s, m_s, v_s = _unpack(d_s, local_shapes), _unpack(m_s, local_shapes), _unpack(v_s, local_shapes)
    for j, k in enumerate(order):
        results[k] = (g_small[k], d_s[j], m_s[j], v_s[j])

    names = ["norm1_g", "w_in", "b_gate", "gmlp_ln_g", "gmlp_ln_b", "w_spatial", "b_spatial", "w_shortconv", "w_branch",
             "w_out", "norm2_g", "w_ffn_up", "w_ffn_conv", "b_ffn_conv", "w_ffn_down", "final_g"]
    return (loss, grad_x, *[results[k][0] for k in names], *[results[k][1] for k in names],
            *[results[k][2] for k in names], *[results[k][3] for k in names])
```

```python
import math

import jax
import jax.numpy as jnp
from jax import lax
from jax.experimental import pallas as pl
from jax.experimental.pallas import tpu as pltpu

F32 = jnp.float32
BF16 = jnp.bfloat16

N_DEV = 8
DEPTH = 2
D_MODEL = 1024
D_A = 512
D_B = 512
D_FF = 2816
D_IN = 4608
N_HEADS = 4
HEAD = 128
GMLP_BLOCK = 128
CAUSAL_CHUNK = 64
OFF_U, OFF_V, OFF_BG, OFF_CG, OFF_HB, OFF_GA, OFF_GB = 0, 512, 1024, 1536, 2048, 2560, 3584
RMS_EPS = 1e-6
LN_EPS = 1e-5
ADAM_LR, ADAM_B1, ADAM_B2, ADAM_EPS, ADAM_WD, ADAM_STEP = 0.001, 0.9, 0.999, 1e-08, 0.01, 10

SUBLANES = 8
HALO = 16
FFN_CHUNK = 256
SG_ROWS, SG_W, SG_LAYER = 40, D_FF, 16
ROW_NORM1, ROW_BGATE, ROW_LN_G, ROW_LN_B, ROW_SCONV, ROW_NORM2, ROW_FCONV, ROW_BFCONV = 0, 1, 2, 3, 4, 7, 8, 11
ROW_FINAL, ROW_LOSS = 32, 33
V7X_VMEM_BYTES = 64 << 20
VMEM_LIMIT = V7X_VMEM_BYTES - (8 << 20)
MESH = pl.DeviceIdType.MESH
GELU_C0 = 0.7978845608028654
GELU_C1 = 0.044715
NT = (((1,), (1,)), ((), ()))
TN = (((0,), (0,)), ((), ()))


def _dot(a, b):
    return jnp.dot(a, b, preferred_element_type=F32)


def _dot_nt(a, b):
    return lax.dot_general(a, b, NT, preferred_element_type=F32)


def _sigmoid(x):
    return 1.0 / (1.0 + jnp.exp(-x))


def _gelu_tanh(x):
    return jnp.tanh(GELU_C0 * (x + GELU_C1 * x * x * x))


def _gelu_grad(x, t):
    return 0.5 * (1.0 + t) + 0.5 * x * (1.0 - t * t) * GELU_C0 * (1.0 + 3.0 * GELU_C1 * x * x)


def _sublane_tile(dtype):
    return SUBLANES * (4 // jnp.dtype(dtype).itemsize)


def _shift_down(a, k, prev):
    p = prev.shape[0]
    r = pltpu.roll(a, k, 0)
    sub = _sublane_tile(a.dtype)
    head = r[0:sub]
    rid = lax.broadcasted_iota(jnp.int32, head.shape, 0)
    for j in range(k):
        head = jnp.where(rid == j, prev[p - k + j:p - k + j + 1, :], head)
    return jnp.concatenate([head, r[sub:]], axis=0)


def _shift_up(a, k, nxt):
    t = a.shape[0]
    r = pltpu.roll(a, t - k, 0)
    sub = _sublane_tile(a.dtype)
    tail = r[t - sub:t]
    rid = lax.broadcasted_iota(jnp.int32, tail.shape, 0)
    for j in range(k):
        tail = jnp.where(rid == sub - k + j, nxt[j:j + 1, :], tail)
    return jnp.concatenate([r[0:t - sub], tail], axis=0)


def _column_sums(p):
    if p.dtype.itemsize < 4:
        t = p.shape[0]
        p = p[:t // 2] + p[t // 2:]
        p = p[:t // 4] + p[t // 4:]
    return jnp.sum(p.astype(F32), axis=0, keepdims=True)


def _sheet_begin(step, sheet_in, sheet_out, first_row, rows):
    @pl.when(step == 0)
    def _():
        sheet_out[...] = sheet_in[...]
        sheet_out[first_row:first_row + rows, :] = jnp.zeros((rows, SG_W), F32)


def _sheet_spec():
    return pl.BlockSpec((SG_ROWS, SG_W), lambda i: (0, 0))


def _spatial_mask(transposed):
    ri = lax.broadcasted_iota(jnp.int32, (GMLP_BLOCK, GMLP_BLOCK), 0) // CAUSAL_CHUNK
    ci = lax.broadcasted_iota(jnp.int32, (GMLP_BLOCK, GMLP_BLOCK), 1) // CAUSAL_CHUNK
    return (ri <= ci) if transposed else (ci <= ri)


def _gmlp_forward(u, v, ln_g, ln_b, ws_ref, bs_ref, f_scr):
    tm = u.shape[0]
    tu = _gelu_tanh(u)
    tv = _gelu_tanh(v)
    gu = 0.5 * u * (1.0 + tu)
    gv = 0.5 * v * (1.0 + tv)
    mu = jnp.mean(gv, axis=-1, keepdims=True)
    cen = gv - mu
    rstd = lax.rsqrt(jnp.mean(cen * cen, axis=-1, keepdims=True) + LN_EPS)
    xh = cen * rstd
    vn = (xh * ln_g + ln_b).astype(BF16)
    mask = _spatial_mask(False)
    wm = [jnp.where(mask, ws_ref[h], 0.0).astype(BF16) for h in range(N_HEADS)]
    for b in range(tm // GMLP_BLOCK):
        rows = slice(b * GMLP_BLOCK, (b + 1) * GMLP_BLOCK)
        for h in range(N_HEADS):
            cols = slice(h * HEAD, (h + 1) * HEAD)
            f_scr[rows, cols] = (_dot(wm[h], vn[rows, cols]) + bs_ref[h]).astype(f_scr.dtype)
    return gu, tu, tv, xh, rstd, vn, f_scr[...]


def _position():
    return lax.axis_index("x"), lax.axis_index("y"), lax.axis_index("c")


class _Gather:
    def __init__(self, arrays):
        self.arrays = list(arrays)
        self.out_shape = [jax.ShapeDtypeStruct((N_DEV,) + a.shape, a.dtype) for a in self.arrays]

    def _plan(self, ins, outs, sems):
        send_sems, recv_sems, local_sems = sems
        x, y, c = _position()
        me, sibling = (x, y, c), (x, y, 1 - c)
        chips = [(1 - x, y), (x, 1 - y), (1 - x, 1 - y)]

        def slot(a, p):
            return outs[a].at[4 * p[0] + 2 * p[1] + p[2]]

        def copy(a, k, block, to, src=None):
            return pltpu.make_async_remote_copy(
                src_ref=slot(a, block) if src is None else src, dst_ref=slot(a, block),
                send_sem=send_sems.at[a, k], recv_sem=recv_sems.at[a, k], device_id=to, device_id_type=MESH)

        n = len(self.arrays)

        def mine():
            return [pltpu.make_async_copy(ins[a], slot(a, me), local_sems.at[a]) for a in range(n)]

        def first():
            out = []
            for a in range(n):
                out.append(copy(a, 0, me, sibling, src=ins[a]))
                out += [copy(a, 1 + j, me, (*chip, c), src=ins[a]) for j, chip in enumerate(chips)]
            return out

        def arrivals():
            return [copy(a, 1 + j, (*chip, c), me) for j, chip in enumerate(chips) for a in range(n)]

        def relays():
            return [copy(a, 4 + j, (*chip, c), sibling) for j, chip in enumerate(chips) for a in range(n)]

        def from_sibling():
            out = [copy(a, 0, sibling, me) for a in range(n)]
            return out + [copy(a, 4 + j, (*chip, 1 - c), me) for j, chip in enumerate(chips) for a in range(n)]

        return mine, first, arrivals, relays, from_sibling

    def start(self, ins, outs, sems):
        mine, first, _, _, _ = self._plan(ins, outs, sems)
        for cp in mine() + first():
            cp.start()

    def relay(self, ins, outs, sems):
        _, _, arrivals, relays, _ = self._plan(ins, outs, sems)
        for arrived, onward in zip(arrivals(), relays()):
            arrived.wait_recv()
            onward.start()

    def finish(self, ins, outs, sems):
        mine, first, _, relays, from_sibling = self._plan(ins, outs, sems)
        for cp in from_sibling():
            cp.wait_recv()
        for cp in first() + relays():
            cp.wait_send()
        for cp in mine():
            cp.wait()


class _Exchange:
    def __init__(self, arrays):
        self.arrays = list(arrays)
        self.out_shape = [jax.ShapeDtypeStruct(a.shape, a.dtype) for a in self.arrays]

    def _plan(self, ins, outs, sems):
        send_sems, recv_sems, local_sems = sems
        x, y, c = _position()
        my_idx = 4 * x + 2 * y + c
        n = len(self.arrays)
        offsets = [(dx, dy, dc) for dx in (0, 1) for dy in (0, 1) for dc in (0, 1) if (dx, dy, dc) != (0, 0, 0)]

        def mine():
            return [pltpu.make_async_copy(ins[a].at[my_idx], outs[a].at[my_idx], local_sems.at[a]) for a in range(n)]

        def remote(arriving):
            out = []
            for k, (dx, dy, dc) in enumerate(offsets):
                px, py, pc = x ^ dx, y ^ dy, c ^ dc
                p_idx = 4 * px + 2 * py + pc
                for a in range(n):
                    out.append(pltpu.make_async_remote_copy(
                        src_ref=ins[a].at[p_idx], dst_ref=outs[a].at[p_idx if arriving else my_idx],
                        send_sem=send_sems.at[a, k], recv_sem=recv_sems.at[a, k],
                        device_id=(px, py, pc), device_id_type=MESH))
            return out

        return mine, remote

    def start(self, ins, outs, sems):
        mine, remote = self._plan(ins, outs, sems)
        for cp in mine() + remote(False):
            cp.start()

    def relay(self, ins, outs, sems):
        pass

    def finish(self, ins, outs, sems):
        mine, remote = self._plan(ins, outs, sems)
        for cp in remote(True):
            cp.wait_recv()
        for cp in remote(False):
            cp.wait_send()
        for cp in mine():
            cp.wait()


def _call(body, *, name, grid, in_specs, out_specs, out_shape, args, scratch_shapes=(), carry=None):
    n_in, n_out, n_scr = len(in_specs), len(out_specs), len(scratch_shapes)
    params = pltpu.CompilerParams(dimension_semantics=("arbitrary",) * len(grid), vmem_limit_bytes=VMEM_LIMIT)
    if carry is None:
        outs = pl.pallas_call(body, name=name, grid=grid, in_specs=in_specs, out_specs=out_specs, out_shape=out_shape,
                              scratch_shapes=list(scratch_shapes), compiler_params=params)(*args)
        return outs, None
    m = len(carry.arrays)
    total = math.prod(grid)

    def wrapped(*refs):
        ins, refs = refs[:n_in], refs[n_in:]
        c_ins, refs = refs[:m], refs[m:]
        outs, refs = refs[:n_out], refs[n_out:]
        c_outs, refs = refs[:m], refs[m:]
        scr, sems = refs[:n_scr], refs[n_scr:]
        flat = pl.program_id(0)
        for d in range(1, len(grid)):
            flat = flat * grid[d] + pl.program_id(d)

        @pl.when(flat == 0)
        def _():
            carry.start(c_ins, c_outs, sems)

        body(*ins, *outs, *scr)

        @pl.when(flat == total - 2)
        def _():
            carry.relay(c_ins, c_outs, sems)

        @pl.when(flat == total - 1)
        def _():
            carry.finish(c_ins, c_outs, sems)

    any_spec = pl.BlockSpec(memory_space=pl.ANY)
    sem_shapes = [pltpu.SemaphoreType.DMA((m, 7)), pltpu.SemaphoreType.DMA((m, 7)), pltpu.SemaphoreType.DMA((m,))]
    outs = pl.pallas_call(
        wrapped, name=name, grid=grid,
        in_specs=list(in_specs) + [any_spec] * m, out_specs=list(out_specs) + [any_spec] * m,
        out_shape=list(out_shape) + carry.out_shape,
        scratch_shapes=list(scratch_shapes) + sem_shapes, compiler_params=params)(*args, *carry.arrays)
    return outs[:n_out], outs[n_out:]


def _gather_now(arrays, *, name):
    carry = _Gather(arrays)
    m = len(arrays)

    def body(*refs):
        ins, outs, sems = refs[:m], refs[m:2 * m], refs[2 * m:]
        carry.start(ins, outs, sems)
        carry.relay(ins, outs, sems)
        carry.finish(ins, outs, sems)

    any_spec = pl.BlockSpec(memory_space=pl.ANY)
    return pl.pallas_call(
        body, name=name, in_specs=[any_spec] * m, out_specs=[any_spec] * m, out_shape=carry.out_shape,
        scratch_shapes=[pltpu.SemaphoreType.DMA((m, 7)), pltpu.SemaphoreType.DMA((m, 7)),
                        pltpu.SemaphoreType.DMA((m,))],
    )(*arrays)


def _all_reduce_small(arrs, *, name):
    n = len(arrs)

    def body(*refs):
        ins, outs, bufs = refs[:n], refs[n:2 * n], refs[2 * n:3 * n]
        send_sems, recv_sems = refs[3 * n:]
        x, y, c = _position()
        me, sibling = (x, y, c), (x, y, 1 - c)
        chips = [(1 - x, y), (x, 1 - y), (1 - x, 1 - y)]

        def copy(a, k, block, to, src=None):
            slot = bufs[a].at[4 * block[0] + 2 * block[1] + block[2]]
            return pltpu.make_async_remote_copy(
                src_ref=slot if src is None else src, dst_ref=slot,
                send_sem=send_sems.at[a, k], recv_sem=recv_sems.at[a, k], device_id=to, device_id_type=MESH)

        first = []
        for a in range(n):
            first.append(copy(a, 0, me, sibling, src=ins[a]))
            first += [copy(a, 1 + j, me, (*chip, c), src=ins[a]) for j, chip in enumerate(chips)]
        for cp in first:
            cp.start()
        passed = []
        for j, chip in enumerate(chips):
            for a in range(n):
                copy(a, 1 + j, (*chip, c), me).wait_recv()
                cp = copy(a, 4 + j, (*chip, c), sibling)
                cp.start()
                passed.append(cp)
        for a in range(n):
            copy(a, 0, sibling, me).wait_recv()
            for j, chip in enumerate(chips):
                copy(a, 4 + j, (*chip, 1 - c), me).wait_recv()
        for cp in first + passed:
            cp.wait_send()
        my_idx = 4 * x + 2 * y + c
        for a in range(n):
            acc = jnp.zeros(ins[a].shape, F32)
            for s in range(N_DEV):
                acc = acc + jnp.where(my_idx == s, ins[a][...], bufs[a][s])
            outs[a][...] = acc

    vmem = pl.BlockSpec(memory_space=pltpu.VMEM)
    return pl.pallas_call(
        body, name=name, in_specs=[vmem] * n, out_specs=[vmem] * n,
        out_shape=[jax.ShapeDtypeStruct(a.shape, F32) for a in arrs],
        scratch_shapes=[pltpu.VMEM((N_DEV,) + a.shape, F32) for a in arrs]
        + [pltpu.SemaphoreType.DMA((n, 7)), pltpu.SemaphoreType.DMA((n, 7))],
        compiler_params=pltpu.CompilerParams(vmem_limit_bytes=VMEM_LIMIT),
    )(*arrs)


def _norm_matmul(x, g, w_t, *, tm, name, carry=None):
    n, d = x.shape
    c = w_t.shape[0]
    ch = 512

    def body(x_ref, g_ref, wt_ref, h_ref, z_ref):
        xv = x_ref[...]
        r = lax.rsqrt(jnp.mean(xv * xv, axis=-1, keepdims=True) + RMS_EPS)
        h = (xv * r * g_ref[...]).astype(BF16)
        h_ref[...] = h
        for c0 in range(0, c, ch):
            z_ref[:, c0:c0 + ch] = _dot_nt(h, wt_ref[c0:c0 + ch, :]).astype(BF16)

    return _call(
        body, name=name, grid=(n // tm,), carry=carry,
        in_specs=[pl.BlockSpec((tm, d), lambda i: (i, 0)),
                  pl.BlockSpec((1, d), lambda i: (0, 0)),
                  pl.BlockSpec((c, d), lambda i: (0, 0))],
        out_specs=[pl.BlockSpec((tm, d), lambda i: (i, 0)),
                   pl.BlockSpec((tm, c), lambda i: (i, 0))],
        out_shape=[jax.ShapeDtypeStruct((n, d), BF16), jax.ShapeDtypeStruct((n, c), BF16)],
        args=(x, g.reshape(1, d), w_t))


def _mix_forward(z, x, b_gate, ln_g, ln_b, w_s, b_s, w_sc, wb, w_out, *, tm, name, carry=None):
    n = z.shape[0]
    hb = tm // HALO

    def body(z_ref, zp_ref, x_ref, bg_ref, lng_ref, lnb_ref, ws_ref, bs_ref, wsc_ref, wb_ref, wo_ref,
             ya_ref, yb_ref, cv_ref, pa_ref, pb_ref, mg_ref, x1_ref, f_scr):
        i = pl.program_id(0)
        u = z_ref[:, OFF_U:OFF_U + D_A]
        v = z_ref[:, OFF_V:OFF_V + D_A].astype(F32)
        gu, _, _, _, _, _, f = _gmlp_forward(u, v, lng_ref[...], lnb_ref[...], ws_ref, bs_ref, f_scr)
        ya = gu * f
        ya_ref[...] = ya

        q = z_ref[:, OFF_CG:OFF_CG + D_B] * z_ref[:, OFF_HB:OFF_HB + D_B]
        qp = zp_ref[:, OFF_CG:OFF_CG + D_B] * zp_ref[:, OFF_HB:OFF_HB + D_B]
        qp = jnp.where(i > 0, qp, jnp.zeros_like(qp))
        w = wsc_ref[...].astype(BF16)
        conv = w[0:1] * _shift_down(q, 2, qp) + w[1:2] * _shift_down(q, 1, qp) + w[2:3] * q
        cv_ref[...] = conv
        yb = z_ref[:, OFF_BG:OFF_BG + D_B] * conv
        yb_ref[...] = yb

        pa = _dot(ya, wb_ref[0]).astype(BF16)
        pb = _dot(yb, wb_ref[1]).astype(BF16)
        pa_ref[...] = pa
        pb_ref[...] = pb
        bg = bg_ref[...].astype(BF16)
        sa = _sigmoid(z_ref[:, OFF_GA:OFF_GA + D_MODEL] + bg[:, 0:D_MODEL])
        sb = _sigmoid(z_ref[:, OFF_GB:OFF_GB + D_MODEL] + bg[:, D_MODEL:2 * D_MODEL])
        mg = sa * pa + sb * pb
        mg_ref[...] = mg
        x1_ref[...] = x_ref[...] + _dot(mg, wo_ref[...])

    row = lambda w: pl.BlockSpec((tm, w), lambda i: (i, 0))
    full = lambda *s: pl.BlockSpec(s, lambda i: (0,) * len(s))
    bf = lambda w: jax.ShapeDtypeStruct((n, w), BF16)
    return _call(
        body, name=name, grid=(n // tm,), carry=carry,
        in_specs=[row(D_IN),
                  pl.BlockSpec((HALO, D_IN), lambda i: (jnp.maximum(i * hb - 1, 0), 0)),
                  row(D_MODEL), full(1, 2 * D_MODEL), full(1, D_A), full(1, D_A),
                  full(N_HEADS, GMLP_BLOCK, GMLP_BLOCK), full(N_HEADS, GMLP_BLOCK, 1), full(3, D_B),
                  full(2, D_A, D_MODEL), full(D_MODEL, D_MODEL)],
        out_specs=[row(D_A), row(D_B), row(D_B), row(D_MODEL), row(D_MODEL), row(D_MODEL), row(D_MODEL)],
        out_shape=[bf(D_A), bf(D_B), bf(D_B), bf(D_MODEL), bf(D_MODEL), bf(D_MODEL),
                   jax.ShapeDtypeStruct((n, D_MODEL), F32)],
        scratch_shapes=[pltpu.VMEM((tm, D_A), BF16)],
        args=(z, z, x, b_gate.reshape(1, -1), ln_g.reshape(1, -1), ln_b.reshape(1, -1), w_s,
              b_s.reshape(N_HEADS, GMLP_BLOCK, 1), w_sc, wb, w_out))


def _loss_tile(xv, gv, tv):
    d = xv.shape[-1]
    r = lax.rsqrt(jnp.mean(xv * xv, axis=-1, keepdims=True) + RMS_EPS)
    xh = xv * r
    e = xh * gv - tv
    per_row = jnp.sum(e * e, axis=-1, keepdims=True) * (0.5 / d)
    dy = e * (1.0 / d)
    dxh = dy * gv
    dx = r * (dxh - xh * jnp.mean(dxh * xh, axis=-1, keepdims=True))
    return dx, jnp.sum(per_row, axis=0, keepdims=True), jnp.sum(dy * xh, axis=0, keepdims=True)


def _ffn_forward(up, x1, w_fc, b_fc, w_down, *, tm, name, carry=None, head=None):
    n = up.shape[0]
    hb = tm // HALO
    n_in = 6 if head is None else 8

    def body(*refs):
        up_ref, upp_ref, x1_ref, wfc_ref, bfc_ref, wd_ref = refs[:6]
        gc_ref, a_ref, out_ref = refs[n_in:n_in + 3]
        acc = refs[-1]
        i = pl.program_id(0)
        acc[...] = x1_ref[...]
        for c0 in range(0, D_FF, FFN_CHUNK):
            cols = slice(c0, c0 + FFN_CHUNK)
            gate = up_ref[:, cols]
            val = up_ref[:, D_FF + c0:D_FF + c0 + FFN_CHUNK]
            gp = upp_ref[:, cols]
            gp = jnp.where(i > 0, gp, jnp.zeros_like(gp))
            w = wfc_ref[:, cols].astype(BF16)
            gc = (w[0:1] * _shift_down(gate, 2, gp) + w[1:2] * _shift_down(gate, 1, gp) + w[2:3] * gate
                  + bfc_ref[:, cols].astype(BF16))
            gc_ref[:, cols] = gc
            a = gc * _sigmoid(gc) * val
            a_ref[:, cols] = a
            acc[...] += _dot(a, wd_ref[cols, :])
        if head is None:
            out_ref[...] = acc[...]
        else:
            g_ref, t_ref = refs[6:8]
            sg_ref = refs[n_in + 3]

            @pl.when(i == 0)
            def _():
                sg_ref[...] = jnp.zeros_like(sg_ref)

            dx, loss, dg = _loss_tile(acc[...], g_ref[...], t_ref[...])
            out_ref[...] = dx
            sg_ref[ROW_LOSS:ROW_LOSS + 1, 0:128] += jnp.broadcast_to(loss, (1, 128))
            sg_ref[ROW_FINAL:ROW_FINAL + 1, 0:D_MODEL] += dg

    row = lambda w: pl.BlockSpec((tm, w), lambda i: (i, 0))
    full = lambda r, c: pl.BlockSpec((r, c), lambda i: (0, 0))
    in_specs = [row(2 * D_FF), pl.BlockSpec((HALO, D_FF), lambda i: (jnp.maximum(i * hb - 1, 0), 0)), row(D_MODEL),
                full(3, D_FF), full(1, D_FF), full(D_FF, D_MODEL)]
    out_specs = [row(D_FF), row(D_FF), row(D_MODEL)]
    out_shape = [jax.ShapeDtypeStruct((n, D_FF), BF16), jax.ShapeDtypeStruct((n, D_FF), BF16),
                 jax.ShapeDtypeStruct((n, D_MODEL), F32)]
    args = (up, up, x1, w_fc, b_fc.reshape(1, -1), w_down)
    if head is not None:
        in_specs += [full(1, D_MODEL), row(D_MODEL)]
        out_specs += [full(SG_ROWS, SG_W)]
        out_shape += [jax.ShapeDtypeStruct((SG_ROWS, SG_W), F32)]
        args += (head[0].reshape(1, -1), head[1])
    return _call(body, name=name, grid=(n // tm,), carry=carry, in_specs=in_specs, out_specs=out_specs,
                 out_shape=out_shape, scratch_shapes=[pltpu.VMEM((tm, D_MODEL), F32)], args=args)


def _ffn_backward(dx2, up, gc, w_fc, w_down, sheet, layer, *, tm, name, carry=None):
    n = up.shape[0]
    steps = n // tm
    hb = tm // HALO
    row = SG_LAYER * layer + ROW_FCONV

    def body(dx_ref, dxn_ref, up_ref, upn_ref, gc_ref, gcn_ref, wfc_ref, wd_ref, sg_in, dup_ref, sg_ref):
        i = pl.program_id(0)
        last = i == steps - 1
        _sheet_begin(i, sg_in, sg_ref, row, 4)

        dxe = jnp.concatenate([dx_ref[...], dxn_ref[...]], axis=0).astype(BF16)
        for c0 in range(0, D_FF, FFN_CHUNK):
            cols = slice(c0, c0 + FFN_CHUNK)
            vcols = slice(D_FF + c0, D_FF + c0 + FFN_CHUNK)
            dae = _dot_nt(dxe, wd_ref[cols, :])
            da, dan = dae[:tm], dae[tm:]
            gate = up_ref[:, cols]
            val = up_ref[:, vcols]
            gcv = gc_ref[:, cols]
            s = _sigmoid(gcv)
            dab = da.astype(BF16)
            dup_ref[:, vcols] = dab * (gcv * s)
            dgc = dab * val * (s * (1.0 + gcv * (1.0 - s)))
            gcn = gcn_ref[:, cols]
            sn = _sigmoid(gcn)
            dgcn = dan.astype(BF16) * upn_ref[:, vcols] * (sn * (1.0 + gcn * (1.0 - sn)))
            dgcn = jnp.where(last, jnp.zeros_like(dgcn), dgcn)
            up1 = _shift_up(dgc, 1, dgcn)
            up2 = _shift_up(dgc, 2, dgcn)
            w = wfc_ref[:, cols].astype(BF16)
            dup_ref[:, cols] = w[2:3] * dgc + w[1:2] * up1 + w[0:1] * up2
            sg_ref[row:row + 1, cols] += _column_sums(gate * up2)
            sg_ref[row + 1:row + 2, cols] += _column_sums(gate * up1)
            sg_ref[row + 2:row + 3, cols] += _column_sums(gate * dgc)
            sg_ref[row + 3:row + 4, cols] += _column_sums(dgc)

    nxt = lambda i: (jnp.minimum((i + 1) * hb, steps * hb - 1), 0)
    return _call(
        body, name=name, grid=(steps,), carry=carry,
        in_specs=[pl.BlockSpec((tm, D_MODEL), lambda i: (i, 0)),
                  pl.BlockSpec((HALO, D_MODEL), nxt),
                  pl.BlockSpec((tm, 2 * D_FF), lambda i: (i, 0)),
                  pl.BlockSpec((HALO, 2 * D_FF), nxt),
                  pl.BlockSpec((tm, D_FF), lambda i: (i, 0)),
                  pl.BlockSpec((HALO, D_FF), nxt),
                  pl.BlockSpec((3, D_FF), lambda i: (0, 0)),
                  pl.BlockSpec((D_FF, D_MODEL), lambda i: (0, 0)), _sheet_spec()],
        out_specs=[pl.BlockSpec((tm, 2 * D_FF), lambda i: (i, 0)), _sheet_spec()],
        out_shape=[jax.ShapeDtypeStruct((n, 2 * D_FF), BF16), jax.ShapeDtypeStruct((SG_ROWS, SG_W), F32)],
        args=(dx2, dx2, up, up, gc, gc, w_fc, w_down, sheet))


def _matmul_norm_backward(dz, w_t, x, g, dres, sheet, row, *, tm, name, carry=None):
    n, c = dz.shape
    d = x.shape[1]
    ch = 512

    def body(dz_ref, wt_ref, x_ref, g_ref, dres_ref, sg_in, dx_ref, sg_ref):
        i = pl.program_id(0)
        _sheet_begin(i, sg_in, sg_ref, row, 1)

        dh = _dot(dz_ref[:, 0:ch], wt_ref[0:ch, :])
        for c0 in range(ch, c, ch):
            dh += _dot(dz_ref[:, c0:c0 + ch], wt_ref[c0:c0 + ch, :])
        xv = x_ref[...]
        r = lax.rsqrt(jnp.mean(xv * xv, axis=-1, keepdims=True) + RMS_EPS)
        xh = xv * r
        sg_ref[row:row + 1, 0:d] += jnp.sum(dh * xh, axis=0, keepdims=True)
        dxh = dh * g_ref[...]
        dx_ref[...] = dres_ref[...] + r * (dxh - xh * jnp.mean(dxh * xh, axis=-1, keepdims=True))

    return _call(
        body, name=name, grid=(n // tm,), carry=carry,
        in_specs=[pl.BlockSpec((tm, c), lambda i: (i, 0)),
                  pl.BlockSpec((c, d), lambda i: (0, 0)),
                  pl.BlockSpec((tm, d), lambda i: (i, 0)),
                  pl.BlockSpec((1, d), lambda i: (0, 0)),
                  pl.BlockSpec((tm, d), lambda i: (i, 0)), _sheet_spec()],
        out_specs=[pl.BlockSpec((tm, d), lambda i: (i, 0)), _sheet_spec()],
        out_shape=[jax.ShapeDtypeStruct((n, d), F32), jax.ShapeDtypeStruct((SG_ROWS, SG_W), F32)],
        args=(dz, w_t, x, g.reshape(1, d), dres, sheet))


def _mix_backward(dx1, z, conv, pa, pb, b_gate, ln_g, ln_b, w_s, w_s_t, b_s, w_sc, w_out, wb, sheet, layer, *, tm, name,
                  carry=None):
    n = z.shape[0]
    steps = n // tm
    hb = tm // HALO
    base = SG_LAYER * layer
    r_bg, r_lng, r_lnb, r_sc = base + ROW_BGATE, base + ROW_LN_G, base + ROW_LN_B, base + ROW_SCONV

    def body(dx_ref, dxn_ref, z_ref, zn_ref, cv_ref, pa_ref, pb_ref, bg_ref, lng_ref, lnb_ref, ws_ref, wst_ref,
             bs_ref, wsc_ref, wo_ref, wb_ref, sg_in,
             dz_ref, dpa_ref, dpb_ref, dws_ref, dbs_ref, sg_ref, f_scr, dvn_scr):
        i = pl.program_id(0)
        last = i == steps - 1
        _sheet_begin(i, sg_in, sg_ref, r_bg, ROW_NORM2 - ROW_BGATE)

        @pl.when(i == 0)
        def _():
            dws_ref[...] = jnp.zeros_like(dws_ref)
            dbs_ref[...] = jnp.zeros_like(dbs_ref)

        dxe = jnp.concatenate([dx_ref[...], dxn_ref[...]], axis=0).astype(BF16)
        dmge = _dot_nt(dxe, wo_ref[...])
        dmg, dmgn = dmge[:tm].astype(BF16), dmge[tm:].astype(BF16)

        pa_v = pa_ref[...]
        pb_v = pb_ref[...]
        bg = bg_ref[...].astype(BF16)
        sa = _sigmoid(z_ref[:, OFF_GA:OFF_GA + D_MODEL] + bg[:, 0:D_MODEL])
        sb = _sigmoid(z_ref[:, OFF_GB:OFF_GB + D_MODEL] + bg[:, D_MODEL:2 * D_MODEL])
        dpa = dmg * sa
        dpb = dmg * sb
        dga = dmg * pa_v * sa * (1.0 - sa)
        dgb = dmg * pb_v * sb * (1.0 - sb)
        dpa_ref[...] = dpa
        dpb_ref[...] = dpb
        dz_ref[:, OFF_GA:OFF_GA + D_MODEL] = dga
        dz_ref[:, OFF_GB:OFF_GB + D_MODEL] = dgb
        sg_ref[r_bg:r_bg + 1, 0:D_MODEL] += _column_sums(dga)
        sg_ref[r_bg:r_bg + 1, D_MODEL:2 * D_MODEL] += _column_sums(dgb)

        dya = _dot_nt(dpa, wb_ref[0]).astype(BF16)
        u = z_ref[:, OFF_U:OFF_U + D_A]
        v = z_ref[:, OFF_V:OFF_V + D_A].astype(F32)
        ln_g = lng_ref[...]
        gu, tu, tv, xh, rstd, vn, f = _gmlp_forward(u, v, ln_g, lnb_ref[...], ws_ref, bs_ref, f_scr)
        dgu = dya * f
        df_bf = dya * gu
        dz_ref[:, OFF_U:OFF_U + D_A] = dgu * _gelu_grad(u, tu)
        mask = _spatial_mask(False)
        mask_t = _spatial_mask(True)
        wmt = [jnp.where(mask_t, wst_ref[h], 0.0).astype(BF16) for h in range(N_HEADS)]
        for b in range(tm // GMLP_BLOCK):
            rows = slice(b * GMLP_BLOCK, (b + 1) * GMLP_BLOCK)
            for h in range(N_HEADS):
                cols = slice(h * HEAD, (h + 1) * HEAD)
                dfb = df_bf[rows, cols]
                dvn_scr[rows, cols] = _dot(wmt[h], dfb)
                dws_ref[h] += jnp.where(mask, _dot_nt(dfb, vn[rows, cols]), 0.0)
                dbs_ref[h] += jnp.sum(dfb.astype(F32), axis=1, keepdims=True)
        dvn = dvn_scr[...]
        sg_ref[r_lng:r_lng + 1, 0:D_A] += jnp.sum(dvn * xh, axis=0, keepdims=True)
        sg_ref[r_lnb:r_lnb + 1, 0:D_A] += jnp.sum(dvn, axis=0, keepdims=True)
        dxh = dvn * ln_g
        dgv = rstd * (dxh - jnp.mean(dxh, axis=-1, keepdims=True) - xh * jnp.mean(dxh * xh, axis=-1, keepdims=True))
        dz_ref[:, OFF_V:OFF_V + D_A] = (dgv * _gelu_grad(v, tv)).astype(BF16)

        sbn = _sigmoid(zn_ref[:, OFF_GB:OFF_GB + D_MODEL] + bg[:, D_MODEL:2 * D_MODEL])
        dpbe = jnp.concatenate([dpb, dmgn * sbn], axis=0)
        dybe = _dot_nt(dpbe, wb_ref[1])
        dyb, dybn = dybe[:tm].astype(BF16), dybe[tm:].astype(BF16)
        bgv = z_ref[:, OFF_BG:OFF_BG + D_B]
        cg = z_ref[:, OFF_CG:OFF_CG + D_B]
        hbv = z_ref[:, OFF_HB:OFF_HB + D_B]
        q = cg * hbv
        dz_ref[:, OFF_BG:OFF_BG + D_B] = dyb * cv_ref[...]
        dconv = dyb * bgv
        dconvn = dybn * zn_ref[:, OFF_BG:OFF_BG + D_B]
        dconvn = jnp.where(last, jnp.zeros_like(dconvn), dconvn)
        up1 = _shift_up(dconv, 1, dconvn)
        up2 = _shift_up(dconv, 2, dconvn)
        sg_ref[r_sc:r_sc + 1, 0:D_B] += _column_sums(q * up2)
        sg_ref[r_sc + 1:r_sc + 2, 0:D_B] += _column_sums(q * up1)
        sg_ref[r_sc + 2:r_sc + 3, 0:D_B] += _column_sums(q * dconv)
        w = wsc_ref[...].astype(BF16)
        dq = w[2:3] * dconv + w[1:2] * up1 + w[0:1] * up2
        dz_ref[:, OFF_CG:OFF_CG + D_B] = dq * hbv
        dz_ref[:, OFF_HB:OFF_HB + D_B] = dq * cg

    row = lambda w: pl.BlockSpec((tm, w), lambda i: (i, 0))
    full = lambda *s: pl.BlockSpec(s, lambda i: (0,) * len(s))
    nxt = lambda i: (jnp.minimum((i + 1) * hb, steps * hb - 1), 0)
    return _call(
        body, name=name, grid=(steps,), carry=carry,
        in_specs=[row(D_MODEL), pl.BlockSpec((HALO, D_MODEL), nxt),
                  row(D_IN), pl.BlockSpec((HALO, D_IN), nxt),
                  row(D_B), row(D_MODEL), row(D_MODEL),
                  full(1, 2 * D_MODEL), full(1, D_A), full(1, D_A),
                  full(N_HEADS, GMLP_BLOCK, GMLP_BLOCK), full(N_HEADS, GMLP_BLOCK, GMLP_BLOCK),
                  full(N_HEADS, GMLP_BLOCK, 1), full(3, D_B),
                  full(D_MODEL, D_MODEL), full(2, D_A, D_MODEL), _sheet_spec()],
        out_specs=[row(D_IN), row(D_MODEL), row(D_MODEL), full(N_HEADS, GMLP_BLOCK, GMLP_BLOCK),
                   full(N_HEADS, GMLP_BLOCK, 1), _sheet_spec()],
        out_shape=[jax.ShapeDtypeStruct((n, D_IN), BF16), jax.ShapeDtypeStruct((n, D_MODEL), BF16),
                   jax.ShapeDtypeStruct((n, D_MODEL), BF16),
                   jax.ShapeDtypeStruct((N_HEADS, GMLP_BLOCK, GMLP_BLOCK), F32),
                   jax.ShapeDtypeStruct((N_HEADS, GMLP_BLOCK, 1), F32), jax.ShapeDtypeStruct((SG_ROWS, SG_W), F32)],
        scratch_shapes=[pltpu.VMEM((tm, D_A), BF16), pltpu.VMEM((tm, D_A), F32)],
        args=(dx1, dx1, z, z, conv, pa, pb, b_gate.reshape(1, -1), ln_g.reshape(1, -1), ln_b.reshape(1, -1), w_s, w_s_t,
              b_s.reshape(N_HEADS, GMLP_BLOCK, 1), w_sc, w_out, wb, sheet))


def _matmul_tn(a, b, *, t1, tn, name, carry=None, pieces=1):
    n, k1 = a.shape
    k2 = b.shape[1]
    steps = n // tn
    w = k2 // pieces

    def body(a_ref, b_ref, *rest):
        o_refs, acc = rest[:pieces], rest[pieces]
        s = pl.program_id(1)

        @pl.when(s == 0)
        def _():
            acc[...] = jnp.zeros_like(acc)

        acc[...] += lax.dot_general(a_ref[...].astype(BF16), b_ref[...].astype(BF16), TN, preferred_element_type=F32)

        @pl.when(s == steps - 1)
        def _():
            for c, o_ref in enumerate(o_refs):
                o_ref[...] = acc[:, c * w:(c + 1) * w].astype(BF16)

    outs, carried = _call(
        body, name=name, grid=(k1 // t1, steps), carry=carry,
        in_specs=[pl.BlockSpec((tn, t1), lambda i, s: (s, i)),
                  pl.BlockSpec((tn, k2), lambda i, s: (s, 0))],
        out_specs=[pl.BlockSpec((t1, w), lambda i, s: (i, 0))] * pieces,
        out_shape=[jax.ShapeDtypeStruct((k1, w), BF16)] * pieces,
        scratch_shapes=[pltpu.VMEM((t1, k2), F32)],
        args=(a, b))
    return (outs[0] if pieces == 1 else list(outs)), carried


def _adamw_math(w, g, m, v):
    m = ADAM_B1 * m + (1.0 - ADAM_B1) * g
    v = ADAM_B2 * v + (1.0 - ADAM_B2) * (g * g)
    m_hat = m / (1.0 - ADAM_B1 ** ADAM_STEP)
    v_hat = v / (1.0 - ADAM_B2 ** ADAM_STEP)
    delta = -ADAM_LR * (m_hat / (jnp.sqrt(v_hat) + ADAM_EPS) + ADAM_WD * w)
    return delta, m, v


def _sum_parts(recvs, *, tr, name):
    _, r, c = recvs[0].shape

    def body(*refs):
        recv_refs, g_ref = refs[:DEPTH], refs[DEPTH]
        layer = pl.program_id(0)
        for l in range(DEPTH):
            @pl.when(layer == l)
            def _(l=l):
                g = recv_refs[l][0].astype(F32)
                for s in range(1, N_DEV):
                    g = g + recv_refs[l][s].astype(F32)
                g_ref[0] = g

    outs, _ = _call(
        body, name=name, grid=(DEPTH, r // tr),
        in_specs=[pl.BlockSpec((N_DEV, tr, c), lambda l, i: (0, i, 0))] * DEPTH,
        out_specs=[pl.BlockSpec((1, tr, c), lambda l, i: (l, i, 0))],
        out_shape=[jax.ShapeDtypeStruct((DEPTH, r, c), F32)],
        args=tuple(recvs))
    return outs[0]


def _adamw(w, g, m, v, *, tr, name):
    r, c = w.shape

    def body(w_ref, g_ref, m_ref, v_ref, d_ref, nm_ref, nv_ref):
        delta, nm, nv = _adamw_math(w_ref[...], g_ref[...], m_ref[...], v_ref[...])
        d_ref[...] = delta
        nm_ref[...] = nm
        nv_ref[...] = nv

    spec = pl.BlockSpec((tr, c), lambda i: (i, 0))
    outs, _ = _call(body, name=name, grid=(r // tr,), in_specs=[spec] * 4, out_specs=[spec] * 3,
                    out_shape=[jax.ShapeDtypeStruct((r, c), F32)] * 3, args=(w, g, m, v))
    return outs


def _sum_adamw(recvs, w, m, v, *, tr, name):
    _, r, c = w.shape
    blocks = len(recvs[0])
    flat = [piece for layer in recvs for piece in layer]

    def body(*refs):
        recv_refs = refs[:len(flat)]
        w_ref, m_ref, v_ref, g_ref, d_ref, nm_ref, nv_ref = refs[len(flat):]
        layer = pl.program_id(0)
        for l in range(DEPTH):
            @pl.when(layer == l)
            def _(l=l):
                cols = []
                for piece in recv_refs[l * blocks:(l + 1) * blocks]:
                    part = piece[0].astype(F32)
                    for s in range(1, N_DEV):
                        part = part + piece[s].astype(F32)
                    cols.append(part)
                g = cols[0] if blocks == 1 else jnp.concatenate(cols, axis=-1)
                delta, nm, nv = _adamw_math(w_ref[0], g, m_ref[0], v_ref[0])
                g_ref[0] = g
                d_ref[0] = delta
                nm_ref[0] = nm
                nv_ref[0] = nv

    spec = pl.BlockSpec((1, tr, c), lambda l, i: (l, i, 0))
    outs, _ = _call(
        body, name=name, grid=(DEPTH, r // tr),
        in_specs=[pl.BlockSpec((N_DEV, tr, c // blocks), lambda l, i: (0, i, 0))] * len(flat) + [spec] * 3,
        out_specs=[spec] * 4, out_shape=[jax.ShapeDtypeStruct((DEPTH, r, c), F32)] * 4,
        args=tuple(flat) + (w, m, v))
    return outs


def _adamw_small(sheet, extra, params, *, name):
    sheet_rows = dict(norm1_g=ROW_NORM1, b_gate=ROW_BGATE, gmlp_ln_g=ROW_LN_G, gmlp_ln_b=ROW_LN_B, norm2_g=ROW_NORM2,
                      b_ffn_conv=ROW_BFCONV)
    names = list(params)
    extra_names = list(extra)

    def body(*refs):
        sg_ref, refs = refs[0], refs[1:]
        extra_refs, refs = dict(zip(extra_names, refs[:len(extra_names)])), refs[len(extra_names):]
        ins, outs = refs[:3 * len(names)], refs[3 * len(names):]
        for j, key in enumerate(names):
            w_ref, m_ref, v_ref = ins[3 * j:3 * j + 3]
            g_ref, d_ref, nm_ref, nv_ref = outs[4 * j:4 * j + 4]
            if key in extra_refs:
                g_ref[...] = extra_refs[key][...]
            elif key == "final_g":
                g_ref[...] = sg_ref[ROW_FINAL:ROW_FINAL + 1, 0:D_MODEL]
            else:
                width = w_ref.shape[-1]
                for l in range(DEPTH):
                    row = SG_LAYER * l + sheet_rows[key]
                    g_ref[l:l + 1, :] = sg_ref[row:row + 1, 0:width]
            delta, nm, nv = _adamw_math(w_ref[...], g_ref[...], m_ref[...], v_ref[...])
            d_ref[...] = delta
            nm_ref[...] = nm
            nv_ref[...] = nv

    args = [sheet] + [extra[k] for k in extra_names] + [t for k in names for t in params[k]]
    vmem = pl.BlockSpec(memory_space=pltpu.VMEM)
    outs = pl.pallas_call(
        body, name=name, in_specs=[vmem] * len(args), out_specs=[vmem] * (4 * len(names)),
        out_shape=[jax.ShapeDtypeStruct(params[k][0].shape, F32) for k in names for _ in range(4)],
    )(*args)
    return {k: tuple(outs[4 * j:4 * j + 4]) for j, k in enumerate(names)}


def _rows(gathered):
    return gathered.reshape(N_DEV * gathered.shape[1], gathered.shape[2])


def _parts(full):
    return full.reshape(N_DEV, full.shape[0] // N_DEV, full.shape[1])


def kernel(x, norm1_g, w_in, b_gate, gmlp_ln_g, gmlp_ln_b, w_spatial, b_spatial, w_shortconv, w_branch, w_out, norm2_g, w_ffn_up, w_ffn_conv, b_ffn_conv, w_ffn_down, final_g, loss_target, m_norm1_g, m_w_in, m_b_gate, m_gmlp_ln_g, m_gmlp_ln_b, m_w_spatial, m_b_spatial, m_w_shortconv, m_w_branch, m_w_out, m_norm2_g, m_w_ffn_up, m_w_ffn_conv, m_b_ffn_conv, m_w_ffn_down, m_final_g, v_norm1_g, v_w_in, v_b_gate, v_gmlp_ln_g, v_gmlp_ln_b, v_w_spatial, v_b_spatial, v_w_shortconv, v_w_branch, v_w_out, v_norm2_g, v_w_ffn_up, v_w_ffn_conv, v_b_ffn_conv, v_w_ffn_down, v_final_g):
    n = x.shape[1]
    tm, tm_mix, tn = 512, 512, 1024
    x0 = x.reshape(n, D_MODEL)
    target = loss_target.reshape(n, D_MODEL)
    my_idx = 4 * lax.axis_index("x") + 2 * lax.axis_index("y") + lax.axis_index("c")
    sc_w, fc_w = D_B // N_DEV, D_FF // N_DEV

    sh_in = [w_in[l].T.astype(BF16) for l in range(DEPTH)]
    sh_up = [w_ffn_up[l].T.astype(BF16) for l in range(DEPTH)]
    sh_br = [w_branch[l].astype(BF16) for l in range(DEPTH)]
    sh_out = [w_out[l].astype(BF16) for l in range(DEPTH)]
    sh_down = [w_ffn_down[l].astype(BF16) for l in range(DEPTH)]
    taps = jnp.concatenate([w_shortconv, w_ffn_conv], axis=-1)

    def branch_weights(g):
        return g.transpose(1, 2, 0, 3).reshape(2, D_A, D_MODEL)

    g_in0, g_taps = _gather_now([sh_in[0], taps], name="gather_first")
    w_sc = [g_taps[:, l, :, :sc_w].transpose(1, 0, 2).reshape(3, D_B) for l in range(DEPTH)]
    w_fc = [g_taps[:, l, :, sc_w:].transpose(1, 0, 2).reshape(3, D_FF) for l in range(DEPTH)]
    w_s_t = [w_spatial[l].transpose(0, 2, 1) for l in range(DEPTH)]
    weights = [dict(), dict()]
    weights[0]["in_t"] = _rows(g_in0)
    saved = []
    xc = x0
    for l in range(DEPTH):
        p = weights[l]
        carry = _Gather([sh_br[0], sh_out[0]] if l == 0 else [sh_up[1]])
        (h, z), got = _norm_matmul(xc, norm1_g[l], p["in_t"], tm=tm, name=f"fwd_in_{l}", carry=carry)
        if l == 0:
            p["wb"], p["out"] = branch_weights(got[0]), _rows(got[1])
        else:
            p["up_t"] = _rows(got[0])
        carry = _Gather([sh_up[0]]) if l == 0 else None
        (ya, yb, conv, pa, pb, mg, x1), got = _mix_forward(
            z, xc, b_gate[l], gmlp_ln_g[l], gmlp_ln_b[l], w_spatial[l], b_spatial[l], w_sc[l], p["wb"], p["out"],
            tm=tm_mix, name=f"fwd_mix_{l}", carry=carry)
        if l == 0:
            p["up_t"] = _rows(got[0])
        carry = _Gather([sh_down[0], sh_in[1]] if l == 0 else [sh_down[1]])
        (h2, up), got = _norm_matmul(x1, norm2_g[l], p["up_t"], tm=tm, name=f"fwd_up_{l}", carry=carry)
        if l == 0:
            p["down"], weights[1]["in_t"] = _rows(got[0]), _rows(got[1])
        else:
            p["down"] = _rows(got[0])
        carry = _Gather([sh_br[1], sh_out[1]]) if l == 0 else None
        head = (final_g, target) if l == DEPTH - 1 else None
        outs, got = _ffn_forward(up, x1, w_fc[l], b_ffn_conv[l], p["down"], tm=tm, name=f"fwd_ffn_{l}", carry=carry, head=head)
        if l == 0:
            weights[1]["wb"], weights[1]["out"] = branch_weights(got[0]), _rows(got[1])
        gc, a = outs[0], outs[1]
        saved.append(dict(x=xc, h=h, z=z, ya=ya, yb=yb, conv=conv, pa=pa, pb=pb, mg=mg, x1=x1, h2=h2, up=up, gc=gc, a=a))
        xc = outs[2]
    dx, sheet = outs[2], outs[3]

    recv = [dict(), dict()]
    small_dws, small_dbs = [None] * DEPTH, [None] * DEPTH
    pending_in = None
    for l in reversed(range(DEPTH)):
        p, s = weights[l], saved[l]
        carry = _Exchange([pending_in]) if pending_in is not None else None
        (dup, sheet), got = _ffn_backward(dx, s["up"], s["gc"], w_fc[l], p["down"], sheet, l, tm=tm, name=f"bwd_ffn_{l}",
                                          carry=carry)
        if got is not None:
            recv[l + 1]["in_t"] = got[0]
        dw_down, _ = _matmul_tn(s["a"], dx, t1=D_FF // 2, tn=tn, name=f"dw_down_{l}")
        dw_up_t, got = _matmul_tn(dup, s["h2"], t1=2 * D_FF // 4, tn=tn, name=f"dw_up_{l}", pieces=2,
                                  carry=_Exchange([_parts(dw_down)]))
        recv[l]["down"] = got[0]
        (dx1, sheet), got_left = _matmul_norm_backward(
            dup, p["up_t"], s["x1"], norm2_g[l], dx, sheet, SG_LAYER * l + ROW_NORM2, tm=tm, name=f"bwd_up_{l}",
            carry=_Exchange([_parts(dw_up_t[0])]))
        (dz, dpa, dpb, small_dws[l], small_dbs[l], sheet), got_right = _mix_backward(
            dx1, s["z"], s["conv"], s["pa"], s["pb"], b_gate[l], gmlp_ln_g[l], gmlp_ln_b[l], w_spatial[l], w_s_t[l],
            b_spatial[l], w_sc[l], p["out"], p["wb"], sheet, l, tm=tm_mix, name=f"bwd_mix_{l}",
            carry=_Exchange([_parts(dw_up_t[1])]))
        recv[l]["up_t"] = [got_left[0], got_right[0]]
        dw_out, _ = _matmul_tn(s["mg"], dx1, t1=D_MODEL, tn=tn, name=f"dw_out_{l}")
        dw_bra_t, _ = _matmul_tn(dpa, s["ya"], t1=D_MODEL, tn=tn, name=f"dw_branch_a_{l}")
        dw_brb_t, _ = _matmul_tn(dpb, s["yb"], t1=D_MODEL, tn=tn, name=f"dw_branch_b_{l}")
        dw_in_t, got = _matmul_tn(dz, s["h"], t1=D_IN // 4, tn=tn, name=f"dw_in_{l}",
                                  carry=_Exchange([_parts(dw_out), _parts(dw_bra_t), _parts(dw_brb_t)]))
        recv[l]["out"], recv[l]["bra_t"], recv[l]["brb_t"] = got
        carry = _Exchange([_parts(dw_in_t)]) if l == 0 else None
        (dx0, sheet), got = _matmul_norm_backward(dz, p["in_t"], s["x"], norm1_g[l], dx1, sheet, SG_LAYER * l + ROW_NORM1,
                                                  tm=tm, name=f"bwd_in_{l}", carry=carry)
        if l == 0:
            recv[0]["in_t"] = got[0]
        else:
            pending_in = _parts(dw_in_t)
        dx = dx0
    grad_x = dx.reshape(x.shape)

    results = {}
    both = lambda key: [recv[l][key] for l in range(DEPTH)]
    blocks = lambda key: [r if isinstance(r, list) else [r] for r in both(key)]
    swap = lambda t: t.transpose(0, 2, 1)
    for key, slab, (w, m, v), tr in [("w_in", "in_t", (w_in, m_w_in, v_w_in), 192),
                                     ("w_ffn_up", "up_t", (w_ffn_up, m_w_ffn_up, v_w_ffn_up), 176)]:
        outs = _sum_adamw(blocks(slab), swap(w), swap(m), swap(v), tr=tr, name=f"adamw_{key}")
        results[key] = tuple(swap(o) for o in outs)
    g_bra = _sum_parts(both("bra_t"), tr=128, name="sum_w_branch_a").transpose(0, 2, 1)
    g_brb = _sum_parts(both("brb_t"), tr=128, name="sum_w_branch_b").transpose(0, 2, 1)
    g_br = jnp.stack([g_bra, g_brb], axis=1)
    flat = lambda t: t.reshape(-1, t.shape[-1])
    outs = _adamw(flat(w_branch), flat(g_br), flat(m_w_branch), flat(v_w_branch), tr=512, name="adamw_w_branch")
    results["w_branch"] = (g_br,) + tuple(o.reshape(w_branch.shape) for o in outs)
    results["w_out"] = tuple(_sum_adamw(blocks("out"), w_out, m_w_out, v_w_out, tr=128, name="adamw_w_out"))
    results["w_ffn_down"] = tuple(_sum_adamw(blocks("down"), w_ffn_down, m_w_ffn_down, v_w_ffn_down, tr=176,
                                             name="adamw_w_ffn_down"))

    dbs = jnp.stack([small_dbs[l].reshape(N_HEADS, GMLP_BLOCK) for l in range(DEPTH)]).reshape(DEPTH * N_HEADS, GMLP_BLOCK)
    sheet, dws0, dws1, dbs = _all_reduce_small([sheet, small_dws[0], small_dws[1], dbs], name="all_reduce_small_grads")
    loss = sheet[ROW_LOSS, 0]
    taps = lambda row, width: jnp.stack([sheet[SG_LAYER * l + row:SG_LAYER * l + row + 3, :width] for l in range(DEPTH)])
    extra = dict(w_spatial=jnp.stack([dws0, dws1]), b_spatial=dbs.reshape(DEPTH, N_HEADS, GMLP_BLOCK),
                 w_shortconv=lax.dynamic_slice_in_dim(taps(ROW_SCONV, D_B), my_idx * sc_w, sc_w, axis=2),
                 w_ffn_conv=lax.dynamic_slice_in_dim(taps(ROW_FCONV, D_FF), my_idx * fc_w, fc_w, axis=2))
    small_w = dict(norm1_g=(norm1_g, m_norm1_g, v_norm1_g), b_gate=(b_gate, m_b_gate, v_b_gate),
                   gmlp_ln_g=(gmlp_ln_g, m_gmlp_ln_g, v_gmlp_ln_g), gmlp_ln_b=(gmlp_ln_b, m_gmlp_ln_b, v_gmlp_ln_b),
                   w_spatial=(w_spatial, m_w_spatial, v_w_spatial), b_spatial=(b_spatial, m_b_spatial, v_b_spatial),
                   w_shortconv=(w_shortconv, m_w_shortconv, v_w_shortconv), norm2_g=(norm2_g, m_norm2_g, v_norm2_g),
                   w_ffn_conv=(w_ffn_conv, m_w_ffn_conv, v_w_ffn_conv), b_ffn_conv=(b_ffn_conv, m_b_ffn_conv, v_b_ffn_conv),
                   final_g=tuple(t.reshape(1, D_MODEL) for t in (final_g, m_final_g, v_final_g)))
    results.update(_adamw_small(sheet, extra, small_w, name="adamw_small"))
    results["final_g"] = tuple(t.reshape(D_MODEL) for t in results["final_g"])

    names = ["norm1_g", "w_in", "b_gate", "gmlp_ln_g", "gmlp_ln_b", "w_spatial", "b_spatial", "w_shortconv", "w_branch",
             "w_out", "norm2_g", "w_ffn_up", "w_ffn_conv", "b_ffn_conv", "w_ffn_down", "final_g"]
    return (loss, grad_x, *[results[k][0] for k in names], *[results[k][1] for k in names],
            *[results[k][2] for k in names], *[results[k][3] for k in names])
```

```python
import math

import jax
import jax.numpy as jnp
from jax import lax
from jax.experimental import pallas as pl
from jax.experimental.pallas import tpu as pltpu

F32 = jnp.float32
BF16 = jnp.bfloat16

N_DEV = 8
DEPTH = 2
D_MODEL = 1024
D_A = 512
D_B = 512
D_FF = 2816
D_IN = 4608
N_HEADS = 4
HEAD = 128
GMLP_BLOCK = 128
CAUSAL_CHUNK = 64
OFF_U, OFF_V, OFF_BG, OFF_CG, OFF_HB, OFF_GA, OFF_GB = 0, 512, 1024, 1536, 2048, 2560, 3584
RMS_EPS = 1e-6
LN_EPS = 1e-5
ADAM_LR, ADAM_B1, ADAM_B2, ADAM_EPS, ADAM_WD, ADAM_STEP = 0.001, 0.9, 0.999, 1e-08, 0.01, 10

SUBLANES = 8
HALO = 16
FFN_CHUNK = 256
SG_ROWS, SG_W, SG_LAYER = 40, D_FF, 16
ROW_NORM1, ROW_BGATE, ROW_LN_G, ROW_LN_B, ROW_SCONV, ROW_NORM2, ROW_FCONV, ROW_BFCONV = 0, 1, 2, 3, 4, 7, 8, 11
ROW_FINAL, ROW_LOSS = 32, 33
V7X_VMEM_BYTES = 64 << 20
VMEM_LIMIT = V7X_VMEM_BYTES - (8 << 20)
MESH = pl.DeviceIdType.MESH
GELU_C0 = 0.7978845608028654
GELU_C1 = 0.044715
NT = (((1,), (1,)), ((), ()))
TN = (((0,), (0,)), ((), ()))


def _dot(a, b):
    return jnp.dot(a, b, preferred_element_type=F32)


def _dot_nt(a, b):
    return lax.dot_general(a, b, NT, preferred_element_type=F32)


def _sigmoid(x):
    return 1.0 / (1.0 + jnp.exp(-x))


def _gelu_tanh(x):
    return jnp.tanh(GELU_C0 * (x + GELU_C1 * x * x * x))


def _gelu_grad(x, t):
    return 0.5 * (1.0 + t) + 0.5 * x * (1.0 - t * t) * GELU_C0 * (1.0 + 3.0 * GELU_C1 * x * x)


def _sublane_tile(dtype):
    return SUBLANES * (4 // jnp.dtype(dtype).itemsize)


def _shift_down(a, k, prev):
    p = prev.shape[0]
    r = pltpu.roll(a, k, 0)
    sub = _sublane_tile(a.dtype)
    head = r[0:sub]
    rid = lax.broadcasted_iota(jnp.int32, head.shape, 0)
    for j in range(k):
        head = jnp.where(rid == j, prev[p - k + j:p - k + j + 1, :], head)
    return jnp.concatenate([head, r[sub:]], axis=0)


def _shift_up(a, k, nxt):
    t = a.shape[0]
    r = pltpu.roll(a, t - k, 0)
    sub = _sublane_tile(a.dtype)
    tail = r[t - sub:t]
    rid = lax.broadcasted_iota(jnp.int32, tail.shape, 0)
    for j in range(k):
        tail = jnp.where(rid == sub - k + j, nxt[j:j + 1, :], tail)
    return jnp.concatenate([r[0:t - sub], tail], axis=0)


def _column_sums(p):
    if p.dtype.itemsize < 4:
        t = p.shape[0]
        p = p[:t // 2] + p[t // 2:]
        p = p[:t // 4] + p[t // 4:]
    return jnp.sum(p.astype(F32), axis=0, keepdims=True)


def _sheet_begin(step, sheet_in, sheet_out, first_row, rows):
    @pl.when(step == 0)
    def _():
        sheet_out[...] = sheet_in[...]
        sheet_out[first_row:first_row + rows, :] = jnp.zeros((rows, SG_W), F32)


def _sheet_spec():
    return pl.BlockSpec((SG_ROWS, SG_W), lambda i: (0, 0))


def _spatial_mask(transposed):
    ri = lax.broadcasted_iota(jnp.int32, (GMLP_BLOCK, GMLP_BLOCK), 0) // CAUSAL_CHUNK
    ci = lax.broadcasted_iota(jnp.int32, (GMLP_BLOCK, GMLP_BLOCK), 1) // CAUSAL_CHUNK
    return (ri <= ci) if transposed else (ci <= ri)


def _gmlp_forward(u, v, ln_g, ln_b, ws_ref, bs_ref, f_scr):
    tm = u.shape[0]
    tu = _gelu_tanh(u)
    tv = _gelu_tanh(v)
    gu = 0.5 * u * (1.0 + tu)
    gv = 0.5 * v * (1.0 + tv)
    mu = jnp.mean(gv, axis=-1, keepdims=True)
    cen = gv - mu
    rstd = lax.rsqrt(jnp.mean(cen * cen, axis=-1, keepdims=True) + LN_EPS)
    xh = cen * rstd
    vn = (xh * ln_g + ln_b).astype(BF16)
    mask = _spatial_mask(False)
    wm = [jnp.where(mask, ws_ref[h], 0.0).astype(BF16) for h in range(N_HEADS)]
    for b in range(tm // GMLP_BLOCK):
        rows = slice(b * GMLP_BLOCK, (b + 1) * GMLP_BLOCK)
        for h in range(N_HEADS):
            cols = slice(h * HEAD, (h + 1) * HEAD)
            f_scr[rows, cols] = (_dot(wm[h], vn[rows, cols]) + bs_ref[h]).astype(f_scr.dtype)
    return gu, tu, tv, xh, rstd, vn, f_scr[...]


def _position():
    return lax.axis_index("x"), lax.axis_index("y"), lax.axis_index("c")


class _Gather:
    def __init__(self, arrays):
        self.arrays = list(arrays)
        self.out_shape = [jax.ShapeDtypeStruct((N_DEV,) + a.shape, a.dtype) for a in self.arrays]
        self.base = 0

    def _plan(self, ins, outs, sems):
        send_sems, recv_sems, local_sems = sems
        x, y, c = _position()
        me, sibling = (x, y, c), (x, y, 1 - c)
        chips = [(1 - x, y), (x, 1 - y), (1 - x, 1 - y)]

        def slot(a, p):
            return outs[a].at[4 * p[0] + 2 * p[1] + p[2]]

        def copy(a, k, block, to, src=None):
            return pltpu.make_async_remote_copy(
                src_ref=slot(a, block) if src is None else src, dst_ref=slot(a, block),
                send_sem=send_sems.at[self.base + a, k], recv_sem=recv_sems.at[self.base + a, k],
                device_id=to, device_id_type=MESH)

        n = len(self.arrays)

        def mine():
            return [pltpu.make_async_copy(ins[a], slot(a, me), local_sems.at[self.base + a]) for a in range(n)]

        def first():
            out = []
            for a in range(n):
                out.append(copy(a, 0, me, sibling, src=ins[a]))
                out += [copy(a, 1 + j, me, (*chip, c), src=ins[a]) for j, chip in enumerate(chips)]
            return out

        def arrivals():
            return [copy(a, 1 + j, (*chip, c), me) for j, chip in enumerate(chips) for a in range(n)]

        def relays():
            return [copy(a, 4 + j, (*chip, c), sibling) for j, chip in enumerate(chips) for a in range(n)]

        def from_sibling():
            out = [copy(a, 0, sibling, me) for a in range(n)]
            return out + [copy(a, 4 + j, (*chip, 1 - c), me) for j, chip in enumerate(chips) for a in range(n)]

        return mine, first, arrivals, relays, from_sibling

    def start(self, ins, outs, sems):
        mine, first, _, _, _ = self._plan(ins, outs, sems)
        for cp in mine() + first():
            cp.start()

    def relay(self, ins, outs, sems):
        _, _, arrivals, relays, _ = self._plan(ins, outs, sems)
        for arrived, onward in zip(arrivals(), relays()):
            arrived.wait_recv()
            onward.start()

    def finish(self, ins, outs, sems):
        mine, first, _, relays, from_sibling = self._plan(ins, outs, sems)
        for cp in from_sibling():
            cp.wait_recv()
        for cp in first() + relays():
            cp.wait_send()
        for cp in mine():
            cp.wait()


class _Exchange:
    def __init__(self, arrays):
        self.arrays = list(arrays)
        self.out_shape = [jax.ShapeDtypeStruct(a.shape, a.dtype) for a in self.arrays]
        self.base = 0

    def _plan(self, ins, outs, sems):
        send_sems, recv_sems, local_sems = sems
        x, y, c = _position()
        my_idx = 4 * x + 2 * y + c
        n = len(self.arrays)
        offsets = [(dx, dy, dc) for dx in (0, 1) for dy in (0, 1) for dc in (0, 1) if (dx, dy, dc) != (0, 0, 0)]

        def mine():
            return [pltpu.make_async_copy(ins[a].at[my_idx], outs[a].at[my_idx], local_sems.at[self.base + a])
                    for a in range(n)]

        def remote(arriving):
            out = []
            for k, (dx, dy, dc) in enumerate(offsets):
                px, py, pc = x ^ dx, y ^ dy, c ^ dc
                p_idx = 4 * px + 2 * py + pc
                for a in range(n):
                    out.append(pltpu.make_async_remote_copy(
                        src_ref=ins[a].at[p_idx], dst_ref=outs[a].at[p_idx if arriving else my_idx],
                        send_sem=send_sems.at[self.base + a, k], recv_sem=recv_sems.at[self.base + a, k],
                        device_id=(px, py, pc), device_id_type=MESH))
            return out

        return mine, remote

    def start(self, ins, outs, sems):
        mine, remote = self._plan(ins, outs, sems)
        for cp in mine() + remote(False):
            cp.start()

    def relay(self, ins, outs, sems):
        pass

    def finish(self, ins, outs, sems):
        mine, remote = self._plan(ins, outs, sems)
        for cp in remote(True):
            cp.wait_recv()
        for cp in remote(False):
            cp.wait_send()
        for cp in mine():
            cp.wait()


class _Both:
    def __init__(self, *carries):
        self.carries = carries
        self.arrays = [a for c in carries for a in c.arrays]
        self.out_shape = [s for c in carries for s in c.out_shape]
        first = 0
        for c in carries:
            c.base = first
            first += len(c.arrays)

    def _each(self, method, ins, outs, sems):
        for c in self.carries:
            rows = slice(c.base, c.base + len(c.arrays))
            getattr(c, method)(ins[rows], outs[rows], sems)

    def start(self, ins, outs, sems):
        self._each("start", ins, outs, sems)

    def relay(self, ins, outs, sems):
        self._each("relay", ins, outs, sems)

    def finish(self, ins, outs, sems):
        self._each("finish", ins, outs, sems)


def _call(body, *, name, grid, in_specs, out_specs, out_shape, args, scratch_shapes=(), carry=None):
    n_in, n_out, n_scr = len(in_specs), len(out_specs), len(scratch_shapes)
    params = pltpu.CompilerParams(dimension_semantics=("arbitrary",) * len(grid), vmem_limit_bytes=VMEM_LIMIT)
    if carry is None:
        outs = pl.pallas_call(body, name=name, grid=grid, in_specs=in_specs, out_specs=out_specs, out_shape=out_shape,
                              scratch_shapes=list(scratch_shapes), compiler_params=params)(*args)
        return outs, None
    m = len(carry.arrays)
    total = math.prod(grid)

    def wrapped(*refs):
        ins, refs = refs[:n_in], refs[n_in:]
        c_ins, refs = refs[:m], refs[m:]
        outs, refs = refs[:n_out], refs[n_out:]
        c_outs, refs = refs[:m], refs[m:]
        scr, sems = refs[:n_scr], refs[n_scr:]
        flat = pl.program_id(0)
        for d in range(1, len(grid)):
            flat = flat * grid[d] + pl.program_id(d)

        @pl.when(flat == 0)
        def _():
            carry.start(c_ins, c_outs, sems)

        body(*ins, *outs, *scr)

        @pl.when(flat == total - 2)
        def _():
            carry.relay(c_ins, c_outs, sems)

        @pl.when(flat == total - 1)
        def _():
            carry.finish(c_ins, c_outs, sems)

    any_spec = pl.BlockSpec(memory_space=pl.ANY)
    sem_shapes = [pltpu.SemaphoreType.DMA((m, 7)), pltpu.SemaphoreType.DMA((m, 7)), pltpu.SemaphoreType.DMA((m,))]
    outs = pl.pallas_call(
        wrapped, name=name, grid=grid,
        in_specs=list(in_specs) + [any_spec] * m, out_specs=list(out_specs) + [any_spec] * m,
        out_shape=list(out_shape) + carry.out_shape,
        scratch_shapes=list(scratch_shapes) + sem_shapes, compiler_params=params)(*args, *carry.arrays)
    return outs[:n_out], outs[n_out:]


def _gather_now(arrays, *, name):
    carry = _Gather(arrays)
    m = len(arrays)

    def body(*refs):
        ins, outs, sems = refs[:m], refs[m:2 * m], refs[2 * m:]
        carry.start(ins, outs, sems)
        carry.relay(ins, outs, sems)
        carry.finish(ins, outs, sems)

    any_spec = pl.BlockSpec(memory_space=pl.ANY)
    return pl.pallas_call(
        body, name=name, in_specs=[any_spec] * m, out_specs=[any_spec] * m, out_shape=carry.out_shape,
        scratch_shapes=[pltpu.SemaphoreType.DMA((m, 7)), pltpu.SemaphoreType.DMA((m, 7)),
                        pltpu.SemaphoreType.DMA((m,))],
    )(*arrays)


def _all_reduce_small(arrs, *, name):
    n = len(arrs)

    def body(*refs):
        ins, outs, bufs = refs[:n], refs[n:2 * n], refs[2 * n:3 * n]
        send_sems, recv_sems = refs[3 * n:]
        x, y, c = _position()
        me, sibling = (x, y, c), (x, y, 1 - c)
        chips = [(1 - x, y), (x, 1 - y), (1 - x, 1 - y)]

        def copy(a, k, block, to, src=None):
            slot = bufs[a].at[4 * block[0] + 2 * block[1] + block[2]]
            return pltpu.make_async_remote_copy(
                src_ref=slot if src is None else src, dst_ref=slot,
                send_sem=send_sems.at[a, k], recv_sem=recv_sems.at[a, k], device_id=to, device_id_type=MESH)

        first = []
        for a in range(n):
            first.append(copy(a, 0, me, sibling, src=ins[a]))
            first += [copy(a, 1 + j, me, (*chip, c), src=ins[a]) for j, chip in enumerate(chips)]
        for cp in first:
            cp.start()
        passed = []
        for j, chip in enumerate(chips):
            for a in range(n):
                copy(a, 1 + j, (*chip, c), me).wait_recv()
                cp = copy(a, 4 + j, (*chip, c), sibling)
                cp.start()
                passed.append(cp)
        for a in range(n):
            copy(a, 0, sibling, me).wait_recv()
            for j, chip in enumerate(chips):
                copy(a, 4 + j, (*chip, 1 - c), me).wait_recv()
        for cp in first + passed:
            cp.wait_send()
        my_idx = 4 * x + 2 * y + c
        for a in range(n):
            acc = jnp.zeros(ins[a].shape, F32)
            for s in range(N_DEV):
                acc = acc + jnp.where(my_idx == s, ins[a][...], bufs[a][s])
            outs[a][...] = acc

    vmem = pl.BlockSpec(memory_space=pltpu.VMEM)
    return pl.pallas_call(
        body, name=name, in_specs=[vmem] * n, out_specs=[vmem] * n,
        out_shape=[jax.ShapeDtypeStruct(a.shape, F32) for a in arrs],
        scratch_shapes=[pltpu.VMEM((N_DEV,) + a.shape, F32) for a in arrs]
        + [pltpu.SemaphoreType.DMA((n, 7)), pltpu.SemaphoreType.DMA((n, 7))],
        compiler_params=pltpu.CompilerParams(vmem_limit_bytes=VMEM_LIMIT),
    )(*arrs)


def _sum_gathered(arrs, *, name):
    n = len(arrs)

    def body(*refs):
        for in_ref, out_ref in zip(refs[:n], refs[n:]):
            acc = in_ref[0]
            for s in range(1, N_DEV):
                acc = acc + in_ref[s]
            out_ref[...] = acc

    vmem = pl.BlockSpec(memory_space=pltpu.VMEM)
    return pl.pallas_call(
        body, name=name, in_specs=[vmem] * n, out_specs=[vmem] * n,
        out_shape=[jax.ShapeDtypeStruct(a.shape[1:], F32) for a in arrs],
        compiler_params=pltpu.CompilerParams(vmem_limit_bytes=VMEM_LIMIT),
    )(*arrs)


def _norm_matmul(x, g, w_t, *, tm, name, carry=None):
    n, d = x.shape
    c = w_t.shape[0]
    ch = 512

    def body(x_ref, g_ref, wt_ref, h_ref, z_ref):
        xv = x_ref[...]
        r = lax.rsqrt(jnp.mean(xv * xv, axis=-1, keepdims=True) + RMS_EPS)
        h = (xv * r * g_ref[...]).astype(BF16)
        h_ref[...] = h
        for c0 in range(0, c, ch):
            z_ref[:, c0:c0 + ch] = _dot_nt(h, wt_ref[c0:c0 + ch, :]).astype(BF16)

    return _call(
        body, name=name, grid=(n // tm,), carry=carry,
        in_specs=[pl.BlockSpec((tm, d), lambda i: (i, 0)),
                  pl.BlockSpec((1, d), lambda i: (0, 0)),
                  pl.BlockSpec((c, d), lambda i: (0, 0))],
        out_specs=[pl.BlockSpec((tm, d), lambda i: (i, 0)),
                   pl.BlockSpec((tm, c), lambda i: (i, 0))],
        out_shape=[jax.ShapeDtypeStruct((n, d), BF16), jax.ShapeDtypeStruct((n, c), BF16)],
        args=(x, g.reshape(1, d), w_t))


def _mix_forward(z, x, b_gate, ln_g, ln_b, w_s, b_s, w_sc, wb, w_out, *, tm, name, carry=None):
    n = z.shape[0]
    hb = tm // HALO

    def body(z_ref, zp_ref, x_ref, bg_ref, lng_ref, lnb_ref, ws_ref, bs_ref, wsc_ref, wb_ref, wo_ref,
             ya_ref, yb_ref, cv_ref, pa_ref, pb_ref, mg_ref, x1_ref, f_scr):
        i = pl.program_id(0)
        u = z_ref[:, OFF_U:OFF_U + D_A]
        v = z_ref[:, OFF_V:OFF_V + D_A].astype(F32)
        gu, _, _, _, _, _, f = _gmlp_forward(u, v, lng_ref[...], lnb_ref[...], ws_ref, bs_ref, f_scr)
        ya = gu * f
        ya_ref[...] = ya

        q = z_ref[:, OFF_CG:OFF_CG + D_B] * z_ref[:, OFF_HB:OFF_HB + D_B]
        qp = zp_ref[:, OFF_CG:OFF_CG + D_B] * zp_ref[:, OFF_HB:OFF_HB + D_B]
        qp = jnp.where(i > 0, qp, jnp.zeros_like(qp))
        w = wsc_ref[...].astype(BF16)
        conv = w[0:1] * _shift_down(q, 2, qp) + w[1:2] * _shift_down(q, 1, qp) + w[2:3] * q
        cv_ref[...] = conv
        yb = z_ref[:, OFF_BG:OFF_BG + D_B] * conv
        yb_ref[...] = yb

        pa = _dot(ya, wb_ref[0]).astype(BF16)
        pb = _dot(yb, wb_ref[1]).astype(BF16)
        pa_ref[...] = pa
        pb_ref[...] = pb
        bg = bg_ref[...].astype(BF16)
        sa = _sigmoid(z_ref[:, OFF_GA:OFF_GA + D_MODEL] + bg[:, 0:D_MODEL])
        sb = _sigmoid(z_ref[:, OFF_GB:OFF_GB + D_MODEL] + bg[:, D_MODEL:2 * D_MODEL])
        mg = sa * pa + sb * pb
        mg_ref[...] = mg
        x1_ref[...] = x_ref[...] + _dot(mg, wo_ref[...])

    row = lambda w: pl.BlockSpec((tm, w), lambda i: (i, 0))
    full = lambda *s: pl.BlockSpec(s, lambda i: (0,) * len(s))
    bf = lambda w: jax.ShapeDtypeStruct((n, w), BF16)
    return _call(
        body, name=name, grid=(n // tm,), carry=carry,
        in_specs=[row(D_IN),
                  pl.BlockSpec((HALO, D_IN), lambda i: (jnp.maximum(i * hb - 1, 0), 0)),
                  row(D_MODEL), full(1, 2 * D_MODEL), full(1, D_A), full(1, D_A),
                  full(N_HEADS, GMLP_BLOCK, GMLP_BLOCK), full(N_HEADS, GMLP_BLOCK, 1), full(3, D_B),
                  full(2, D_A, D_MODEL), full(D_MODEL, D_MODEL)],
        out_specs=[row(D_A), row(D_B), row(D_B), row(D_MODEL), row(D_MODEL), row(D_MODEL), row(D_MODEL)],
        out_shape=[bf(D_A), bf(D_B), bf(D_B), bf(D_MODEL), bf(D_MODEL), bf(D_MODEL),
                   jax.ShapeDtypeStruct((n, D_MODEL), F32)],
        scratch_shapes=[pltpu.VMEM((tm, D_A), BF16)],
        args=(z, z, x, b_gate.reshape(1, -1), ln_g.reshape(1, -1), ln_b.reshape(1, -1), w_s,
              b_s.reshape(N_HEADS, GMLP_BLOCK, 1), w_sc, wb, w_out))


def _loss_tile(xv, gv, tv):
    d = xv.shape[-1]
    r = lax.rsqrt(jnp.mean(xv * xv, axis=-1, keepdims=True) + RMS_EPS)
    xh = xv * r
    e = xh * gv - tv
    per_row = jnp.sum(e * e, axis=-1, keepdims=True) * (0.5 / d)
    dy = e * (1.0 / d)
    dxh = dy * gv
    dx = r * (dxh - xh * jnp.mean(dxh * xh, axis=-1, keepdims=True))
    return dx, jnp.sum(per_row, axis=0, keepdims=True), jnp.sum(dy * xh, axis=0, keepdims=True)


def _ffn_forward(up, x1, w_fc, b_fc, w_down, *, tm, name, carry=None, head=None):
    n = up.shape[0]
    hb = tm // HALO
    n_in = 6 if head is None else 8

    def body(*refs):
        up_ref, upp_ref, x1_ref, wfc_ref, bfc_ref, wd_ref = refs[:6]
        gc_ref, a_ref, out_ref = refs[n_in:n_in + 3]
        acc = refs[-1]
        i = pl.program_id(0)
        acc[...] = x1_ref[...]
        for c0 in range(0, D_FF, FFN_CHUNK):
            cols = slice(c0, c0 + FFN_CHUNK)
            gate = up_ref[:, cols]
            val = up_ref[:, D_FF + c0:D_FF + c0 + FFN_CHUNK]
            gp = upp_ref[:, cols]
            gp = jnp.where(i > 0, gp, jnp.zeros_like(gp))
            w = wfc_ref[:, cols].astype(BF16)
            gc = (w[0:1] * _shift_down(gate, 2, gp) + w[1:2] * _shift_down(gate, 1, gp) + w[2:3] * gate
                  + bfc_ref[:, cols].astype(BF16))
            gc_ref[:, cols] = gc
            a = gc * _sigmoid(gc) * val
            a_ref[:, cols] = a
            acc[...] += _dot(a, wd_ref[cols, :])
        if head is None:
            out_ref[...] = acc[...]
        else:
            g_ref, t_ref = refs[6:8]
            sg_ref = refs[n_in + 3]

            @pl.when(i == 0)
            def _():
                sg_ref[...] = jnp.zeros_like(sg_ref)

            dx, loss, dg = _loss_tile(acc[...], g_ref[...], t_ref[...])
            out_ref[...] = dx
            sg_ref[ROW_LOSS:ROW_LOSS + 1, 0:128] += jnp.broadcast_to(loss, (1, 128))
            sg_ref[ROW_FINAL:ROW_FINAL + 1, 0:D_MODEL] += dg

    row = lambda w: pl.BlockSpec((tm, w), lambda i: (i, 0))
    full = lambda r, c: pl.BlockSpec((r, c), lambda i: (0, 0))
    in_specs = [row(2 * D_FF), pl.BlockSpec((HALO, D_FF), lambda i: (jnp.maximum(i * hb - 1, 0), 0)), row(D_MODEL),
                full(3, D_FF), full(1, D_FF), full(D_FF, D_MODEL)]
    out_specs = [row(D_FF), row(D_FF), row(D_MODEL)]
    out_shape = [jax.ShapeDtypeStruct((n, D_FF), BF16), jax.ShapeDtypeStruct((n, D_FF), BF16),
                 jax.ShapeDtypeStruct((n, D_MODEL), F32)]
    args = (up, up, x1, w_fc, b_fc.reshape(1, -1), w_down)
    if head is not None:
        in_specs += [full(1, D_MODEL), row(D_MODEL)]
        out_specs += [full(SG_ROWS, SG_W)]
        out_shape += [jax.ShapeDtypeStruct((SG_ROWS, SG_W), F32)]
        args += (head[0].reshape(1, -1), head[1])
    return _call(body, name=name, grid=(n // tm,), carry=carry, in_specs=in_specs, out_specs=out_specs,
                 out_shape=out_shape, scratch_shapes=[pltpu.VMEM((tm, D_MODEL), F32)], args=args)


def _ffn_backward(dx2, up, gc, w_fc, w_down, sheet, layer, *, tm, name, carry=None):
    n = up.shape[0]
    steps = n // tm
    hb = tm // HALO
    row = SG_LAYER * layer + ROW_FCONV

    def body(dx_ref, dxn_ref, up_ref, upn_ref, gc_ref, gcn_ref, wfc_ref, wd_ref, sg_in, dup_ref, sg_ref):
        i = pl.program_id(0)
        last = i == steps - 1
        _sheet_begin(i, sg_in, sg_ref, row, 4)

        dxe = jnp.concatenate([dx_ref[...], dxn_ref[...]], axis=0).astype(BF16)
        for c0 in range(0, D_FF, FFN_CHUNK):
            cols = slice(c0, c0 + FFN_CHUNK)
            vcols = slice(D_FF + c0, D_FF + c0 + FFN_CHUNK)
            dae = _dot_nt(dxe, wd_ref[cols, :])
            da, dan = dae[:tm], dae[tm:]
            gate = up_ref[:, cols]
            val = up_ref[:, vcols]
            gcv = gc_ref[:, cols]
            s = _sigmoid(gcv)
            dab = da.astype(BF16)
            dup_ref[:, vcols] = dab * (gcv * s)
            dgc = dab * val * (s * (1.0 + gcv * (1.0 - s)))
            gcn = gcn_ref[:, cols]
            sn = _sigmoid(gcn)
            dgcn = dan.astype(BF16) * upn_ref[:, vcols] * (sn * (1.0 + gcn * (1.0 - sn)))
            dgcn = jnp.where(last, jnp.zeros_like(dgcn), dgcn)
            up1 = _shift_up(dgc, 1, dgcn)
            up2 = _shift_up(dgc, 2, dgcn)
            w = wfc_ref[:, cols].astype(BF16)
            dup_ref[:, cols] = w[2:3] * dgc + w[1:2] * up1 + w[0:1] * up2
            sg_ref[row:row + 1, cols] += _column_sums(gate * up2)
            sg_ref[row + 1:row + 2, cols] += _column_sums(gate * up1)
            sg_ref[row + 2:row + 3, cols] += _column_sums(gate * dgc)
            sg_ref[row + 3:row + 4, cols] += _column_sums(dgc)

    nxt = lambda i: (jnp.minimum((i + 1) * hb, steps * hb - 1), 0)
    return _call(
        body, name=name, grid=(steps,), carry=carry,
        in_specs=[pl.BlockSpec((tm, D_MODEL), lambda i: (i, 0)),
                  pl.BlockSpec((HALO, D_MODEL), nxt),
                  pl.BlockSpec((tm, 2 * D_FF), lambda i: (i, 0)),
                  pl.BlockSpec((HALO, 2 * D_FF), nxt),
                  pl.BlockSpec((tm, D_FF), lambda i: (i, 0)),
                  pl.BlockSpec((HALO, D_FF), nxt),
                  pl.BlockSpec((3, D_FF), lambda i: (0, 0)),
                  pl.BlockSpec((D_FF, D_MODEL), lambda i: (0, 0)), _sheet_spec()],
        out_specs=[pl.BlockSpec((tm, 2 * D_FF), lambda i: (i, 0)), _sheet_spec()],
        out_shape=[jax.ShapeDtypeStruct((n, 2 * D_FF), BF16), jax.ShapeDtypeStruct((SG_ROWS, SG_W), F32)],
        args=(dx2, dx2, up, up, gc, gc, w_fc, w_down, sheet))


def _matmul_norm_backward(dz, w_t, x, g, dres, sheet, row, *, tm, name, carry=None):
    n, c = dz.shape
    d = x.shape[1]
    ch = 512

    def body(dz_ref, wt_ref, x_ref, g_ref, dres_ref, *rest):
        i = pl.program_id(0)
        if sheet is None:
            dx_ref, sg_ref = rest

            @pl.when(i == 0)
            def _():
                sg_ref[...] = jnp.zeros_like(sg_ref)
        else:
            sg_in, dx_ref, sg_ref = rest
            _sheet_begin(i, sg_in, sg_ref, row, 1)

        dh = _dot(dz_ref[:, 0:ch], wt_ref[0:ch, :])
        for c0 in range(ch, c, ch):
            dh += _dot(dz_ref[:, c0:c0 + ch], wt_ref[c0:c0 + ch, :])
        xv = x_ref[...]
        r = lax.rsqrt(jnp.mean(xv * xv, axis=-1, keepdims=True) + RMS_EPS)
        xh = xv * r
        sg_ref[row:row + 1, 0:d] += jnp.sum(dh * xh, axis=0, keepdims=True)
        dxh = dh * g_ref[...]
        dx_ref[...] = dres_ref[...] + r * (dxh - xh * jnp.mean(dxh * xh, axis=-1, keepdims=True))

    in_specs = [pl.BlockSpec((tm, c), lambda i: (i, 0)),
                pl.BlockSpec((c, d), lambda i: (0, 0)),
                pl.BlockSpec((tm, d), lambda i: (i, 0)),
                pl.BlockSpec((1, d), lambda i: (0, 0)),
                pl.BlockSpec((tm, d), lambda i: (i, 0))]
    args = (dz, w_t, x, g.reshape(1, d), dres)
    if sheet is None:
        small_spec, small_shape = pl.BlockSpec((8, d), lambda i: (0, 0)), jax.ShapeDtypeStruct((8, d), F32)
    else:
        in_specs, args = in_specs + [_sheet_spec()], args + (sheet,)
        small_spec, small_shape = _sheet_spec(), jax.ShapeDtypeStruct((SG_ROWS, SG_W), F32)
    return _call(
        body, name=name, grid=(n // tm,), carry=carry, in_specs=in_specs,
        out_specs=[pl.BlockSpec((tm, d), lambda i: (i, 0)), small_spec],
        out_shape=[jax.ShapeDtypeStruct((n, d), F32), small_shape], args=args)


def _mix_backward(dx1, z, conv, pa, pb, b_gate, ln_g, ln_b, w_s, w_s_t, b_s, w_sc, w_out, wb, sheet, layer, *, tm, name,
                  carry=None):
    n = z.shape[0]
    steps = n // tm
    hb = tm // HALO
    base = SG_LAYER * layer
    r_bg, r_lng, r_lnb, r_sc = base + ROW_BGATE, base + ROW_LN_G, base + ROW_LN_B, base + ROW_SCONV

    def body(dx_ref, dxn_ref, z_ref, zn_ref, cv_ref, pa_ref, pb_ref, bg_ref, lng_ref, lnb_ref, ws_ref, wst_ref,
             bs_ref, wsc_ref, wo_ref, wb_ref, sg_in,
             dz_ref, dpa_ref, dpb_ref, dws_ref, dbs_ref, sg_ref, f_scr, dvn_scr):
        i = pl.program_id(0)
        last = i == steps - 1
        _sheet_begin(i, sg_in, sg_ref, r_bg, ROW_NORM2 - ROW_BGATE)

        @pl.when(i == 0)
        def _():
            dws_ref[...] = jnp.zeros_like(dws_ref)
            dbs_ref[...] = jnp.zeros_like(dbs_ref)

        dxe = jnp.concatenate([dx_ref[...], dxn_ref[...]], axis=0).astype(BF16)
        dmge = _dot_nt(dxe, wo_ref[...])
        dmg, dmgn = dmge[:tm].astype(BF16), dmge[tm:].astype(BF16)

        pa_v = pa_ref[...]
        pb_v = pb_ref[...]
        bg = bg_ref[...].astype(BF16)
        sa = _sigmoid(z_ref[:, OFF_GA:OFF_GA + D_MODEL] + bg[:, 0:D_MODEL])
        sb = _sigmoid(z_ref[:, OFF_GB:OFF_GB + D_MODEL] + bg[:, D_MODEL:2 * D_MODEL])
        dpa = dmg * sa
        dpb = dmg * sb
        dga = dmg * pa_v * sa * (1.0 - sa)
        dgb = dmg * pb_v * sb * (1.0 - sb)
        dpa_ref[...] = dpa
        dpb_ref[...] = dpb
        dz_ref[:, OFF_GA:OFF_GA + D_MODEL] = dga
        dz_ref[:, OFF_GB:OFF_GB + D_MODEL] = dgb
        sg_ref[r_bg:r_bg + 1, 0:D_MODEL] += _column_sums(dga)
        sg_ref[r_bg:r_bg + 1, D_MODEL:2 * D_MODEL] += _column_sums(dgb)

        dya = _dot_nt(dpa, wb_ref[0]).astype(BF16)
        u = z_ref[:, OFF_U:OFF_U + D_A]
        v = z_ref[:, OFF_V:OFF_V + D_A].astype(F32)
        ln_g = lng_ref[...]
        gu, tu, tv, xh, rstd, vn, f = _gmlp_forward(u, v, ln_g, lnb_ref[...], ws_ref, bs_ref, f_scr)
        dgu = dya * f
        df_bf = dya * gu
        dz_ref[:, OFF_U:OFF_U + D_A] = dgu * _gelu_grad(u, tu)
        mask = _spatial_mask(False)
        mask_t = _spatial_mask(True)
        wmt = [jnp.where(mask_t, wst_ref[h], 0.0).astype(BF16) for h in range(N_HEADS)]
        for b in range(tm // GMLP_BLOCK):
            rows = slice(b * GMLP_BLOCK, (b + 1) * GMLP_BLOCK)
            for h in range(N_HEADS):
                cols = slice(h * HEAD, (h + 1) * HEAD)
                dfb = df_bf[rows, cols]
                dvn_scr[rows, cols] = _dot(wmt[h], dfb)
                dws_ref[h] += jnp.where(mask, _dot_nt(dfb, vn[rows, cols]), 0.0)
                dbs_ref[h] += jnp.sum(dfb.astype(F32), axis=1, keepdims=True)
        dvn = dvn_scr[...]
        sg_ref[r_lng:r_lng + 1, 0:D_A] += jnp.sum(dvn * xh, axis=0, keepdims=True)
        sg_ref[r_lnb:r_lnb + 1, 0:D_A] += jnp.sum(dvn, axis=0, keepdims=True)
        dxh = dvn * ln_g
        dgv = rstd * (dxh - jnp.mean(dxh, axis=-1, keepdims=True) - xh * jnp.mean(dxh * xh, axis=-1, keepdims=True))
        dz_ref[:, OFF_V:OFF_V + D_A] = (dgv * _gelu_grad(v, tv)).astype(BF16)

        sbn = _sigmoid(zn_ref[:, OFF_GB:OFF_GB + D_MODEL] + bg[:, D_MODEL:2 * D_MODEL])
        dpbe = jnp.concatenate([dpb, dmgn * sbn], axis=0)
        dybe = _dot_nt(dpbe, wb_ref[1])
        dyb, dybn = dybe[:tm].astype(BF16), dybe[tm:].astype(BF16)
        bgv = z_ref[:, OFF_BG:OFF_BG + D_B]
        cg = z_ref[:, OFF_CG:OFF_CG + D_B]
        hbv = z_ref[:, OFF_HB:OFF_HB + D_B]
        q = cg * hbv
        dz_ref[:, OFF_BG:OFF_BG + D_B] = dyb * cv_ref[...]
        dconv = dyb * bgv
        dconvn = dybn * zn_ref[:, OFF_BG:OFF_BG + D_B]
        dconvn = jnp.where(last, jnp.zeros_like(dconvn), dconvn)
        up1 = _shift_up(dconv, 1, dconvn)
        up2 = _shift_up(dconv, 2, dconvn)
        sg_ref[r_sc:r_sc + 1, 0:D_B] += _column_sums(q * up2)
        sg_ref[r_sc + 1:r_sc + 2, 0:D_B] += _column_sums(q * up1)
        sg_ref[r_sc + 2:r_sc + 3, 0:D_B] += _column_sums(q * dconv)
        w = wsc_ref[...].astype(BF16)
        dq = w[2:3] * dconv + w[1:2] * up1 + w[0:1] * up2
        dz_ref[:, OFF_CG:OFF_CG + D_B] = dq * hbv
        dz_ref[:, OFF_HB:OFF_HB + D_B] = dq * cg

    row = lambda w: pl.BlockSpec((tm, w), lambda i: (i, 0))
    full = lambda *s: pl.BlockSpec(s, lambda i: (0,) * len(s))
    nxt = lambda i: (jnp.minimum((i + 1) * hb, steps * hb - 1), 0)
    return _call(
        body, name=name, grid=(steps,), carry=carry,
        in_specs=[row(D_MODEL), pl.BlockSpec((HALO, D_MODEL), nxt),
                  row(D_IN), pl.BlockSpec((HALO, D_IN), nxt),
                  row(D_B), row(D_MODEL), row(D_MODEL),
                  full(1, 2 * D_MODEL), full(1, D_A), full(1, D_A),
                  full(N_HEADS, GMLP_BLOCK, GMLP_BLOCK), full(N_HEADS, GMLP_BLOCK, GMLP_BLOCK),
                  full(N_HEADS, GMLP_BLOCK, 1), full(3, D_B),
                  full(D_MODEL, D_MODEL), full(2, D_A, D_MODEL), _sheet_spec()],
        out_specs=[row(D_IN), row(D_MODEL), row(D_MODEL), full(N_HEADS, GMLP_BLOCK, GMLP_BLOCK),
                   full(N_HEADS, GMLP_BLOCK, 1), _sheet_spec()],
        out_shape=[jax.ShapeDtypeStruct((n, D_IN), BF16), jax.ShapeDtypeStruct((n, D_MODEL), BF16),
                   jax.ShapeDtypeStruct((n, D_MODEL), BF16),
                   jax.ShapeDtypeStruct((N_HEADS, GMLP_BLOCK, GMLP_BLOCK), F32),
                   jax.ShapeDtypeStruct((N_HEADS, GMLP_BLOCK, 1), F32), jax.ShapeDtypeStruct((SG_ROWS, SG_W), F32)],
        scratch_shapes=[pltpu.VMEM((tm, D_A), BF16), pltpu.VMEM((tm, D_A), F32)],
        args=(dx1, dx1, z, z, conv, pa, pb, b_gate.reshape(1, -1), ln_g.reshape(1, -1), ln_b.reshape(1, -1), w_s, w_s_t,
              b_s.reshape(N_HEADS, GMLP_BLOCK, 1), w_sc, w_out, wb, sheet))


def _matmul_tn(a, b, *, t1, tn, name, carry=None, pieces=1):
    n, k1 = a.shape
    k2 = b.shape[1]
    steps = n // tn
    w = k2 // pieces

    def body(a_ref, b_ref, *rest):
        o_refs, acc = rest[:pieces], rest[pieces]
        s = pl.program_id(1)

        @pl.when(s == 0)
        def _():
            acc[...] = jnp.zeros_like(acc)

        acc[...] += lax.dot_general(a_ref[...].astype(BF16), b_ref[...].astype(BF16), TN, preferred_element_type=F32)

        @pl.when(s == steps - 1)
        def _():
            for c, o_ref in enumerate(o_refs):
                o_ref[...] = acc[:, c * w:(c + 1) * w].astype(BF16)

    outs, carried = _call(
        body, name=name, grid=(k1 // t1, steps), carry=carry,
        in_specs=[pl.BlockSpec((tn, t1), lambda i, s: (s, i)),
                  pl.BlockSpec((tn, k2), lambda i, s: (s, 0))],
        out_specs=[pl.BlockSpec((t1, w), lambda i, s: (i, 0))] * pieces,
        out_shape=[jax.ShapeDtypeStruct((k1, w), BF16)] * pieces,
        scratch_shapes=[pltpu.VMEM((t1, k2), F32)],
        args=(a, b))
    return (outs[0] if pieces == 1 else list(outs)), carried


def _adamw_math(w, g, m, v):
    m = ADAM_B1 * m + (1.0 - ADAM_B1) * g
    v = ADAM_B2 * v + (1.0 - ADAM_B2) * (g * g)
    m_hat = m / (1.0 - ADAM_B1 ** ADAM_STEP)
    v_hat = v / (1.0 - ADAM_B2 ** ADAM_STEP)
    delta = -ADAM_LR * (m_hat / (jnp.sqrt(v_hat) + ADAM_EPS) + ADAM_WD * w)
    return delta, m, v


def _sum_parts(recvs, *, tr, name):
    _, r, c = recvs[0].shape

    def body(*refs):
        recv_refs, g_ref = refs[:DEPTH], refs[DEPTH]
        layer = pl.program_id(0)
        for l in range(DEPTH):
            @pl.when(layer == l)
            def _(l=l):
                g = recv_refs[l][0].astype(F32)
                for s in range(1, N_DEV):
                    g = g + recv_refs[l][s].astype(F32)
                g_ref[0] = g

    outs, _ = _call(
        body, name=name, grid=(DEPTH, r // tr),
        in_specs=[pl.BlockSpec((N_DEV, tr, c), lambda l, i: (0, i, 0))] * DEPTH,
        out_specs=[pl.BlockSpec((1, tr, c), lambda l, i: (l, i, 0))],
        out_shape=[jax.ShapeDtypeStruct((DEPTH, r, c), F32)],
        args=tuple(recvs))
    return outs[0]


def _adamw(w, g, m, v, *, tr, name):
    r, c = w.shape

    def body(w_ref, g_ref, m_ref, v_ref, d_ref, nm_ref, nv_ref):
        delta, nm, nv = _adamw_math(w_ref[...], g_ref[...], m_ref[...], v_ref[...])
        d_ref[...] = delta
        nm_ref[...] = nm
        nv_ref[...] = nv

    spec = pl.BlockSpec((tr, c), lambda i: (i, 0))
    outs, _ = _call(body, name=name, grid=(r // tr,), in_specs=[spec] * 4, out_specs=[spec] * 3,
                    out_shape=[jax.ShapeDtypeStruct((r, c), F32)] * 3, args=(w, g, m, v))
    return outs


def _sum_adamw(recvs, w, m, v, *, tr, name):
    _, r, c = w.shape
    blocks = len(recvs[0])
    flat = [piece for layer in recvs for piece in layer]

    def body(*refs):
        recv_refs = refs[:len(flat)]
        w_ref, m_ref, v_ref, g_ref, d_ref, nm_ref, nv_ref = refs[len(flat):]
        layer = pl.program_id(0)
        for l in range(DEPTH):
            @pl.when(layer == l)
            def _(l=l):
                cols = []
                for piece in recv_refs[l * blocks:(l + 1) * blocks]:
                    part = piece[0].astype(F32)
                    for s in range(1, N_DEV):
                        part = part + piece[s].astype(F32)
                    cols.append(part)
                g = cols[0] if blocks == 1 else jnp.concatenate(cols, axis=-1)
                delta, nm, nv = _adamw_math(w_ref[0], g, m_ref[0], v_ref[0])
                g_ref[0] = g
                d_ref[0] = delta
                nm_ref[0] = nm
                nv_ref[0] = nv

    spec = pl.BlockSpec((1, tr, c), lambda l, i: (l, i, 0))
    outs, _ = _call(
        body, name=name, grid=(DEPTH, r // tr),
        in_specs=[pl.BlockSpec((N_DEV, tr, c // blocks), lambda l, i: (0, i, 0))] * len(flat) + [spec] * 3,
        out_specs=[spec] * 4, out_shape=[jax.ShapeDtypeStruct((DEPTH, r, c), F32)] * 4,
        args=tuple(flat) + (w, m, v))
    return outs


def _adamw_small(sheet, extra, params, *, name):
    sheet_rows = dict(norm1_g=ROW_NORM1, b_gate=ROW_BGATE, gmlp_ln_g=ROW_LN_G, gmlp_ln_b=ROW_LN_B, norm2_g=ROW_NORM2,
                      b_ffn_conv=ROW_BFCONV)
    names = list(params)
    extra_names = list(extra)

    def body(*refs):
        sg_ref, refs = refs[0], refs[1:]
        extra_refs, refs = dict(zip(extra_names, refs[:len(extra_names)])), refs[len(extra_names):]
        ins, outs = refs[:3 * len(names)], refs[3 * len(names):]
        for j, key in enumerate(names):
            w_ref, m_ref, v_ref = ins[3 * j:3 * j + 3]
            g_ref, d_ref, nm_ref, nv_ref = outs[4 * j:4 * j + 4]
            if key in extra_refs:
                g_ref[...] = extra_refs[key][...]
            elif key == "final_g":
                g_ref[...] = sg_ref[ROW_FINAL:ROW_FINAL + 1, 0:D_MODEL]
            else:
                width = w_ref.shape[-1]
                for l in range(DEPTH):
                    row = SG_LAYER * l + sheet_rows[key]
                    g_ref[l:l + 1, :] = sg_ref[row:row + 1, 0:width]
            delta, nm, nv = _adamw_math(w_ref[...], g_ref[...], m_ref[...], v_ref[...])
            d_ref[...] = delta
            nm_ref[...] = nm
            nv_ref[...] = nv

    args = [sheet] + [extra[k] for k in extra_names] + [t for k in names for t in params[k]]
    vmem = pl.BlockSpec(memory_space=pltpu.VMEM)
    outs = pl.pallas_call(
        body, name=name, in_specs=[vmem] * len(args), out_specs=[vmem] * (4 * len(names)),
        out_shape=[jax.ShapeDtypeStruct(params[k][0].shape, F32) for k in names for _ in range(4)],
    )(*args)
    return {k: tuple(outs[4 * j:4 * j + 4]) for j, k in enumerate(names)}


def _rows(gathered):
    return gathered.reshape(N_DEV * gathered.shape[1], gathered.shape[2])


def _parts(full):
    return full.reshape(N_DEV, full.shape[0] // N_DEV, full.shape[1])


def kernel(x, norm1_g, w_in, b_gate, gmlp_ln_g, gmlp_ln_b, w_spatial, b_spatial, w_shortconv, w_branch, w_out, norm2_g, w_ffn_up, w_ffn_conv, b_ffn_conv, w_ffn_down, final_g, loss_target, m_norm1_g, m_w_in, m_b_gate, m_gmlp_ln_g, m_gmlp_ln_b, m_w_spatial, m_b_spatial, m_w_shortconv, m_w_branch, m_w_out, m_norm2_g, m_w_ffn_up, m_w_ffn_conv, m_b_ffn_conv, m_w_ffn_down, m_final_g, v_norm1_g, v_w_in, v_b_gate, v_gmlp_ln_g, v_gmlp_ln_b, v_w_spatial, v_b_spatial, v_w_shortconv, v_w_branch, v_w_out, v_norm2_g, v_w_ffn_up, v_w_ffn_conv, v_b_ffn_conv, v_w_ffn_down, v_final_g):
    n = x.shape[1]
    tm, tm_mix, tn = 512, 512, 1024
    x0 = x.reshape(n, D_MODEL)
    target = loss_target.reshape(n, D_MODEL)
    my_idx = 4 * lax.axis_index("x") + 2 * lax.axis_index("y") + lax.axis_index("c")
    sc_w, fc_w = D_B // N_DEV, D_FF // N_DEV

    sh_in = [w_in[l].T.astype(BF16) for l in range(DEPTH)]
    sh_up = [w_ffn_up[l].T.astype(BF16) for l in range(DEPTH)]
    sh_br = [w_branch[l].astype(BF16) for l in range(DEPTH)]
    sh_out = [w_out[l].astype(BF16) for l in range(DEPTH)]
    sh_down = [w_ffn_down[l].astype(BF16) for l in range(DEPTH)]
    taps = jnp.concatenate([w_shortconv, w_ffn_conv], axis=-1)

    def branch_weights(g):
        return g.transpose(1, 2, 0, 3).reshape(2, D_A, D_MODEL)

    g_in0, g_taps = _gather_now([sh_in[0], taps], name="gather_first")
    w_sc = [g_taps[:, l, :, :sc_w].transpose(1, 0, 2).reshape(3, D_B) for l in range(DEPTH)]
    w_fc = [g_taps[:, l, :, sc_w:].transpose(1, 0, 2).reshape(3, D_FF) for l in range(DEPTH)]
    w_s_t = [w_spatial[l].transpose(0, 2, 1) for l in range(DEPTH)]
    weights = [dict(), dict()]
    weights[0]["in_t"] = _rows(g_in0)
    saved = []
    xc = x0
    for l in range(DEPTH):
        p = weights[l]
        carry = _Gather([sh_br[0], sh_out[0]] if l == 0 else [sh_up[1]])
        (h, z), got = _norm_matmul(xc, norm1_g[l], p["in_t"], tm=tm, name=f"fwd_in_{l}", carry=carry)
        if l == 0:
            p["wb"], p["out"] = branch_weights(got[0]), _rows(got[1])
        else:
            p["up_t"] = _rows(got[0])
        carry = _Gather([sh_up[0]]) if l == 0 else None
        (ya, yb, conv, pa, pb, mg, x1), got = _mix_forward(
            z, xc, b_gate[l], gmlp_ln_g[l], gmlp_ln_b[l], w_spatial[l], b_spatial[l], w_sc[l], p["wb"], p["out"],
            tm=tm_mix, name=f"fwd_mix_{l}", carry=carry)
        if l == 0:
            p["up_t"] = _rows(got[0])
        carry = _Gather([sh_down[0], sh_in[1]] if l == 0 else [sh_down[1]])
        (h2, up), got = _norm_matmul(x1, norm2_g[l], p["up_t"], tm=tm, name=f"fwd_up_{l}", carry=carry)
        if l == 0:
            p["down"], weights[1]["in_t"] = _rows(got[0]), _rows(got[1])
        else:
            p["down"] = _rows(got[0])
        carry = _Gather([sh_br[1], sh_out[1]]) if l == 0 else None
        head = (final_g, target) if l == DEPTH - 1 else None
        outs, got = _ffn_forward(up, x1, w_fc[l], b_ffn_conv[l], p["down"], tm=tm, name=f"fwd_ffn_{l}", carry=carry, head=head)
        if l == 0:
            weights[1]["wb"], weights[1]["out"] = branch_weights(got[0]), _rows(got[1])
        gc, a = outs[0], outs[1]
        saved.append(dict(x=xc, h=h, z=z, ya=ya, yb=yb, conv=conv, pa=pa, pb=pb, mg=mg, x1=x1, h2=h2, up=up, gc=gc, a=a))
        xc = outs[2]
    dx, sheet = outs[2], outs[3]

    recv = [dict(), dict()]
    small_dws, small_dbs = [None] * DEPTH, [None] * DEPTH
    pending_in = None
    for l in reversed(range(DEPTH)):
        p, s = weights[l], saved[l]
        carry = _Exchange([pending_in]) if pending_in is not None else None
        (dup, sheet), got = _ffn_backward(dx, s["up"], s["gc"], w_fc[l], p["down"], sheet, l, tm=tm, name=f"bwd_ffn_{l}",
                                          carry=carry)
        if got is not None:
            recv[l + 1]["in_t"] = got[0]
        dw_down, _ = _matmul_tn(s["a"], dx, t1=D_FF // 2, tn=tn, name=f"dw_down_{l}")
        dw_up_t, got = _matmul_tn(dup, s["h2"], t1=2 * D_FF // 4, tn=tn, name=f"dw_up_{l}", pieces=2,
                                  carry=_Exchange([_parts(dw_down)]))
        recv[l]["down"] = got[0]
        (dx1, sheet), got_left = _matmul_norm_backward(
            dup, p["up_t"], s["x1"], norm2_g[l], dx, sheet, SG_LAYER * l + ROW_NORM2, tm=tm, name=f"bwd_up_{l}",
            carry=_Exchange([_parts(dw_up_t[0])]))
        (dz, dpa, dpb, small_dws[l], small_dbs[l], sheet), got_right = _mix_backward(
            dx1, s["z"], s["conv"], s["pa"], s["pb"], b_gate[l], gmlp_ln_g[l], gmlp_ln_b[l], w_spatial[l], w_s_t[l],
            b_spatial[l], w_sc[l], p["out"], p["wb"], sheet, l, tm=tm_mix, name=f"bwd_mix_{l}",
            carry=_Exchange([_parts(dw_up_t[1])]))
        recv[l]["up_t"] = [got_left[0], got_right[0]]
        dw_out, _ = _matmul_tn(s["mg"], dx1, t1=D_MODEL, tn=tn, name=f"dw_out_{l}")
        dw_bra_t, _ = _matmul_tn(dpa, s["ya"], t1=D_MODEL, tn=tn, name=f"dw_branch_a_{l}")
        dw_brb_t, _ = _matmul_tn(dpb, s["yb"], t1=D_MODEL, tn=tn, name=f"dw_branch_b_{l}")
        carry = _Exchange([_parts(dw_out), _parts(dw_bra_t), _parts(dw_brb_t)])
        if l == 0:
            dbs = jnp.stack([t.reshape(N_HEADS, GMLP_BLOCK) for t in small_dbs]).reshape(DEPTH * N_HEADS, GMLP_BLOCK)
            carry = _Both(carry, _Gather([sheet, small_dws[0], small_dws[1], dbs]))
        dw_in_t, got = _matmul_tn(dz, s["h"], t1=D_IN // 4, tn=tn, name=f"dw_in_{l}", carry=carry)
        recv[l]["out"], recv[l]["bra_t"], recv[l]["brb_t"] = got[:3]
        if l == 0:
            gathered_small = got[3:]
            (dx0, dg1_first), got = _matmul_norm_backward(dz, p["in_t"], s["x"], norm1_g[l], dx1, None, 0, tm=tm,
                                                         name=f"bwd_in_{l}", carry=_Exchange([_parts(dw_in_t)]))
            recv[0]["in_t"] = got[0]
        else:
            (dx0, sheet), _ = _matmul_norm_backward(dz, p["in_t"], s["x"], norm1_g[l], dx1, sheet,
                                                    SG_LAYER * l + ROW_NORM1, tm=tm, name=f"bwd_in_{l}")
            pending_in = _parts(dw_in_t)
        dx = dx0
    grad_x = dx.reshape(x.shape)

    results = {}
    both = lambda key: [recv[l][key] for l in range(DEPTH)]
    blocks = lambda key: [r if isinstance(r, list) else [r] for r in both(key)]
    swap = lambda t: t.transpose(0, 2, 1)
    for key, slab, (w, m, v), tr in [("w_in", "in_t", (w_in, m_w_in, v_w_in), 192),
                                     ("w_ffn_up", "up_t", (w_ffn_up, m_w_ffn_up, v_w_ffn_up), 176)]:
        outs = _sum_adamw(blocks(slab), swap(w), swap(m), swap(v), tr=tr, name=f"adamw_{key}")
        results[key] = tuple(swap(o) for o in outs)
    g_bra = _sum_parts(both("bra_t"), tr=128, name="sum_w_branch_a").transpose(0, 2, 1)
    g_brb = _sum_parts(both("brb_t"), tr=128, name="sum_w_branch_b").transpose(0, 2, 1)
    g_br = jnp.stack([g_bra, g_brb], axis=1)
    flat = lambda t: t.reshape(-1, t.shape[-1])
    outs = _adamw(flat(w_branch), flat(g_br), flat(m_w_branch), flat(v_w_branch), tr=512, name="adamw_w_branch")
    results["w_branch"] = (g_br,) + tuple(o.reshape(w_branch.shape) for o in outs)
    results["w_out"] = tuple(_sum_adamw(blocks("out"), w_out, m_w_out, v_w_out, tr=128, name="adamw_w_out"))
    results["w_ffn_down"] = tuple(_sum_adamw(blocks("down"), w_ffn_down, m_w_ffn_down, v_w_ffn_down, tr=176,
                                             name="adamw_w_ffn_down"))

    sheet, dws0, dws1, dbs = _sum_gathered(gathered_small, name="sum_small_grads")
    (dg1_first,) = _all_reduce_small([dg1_first], name="all_reduce_last_gain")
    sheet = sheet.at[ROW_NORM1, :D_MODEL].set(dg1_first[0])
    loss = sheet[ROW_LOSS, 0]
    taps = lambda row, width: jnp.stack([sheet[SG_LAYER * l + row:SG_LAYER * l + row + 3, :width] for l in range(DEPTH)])
    extra = dict(w_spatial=jnp.stack([dws0, dws1]), b_spatial=dbs.reshape(DEPTH, N_HEADS, GMLP_BLOCK),
                 w_shortconv=lax.dynamic_slice_in_dim(taps(ROW_SCONV, D_B), my_idx * sc_w, sc_w, axis=2),
                 w_ffn_conv=lax.dynamic_slice_in_dim(taps(ROW_FCONV, D_FF), my_idx * fc_w, fc_w, axis=2))
    small_w = dict(norm1_g=(norm1_g, m_norm1_g, v_norm1_g), b_gate=(b_gate, m_b_gate, v_b_gate),
                   gmlp_ln_g=(gmlp_ln_g, m_gmlp_ln_g, v_gmlp_ln_g), gmlp_ln_b=(gmlp_ln_b, m_gmlp_ln_b, v_gmlp_ln_b),
                   w_spatial=(w_spatial, m_w_spatial, v_w_spatial), b_spatial=(b_spatial, m_b_spatial, v_b_spatial),
                   w_shortconv=(w_shortconv, m_w_shortconv, v_w_shortconv), norm2_g=(norm2_g, m_norm2_g, v_norm2_g),
                   w_ffn_conv=(w_ffn_conv, m_w_ffn_conv, v_w_ffn_conv), b_ffn_conv=(b_ffn_conv, m_b_ffn_conv, v_b_ffn_conv),
                   final_g=tuple(t.reshape(1, D_MODEL) for t in (final_g, m_final_g, v_final_g)))
    results.update(_adamw_small(sheet, extra, small_w, name="adamw_small"))
    results["final_g"] = tuple(t.reshape(D_MODEL) for t in results["final_g"])

    names = ["norm1_g", "w_in", "b_gate", "gmlp_ln_g", "gmlp_ln_b", "w_spatial", "b_spatial", "w_shortconv", "w_branch",
             "w_out", "norm2_g", "w_ffn_up", "w_ffn_conv", "b_ffn_conv", "w_ffn_down", "final_g"]
    return (loss, grad_x, *[results[k][0] for k in names], *[results[k][1] for k in names],
            *[results[k][2] for k in names], *[results[k][3] for k in names])
```

```python
import math

import jax
import jax.numpy as jnp
from jax import lax
from jax.experimental import pallas as pl
from jax.experimental.pallas import tpu as pltpu

F32 = jnp.float32
BF16 = jnp.bfloat16

N_DEV = 8
DEPTH = 2
D_MODEL = 1024
D_A = 512
D_B = 512
D_FF = 2816
D_IN = 4608
N_HEADS = 4
HEAD = 128
GMLP_BLOCK = 128
CAUSAL_CHUNK = 64
OFF_U, OFF_V, OFF_BG, OFF_CG, OFF_HB, OFF_GA, OFF_GB = 0, 512, 1024, 1536, 2048, 2560, 3584
RMS_EPS = 1e-6
LN_EPS = 1e-5
ADAM_LR, ADAM_B1, ADAM_B2, ADAM_EPS, ADAM_WD, ADAM_STEP = 0.001, 0.9, 0.999, 1e-08, 0.01, 10

SUBLANES = 8
HALO = 16
FFN_CHUNK = 256
SG_ROWS, SG_W, SG_LAYER = 40, D_FF, 16
ROW_NORM1, ROW_BGATE, ROW_LN_G, ROW_LN_B, ROW_SCONV, ROW_NORM2, ROW_FCONV, ROW_BFCONV = 0, 1, 2, 3, 4, 7, 8, 11
ROW_FINAL, ROW_LOSS = 32, 33
V7X_VMEM_BYTES = 64 << 20
VMEM_LIMIT = V7X_VMEM_BYTES - (8 << 20)
MESH = pl.DeviceIdType.MESH
GELU_C0 = 0.7978845608028654
GELU_C1 = 0.044715
NT = (((1,), (1,)), ((), ()))
TN = (((0,), (0,)), ((), ()))


def _dot(a, b):
    return jnp.dot(a, b, preferred_element_type=F32)


def _dot_nt(a, b):
    return lax.dot_general(a, b, NT, preferred_element_type=F32)


def _sigmoid(x):
    return 1.0 / (1.0 + jnp.exp(-x))


def _gelu_tanh(x):
    return jnp.tanh(GELU_C0 * (x + GELU_C1 * x * x * x))


def _gelu_grad(x, t):
    return 0.5 * (1.0 + t) + 0.5 * x * (1.0 - t * t) * GELU_C0 * (1.0 + 3.0 * GELU_C1 * x * x)


def _sublane_tile(dtype):
    return SUBLANES * (4 // jnp.dtype(dtype).itemsize)


def _shift_down(a, k, prev):
    p = prev.shape[0]
    r = pltpu.roll(a, k, 0)
    sub = _sublane_tile(a.dtype)
    head = r[0:sub]
    rid = lax.broadcasted_iota(jnp.int32, head.shape, 0)
    for j in range(k):
        head = jnp.where(rid == j, prev[p - k + j:p - k + j + 1, :], head)
    return jnp.concatenate([head, r[sub:]], axis=0)


def _shift_up(a, k, nxt):
    t = a.shape[0]
    r = pltpu.roll(a, t - k, 0)
    sub = _sublane_tile(a.dtype)
    tail = r[t - sub:t]
    rid = lax.broadcasted_iota(jnp.int32, tail.shape, 0)
    for j in range(k):
        tail = jnp.where(rid == sub - k + j, nxt[j:j + 1, :], tail)
    return jnp.concatenate([r[0:t - sub], tail], axis=0)


def _column_sums(p):
    if p.dtype.itemsize < 4:
        t = p.shape[0]
        p = p[:t // 2] + p[t // 2:]
        p = p[:t // 4] + p[t // 4:]
    return jnp.sum(p.astype(F32), axis=0, keepdims=True)


def _sheet_begin(step, sheet_in, sheet_out, first_row, rows):
    @pl.when(step == 0)
    def _():
        sheet_out[...] = sheet_in[...]
        sheet_out[first_row:first_row + rows, :] = jnp.zeros((rows, SG_W), F32)


def _sheet_spec():
    return pl.BlockSpec((SG_ROWS, SG_W), lambda i: (0, 0))


def _spatial_mask(transposed):
    ri = lax.broadcasted_iota(jnp.int32, (GMLP_BLOCK, GMLP_BLOCK), 0) // CAUSAL_CHUNK
    ci = lax.broadcasted_iota(jnp.int32, (GMLP_BLOCK, GMLP_BLOCK), 1) // CAUSAL_CHUNK
    return (ri <= ci) if transposed else (ci <= ri)


def _gmlp_forward(u, v, ln_g, ln_b, ws_ref, bs_ref, f_scr):
    tm = u.shape[0]
    tu = _gelu_tanh(u)
    tv = _gelu_tanh(v)
    gu = 0.5 * u * (1.0 + tu)
    gv = 0.5 * v * (1.0 + tv)
    mu = jnp.mean(gv, axis=-1, keepdims=True)
    cen = gv - mu
    rstd = lax.rsqrt(jnp.mean(cen * cen, axis=-1, keepdims=True) + LN_EPS)
    xh = cen * rstd
    vn = (xh * ln_g + ln_b).astype(BF16)
    mask = _spatial_mask(False)
    wm = [jnp.where(mask, ws_ref[h], 0.0).astype(BF16) for h in range(N_HEADS)]
    for b in range(tm // GMLP_BLOCK):
        rows = slice(b * GMLP_BLOCK, (b + 1) * GMLP_BLOCK)
        for h in range(N_HEADS):
            cols = slice(h * HEAD, (h + 1) * HEAD)
            f_scr[rows, cols] = (_dot(wm[h], vn[rows, cols]) + bs_ref[h]).astype(f_scr.dtype)
    return gu, tu, tv, xh, rstd, vn, f_scr[...]


def _position():
    return lax.axis_index("x"), lax.axis_index("y"), lax.axis_index("c")


class _Gather:
    def __init__(self, arrays):
        self.arrays = list(arrays)
        self.out_shape = [jax.ShapeDtypeStruct((N_DEV,) + a.shape, a.dtype) for a in self.arrays]
        self.base = 0

    def _plan(self, ins, outs, sems):
        send_sems, recv_sems, local_sems = sems
        x, y, c = _position()
        me, sibling = (x, y, c), (x, y, 1 - c)
        chips = [(1 - x, y), (x, 1 - y), (1 - x, 1 - y)]

        def slot(a, p):
            return outs[a].at[4 * p[0] + 2 * p[1] + p[2]]

        def copy(a, k, block, to, src=None):
            return pltpu.make_async_remote_copy(
                src_ref=slot(a, block) if src is None else src, dst_ref=slot(a, block),
                send_sem=send_sems.at[self.base + a, k], recv_sem=recv_sems.at[self.base + a, k],
                device_id=to, device_id_type=MESH)

        n = len(self.arrays)

        def mine():
            return [pltpu.make_async_copy(ins[a], slot(a, me), local_sems.at[self.base + a]) for a in range(n)]

        def first():
            out = []
            for a in range(n):
                out.append(copy(a, 0, me, sibling, src=ins[a]))
                out += [copy(a, 1 + j, me, (*chip, c), src=ins[a]) for j, chip in enumerate(chips)]
            return out

        def arrivals():
            return [copy(a, 1 + j, (*chip, c), me) for j, chip in enumerate(chips) for a in range(n)]

        def relays():
            return [copy(a, 4 + j, (*chip, c), sibling) for j, chip in enumerate(chips) for a in range(n)]

        def from_sibling():
            out = [copy(a, 0, sibling, me) for a in range(n)]
            return out + [copy(a, 4 + j, (*chip, 1 - c), me) for j, chip in enumerate(chips) for a in range(n)]

        return mine, first, arrivals, relays, from_sibling

    def start(self, ins, outs, sems):
        mine, first, _, _, _ = self._plan(ins, outs, sems)
        for cp in mine() + first():
            cp.start()

    def relay(self, ins, outs, sems):
        _, _, arrivals, relays, _ = self._plan(ins, outs, sems)
        for arrived, onward in zip(arrivals(), relays()):
            arrived.wait_recv()
            onward.start()

    def finish(self, ins, outs, sems):
        mine, first, _, relays, from_sibling = self._plan(ins, outs, sems)
        for cp in from_sibling():
            cp.wait_recv()
        for cp in first() + relays():
            cp.wait_send()
        for cp in mine():
            cp.wait()


class _Exchange:
    def __init__(self, arrays):
        self.arrays = list(arrays)
        self.out_shape = [jax.ShapeDtypeStruct(a.shape, a.dtype) for a in self.arrays]
        self.base = 0

    def _plan(self, ins, outs, sems):
        send_sems, recv_sems, local_sems = sems
        x, y, c = _position()
        my_idx = 4 * x + 2 * y + c
        n = len(self.arrays)
        offsets = [(dx, dy, dc) for dx in (0, 1) for dy in (0, 1) for dc in (0, 1) if (dx, dy, dc) != (0, 0, 0)]

        def mine():
            return [pltpu.make_async_copy(ins[a].at[my_idx], outs[a].at[my_idx], local_sems.at[self.base + a])
                    for a in range(n)]

        def remote(arriving):
            out = []
            for k, (dx, dy, dc) in enumerate(offsets):
                px, py, pc = x ^ dx, y ^ dy, c ^ dc
                p_idx = 4 * px + 2 * py + pc
                for a in range(n):
                    out.append(pltpu.make_async_remote_copy(
                        src_ref=ins[a].at[p_idx], dst_ref=outs[a].at[p_idx if arriving else my_idx],
                        send_sem=send_sems.at[self.base + a, k], recv_sem=recv_sems.at[self.base + a, k],
                        device_id=(px, py, pc), device_id_type=MESH))
            return out

        return mine, remote

    def start(self, ins, outs, sems):
        mine, remote = self._plan(ins, outs, sems)
        for cp in mine() + remote(False):
            cp.start()

    def relay(self, ins, outs, sems):
        pass

    def finish(self, ins, outs, sems):
        mine, remote = self._plan(ins, outs, sems)
        for cp in remote(True):
            cp.wait_recv()
        for cp in remote(False):
            cp.wait_send()
        for cp in mine():
            cp.wait()


class _Both:
    def __init__(self, *carries):
        self.carries = carries
        self.arrays = [a for c in carries for a in c.arrays]
        self.out_shape = [s for c in carries for s in c.out_shape]
        first = 0
        for c in carries:
            c.base = first
            first += len(c.arrays)

    def _each(self, method, ins, outs, sems):
        for c in self.carries:
            rows = slice(c.base, c.base + len(c.arrays))
            getattr(c, method)(ins[rows], outs[rows], sems)

    def start(self, ins, outs, sems):
        self._each("start", ins, outs, sems)

    def relay(self, ins, outs, sems):
        self._each("relay", ins, outs, sems)

    def finish(self, ins, outs, sems):
        self._each("finish", ins, outs, sems)


def _call(body, *, name, grid, in_specs, out_specs, out_shape, args, scratch_shapes=(), carry=None):
    n_in, n_out, n_scr = len(in_specs), len(out_specs), len(scratch_shapes)
    params = pltpu.CompilerParams(dimension_semantics=("arbitrary",) * len(grid), vmem_limit_bytes=VMEM_LIMIT)
    if carry is None:
        outs = pl.pallas_call(body, name=name, grid=grid, in_specs=in_specs, out_specs=out_specs, out_shape=out_shape,
                              scratch_shapes=list(scratch_shapes), compiler_params=params)(*args)
        return outs, None
    m = len(carry.arrays)
    total = math.prod(grid)

    def wrapped(*refs):
        ins, refs = refs[:n_in], refs[n_in:]
        c_ins, refs = refs[:m], refs[m:]
        outs, refs = refs[:n_out], refs[n_out:]
        c_outs, refs = refs[:m], refs[m:]
        scr, sems = refs[:n_scr], refs[n_scr:]
        flat = pl.program_id(0)
        for d in range(1, len(grid)):
            flat = flat * grid[d] + pl.program_id(d)

        @pl.when(flat == 0)
        def _():
            carry.start(c_ins, c_outs, sems)

        body(*ins, *outs, *scr)

        @pl.when(flat == total - 2)
        def _():
            carry.relay(c_ins, c_outs, sems)

        @pl.when(flat == total - 1)
        def _():
            carry.finish(c_ins, c_outs, sems)

    any_spec = pl.BlockSpec(memory_space=pl.ANY)
    sem_shapes = [pltpu.SemaphoreType.DMA((m, 7)), pltpu.SemaphoreType.DMA((m, 7)), pltpu.SemaphoreType.DMA((m,))]
    outs = pl.pallas_call(
        wrapped, name=name, grid=grid,
        in_specs=list(in_specs) + [any_spec] * m, out_specs=list(out_specs) + [any_spec] * m,
        out_shape=list(out_shape) + carry.out_shape,
        scratch_shapes=list(scratch_shapes) + sem_shapes, compiler_params=params)(*args, *carry.arrays)
    return outs[:n_out], outs[n_out:]


def _gather_now(arrays, *, name):
    carry = _Gather(arrays)
    m = len(arrays)

    def body(*refs):
        ins, outs, sems = refs[:m], refs[m:2 * m], refs[2 * m:]
        carry.start(ins, outs, sems)
        carry.relay(ins, outs, sems)
        carry.finish(ins, outs, sems)

    any_spec = pl.BlockSpec(memory_space=pl.ANY)
    return pl.pallas_call(
        body, name=name, in_specs=[any_spec] * m, out_specs=[any_spec] * m, out_shape=carry.out_shape,
        scratch_shapes=[pltpu.SemaphoreType.DMA((m, 7)), pltpu.SemaphoreType.DMA((m, 7)),
                        pltpu.SemaphoreType.DMA((m,))],
    )(*arrays)


def _all_reduce_small(arrs, *, name):
    n = len(arrs)

    def body(*refs):
        ins, outs, bufs = refs[:n], refs[n:2 * n], refs[2 * n:3 * n]
        send_sems, recv_sems = refs[3 * n:]
        x, y, c = _position()
        me, sibling = (x, y, c), (x, y, 1 - c)
        chips = [(1 - x, y), (x, 1 - y), (1 - x, 1 - y)]

        def copy(a, k, block, to, src=None):
            slot = bufs[a].at[4 * block[0] + 2 * block[1] + block[2]]
            return pltpu.make_async_remote_copy(
                src_ref=slot if src is None else src, dst_ref=slot,
                send_sem=send_sems.at[a, k], recv_sem=recv_sems.at[a, k], device_id=to, device_id_type=MESH)

        first = []
        for a in range(n):
            first.append(copy(a, 0, me, sibling, src=ins[a]))
            first += [copy(a, 1 + j, me, (*chip, c), src=ins[a]) for j, chip in enumerate(chips)]
        for cp in first:
            cp.start()
        passed = []
        for j, chip in enumerate(chips):
            for a in range(n):
                copy(a, 1 + j, (*chip, c), me).wait_recv()
                cp = copy(a, 4 + j, (*chip, c), sibling)
                cp.start()
                passed.append(cp)
        for a in range(n):
            copy(a, 0, sibling, me).wait_recv()
            for j, chip in enumerate(chips):
                copy(a, 4 + j, (*chip, 1 - c), me).wait_recv()
        for cp in first + passed:
            cp.wait_send()
        my_idx = 4 * x + 2 * y + c
        for a in range(n):
            acc = jnp.zeros(ins[a].shape, F32)
            for s in range(N_DEV):
                acc = acc + jnp.where(my_idx == s, ins[a][...], bufs[a][s])
            outs[a][...] = acc

    vmem = pl.BlockSpec(memory_space=pltpu.VMEM)
    return pl.pallas_call(
        body, name=name, in_specs=[vmem] * n, out_specs=[vmem] * n,
        out_shape=[jax.ShapeDtypeStruct(a.shape, F32) for a in arrs],
        scratch_shapes=[pltpu.VMEM((N_DEV,) + a.shape, F32) for a in arrs]
        + [pltpu.SemaphoreType.DMA((n, 7)), pltpu.SemaphoreType.DMA((n, 7))],
        compiler_params=pltpu.CompilerParams(vmem_limit_bytes=VMEM_LIMIT),
    )(*arrs)


def _sum_gathered(arrs, *, name):
    n = len(arrs)

    def body(*refs):
        for in_ref, out_ref in zip(refs[:n], refs[n:]):
            acc = in_ref[0]
            for s in range(1, N_DEV):
                acc = acc + in_ref[s]
            out_ref[...] = acc

    vmem = pl.BlockSpec(memory_space=pltpu.VMEM)
    return pl.pallas_call(
        body, name=name, in_specs=[vmem] * n, out_specs=[vmem] * n,
        out_shape=[jax.ShapeDtypeStruct(a.shape[1:], F32) for a in arrs],
        compiler_params=pltpu.CompilerParams(vmem_limit_bytes=VMEM_LIMIT),
    )(*arrs)


def _norm_matmul(x, g, w_t, *, tm, name, carry=None):
    n, d = x.shape
    c = w_t.shape[0]
    ch = 512

    def body(x_ref, g_ref, wt_ref, h_ref, z_ref):
        xv = x_ref[...]
        r = lax.rsqrt(jnp.mean(xv * xv, axis=-1, keepdims=True) + RMS_EPS)
        h = (xv * r * g_ref[...]).astype(BF16)
        h_ref[...] = h
        for c0 in range(0, c, ch):
            z_ref[:, c0:c0 + ch] = _dot_nt(h, wt_ref[c0:c0 + ch, :]).astype(BF16)

    return _call(
        body, name=name, grid=(n // tm,), carry=carry,
        in_specs=[pl.BlockSpec((tm, d), lambda i: (i, 0)),
                  pl.BlockSpec((1, d), lambda i: (0, 0)),
                  pl.BlockSpec((c, d), lambda i: (0, 0))],
        out_specs=[pl.BlockSpec((tm, d), lambda i: (i, 0)),
                   pl.BlockSpec((tm, c), lambda i: (i, 0))],
        out_shape=[jax.ShapeDtypeStruct((n, d), BF16), jax.ShapeDtypeStruct((n, c), BF16)],
        args=(x, g.reshape(1, d), w_t))


def _mix_forward(z, x, b_gate, ln_g, ln_b, w_s, b_s, w_sc, wb, w_out, *, tm, name, carry=None):
    n = z.shape[0]
    hb = tm // HALO

    def body(z_ref, zp_ref, x_ref, bg_ref, lng_ref, lnb_ref, ws_ref, bs_ref, wsc_ref, wb_ref, wo_ref,
             ya_ref, yb_ref, cv_ref, pa_ref, pb_ref, mg_ref, x1_ref, f_scr):
        i = pl.program_id(0)
        u = z_ref[:, OFF_U:OFF_U + D_A]
        v = z_ref[:, OFF_V:OFF_V + D_A].astype(F32)
        gu, _, _, _, _, _, f = _gmlp_forward(u, v, lng_ref[...], lnb_ref[...], ws_ref, bs_ref, f_scr)
        ya = gu * f
        ya_ref[...] = ya

        q = z_ref[:, OFF_CG:OFF_CG + D_B] * z_ref[:, OFF_HB:OFF_HB + D_B]
        qp = zp_ref[:, OFF_CG:OFF_CG + D_B] * zp_ref[:, OFF_HB:OFF_HB + D_B]
        qp = jnp.where(i > 0, qp, jnp.zeros_like(qp))
        w = wsc_ref[...].astype(BF16)
        conv = w[0:1] * _shift_down(q, 2, qp) + w[1:2] * _shift_down(q, 1, qp) + w[2:3] * q
        cv_ref[...] = conv
        yb = z_ref[:, OFF_BG:OFF_BG + D_B] * conv
        yb_ref[...] = yb

        pa = _dot(ya, wb_ref[0]).astype(BF16)
        pb = _dot(yb, wb_ref[1]).astype(BF16)
        pa_ref[...] = pa
        pb_ref[...] = pb
        bg = bg_ref[...].astype(BF16)
        sa = _sigmoid(z_ref[:, OFF_GA:OFF_GA + D_MODEL] + bg[:, 0:D_MODEL])
        sb = _sigmoid(z_ref[:, OFF_GB:OFF_GB + D_MODEL] + bg[:, D_MODEL:2 * D_MODEL])
        mg = sa * pa + sb * pb
        mg_ref[...] = mg
        x1_ref[...] = x_ref[...] + _dot(mg, wo_ref[...])

    row = lambda w: pl.BlockSpec((tm, w), lambda i: (i, 0))
    full = lambda *s: pl.BlockSpec(s, lambda i: (0,) * len(s))
    bf = lambda w: jax.ShapeDtypeStruct((n, w), BF16)
    return _call(
        body, name=name, grid=(n // tm,), carry=carry,
        in_specs=[row(D_IN),
                  pl.BlockSpec((HALO, D_IN), lambda i: (jnp.maximum(i * hb - 1, 0), 0)),
                  row(D_MODEL), full(1, 2 * D_MODEL), full(1, D_A), full(1, D_A),
                  full(N_HEADS, GMLP_BLOCK, GMLP_BLOCK), full(N_HEADS, GMLP_BLOCK, 1), full(3, D_B),
                  full(2, D_A, D_MODEL), full(D_MODEL, D_MODEL)],
        out_specs=[row(D_A), row(D_B), row(D_B), row(D_MODEL), row(D_MODEL), row(D_MODEL), row(D_MODEL)],
        out_shape=[bf(D_A), bf(D_B), bf(D_B), bf(D_MODEL), bf(D_MODEL), bf(D_MODEL),
                   jax.ShapeDtypeStruct((n, D_MODEL), F32)],
        scratch_shapes=[pltpu.VMEM((tm, D_A), BF16)],
        args=(z, z, x, b_gate.reshape(1, -1), ln_g.reshape(1, -1), ln_b.reshape(1, -1), w_s,
              b_s.reshape(N_HEADS, GMLP_BLOCK, 1), w_sc, wb, w_out))


def _loss_tile(xv, gv, tv):
    d = xv.shape[-1]
    r = lax.rsqrt(jnp.mean(xv * xv, axis=-1, keepdims=True) + RMS_EPS)
    xh = xv * r
    e = xh * gv - tv
    per_row = jnp.sum(e * e, axis=-1, keepdims=True) * (0.5 / d)
    dy = e * (1.0 / d)
    dxh = dy * gv
    dx = r * (dxh - xh * jnp.mean(dxh * xh, axis=-1, keepdims=True))
    return dx, jnp.sum(per_row, axis=0, keepdims=True), jnp.sum(dy * xh, axis=0, keepdims=True)


def _ffn_forward(up, x1, w_fc, b_fc, w_down, *, tm, name, carry=None, head=None):
    n = up.shape[0]
    hb = tm // HALO
    n_in = 6 if head is None else 8

    def body(*refs):
        up_ref, upp_ref, x1_ref, wfc_ref, bfc_ref, wd_ref = refs[:6]
        gc_ref, a_ref, out_ref = refs[n_in:n_in + 3]
        acc = refs[-1]
        i = pl.program_id(0)
        acc[...] = x1_ref[...]
        for c0 in range(0, D_FF, FFN_CHUNK):
            cols = slice(c0, c0 + FFN_CHUNK)
            gate = up_ref[:, cols]
            val = up_ref[:, D_FF + c0:D_FF + c0 + FFN_CHUNK]
            gp = upp_ref[:, cols]
            gp = jnp.where(i > 0, gp, jnp.zeros_like(gp))
            w = wfc_ref[:, cols].astype(BF16)
            gc = (w[0:1] * _shift_down(gate, 2, gp) + w[1:2] * _shift_down(gate, 1, gp) + w[2:3] * gate
                  + bfc_ref[:, cols].astype(BF16))
            gc_ref[:, cols] = gc
            a = gc * _sigmoid(gc) * val
            a_ref[:, cols] = a
            acc[...] += _dot(a, wd_ref[cols, :])
        if head is None:
            out_ref[...] = acc[...]
        else:
            g_ref, t_ref = refs[6:8]
            sg_ref = refs[n_in + 3]

            @pl.when(i == 0)
            def _():
                sg_ref[...] = jnp.zeros_like(sg_ref)

            dx, loss, dg = _loss_tile(acc[...], g_ref[...], t_ref[...])
            out_ref[...] = dx
            sg_ref[ROW_LOSS:ROW_LOSS + 1, 0:128] += jnp.broadcast_to(loss, (1, 128))
            sg_ref[ROW_FINAL:ROW_FINAL + 1, 0:D_MODEL] += dg

    row = lambda w: pl.BlockSpec((tm, w), lambda i: (i, 0))
    full = lambda r, c: pl.BlockSpec((r, c), lambda i: (0, 0))
    in_specs = [row(2 * D_FF), pl.BlockSpec((HALO, D_FF), lambda i: (jnp.maximum(i * hb - 1, 0), 0)), row(D_MODEL),
                full(3, D_FF), full(1, D_FF), full(D_FF, D_MODEL)]
    out_specs = [row(D_FF), row(D_FF), row(D_MODEL)]
    out_shape = [jax.ShapeDtypeStruct((n, D_FF), BF16), jax.ShapeDtypeStruct((n, D_FF), BF16),
                 jax.ShapeDtypeStruct((n, D_MODEL), F32)]
    args = (up, up, x1, w_fc, b_fc.reshape(1, -1), w_down)
    if head is not None:
        in_specs += [full(1, D_MODEL), row(D_MODEL)]
        out_specs += [full(SG_ROWS, SG_W)]
        out_shape += [jax.ShapeDtypeStruct((SG_ROWS, SG_W), F32)]
        args += (head[0].reshape(1, -1), head[1])
    return _call(body, name=name, grid=(n // tm,), carry=carry, in_specs=in_specs, out_specs=out_specs,
                 out_shape=out_shape, scratch_shapes=[pltpu.VMEM((tm, D_MODEL), F32)], args=args)


def _ffn_backward(dx2, up, gc, w_fc, w_down, sheet, layer, *, tm, name, carry=None):
    n = up.shape[0]
    steps = n // tm
    hb = tm // HALO
    row = SG_LAYER * layer + ROW_FCONV

    def body(dx_ref, dxn_ref, up_ref, upn_ref, gc_ref, gcn_ref, wfc_ref, wd_ref, sg_in, dup_ref, sg_ref):
        i = pl.program_id(0)
        last = i == steps - 1
        _sheet_begin(i, sg_in, sg_ref, row, 4)

        dxe = jnp.concatenate([dx_ref[...], dxn_ref[...]], axis=0).astype(BF16)
        for c0 in range(0, D_FF, FFN_CHUNK):
            cols = slice(c0, c0 + FFN_CHUNK)
            vcols = slice(D_FF + c0, D_FF + c0 + FFN_CHUNK)
            dae = _dot_nt(dxe, wd_ref[cols, :])
            da, dan = dae[:tm], dae[tm:]
            gate = up_ref[:, cols]
            val = up_ref[:, vcols]
            gcv = gc_ref[:, cols]
            s = _sigmoid(gcv)
            dab = da.astype(BF16)
            dup_ref[:, vcols] = dab * (gcv * s)
            dgc = dab * val * (s * (1.0 + gcv * (1.0 - s)))
            gcn = gcn_ref[:, cols]
            sn = _sigmoid(gcn)
            dgcn = dan.astype(BF16) * upn_ref[:, vcols] * (sn * (1.0 + gcn * (1.0 - sn)))
            dgcn = jnp.where(last, jnp.zeros_like(dgcn), dgcn)
            up1 = _shift_up(dgc, 1, dgcn)
            up2 = _shift_up(dgc, 2, dgcn)
            w = wfc_ref[:, cols].astype(BF16)
            dup_ref[:, cols] = w[2:3] * dgc + w[1:2] * up1 + w[0:1] * up2
            sg_ref[row:row + 1, cols] += _column_sums(gate * up2)
            sg_ref[row + 1:row + 2, cols] += _column_sums(gate * up1)
            sg_ref[row + 2:row + 3, cols] += _column_sums(gate * dgc)
            sg_ref[row + 3:row + 4, cols] += _column_sums(dgc)

    nxt = lambda i: (jnp.minimum((i + 1) * hb, steps * hb - 1), 0)
    return _call(
        body, name=name, grid=(steps,), carry=carry,
        in_specs=[pl.BlockSpec((tm, D_MODEL), lambda i: (i, 0)),
                  pl.BlockSpec((HALO, D_MODEL), nxt),
                  pl.BlockSpec((tm, 2 * D_FF), lambda i: (i, 0)),
                  pl.BlockSpec((HALO, 2 * D_FF), nxt),
                  pl.BlockSpec((tm, D_FF), lambda i: (i, 0)),
                  pl.BlockSpec((HALO, D_FF), nxt),
                  pl.BlockSpec((3, D_FF), lambda i: (0, 0)),
                  pl.BlockSpec((D_FF, D_MODEL), lambda i: (0, 0)), _sheet_spec()],
        out_specs=[pl.BlockSpec((tm, 2 * D_FF), lambda i: (i, 0)), _sheet_spec()],
        out_shape=[jax.ShapeDtypeStruct((n, 2 * D_FF), BF16), jax.ShapeDtypeStruct((SG_ROWS, SG_W), F32)],
        args=(dx2, dx2, up, up, gc, gc, w_fc, w_down, sheet))


def _matmul_norm_backward(dz, w_t, x, g, dres, sheet, row, *, tm, name, carry=None):
    n, c = dz.shape
    d = x.shape[1]
    ch = 512

    def body(dz_ref, wt_ref, x_ref, g_ref, dres_ref, *rest):
        i = pl.program_id(0)
        if sheet is None:
            dx_ref, sg_ref = rest

            @pl.when(i == 0)
            def _():
                sg_ref[...] = jnp.zeros_like(sg_ref)
        else:
            sg_in, dx_ref, sg_ref = rest
            _sheet_begin(i, sg_in, sg_ref, row, 1)

        dh = _dot(dz_ref[:, 0:ch], wt_ref[0:ch, :])
        for c0 in range(ch, c, ch):
            dh += _dot(dz_ref[:, c0:c0 + ch], wt_ref[c0:c0 + ch, :])
        xv = x_ref[...]
        r = lax.rsqrt(jnp.mean(xv * xv, axis=-1, keepdims=True) + RMS_EPS)
        xh = xv * r
        sg_ref[row:row + 1, 0:d] += jnp.sum(dh * xh, axis=0, keepdims=True)
        dxh = dh * g_ref[...]
        dx_ref[...] = dres_ref[...] + r * (dxh - xh * jnp.mean(dxh * xh, axis=-1, keepdims=True))

    in_specs = [pl.BlockSpec((tm, c), lambda i: (i, 0)),
                pl.BlockSpec((c, d), lambda i: (0, 0)),
                pl.BlockSpec((tm, d), lambda i: (i, 0)),
                pl.BlockSpec((1, d), lambda i: (0, 0)),
                pl.BlockSpec((tm, d), lambda i: (i, 0))]
    args = (dz, w_t, x, g.reshape(1, d), dres)
    if sheet is None:
        small_spec, small_shape = pl.BlockSpec((8, d), lambda i: (0, 0)), jax.ShapeDtypeStruct((8, d), F32)
    else:
        in_specs, args = in_specs + [_sheet_spec()], args + (sheet,)
        small_spec, small_shape = _sheet_spec(), jax.ShapeDtypeStruct((SG_ROWS, SG_W), F32)
    return _call(
        body, name=name, grid=(n // tm,), carry=carry, in_specs=in_specs,
        out_specs=[pl.BlockSpec((tm, d), lambda i: (i, 0)), small_spec],
        out_shape=[jax.ShapeDtypeStruct((n, d), F32), small_shape], args=args)


def _mix_backward(dx1, z, conv, pa, pb, b_gate, ln_g, ln_b, w_s, w_s_t, b_s, w_sc, w_out, wb, sheet, layer, *, tm, name,
                  carry=None):
    n = z.shape[0]
    steps = n // tm
    hb = tm // HALO
    base = SG_LAYER * layer
    r_bg, r_lng, r_lnb, r_sc = base + ROW_BGATE, base + ROW_LN_G, base + ROW_LN_B, base + ROW_SCONV

    def body(dx_ref, dxn_ref, z_ref, zn_ref, cv_ref, pa_ref, pb_ref, bg_ref, lng_ref, lnb_ref, ws_ref, wst_ref,
             bs_ref, wsc_ref, wo_ref, wb_ref, sg_in,
             dz_ref, dpa_ref, dpb_ref, dws_ref, dbs_ref, sg_ref, f_scr, dvn_scr):
        i = pl.program_id(0)
        last = i == steps - 1
        _sheet_begin(i, sg_in, sg_ref, r_bg, ROW_NORM2 - ROW_BGATE)

        @pl.when(i == 0)
        def _():
            dws_ref[...] = jnp.zeros_like(dws_ref)
            dbs_ref[...] = jnp.zeros_like(dbs_ref)

        dxe = jnp.concatenate([dx_ref[...], dxn_ref[...]], axis=0).astype(BF16)
        dmge = _dot_nt(dxe, wo_ref[...])
        dmg, dmgn = dmge[:tm].astype(BF16), dmge[tm:].astype(BF16)

        pa_v = pa_ref[...]
        pb_v = pb_ref[...]
        bg = bg_ref[...].astype(BF16)
        sa = _sigmoid(z_ref[:, OFF_GA:OFF_GA + D_MODEL] + bg[:, 0:D_MODEL])
        sb = _sigmoid(z_ref[:, OFF_GB:OFF_GB + D_MODEL] + bg[:, D_MODEL:2 * D_MODEL])
        dpa = dmg * sa
        dpb = dmg * sb
        dga = dmg * pa_v * sa * (1.0 - sa)
        dgb = dmg * pb_v * sb * (1.0 - sb)
        dpa_ref[...] = dpa
        dpb_ref[...] = dpb
        dz_ref[:, OFF_GA:OFF_GA + D_MODEL] = dga
        dz_ref[:, OFF_GB:OFF_GB + D_MODEL] = dgb
        sg_ref[r_bg:r_bg + 1, 0:D_MODEL] += _column_sums(dga)
        sg_ref[r_bg:r_bg + 1, D_MODEL:2 * D_MODEL] += _column_sums(dgb)

        dya = _dot_nt(dpa, wb_ref[0]).astype(BF16)
        u = z_ref[:, OFF_U:OFF_U + D_A]
        v = z_ref[:, OFF_V:OFF_V + D_A].astype(F32)
        ln_g = lng_ref[...]
        gu, tu, tv, xh, rstd, vn, f = _gmlp_forward(u, v, ln_g, lnb_ref[...], ws_ref, bs_ref, f_scr)
        dgu = dya * f
        df_bf = dya * gu
        dz_ref[:, OFF_U:OFF_U + D_A] = dgu * _gelu_grad(u, tu)
        mask = _spatial_mask(False)
        mask_t = _spatial_mask(True)
        wmt = [jnp.where(mask_t, wst_ref[h], 0.0).astype(BF16) for h in range(N_HEADS)]
        for b in range(tm // GMLP_BLOCK):
            rows = slice(b * GMLP_BLOCK, (b + 1) * GMLP_BLOCK)
            for h in range(N_HEADS):
                cols = slice(h * HEAD, (h + 1) * HEAD)
                dfb = df_bf[rows, cols]
                dvn_scr[rows, cols] = _dot(wmt[h], dfb)
                dws_ref[h] += jnp.where(mask, _dot_nt(dfb, vn[rows, cols]), 0.0)
                dbs_ref[h] += jnp.sum(dfb.astype(F32), axis=1, keepdims=True)
        dvn = dvn_scr[...]
        sg_ref[r_lng:r_lng + 1, 0:D_A] += jnp.sum(dvn * xh, axis=0, keepdims=True)
        sg_ref[r_lnb:r_lnb + 1, 0:D_A] += jnp.sum(dvn, axis=0, keepdims=True)
        dxh = dvn * ln_g
        dgv = rstd * (dxh - jnp.mean(dxh, axis=-1, keepdims=True) - xh * jnp.mean(dxh * xh, axis=-1, keepdims=True))
        dz_ref[:, OFF_V:OFF_V + D_A] = (dgv * _gelu_grad(v, tv)).astype(BF16)

        sbn = _sigmoid(zn_ref[:, OFF_GB:OFF_GB + D_MODEL] + bg[:, D_MODEL:2 * D_MODEL])
        dpbe = jnp.concatenate([dpb, dmgn * sbn], axis=0)
        dybe = _dot_nt(dpbe, wb_ref[1])
        dyb, dybn = dybe[:tm].astype(BF16), dybe[tm:].astype(BF16)
        bgv = z_ref[:, OFF_BG:OFF_BG + D_B]
        cg = z_ref[:, OFF_CG:OFF_CG + D_B]
        hbv = z_ref[:, OFF_HB:OFF_HB + D_B]
        q = cg * hbv
        dz_ref[:, OFF_BG:OFF_BG + D_B] = dyb * cv_ref[...]
        dconv = dyb * bgv
        dconvn = dybn * zn_ref[:, OFF_BG:OFF_BG + D_B]
        dconvn = jnp.where(last, jnp.zeros_like(dconvn), dconvn)
        up1 = _shift_up(dconv, 1, dconvn)
        up2 = _shift_up(dconv, 2, dconvn)
        sg_ref[r_sc:r_sc + 1, 0:D_B] += _column_sums(q * up2)
        sg_ref[r_sc + 1:r_sc + 2, 0:D_B] += _column_sums(q * up1)
        sg_ref[r_sc + 2:r_sc + 3, 0:D_B] += _column_sums(q * dconv)
        w = wsc_ref[...].astype(BF16)
        dq = w[2:3] * dconv + w[1:2] * up1 + w[0:1] * up2
        dz_ref[:, OFF_CG:OFF_CG + D_B] = dq * hbv
        dz_ref[:, OFF_HB:OFF_HB + D_B] = dq * cg

    row = lambda w: pl.BlockSpec((tm, w), lambda i: (i, 0))
    full = lambda *s: pl.BlockSpec(s, lambda i: (0,) * len(s))
    nxt = lambda i: (jnp.minimum((i + 1) * hb, steps * hb - 1), 0)
    return _call(
        body, name=name, grid=(steps,), carry=carry,
        in_specs=[row(D_MODEL), pl.BlockSpec((HALO, D_MODEL), nxt),
                  row(D_IN), pl.BlockSpec((HALO, D_IN), nxt),
                  row(D_B), row(D_MODEL), row(D_MODEL),
                  full(1, 2 * D_MODEL), full(1, D_A), full(1, D_A),
                  full(N_HEADS, GMLP_BLOCK, GMLP_BLOCK), full(N_HEADS, GMLP_BLOCK, GMLP_BLOCK),
                  full(N_HEADS, GMLP_BLOCK, 1), full(3, D_B),
                  full(D_MODEL, D_MODEL), full(2, D_A, D_MODEL), _sheet_spec()],
        out_specs=[row(D_IN), row(D_MODEL), row(D_MODEL), full(N_HEADS, GMLP_BLOCK, GMLP_BLOCK),
                   full(N_HEADS, GMLP_BLOCK, 1), _sheet_spec()],
        out_shape=[jax.ShapeDtypeStruct((n, D_IN), BF16), jax.ShapeDtypeStruct((n, D_MODEL), BF16),
                   jax.ShapeDtypeStruct((n, D_MODEL), BF16),
                   jax.ShapeDtypeStruct((N_HEADS, GMLP_BLOCK, GMLP_BLOCK), F32),
                   jax.ShapeDtypeStruct((N_HEADS, GMLP_BLOCK, 1), F32), jax.ShapeDtypeStruct((SG_ROWS, SG_W), F32)],
        scratch_shapes=[pltpu.VMEM((tm, D_A), BF16), pltpu.VMEM((tm, D_A), F32)],
        args=(dx1, dx1, z, z, conv, pa, pb, b_gate.reshape(1, -1), ln_g.reshape(1, -1), ln_b.reshape(1, -1), w_s, w_s_t,
              b_s.reshape(N_HEADS, GMLP_BLOCK, 1), w_sc, w_out, wb, sheet))


def _matmul_tn(a, b, *, t1, tn, name, carry=None, pieces=1):
    n, k1 = a.shape
    k2 = b.shape[1]
    steps = n // tn
    w = k2 // pieces

    def body(a_ref, b_ref, *rest):
        o_refs, acc = rest[:pieces], rest[pieces]
        s = pl.program_id(1)

        @pl.when(s == 0)
        def _():
            acc[...] = jnp.zeros_like(acc)

        acc[...] += lax.dot_general(a_ref[...].astype(BF16), b_ref[...].astype(BF16), TN, preferred_element_type=F32)

        @pl.when(s == steps - 1)
        def _():
            for c, o_ref in enumerate(o_refs):
                o_ref[...] = acc[:, c * w:(c + 1) * w].astype(BF16)

    outs, carried = _call(
        body, name=name, grid=(k1 // t1, steps), carry=carry,
        in_specs=[pl.BlockSpec((tn, t1), lambda i, s: (s, i)),
                  pl.BlockSpec((tn, k2), lambda i, s: (s, 0))],
        out_specs=[pl.BlockSpec((t1, w), lambda i, s: (i, 0))] * pieces,
        out_shape=[jax.ShapeDtypeStruct((k1, w), BF16)] * pieces,
        scratch_shapes=[pltpu.VMEM((t1, k2), F32)],
        args=(a, b))
    return (outs[0] if pieces == 1 else list(outs)), carried


def _adamw_math(w, g, m, v):
    m = ADAM_B1 * m + (1.0 - ADAM_B1) * g
    v = ADAM_B2 * v + (1.0 - ADAM_B2) * (g * g)
    m_hat = m / (1.0 - ADAM_B1 ** ADAM_STEP)
    v_hat = v / (1.0 - ADAM_B2 ** ADAM_STEP)
    delta = -ADAM_LR * (m_hat / (jnp.sqrt(v_hat) + ADAM_EPS) + ADAM_WD * w)
    return delta, m, v


def _sum_parts(recvs, *, tr, name):
    _, r, c = recvs[0].shape

    def body(*refs):
        recv_refs, g_ref = refs[:DEPTH], refs[DEPTH]
        layer = pl.program_id(0)
        for l in range(DEPTH):
            @pl.when(layer == l)
            def _(l=l):
                g = recv_refs[l][0].astype(F32)
                for s in range(1, N_DEV):
                    g = g + recv_refs[l][s].astype(F32)
                g_ref[0] = g

    outs, _ = _call(
        body, name=name, grid=(DEPTH, r // tr),
        in_specs=[pl.BlockSpec((N_DEV, tr, c), lambda l, i: (0, i, 0))] * DEPTH,
        out_specs=[pl.BlockSpec((1, tr, c), lambda l, i: (l, i, 0))],
        out_shape=[jax.ShapeDtypeStruct((DEPTH, r, c), F32)],
        args=tuple(recvs))
    return outs[0]


def _adamw(w, g, m, v, *, tr, name):
    r, c = w.shape

    def body(w_ref, g_ref, m_ref, v_ref, d_ref, nm_ref, nv_ref):
        delta, nm, nv = _adamw_math(w_ref[...], g_ref[...], m_ref[...], v_ref[...])
        d_ref[...] = delta
        nm_ref[...] = nm
        nv_ref[...] = nv

    spec = pl.BlockSpec((tr, c), lambda i: (i, 0))
    outs, _ = _call(body, name=name, grid=(r // tr,), in_specs=[spec] * 4, out_specs=[spec] * 3,
                    out_shape=[jax.ShapeDtypeStruct((r, c), F32)] * 3, args=(w, g, m, v))
    return outs


def _sum_adamw(recvs, w, m, v, *, tr, name):
    _, r, c = w.shape
    blocks = len(recvs[0])
    flat = [piece for layer in recvs for piece in layer]

    def body(*refs):
        recv_refs = refs[:len(flat)]
        w_ref, m_ref, v_ref, g_ref, d_ref, nm_ref, nv_ref = refs[len(flat):]
        layer = pl.program_id(0)
        for l in range(DEPTH):
            @pl.when(layer == l)
            def _(l=l):
                cols = []
                for piece in recv_refs[l * blocks:(l + 1) * blocks]:
                    part = piece[0].astype(F32)
                    for s in range(1, N_DEV):
                        part = part + piece[s].astype(F32)
                    cols.append(part)
                g = cols[0] if blocks == 1 else jnp.concatenate(cols, axis=-1)
                delta, nm, nv = _adamw_math(w_ref[0], g, m_ref[0], v_ref[0])
                g_ref[0] = g
                d_ref[0] = delta
                nm_ref[0] = nm
                nv_ref[0] = nv

    spec = pl.BlockSpec((1, tr, c), lambda l, i: (l, i, 0))
    outs, _ = _call(
        body, name=name, grid=(DEPTH, r // tr),
        in_specs=[pl.BlockSpec((N_DEV, tr, c // blocks), lambda l, i: (0, i, 0))] * len(flat) + [spec] * 3,
        out_specs=[spec] * 4, out_shape=[jax.ShapeDtypeStruct((DEPTH, r, c), F32)] * 4,
        args=tuple(flat) + (w, m, v))
    return outs


def _adamw_small(sheet, extra, params, *, name):
    sheet_rows = dict(norm1_g=ROW_NORM1, b_gate=ROW_BGATE, gmlp_ln_g=ROW_LN_G, gmlp_ln_b=ROW_LN_B, norm2_g=ROW_NORM2,
                      b_ffn_conv=ROW_BFCONV)
    names = list(params)
    extra_names = list(extra)

    def body(*refs):
        sg_ref, refs = refs[0], refs[1:]
        extra_refs, refs = dict(zip(extra_names, refs[:len(extra_names)])), refs[len(extra_names):]
        ins, outs = refs[:3 * len(names)], refs[3 * len(names):]
        for j, key in enumerate(names):
            w_ref, m_ref, v_ref = ins[3 * j:3 * j + 3]
            g_ref, d_ref, nm_ref, nv_ref = outs[4 * j:4 * j + 4]
            if key in extra_refs:
                g_ref[...] = extra_refs[key][...]
            elif key == "final_g":
                g_ref[...] = sg_ref[ROW_FINAL:ROW_FINAL + 1, 0:D_MODEL]
            else:
                width = w_ref.shape[-1]
                for l in range(DEPTH):
                    row = SG_LAYER * l + sheet_rows[key]
                    g_ref[l:l + 1, :] = sg_ref[row:row + 1, 0:width]
            delta, nm, nv = _adamw_math(w_ref[...], g_ref[...], m_ref[...], v_ref[...])
            d_ref[...] = delta
            nm_ref[...] = nm
            nv_ref[...] = nv

    args = [sheet] + [extra[k] for k in extra_names] + [t for k in names for t in params[k]]
    vmem = pl.BlockSpec(memory_space=pltpu.VMEM)
    outs = pl.pallas_call(
        body, name=name, in_specs=[vmem] * len(args), out_specs=[vmem] * (4 * len(names)),
        out_shape=[jax.ShapeDtypeStruct(params[k][0].shape, F32) for k in names for _ in range(4)],
    )(*args)
    return {k: tuple(outs[4 * j:4 * j + 4]) for j, k in enumerate(names)}


def _rows(gathered):
    return gathered.reshape(N_DEV * gathered.shape[1], gathered.shape[2])


def _parts(full):
    return full.reshape(N_DEV, full.shape[0] // N_DEV, full.shape[1])


def kernel(x, norm1_g, w_in, b_gate, gmlp_ln_g, gmlp_ln_b, w_spatial, b_spatial, w_shortconv, w_branch, w_out, norm2_g, w_ffn_up, w_ffn_conv, b_ffn_conv, w_ffn_down, final_g, loss_target, m_norm1_g, m_w_in, m_b_gate, m_gmlp_ln_g, m_gmlp_ln_b, m_w_spatial, m_b_spatial, m_w_shortconv, m_w_branch, m_w_out, m_norm2_g, m_w_ffn_up, m_w_ffn_conv, m_b_ffn_conv, m_w_ffn_down, m_final_g, v_norm1_g, v_w_in, v_b_gate, v_gmlp_ln_g, v_gmlp_ln_b, v_w_spatial, v_b_spatial, v_w_shortconv, v_w_branch, v_w_out, v_norm2_g, v_w_ffn_up, v_w_ffn_conv, v_b_ffn_conv, v_w_ffn_down, v_final_g):
    n = x.shape[1]
    tm, tm_mix, tn = 512, 512, 1024
    x0 = x.reshape(n, D_MODEL)
    target = loss_target.reshape(n, D_MODEL)
    my_idx = 4 * lax.axis_index("x") + 2 * lax.axis_index("y") + lax.axis_index("c")
    sc_w, fc_w = D_B // N_DEV, D_FF // N_DEV

    sh_in = [w_in[l].T.astype(BF16) for l in range(DEPTH)]
    sh_up = [w_ffn_up[l].T.astype(BF16) for l in range(DEPTH)]
    sh_br = [w_branch[l].astype(BF16) for l in range(DEPTH)]
    sh_out = [w_out[l].astype(BF16) for l in range(DEPTH)]
    sh_down = [w_ffn_down[l].astype(BF16) for l in range(DEPTH)]
    taps = jnp.concatenate([w_shortconv, w_ffn_conv], axis=-1)

    def branch_weights(g):
        return g.transpose(1, 2, 0, 3).reshape(2, D_A, D_MODEL)

    g_in0, g_taps = _gather_now([sh_in[0], taps], name="gather_first")
    w_sc = [g_taps[:, l, :, :sc_w].transpose(1, 0, 2).reshape(3, D_B) for l in range(DEPTH)]
    w_fc = [g_taps[:, l, :, sc_w:].transpose(1, 0, 2).reshape(3, D_FF) for l in range(DEPTH)]
    w_s_t = [w_spatial[l].transpose(0, 2, 1) for l in range(DEPTH)]
    weights = [dict(), dict()]
    weights[0]["in_t"] = _rows(g_in0)
    saved = []
    xc = x0
    for l in range(DEPTH):
        p = weights[l]
        carry = _Gather([sh_br[0], sh_out[0]] if l == 0 else [sh_up[1]])
        (h, z), got = _norm_matmul(xc, norm1_g[l], p["in_t"], tm=tm, name=f"fwd_in_{l}", carry=carry)
        if l == 0:
            p["wb"], p["out"] = branch_weights(got[0]), _rows(got[1])
        else:
            p["up_t"] = _rows(got[0])
        carry = _Gather([sh_up[0]]) if l == 0 else None
        (ya, yb, conv, pa, pb, mg, x1), got = _mix_forward(
            z, xc, b_gate[l], gmlp_ln_g[l], gmlp_ln_b[l], w_spatial[l], b_spatial[l], w_sc[l], p["wb"], p["out"],
            tm=tm_mix, name=f"fwd_mix_{l}", carry=carry)
        if l == 0:
            p["up_t"] = _rows(got[0])
        (h2, up), got = _norm_matmul(x1, norm2_g[l], p["up_t"], tm=tm, name=f"fwd_up_{l}", carry=_Gather([sh_down[l]]))
        p["down"] = _rows(got[0])
        carry = _Gather([sh_br[1], sh_out[1], sh_in[1]]) if l == 0 else None
        head = (final_g, target) if l == DEPTH - 1 else None
        outs, got = _ffn_forward(up, x1, w_fc[l], b_ffn_conv[l], p["down"], tm=tm, name=f"fwd_ffn_{l}", carry=carry, head=head)
        if l == 0:
            weights[1]["wb"], weights[1]["out"], weights[1]["in_t"] = branch_weights(got[0]), _rows(got[1]), _rows(got[2])
        gc, a = outs[0], outs[1]
        saved.append(dict(x=xc, h=h, z=z, ya=ya, yb=yb, conv=conv, pa=pa, pb=pb, mg=mg, x1=x1, h2=h2, up=up, gc=gc, a=a))
        xc = outs[2]
    dx, sheet = outs[2], outs[3]

    recv = [dict(), dict()]
    small_dws, small_dbs = [None] * DEPTH, [None] * DEPTH
    pending_in = None
    for l in reversed(range(DEPTH)):
        p, s = weights[l], saved[l]
        carry = _Exchange([pending_in]) if pending_in is not None else None
        (dup, sheet), got = _ffn_backward(dx, s["up"], s["gc"], w_fc[l], p["down"], sheet, l, tm=tm, name=f"bwd_ffn_{l}",
                                          carry=carry)
        if got is not None:
            recv[l + 1]["in_t"] = got[0]
        dw_down, _ = _matmul_tn(s["a"], dx, t1=D_FF // 2, tn=tn, name=f"dw_down_{l}")
        dw_up_t, got = _matmul_tn(dup, s["h2"], t1=2 * D_FF // 4, tn=tn, name=f"dw_up_{l}", pieces=2,
                                  carry=_Exchange([_parts(dw_down)]))
        recv[l]["down"] = got[0]
        (dx1, sheet), got_left = _matmul_norm_backward(
            dup, p["up_t"], s["x1"], norm2_g[l], dx, sheet, SG_LAYER * l + ROW_NORM2, tm=tm, name=f"bwd_up_{l}",
            carry=_Exchange([_parts(dw_up_t[0])]))
        (dz, dpa, dpb, small_dws[l], small_dbs[l], sheet), got_right = _mix_backward(
            dx1, s["z"], s["conv"], s["pa"], s["pb"], b_gate[l], gmlp_ln_g[l], gmlp_ln_b[l], w_spatial[l], w_s_t[l],
            b_spatial[l], w_sc[l], p["out"], p["wb"], sheet, l, tm=tm_mix, name=f"bwd_mix_{l}",
            carry=_Exchange([_parts(dw_up_t[1])]))
        recv[l]["up_t"] = [got_left[0], got_right[0]]
        dw_out, _ = _matmul_tn(s["mg"], dx1, t1=D_MODEL, tn=tn, name=f"dw_out_{l}")
        dw_bra_t, _ = _matmul_tn(dpa, s["ya"], t1=D_MODEL, tn=tn, name=f"dw_branch_a_{l}")
        dw_brb_t, _ = _matmul_tn(dpb, s["yb"], t1=D_MODEL, tn=tn, name=f"dw_branch_b_{l}")
        carry = _Exchange([_parts(dw_out), _parts(dw_bra_t), _parts(dw_brb_t)])
        if l == 0:
            dbs = jnp.stack([t.reshape(N_HEADS, GMLP_BLOCK) for t in small_dbs]).reshape(DEPTH * N_HEADS, GMLP_BLOCK)
            carry = _Both(carry, _Gather([sheet, small_dws[0], small_dws[1], dbs]))
        dw_in_t, got = _matmul_tn(dz, s["h"], t1=D_IN // 4, tn=tn, name=f"dw_in_{l}", carry=carry)
        recv[l]["out"], recv[l]["bra_t"], recv[l]["brb_t"] = got[:3]
        if l == 0:
            gathered_small = got[3:]
            (dx0, dg1_first), got = _matmul_norm_backward(dz, p["in_t"], s["x"], norm1_g[l], dx1, None, 0, tm=tm,
                                                         name=f"bwd_in_{l}", carry=_Exchange([_parts(dw_in_t)]))
            recv[0]["in_t"] = got[0]
        else:
            (dx0, sheet), _ = _matmul_norm_backward(dz, p["in_t"], s["x"], norm1_g[l], dx1, sheet,
                                                    SG_LAYER * l + ROW_NORM1, tm=tm, name=f"bwd_in_{l}")
            pending_in = _parts(dw_in_t)
        dx = dx0
    grad_x = dx.reshape(x.shape)

    results = {}
    both = lambda key: [recv[l][key] for l in range(DEPTH)]
    blocks = lambda key: [r if isinstance(r, list) else [r] for r in both(key)]
    swap = lambda t: t.transpose(0, 2, 1)
    for key, slab, (w, m, v), tr in [("w_in", "in_t", (w_in, m_w_in, v_w_in), 192),
                                     ("w_ffn_up", "up_t", (w_ffn_up, m_w_ffn_up, v_w_ffn_up), 176)]:
        outs = _sum_adamw(blocks(slab), swap(w), swap(m), swap(v), tr=tr, name=f"adamw_{key}")
        results[key] = tuple(swap(o) for o in outs)
    g_bra = _sum_parts(both("bra_t"), tr=128, name="sum_w_branch_a").transpose(0, 2, 1)
    g_brb = _sum_parts(both("brb_t"), tr=128, name="sum_w_branch_b").transpose(0, 2, 1)
    g_br = jnp.stack([g_bra, g_brb], axis=1)
    flat = lambda t: t.reshape(-1, t.shape[-1])
    outs = _adamw(flat(w_branch), flat(g_br), flat(m_w_branch), flat(v_w_branch), tr=512, name="adamw_w_branch")
    results["w_branch"] = (g_br,) + tuple(o.reshape(w_branch.shape) for o in outs)
    results["w_out"] = tuple(_sum_adamw(blocks("out"), w_out, m_w_out, v_w_out, tr=128, name="adamw_w_out"))
    results["w_ffn_down"] = tuple(_sum_adamw(blocks("down"), w_ffn_down, m_w_ffn_down, v_w_ffn_down, tr=176,
                                             name="adamw_w_ffn_down"))

    sheet, dws0, dws1, dbs = _sum_gathered(gathered_small, name="sum_small_grads")
    (dg1_first,) = _all_reduce_small([dg1_first], name="all_reduce_last_gain")
    sheet = sheet.at[ROW_NORM1, :D_MODEL].set(dg1_first[0])
    loss = sheet[ROW_LOSS, 0]
    taps = lambda row, width: jnp.stack([sheet[SG_LAYER * l + row:SG_LAYER * l + row + 3, :width] for l in range(DEPTH)])
    extra = dict(w_spatial=jnp.stack([dws0, dws1]), b_spatial=dbs.reshape(DEPTH, N_HEADS, GMLP_BLOCK),
                 w_shortconv=lax.dynamic_slice_in_dim(taps(ROW_SCONV, D_B), my_idx * sc_w, sc_w, axis=2),
                 w_ffn_conv=lax.dynamic_slice_in_dim(taps(ROW_FCONV, D_FF), my_idx * fc_w, fc_w, axis=2))
    small_w = dict(norm1_g=(norm1_g, m_norm1_g, v_norm1_g), b_gate=(b_gate, m_b_gate, v_b_gate),
                   gmlp_ln_g=(gmlp_ln_g, m_gmlp_ln_g, v_gmlp_ln_g), gmlp_ln_b=(gmlp_ln_b, m_gmlp_ln_b, v_gmlp_ln_b),
                   w_spatial=(w_spatial, m_w_spatial, v_w_spatial), b_spatial=(b_spatial, m_b_spatial, v_b_spatial),
                   w_shortconv=(w_shortconv, m_w_shortconv, v_w_shortconv), norm2_g=(norm2_g, m_norm2_g, v_norm2_g),
                   w_ffn_conv=(w_ffn_conv, m_w_ffn_conv, v_w_ffn_conv), b_ffn_conv=(b_ffn_conv, m_b_ffn_conv, v_b_ffn_conv),
                   final_g=tuple(t.reshape(1, D_MODEL) for t in (final_g, m_final_g, v_final_g)))
    results.update(_adamw_small(sheet, extra, small_w, name="adamw_small"))
    results["final_g"] = tuple(t.reshape(D_MODEL) for t in results["final_g"])

    names = ["norm1_g", "w_in", "b_gate", "gmlp_ln_g", "gmlp_ln_b", "w_spatial", "b_spatial", "w_shortconv", "w_branch",
             "w_out", "norm2_g", "w_ffn_up", "w_ffn_conv", "b_ffn_conv", "w_ffn_down", "final_g"]
    return (loss, grad_x, *[results[k][0] for k in names], *[results[k][1] for k in names],
            *[results[k][2] for k in names], *[results[k][3] for k in names])
```

```python
import math

import jax
import jax.numpy as jnp
from jax import lax
from jax.experimental import pallas as pl
from jax.experimental.pallas import tpu as pltpu

F32 = jnp.float32
BF16 = jnp.bfloat16

N_DEV = 8
DEPTH = 2
D_MODEL = 1024
D_A = 512
D_B = 512
D_FF = 2816
D_IN = 4608
N_HEADS = 4
HEAD = 128
GMLP_BLOCK = 128
CAUSAL_CHUNK = 64
OFF_U, OFF_V, OFF_BG, OFF_CG, OFF_HB, OFF_GA, OFF_GB = 0, 512, 1024, 1536, 2048, 2560, 3584
RMS_EPS = 1e-6
LN_EPS = 1e-5
ADAM_LR, ADAM_B1, ADAM_B2, ADAM_EPS, ADAM_WD, ADAM_STEP = 0.001, 0.9, 0.999, 1e-08, 0.01, 10

SUBLANES = 8
HALO = 16
FFN_CHUNK = 256
SG_ROWS, SG_W, SG_LAYER = 40, D_FF, 16
ROW_NORM1, ROW_BGATE, ROW_LN_G, ROW_LN_B, ROW_SCONV, ROW_NORM2, ROW_FCONV, ROW_BFCONV = 0, 1, 2, 3, 4, 7, 8, 11
ROW_FINAL, ROW_LOSS = 32, 33
V7X_VMEM_BYTES = 64 << 20
VMEM_LIMIT = V7X_VMEM_BYTES - (8 << 20)
MESH = pl.DeviceIdType.MESH
GELU_C0 = 0.7978845608028654
GELU_C1 = 0.044715
NT = (((1,), (1,)), ((), ()))
TN = (((0,), (0,)), ((), ()))


def _dot(a, b):
    return jnp.dot(a, b, preferred_element_type=F32)


def _dot_nt(a, b):
    return lax.dot_general(a, b, NT, preferred_element_type=F32)


def _sigmoid(x):
    return 1.0 / (1.0 + jnp.exp(-x))


def _gelu_tanh(x):
    return jnp.tanh(GELU_C0 * (x + GELU_C1 * x * x * x))


def _gelu_grad(x, t):
    return 0.5 * (1.0 + t) + 0.5 * x * (1.0 - t * t) * GELU_C0 * (1.0 + 3.0 * GELU_C1 * x * x)


def _sublane_tile(dtype):
    return SUBLANES * (4 // jnp.dtype(dtype).itemsize)


def _shift_down(a, k, prev):
    p = prev.shape[0]
    r = pltpu.roll(a, k, 0)
    sub = _sublane_tile(a.dtype)
    head = r[0:sub]
    rid = lax.broadcasted_iota(jnp.int32, head.shape, 0)
    for j in range(k):
        head = jnp.where(rid == j, prev[p - k + j:p - k + j + 1, :], head)
    return jnp.concatenate([head, r[sub:]], axis=0)


def _shift_up(a, k, nxt):
    t = a.shape[0]
    r = pltpu.roll(a, t - k, 0)
    sub = _sublane_tile(a.dtype)
    tail = r[t - sub:t]
    rid = lax.broadcasted_iota(jnp.int32, tail.shape, 0)
    for j in range(k):
        tail = jnp.where(rid == sub - k + j, nxt[j:j + 1, :], tail)
    return jnp.concatenate([r[0:t - sub], tail], axis=0)


def _column_sums(p):
    if p.dtype.itemsize < 4:
        t = p.shape[0]
        p = p[:t // 2] + p[t // 2:]
        p = p[:t // 4] + p[t // 4:]
    return jnp.sum(p.astype(F32), axis=0, keepdims=True)


def _sheet_begin(step, sheet_in, sheet_out, first_row, rows):
    @pl.when(step == 0)
    def _():
        sheet_out[...] = sheet_in[...]
        sheet_out[first_row:first_row + rows, :] = jnp.zeros((rows, SG_W), F32)


def _sheet_spec():
    return pl.BlockSpec((SG_ROWS, SG_W), lambda i: (0, 0))


def _spatial_mask(transposed):
    ri = lax.broadcasted_iota(jnp.int32, (GMLP_BLOCK, GMLP_BLOCK), 0) // CAUSAL_CHUNK
    ci = lax.broadcasted_iota(jnp.int32, (GMLP_BLOCK, GMLP_BLOCK), 1) // CAUSAL_CHUNK
    return (ri <= ci) if transposed else (ci <= ri)


def _gmlp_forward(u, v, ln_g, ln_b, ws_ref, bs_ref, f_scr):
    tm = u.shape[0]
    tu = _gelu_tanh(u)
    tv = _gelu_tanh(v)
    gu = 0.5 * u * (1.0 + tu)
    gv = 0.5 * v * (1.0 + tv)
    mu = jnp.mean(gv, axis=-1, keepdims=True)
    cen = gv - mu
    rstd = lax.rsqrt(jnp.mean(cen * cen, axis=-1, keepdims=True) + LN_EPS)
    xh = cen * rstd
    vn = (xh * ln_g + ln_b).astype(BF16)
    mask = _spatial_mask(False)
    wm = [jnp.where(mask, ws_ref[h], 0.0).astype(BF16) for h in range(N_HEADS)]
    for b in range(tm // GMLP_BLOCK):
        rows = slice(b * GMLP_BLOCK, (b + 1) * GMLP_BLOCK)
        for h in range(N_HEADS):
            cols = slice(h * HEAD, (h + 1) * HEAD)
            f_scr[rows, cols] = (_dot(wm[h], vn[rows, cols]) + bs_ref[h]).astype(f_scr.dtype)
    return gu, tu, tv, xh, rstd, vn, f_scr[...]


def _position():
    return lax.axis_index("x"), lax.axis_index("y"), lax.axis_index("c")


class _Gather:
    def __init__(self, arrays):
        self.arrays = list(arrays)
        self.out_shape = [jax.ShapeDtypeStruct((N_DEV,) + a.shape, a.dtype) for a in self.arrays]
        self.base = 0

    def _plan(self, ins, outs, sems):
        send_sems, recv_sems, local_sems = sems
        x, y, c = _position()
        me, sibling = (x, y, c), (x, y, 1 - c)
        chips = [(1 - x, y), (x, 1 - y), (1 - x, 1 - y)]

        def slot(a, p):
            return outs[a].at[4 * p[0] + 2 * p[1] + p[2]]

        def copy(a, k, block, to, src=None):
            return pltpu.make_async_remote_copy(
                src_ref=slot(a, block) if src is None else src, dst_ref=slot(a, block),
                send_sem=send_sems.at[self.base + a, k], recv_sem=recv_sems.at[self.base + a, k],
                device_id=to, device_id_type=MESH)

        n = len(self.arrays)

        def mine():
            return [pltpu.make_async_copy(ins[a], slot(a, me), local_sems.at[self.base + a]) for a in range(n)]

        def first():
            out = []
            for a in range(n):
                out.append(copy(a, 0, me, sibling, src=ins[a]))
                out += [copy(a, 1 + j, me, (*chip, c), src=ins[a]) for j, chip in enumerate(chips)]
            return out

        def arrivals():
            return [copy(a, 1 + j, (*chip, c), me) for j, chip in enumerate(chips) for a in range(n)]

        def relays():
            return [copy(a, 4 + j, (*chip, c), sibling) for j, chip in enumerate(chips) for a in range(n)]

        def from_sibling():
            out = [copy(a, 0, sibling, me) for a in range(n)]
            return out + [copy(a, 4 + j, (*chip, 1 - c), me) for j, chip in enumerate(chips) for a in range(n)]

        return mine, first, arrivals, relays, from_sibling

    def start(self, ins, outs, sems):
        mine, first, _, _, _ = self._plan(ins, outs, sems)
        for cp in mine() + first():
            cp.start()

    def relay(self, ins, outs, sems):
        _, _, arrivals, relays, _ = self._plan(ins, outs, sems)
        for arrived, onward in zip(arrivals(), relays()):
            arrived.wait_recv()
            onward.start()

    def finish(self, ins, outs, sems):
        mine, first, _, relays, from_sibling = self._plan(ins, outs, sems)
        for cp in from_sibling():
            cp.wait_recv()
        for cp in first() + relays():
            cp.wait_send()
        for cp in mine():
            cp.wait()


class _Exchange:
    def __init__(self, arrays):
        self.arrays = list(arrays)
        self.out_shape = [jax.ShapeDtypeStruct(a.shape, a.dtype) for a in self.arrays]
        self.base = 0

    def _plan(self, ins, outs, sems):
        send_sems, recv_sems, local_sems = sems
        x, y, c = _position()
        my_idx = 4 * x + 2 * y + c
        n = len(self.arrays)
        offsets = [(dx, dy, dc) for dx in (0, 1) for dy in (0, 1) for dc in (0, 1) if (dx, dy, dc) != (0, 0, 0)]

        def mine():
            return [pltpu.make_async_copy(ins[a].at[my_idx], outs[a].at[my_idx], local_sems.at[self.base + a])
                    for a in range(n)]

        def remote(arriving):
            out = []
            for k, (dx, dy, dc) in enumerate(offsets):
                px, py, pc = x ^ dx, y ^ dy, c ^ dc
                p_idx = 4 * px + 2 * py + pc
                for a in range(n):
                    out.append(pltpu.make_async_remote_copy(
                        src_ref=ins[a].at[p_idx], dst_ref=outs[a].at[p_idx if arriving else my_idx],
                        send_sem=send_sems.at[self.base + a, k], recv_sem=recv_sems.at[self.base + a, k],
                        device_id=(px, py, pc), device_id_type=MESH))
            return out

        return mine, remote

    def start(self, ins, outs, sems):
        mine, remote = self._plan(ins, outs, sems)
        for cp in mine() + remote(False):
            cp.start()

    def relay(self, ins, outs, sems):
        pass

    def finish(self, ins, outs, sems):
        mine, remote = self._plan(ins, outs, sems)
        for cp in remote(True):
            cp.wait_recv()
        for cp in remote(False):
            cp.wait_send()
        for cp in mine():
            cp.wait()


class _Both:
    def __init__(self, *carries):
        self.carries = carries
        self.arrays = [a for c in carries for a in c.arrays]
        self.out_shape = [s for c in carries for s in c.out_shape]
        first = 0
        for c in carries:
            c.base = first
            first += len(c.arrays)

    def _each(self, method, ins, outs, sems):
        for c in self.carries:
            rows = slice(c.base, c.base + len(c.arrays))
            getattr(c, method)(ins[rows], outs[rows], sems)

    def start(self, ins, outs, sems):
        self._each("start", ins, outs, sems)

    def relay(self, ins, outs, sems):
        self._each("relay", ins, outs, sems)

    def finish(self, ins, outs, sems):
        self._each("finish", ins, outs, sems)


def _call(body, *, name, grid, in_specs, out_specs, out_shape, args, scratch_shapes=(), carry=None):
    n_in, n_out, n_scr = len(in_specs), len(out_specs), len(scratch_shapes)
    params = pltpu.CompilerParams(dimension_semantics=("arbitrary",) * len(grid), vmem_limit_bytes=VMEM_LIMIT)
    if carry is None:
        outs = pl.pallas_call(body, name=name, grid=grid, in_specs=in_specs, out_specs=out_specs, out_shape=out_shape,
                              scratch_shapes=list(scratch_shapes), compiler_params=params)(*args)
        return outs, None
    m = len(carry.arrays)
    total = math.prod(grid)

    def wrapped(*refs):
        ins, refs = refs[:n_in], refs[n_in:]
        c_ins, refs = refs[:m], refs[m:]
        outs, refs = refs[:n_out], refs[n_out:]
        c_outs, refs = refs[:m], refs[m:]
        scr, sems = refs[:n_scr], refs[n_scr:]
        flat = pl.program_id(0)
        for d in range(1, len(grid)):
            flat = flat * grid[d] + pl.program_id(d)

        @pl.when(flat == 0)
        def _():
            carry.start(c_ins, c_outs, sems)

        body(*ins, *outs, *scr)

        @pl.when(flat == total - 2)
        def _():
            carry.relay(c_ins, c_outs, sems)

        @pl.when(flat == total - 1)
        def _():
            carry.finish(c_ins, c_outs, sems)

    any_spec = pl.BlockSpec(memory_space=pl.ANY)
    sem_shapes = [pltpu.SemaphoreType.DMA((m, 7)), pltpu.SemaphoreType.DMA((m, 7)), pltpu.SemaphoreType.DMA((m,))]
    outs = pl.pallas_call(
        wrapped, name=name, grid=grid,
        in_specs=list(in_specs) + [any_spec] * m, out_specs=list(out_specs) + [any_spec] * m,
        out_shape=list(out_shape) + carry.out_shape,
        scratch_shapes=list(scratch_shapes) + sem_shapes, compiler_params=params)(*args, *carry.arrays)
    return outs[:n_out], outs[n_out:]


def _gather_now(arrays, *, name):
    carry = _Gather(arrays)
    m = len(arrays)

    def body(*refs):
        ins, outs, sems = refs[:m], refs[m:2 * m], refs[2 * m:]
        carry.start(ins, outs, sems)
        carry.relay(ins, outs, sems)
        carry.finish(ins, outs, sems)

    any_spec = pl.BlockSpec(memory_space=pl.ANY)
    return pl.pallas_call(
        body, name=name, in_specs=[any_spec] * m, out_specs=[any_spec] * m, out_shape=carry.out_shape,
        scratch_shapes=[pltpu.SemaphoreType.DMA((m, 7)), pltpu.SemaphoreType.DMA((m, 7)),
                        pltpu.SemaphoreType.DMA((m,))],
    )(*arrays)


def _all_reduce_small(arrs, *, name):
    n = len(arrs)

    def body(*refs):
        ins, outs, bufs = refs[:n], refs[n:2 * n], refs[2 * n:3 * n]
        send_sems, recv_sems = refs[3 * n:]
        x, y, c = _position()
        me, sibling = (x, y, c), (x, y, 1 - c)
        chips = [(1 - x, y), (x, 1 - y), (1 - x, 1 - y)]

        def copy(a, k, block, to, src=None):
            slot = bufs[a].at[4 * block[0] + 2 * block[1] + block[2]]
            return pltpu.make_async_remote_copy(
                src_ref=slot if src is None else src, dst_ref=slot,
                send_sem=send_sems.at[a, k], recv_sem=recv_sems.at[a, k], device_id=to, device_id_type=MESH)

        first = []
        for a in range(n):
            first.append(copy(a, 0, me, sibling, src=ins[a]))
            first += [copy(a, 1 + j, me, (*chip, c), src=ins[a]) for j, chip in enumerate(chips)]
        for cp in first:
            cp.start()
        passed = []
        for j, chip in enumerate(chips):
            for a in range(n):
                copy(a, 1 + j, (*chip, c), me).wait_recv()
                cp = copy(a, 4 + j, (*chip, c), sibling)
                cp.start()
                passed.append(cp)
        for a in range(n):
            copy(a, 0, sibling, me).wait_recv()
            for j, chip in enumerate(chips):
                copy(a, 4 + j, (*chip, 1 - c), me).wait_recv()
        for cp in first + passed:
            cp.wait_send()
        my_idx = 4 * x + 2 * y + c
        for a in range(n):
            acc = jnp.zeros(ins[a].shape, F32)
            for s in range(N_DEV):
                acc = acc + jnp.where(my_idx == s, ins[a][...], bufs[a][s])
            outs[a][...] = acc

    vmem = pl.BlockSpec(memory_space=pltpu.VMEM)
    return pl.pallas_call(
        body, name=name, in_specs=[vmem] * n, out_specs=[vmem] * n,
        out_shape=[jax.ShapeDtypeStruct(a.shape, F32) for a in arrs],
        scratch_shapes=[pltpu.VMEM((N_DEV,) + a.shape, F32) for a in arrs]
        + [pltpu.SemaphoreType.DMA((n, 7)), pltpu.SemaphoreType.DMA((n, 7))],
        compiler_params=pltpu.CompilerParams(vmem_limit_bytes=VMEM_LIMIT),
    )(*arrs)


def _sum_gathered(arrs, *, name):
    n = len(arrs)

    def body(*refs):
        for in_ref, out_ref in zip(refs[:n], refs[n:]):
            acc = in_ref[0]
            for s in range(1, N_DEV):
                acc = acc + in_ref[s]
            out_ref[...] = acc

    vmem = pl.BlockSpec(memory_space=pltpu.VMEM)
    return pl.pallas_call(
        body, name=name, in_specs=[vmem] * n, out_specs=[vmem] * n,
        out_shape=[jax.ShapeDtypeStruct(a.shape[1:], F32) for a in arrs],
        compiler_params=pltpu.CompilerParams(vmem_limit_bytes=VMEM_LIMIT),
    )(*arrs)


def _norm_matmul(x, g, w_t, *, tm, name, carry=None):
    n, d = x.shape
    c = w_t.shape[0]
    ch = 512

    def body(x_ref, g_ref, wt_ref, h_ref, z_ref):
        xv = x_ref[...]
        r = lax.rsqrt(jnp.mean(xv * xv, axis=-1, keepdims=True) + RMS_EPS)
        h = (xv * r * g_ref[...]).astype(BF16)
        h_ref[...] = h
        for c0 in range(0, c, ch):
            z_ref[:, c0:c0 + ch] = _dot_nt(h, wt_ref[c0:c0 + ch, :]).astype(BF16)

    return _call(
        body, name=name, grid=(n // tm,), carry=carry,
        in_specs=[pl.BlockSpec((tm, d), lambda i: (i, 0)),
                  pl.BlockSpec((1, d), lambda i: (0, 0)),
                  pl.BlockSpec((c, d), lambda i: (0, 0))],
        out_specs=[pl.BlockSpec((tm, d), lambda i: (i, 0)),
                   pl.BlockSpec((tm, c), lambda i: (i, 0))],
        out_shape=[jax.ShapeDtypeStruct((n, d), BF16), jax.ShapeDtypeStruct((n, c), BF16)],
        args=(x, g.reshape(1, d), w_t))


def _mix_forward(z, x, b_gate, ln_g, ln_b, w_s, b_s, w_sc, wb, w_out, *, tm, name, carry=None):
    n = z.shape[0]
    hb = tm // HALO

    def body(z_ref, zp_ref, x_ref, bg_ref, lng_ref, lnb_ref, ws_ref, bs_ref, wsc_ref, wb_ref, wo_ref,
             ya_ref, yb_ref, cv_ref, pa_ref, pb_ref, mg_ref, x1_ref, f_scr):
        i = pl.program_id(0)
        u = z_ref[:, OFF_U:OFF_U + D_A]
        v = z_ref[:, OFF_V:OFF_V + D_A].astype(F32)
        gu, _, _, _, _, _, f = _gmlp_forward(u, v, lng_ref[...], lnb_ref[...], ws_ref, bs_ref, f_scr)
        ya = gu * f
        ya_ref[...] = ya

        q = z_ref[:, OFF_CG:OFF_CG + D_B] * z_ref[:, OFF_HB:OFF_HB + D_B]
        qp = zp_ref[:, OFF_CG:OFF_CG + D_B] * zp_ref[:, OFF_HB:OFF_HB + D_B]
        qp = jnp.where(i > 0, qp, jnp.zeros_like(qp))
        w = wsc_ref[...].astype(BF16)
        conv = w[0:1] * _shift_down(q, 2, qp) + w[1:2] * _shift_down(q, 1, qp) + w[2:3] * q
        cv_ref[...] = conv
        yb = z_ref[:, OFF_BG:OFF_BG + D_B] * conv
        yb_ref[...] = yb

        pa = _dot(ya, wb_ref[0]).astype(BF16)
        pb = _dot(yb, wb_ref[1]).astype(BF16)
        pa_ref[...] = pa
        pb_ref[...] = pb
        bg = bg_ref[...].astype(BF16)
        sa = _sigmoid(z_ref[:, OFF_GA:OFF_GA + D_MODEL] + bg[:, 0:D_MODEL])
        sb = _sigmoid(z_ref[:, OFF_GB:OFF_GB + D_MODEL] + bg[:, D_MODEL:2 * D_MODEL])
        mg = sa * pa + sb * pb
        mg_ref[...] = mg
        x1_ref[...] = x_ref[...] + _dot(mg, wo_ref[...])

    row = lambda w: pl.BlockSpec((tm, w), lambda i: (i, 0))
    full = lambda *s: pl.BlockSpec(s, lambda i: (0,) * len(s))
    bf = lambda w: jax.ShapeDtypeStruct((n, w), BF16)
    return _call(
        body, name=name, grid=(n // tm,), carry=carry,
        in_specs=[row(D_IN),
                  pl.BlockSpec((HALO, D_IN), lambda i: (jnp.maximum(i * hb - 1, 0), 0)),
                  row(D_MODEL), full(1, 2 * D_MODEL), full(1, D_A), full(1, D_A),
                  full(N_HEADS, GMLP_BLOCK, GMLP_BLOCK), full(N_HEADS, GMLP_BLOCK, 1), full(3, D_B),
                  full(2, D_A, D_MODEL), full(D_MODEL, D_MODEL)],
        out_specs=[row(D_A), row(D_B), row(D_B), row(D_MODEL), row(D_MODEL), row(D_MODEL), row(D_MODEL)],
        out_shape=[bf(D_A), bf(D_B), bf(D_B), bf(D_MODEL), bf(D_MODEL), bf(D_MODEL),
                   jax.ShapeDtypeStruct((n, D_MODEL), F32)],
        scratch_shapes=[pltpu.VMEM((tm, D_A), BF16)],
        args=(z, z, x, b_gate.reshape(1, -1), ln_g.reshape(1, -1), ln_b.reshape(1, -1), w_s,
              b_s.reshape(N_HEADS, GMLP_BLOCK, 1), w_sc, wb, w_out))


def _loss_tile(xv, gv, tv):
    d = xv.shape[-1]
    r = lax.rsqrt(jnp.mean(xv * xv, axis=-1, keepdims=True) + RMS_EPS)
    xh = xv * r
    e = xh * gv - tv
    per_row = jnp.sum(e * e, axis=-1, keepdims=True) * (0.5 / d)
    dy = e * (1.0 / d)
    dxh = dy * gv
    dx = r * (dxh - xh * jnp.mean(dxh * xh, axis=-1, keepdims=True))
    return dx, jnp.sum(per_row, axis=0, keepdims=True), jnp.sum(dy * xh, axis=0, keepdims=True)


def _ffn_forward(up, x1, w_fc, b_fc, w_down, *, tm, name, carry=None, head=None):
    n = up.shape[0]
    hb = tm // HALO
    n_in = 6 if head is None else 8

    def body(*refs):
        up_ref, upp_ref, x1_ref, wfc_ref, bfc_ref, wd_ref = refs[:6]
        gc_ref, a_ref, out_ref = refs[n_in:n_in + 3]
        acc = refs[-1]
        i = pl.program_id(0)
        acc[...] = x1_ref[...]
        for c0 in range(0, D_FF, FFN_CHUNK):
            cols = slice(c0, c0 + FFN_CHUNK)
            gate = up_ref[:, cols]
            val = up_ref[:, D_FF + c0:D_FF + c0 + FFN_CHUNK]
            gp = upp_ref[:, cols]
            gp = jnp.where(i > 0, gp, jnp.zeros_like(gp))
            w = wfc_ref[:, cols].astype(BF16)
            gc = (w[0:1] * _shift_down(gate, 2, gp) + w[1:2] * _shift_down(gate, 1, gp) + w[2:3] * gate
                  + bfc_ref[:, cols].astype(BF16))
            gc_ref[:, cols] = gc
            a = gc * _sigmoid(gc) * val
            a_ref[:, cols] = a
            acc[...] += _dot(a, wd_ref[cols, :])
        if head is None:
            out_ref[...] = acc[...]
        else:
            g_ref, t_ref = refs[6:8]
            sg_ref = refs[n_in + 3]

            @pl.when(i == 0)
            def _():
                sg_ref[...] = jnp.zeros_like(sg_ref)

            dx, loss, dg = _loss_tile(acc[...], g_ref[...], t_ref[...])
            out_ref[...] = dx
            sg_ref[ROW_LOSS:ROW_LOSS + 1, 0:128] += jnp.broadcast_to(loss, (1, 128))
            sg_ref[ROW_FINAL:ROW_FINAL + 1, 0:D_MODEL] += dg

    row = lambda w: pl.BlockSpec((tm, w), lambda i: (i, 0))
    full = lambda r, c: pl.BlockSpec((r, c), lambda i: (0, 0))
    in_specs = [row(2 * D_FF), pl.BlockSpec((HALO, D_FF), lambda i: (jnp.maximum(i * hb - 1, 0), 0)), row(D_MODEL),
                full(3, D_FF), full(1, D_FF), full(D_FF, D_MODEL)]
    out_specs = [row(D_FF), row(D_FF), row(D_MODEL)]
    out_shape = [jax.ShapeDtypeStruct((n, D_FF), BF16), jax.ShapeDtypeStruct((n, D_FF), BF16),
                 jax.ShapeDtypeStruct((n, D_MODEL), F32)]
    args = (up, up, x1, w_fc, b_fc.reshape(1, -1), w_down)
    if head is not None:
        in_specs += [full(1, D_MODEL), row(D_MODEL)]
        out_specs += [full(SG_ROWS, SG_W)]
        out_shape += [jax.ShapeDtypeStruct((SG_ROWS, SG_W), F32)]
        args += (head[0].reshape(1, -1), head[1])
    return _call(body, name=name, grid=(n // tm,), carry=carry, in_specs=in_specs, out_specs=out_specs,
                 out_shape=out_shape, scratch_shapes=[pltpu.VMEM((tm, D_MODEL), F32)], args=args)


def _ffn_backward(dx2, up, gc, w_fc, w_down, sheet, layer, *, tm, name, carry=None):
    n = up.shape[0]
    steps = n // tm
    hb = tm // HALO
    row = SG_LAYER * layer + ROW_FCONV

    def body(dx_ref, dxn_ref, up_ref, upn_ref, gc_ref, gcn_ref, wfc_ref, wd_ref, sg_in, dup_ref, sg_ref):
        i = pl.program_id(0)
        last = i == steps - 1
        _sheet_begin(i, sg_in, sg_ref, row, 4)

        dxe = jnp.concatenate([dx_ref[...], dxn_ref[...]], axis=0).astype(BF16)
        for c0 in range(0, D_FF, FFN_CHUNK):
            cols = slice(c0, c0 + FFN_CHUNK)
            vcols = slice(D_FF + c0, D_FF + c0 + FFN_CHUNK)
            dae = _dot_nt(dxe, wd_ref[cols, :])
            da, dan = dae[:tm], dae[tm:]
            gate = up_ref[:, cols]
            val = up_ref[:, vcols]
            gcv = gc_ref[:, cols]
            s = _sigmoid(gcv)
            dab = da.astype(BF16)
            dup_ref[:, vcols] = dab * (gcv * s)
            dgc = dab * val * (s * (1.0 + gcv * (1.0 - s)))
            gcn = gcn_ref[:, cols]
            sn = _sigmoid(gcn)
            dgcn = dan.astype(BF16) * upn_ref[:, vcols] * (sn * (1.0 + gcn * (1.0 - sn)))
            dgcn = jnp.where(last, jnp.zeros_like(dgcn), dgcn)
            up1 = _shift_up(dgc, 1, dgcn)
            up2 = _shift_up(dgc, 2, dgcn)
            w = wfc_ref[:, cols].astype(BF16)
            dup_ref[:, cols] = w[2:3] * dgc + w[1:2] * up1 + w[0:1] * up2
            sg_ref[row:row + 1, cols] += _column_sums(gate * up2)
            sg_ref[row + 1:row + 2, cols] += _column_sums(gate * up1)
            sg_ref[row + 2:row + 3, cols] += _column_sums(gate * dgc)
            sg_ref[row + 3:row + 4, cols] += _column_sums(dgc)

    nxt = lambda i: (jnp.minimum((i + 1) * hb, steps * hb - 1), 0)
    return _call(
        body, name=name, grid=(steps,), carry=carry,
        in_specs=[pl.BlockSpec((tm, D_MODEL), lambda i: (i, 0)),
                  pl.BlockSpec((HALO, D_MODEL), nxt),
                  pl.BlockSpec((tm, 2 * D_FF), lambda i: (i, 0)),
                  pl.BlockSpec((HALO, 2 * D_FF), nxt),
                  pl.BlockSpec((tm, D_FF), lambda i: (i, 0)),
                  pl.BlockSpec((HALO, D_FF), nxt),
                  pl.BlockSpec((3, D_FF), lambda i: (0, 0)),
                  pl.BlockSpec((D_FF, D_MODEL), lambda i: (0, 0)), _sheet_spec()],
        out_specs=[pl.BlockSpec((tm, 2 * D_FF), lambda i: (i, 0)), _sheet_spec()],
        out_shape=[jax.ShapeDtypeStruct((n, 2 * D_FF), BF16), jax.ShapeDtypeStruct((SG_ROWS, SG_W), F32)],
        args=(dx2, dx2, up, up, gc, gc, w_fc, w_down, sheet))


def _matmul_norm_backward(dz, w_t, x, g, dres, sheet, row, *, tm, name, carry=None):
    n, c = dz.shape
    d = x.shape[1]
    ch = 512

    def body(dz_ref, wt_ref, x_ref, g_ref, dres_ref, *rest):
        i = pl.program_id(0)
        if sheet is None:
            dx_ref, sg_ref = rest

            @pl.when(i == 0)
            def _():
                sg_ref[...] = jnp.zeros_like(sg_ref)
        else:
            sg_in, dx_ref, sg_ref = rest
            _sheet_begin(i, sg_in, sg_ref, row, 1)

        dh = _dot(dz_ref[:, 0:ch], wt_ref[0:ch, :])
        for c0 in range(ch, c, ch):
            dh += _dot(dz_ref[:, c0:c0 + ch], wt_ref[c0:c0 + ch, :])
        xv = x_ref[...]
        r = lax.rsqrt(jnp.mean(xv * xv, axis=-1, keepdims=True) + RMS_EPS)
        xh = xv * r
        sg_ref[row:row + 1, 0:d] += jnp.sum(dh * xh, axis=0, keepdims=True)
        dxh = dh * g_ref[...]
        dx_ref[...] = dres_ref[...] + r * (dxh - xh * jnp.mean(dxh * xh, axis=-1, keepdims=True))

    in_specs = [pl.BlockSpec((tm, c), lambda i: (i, 0)),
                pl.BlockSpec((c, d), lambda i: (0, 0)),
                pl.BlockSpec((tm, d), lambda i: (i, 0)),
                pl.BlockSpec((1, d), lambda i: (0, 0)),
                pl.BlockSpec((tm, d), lambda i: (i, 0))]
    args = (dz, w_t, x, g.reshape(1, d), dres)
    if sheet is None:
        small_spec, small_shape = pl.BlockSpec((8, d), lambda i: (0, 0)), jax.ShapeDtypeStruct((8, d), F32)
    else:
        in_specs, args = in_specs + [_sheet_spec()], args + (sheet,)
        small_spec, small_shape = _sheet_spec(), jax.ShapeDtypeStruct((SG_ROWS, SG_W), F32)
    return _call(
        body, name=name, grid=(n // tm,), carry=carry, in_specs=in_specs,
        out_specs=[pl.BlockSpec((tm, d), lambda i: (i, 0)), small_spec],
        out_shape=[jax.ShapeDtypeStruct((n, d), F32), small_shape], args=args)


def _mix_backward(dx1, z, conv, pa, pb, b_gate, ln_g, ln_b, w_s, w_s_t, b_s, w_sc, w_out, wb, sheet, layer, *, tm, name,
                  carry=None):
    n = z.shape[0]
    steps = n // tm
    hb = tm // HALO
    base = SG_LAYER * layer
    r_bg, r_lng, r_lnb, r_sc = base + ROW_BGATE, base + ROW_LN_G, base + ROW_LN_B, base + ROW_SCONV

    def body(dx_ref, dxn_ref, z_ref, zn_ref, cv_ref, pa_ref, pb_ref, bg_ref, lng_ref, lnb_ref, ws_ref, wst_ref,
             bs_ref, wsc_ref, wo_ref, wb_ref, sg_in,
             dz_ref, dpa_ref, dpb_ref, dws_ref, dbs_ref, sg_ref, f_scr, dvn_scr):
        i = pl.program_id(0)
        last = i == steps - 1
        _sheet_begin(i, sg_in, sg_ref, r_bg, ROW_NORM2 - ROW_BGATE)

        @pl.when(i == 0)
        def _():
            dws_ref[...] = jnp.zeros_like(dws_ref)
            dbs_ref[...] = jnp.zeros_like(dbs_ref)

        dxe = jnp.concatenate([dx_ref[...], dxn_ref[...]], axis=0).astype(BF16)
        dmge = _dot_nt(dxe, wo_ref[...])
        dmg, dmgn = dmge[:tm].astype(BF16), dmge[tm:].astype(BF16)

        pa_v = pa_ref[...]
        pb_v = pb_ref[...]
        bg = bg_ref[...].astype(BF16)
        sa = _sigmoid(z_ref[:, OFF_GA:OFF_GA + D_MODEL] + bg[:, 0:D_MODEL])
        sb = _sigmoid(z_ref[:, OFF_GB:OFF_GB + D_MODEL] + bg[:, D_MODEL:2 * D_MODEL])
        dpa = dmg * sa
        dpb = dmg * sb
        dga = dmg * pa_v * sa * (1.0 - sa)
        dgb = dmg * pb_v * sb * (1.0 - sb)
        dpa_ref[...] = dpa
        dpb_ref[...] = dpb
        dz_ref[:, OFF_GA:OFF_GA + D_MODEL] = dga
        dz_ref[:, OFF_GB:OFF_GB + D_MODEL] = dgb
        sg_ref[r_bg:r_bg + 1, 0:D_MODEL] += _column_sums(dga)
        sg_ref[r_bg:r_bg + 1, D_MODEL:2 * D_MODEL] += _column_sums(dgb)

        dya = _dot_nt(dpa, wb_ref[0]).astype(BF16)
        u = z_ref[:, OFF_U:OFF_U + D_A]
        v = z_ref[:, OFF_V:OFF_V + D_A].astype(F32)
        ln_g = lng_ref[...]
        gu, tu, tv, xh, rstd, vn, f = _gmlp_forward(u, v, ln_g, lnb_ref[...], ws_ref, bs_ref, f_scr)
        dgu = dya * f
        df_bf = dya * gu
        dz_ref[:, OFF_U:OFF_U + D_A] = dgu * _gelu_grad(u, tu)
        mask = _spatial_mask(False)
        mask_t = _spatial_mask(True)
        wmt = [jnp.where(mask_t, wst_ref[h], 0.0).astype(BF16) for h in range(N_HEADS)]
        for b in range(tm // GMLP_BLOCK):
            rows = slice(b * GMLP_BLOCK, (b + 1) * GMLP_BLOCK)
            for h in range(N_HEADS):
                cols = slice(h * HEAD, (h + 1) * HEAD)
                dfb = df_bf[rows, cols]
                dvn_scr[rows, cols] = _dot(wmt[h], dfb)
                dws_ref[h] += jnp.where(mask, _dot_nt(dfb, vn[rows, cols]), 0.0)
                dbs_ref[h] += jnp.sum(dfb.astype(F32), axis=1, keepdims=True)
        dvn = dvn_scr[...]
        sg_ref[r_lng:r_lng + 1, 0:D_A] += jnp.sum(dvn * xh, axis=0, keepdims=True)
        sg_ref[r_lnb:r_lnb + 1, 0:D_A] += jnp.sum(dvn, axis=0, keepdims=True)
        dxh = dvn * ln_g
        dgv = rstd * (dxh - jnp.mean(dxh, axis=-1, keepdims=True) - xh * jnp.mean(dxh * xh, axis=-1, keepdims=True))
        dz_ref[:, OFF_V:OFF_V + D_A] = (dgv * _gelu_grad(v, tv)).astype(BF16)

        sbn = _sigmoid(zn_ref[:, OFF_GB:OFF_GB + D_MODEL] + bg[:, D_MODEL:2 * D_MODEL])
        dpbe = jnp.concatenate([dpb, dmgn * sbn], axis=0)
        dybe = _dot_nt(dpbe, wb_ref[1])
        dyb, dybn = dybe[:tm].astype(BF16), dybe[tm:].astype(BF16)
        bgv = z_ref[:, OFF_BG:OFF_BG + D_B]
        cg = z_ref[:, OFF_CG:OFF_CG + D_B]
        hbv = z_ref[:, OFF_HB:OFF_HB + D_B]
        q = cg * hbv
        dz_ref[:, OFF_BG:OFF_BG + D_B] = dyb * cv_ref[...]
        dconv = dyb * bgv
        dconvn = dybn * zn_ref[:, OFF_BG:OFF_BG + D_B]
        dconvn = jnp.where(last, jnp.zeros_like(dconvn), dconvn)
        up1 = _shift_up(dconv, 1, dconvn)
        up2 = _shift_up(dconv, 2, dconvn)
        sg_ref[r_sc:r_sc + 1, 0:D_B] += _column_sums(q * up2)
        sg_ref[r_sc + 1:r_sc + 2, 0:D_B] += _column_sums(q * up1)
        sg_ref[r_sc + 2:r_sc + 3, 0:D_B] += _column_sums(q * dconv)
        w = wsc_ref[...].astype(BF16)
        dq = w[2:3] * dconv + w[1:2] * up1 + w[0:1] * up2
        dz_ref[:, OFF_CG:OFF_CG + D_B] = dq * hbv
        dz_ref[:, OFF_HB:OFF_HB + D_B] = dq * cg

    row = lambda w: pl.BlockSpec((tm, w), lambda i: (i, 0))
    full = lambda *s: pl.BlockSpec(s, lambda i: (0,) * len(s))
    nxt = lambda i: (jnp.minimum((i + 1) * hb, steps * hb - 1), 0)
    return _call(
        body, name=name, grid=(steps,), carry=carry,
        in_specs=[row(D_MODEL), pl.BlockSpec((HALO, D_MODEL), nxt),
                  row(D_IN), pl.BlockSpec((HALO, D_IN), nxt),
                  row(D_B), row(D_MODEL), row(D_MODEL),
                  full(1, 2 * D_MODEL), full(1, D_A), full(1, D_A),
                  full(N_HEADS, GMLP_BLOCK, GMLP_BLOCK), full(N_HEADS, GMLP_BLOCK, GMLP_BLOCK),
                  full(N_HEADS, GMLP_BLOCK, 1), full(3, D_B),
                  full(D_MODEL, D_MODEL), full(2, D_A, D_MODEL), _sheet_spec()],
        out_specs=[row(D_IN), row(D_MODEL), row(D_MODEL), full(N_HEADS, GMLP_BLOCK, GMLP_BLOCK),
                   full(N_HEADS, GMLP_BLOCK, 1), _sheet_spec()],
        out_shape=[jax.ShapeDtypeStruct((n, D_IN), BF16), jax.ShapeDtypeStruct((n, D_MODEL), BF16),
                   jax.ShapeDtypeStruct((n, D_MODEL), BF16),
                   jax.ShapeDtypeStruct((N_HEADS, GMLP_BLOCK, GMLP_BLOCK), F32),
                   jax.ShapeDtypeStruct((N_HEADS, GMLP_BLOCK, 1), F32), jax.ShapeDtypeStruct((SG_ROWS, SG_W), F32)],
        scratch_shapes=[pltpu.VMEM((tm, D_A), BF16), pltpu.VMEM((tm, D_A), F32)],
        args=(dx1, dx1, z, z, conv, pa, pb, b_gate.reshape(1, -1), ln_g.reshape(1, -1), ln_b.reshape(1, -1), w_s, w_s_t,
              b_s.reshape(N_HEADS, GMLP_BLOCK, 1), w_sc, w_out, wb, sheet))


def _matmul_tn(a, b, *, t1, tn, name, carry=None, pieces=1):
    n, k1 = a.shape
    k2 = b.shape[1]
    steps = n // tn
    w = k2 // pieces

    def body(a_ref, b_ref, *rest):
        o_refs, acc = rest[:pieces], rest[pieces]
        s = pl.program_id(1)

        @pl.when(s == 0)
        def _():
            acc[...] = jnp.zeros_like(acc)

        acc[...] += lax.dot_general(a_ref[...].astype(BF16), b_ref[...].astype(BF16), TN, preferred_element_type=F32)

        @pl.when(s == steps - 1)
        def _():
            for c, o_ref in enumerate(o_refs):
                o_ref[...] = acc[:, c * w:(c + 1) * w].astype(BF16)

    outs, carried = _call(
        body, name=name, grid=(k1 // t1, steps), carry=carry,
        in_specs=[pl.BlockSpec((tn, t1), lambda i, s: (s, i)),
                  pl.BlockSpec((tn, k2), lambda i, s: (s, 0))],
        out_specs=[pl.BlockSpec((t1, w), lambda i, s: (i, 0))] * pieces,
        out_shape=[jax.ShapeDtypeStruct((k1, w), BF16)] * pieces,
        scratch_shapes=[pltpu.VMEM((t1, k2), F32)],
        args=(a, b))
    return (outs[0] if pieces == 1 else list(outs)), carried


def _adamw_math(w, g, m, v):
    m = ADAM_B1 * m + (1.0 - ADAM_B1) * g
    v = ADAM_B2 * v + (1.0 - ADAM_B2) * (g * g)
    m_hat = m / (1.0 - ADAM_B1 ** ADAM_STEP)
    v_hat = v / (1.0 - ADAM_B2 ** ADAM_STEP)
    delta = -ADAM_LR * (m_hat / (jnp.sqrt(v_hat) + ADAM_EPS) + ADAM_WD * w)
    return delta, m, v


def _sum_parts(recvs, *, tr, name):
    _, r, c = recvs[0].shape

    def body(*refs):
        recv_refs, g_ref = refs[:DEPTH], refs[DEPTH]
        layer = pl.program_id(0)
        for l in range(DEPTH):
            @pl.when(layer == l)
            def _(l=l):
                g = recv_refs[l][0].astype(F32)
                for s in range(1, N_DEV):
                    g = g + recv_refs[l][s].astype(F32)
                g_ref[0] = g

    outs, _ = _call(
        body, name=name, grid=(DEPTH, r // tr),
        in_specs=[pl.BlockSpec((N_DEV, tr, c), lambda l, i: (0, i, 0))] * DEPTH,
        out_specs=[pl.BlockSpec((1, tr, c), lambda l, i: (l, i, 0))],
        out_shape=[jax.ShapeDtypeStruct((DEPTH, r, c), F32)],
        args=tuple(recvs))
    return outs[0]


def _adamw(w, g, m, v, *, tr, name):
    r, c = w.shape

    def body(w_ref, g_ref, m_ref, v_ref, d_ref, nm_ref, nv_ref):
        delta, nm, nv = _adamw_math(w_ref[...], g_ref[...], m_ref[...], v_ref[...])
        d_ref[...] = delta
        nm_ref[...] = nm
        nv_ref[...] = nv

    spec = pl.BlockSpec((tr, c), lambda i: (i, 0))
    outs, _ = _call(body, name=name, grid=(r // tr,), in_specs=[spec] * 4, out_specs=[spec] * 3,
                    out_shape=[jax.ShapeDtypeStruct((r, c), F32)] * 3, args=(w, g, m, v))
    return outs


def _sum_adamw(recvs, w, m, v, *, tr, name):
    _, r, c = w.shape
    blocks = len(recvs[0])
    flat = [piece for layer in recvs for piece in layer]

    def body(*refs):
        recv_refs = refs[:len(flat)]
        w_ref, m_ref, v_ref, g_ref, d_ref, nm_ref, nv_ref = refs[len(flat):]
        layer = pl.program_id(0)
        for l in range(DEPTH):
            @pl.when(layer == l)
            def _(l=l):
                cols = []
                for piece in recv_refs[l * blocks:(l + 1) * blocks]:
                    part = piece[0].astype(F32)
                    for s in range(1, N_DEV):
                        part = part + piece[s].astype(F32)
                    cols.append(part)
                g = cols[0] if blocks == 1 else jnp.concatenate(cols, axis=-1)
                delta, nm, nv = _adamw_math(w_ref[0], g, m_ref[0], v_ref[0])
                g_ref[0] = g
                d_ref[0] = delta
                nm_ref[0] = nm
                nv_ref[0] = nv

    spec = pl.BlockSpec((1, tr, c), lambda l, i: (l, i, 0))
    outs, _ = _call(
        body, name=name, grid=(DEPTH, r // tr),
        in_specs=[pl.BlockSpec((N_DEV, tr, c // blocks), lambda l, i: (0, i, 0))] * len(flat) + [spec] * 3,
        out_specs=[spec] * 4, out_shape=[jax.ShapeDtypeStruct((DEPTH, r, c), F32)] * 4,
        args=tuple(flat) + (w, m, v))
    return outs


def _adamw_small(sheet, extra, params, *, name):
    sheet_rows = dict(norm1_g=ROW_NORM1, b_gate=ROW_BGATE, gmlp_ln_g=ROW_LN_G, gmlp_ln_b=ROW_LN_B, norm2_g=ROW_NORM2,
                      b_ffn_conv=ROW_BFCONV)
    names = list(params)
    extra_names = list(extra)

    def body(*refs):
        sg_ref, refs = refs[0], refs[1:]
        extra_refs, refs = dict(zip(extra_names, refs[:len(extra_names)])), refs[len(extra_names):]
        ins, outs = refs[:3 * len(names)], refs[3 * len(names):]
        for j, key in enumerate(names):
            w_ref, m_ref, v_ref = ins[3 * j:3 * j + 3]
            g_ref, d_ref, nm_ref, nv_ref = outs[4 * j:4 * j + 4]
            if key in extra_refs:
                g_ref[...] = extra_refs[key][...]
            elif key == "final_g":
                g_ref[...] = sg_ref[ROW_FINAL:ROW_FINAL + 1, 0:D_MODEL]
            else:
                width = w_ref.shape[-1]
                for l in range(DEPTH):
                    row = SG_LAYER * l + sheet_rows[key]
                    g_ref[l:l + 1, :] = sg_ref[row:row + 1, 0:width]
            delta, nm, nv = _adamw_math(w_ref[...], g_ref[...], m_ref[...], v_ref[...])
            d_ref[...] = delta
            nm_ref[...] = nm
            nv_ref[...] = nv

    args = [sheet] + [extra[k] for k in extra_names] + [t for k in names for t in params[k]]
    vmem = pl.BlockSpec(memory_space=pltpu.VMEM)
    outs = pl.pallas_call(
        body, name=name, in_specs=[vmem] * len(args), out_specs=[vmem] * (4 * len(names)),
        out_shape=[jax.ShapeDtypeStruct(params[k][0].shape, F32) for k in names for _ in range(4)],
    )(*args)
    return {k: tuple(outs[4 * j:4 * j + 4]) for j, k in enumerate(names)}


def _rows(gathered):
    return gathered.reshape(N_DEV * gathered.shape[1], gathered.shape[2])


def _parts(full):
    return full.reshape(N_DEV, full.shape[0] // N_DEV, full.shape[1])


def kernel(x, norm1_g, w_in, b_gate, gmlp_ln_g, gmlp_ln_b, w_spatial, b_spatial, w_shortconv, w_branch, w_out, norm2_g, w_ffn_up, w_ffn_conv, b_ffn_conv, w_ffn_down, final_g, loss_target, m_norm1_g, m_w_in, m_b_gate, m_gmlp_ln_g, m_gmlp_ln_b, m_w_spatial, m_b_spatial, m_w_shortconv, m_w_branch, m_w_out, m_norm2_g, m_w_ffn_up, m_w_ffn_conv, m_b_ffn_conv, m_w_ffn_down, m_final_g, v_norm1_g, v_w_in, v_b_gate, v_gmlp_ln_g, v_gmlp_ln_b, v_w_spatial, v_b_spatial, v_w_shortconv, v_w_branch, v_w_out, v_norm2_g, v_w_ffn_up, v_w_ffn_conv, v_b_ffn_conv, v_w_ffn_down, v_final_g):
    n = x.shape[1]
    tm, tm_mix, tn = 512, 512, 2048
    x0 = x.reshape(n, D_MODEL)
    target = loss_target.reshape(n, D_MODEL)
    my_idx = 4 * lax.axis_index("x") + 2 * lax.axis_index("y") + lax.axis_index("c")
    sc_w, fc_w = D_B // N_DEV, D_FF // N_DEV

    sh_in = [w_in[l].T.astype(BF16) for l in range(DEPTH)]
    sh_up = [w_ffn_up[l].T.astype(BF16) for l in range(DEPTH)]
    sh_br = [w_branch[l].astype(BF16) for l in range(DEPTH)]
    sh_out = [w_out[l].astype(BF16) for l in range(DEPTH)]
    sh_down = [w_ffn_down[l].astype(BF16) for l in range(DEPTH)]
    taps = jnp.concatenate([w_shortconv, w_ffn_conv], axis=-1)

    def branch_weights(g):
        return g.transpose(1, 2, 0, 3).reshape(2, D_A, D_MODEL)

    g_in0, g_taps = _gather_now([sh_in[0], taps], name="gather_first")
    w_sc = [g_taps[:, l, :, :sc_w].transpose(1, 0, 2).reshape(3, D_B) for l in range(DEPTH)]
    w_fc = [g_taps[:, l, :, sc_w:].transpose(1, 0, 2).reshape(3, D_FF) for l in range(DEPTH)]
    w_s_t = [w_spatial[l].transpose(0, 2, 1) for l in range(DEPTH)]
    weights = [dict(), dict()]
    weights[0]["in_t"] = _rows(g_in0)
    saved = []
    xc = x0
    for l in range(DEPTH):
        p = weights[l]
        carry = _Gather([sh_br[0], sh_out[0]] if l == 0 else [sh_up[1]])
        (h, z), got = _norm_matmul(xc, norm1_g[l], p["in_t"], tm=tm, name=f"fwd_in_{l}", carry=carry)
        if l == 0:
            p["wb"], p["out"] = branch_weights(got[0]), _rows(got[1])
        else:
            p["up_t"] = _rows(got[0])
        carry = _Gather([sh_up[0]]) if l == 0 else None
        (ya, yb, conv, pa, pb, mg, x1), got = _mix_forward(
            z, xc, b_gate[l], gmlp_ln_g[l], gmlp_ln_b[l], w_spatial[l], b_spatial[l], w_sc[l], p["wb"], p["out"],
            tm=tm_mix, name=f"fwd_mix_{l}", carry=carry)
        if l == 0:
            p["up_t"] = _rows(got[0])
        (h2, up), got = _norm_matmul(x1, norm2_g[l], p["up_t"], tm=tm, name=f"fwd_up_{l}", carry=_Gather([sh_down[l]]))
        p["down"] = _rows(got[0])
        carry = _Gather([sh_br[1], sh_out[1], sh_in[1]]) if l == 0 else None
        head = (final_g, target) if l == DEPTH - 1 else None
        outs, got = _ffn_forward(up, x1, w_fc[l], b_ffn_conv[l], p["down"], tm=tm, name=f"fwd_ffn_{l}", carry=carry, head=head)
        if l == 0:
            weights[1]["wb"], weights[1]["out"], weights[1]["in_t"] = branch_weights(got[0]), _rows(got[1]), _rows(got[2])
        gc, a = outs[0], outs[1]
        saved.append(dict(x=xc, h=h, z=z, ya=ya, yb=yb, conv=conv, pa=pa, pb=pb, mg=mg, x1=x1, h2=h2, up=up, gc=gc, a=a))
        xc = outs[2]
    dx, sheet = outs[2], outs[3]

    recv = [dict(), dict()]
    small_dws, small_dbs = [None] * DEPTH, [None] * DEPTH
    pending_in = None
    for l in reversed(range(DEPTH)):
        p, s = weights[l], saved[l]
        carry = _Exchange([pending_in]) if pending_in is not None else None
        (dup, sheet), got = _ffn_backward(dx, s["up"], s["gc"], w_fc[l], p["down"], sheet, l, tm=tm, name=f"bwd_ffn_{l}",
                                          carry=carry)
        if got is not None:
            recv[l + 1]["in_t"] = got[0]
        dw_down, _ = _matmul_tn(s["a"], dx, t1=D_FF // 2, tn=tn, name=f"dw_down_{l}")
        dw_up_t, got = _matmul_tn(dup, s["h2"], t1=2 * D_FF // 4, tn=tn, name=f"dw_up_{l}", pieces=2,
                                  carry=_Exchange([_parts(dw_down)]))
        recv[l]["down"] = got[0]
        (dx1, sheet), got_left = _matmul_norm_backward(
            dup, p["up_t"], s["x1"], norm2_g[l], dx, sheet, SG_LAYER * l + ROW_NORM2, tm=tm, name=f"bwd_up_{l}",
            carry=_Exchange([_parts(dw_up_t[0])]))
        (dz, dpa, dpb, small_dws[l], small_dbs[l], sheet), got_right = _mix_backward(
            dx1, s["z"], s["conv"], s["pa"], s["pb"], b_gate[l], gmlp_ln_g[l], gmlp_ln_b[l], w_spatial[l], w_s_t[l],
            b_spatial[l], w_sc[l], p["out"], p["wb"], sheet, l, tm=tm_mix, name=f"bwd_mix_{l}",
            carry=_Exchange([_parts(dw_up_t[1])]))
        recv[l]["up_t"] = [got_left[0], got_right[0]]
        dw_out, _ = _matmul_tn(s["mg"], dx1, t1=D_MODEL, tn=tn, name=f"dw_out_{l}")
        dw_bra_t, _ = _matmul_tn(dpa, s["ya"], t1=D_MODEL, tn=tn, name=f"dw_branch_a_{l}")
        dw_brb_t, _ = _matmul_tn(dpb, s["yb"], t1=D_MODEL, tn=tn, name=f"dw_branch_b_{l}")
        carry = _Exchange([_parts(dw_out), _parts(dw_bra_t), _parts(dw_brb_t)])
        if l == 0:
            dbs = jnp.stack([t.reshape(N_HEADS, GMLP_BLOCK) for t in small_dbs]).reshape(DEPTH * N_HEADS, GMLP_BLOCK)
            carry = _Both(carry, _Gather([sheet, small_dws[0], small_dws[1], dbs]))
        dw_in_t, got = _matmul_tn(dz, s["h"], t1=D_IN // 4, tn=tn, name=f"dw_in_{l}", carry=carry)
        recv[l]["out"], recv[l]["bra_t"], recv[l]["brb_t"] = got[:3]
        if l == 0:
            gathered_small = got[3:]
            (dx0, dg1_first), got = _matmul_norm_backward(dz, p["in_t"], s["x"], norm1_g[l], dx1, None, 0, tm=tm,
                                                         name=f"bwd_in_{l}", carry=_Exchange([_parts(dw_in_t)]))
            recv[0]["in_t"] = got[0]
        else:
            (dx0, sheet), _ = _matmul_norm_backward(dz, p["in_t"], s["x"], norm1_g[l], dx1, sheet,
                                                    SG_LAYER * l + ROW_NORM1, tm=tm, name=f"bwd_in_{l}")
            pending_in = _parts(dw_in_t)
        dx = dx0
    grad_x = dx.reshape(x.shape)

    results = {}
    both = lambda key: [recv[l][key] for l in range(DEPTH)]
    blocks = lambda key: [r if isinstance(r, list) else [r] for r in both(key)]
    swap = lambda t: t.transpose(0, 2, 1)
    for key, slab, (w, m, v), tr in [("w_in", "in_t", (w_in, m_w_in, v_w_in), 192),
                                     ("w_ffn_up", "up_t", (w_ffn_up, m_w_ffn_up, v_w_ffn_up), 176)]:
        outs = _sum_adamw(blocks(slab), swap(w), swap(m), swap(v), tr=tr, name=f"adamw_{key}")
        results[key] = tuple(swap(o) for o in outs)
    g_bra = _sum_parts(both("bra_t"), tr=128, name="sum_w_branch_a").transpose(0, 2, 1)
    g_brb = _sum_parts(both("brb_t"), tr=128, name="sum_w_branch_b").transpose(0, 2, 1)
    g_br = jnp.stack([g_bra, g_brb], axis=1)
    flat = lambda t: t.reshape(-1, t.shape[-1])
    outs = _adamw(flat(w_branch), flat(g_br), flat(m_w_branch), flat(v_w_branch), tr=512, name="adamw_w_branch")
    results["w_branch"] = (g_br,) + tuple(o.reshape(w_branch.shape) for o in outs)
    results["w_out"] = tuple(_sum_adamw(blocks("out"), w_out, m_w_out, v_w_out, tr=128, name="adamw_w_out"))
    results["w_ffn_down"] = tuple(_sum_adamw(blocks("down"), w_ffn_down, m_w_ffn_down, v_w_ffn_down, tr=176,
                                             name="adamw_w_ffn_down"))

    sheet, dws0, dws1, dbs = _sum_gathered(gathered_small, name="sum_small_grads")
    (dg1_first,) = _all_reduce_small([dg1_first], name="all_reduce_last_gain")
    sheet = sheet.at[ROW_NORM1, :D_MODEL].set(dg1_first[0])
    loss = sheet[ROW_LOSS, 0]
    taps = lambda row, width: jnp.stack([sheet[SG_LAYER * l + row:SG_LAYER * l + row + 3, :width] for l in range(DEPTH)])
    extra = dict(w_spatial=jnp.stack([dws0, dws1]), b_spatial=dbs.reshape(DEPTH, N_HEADS, GMLP_BLOCK),
                 w_shortconv=lax.dynamic_slice_in_dim(taps(ROW_SCONV, D_B), my_idx * sc_w, sc_w, axis=2),
                 w_ffn_conv=lax.dynamic_slice_in_dim(taps(ROW_FCONV, D_FF), my_idx * fc_w, fc_w, axis=2))
    small_w = dict(norm1_g=(norm1_g, m_norm1_g, v_norm1_g), b_gate=(b_gate, m_b_gate, v_b_gate),
                   gmlp_ln_g=(gmlp_ln_g, m_gmlp_ln_g, v_gmlp_ln_g), gmlp_ln_b=(gmlp_ln_b, m_gmlp_ln_b, v_gmlp_ln_b),
                   w_spatial=(w_spatial, m_w_spatial, v_w_spatial), b_spatial=(b_spatial, m_b_spatial, v_b_spatial),
                   w_shortconv=(w_shortconv, m_w_shortconv, v_w_shortconv), norm2_g=(norm2_g, m_norm2_g, v_norm2_g),
                   w_ffn_conv=(w_ffn_conv, m_w_ffn_conv, v_w_ffn_conv), b_ffn_conv=(b_ffn_conv, m_b_ffn_conv, v_b_ffn_conv),
                   final_g=tuple(t.reshape(1, D_MODEL) for t in (final_g, m_final_g, v_final_g)))
    results.update(_adamw_small(sheet, extra, small_w, name="adamw_small"))
    results["final_g"] = tuple(t.reshape(D_MODEL) for t in results["final_g"])

    names = ["norm1_g", "w_in", "b_gate", "gmlp_ln_g", "gmlp_ln_b", "w_spatial", "b_spatial", "w_shortconv", "w_branch",
             "w_out", "norm2_g", "w_ffn_up", "w_ffn_conv", "b_ffn_conv", "w_ffn_down", "final_g"]
    return (loss, grad_x, *[results[k][0] for k in names], *[results[k][1] for k in names],
            *[results[k][2] for k in names], *[results[k][3] for k in names])
```

```python
import math

import jax
import jax.numpy as jnp
from jax import lax
from jax.experimental import pallas as pl
from jax.experimental.pallas import tpu as pltpu

F32 = jnp.float32
BF16 = jnp.bfloat16

N_DEV = 8
DEPTH = 2
D_MODEL = 1024
D_A = 512
D_B = 512
D_FF = 2816
D_IN = 4608
N_HEADS = 4
HEAD = 128
GMLP_BLOCK = 128
CAUSAL_CHUNK = 64
OFF_U, OFF_V, OFF_BG, OFF_CG, OFF_HB, OFF_GA, OFF_GB = 0, 512, 1024, 1536, 2048, 2560, 3584
RMS_EPS = 1e-6
LN_EPS = 1e-5
ADAM_LR, ADAM_B1, ADAM_B2, ADAM_EPS, ADAM_WD, ADAM_STEP = 0.001, 0.9, 0.999, 1e-08, 0.01, 10

SUBLANES, LANES = 8, 128
MATMUL_CHUNK = 512
HALO = 16
FFN_CHUNK = 256
SG_ROWS, SG_W, SG_LAYER = 40, D_FF, 16
ROW_NORM1, ROW_BGATE, ROW_LN_G, ROW_LN_B, ROW_SCONV, ROW_NORM2, ROW_FCONV, ROW_BFCONV = 0, 1, 2, 3, 4, 7, 8, 11
ROW_FINAL, ROW_LOSS = 32, 33
V7X_VMEM_BYTES = 64 << 20
VMEM_LIMIT = V7X_VMEM_BYTES - (8 << 20)
MESH = pl.DeviceIdType.MESH
GELU_C0 = 0.7978845608028654
GELU_C1 = 0.044715
NT = (((1,), (1,)), ((), ()))
TN = (((0,), (0,)), ((), ()))


def _dot(a, b):
    return jnp.dot(a, b, preferred_element_type=F32)


def _dot_nt(a, b):
    return lax.dot_general(a, b, NT, preferred_element_type=F32)


def _sigmoid(x):
    return 1.0 / (1.0 + jnp.exp(-x))


def _gelu_tanh(x):
    return jnp.tanh(GELU_C0 * (x + GELU_C1 * x * x * x))


def _gelu_grad(x, t):
    return 0.5 * (1.0 + t) + 0.5 * x * (1.0 - t * t) * GELU_C0 * (1.0 + 3.0 * GELU_C1 * x * x)


def _sublane_tile(dtype):
    return SUBLANES * (4 // jnp.dtype(dtype).itemsize)


def _shift_down(a, k, prev):
    p = prev.shape[0]
    r = pltpu.roll(a, k, 0)
    sub = _sublane_tile(a.dtype)
    head = r[0:sub]
    rid = lax.broadcasted_iota(jnp.int32, head.shape, 0)
    for j in range(k):
        head = jnp.where(rid == j, prev[p - k + j:p - k + j + 1, :], head)
    return jnp.concatenate([head, r[sub:]], axis=0)


def _shift_up(a, k, nxt):
    t = a.shape[0]
    r = pltpu.roll(a, t - k, 0)
    sub = _sublane_tile(a.dtype)
    tail = r[t - sub:t]
    rid = lax.broadcasted_iota(jnp.int32, tail.shape, 0)
    for j in range(k):
        tail = jnp.where(rid == sub - k + j, nxt[j:j + 1, :], tail)
    return jnp.concatenate([r[0:t - sub], tail], axis=0)


def _column_sums(p):
    if p.dtype.itemsize < 4:
        t = p.shape[0]
        p = p[:t // 2] + p[t // 2:]
        p = p[:t // 4] + p[t // 4:]
    return jnp.sum(p.astype(F32), axis=0, keepdims=True)


def _sheet_begin(step, sheet_in, sheet_out, first_row, rows):
    @pl.when(step == 0)
    def _():
        sheet_out[...] = sheet_in[...]
        sheet_out[first_row:first_row + rows, :] = jnp.zeros((rows, SG_W), F32)


def _sheet_spec():
    return pl.BlockSpec((SG_ROWS, SG_W), lambda i: (0, 0))


def _spatial_mask(transposed):
    ri = lax.broadcasted_iota(jnp.int32, (GMLP_BLOCK, GMLP_BLOCK), 0) // CAUSAL_CHUNK
    ci = lax.broadcasted_iota(jnp.int32, (GMLP_BLOCK, GMLP_BLOCK), 1) // CAUSAL_CHUNK
    return (ri <= ci) if transposed else (ci <= ri)


def _gmlp_forward(u, v, ln_g, ln_b, ws_ref, bs_ref, f_scr):
    tm = u.shape[0]
    tu = _gelu_tanh(u)
    tv = _gelu_tanh(v)
    gu = 0.5 * u * (1.0 + tu)
    gv = 0.5 * v * (1.0 + tv)
    mu = jnp.mean(gv, axis=-1, keepdims=True)
    cen = gv - mu
    rstd = lax.rsqrt(jnp.mean(cen * cen, axis=-1, keepdims=True) + LN_EPS)
    xh = cen * rstd
    vn = (xh * ln_g + ln_b).astype(BF16)
    mask = _spatial_mask(False)
    wm = [jnp.where(mask, ws_ref[h], 0.0).astype(BF16) for h in range(N_HEADS)]
    for b in range(tm // GMLP_BLOCK):
        rows = slice(b * GMLP_BLOCK, (b + 1) * GMLP_BLOCK)
        for h in range(N_HEADS):
            cols = slice(h * HEAD, (h + 1) * HEAD)
            f_scr[rows, cols] = (_dot(wm[h], vn[rows, cols]) + bs_ref[h]).astype(f_scr.dtype)
    return gu, tu, tv, xh, rstd, vn, f_scr[...]


def _position():
    return lax.axis_index("x"), lax.axis_index("y"), lax.axis_index("c")


class _Gather:
    def __init__(self, arrays):
        self.arrays = list(arrays)
        self.out_shape = [jax.ShapeDtypeStruct((N_DEV,) + a.shape, a.dtype) for a in self.arrays]
        self.base = 0

    def _plan(self, ins, outs, sems):
        send_sems, recv_sems, local_sems = sems
        x, y, c = _position()
        me, sibling = (x, y, c), (x, y, 1 - c)
        chips = [(1 - x, y), (x, 1 - y), (1 - x, 1 - y)]

        def slot(a, p):
            return outs[a].at[4 * p[0] + 2 * p[1] + p[2]]

        def copy(a, k, block, to, src=None):
            return pltpu.make_async_remote_copy(
                src_ref=slot(a, block) if src is None else src, dst_ref=slot(a, block),
                send_sem=send_sems.at[self.base + a, k], recv_sem=recv_sems.at[self.base + a, k],
                device_id=to, device_id_type=MESH)

        n = len(self.arrays)

        def mine():
            return [pltpu.make_async_copy(ins[a], slot(a, me), local_sems.at[self.base + a]) for a in range(n)]

        def first():
            out = []
            for a in range(n):
                out.append(copy(a, 0, me, sibling, src=ins[a]))
                out += [copy(a, 1 + j, me, (*chip, c), src=ins[a]) for j, chip in enumerate(chips)]
            return out

        def arrivals():
            return [copy(a, 1 + j, (*chip, c), me) for j, chip in enumerate(chips) for a in range(n)]

        def relays():
            return [copy(a, 4 + j, (*chip, c), sibling) for j, chip in enumerate(chips) for a in range(n)]

        def from_sibling():
            out = [copy(a, 0, sibling, me) for a in range(n)]
            return out + [copy(a, 4 + j, (*chip, 1 - c), me) for j, chip in enumerate(chips) for a in range(n)]

        return mine, first, arrivals, relays, from_sibling

    def start(self, ins, outs, sems):
        mine, first, _, _, _ = self._plan(ins, outs, sems)
        for cp in mine() + first():
            cp.start()

    def relay(self, ins, outs, sems):
        _, _, arrivals, relays, _ = self._plan(ins, outs, sems)
        for arrived, onward in zip(arrivals(), relays()):
            arrived.wait_recv()
            onward.start()

    def finish(self, ins, outs, sems):
        mine, first, _, relays, from_sibling = self._plan(ins, outs, sems)
        for cp in from_sibling():
            cp.wait_recv()
        for cp in first() + relays():
            cp.wait_send()
        for cp in mine():
            cp.wait()


class _Exchange:
    def __init__(self, arrays):
        self.arrays = list(arrays)
        self.out_shape = [jax.ShapeDtypeStruct(a.shape, a.dtype) for a in self.arrays]
        self.base = 0

    def _plan(self, ins, outs, sems):
        send_sems, recv_sems, local_sems = sems
        x, y, c = _position()
        my_idx = 4 * x + 2 * y + c
        n = len(self.arrays)
        offsets = [(dx, dy, dc) for dx in (0, 1) for dy in (0, 1) for dc in (0, 1) if (dx, dy, dc) != (0, 0, 0)]

        def mine():
            return [pltpu.make_async_copy(ins[a].at[my_idx], outs[a].at[my_idx], local_sems.at[self.base + a])
                    for a in range(n)]

        def remote(arriving):
            out = []
            for k, (dx, dy, dc) in enumerate(offsets):
                px, py, pc = x ^ dx, y ^ dy, c ^ dc
                p_idx = 4 * px + 2 * py + pc
                for a in range(n):
                    out.append(pltpu.make_async_remote_copy(
                        src_ref=ins[a].at[p_idx], dst_ref=outs[a].at[p_idx if arriving else my_idx],
                        send_sem=send_sems.at[self.base + a, k], recv_sem=recv_sems.at[self.base + a, k],
                        device_id=(px, py, pc), device_id_type=MESH))
            return out

        return mine, remote

    def start(self, ins, outs, sems):
        mine, remote = self._plan(ins, outs, sems)
        for cp in mine() + remote(False):
            cp.start()

    def relay(self, ins, outs, sems):
        pass

    def finish(self, ins, outs, sems):
        mine, remote = self._plan(ins, outs, sems)
        for cp in remote(True):
            cp.wait_recv()
        for cp in remote(False):
            cp.wait_send()
        for cp in mine():
            cp.wait()


class _Both:
    def __init__(self, *carries):
        self.carries = carries
        self.arrays = [a for c in carries for a in c.arrays]
        self.out_shape = [s for c in carries for s in c.out_shape]
        first = 0
        for c in carries:
            c.base = first
            first += len(c.arrays)

    def _each(self, method, ins, outs, sems):
        for c in self.carries:
            rows = slice(c.base, c.base + len(c.arrays))
            getattr(c, method)(ins[rows], outs[rows], sems)

    def start(self, ins, outs, sems):
        self._each("start", ins, outs, sems)

    def relay(self, ins, outs, sems):
        self._each("relay", ins, outs, sems)

    def finish(self, ins, outs, sems):
        self._each("finish", ins, outs, sems)


def _call(body, *, name, grid, in_specs, out_specs, out_shape, args, scratch_shapes=(), carry=None):
    n_in, n_out, n_scr = len(in_specs), len(out_specs), len(scratch_shapes)
    params = pltpu.CompilerParams(dimension_semantics=("arbitrary",) * len(grid), vmem_limit_bytes=VMEM_LIMIT)
    if carry is None:
        outs = pl.pallas_call(body, name=name, grid=grid, in_specs=in_specs, out_specs=out_specs, out_shape=out_shape,
                              scratch_shapes=list(scratch_shapes), compiler_params=params)(*args)
        return outs, None
    m = len(carry.arrays)
    total = math.prod(grid)

    def wrapped(*refs):
        ins, refs = refs[:n_in], refs[n_in:]
        c_ins, refs = refs[:m], refs[m:]
        outs, refs = refs[:n_out], refs[n_out:]
        c_outs, refs = refs[:m], refs[m:]
        scr, sems = refs[:n_scr], refs[n_scr:]
        flat = pl.program_id(0)
        for d in range(1, len(grid)):
            flat = flat * grid[d] + pl.program_id(d)

        @pl.when(flat == 0)
        def _():
            carry.start(c_ins, c_outs, sems)

        body(*ins, *outs, *scr)

        @pl.when(flat == total - 2)
        def _():
            carry.relay(c_ins, c_outs, sems)

        @pl.when(flat == total - 1)
        def _():
            carry.finish(c_ins, c_outs, sems)

    any_spec = pl.BlockSpec(memory_space=pl.ANY)
    sem_shapes = [pltpu.SemaphoreType.DMA((m, 7)), pltpu.SemaphoreType.DMA((m, 7)), pltpu.SemaphoreType.DMA((m,))]
    outs = pl.pallas_call(
        wrapped, name=name, grid=grid,
        in_specs=list(in_specs) + [any_spec] * m, out_specs=list(out_specs) + [any_spec] * m,
        out_shape=list(out_shape) + carry.out_shape,
        scratch_shapes=list(scratch_shapes) + sem_shapes, compiler_params=params)(*args, *carry.arrays)
    return outs[:n_out], outs[n_out:]


def _gather_now(arrays, *, name):
    carry = _Gather(arrays)
    m = len(arrays)

    def body(*refs):
        ins, outs, sems = refs[:m], refs[m:2 * m], refs[2 * m:]
        carry.start(ins, outs, sems)
        carry.relay(ins, outs, sems)
        carry.finish(ins, outs, sems)

    any_spec = pl.BlockSpec(memory_space=pl.ANY)
    return pl.pallas_call(
        body, name=name, in_specs=[any_spec] * m, out_specs=[any_spec] * m, out_shape=carry.out_shape,
        scratch_shapes=[pltpu.SemaphoreType.DMA((m, 7)), pltpu.SemaphoreType.DMA((m, 7)),
                        pltpu.SemaphoreType.DMA((m,))],
    )(*arrays)


def _all_reduce_small(arrs, *, name):
    n = len(arrs)

    def body(*refs):
        ins, outs, bufs = refs[:n], refs[n:2 * n], refs[2 * n:3 * n]
        send_sems, recv_sems = refs[3 * n:]
        x, y, c = _position()
        me, sibling = (x, y, c), (x, y, 1 - c)
        chips = [(1 - x, y), (x, 1 - y), (1 - x, 1 - y)]

        def copy(a, k, block, to, src=None):
            slot = bufs[a].at[4 * block[0] + 2 * block[1] + block[2]]
            return pltpu.make_async_remote_copy(
                src_ref=slot if src is None else src, dst_ref=slot,
                send_sem=send_sems.at[a, k], recv_sem=recv_sems.at[a, k], device_id=to, device_id_type=MESH)

        first = []
        for a in range(n):
            first.append(copy(a, 0, me, sibling, src=ins[a]))
            first += [copy(a, 1 + j, me, (*chip, c), src=ins[a]) for j, chip in enumerate(chips)]
        for cp in first:
            cp.start()
        passed = []
        for j, chip in enumerate(chips):
            for a in range(n):
                copy(a, 1 + j, (*chip, c), me).wait_recv()
                cp = copy(a, 4 + j, (*chip, c), sibling)
                cp.start()
                passed.append(cp)
        for a in range(n):
            copy(a, 0, sibling, me).wait_recv()
            for j, chip in enumerate(chips):
                copy(a, 4 + j, (*chip, 1 - c), me).wait_recv()
        for cp in first + passed:
            cp.wait_send()
        my_idx = 4 * x + 2 * y + c
        for a in range(n):
            acc = jnp.zeros(ins[a].shape, F32)
            for s in range(N_DEV):
                acc = acc + jnp.where(my_idx == s, ins[a][...], bufs[a][s])
            outs[a][...] = acc

    vmem = pl.BlockSpec(memory_space=pltpu.VMEM)
    return pl.pallas_call(
        body, name=name, in_specs=[vmem] * n, out_specs=[vmem] * n,
        out_shape=[jax.ShapeDtypeStruct(a.shape, F32) for a in arrs],
        scratch_shapes=[pltpu.VMEM((N_DEV,) + a.shape, F32) for a in arrs]
        + [pltpu.SemaphoreType.DMA((n, 7)), pltpu.SemaphoreType.DMA((n, 7))],
        compiler_params=pltpu.CompilerParams(vmem_limit_bytes=VMEM_LIMIT),
    )(*arrs)


def _sum_gathered(arrs, *, name):
    n = len(arrs)

    def body(*refs):
        for in_ref, out_ref in zip(refs[:n], refs[n:]):
            acc = in_ref[0]
            for s in range(1, N_DEV):
                acc = acc + in_ref[s]
            out_ref[...] = acc

    vmem = pl.BlockSpec(memory_space=pltpu.VMEM)
    return pl.pallas_call(
        body, name=name, in_specs=[vmem] * n, out_specs=[vmem] * n,
        out_shape=[jax.ShapeDtypeStruct(a.shape[1:], F32) for a in arrs],
        compiler_params=pltpu.CompilerParams(vmem_limit_bytes=VMEM_LIMIT),
    )(*arrs)


def _norm_matmul(x, g, w_t, *, tm, name, carry=None):
    n, d = x.shape
    c = w_t.shape[0]
    ch = MATMUL_CHUNK

    def body(x_ref, g_ref, wt_ref, h_ref, z_ref):
        xv = x_ref[...]
        r = lax.rsqrt(jnp.mean(xv * xv, axis=-1, keepdims=True) + RMS_EPS)
        h = (xv * r * g_ref[...]).astype(BF16)
        h_ref[...] = h
        for c0 in range(0, c, ch):
            z_ref[:, c0:c0 + ch] = _dot_nt(h, wt_ref[c0:c0 + ch, :]).astype(BF16)

    return _call(
        body, name=name, grid=(n // tm,), carry=carry,
        in_specs=[pl.BlockSpec((tm, d), lambda i: (i, 0)),
                  pl.BlockSpec((1, d), lambda i: (0, 0)),
                  pl.BlockSpec((c, d), lambda i: (0, 0))],
        out_specs=[pl.BlockSpec((tm, d), lambda i: (i, 0)),
                   pl.BlockSpec((tm, c), lambda i: (i, 0))],
        out_shape=[jax.ShapeDtypeStruct((n, d), BF16), jax.ShapeDtypeStruct((n, c), BF16)],
        args=(x, g.reshape(1, d), w_t))


def _mix_forward(z, x, b_gate, ln_g, ln_b, w_s, b_s, w_sc, wb, w_out, *, tm, name, carry=None):
    n = z.shape[0]
    hb = tm // HALO

    def body(z_ref, zp_ref, x_ref, bg_ref, lng_ref, lnb_ref, ws_ref, bs_ref, wsc_ref, wb_ref, wo_ref,
             ya_ref, yb_ref, cv_ref, pa_ref, pb_ref, mg_ref, x1_ref, f_scr):
        i = pl.program_id(0)
        u = z_ref[:, OFF_U:OFF_U + D_A]
        v = z_ref[:, OFF_V:OFF_V + D_A].astype(F32)
        gu, _, _, _, _, _, f = _gmlp_forward(u, v, lng_ref[...], lnb_ref[...], ws_ref, bs_ref, f_scr)
        ya = gu * f
        ya_ref[...] = ya

        q = z_ref[:, OFF_CG:OFF_CG + D_B] * z_ref[:, OFF_HB:OFF_HB + D_B]
        qp = zp_ref[:, OFF_CG:OFF_CG + D_B] * zp_ref[:, OFF_HB:OFF_HB + D_B]
        qp = jnp.where(i > 0, qp, jnp.zeros_like(qp))
        w = wsc_ref[...].astype(BF16)
        conv = w[0:1] * _shift_down(q, 2, qp) + w[1:2] * _shift_down(q, 1, qp) + w[2:3] * q
        cv_ref[...] = conv
        yb = z_ref[:, OFF_BG:OFF_BG + D_B] * conv
        yb_ref[...] = yb

        pa = _dot(ya, wb_ref[0]).astype(BF16)
        pb = _dot(yb, wb_ref[1]).astype(BF16)
        pa_ref[...] = pa
        pb_ref[...] = pb
        bg = bg_ref[...].astype(BF16)
        sa = _sigmoid(z_ref[:, OFF_GA:OFF_GA + D_MODEL] + bg[:, 0:D_MODEL])
        sb = _sigmoid(z_ref[:, OFF_GB:OFF_GB + D_MODEL] + bg[:, D_MODEL:2 * D_MODEL])
        mg = sa * pa + sb * pb
        mg_ref[...] = mg
        x1_ref[...] = x_ref[...] + _dot(mg, wo_ref[...])

    row = lambda w: pl.BlockSpec((tm, w), lambda i: (i, 0))
    full = lambda *s: pl.BlockSpec(s, lambda i: (0,) * len(s))
    bf = lambda w: jax.ShapeDtypeStruct((n, w), BF16)
    return _call(
        body, name=name, grid=(n // tm,), carry=carry,
        in_specs=[row(D_IN),
                  pl.BlockSpec((HALO, D_IN), lambda i: (jnp.maximum(i * hb - 1, 0), 0)),
                  row(D_MODEL), full(1, 2 * D_MODEL), full(1, D_A), full(1, D_A),
                  full(N_HEADS, GMLP_BLOCK, GMLP_BLOCK), full(N_HEADS, GMLP_BLOCK, 1), full(3, D_B),
                  full(2, D_A, D_MODEL), full(D_MODEL, D_MODEL)],
        out_specs=[row(D_A), row(D_B), row(D_B), row(D_MODEL), row(D_MODEL), row(D_MODEL), row(D_MODEL)],
        out_shape=[bf(D_A), bf(D_B), bf(D_B), bf(D_MODEL), bf(D_MODEL), bf(D_MODEL),
                   jax.ShapeDtypeStruct((n, D_MODEL), F32)],
        scratch_shapes=[pltpu.VMEM((tm, D_A), BF16)],
        args=(z, z, x, b_gate.reshape(1, -1), ln_g.reshape(1, -1), ln_b.reshape(1, -1), w_s,
              b_s.reshape(N_HEADS, GMLP_BLOCK, 1), w_sc, wb, w_out))


def _loss_tile(xv, gv, tv):
    d = xv.shape[-1]
    r = lax.rsqrt(jnp.mean(xv * xv, axis=-1, keepdims=True) + RMS_EPS)
    xh = xv * r
    e = xh * gv - tv
    per_row = jnp.sum(e * e, axis=-1, keepdims=True) * (0.5 / d)
    dy = e * (1.0 / d)
    dxh = dy * gv
    dx = r * (dxh - xh * jnp.mean(dxh * xh, axis=-1, keepdims=True))
    return dx, jnp.sum(per_row, axis=0, keepdims=True), jnp.sum(dy * xh, axis=0, keepdims=True)


def _ffn_forward(up, x1, w_fc, b_fc, w_down, *, tm, name, carry=None, head=None):
    n = up.shape[0]
    hb = tm // HALO
    n_in = 6 if head is None else 8

    def body(*refs):
        up_ref, upp_ref, x1_ref, wfc_ref, bfc_ref, wd_ref = refs[:6]
        gc_ref, a_ref, out_ref = refs[n_in:n_in + 3]
        acc = refs[-1]
        i = pl.program_id(0)
        acc[...] = x1_ref[...]
        for c0 in range(0, D_FF, FFN_CHUNK):
            cols = slice(c0, c0 + FFN_CHUNK)
            gate = up_ref[:, cols]
            val = up_ref[:, D_FF + c0:D_FF + c0 + FFN_CHUNK]
            gp = upp_ref[:, cols]
            gp = jnp.where(i > 0, gp, jnp.zeros_like(gp))
            w = wfc_ref[:, cols].astype(BF16)
            gc = (w[0:1] * _shift_down(gate, 2, gp) + w[1:2] * _shift_down(gate, 1, gp) + w[2:3] * gate
                  + bfc_ref[:, cols].astype(BF16))
            gc_ref[:, cols] = gc
            a = gc * _sigmoid(gc) * val
            a_ref[:, cols] = a
            acc[...] += _dot(a, wd_ref[cols, :])
        if head is None:
            out_ref[...] = acc[...]
        else:
            g_ref, t_ref = refs[6:8]
            sg_ref = refs[n_in + 3]

            @pl.when(i == 0)
            def _():
                sg_ref[...] = jnp.zeros_like(sg_ref)

            dx, loss, dg = _loss_tile(acc[...], g_ref[...], t_ref[...])
            out_ref[...] = dx
            sg_ref[ROW_LOSS:ROW_LOSS + 1, 0:LANES] += jnp.broadcast_to(loss, (1, LANES))
            sg_ref[ROW_FINAL:ROW_FINAL + 1, 0:D_MODEL] += dg

    row = lambda w: pl.BlockSpec((tm, w), lambda i: (i, 0))
    full = lambda r, c: pl.BlockSpec((r, c), lambda i: (0, 0))
    in_specs = [row(2 * D_FF), pl.BlockSpec((HALO, D_FF), lambda i: (jnp.maximum(i * hb - 1, 0), 0)), row(D_MODEL),
                full(3, D_FF), full(1, D_FF), full(D_FF, D_MODEL)]
    out_specs = [row(D_FF), row(D_FF), row(D_MODEL)]
    out_shape = [jax.ShapeDtypeStruct((n, D_FF), BF16), jax.ShapeDtypeStruct((n, D_FF), BF16),
                 jax.ShapeDtypeStruct((n, D_MODEL), F32)]
    args = (up, up, x1, w_fc, b_fc.reshape(1, -1), w_down)
    if head is not None:
        in_specs += [full(1, D_MODEL), row(D_MODEL)]
        out_specs += [full(SG_ROWS, SG_W)]
        out_shape += [jax.ShapeDtypeStruct((SG_ROWS, SG_W), F32)]
        args += (head[0].reshape(1, -1), head[1])
    return _call(body, name=name, grid=(n // tm,), carry=carry, in_specs=in_specs, out_specs=out_specs,
                 out_shape=out_shape, scratch_shapes=[pltpu.VMEM((tm, D_MODEL), F32)], args=args)


def _ffn_backward(dx2, up, gc, w_fc, w_down, sheet, layer, *, tm, name, carry=None):
    n = up.shape[0]
    steps = n // tm
    hb = tm // HALO
    row = SG_LAYER * layer + ROW_FCONV

    def body(dx_ref, dxn_ref, up_ref, upn_ref, gc_ref, gcn_ref, wfc_ref, wd_ref, sg_in, dup_ref, sg_ref):
        i = pl.program_id(0)
        last = i == steps - 1
        _sheet_begin(i, sg_in, sg_ref, row, 4)

        dxe = jnp.concatenate([dx_ref[...], dxn_ref[...]], axis=0).astype(BF16)
        for c0 in range(0, D_FF, FFN_CHUNK):
            cols = slice(c0, c0 + FFN_CHUNK)
            vcols = slice(D_FF + c0, D_FF + c0 + FFN_CHUNK)
            dae = _dot_nt(dxe, wd_ref[cols, :])
            da, dan = dae[:tm], dae[tm:]
            gate = up_ref[:, cols]
            val = up_ref[:, vcols]
            gcv = gc_ref[:, cols]
            s = _sigmoid(gcv)
            dab = da.astype(BF16)
            dup_ref[:, vcols] = dab * (gcv * s)
            dgc = dab * val * (s * (1.0 + gcv * (1.0 - s)))
            gcn = gcn_ref[:, cols]
            sn = _sigmoid(gcn)
            dgcn = dan.astype(BF16) * upn_ref[:, vcols] * (sn * (1.0 + gcn * (1.0 - sn)))
            dgcn = jnp.where(last, jnp.zeros_like(dgcn), dgcn)
            up1 = _shift_up(dgc, 1, dgcn)
            up2 = _shift_up(dgc, 2, dgcn)
            w = wfc_ref[:, cols].astype(BF16)
            dup_ref[:, cols] = w[2:3] * dgc + w[1:2] * up1 + w[0:1] * up2
            sg_ref[row:row + 1, cols] += _column_sums(gate * up2)
            sg_ref[row + 1:row + 2, cols] += _column_sums(gate * up1)
            sg_ref[row + 2:row + 3, cols] += _column_sums(gate * dgc)
            sg_ref[row + 3:row + 4, cols] += _column_sums(dgc)

    nxt = lambda i: (jnp.minimum((i + 1) * hb, steps * hb - 1), 0)
    return _call(
        body, name=name, grid=(steps,), carry=carry,
        in_specs=[pl.BlockSpec((tm, D_MODEL), lambda i: (i, 0)),
                  pl.BlockSpec((HALO, D_MODEL), nxt),
                  pl.BlockSpec((tm, 2 * D_FF), lambda i: (i, 0)),
                  pl.BlockSpec((HALO, 2 * D_FF), nxt),
                  pl.BlockSpec((tm, D_FF), lambda i: (i, 0)),
                  pl.BlockSpec((HALO, D_FF), nxt),
                  pl.BlockSpec((3, D_FF), lambda i: (0, 0)),
                  pl.BlockSpec((D_FF, D_MODEL), lambda i: (0, 0)), _sheet_spec()],
        out_specs=[pl.BlockSpec((tm, 2 * D_FF), lambda i: (i, 0)), _sheet_spec()],
        out_shape=[jax.ShapeDtypeStruct((n, 2 * D_FF), BF16), jax.ShapeDtypeStruct((SG_ROWS, SG_W), F32)],
        args=(dx2, dx2, up, up, gc, gc, w_fc, w_down, sheet))


def _matmul_norm_backward(dz, w_t, x, g, dres, sheet, row, *, tm, name, carry=None):
    n, c = dz.shape
    d = x.shape[1]
    ch = MATMUL_CHUNK

    def body(dz_ref, wt_ref, x_ref, g_ref, dres_ref, *rest):
        i = pl.program_id(0)
        if sheet is None:
            dx_ref, sg_ref = rest

            @pl.when(i == 0)
            def _():
                sg_ref[...] = jnp.zeros_like(sg_ref)
        else:
            sg_in, dx_ref, sg_ref = rest
            _sheet_begin(i, sg_in, sg_ref, row, 1)

        dh = _dot(dz_ref[:, 0:ch], wt_ref[0:ch, :])
        for c0 in range(ch, c, ch):
            dh += _dot(dz_ref[:, c0:c0 + ch], wt_ref[c0:c0 + ch, :])
        xv = x_ref[...]
        r = lax.rsqrt(jnp.mean(xv * xv, axis=-1, keepdims=True) + RMS_EPS)
        xh = xv * r
        sg_ref[row:row + 1, 0:d] += jnp.sum(dh * xh, axis=0, keepdims=True)
        dxh = dh * g_ref[...]
        dx_ref[...] = dres_ref[...] + r * (dxh - xh * jnp.mean(dxh * xh, axis=-1, keepdims=True))

    in_specs = [pl.BlockSpec((tm, c), lambda i: (i, 0)),
                pl.BlockSpec((c, d), lambda i: (0, 0)),
                pl.BlockSpec((tm, d), lambda i: (i, 0)),
                pl.BlockSpec((1, d), lambda i: (0, 0)),
                pl.BlockSpec((tm, d), lambda i: (i, 0))]
    args = (dz, w_t, x, g.reshape(1, d), dres)
    if sheet is None:
        small_spec, small_shape = pl.BlockSpec((8, d), lambda i: (0, 0)), jax.ShapeDtypeStruct((8, d), F32)
    else:
        in_specs, args = in_specs + [_sheet_spec()], args + (sheet,)
        small_spec, small_shape = _sheet_spec(), jax.ShapeDtypeStruct((SG_ROWS, SG_W), F32)
    return _call(
        body, name=name, grid=(n // tm,), carry=carry, in_specs=in_specs,
        out_specs=[pl.BlockSpec((tm, d), lambda i: (i, 0)), small_spec],
        out_shape=[jax.ShapeDtypeStruct((n, d), F32), small_shape], args=args)


def _mix_backward(dx1, z, conv, pa, pb, b_gate, ln_g, ln_b, w_s, w_s_t, b_s, w_sc, w_out, wb, sheet, layer, *, tm, name,
                  carry=None):
    n = z.shape[0]
    steps = n // tm
    hb = tm // HALO
    base = SG_LAYER * layer
    r_bg, r_lng, r_lnb, r_sc = base + ROW_BGATE, base + ROW_LN_G, base + ROW_LN_B, base + ROW_SCONV

    def body(dx_ref, dxn_ref, z_ref, zn_ref, cv_ref, pa_ref, pb_ref, bg_ref, lng_ref, lnb_ref, ws_ref, wst_ref,
             bs_ref, wsc_ref, wo_ref, wb_ref, sg_in,
             dz_ref, dpa_ref, dpb_ref, dws_ref, dbs_ref, sg_ref, f_scr, dvn_scr):
        i = pl.program_id(0)
        last = i == steps - 1
        _sheet_begin(i, sg_in, sg_ref, r_bg, ROW_NORM2 - ROW_BGATE)

        @pl.when(i == 0)
        def _():
            dws_ref[...] = jnp.zeros_like(dws_ref)
            dbs_ref[...] = jnp.zeros_like(dbs_ref)

        dxe = jnp.concatenate([dx_ref[...], dxn_ref[...]], axis=0).astype(BF16)
        dmge = _dot_nt(dxe, wo_ref[...])
        dmg, dmgn = dmge[:tm].astype(BF16), dmge[tm:].astype(BF16)

        pa_v = pa_ref[...]
        pb_v = pb_ref[...]
        bg = bg_ref[...].astype(BF16)
        sa = _sigmoid(z_ref[:, OFF_GA:OFF_GA + D_MODEL] + bg[:, 0:D_MODEL])
        sb = _sigmoid(z_ref[:, OFF_GB:OFF_GB + D_MODEL] + bg[:, D_MODEL:2 * D_MODEL])
        dpa = dmg * sa
        dpb = dmg * sb
        dga = dmg * pa_v * sa * (1.0 - sa)
        dgb = dmg * pb_v * sb * (1.0 - sb)
        dpa_ref[...] = dpa
        dpb_ref[...] = dpb
        dz_ref[:, OFF_GA:OFF_GA + D_MODEL] = dga
        dz_ref[:, OFF_GB:OFF_GB + D_MODEL] = dgb
        sg_ref[r_bg:r_bg + 1, 0:D_MODEL] += _column_sums(dga)
        sg_ref[r_bg:r_bg + 1, D_MODEL:2 * D_MODEL] += _column_sums(dgb)

        dya = _dot_nt(dpa, wb_ref[0]).astype(BF16)
        u = z_ref[:, OFF_U:OFF_U + D_A]
        v = z_ref[:, OFF_V:OFF_V + D_A].astype(F32)
        ln_g = lng_ref[...]
        gu, tu, tv, xh, rstd, vn, f = _gmlp_forward(u, v, ln_g, lnb_ref[...], ws_ref, bs_ref, f_scr)
        dgu = dya * f
        df_bf = dya * gu
        dz_ref[:, OFF_U:OFF_U + D_A] = dgu * _gelu_grad(u, tu)
        mask = _spatial_mask(False)
        mask_t = _spatial_mask(True)
        wmt = [jnp.where(mask_t, wst_ref[h], 0.0).astype(BF16) for h in range(N_HEADS)]
        for b in range(tm // GMLP_BLOCK):
            rows = slice(b * GMLP_BLOCK, (b + 1) * GMLP_BLOCK)
            for h in range(N_HEADS):
                cols = slice(h * HEAD, (h + 1) * HEAD)
                dfb = df_bf[rows, cols]
                dvn_scr[rows, cols] = _dot(wmt[h], dfb)
                dws_ref[h] += jnp.where(mask, _dot_nt(dfb, vn[rows, cols]), 0.0)
                dbs_ref[h] += jnp.sum(dfb.astype(F32), axis=1, keepdims=True)
        dvn = dvn_scr[...]
        sg_ref[r_lng:r_lng + 1, 0:D_A] += jnp.sum(dvn * xh, axis=0, keepdims=True)
        sg_ref[r_lnb:r_lnb + 1, 0:D_A] += jnp.sum(dvn, axis=0, keepdims=True)
        dxh = dvn * ln_g
        dgv = rstd * (dxh - jnp.mean(dxh, axis=-1, keepdims=True) - xh * jnp.mean(dxh * xh, axis=-1, keepdims=True))
        dz_ref[:, OFF_V:OFF_V + D_A] = (dgv * _gelu_grad(v, tv)).astype(BF16)

        sbn = _sigmoid(zn_ref[:, OFF_GB:OFF_GB + D_MODEL] + bg[:, D_MODEL:2 * D_MODEL])
        dpbe = jnp.concatenate([dpb, dmgn * sbn], axis=0)
        dybe = _dot_nt(dpbe, wb_ref[1])
        dyb, dybn = dybe[:tm].astype(BF16), dybe[tm:].astype(BF16)
        bgv = z_ref[:, OFF_BG:OFF_BG + D_B]
        cg = z_ref[:, OFF_CG:OFF_CG + D_B]
        hbv = z_ref[:, OFF_HB:OFF_HB + D_B]
        q = cg * hbv
        dz_ref[:, OFF_BG:OFF_BG + D_B] = dyb * cv_ref[...]
        dconv = dyb * bgv
        dconvn = dybn * zn_ref[:, OFF_BG:OFF_BG + D_B]
        dconvn = jnp.where(last, jnp.zeros_like(dconvn), dconvn)
        up1 = _shift_up(dconv, 1, dconvn)
        up2 = _shift_up(dconv, 2, dconvn)
        sg_ref[r_sc:r_sc + 1, 0:D_B] += _column_sums(q * up2)
        sg_ref[r_sc + 1:r_sc + 2, 0:D_B] += _column_sums(q * up1)
        sg_ref[r_sc + 2:r_sc + 3, 0:D_B] += _column_sums(q * dconv)
        w = wsc_ref[...].astype(BF16)
        dq = w[2:3] * dconv + w[1:2] * up1 + w[0:1] * up2
        dz_ref[:, OFF_CG:OFF_CG + D_B] = dq * hbv
        dz_ref[:, OFF_HB:OFF_HB + D_B] = dq * cg

    row = lambda w: pl.BlockSpec((tm, w), lambda i: (i, 0))
    full = lambda *s: pl.BlockSpec(s, lambda i: (0,) * len(s))
    nxt = lambda i: (jnp.minimum((i + 1) * hb, steps * hb - 1), 0)
    return _call(
        body, name=name, grid=(steps,), carry=carry,
        in_specs=[row(D_MODEL), pl.BlockSpec((HALO, D_MODEL), nxt),
                  row(D_IN), pl.BlockSpec((HALO, D_IN), nxt),
                  row(D_B), row(D_MODEL), row(D_MODEL),
                  full(1, 2 * D_MODEL), full(1, D_A), full(1, D_A),
                  full(N_HEADS, GMLP_BLOCK, GMLP_BLOCK), full(N_HEADS, GMLP_BLOCK, GMLP_BLOCK),
                  full(N_HEADS, GMLP_BLOCK, 1), full(3, D_B),
                  full(D_MODEL, D_MODEL), full(2, D_A, D_MODEL), _sheet_spec()],
        out_specs=[row(D_IN), row(D_MODEL), row(D_MODEL), full(N_HEADS, GMLP_BLOCK, GMLP_BLOCK),
                   full(N_HEADS, GMLP_BLOCK, 1), _sheet_spec()],
        out_shape=[jax.ShapeDtypeStruct((n, D_IN), BF16), jax.ShapeDtypeStruct((n, D_MODEL), BF16),
                   jax.ShapeDtypeStruct((n, D_MODEL), BF16),
                   jax.ShapeDtypeStruct((N_HEADS, GMLP_BLOCK, GMLP_BLOCK), F32),
                   jax.ShapeDtypeStruct((N_HEADS, GMLP_BLOCK, 1), F32), jax.ShapeDtypeStruct((SG_ROWS, SG_W), F32)],
        scratch_shapes=[pltpu.VMEM((tm, D_A), BF16), pltpu.VMEM((tm, D_A), F32)],
        args=(dx1, dx1, z, z, conv, pa, pb, b_gate.reshape(1, -1), ln_g.reshape(1, -1), ln_b.reshape(1, -1), w_s, w_s_t,
              b_s.reshape(N_HEADS, GMLP_BLOCK, 1), w_sc, w_out, wb, sheet))


def _matmul_tn(a, b, *, t1, tn, name, carry=None, pieces=1):
    n, k1 = a.shape
    k2 = b.shape[1]
    steps = n // tn
    w = k2 // pieces

    def body(a_ref, b_ref, *rest):
        o_refs, acc = rest[:pieces], rest[pieces]
        s = pl.program_id(1)

        @pl.when(s == 0)
        def _():
            acc[...] = jnp.zeros_like(acc)

        acc[...] += lax.dot_general(a_ref[...].astype(BF16), b_ref[...].astype(BF16), TN, preferred_element_type=F32)

        @pl.when(s == steps - 1)
        def _():
            for c, o_ref in enumerate(o_refs):
                o_ref[...] = acc[:, c * w:(c + 1) * w].astype(BF16)

    outs, carried = _call(
        body, name=name, grid=(k1 // t1, steps), carry=carry,
        in_specs=[pl.BlockSpec((tn, t1), lambda i, s: (s, i)),
                  pl.BlockSpec((tn, k2), lambda i, s: (s, 0))],
        out_specs=[pl.BlockSpec((t1, w), lambda i, s: (i, 0))] * pieces,
        out_shape=[jax.ShapeDtypeStruct((k1, w), BF16)] * pieces,
        scratch_shapes=[pltpu.VMEM((t1, k2), F32)],
        args=(a, b))
    return (outs[0] if pieces == 1 else list(outs)), carried


def _adamw_math(w, g, m, v):
    m = ADAM_B1 * m + (1.0 - ADAM_B1) * g
    v = ADAM_B2 * v + (1.0 - ADAM_B2) * (g * g)
    m_hat = m / (1.0 - ADAM_B1 ** ADAM_STEP)
    v_hat = v / (1.0 - ADAM_B2 ** ADAM_STEP)
    delta = -ADAM_LR * (m_hat / (jnp.sqrt(v_hat) + ADAM_EPS) + ADAM_WD * w)
    return delta, m, v


def _sum_parts(recvs, *, tr, name):
    _, r, c = recvs[0].shape

    def body(*refs):
        recv_refs, g_ref = refs[:DEPTH], refs[DEPTH]
        layer = pl.program_id(0)
        for l in range(DEPTH):
            @pl.when(layer == l)
            def _(l=l):
                g = recv_refs[l][0].astype(F32)
                for s in range(1, N_DEV):
                    g = g + recv_refs[l][s].astype(F32)
                g_ref[0] = g

    outs, _ = _call(
        body, name=name, grid=(DEPTH, r // tr),
        in_specs=[pl.BlockSpec((N_DEV, tr, c), lambda l, i: (0, i, 0))] * DEPTH,
        out_specs=[pl.BlockSpec((1, tr, c), lambda l, i: (l, i, 0))],
        out_shape=[jax.ShapeDtypeStruct((DEPTH, r, c), F32)],
        args=tuple(recvs))
    return outs[0]


def _adamw(w, g, m, v, *, tr, name):
    r, c = w.shape

    def body(w_ref, g_ref, m_ref, v_ref, d_ref, nm_ref, nv_ref):
        delta, nm, nv = _adamw_math(w_ref[...], g_ref[...], m_ref[...], v_ref[...])
        d_ref[...] = delta
        nm_ref[...] = nm
        nv_ref[...] = nv

    spec = pl.BlockSpec((tr, c), lambda i: (i, 0))
    outs, _ = _call(body, name=name, grid=(r // tr,), in_specs=[spec] * 4, out_specs=[spec] * 3,
                    out_shape=[jax.ShapeDtypeStruct((r, c), F32)] * 3, args=(w, g, m, v))
    return outs


def _sum_adamw(recvs, w, m, v, *, tr, name):
    _, r, c = w.shape
    blocks = len(recvs[0])
    flat = [piece for layer in recvs for piece in layer]

    def body(*refs):
        recv_refs = refs[:len(flat)]
        w_ref, m_ref, v_ref, g_ref, d_ref, nm_ref, nv_ref = refs[len(flat):]
        layer = pl.program_id(0)
        for l in range(DEPTH):
            @pl.when(layer == l)
            def _(l=l):
                cols = []
                for piece in recv_refs[l * blocks:(l + 1) * blocks]:
                    part = piece[0].astype(F32)
                    for s in range(1, N_DEV):
                        part = part + piece[s].astype(F32)
                    cols.append(part)
                g = cols[0] if blocks == 1 else jnp.concatenate(cols, axis=-1)
                delta, nm, nv = _adamw_math(w_ref[0], g, m_ref[0], v_ref[0])
                g_ref[0] = g
                d_ref[0] = delta
                nm_ref[0] = nm
                nv_ref[0] = nv

    spec = pl.BlockSpec((1, tr, c), lambda l, i: (l, i, 0))
    outs, _ = _call(
        body, name=name, grid=(DEPTH, r // tr),
        in_specs=[pl.BlockSpec((N_DEV, tr, c // blocks), lambda l, i: (0, i, 0))] * len(flat) + [spec] * 3,
        out_specs=[spec] * 4, out_shape=[jax.ShapeDtypeStruct((DEPTH, r, c), F32)] * 4,
        args=tuple(flat) + (w, m, v))
    return outs


def _adamw_small(sheet, extra, params, *, name):
    sheet_rows = dict(norm1_g=ROW_NORM1, b_gate=ROW_BGATE, gmlp_ln_g=ROW_LN_G, gmlp_ln_b=ROW_LN_B, norm2_g=ROW_NORM2,
                      b_ffn_conv=ROW_BFCONV)
    names = list(params)
    extra_names = list(extra)

    def body(*refs):
        sg_ref, refs = refs[0], refs[1:]
        extra_refs, refs = dict(zip(extra_names, refs[:len(extra_names)])), refs[len(extra_names):]
        ins, outs = refs[:3 * len(names)], refs[3 * len(names):]
        for j, key in enumerate(names):
            w_ref, m_ref, v_ref = ins[3 * j:3 * j + 3]
            g_ref, d_ref, nm_ref, nv_ref = outs[4 * j:4 * j + 4]
            if key in extra_refs:
                g_ref[...] = extra_refs[key][...]
            elif key == "final_g":
                g_ref[...] = sg_ref[ROW_FINAL:ROW_FINAL + 1, 0:D_MODEL]
            else:
                width = w_ref.shape[-1]
                for l in range(DEPTH):
                    row = SG_LAYER * l + sheet_rows[key]
                    g_ref[l:l + 1, :] = sg_ref[row:row + 1, 0:width]
            delta, nm, nv = _adamw_math(w_ref[...], g_ref[...], m_ref[...], v_ref[...])
            d_ref[...] = delta
            nm_ref[...] = nm
            nv_ref[...] = nv

    args = [sheet] + [extra[k] for k in extra_names] + [t for k in names for t in params[k]]
    vmem = pl.BlockSpec(memory_space=pltpu.VMEM)
    outs = pl.pallas_call(
        body, name=name, in_specs=[vmem] * len(args), out_specs=[vmem] * (4 * len(names)),
        out_shape=[jax.ShapeDtypeStruct(params[k][0].shape, F32) for k in names for _ in range(4)],
    )(*args)
    return {k: tuple(outs[4 * j:4 * j + 4]) for j, k in enumerate(names)}


def _rows(gathered):
    return gathered.reshape(N_DEV * gathered.shape[1], gathered.shape[2])


def _parts(full):
    return full.reshape(N_DEV, full.shape[0] // N_DEV, full.shape[1])


def kernel(x, norm1_g, w_in, b_gate, gmlp_ln_g, gmlp_ln_b, w_spatial, b_spatial, w_shortconv, w_branch, w_out, norm2_g, w_ffn_up, w_ffn_conv, b_ffn_conv, w_ffn_down, final_g, loss_target, m_norm1_g, m_w_in, m_b_gate, m_gmlp_ln_g, m_gmlp_ln_b, m_w_spatial, m_b_spatial, m_w_shortconv, m_w_branch, m_w_out, m_norm2_g, m_w_ffn_up, m_w_ffn_conv, m_b_ffn_conv, m_w_ffn_down, m_final_g, v_norm1_g, v_w_in, v_b_gate, v_gmlp_ln_g, v_gmlp_ln_b, v_w_spatial, v_b_spatial, v_w_shortconv, v_w_branch, v_w_out, v_norm2_g, v_w_ffn_up, v_w_ffn_conv, v_b_ffn_conv, v_w_ffn_down, v_final_g):
    n = x.shape[1]
    tm_in, tm, tn = 1024, 512, 2048
    x0 = x.reshape(n, D_MODEL)
    target = loss_target.reshape(n, D_MODEL)
    my_idx = 4 * lax.axis_index("x") + 2 * lax.axis_index("y") + lax.axis_index("c")
    sc_w, fc_w = D_B // N_DEV, D_FF // N_DEV

    sh_in = [w_in[l].T.astype(BF16) for l in range(DEPTH)]
    sh_up = [w_ffn_up[l].T.astype(BF16) for l in range(DEPTH)]
    sh_br = [w_branch[l].astype(BF16) for l in range(DEPTH)]
    sh_out = [w_out[l].astype(BF16) for l in range(DEPTH)]
    sh_down = [w_ffn_down[l].astype(BF16) for l in range(DEPTH)]
    taps = jnp.concatenate([w_shortconv, w_ffn_conv], axis=-1)

    def branch_weights(g):
        return g.transpose(1, 2, 0, 3).reshape(2, D_A, D_MODEL)

    g_in0, g_taps = _gather_now([sh_in[0], taps], name="gather_first")
    w_sc = [g_taps[:, l, :, :sc_w].transpose(1, 0, 2).reshape(3, D_B) for l in range(DEPTH)]
    w_fc = [g_taps[:, l, :, sc_w:].transpose(1, 0, 2).reshape(3, D_FF) for l in range(DEPTH)]
    w_s_t = [w_spatial[l].transpose(0, 2, 1) for l in range(DEPTH)]
    weights = [dict(), dict()]
    weights[0]["in_t"] = _rows(g_in0)
    saved = []
    xc = x0
    for l in range(DEPTH):
        p = weights[l]
        carry = _Gather([sh_br[0], sh_out[0]] if l == 0 else [sh_up[1]])
        (h, z), got = _norm_matmul(xc, norm1_g[l], p["in_t"], tm=tm_in, name=f"fwd_in_{l}", carry=carry)
        if l == 0:
            p["wb"], p["out"] = branch_weights(got[0]), _rows(got[1])
        else:
            p["up_t"] = _rows(got[0])
        carry = _Gather([sh_up[0]]) if l == 0 else None
        (ya, yb, conv, pa, pb, mg, x1), got = _mix_forward(
            z, xc, b_gate[l], gmlp_ln_g[l], gmlp_ln_b[l], w_spatial[l], b_spatial[l], w_sc[l], p["wb"], p["out"],
            tm=tm, name=f"fwd_mix_{l}", carry=carry)
        if l == 0:
            p["up_t"] = _rows(got[0])
        (h2, up), got = _norm_matmul(x1, norm2_g[l], p["up_t"], tm=tm, name=f"fwd_up_{l}", carry=_Gather([sh_down[l]]))
        p["down"] = _rows(got[0])
        carry = _Gather([sh_br[1], sh_out[1], sh_in[1]]) if l == 0 else None
        head = (final_g, target) if l == DEPTH - 1 else None
        outs, got = _ffn_forward(up, x1, w_fc[l], b_ffn_conv[l], p["down"], tm=tm, name=f"fwd_ffn_{l}", carry=carry, head=head)
        if l == 0:
            weights[1]["wb"], weights[1]["out"], weights[1]["in_t"] = branch_weights(got[0]), _rows(got[1]), _rows(got[2])
        gc, a = outs[0], outs[1]
        saved.append(dict(x=xc, h=h, z=z, ya=ya, yb=yb, conv=conv, pa=pa, pb=pb, mg=mg, x1=x1, h2=h2, up=up, gc=gc, a=a))
        xc = outs[2]
    dx, sheet = outs[2], outs[3]

    recv = [dict(), dict()]
    small_dws, small_dbs = [None] * DEPTH, [None] * DEPTH
    pending_in = None
    for l in reversed(range(DEPTH)):
        p, s = weights[l], saved[l]
        carry = _Exchange([pending_in]) if pending_in is not None else None
        (dup, sheet), got = _ffn_backward(dx, s["up"], s["gc"], w_fc[l], p["down"], sheet, l, tm=tm, name=f"bwd_ffn_{l}",
                                          carry=carry)
        if got is not None:
            recv[l + 1]["in_t"] = got[0]
        dw_down, _ = _matmul_tn(s["a"], dx, t1=D_FF // 2, tn=tn, name=f"dw_down_{l}")
        dw_up_t, got = _matmul_tn(dup, s["h2"], t1=2 * D_FF // 4, tn=tn, name=f"dw_up_{l}", pieces=2,
                                  carry=_Exchange([_parts(dw_down)]))
        recv[l]["down"] = got[0]
        (dx1, sheet), got_left = _matmul_norm_backward(
            dup, p["up_t"], s["x1"], norm2_g[l], dx, sheet, SG_LAYER * l + ROW_NORM2, tm=tm, name=f"bwd_up_{l}",
            carry=_Exchange([_parts(dw_up_t[0])]))
        dw_out, _ = _matmul_tn(s["mg"], dx1, t1=D_MODEL, tn=tn, name=f"dw_out_{l}")
        (dz, dpa, dpb, small_dws[l], small_dbs[l], sheet), got_right = _mix_backward(
            dx1, s["z"], s["conv"], s["pa"], s["pb"], b_gate[l], gmlp_ln_g[l], gmlp_ln_b[l], w_spatial[l], w_s_t[l],
            b_spatial[l], w_sc[l], p["out"], p["wb"], sheet, l, tm=tm, name=f"bwd_mix_{l}",
            carry=_Exchange([_parts(dw_up_t[1]), _parts(dw_out)]))
        recv[l]["up_t"], recv[l]["out"] = [got_left[0], got_right[0]], got_right[1]
        dw_bra_t, _ = _matmul_tn(dpa, s["ya"], t1=D_MODEL, tn=tn, name=f"dw_branch_a_{l}")
        dw_brb_t, _ = _matmul_tn(dpb, s["yb"], t1=D_MODEL, tn=tn, name=f"dw_branch_b_{l}")
        carry = _Exchange([_parts(dw_bra_t), _parts(dw_brb_t)])
        if l == 0:
            dbs = jnp.stack([t.reshape(N_HEADS, GMLP_BLOCK) for t in small_dbs]).reshape(DEPTH * N_HEADS, GMLP_BLOCK)
            carry = _Both(carry, _Gather([sheet, small_dws[0], small_dws[1], dbs]))
        dw_in_t, got = _matmul_tn(dz, s["h"], t1=D_IN // 4, tn=tn, name=f"dw_in_{l}", carry=carry)
        recv[l]["bra_t"], recv[l]["brb_t"] = got[:2]
        if l == 0:
            gathered_small = got[2:]
            (dx0, dg1_first), got = _matmul_norm_backward(dz, p["in_t"], s["x"], norm1_g[l], dx1, None, 0, tm=tm,
                                                         name=f"bwd_in_{l}", carry=_Exchange([_parts(dw_in_t)]))
            recv[0]["in_t"] = got[0]
        else:
            (dx0, sheet), _ = _matmul_norm_backward(dz, p["in_t"], s["x"], norm1_g[l], dx1, sheet,
                                                    SG_LAYER * l + ROW_NORM1, tm=tm, name=f"bwd_in_{l}")
            pending_in = _parts(dw_in_t)
        dx = dx0
    grad_x = dx.reshape(x.shape)

    results = {}
    both = lambda key: [recv[l][key] for l in range(DEPTH)]
    blocks = lambda key: [r if isinstance(r, list) else [r] for r in both(key)]
    swap = lambda t: t.transpose(0, 2, 1)
    for key, slab, (w, m, v), tr in [("w_in", "in_t", (w_in, m_w_in, v_w_in), 192),
                                     ("w_ffn_up", "up_t", (w_ffn_up, m_w_ffn_up, v_w_ffn_up), 176)]:
        outs = _sum_adamw(blocks(slab), swap(w), swap(m), swap(v), tr=tr, name=f"adamw_{key}")
        results[key] = tuple(swap(o) for o in outs)
    g_bra = _sum_parts(both("bra_t"), tr=128, name="sum_w_branch_a").transpose(0, 2, 1)
    g_brb = _sum_parts(both("brb_t"), tr=128, name="sum_w_branch_b").transpose(0, 2, 1)
    g_br = jnp.stack([g_bra, g_brb], axis=1)
    flat = lambda t: t.reshape(-1, t.shape[-1])
    outs = _adamw(flat(w_branch), flat(g_br), flat(m_w_branch), flat(v_w_branch), tr=512, name="adamw_w_branch")
    results["w_branch"] = (g_br,) + tuple(o.reshape(w_branch.shape) for o in outs)
    results["w_out"] = tuple(_sum_adamw(blocks("out"), w_out, m_w_out, v_w_out, tr=128, name="adamw_w_out"))
    results["w_ffn_down"] = tuple(_sum_adamw(blocks("down"), w_ffn_down, m_w_ffn_down, v_w_ffn_down, tr=176,
                                             name="adamw_w_ffn_down"))

    sheet, dws0, dws1, dbs = _sum_gathered(gathered_small, name="sum_small_grads")
    (dg1_first,) = _all_reduce_small([dg1_first], name="all_reduce_last_gain")
    sheet = sheet.at[ROW_NORM1, :D_MODEL].set(dg1_first[0])
    loss = sheet[ROW_LOSS, 0]
    taps = lambda row, width: jnp.stack([sheet[SG_LAYER * l + row:SG_LAYER * l + row + 3, :width] for l in range(DEPTH)])
    extra = dict(w_spatial=jnp.stack([dws0, dws1]), b_spatial=dbs.reshape(DEPTH, N_HEADS, GMLP_BLOCK),
                 w_shortconv=lax.dynamic_slice_in_dim(taps(ROW_SCONV, D_B), my_idx * sc_w, sc_w, axis=2),
                 w_ffn_conv=lax.dynamic_slice_in_dim(taps(ROW_FCONV, D_FF), my_idx * fc_w, fc_w, axis=2))
    small_w = dict(norm1_g=(norm1_g, m_norm1_g, v_norm1_g), b_gate=(b_gate, m_b_gate, v_b_gate),
                   gmlp_ln_g=(gmlp_ln_g, m_gmlp_ln_g, v_gmlp_ln_g), gmlp_ln_b=(gmlp_ln_b, m_gmlp_ln_b, v_gmlp_ln_b),
                   w_spatial=(w_spatial, m_w_spatial, v_w_spatial), b_spatial=(b_spatial, m_b_spatial, v_b_spatial),
                   w_shortconv=(w_shortconv, m_w_shortconv, v_w_shortconv), norm2_g=(norm2_g, m_norm2_g, v_norm2_g),
                   w_ffn_conv=(w_ffn_conv, m_w_ffn_conv, v_w_ffn_conv), b_ffn_conv=(b_ffn_conv, m_b_ffn_conv, v_b_ffn_conv),
                   final_g=tuple(t.reshape(1, D_MODEL) for t in (final_g, m_final_g, v_final_g)))
    results.update(_adamw_small(sheet, extra, small_w, name="adamw_small"))
    results["final_g"] = tuple(t.reshape(D_MODEL) for t in results["final_g"])

    names = ["norm1_g", "w_in", "b_gate", "gmlp_ln_g", "gmlp_ln_b", "w_spatial", "b_spatial", "w_shortconv", "w_branch",
             "w_out", "norm2_g", "w_ffn_up", "w_ffn_conv", "b_ffn_conv", "w_ffn_down", "final_g"]
    return (loss, grad_x, *[results[k][0] for k in names], *[results[k][1] for k in names],
            *[results[k][2] for k in names], *[results[k][3] for k in names])
```

```python
import math

import jax
import jax.numpy as jnp
from jax import lax
from jax.experimental import pallas as pl
from jax.experimental.pallas import tpu as pltpu

F32 = jnp.float32
BF16 = jnp.bfloat16

N_DEV = 8
DEPTH = 2
D_MODEL = 1024
D_A = 512
D_B = 512
D_FF = 2816
D_IN = 4608
N_HEADS = 4
HEAD = 128
GMLP_BLOCK = 128
CAUSAL_CHUNK = 64
OFF_U, OFF_V, OFF_BG, OFF_CG, OFF_HB, OFF_GA, OFF_GB = 0, 512, 1024, 1536, 2048, 2560, 3584
RMS_EPS = 1e-6
LN_EPS = 1e-5
ADAM_LR, ADAM_B1, ADAM_B2, ADAM_EPS, ADAM_WD, ADAM_STEP = 0.001, 0.9, 0.999, 1e-08, 0.01, 10

SUBLANES, LANES = 8, 128
MATMUL_CHUNK = 512
HALO = 16
FFN_CHUNK = 256
SG_ROWS, SG_W, SG_LAYER = 40, D_FF, 16
ROW_NORM1, ROW_BGATE, ROW_LN_G, ROW_LN_B, ROW_SCONV, ROW_NORM2, ROW_FCONV, ROW_BFCONV = 0, 1, 2, 3, 4, 7, 8, 11
ROW_FINAL, ROW_LOSS = 32, 33
V7X_VMEM_BYTES = 64 << 20
VMEM_LIMIT = V7X_VMEM_BYTES - (8 << 20)
MESH = pl.DeviceIdType.MESH
GELU_C0 = 0.7978845608028654
GELU_C1 = 0.044715
NT = (((1,), (1,)), ((), ()))
TN = (((0,), (0,)), ((), ()))


def _dot(a, b):
    return jnp.dot(a, b, preferred_element_type=F32)


def _dot_nt(a, b):
    return lax.dot_general(a, b, NT, preferred_element_type=F32)


def _sigmoid(x):
    return 1.0 / (1.0 + jnp.exp(-x))


def _gelu_tanh(x):
    return jnp.tanh(GELU_C0 * (x + GELU_C1 * x * x * x))


def _gelu_grad(x, t):
    return 0.5 * (1.0 + t) + 0.5 * x * (1.0 - t * t) * GELU_C0 * (1.0 + 3.0 * GELU_C1 * x * x)


def _sublane_tile(dtype):
    return SUBLANES * (4 // jnp.dtype(dtype).itemsize)


def _shift_down(a, k, prev):
    p = prev.shape[0]
    r = pltpu.roll(a, k, 0)
    sub = _sublane_tile(a.dtype)
    head = r[0:sub]
    rid = lax.broadcasted_iota(jnp.int32, head.shape, 0)
    for j in range(k):
        head = jnp.where(rid == j, prev[p - k + j:p - k + j + 1, :], head)
    return jnp.concatenate([head, r[sub:]], axis=0)


def _shift_up(a, k, nxt):
    t = a.shape[0]
    r = pltpu.roll(a, t - k, 0)
    sub = _sublane_tile(a.dtype)
    tail = r[t - sub:t]
    rid = lax.broadcasted_iota(jnp.int32, tail.shape, 0)
    for j in range(k):
        tail = jnp.where(rid == sub - k + j, nxt[j:j + 1, :], tail)
    return jnp.concatenate([r[0:t - sub], tail], axis=0)


def _column_sums(p):
    if p.dtype.itemsize < 4:
        t = p.shape[0]
        p = p[:t // 2] + p[t // 2:]
        p = p[:t // 4] + p[t // 4:]
    return jnp.sum(p.astype(F32), axis=0, keepdims=True)


def _sheet_begin(step, sheet_in, sheet_out, first_row, rows):
    @pl.when(step == 0)
    def _():
        sheet_out[...] = sheet_in[...]
        sheet_out[first_row:first_row + rows, :] = jnp.zeros((rows, SG_W), F32)


def _sheet_spec():
    return pl.BlockSpec((SG_ROWS, SG_W), lambda i: (0, 0))


def _spatial_mask(transposed):
    ri = lax.broadcasted_iota(jnp.int32, (GMLP_BLOCK, GMLP_BLOCK), 0) // CAUSAL_CHUNK
    ci = lax.broadcasted_iota(jnp.int32, (GMLP_BLOCK, GMLP_BLOCK), 1) // CAUSAL_CHUNK
    return (ri <= ci) if transposed else (ci <= ri)


def _gmlp_forward(u, v, ln_g, ln_b, ws_ref, bs_ref, f_scr):
    tm = u.shape[0]
    tu = _gelu_tanh(u)
    tv = _gelu_tanh(v)
    gu = 0.5 * u * (1.0 + tu)
    gv = 0.5 * v * (1.0 + tv)
    mu = jnp.mean(gv, axis=-1, keepdims=True)
    cen = gv - mu
    rstd = lax.rsqrt(jnp.mean(cen * cen, axis=-1, keepdims=True) + LN_EPS)
    xh = cen * rstd
    vn = (xh * ln_g + ln_b).astype(BF16)
    mask = _spatial_mask(False)
    wm = [jnp.where(mask, ws_ref[h], 0.0).astype(BF16) for h in range(N_HEADS)]
    for b in range(tm // GMLP_BLOCK):
        rows = slice(b * GMLP_BLOCK, (b + 1) * GMLP_BLOCK)
        for h in range(N_HEADS):
            cols = slice(h * HEAD, (h + 1) * HEAD)
            f_scr[rows, cols] = (_dot(wm[h], vn[rows, cols]) + bs_ref[h]).astype(f_scr.dtype)
    return gu, tu, tv, xh, rstd, vn, f_scr[...]


def _position():
    return lax.axis_index("x"), lax.axis_index("y"), lax.axis_index("c")


def _handshake(peers):
    barrier = pltpu.get_barrier_semaphore()
    for peer in peers:
        pl.semaphore_signal(barrier, inc=1, device_id=peer, device_id_type=MESH)
    pl.semaphore_wait(barrier, len(peers))


class _Gather:
    collective_id = 1

    def __init__(self, arrays):
        self.arrays = list(arrays)
        self.out_shape = [jax.ShapeDtypeStruct((N_DEV,) + a.shape, a.dtype) for a in self.arrays]
        self.base = 0

    def barrier(self):
        x, y, c = _position()
        _handshake([(x, y, 1 - c), (1 - x, y, c), (x, 1 - y, c), (1 - x, 1 - y, c)])

    def _plan(self, ins, outs, sems):
        send_sems, recv_sems, local_sems = sems
        x, y, c = _position()
        me, sibling = (x, y, c), (x, y, 1 - c)
        chips = [(1 - x, y), (x, 1 - y), (1 - x, 1 - y)]

        def slot(a, p):
            return outs[a].at[4 * p[0] + 2 * p[1] + p[2]]

        def copy(a, k, block, to, src=None):
            return pltpu.make_async_remote_copy(
                src_ref=slot(a, block) if src is None else src, dst_ref=slot(a, block),
                send_sem=send_sems.at[self.base + a, k], recv_sem=recv_sems.at[self.base + a, k],
                device_id=to, device_id_type=MESH)

        n = len(self.arrays)

        def mine():
            return [pltpu.make_async_copy(ins[a], slot(a, me), local_sems.at[self.base + a]) for a in range(n)]

        def first():
            out = []
            for a in range(n):
                out.append(copy(a, 0, me, sibling, src=ins[a]))
                out += [copy(a, 1 + j, me, (*chip, c), src=ins[a]) for j, chip in enumerate(chips)]
            return out

        def arrivals():
            return [copy(a, 1 + j, (*chip, c), me) for j, chip in enumerate(chips) for a in range(n)]

        def relays():
            return [copy(a, 4 + j, (*chip, c), sibling) for j, chip in enumerate(chips) for a in range(n)]

        def from_sibling():
            out = [copy(a, 0, sibling, me) for a in range(n)]
            return out + [copy(a, 4 + j, (*chip, 1 - c), me) for j, chip in enumerate(chips) for a in range(n)]

        return mine, first, arrivals, relays, from_sibling

    def start(self, ins, outs, sems):
        mine, first, _, _, _ = self._plan(ins, outs, sems)
        for cp in mine() + first():
            cp.start()

    def relay(self, ins, outs, sems):
        _, _, arrivals, relays, _ = self._plan(ins, outs, sems)
        for arrived, onward in zip(arrivals(), relays()):
            arrived.wait_recv()
            onward.start()

    def finish(self, ins, outs, sems):
        mine, first, _, relays, from_sibling = self._plan(ins, outs, sems)
        for cp in from_sibling():
            cp.wait_recv()
        for cp in first() + relays():
            cp.wait_send()
        for cp in mine():
            cp.wait()


class _Exchange:
    collective_id = 0

    def __init__(self, arrays):
        self.arrays = list(arrays)
        self.out_shape = [jax.ShapeDtypeStruct(a.shape, a.dtype) for a in self.arrays]
        self.base = 0

    def barrier(self):
        x, y, c = _position()
        _handshake([(x ^ dx, y ^ dy, c ^ dc) for dx in (0, 1) for dy in (0, 1) for dc in (0, 1) if dx or dy or dc])

    def _plan(self, ins, outs, sems):
        send_sems, recv_sems, local_sems = sems
        x, y, c = _position()
        my_idx = 4 * x + 2 * y + c
        n = len(self.arrays)
        offsets = [(dx, dy, dc) for dx in (0, 1) for dy in (0, 1) for dc in (0, 1) if (dx, dy, dc) != (0, 0, 0)]

        def mine():
            return [pltpu.make_async_copy(ins[a].at[my_idx], outs[a].at[my_idx], local_sems.at[self.base + a])
                    for a in range(n)]

        def remote(arriving):
            out = []
            for k, (dx, dy, dc) in enumerate(offsets):
                px, py, pc = x ^ dx, y ^ dy, c ^ dc
                p_idx = 4 * px + 2 * py + pc
                for a in range(n):
                    out.append(pltpu.make_async_remote_copy(
                        src_ref=ins[a].at[p_idx], dst_ref=outs[a].at[p_idx if arriving else my_idx],
                        send_sem=send_sems.at[self.base + a, k], recv_sem=recv_sems.at[self.base + a, k],
                        device_id=(px, py, pc), device_id_type=MESH))
            return out

        return mine, remote

    def start(self, ins, outs, sems):
        mine, remote = self._plan(ins, outs, sems)
        for cp in mine() + remote(False):
            cp.start()

    def relay(self, ins, outs, sems):
        pass

    def finish(self, ins, outs, sems):
        mine, remote = self._plan(ins, outs, sems)
        for cp in remote(True):
            cp.wait_recv()
        for cp in remote(False):
            cp.wait_send()
        for cp in mine():
            cp.wait()


class _Both:
    def __init__(self, *carries):
        self.carries = carries
        self.arrays = [a for c in carries for a in c.arrays]
        self.out_shape = [s for c in carries for s in c.out_shape]
        first = 0
        for c in carries:
            c.base = first
            first += len(c.arrays)
        self.collective_id = min(c.collective_id for c in carries)

    def barrier(self):
        min(self.carries, key=lambda c: c.collective_id).barrier()

    def _each(self, method, ins, outs, sems):
        for c in self.carries:
            rows = slice(c.base, c.base + len(c.arrays))
            getattr(c, method)(ins[rows], outs[rows], sems)

    def start(self, ins, outs, sems):
        self._each("start", ins, outs, sems)

    def relay(self, ins, outs, sems):
        self._each("relay", ins, outs, sems)

    def finish(self, ins, outs, sems):
        self._each("finish", ins, outs, sems)


def _call(body, *, name, grid, in_specs, out_specs, out_shape, args, scratch_shapes=(), carry=None):
    n_in, n_out, n_scr = len(in_specs), len(out_specs), len(scratch_shapes)
    params = pltpu.CompilerParams(dimension_semantics=("arbitrary",) * len(grid), vmem_limit_bytes=VMEM_LIMIT)
    if carry is None:
        outs = pl.pallas_call(body, name=name, grid=grid, in_specs=in_specs, out_specs=out_specs, out_shape=out_shape,
                              scratch_shapes=list(scratch_shapes), compiler_params=params)(*args)
        return outs, None
    m = len(carry.arrays)
    total = math.prod(grid)

    def wrapped(*refs):
        ins, refs = refs[:n_in], refs[n_in:]
        c_ins, refs = refs[:m], refs[m:]
        outs, refs = refs[:n_out], refs[n_out:]
        c_outs, refs = refs[:m], refs[m:]
        scr, sems = refs[:n_scr], refs[n_scr:]
        flat = pl.program_id(0)
        for d in range(1, len(grid)):
            flat = flat * grid[d] + pl.program_id(d)

        @pl.when(flat == 0)
        def _():
            carry.barrier()
            carry.start(c_ins, c_outs, sems)

        body(*ins, *outs, *scr)

        @pl.when(flat == total - 2)
        def _():
            carry.relay(c_ins, c_outs, sems)

        @pl.when(flat == total - 1)
        def _():
            carry.finish(c_ins, c_outs, sems)

    any_spec = pl.BlockSpec(memory_space=pl.ANY)
    sem_shapes = [pltpu.SemaphoreType.DMA((m, 7)), pltpu.SemaphoreType.DMA((m, 7)), pltpu.SemaphoreType.DMA((m,))]
    params = pltpu.CompilerParams(dimension_semantics=("arbitrary",) * len(grid), vmem_limit_bytes=VMEM_LIMIT,
                                  collective_id=carry.collective_id)
    outs = pl.pallas_call(
        wrapped, name=name, grid=grid,
        in_specs=list(in_specs) + [any_spec] * m, out_specs=list(out_specs) + [any_spec] * m,
        out_shape=list(out_shape) + carry.out_shape,
        scratch_shapes=list(scratch_shapes) + sem_shapes, compiler_params=params)(*args, *carry.arrays)
    return outs[:n_out], outs[n_out:]


def _gather_now(arrays, *, name):
    carry = _Gather(arrays)
    m = len(arrays)

    def body(*refs):
        ins, outs, sems = refs[:m], refs[m:2 * m], refs[2 * m:]
        carry.start(ins, outs, sems)
        carry.relay(ins, outs, sems)
        carry.finish(ins, outs, sems)

    any_spec = pl.BlockSpec(memory_space=pl.ANY)
    return pl.pallas_call(
        body, name=name, in_specs=[any_spec] * m, out_specs=[any_spec] * m, out_shape=carry.out_shape,
        scratch_shapes=[pltpu.SemaphoreType.DMA((m, 7)), pltpu.SemaphoreType.DMA((m, 7)),
                        pltpu.SemaphoreType.DMA((m,))],
    )(*arrays)


def _all_reduce_small(arrs, *, name):
    n = len(arrs)

    def body(*refs):
        ins, outs, bufs = refs[:n], refs[n:2 * n], refs[2 * n:3 * n]
        send_sems, recv_sems = refs[3 * n:]
        x, y, c = _position()
        me, sibling = (x, y, c), (x, y, 1 - c)
        chips = [(1 - x, y), (x, 1 - y), (1 - x, 1 - y)]

        def copy(a, k, block, to, src=None):
            slot = bufs[a].at[4 * block[0] + 2 * block[1] + block[2]]
            return pltpu.make_async_remote_copy(
                src_ref=slot if src is None else src, dst_ref=slot,
                send_sem=send_sems.at[a, k], recv_sem=recv_sems.at[a, k], device_id=to, device_id_type=MESH)

        first = []
        for a in range(n):
            first.append(copy(a, 0, me, sibling, src=ins[a]))
            first += [copy(a, 1 + j, me, (*chip, c), src=ins[a]) for j, chip in enumerate(chips)]
        for cp in first:
            cp.start()
        passed = []
        for j, chip in enumerate(chips):
            for a in range(n):
                copy(a, 1 + j, (*chip, c), me).wait_recv()
                cp = copy(a, 4 + j, (*chip, c), sibling)
                cp.start()
                passed.append(cp)
        for a in range(n):
            copy(a, 0, sibling, me).wait_recv()
            for j, chip in enumerate(chips):
                copy(a, 4 + j, (*chip, 1 - c), me).wait_recv()
        for cp in first + passed:
            cp.wait_send()
        my_idx = 4 * x + 2 * y + c
        for a in range(n):
            acc = jnp.zeros(ins[a].shape, F32)
            for s in range(N_DEV):
                acc = acc + jnp.where(my_idx == s, ins[a][...], bufs[a][s])
            outs[a][...] = acc

    vmem = pl.BlockSpec(memory_space=pltpu.VMEM)
    return pl.pallas_call(
        body, name=name, in_specs=[vmem] * n, out_specs=[vmem] * n,
        out_shape=[jax.ShapeDtypeStruct(a.shape, F32) for a in arrs],
        scratch_shapes=[pltpu.VMEM((N_DEV,) + a.shape, F32) for a in arrs]
        + [pltpu.SemaphoreType.DMA((n, 7)), pltpu.SemaphoreType.DMA((n, 7))],
        compiler_params=pltpu.CompilerParams(vmem_limit_bytes=VMEM_LIMIT),
    )(*arrs)


def _sum_gathered(arrs, *, name):
    n = len(arrs)

    def body(*refs):
        for in_ref, out_ref in zip(refs[:n], refs[n:]):
            acc = in_ref[0]
            for s in range(1, N_DEV):
                acc = acc + in_ref[s]
            out_ref[...] = acc

    vmem = pl.BlockSpec(memory_space=pltpu.VMEM)
    return pl.pallas_call(
        body, name=name, in_specs=[vmem] * n, out_specs=[vmem] * n,
        out_shape=[jax.ShapeDtypeStruct(a.shape[1:], F32) for a in arrs],
        compiler_params=pltpu.CompilerParams(vmem_limit_bytes=VMEM_LIMIT),
    )(*arrs)


def _norm_matmul(x, g, w_t, *, tm, name, carry=None):
    n, d = x.shape
    c = w_t.shape[0]
    ch = MATMUL_CHUNK

    def body(x_ref, g_ref, wt_ref, h_ref, z_ref):
        xv = x_ref[...]
        r = lax.rsqrt(jnp.mean(xv * xv, axis=-1, keepdims=True) + RMS_EPS)
        h = (xv * r * g_ref[...]).astype(BF16)
        h_ref[...] = h
        for c0 in range(0, c, ch):
            z_ref[:, c0:c0 + ch] = _dot_nt(h, wt_ref[c0:c0 + ch, :]).astype(BF16)

    return _call(
        body, name=name, grid=(n // tm,), carry=carry,
        in_specs=[pl.BlockSpec((tm, d), lambda i: (i, 0)),
                  pl.BlockSpec((1, d), lambda i: (0, 0)),
                  pl.BlockSpec((c, d), lambda i: (0, 0))],
        out_specs=[pl.BlockSpec((tm, d), lambda i: (i, 0)),
                   pl.BlockSpec((tm, c), lambda i: (i, 0))],
        out_shape=[jax.ShapeDtypeStruct((n, d), BF16), jax.ShapeDtypeStruct((n, c), BF16)],
        args=(x, g.reshape(1, d), w_t))


def _mix_forward(z, x, b_gate, ln_g, ln_b, w_s, b_s, w_sc, wb, w_out, *, tm, name, carry=None):
    n = z.shape[0]
    hb = tm // HALO

    def body(z_ref, zp_ref, x_ref, bg_ref, lng_ref, lnb_ref, ws_ref, bs_ref, wsc_ref, wb_ref, wo_ref,
             ya_ref, yb_ref, cv_ref, pa_ref, pb_ref, mg_ref, x1_ref, f_scr):
        i = pl.program_id(0)
        u = z_ref[:, OFF_U:OFF_U + D_A]
        v = z_ref[:, OFF_V:OFF_V + D_A].astype(F32)
        gu, _, _, _, _, _, f = _gmlp_forward(u, v, lng_ref[...], lnb_ref[...], ws_ref, bs_ref, f_scr)
        ya = gu * f
        ya_ref[...] = ya

        q = z_ref[:, OFF_CG:OFF_CG + D_B] * z_ref[:, OFF_HB:OFF_HB + D_B]
        qp = zp_ref[:, OFF_CG:OFF_CG + D_B] * zp_ref[:, OFF_HB:OFF_HB + D_B]
        qp = jnp.where(i > 0, qp, jnp.zeros_like(qp))
        w = wsc_ref[...].astype(BF16)
        conv = w[0:1] * _shift_down(q, 2, qp) + w[1:2] * _shift_down(q, 1, qp) + w[2:3] * q
        cv_ref[...] = conv
        yb = z_ref[:, OFF_BG:OFF_BG + D_B] * conv
        yb_ref[...] = yb

        pa = _dot(ya, wb_ref[0]).astype(BF16)
        pb = _dot(yb, wb_ref[1]).astype(BF16)
        pa_ref[...] = pa
        pb_ref[...] = pb
        bg = bg_ref[...].astype(BF16)
        sa = _sigmoid(z_ref[:, OFF_GA:OFF_GA + D_MODEL] + bg[:, 0:D_MODEL])
        sb = _sigmoid(z_ref[:, OFF_GB:OFF_GB + D_MODEL] + bg[:, D_MODEL:2 * D_MODEL])
        mg = sa * pa + sb * pb
        mg_ref[...] = mg
        x1_ref[...] = x_ref[...] + _dot(mg, wo_ref[...])

    row = lambda w: pl.BlockSpec((tm, w), lambda i: (i, 0))
    full = lambda *s: pl.BlockSpec(s, lambda i: (0,) * len(s))
    bf = lambda w: jax.ShapeDtypeStruct((n, w), BF16)
    return _call(
        body, name=name, grid=(n // tm,), carry=carry,
        in_specs=[row(D_IN),
                  pl.BlockSpec((HALO, D_IN), lambda i: (jnp.maximum(i * hb - 1, 0), 0)),
                  row(D_MODEL), full(1, 2 * D_MODEL), full(1, D_A), full(1, D_A),
                  full(N_HEADS, GMLP_BLOCK, GMLP_BLOCK), full(N_HEADS, GMLP_BLOCK, 1), full(3, D_B),
                  full(2, D_A, D_MODEL), full(D_MODEL, D_MODEL)],
        out_specs=[row(D_A), row(D_B), row(D_B), row(D_MODEL), row(D_MODEL), row(D_MODEL), row(D_MODEL)],
        out_shape=[bf(D_A), bf(D_B), bf(D_B), bf(D_MODEL), bf(D_MODEL), bf(D_MODEL),
                   jax.ShapeDtypeStruct((n, D_MODEL), F32)],
        scratch_shapes=[pltpu.VMEM((tm, D_A), BF16)],
        args=(z, z, x, b_gate.reshape(1, -1), ln_g.reshape(1, -1), ln_b.reshape(1, -1), w_s,
              b_s.reshape(N_HEADS, GMLP_BLOCK, 1), w_sc, wb, w_out))


def _loss_tile(xv, gv, tv):
    d = xv.shape[-1]
    r = lax.rsqrt(jnp.mean(xv * xv, axis=-1, keepdims=True) + RMS_EPS)
    xh = xv * r
    e = xh * gv - tv
    per_row = jnp.sum(e * e, axis=-1, keepdims=True) * (0.5 / d)
    dy = e * (1.0 / d)
    dxh = dy * gv
    dx = r * (dxh - xh * jnp.mean(dxh * xh, axis=-1, keepdims=True))
    return dx, jnp.sum(per_row, axis=0, keepdims=True), jnp.sum(dy * xh, axis=0, keepdims=True)


def _ffn_forward(up, x1, w_fc, b_fc, w_down, *, tm, name, carry=None, head=None):
    n = up.shape[0]
    hb = tm // HALO
    n_in = 6 if head is None else 8

    def body(*refs):
        up_ref, upp_ref, x1_ref, wfc_ref, bfc_ref, wd_ref = refs[:6]
        gc_ref, a_ref, out_ref = refs[n_in:n_in + 3]
        acc = refs[-1]
        i = pl.program_id(0)
        acc[...] = x1_ref[...]
        for c0 in range(0, D_FF, FFN_CHUNK):
            cols = slice(c0, c0 + FFN_CHUNK)
            gate = up_ref[:, cols]
            val = up_ref[:, D_FF + c0:D_FF + c0 + FFN_CHUNK]
            gp = upp_ref[:, cols]
            gp = jnp.where(i > 0, gp, jnp.zeros_like(gp))
            w = wfc_ref[:, cols].astype(BF16)
            gc = (w[0:1] * _shift_down(gate, 2, gp) + w[1:2] * _shift_down(gate, 1, gp) + w[2:3] * gate
                  + bfc_ref[:, cols].astype(BF16))
            gc_ref[:, cols] = gc
            a = gc * _sigmoid(gc) * val
            a_ref[:, cols] = a
            acc[...] += _dot(a, wd_ref[cols, :])
        if head is None:
            out_ref[...] = acc[...]
        else:
            g_ref, t_ref = refs[6:8]
            sg_ref = refs[n_in + 3]

            @pl.when(i == 0)
            def _():
                sg_ref[...] = jnp.zeros_like(sg_ref)

            dx, loss, dg = _loss_tile(acc[...], g_ref[...], t_ref[...])
            out_ref[...] = dx
            sg_ref[ROW_LOSS:ROW_LOSS + 1, 0:LANES] += jnp.broadcast_to(loss, (1, LANES))
            sg_ref[ROW_FINAL:ROW_FINAL + 1, 0:D_MODEL] += dg

    row = lambda w: pl.BlockSpec((tm, w), lambda i: (i, 0))
    full = lambda r, c: pl.BlockSpec((r, c), lambda i: (0, 0))
    in_specs = [row(2 * D_FF), pl.BlockSpec((HALO, D_FF), lambda i: (jnp.maximum(i * hb - 1, 0), 0)), row(D_MODEL),
                full(3, D_FF), full(1, D_FF), full(D_FF, D_MODEL)]
    out_specs = [row(D_FF), row(D_FF), row(D_MODEL)]
    out_shape = [jax.ShapeDtypeStruct((n, D_FF), BF16), jax.ShapeDtypeStruct((n, D_FF), BF16),
                 jax.ShapeDtypeStruct((n, D_MODEL), F32)]
    args = (up, up, x1, w_fc, b_fc.reshape(1, -1), w_down)
    if head is not None:
        in_specs += [full(1, D_MODEL), row(D_MODEL)]
        out_specs += [full(SG_ROWS, SG_W)]
        out_shape += [jax.ShapeDtypeStruct((SG_ROWS, SG_W), F32)]
        args += (head[0].reshape(1, -1), head[1])
    return _call(body, name=name, grid=(n // tm,), carry=carry, in_specs=in_specs, out_specs=out_specs,
                 out_shape=out_shape, scratch_shapes=[pltpu.VMEM((tm, D_MODEL), F32)], args=args)


def _ffn_backward(dx2, up, gc, w_fc, w_down, sheet, layer, *, tm, name, carry=None):
    n = up.shape[0]
    steps = n // tm
    hb = tm // HALO
    row = SG_LAYER * layer + ROW_FCONV

    def body(dx_ref, dxn_ref, up_ref, upn_ref, gc_ref, gcn_ref, wfc_ref, wd_ref, sg_in, dup_ref, sg_ref):
        i = pl.program_id(0)
        last = i == steps - 1
        _sheet_begin(i, sg_in, sg_ref, row, 4)

        dxe = jnp.concatenate([dx_ref[...], dxn_ref[...]], axis=0).astype(BF16)
        for c0 in range(0, D_FF, FFN_CHUNK):
            cols = slice(c0, c0 + FFN_CHUNK)
            vcols = slice(D_FF + c0, D_FF + c0 + FFN_CHUNK)
            dae = _dot_nt(dxe, wd_ref[cols, :])
            da, dan = dae[:tm], dae[tm:]
            gate = up_ref[:, cols]
            val = up_ref[:, vcols]
            gcv = gc_ref[:, cols]
            s = _sigmoid(gcv)
            dab = da.astype(BF16)
            dup_ref[:, vcols] = dab * (gcv * s)
            dgc = dab * val * (s * (1.0 + gcv * (1.0 - s)))
            gcn = gcn_ref[:, cols]
            sn = _sigmoid(gcn)
            dgcn = dan.astype(BF16) * upn_ref[:, vcols] * (sn * (1.0 + gcn * (1.0 - sn)))
            dgcn = jnp.where(last, jnp.zeros_like(dgcn), dgcn)
            up1 = _shift_up(dgc, 1, dgcn)
            up2 = _shift_up(dgc, 2, dgcn)
            w = wfc_ref[:, cols].astype(BF16)
            dup_ref[:, cols] = w[2:3] * dgc + w[1:2] * up1 + w[0:1] * up2
            sg_ref[row:row + 1, cols] += _column_sums(gate * up2)
            sg_ref[row + 1:row + 2, cols] += _column_sums(gate * up1)
            sg_ref[row + 2:row + 3, cols] += _column_sums(gate * dgc)
            sg_ref[row + 3:row + 4, cols] += _column_sums(dgc)

    nxt = lambda i: (jnp.minimum((i + 1) * hb, steps * hb - 1), 0)
    return _call(
        body, name=name, grid=(steps,), carry=carry,
        in_specs=[pl.BlockSpec((tm, D_MODEL), lambda i: (i, 0)),
                  pl.BlockSpec((HALO, D_MODEL), nxt),
                  pl.BlockSpec((tm, 2 * D_FF), lambda i: (i, 0)),
                  pl.BlockSpec((HALO, 2 * D_FF), nxt),
                  pl.BlockSpec((tm, D_FF), lambda i: (i, 0)),
                  pl.BlockSpec((HALO, D_FF), nxt),
                  pl.BlockSpec((3, D_FF), lambda i: (0, 0)),
                  pl.BlockSpec((D_FF, D_MODEL), lambda i: (0, 0)), _sheet_spec()],
        out_specs=[pl.BlockSpec((tm, 2 * D_FF), lambda i: (i, 0)), _sheet_spec()],
        out_shape=[jax.ShapeDtypeStruct((n, 2 * D_FF), BF16), jax.ShapeDtypeStruct((SG_ROWS, SG_W), F32)],
        args=(dx2, dx2, up, up, gc, gc, w_fc, w_down, sheet))


def _matmul_norm_backward(dz, w_t, x, g, dres, sheet, row, *, tm, name, carry=None):
    n, c = dz.shape
    d = x.shape[1]
    ch = MATMUL_CHUNK

    def body(dz_ref, wt_ref, x_ref, g_ref, dres_ref, *rest):
        i = pl.program_id(0)
        if sheet is None:
            dx_ref, sg_ref = rest

            @pl.when(i == 0)
            def _():
                sg_ref[...] = jnp.zeros_like(sg_ref)
        else:
            sg_in, dx_ref, sg_ref = rest
            _sheet_begin(i, sg_in, sg_ref, row, 1)

        dh = _dot(dz_ref[:, 0:ch], wt_ref[0:ch, :])
        for c0 in range(ch, c, ch):
            dh += _dot(dz_ref[:, c0:c0 + ch], wt_ref[c0:c0 + ch, :])
        xv = x_ref[...]
        r = lax.rsqrt(jnp.mean(xv * xv, axis=-1, keepdims=True) + RMS_EPS)
        xh = xv * r
        sg_ref[row:row + 1, 0:d] += jnp.sum(dh * xh, axis=0, keepdims=True)
        dxh = dh * g_ref[...]
        dx_ref[...] = dres_ref[...] + r * (dxh - xh * jnp.mean(dxh * xh, axis=-1, keepdims=True))

    in_specs = [pl.BlockSpec((tm, c), lambda i: (i, 0)),
                pl.BlockSpec((c, d), lambda i: (0, 0)),
                pl.BlockSpec((tm, d), lambda i: (i, 0)),
                pl.BlockSpec((1, d), lambda i: (0, 0)),
                pl.BlockSpec((tm, d), lambda i: (i, 0))]
    args = (dz, w_t, x, g.reshape(1, d), dres)
    if sheet is None:
        small_spec, small_shape = pl.BlockSpec((8, d), lambda i: (0, 0)), jax.ShapeDtypeStruct((8, d), F32)
    else:
        in_specs, args = in_specs + [_sheet_spec()], args + (sheet,)
        small_spec, small_shape = _sheet_spec(), jax.ShapeDtypeStruct((SG_ROWS, SG_W), F32)
    return _call(
        body, name=name, grid=(n // tm,), carry=carry, in_specs=in_specs,
        out_specs=[pl.BlockSpec((tm, d), lambda i: (i, 0)), small_spec],
        out_shape=[jax.ShapeDtypeStruct((n, d), F32), small_shape], args=args)


def _mix_backward(dx1, z, conv, pa, pb, b_gate, ln_g, ln_b, w_s, w_s_t, b_s, w_sc, w_out, wb, sheet, layer, *, tm, name,
                  carry=None):
    n = z.shape[0]
    steps = n // tm
    hb = tm // HALO
    base = SG_LAYER * layer
    r_bg, r_lng, r_lnb, r_sc = base + ROW_BGATE, base + ROW_LN_G, base + ROW_LN_B, base + ROW_SCONV

    def body(dx_ref, dxn_ref, z_ref, zn_ref, cv_ref, pa_ref, pb_ref, bg_ref, lng_ref, lnb_ref, ws_ref, wst_ref,
             bs_ref, wsc_ref, wo_ref, wb_ref, sg_in,
             dz_ref, dpa_ref, dpb_ref, dws_ref, dbs_ref, sg_ref, f_scr, dvn_scr):
        i = pl.program_id(0)
        last = i == steps - 1
        _sheet_begin(i, sg_in, sg_ref, r_bg, ROW_NORM2 - ROW_BGATE)

        @pl.when(i == 0)
        def _():
            dws_ref[...] = jnp.zeros_like(dws_ref)
            dbs_ref[...] = jnp.zeros_like(dbs_ref)

        dxe = jnp.concatenate([dx_ref[...], dxn_ref[...]], axis=0).astype(BF16)
        dmge = _dot_nt(dxe, wo_ref[...])
        dmg, dmgn = dmge[:tm].astype(BF16), dmge[tm:].astype(BF16)

        pa_v = pa_ref[...]
        pb_v = pb_ref[...]
        bg = bg_ref[...].astype(BF16)
        sa = _sigmoid(z_ref[:, OFF_GA:OFF_GA + D_MODEL] + bg[:, 0:D_MODEL])
        sb = _sigmoid(z_ref[:, OFF_GB:OFF_GB + D_MODEL] + bg[:, D_MODEL:2 * D_MODEL])
        dpa = dmg * sa
        dpb = dmg * sb
        dga = dmg * pa_v * sa * (1.0 - sa)
        dgb = dmg * pb_v * sb * (1.0 - sb)
        dpa_ref[...] = dpa
        dpb_ref[...] = dpb
        dz_ref[:, OFF_GA:OFF_GA + D_MODEL] = dga
        dz_ref[:, OFF_GB:OFF_GB + D_MODEL] = dgb
        sg_ref[r_bg:r_bg + 1, 0:D_MODEL] += _column_sums(dga)
        sg_ref[r_bg:r_bg + 1, D_MODEL:2 * D_MODEL] += _column_sums(dgb)

        dya = _dot_nt(dpa, wb_ref[0]).astype(BF16)
        u = z_ref[:, OFF_U:OFF_U + D_A]
        v = z_ref[:, OFF_V:OFF_V + D_A].astype(F32)
        ln_g = lng_ref[...]
        gu, tu, tv, xh, rstd, vn, f = _gmlp_forward(u, v, ln_g, lnb_ref[...], ws_ref, bs_ref, f_scr)
        dgu = dya * f
        df_bf = dya * gu
        dz_ref[:, OFF_U:OFF_U + D_A] = dgu * _gelu_grad(u, tu)
        mask = _spatial_mask(False)
        mask_t = _spatial_mask(True)
        wmt = [jnp.where(mask_t, wst_ref[h], 0.0).astype(BF16) for h in range(N_HEADS)]
        for b in range(tm // GMLP_BLOCK):
            rows = slice(b * GMLP_BLOCK, (b + 1) * GMLP_BLOCK)
            for h in range(N_HEADS):
                cols = slice(h * HEAD, (h + 1) * HEAD)
                dfb = df_bf[rows, cols]
                dvn_scr[rows, cols] = _dot(wmt[h], dfb)
                dws_ref[h] += jnp.where(mask, _dot_nt(dfb, vn[rows, cols]), 0.0)
                dbs_ref[h] += jnp.sum(dfb.astype(F32), axis=1, keepdims=True)
        dvn = dvn_scr[...]
        sg_ref[r_lng:r_lng + 1, 0:D_A] += jnp.sum(dvn * xh, axis=0, keepdims=True)
        sg_ref[r_lnb:r_lnb + 1, 0:D_A] += jnp.sum(dvn, axis=0, keepdims=True)
        dxh = dvn * ln_g
        dgv = rstd * (dxh - jnp.mean(dxh, axis=-1, keepdims=True) - xh * jnp.mean(dxh * xh, axis=-1, keepdims=True))
        dz_ref[:, OFF_V:OFF_V + D_A] = (dgv * _gelu_grad(v, tv)).astype(BF16)

        sbn = _sigmoid(zn_ref[:, OFF_GB:OFF_GB + D_MODEL] + bg[:, D_MODEL:2 * D_MODEL])
        dpbe = jnp.concatenate([dpb, dmgn * sbn], axis=0)
        dybe = _dot_nt(dpbe, wb_ref[1])
        dyb, dybn = dybe[:tm].astype(BF16), dybe[tm:].astype(BF16)
        bgv = z_ref[:, OFF_BG:OFF_BG + D_B]
        cg = z_ref[:, OFF_CG:OFF_CG + D_B]
        hbv = z_ref[:, OFF_HB:OFF_HB + D_B]
        q = cg * hbv
        dz_ref[:, OFF_BG:OFF_BG + D_B] = dyb * cv_ref[...]
        dconv = dyb * bgv
        dconvn = dybn * zn_ref[:, OFF_BG:OFF_BG + D_B]
        dconvn = jnp.where(last, jnp.zeros_like(dconvn), dconvn)
        up1 = _shift_up(dconv, 1, dconvn)
        up2 = _shift_up(dconv, 2, dconvn)
        sg_ref[r_sc:r_sc + 1, 0:D_B] += _column_sums(q * up2)
        sg_ref[r_sc + 1:r_sc + 2, 0:D_B] += _column_sums(q * up1)
        sg_ref[r_sc + 2:r_sc + 3, 0:D_B] += _column_sums(q * dconv)
        w = wsc_ref[...].astype(BF16)
        dq = w[2:3] * dconv + w[1:2] * up1 + w[0:1] * up2
        dz_ref[:, OFF_CG:OFF_CG + D_B] = dq * hbv
        dz_ref[:, OFF_HB:OFF_HB + D_B] = dq * cg

    row = lambda w: pl.BlockSpec((tm, w), lambda i: (i, 0))
    full = lambda *s: pl.BlockSpec(s, lambda i: (0,) * len(s))
    nxt = lambda i: (jnp.minimum((i + 1) * hb, steps * hb - 1), 0)
    return _call(
        body, name=name, grid=(steps,), carry=carry,
        in_specs=[row(D_MODEL), pl.BlockSpec((HALO, D_MODEL), nxt),
                  row(D_IN), pl.BlockSpec((HALO, D_IN), nxt),
                  row(D_B), row(D_MODEL), row(D_MODEL),
                  full(1, 2 * D_MODEL), full(1, D_A), full(1, D_A),
                  full(N_HEADS, GMLP_BLOCK, GMLP_BLOCK), full(N_HEADS, GMLP_BLOCK, GMLP_BLOCK),
                  full(N_HEADS, GMLP_BLOCK, 1), full(3, D_B),
                  full(D_MODEL, D_MODEL), full(2, D_A, D_MODEL), _sheet_spec()],
        out_specs=[row(D_IN), row(D_MODEL), row(D_MODEL), full(N_HEADS, GMLP_BLOCK, GMLP_BLOCK),
                   full(N_HEADS, GMLP_BLOCK, 1), _sheet_spec()],
        out_shape=[jax.ShapeDtypeStruct((n, D_IN), BF16), jax.ShapeDtypeStruct((n, D_MODEL), BF16),
                   jax.ShapeDtypeStruct((n, D_MODEL), BF16),
                   jax.ShapeDtypeStruct((N_HEADS, GMLP_BLOCK, GMLP_BLOCK), F32),
                   jax.ShapeDtypeStruct((N_HEADS, GMLP_BLOCK, 1), F32), jax.ShapeDtypeStruct((SG_ROWS, SG_W), F32)],
        scratch_shapes=[pltpu.VMEM((tm, D_A), BF16), pltpu.VMEM((tm, D_A), F32)],
        args=(dx1, dx1, z, z, conv, pa, pb, b_gate.reshape(1, -1), ln_g.reshape(1, -1), ln_b.reshape(1, -1), w_s, w_s_t,
              b_s.reshape(N_HEADS, GMLP_BLOCK, 1), w_sc, w_out, wb, sheet))


def _matmul_tn(a, b, *, t1, tn, name, carry=None, pieces=1):
    n, k1 = a.shape
    k2 = b.shape[1]
    steps = n // tn
    w = k2 // pieces

    def body(a_ref, b_ref, *rest):
        o_refs, acc = rest[:pieces], rest[pieces]
        s = pl.program_id(1)

        @pl.when(s == 0)
        def _():
            acc[...] = jnp.zeros_like(acc)

        acc[...] += lax.dot_general(a_ref[...].astype(BF16), b_ref[...].astype(BF16), TN, preferred_element_type=F32)

        @pl.when(s == steps - 1)
        def _():
            for c, o_ref in enumerate(o_refs):
                o_ref[...] = acc[:, c * w:(c + 1) * w].astype(BF16)

    outs, carried = _call(
        body, name=name, grid=(k1 // t1, steps), carry=carry,
        in_specs=[pl.BlockSpec((tn, t1), lambda i, s: (s, i)),
                  pl.BlockSpec((tn, k2), lambda i, s: (s, 0))],
        out_specs=[pl.BlockSpec((t1, w), lambda i, s: (i, 0))] * pieces,
        out_shape=[jax.ShapeDtypeStruct((k1, w), BF16)] * pieces,
        scratch_shapes=[pltpu.VMEM((t1, k2), F32)],
        args=(a, b))
    return (outs[0] if pieces == 1 else list(outs)), carried


def _adamw_math(w, g, m, v):
    m = ADAM_B1 * m + (1.0 - ADAM_B1) * g
    v = ADAM_B2 * v + (1.0 - ADAM_B2) * (g * g)
    m_hat = m / (1.0 - ADAM_B1 ** ADAM_STEP)
    v_hat = v / (1.0 - ADAM_B2 ** ADAM_STEP)
    delta = -ADAM_LR * (m_hat / (jnp.sqrt(v_hat) + ADAM_EPS) + ADAM_WD * w)
    return delta, m, v


def _sum_parts(recvs, *, tr, name):
    _, r, c = recvs[0].shape

    def body(*refs):
        recv_refs, g_ref = refs[:DEPTH], refs[DEPTH]
        layer = pl.program_id(0)
        for l in range(DEPTH):
            @pl.when(layer == l)
            def _(l=l):
                g = recv_refs[l][0].astype(F32)
                for s in range(1, N_DEV):
                    g = g + recv_refs[l][s].astype(F32)
                g_ref[0] = g

    outs, _ = _call(
        body, name=name, grid=(DEPTH, r // tr),
        in_specs=[pl.BlockSpec((N_DEV, tr, c), lambda l, i: (0, i, 0))] * DEPTH,
        out_specs=[pl.BlockSpec((1, tr, c), lambda l, i: (l, i, 0))],
        out_shape=[jax.ShapeDtypeStruct((DEPTH, r, c), F32)],
        args=tuple(recvs))
    return outs[0]


def _adamw(w, g, m, v, *, tr, name):
    r, c = w.shape

    def body(w_ref, g_ref, m_ref, v_ref, d_ref, nm_ref, nv_ref):
        delta, nm, nv = _adamw_math(w_ref[...], g_ref[...], m_ref[...], v_ref[...])
        d_ref[...] = delta
        nm_ref[...] = nm
        nv_ref[...] = nv

    spec = pl.BlockSpec((tr, c), lambda i: (i, 0))
    outs, _ = _call(body, name=name, grid=(r // tr,), in_specs=[spec] * 4, out_specs=[spec] * 3,
                    out_shape=[jax.ShapeDtypeStruct((r, c), F32)] * 3, args=(w, g, m, v))
    return outs


def _sum_adamw(recvs, w, m, v, *, tr, name):
    _, r, c = w.shape
    blocks = len(recvs[0])
    flat = [piece for layer in recvs for piece in layer]

    def body(*refs):
        recv_refs = refs[:len(flat)]
        w_ref, m_ref, v_ref, g_ref, d_ref, nm_ref, nv_ref = refs[len(flat):]
        layer = pl.program_id(0)
        for l in range(DEPTH):
            @pl.when(layer == l)
            def _(l=l):
                cols = []
                for piece in recv_refs[l * blocks:(l + 1) * blocks]:
                    part = piece[0].astype(F32)
                    for s in range(1, N_DEV):
                        part = part + piece[s].astype(F32)
                    cols.append(part)
                g = cols[0] if blocks == 1 else jnp.concatenate(cols, axis=-1)
                delta, nm, nv = _adamw_math(w_ref[0], g, m_ref[0], v_ref[0])
                g_ref[0] = g
                d_ref[0] = delta
                nm_ref[0] = nm
                nv_ref[0] = nv

    spec = pl.BlockSpec((1, tr, c), lambda l, i: (l, i, 0))
    outs, _ = _call(
        body, name=name, grid=(DEPTH, r // tr),
        in_specs=[pl.BlockSpec((N_DEV, tr, c // blocks), lambda l, i: (0, i, 0))] * len(flat) + [spec] * 3,
        out_specs=[spec] * 4, out_shape=[jax.ShapeDtypeStruct((DEPTH, r, c), F32)] * 4,
        args=tuple(flat) + (w, m, v))
    return outs


def _adamw_small(sheet, extra, params, *, name):
    sheet_rows = dict(norm1_g=ROW_NORM1, b_gate=ROW_BGATE, gmlp_ln_g=ROW_LN_G, gmlp_ln_b=ROW_LN_B, norm2_g=ROW_NORM2,
                      b_ffn_conv=ROW_BFCONV)
    names = list(params)
    extra_names = list(extra)

    def body(*refs):
        sg_ref, refs = refs[0], refs[1:]
        extra_refs, refs = dict(zip(extra_names, refs[:len(extra_names)])), refs[len(extra_names):]
        ins, outs = refs[:3 * len(names)], refs[3 * len(names):]
        for j, key in enumerate(names):
            w_ref, m_ref, v_ref = ins[3 * j:3 * j + 3]
            g_ref, d_ref, nm_ref, nv_ref = outs[4 * j:4 * j + 4]
            if key in extra_refs:
                g_ref[...] = extra_refs[key][...]
            elif key == "final_g":
                g_ref[...] = sg_ref[ROW_FINAL:ROW_FINAL + 1, 0:D_MODEL]
            else:
                width = w_ref.shape[-1]
                for l in range(DEPTH):
                    row = SG_LAYER * l + sheet_rows[key]
                    g_ref[l:l + 1, :] = sg_ref[row:row + 1, 0:width]
            delta, nm, nv = _adamw_math(w_ref[...], g_ref[...], m_ref[...], v_ref[...])
            d_ref[...] = delta
            nm_ref[...] = nm
            nv_ref[...] = nv

    args = [sheet] + [extra[k] for k in extra_names] + [t for k in names for t in params[k]]
    vmem = pl.BlockSpec(memory_space=pltpu.VMEM)
    outs = pl.pallas_call(
        body, name=name, in_specs=[vmem] * len(args), out_specs=[vmem] * (4 * len(names)),
        out_shape=[jax.ShapeDtypeStruct(params[k][0].shape, F32) for k in names for _ in range(4)],
    )(*args)
    return {k: tuple(outs[4 * j:4 * j + 4]) for j, k in enumerate(names)}


def _rows(gathered):
    return gathered.reshape(N_DEV * gathered.shape[1], gathered.shape[2])


def _parts(full):
    return full.reshape(N_DEV, full.shape[0] // N_DEV, full.shape[1])


def kernel(x, norm1_g, w_in, b_gate, gmlp_ln_g, gmlp_ln_b, w_spatial, b_spatial, w_shortconv, w_branch, w_out, norm2_g, w_ffn_up, w_ffn_conv, b_ffn_conv, w_ffn_down, final_g, loss_target, m_norm1_g, m_w_in, m_b_gate, m_gmlp_ln_g, m_gmlp_ln_b, m_w_spatial, m_b_spatial, m_w_shortconv, m_w_branch, m_w_out, m_norm2_g, m_w_ffn_up, m_w_ffn_conv, m_b_ffn_conv, m_w_ffn_down, m_final_g, v_norm1_g, v_w_in, v_b_gate, v_gmlp_ln_g, v_gmlp_ln_b, v_w_spatial, v_b_spatial, v_w_shortconv, v_w_branch, v_w_out, v_norm2_g, v_w_ffn_up, v_w_ffn_conv, v_b_ffn_conv, v_w_ffn_down, v_final_g):
    n = x.shape[1]
    tm_in, tm, tn = 1024, 512, 2048
    x0 = x.reshape(n, D_MODEL)
    target = loss_target.reshape(n, D_MODEL)
    my_idx = 4 * lax.axis_index("x") + 2 * lax.axis_index("y") + lax.axis_index("c")
    sc_w, fc_w = D_B // N_DEV, D_FF // N_DEV

    sh_in = [w_in[l].T.astype(BF16) for l in range(DEPTH)]
    sh_up = [w_ffn_up[l].T.astype(BF16) for l in range(DEPTH)]
    sh_br = [w_branch[l].astype(BF16) for l in range(DEPTH)]
    sh_out = [w_out[l].astype(BF16) for l in range(DEPTH)]
    sh_down = [w_ffn_down[l].astype(BF16) for l in range(DEPTH)]
    taps = jnp.concatenate([w_shortconv, w_ffn_conv], axis=-1)

    def branch_weights(g):
        return g.transpose(1, 2, 0, 3).reshape(2, D_A, D_MODEL)

    g_in0, g_taps = _gather_now([sh_in[0], taps], name="gather_first")
    w_sc = [g_taps[:, l, :, :sc_w].transpose(1, 0, 2).reshape(3, D_B) for l in range(DEPTH)]
    w_fc = [g_taps[:, l, :, sc_w:].transpose(1, 0, 2).reshape(3, D_FF) for l in range(DEPTH)]
    w_s_t = [w_spatial[l].transpose(0, 2, 1) for l in range(DEPTH)]
    weights = [dict(), dict()]
    weights[0]["in_t"] = _rows(g_in0)
    saved = []
    xc = x0
    for l in range(DEPTH):
        p = weights[l]
        carry = _Gather([sh_br[0], sh_out[0]] if l == 0 else [sh_up[1]])
        (h, z), got = _norm_matmul(xc, norm1_g[l], p["in_t"], tm=tm_in, name=f"fwd_in_{l}", carry=carry)
        if l == 0:
            p["wb"], p["out"] = branch_weights(got[0]), _rows(got[1])
        else:
            p["up_t"] = _rows(got[0])
        carry = _Gather([sh_up[0]]) if l == 0 else None
        (ya, yb, conv, pa, pb, mg, x1), got = _mix_forward(
            z, xc, b_gate[l], gmlp_ln_g[l], gmlp_ln_b[l], w_spatial[l], b_spatial[l], w_sc[l], p["wb"], p["out"],
            tm=tm, name=f"fwd_mix_{l}", carry=carry)
        if l == 0:
            p["up_t"] = _rows(got[0])
        (h2, up), got = _norm_matmul(x1, norm2_g[l], p["up_t"], tm=tm, name=f"fwd_up_{l}", carry=_Gather([sh_down[l]]))
        p["down"] = _rows(got[0])
        carry = _Gather([sh_br[1], sh_out[1], sh_in[1]]) if l == 0 else None
        head = (final_g, target) if l == DEPTH - 1 else None
        outs, got = _ffn_forward(up, x1, w_fc[l], b_ffn_conv[l], p["down"], tm=tm, name=f"fwd_ffn_{l}", carry=carry, head=head)
        if l == 0:
            weights[1]["wb"], weights[1]["out"], weights[1]["in_t"] = branch_weights(got[0]), _rows(got[1]), _rows(got[2])
        gc, a = outs[0], outs[1]
        saved.append(dict(x=xc, h=h, z=z, ya=ya, yb=yb, conv=conv, pa=pa, pb=pb, mg=mg, x1=x1, h2=h2, up=up, gc=gc, a=a))
        xc = outs[2]
    dx, sheet = outs[2], outs[3]

    recv = [dict(), dict()]
    small_dws, small_dbs = [None] * DEPTH, [None] * DEPTH
    pending_in = None
    for l in reversed(range(DEPTH)):
        p, s = weights[l], saved[l]
        carry = _Exchange([pending_in]) if pending_in is not None else None
        (dup, sheet), got = _ffn_backward(dx, s["up"], s["gc"], w_fc[l], p["down"], sheet, l, tm=tm, name=f"bwd_ffn_{l}",
                                          carry=carry)
        if got is not None:
            recv[l + 1]["in_t"] = got[0]
        dw_down, _ = _matmul_tn(s["a"], dx, t1=D_FF // 2, tn=tn, name=f"dw_down_{l}")
        dw_up_t, got = _matmul_tn(dup, s["h2"], t1=2 * D_FF // 4, tn=tn, name=f"dw_up_{l}", pieces=2,
                                  carry=_Exchange([_parts(dw_down)]))
        recv[l]["down"] = got[0]
        (dx1, sheet), got_left = _matmul_norm_backward(
            dup, p["up_t"], s["x1"], norm2_g[l], dx, sheet, SG_LAYER * l + ROW_NORM2, tm=tm, name=f"bwd_up_{l}",
            carry=_Exchange([_parts(dw_up_t[0])]))
        dw_out, _ = _matmul_tn(s["mg"], dx1, t1=D_MODEL, tn=tn, name=f"dw_out_{l}")
        (dz, dpa, dpb, small_dws[l], small_dbs[l], sheet), got_right = _mix_backward(
            dx1, s["z"], s["conv"], s["pa"], s["pb"], b_gate[l], gmlp_ln_g[l], gmlp_ln_b[l], w_spatial[l], w_s_t[l],
            b_spatial[l], w_sc[l], p["out"], p["wb"], sheet, l, tm=tm, name=f"bwd_mix_{l}",
            carry=_Exchange([_parts(dw_up_t[1]), _parts(dw_out)]))
        recv[l]["up_t"], recv[l]["out"] = [got_left[0], got_right[0]], got_right[1]
        dw_bra_t, _ = _matmul_tn(dpa, s["ya"], t1=D_MODEL, tn=tn, name=f"dw_branch_a_{l}")
        dw_brb_t, _ = _matmul_tn(dpb, s["yb"], t1=D_MODEL, tn=tn, name=f"dw_branch_b_{l}")
        carry = _Exchange([_parts(dw_bra_t), _parts(dw_brb_t)])
        if l == 0:
            dbs = jnp.stack([t.reshape(N_HEADS, GMLP_BLOCK) for t in small_dbs]).reshape(DEPTH * N_HEADS, GMLP_BLOCK)
            carry = _Both(carry, _Gather([sheet, small_dws[0], small_dws[1], dbs]))
        dw_in_t, got = _matmul_tn(dz, s["h"], t1=D_IN // 4, tn=tn, name=f"dw_in_{l}", carry=carry)
        recv[l]["bra_t"], recv[l]["brb_t"] = got[:2]
        if l == 0:
            gathered_small = got[2:]
            (dx0, dg1_first), got = _matmul_norm_backward(dz, p["in_t"], s["x"], norm1_g[l], dx1, None, 0, tm=tm,
                                                         name=f"bwd_in_{l}", carry=_Exchange([_parts(dw_in_t)]))
            recv[0]["in_t"] = got[0]
        else:
            (dx0, sheet), _ = _matmul_norm_backward(dz, p["in_t"], s["x"], norm1_g[l], dx1, sheet,
                                                    SG_LAYER * l + ROW_NORM1, tm=tm, name=f"bwd_in_{l}")
            pending_in = _parts(dw_in_t)
        dx = dx0
    grad_x = dx.reshape(x.shape)

    results = {}
    both = lambda key: [recv[l][key] for l in range(DEPTH)]
    blocks = lambda key: [r if isinstance(r, list) else [r] for r in both(key)]
    swap = lambda t: t.transpose(0, 2, 1)
    for key, slab, (w, m, v), tr in [("w_in", "in_t", (w_in, m_w_in, v_w_in), 192),
                                     ("w_ffn_up", "up_t", (w_ffn_up, m_w_ffn_up, v_w_ffn_up), 176)]:
        outs = _sum_adamw(blocks(slab), swap(w), swap(m), swap(v), tr=tr, name=f"adamw_{key}")
        results[key] = tuple(swap(o) for o in outs)
    g_bra = _sum_parts(both("bra_t"), tr=128, name="sum_w_branch_a").transpose(0, 2, 1)
    g_brb = _sum_parts(both("brb_t"), tr=128, name="sum_w_branch_b").transpose(0, 2, 1)
    g_br = jnp.stack([g_bra, g_brb], axis=1)
    flat = lambda t: t.reshape(-1, t.shape[-1])
    outs = _adamw(flat(w_branch), flat(g_br), flat(m_w_branch), flat(v_w_branch), tr=512, name="adamw_w_branch")
    results["w_branch"] = (g_br,) + tuple(o.reshape(w_branch.shape) for o in outs)
    results["w_out"] = tuple(_sum_adamw(blocks("out"), w_out, m_w_out, v_w_out, tr=128, name="adamw_w_out"))
    results["w_ffn_down"] = tuple(_sum_adamw(blocks("down"), w_ffn_down, m_w_ffn_down, v_w_ffn_down, tr=176,
                                             name="adamw_w_ffn_down"))

    sheet, dws0, dws1, dbs = _sum_gathered(gathered_small, name="sum_small_grads")
    (dg1_first,) = _all_reduce_small([dg1_first], name="all_reduce_last_gain")
    sheet = sheet.at[ROW_NORM1, :D_MODEL].set(dg1_first[0])
    loss = sheet[ROW_LOSS, 0]
    taps = lambda row, width: jnp.stack([sheet[SG_LAYER * l + row:SG_LAYER * l + row + 3, :width] for l in range(DEPTH)])
    extra = dict(w_spatial=jnp.stack([dws0, dws1]), b_spatial=dbs.reshape(DEPTH, N_HEADS, GMLP_BLOCK),
                 w_shortconv=lax.dynamic_slice_in_dim(taps(ROW_SCONV, D_B), my_idx * sc_w, sc_w, axis=2),
                 w_ffn_conv=lax.dynamic_slice_in_dim(taps(ROW_FCONV, D_FF), my_idx * fc_w, fc_w, axis=2))
    small_w = dict(norm1_g=(norm1_g, m_norm1_g, v_norm1_g), b_gate=(b_gate, m_b_gate, v_b_gate),
                   gmlp_ln_g=(gmlp_ln_g, m_gmlp_ln_g, v_gmlp_ln_g), gmlp_ln_b=(gmlp_ln_b, m_gmlp_ln_b, v_gmlp_ln_b),
                   w_spatial=(w_spatial, m_w_spatial, v_w_spatial), b_spatial=(b_spatial, m_b_spatial, v_b_spatial),
                   w_shortconv=(w_shortconv, m_w_shortconv, v_w_shortconv), norm2_g=(norm2_g, m_norm2_g, v_norm2_g),
                   w_ffn_conv=(w_ffn_conv, m_w_ffn_conv, v_w_ffn_conv), b_ffn_conv=(b_ffn_conv, m_b_ffn_conv, v_b_ffn_conv),
                   final_g=tuple(t.reshape(1, D_MODEL) for t in (final_g, m_final_g, v_final_g)))
    results.update(_adamw_small(sheet, extra, small_w, name="adamw_small"))
    results["final_g"] = tuple(t.reshape(D_MODEL) for t in results["final_g"])

    names = ["norm1_g", "w_in", "b_gate", "gmlp_ln_g", "gmlp_ln_b", "w_spatial", "b_spatial", "w_shortconv", "w_branch",
             "w_out", "norm2_g", "w_ffn_up", "w_ffn_conv", "b_ffn_conv", "w_ffn_down", "final_g"]
    return (loss, grad_x, *[results[k][0] for k in names], *[results[k][1] for k in names],
            *[results[k][2] for k in names], *[results[k][3] for k in names])
```

```python
import math

import jax
import jax.numpy as jnp
from jax import lax
from jax.experimental import pallas as pl
from jax.experimental.pallas import tpu as pltpu

F32 = jnp.float32
BF16 = jnp.bfloat16

N_DEV = 8
DEPTH = 2
D_MODEL = 1024
D_A = 512
D_B = 512
D_FF = 2816
D_IN = 4608
N_HEADS = 4
HEAD = 128
GMLP_BLOCK = 128
CAUSAL_CHUNK = 64
OFF_U, OFF_V, OFF_BG, OFF_CG, OFF_HB, OFF_GA, OFF_GB = 0, 512, 1024, 1536, 2048, 2560, 3584
RMS_EPS = 1e-6
LN_EPS = 1e-5
ADAM_LR, ADAM_B1, ADAM_B2, ADAM_EPS, ADAM_WD, ADAM_STEP = 0.001, 0.9, 0.999, 1e-08, 0.01, 10

SUBLANES, LANES = 8, 128
MATMUL_CHUNK = 512
HALO = 16
FFN_CHUNK = 256
SG_ROWS, SG_W, SG_LAYER = 40, D_FF, 16
ROW_NORM1, ROW_BGATE, ROW_LN_G, ROW_LN_B, ROW_SCONV, ROW_NORM2, ROW_FCONV, ROW_BFCONV = 0, 1, 2, 3, 4, 7, 8, 11
ROW_FINAL, ROW_LOSS = 32, 33
V7X_VMEM_BYTES = 64 << 20
VMEM_LIMIT = V7X_VMEM_BYTES - (8 << 20)
MESH = pl.DeviceIdType.MESH
GELU_C0 = 0.7978845608028654
GELU_C1 = 0.044715
NT = (((1,), (1,)), ((), ()))
TN = (((0,), (0,)), ((), ()))


def _dot(a, b):
    return jnp.dot(a, b, preferred_element_type=F32)


def _dot_nt(a, b):
    return lax.dot_general(a, b, NT, preferred_element_type=F32)


def _sigmoid(x):
    return 1.0 / (1.0 + jnp.exp(-x))


def _gelu_tanh(x):
    return jnp.tanh(GELU_C0 * (x + GELU_C1 * x * x * x))


def _gelu_grad(x, t):
    return 0.5 * (1.0 + t) + 0.5 * x * (1.0 - t * t) * GELU_C0 * (1.0 + 3.0 * GELU_C1 * x * x)


def _sublane_tile(dtype):
    return SUBLANES * (4 // jnp.dtype(dtype).itemsize)


def _shift_down(a, k, prev):
    p = prev.shape[0]
    r = pltpu.roll(a, k, 0)
    sub = _sublane_tile(a.dtype)
    head = r[0:sub]
    rid = lax.broadcasted_iota(jnp.int32, head.shape, 0)
    for j in range(k):
        head = jnp.where(rid == j, prev[p - k + j:p - k + j + 1, :], head)
    return jnp.concatenate([head, r[sub:]], axis=0)


def _shift_up(a, k, nxt):
    t = a.shape[0]
    r = pltpu.roll(a, t - k, 0)
    sub = _sublane_tile(a.dtype)
    tail = r[t - sub:t]
    rid = lax.broadcasted_iota(jnp.int32, tail.shape, 0)
    for j in range(k):
        tail = jnp.where(rid == sub - k + j, nxt[j:j + 1, :], tail)
    return jnp.concatenate([r[0:t - sub], tail], axis=0)


def _column_sums(p):
    if p.dtype.itemsize < 4:
        t = p.shape[0]
        p = p[:t // 2] + p[t // 2:]
        p = p[:t // 4] + p[t // 4:]
    return jnp.sum(p.astype(F32), axis=0, keepdims=True)


def _sheet_begin(step, sheet_in, sheet_out, first_row, rows):
    @pl.when(step == 0)
    def _():
        sheet_out[...] = sheet_in[...]
        sheet_out[first_row:first_row + rows, :] = jnp.zeros((rows, SG_W), F32)


def _sheet_spec():
    return pl.BlockSpec((SG_ROWS, SG_W), lambda i: (0, 0))


def _spatial_mask(transposed):
    ri = lax.broadcasted_iota(jnp.int32, (GMLP_BLOCK, GMLP_BLOCK), 0) // CAUSAL_CHUNK
    ci = lax.broadcasted_iota(jnp.int32, (GMLP_BLOCK, GMLP_BLOCK), 1) // CAUSAL_CHUNK
    return (ri <= ci) if transposed else (ci <= ri)


def _gmlp_forward(u, v, ln_g, ln_b, ws_ref, bs_ref, f_scr):
    tm = u.shape[0]
    tu = _gelu_tanh(u)
    tv = _gelu_tanh(v)
    gu = 0.5 * u * (1.0 + tu)
    gv = 0.5 * v * (1.0 + tv)
    mu = jnp.mean(gv, axis=-1, keepdims=True)
    cen = gv - mu
    rstd = lax.rsqrt(jnp.mean(cen * cen, axis=-1, keepdims=True) + LN_EPS)
    xh = cen * rstd
    vn = (xh * ln_g + ln_b).astype(BF16)
    mask = _spatial_mask(False)
    wm = [jnp.where(mask, ws_ref[h], 0.0).astype(BF16) for h in range(N_HEADS)]
    for b in range(tm // GMLP_BLOCK):
        rows = slice(b * GMLP_BLOCK, (b + 1) * GMLP_BLOCK)
        for h in range(N_HEADS):
            cols = slice(h * HEAD, (h + 1) * HEAD)
            f_scr[rows, cols] = (_dot(wm[h], vn[rows, cols]) + bs_ref[h]).astype(f_scr.dtype)
    return gu, tu, tv, xh, rstd, vn, f_scr[...]


def _position():
    return lax.axis_index("x"), lax.axis_index("y"), lax.axis_index("c")


def _handshake(peers):
    barrier = pltpu.get_barrier_semaphore()
    for peer in peers:
        pl.semaphore_signal(barrier, inc=1, device_id=peer, device_id_type=MESH)
    pl.semaphore_wait(barrier, len(peers))


class _Gather:
    collective_id = 1

    def __init__(self, arrays, relay_early=False):
        self.arrays = list(arrays)
        self.out_shape = [jax.ShapeDtypeStruct((N_DEV,) + a.shape, a.dtype) for a in self.arrays]
        self.base = 0
        self.relay_early = relay_early

    def barrier(self):
        x, y, c = _position()
        _handshake([(x, y, 1 - c), (1 - x, y, c), (x, 1 - y, c), (1 - x, 1 - y, c)])

    def _plan(self, ins, outs, sems):
        send_sems, recv_sems, local_sems = sems
        x, y, c = _position()
        me, sibling = (x, y, c), (x, y, 1 - c)
        chips = [(1 - x, y), (x, 1 - y), (1 - x, 1 - y)]

        def slot(a, p):
            return outs[a].at[4 * p[0] + 2 * p[1] + p[2]]

        def copy(a, k, block, to, src=None):
            return pltpu.make_async_remote_copy(
                src_ref=slot(a, block) if src is None else src, dst_ref=slot(a, block),
                send_sem=send_sems.at[self.base + a, k], recv_sem=recv_sems.at[self.base + a, k],
                device_id=to, device_id_type=MESH)

        n = len(self.arrays)

        def mine():
            return [pltpu.make_async_copy(ins[a], slot(a, me), local_sems.at[self.base + a]) for a in range(n)]

        def first():
            out = []
            for a in range(n):
                out.append(copy(a, 0, me, sibling, src=ins[a]))
                out += [copy(a, 1 + j, me, (*chip, c), src=ins[a]) for j, chip in enumerate(chips)]
            return out

        def arrivals():
            return [copy(a, 1 + j, (*chip, c), me) for j, chip in enumerate(chips) for a in range(n)]

        def relays():
            return [copy(a, 4 + j, (*chip, c), sibling) for j, chip in enumerate(chips) for a in range(n)]

        def from_sibling():
            out = [copy(a, 0, sibling, me) for a in range(n)]
            return out + [copy(a, 4 + j, (*chip, 1 - c), me) for j, chip in enumerate(chips) for a in range(n)]

        return mine, first, arrivals, relays, from_sibling

    def start(self, ins, outs, sems):
        mine, first, _, _, _ = self._plan(ins, outs, sems)
        for cp in mine() + first():
            cp.start()

    def relay(self, ins, outs, sems):
        _, _, arrivals, relays, _ = self._plan(ins, outs, sems)
        for arrived, onward in zip(arrivals(), relays()):
            arrived.wait_recv()
            onward.start()

    def finish(self, ins, outs, sems):
        mine, first, _, relays, from_sibling = self._plan(ins, outs, sems)
        for cp in from_sibling():
            cp.wait_recv()
        for cp in first() + relays():
            cp.wait_send()
        for cp in mine():
            cp.wait()


class _Exchange:
    collective_id = 0

    def __init__(self, arrays):
        self.arrays = list(arrays)
        self.out_shape = [jax.ShapeDtypeStruct(a.shape, a.dtype) for a in self.arrays]
        self.base = 0

    def barrier(self):
        x, y, c = _position()
        _handshake([(x ^ dx, y ^ dy, c ^ dc) for dx in (0, 1) for dy in (0, 1) for dc in (0, 1) if dx or dy or dc])

    def _plan(self, ins, outs, sems):
        send_sems, recv_sems, local_sems = sems
        x, y, c = _position()
        my_idx = 4 * x + 2 * y + c
        n = len(self.arrays)
        offsets = [(dx, dy, dc) for dx in (0, 1) for dy in (0, 1) for dc in (0, 1) if (dx, dy, dc) != (0, 0, 0)]

        def mine():
            return [pltpu.make_async_copy(ins[a].at[my_idx], outs[a].at[my_idx], local_sems.at[self.base + a])
                    for a in range(n)]

        def remote(arriving):
            out = []
            for k, (dx, dy, dc) in enumerate(offsets):
                px, py, pc = x ^ dx, y ^ dy, c ^ dc
                p_idx = 4 * px + 2 * py + pc
                for a in range(n):
                    out.append(pltpu.make_async_remote_copy(
                        src_ref=ins[a].at[p_idx], dst_ref=outs[a].at[p_idx if arriving else my_idx],
                        send_sem=send_sems.at[self.base + a, k], recv_sem=recv_sems.at[self.base + a, k],
                        device_id=(px, py, pc), device_id_type=MESH))
            return out

        return mine, remote

    def start(self, ins, outs, sems):
        mine, remote = self._plan(ins, outs, sems)
        for cp in mine() + remote(False):
            cp.start()

    def relay(self, ins, outs, sems):
        pass

    def finish(self, ins, outs, sems):
        mine, remote = self._plan(ins, outs, sems)
        for cp in remote(True):
            cp.wait_recv()
        for cp in remote(False):
            cp.wait_send()
        for cp in mine():
            cp.wait()


class _Both:
    def __init__(self, *carries):
        self.carries = carries
        self.arrays = [a for c in carries for a in c.arrays]
        self.out_shape = [s for c in carries for s in c.out_shape]
        first = 0
        for c in carries:
            c.base = first
            first += len(c.arrays)
        self.collective_id = min(c.collective_id for c in carries)

    def barrier(self):
        min(self.carries, key=lambda c: c.collective_id).barrier()

    def _each(self, method, ins, outs, sems):
        for c in self.carries:
            rows = slice(c.base, c.base + len(c.arrays))
            getattr(c, method)(ins[rows], outs[rows], sems)

    def start(self, ins, outs, sems):
        self._each("start", ins, outs, sems)

    def relay(self, ins, outs, sems):
        self._each("relay", ins, outs, sems)

    def finish(self, ins, outs, sems):
        self._each("finish", ins, outs, sems)


def _call(body, *, name, grid, in_specs, out_specs, out_shape, args, scratch_shapes=(), carry=None):
    n_in, n_out, n_scr = len(in_specs), len(out_specs), len(scratch_shapes)
    params = pltpu.CompilerParams(dimension_semantics=("arbitrary",) * len(grid), vmem_limit_bytes=VMEM_LIMIT)
    if carry is None:
        outs = pl.pallas_call(body, name=name, grid=grid, in_specs=in_specs, out_specs=out_specs, out_shape=out_shape,
                              scratch_shapes=list(scratch_shapes), compiler_params=params)(*args)
        return outs, None
    m = len(carry.arrays)
    total = math.prod(grid)

    def wrapped(*refs):
        ins, refs = refs[:n_in], refs[n_in:]
        c_ins, refs = refs[:m], refs[m:]
        outs, refs = refs[:n_out], refs[n_out:]
        c_outs, refs = refs[:m], refs[m:]
        scr, sems = refs[:n_scr], refs[n_scr:]
        flat = pl.program_id(0)
        for d in range(1, len(grid)):
            flat = flat * grid[d] + pl.program_id(d)

        @pl.when(flat == 0)
        def _():
            carry.barrier()
            carry.start(c_ins, c_outs, sems)

        body(*ins, *outs, *scr)

        @pl.when(flat == (total // 2 if getattr(carry, "relay_early", False) else total - 2))
        def _():
            carry.relay(c_ins, c_outs, sems)

        @pl.when(flat == total - 1)
        def _():
            carry.finish(c_ins, c_outs, sems)

    any_spec = pl.BlockSpec(memory_space=pl.ANY)
    sem_shapes = [pltpu.SemaphoreType.DMA((m, 7)), pltpu.SemaphoreType.DMA((m, 7)), pltpu.SemaphoreType.DMA((m,))]
    params = pltpu.CompilerParams(dimension_semantics=("arbitrary",) * len(grid), vmem_limit_bytes=VMEM_LIMIT,
                                  collective_id=carry.collective_id)
    outs = pl.pallas_call(
        wrapped, name=name, grid=grid,
        in_specs=list(in_specs) + [any_spec] * m, out_specs=list(out_specs) + [any_spec] * m,
        out_shape=list(out_shape) + carry.out_shape,
        scratch_shapes=list(scratch_shapes) + sem_shapes, compiler_params=params)(*args, *carry.arrays)
    return outs[:n_out], outs[n_out:]


def _gather_now(arrays, *, name):
    carry = _Gather(arrays)
    m = len(arrays)

    def body(*refs):
        ins, outs, sems = refs[:m], refs[m:2 * m], refs[2 * m:]
        carry.barrier()
        carry.start(ins, outs, sems)
        carry.relay(ins, outs, sems)
        carry.finish(ins, outs, sems)

    any_spec = pl.BlockSpec(memory_space=pl.ANY)
    return pl.pallas_call(
        body, name=name, in_specs=[any_spec] * m, out_specs=[any_spec] * m, out_shape=carry.out_shape,
        scratch_shapes=[pltpu.SemaphoreType.DMA((m, 7)), pltpu.SemaphoreType.DMA((m, 7)),
                        pltpu.SemaphoreType.DMA((m,))],
        compiler_params=pltpu.CompilerParams(collective_id=carry.collective_id),
    )(*arrays)


def _all_reduce_small(arrs, *, name):
    n = len(arrs)

    def body(*refs):
        ins, outs, bufs = refs[:n], refs[n:2 * n], refs[2 * n:3 * n]
        send_sems, recv_sems = refs[3 * n:]
        x, y, c = _position()
        me, sibling = (x, y, c), (x, y, 1 - c)
        chips = [(1 - x, y), (x, 1 - y), (1 - x, 1 - y)]
        _handshake([sibling] + [(*chip, c) for chip in chips])

        def copy(a, k, block, to, src=None):
            slot = bufs[a].at[4 * block[0] + 2 * block[1] + block[2]]
            return pltpu.make_async_remote_copy(
                src_ref=slot if src is None else src, dst_ref=slot,
                send_sem=send_sems.at[a, k], recv_sem=recv_sems.at[a, k], device_id=to, device_id_type=MESH)

        first = []
        for a in range(n):
            first.append(copy(a, 0, me, sibling, src=ins[a]))
            first += [copy(a, 1 + j, me, (*chip, c), src=ins[a]) for j, chip in enumerate(chips)]
        for cp in first:
            cp.start()
        passed = []
        for j, chip in enumerate(chips):
            for a in range(n):
                copy(a, 1 + j, (*chip, c), me).wait_recv()
                cp = copy(a, 4 + j, (*chip, c), sibling)
                cp.start()
                passed.append(cp)
        for a in range(n):
            copy(a, 0, sibling, me).wait_recv()
            for j, chip in enumerate(chips):
                copy(a, 4 + j, (*chip, 1 - c), me).wait_recv()
        for cp in first + passed:
            cp.wait_send()
        my_idx = 4 * x + 2 * y + c
        for a in range(n):
            acc = jnp.zeros(ins[a].shape, F32)
            for s in range(N_DEV):
                acc = acc + jnp.where(my_idx == s, ins[a][...], bufs[a][s])
            outs[a][...] = acc

    vmem = pl.BlockSpec(memory_space=pltpu.VMEM)
    return pl.pallas_call(
        body, name=name, in_specs=[vmem] * n, out_specs=[vmem] * n,
        out_shape=[jax.ShapeDtypeStruct(a.shape, F32) for a in arrs],
        scratch_shapes=[pltpu.VMEM((N_DEV,) + a.shape, F32) for a in arrs]
        + [pltpu.SemaphoreType.DMA((n, 7)), pltpu.SemaphoreType.DMA((n, 7))],
        compiler_params=pltpu.CompilerParams(vmem_limit_bytes=VMEM_LIMIT, collective_id=_Gather.collective_id),
    )(*arrs)


def _sum_gathered(arrs, *, name):
    n = len(arrs)

    def body(*refs):
        for in_ref, out_ref in zip(refs[:n], refs[n:]):
            acc = in_ref[0]
            for s in range(1, N_DEV):
                acc = acc + in_ref[s]
            out_ref[...] = acc

    vmem = pl.BlockSpec(memory_space=pltpu.VMEM)
    return pl.pallas_call(
        body, name=name, in_specs=[vmem] * n, out_specs=[vmem] * n,
        out_shape=[jax.ShapeDtypeStruct(a.shape[1:], F32) for a in arrs],
        compiler_params=pltpu.CompilerParams(vmem_limit_bytes=VMEM_LIMIT),
    )(*arrs)


def _norm_matmul(x, g, w_t, *, tm, name, carry=None):
    n, d = x.shape
    c = w_t.shape[0]
    ch = MATMUL_CHUNK

    def body(x_ref, g_ref, wt_ref, h_ref, z_ref):
        xv = x_ref[...]
        r = lax.rsqrt(jnp.mean(xv * xv, axis=-1, keepdims=True) + RMS_EPS)
        h = (xv * r * g_ref[...]).astype(BF16)
        h_ref[...] = h
        for c0 in range(0, c, ch):
            z_ref[:, c0:c0 + ch] = _dot_nt(h, wt_ref[c0:c0 + ch, :]).astype(BF16)

    return _call(
        body, name=name, grid=(n // tm,), carry=carry,
        in_specs=[pl.BlockSpec((tm, d), lambda i: (i, 0)),
                  pl.BlockSpec((1, d), lambda i: (0, 0)),
                  pl.BlockSpec((c, d), lambda i: (0, 0))],
        out_specs=[pl.BlockSpec((tm, d), lambda i: (i, 0)),
                   pl.BlockSpec((tm, c), lambda i: (i, 0))],
        out_shape=[jax.ShapeDtypeStruct((n, d), BF16), jax.ShapeDtypeStruct((n, c), BF16)],
        args=(x, g.reshape(1, d), w_t))


def _mix_forward(z, x, b_gate, ln_g, ln_b, w_s, b_s, w_sc, wb, w_out, *, tm, name, carry=None):
    n = z.shape[0]
    hb = tm // HALO

    def body(z_ref, zp_ref, x_ref, bg_ref, lng_ref, lnb_ref, ws_ref, bs_ref, wsc_ref, wb_ref, wo_ref,
             ya_ref, yb_ref, cv_ref, pa_ref, pb_ref, mg_ref, x1_ref, f_scr):
        i = pl.program_id(0)
        u = z_ref[:, OFF_U:OFF_U + D_A]
        v = z_ref[:, OFF_V:OFF_V + D_A].astype(F32)
        gu, _, _, _, _, _, f = _gmlp_forward(u, v, lng_ref[...], lnb_ref[...], ws_ref, bs_ref, f_scr)
        ya = gu * f
        ya_ref[...] = ya

        q = z_ref[:, OFF_CG:OFF_CG + D_B] * z_ref[:, OFF_HB:OFF_HB + D_B]
        qp = zp_ref[:, OFF_CG:OFF_CG + D_B] * zp_ref[:, OFF_HB:OFF_HB + D_B]
        qp = jnp.where(i > 0, qp, jnp.zeros_like(qp))
        w = wsc_ref[...].astype(BF16)
        conv = w[0:1] * _shift_down(q, 2, qp) + w[1:2] * _shift_down(q, 1, qp) + w[2:3] * q
        cv_ref[...] = conv
        yb = z_ref[:, OFF_BG:OFF_BG + D_B] * conv
        yb_ref[...] = yb

        pa = _dot(ya, wb_ref[0]).astype(BF16)
        pb = _dot(yb, wb_ref[1]).astype(BF16)
        pa_ref[...] = pa
        pb_ref[...] = pb
        bg = bg_ref[...].astype(BF16)
        sa = _sigmoid(z_ref[:, OFF_GA:OFF_GA + D_MODEL] + bg[:, 0:D_MODEL])
        sb = _sigmoid(z_ref[:, OFF_GB:OFF_GB + D_MODEL] + bg[:, D_MODEL:2 * D_MODEL])
        mg = sa * pa + sb * pb
        mg_ref[...] = mg
        x1_ref[...] = x_ref[...] + _dot(mg, wo_ref[...])

    row = lambda w: pl.BlockSpec((tm, w), lambda i: (i, 0))
    full = lambda *s: pl.BlockSpec(s, lambda i: (0,) * len(s))
    bf = lambda w: jax.ShapeDtypeStruct((n, w), BF16)
    return _call(
        body, name=name, grid=(n // tm,), carry=carry,
        in_specs=[row(D_IN),
                  pl.BlockSpec((HALO, D_IN), lambda i: (jnp.maximum(i * hb - 1, 0), 0)),
                  row(D_MODEL), full(1, 2 * D_MODEL), full(1, D_A), full(1, D_A),
                  full(N_HEADS, GMLP_BLOCK, GMLP_BLOCK), full(N_HEADS, GMLP_BLOCK, 1), full(3, D_B),
                  full(2, D_A, D_MODEL), full(D_MODEL, D_MODEL)],
        out_specs=[row(D_A), row(D_B), row(D_B), row(D_MODEL), row(D_MODEL), row(D_MODEL), row(D_MODEL)],
        out_shape=[bf(D_A), bf(D_B), bf(D_B), bf(D_MODEL), bf(D_MODEL), bf(D_MODEL),
                   jax.ShapeDtypeStruct((n, D_MODEL), F32)],
        scratch_shapes=[pltpu.VMEM((tm, D_A), BF16)],
        args=(z, z, x, b_gate.reshape(1, -1), ln_g.reshape(1, -1), ln_b.reshape(1, -1), w_s,
              b_s.reshape(N_HEADS, GMLP_BLOCK, 1), w_sc, wb, w_out))


def _loss_tile(xv, gv, tv):
    d = xv.shape[-1]
    r = lax.rsqrt(jnp.mean(xv * xv, axis=-1, keepdims=True) + RMS_EPS)
    xh = xv * r
    e = xh * gv - tv
    per_row = jnp.sum(e * e, axis=-1, keepdims=True) * (0.5 / d)
    dy = e * (1.0 / d)
    dxh = dy * gv
    dx = r * (dxh - xh * jnp.mean(dxh * xh, axis=-1, keepdims=True))
    return dx, jnp.sum(per_row, axis=0, keepdims=True), jnp.sum(dy * xh, axis=0, keepdims=True)


def _ffn_forward(up, x1, w_fc, b_fc, w_down, *, tm, name, carry=None, head=None):
    n = up.shape[0]
    hb = tm // HALO
    n_in = 6 if head is None else 8

    def body(*refs):
        up_ref, upp_ref, x1_ref, wfc_ref, bfc_ref, wd_ref = refs[:6]
        gc_ref, a_ref, out_ref = refs[n_in:n_in + 3]
        acc = refs[-1]
        i = pl.program_id(0)
        acc[...] = x1_ref[...]
        for c0 in range(0, D_FF, FFN_CHUNK):
            cols = slice(c0, c0 + FFN_CHUNK)
            gate = up_ref[:, cols]
            val = up_ref[:, D_FF + c0:D_FF + c0 + FFN_CHUNK]
            gp = upp_ref[:, cols]
            gp = jnp.where(i > 0, gp, jnp.zeros_like(gp))
            w = wfc_ref[:, cols].astype(BF16)
            gc = (w[0:1] * _shift_down(gate, 2, gp) + w[1:2] * _shift_down(gate, 1, gp) + w[2:3] * gate
                  + bfc_ref[:, cols].astype(BF16))
            gc_ref[:, cols] = gc
            a = gc * _sigmoid(gc) * val
            a_ref[:, cols] = a
            acc[...] += _dot(a, wd_ref[cols, :])
        if head is None:
            out_ref[...] = acc[...]
        else:
            g_ref, t_ref = refs[6:8]
            sg_ref = refs[n_in + 3]

            @pl.when(i == 0)
            def _():
                sg_ref[...] = jnp.zeros_like(sg_ref)

            dx, loss, dg = _loss_tile(acc[...], g_ref[...], t_ref[...])
            out_ref[...] = dx
            sg_ref[ROW_LOSS:ROW_LOSS + 1, 0:LANES] += jnp.broadcast_to(loss, (1, LANES))
            sg_ref[ROW_FINAL:ROW_FINAL + 1, 0:D_MODEL] += dg

    row = lambda w: pl.BlockSpec((tm, w), lambda i: (i, 0))
    full = lambda r, c: pl.BlockSpec((r, c), lambda i: (0, 0))
    in_specs = [row(2 * D_FF), pl.BlockSpec((HALO, D_FF), lambda i: (jnp.maximum(i * hb - 1, 0), 0)), row(D_MODEL),
                full(3, D_FF), full(1, D_FF), full(D_FF, D_MODEL)]
    out_specs = [row(D_FF), row(D_FF), row(D_MODEL)]
    out_shape = [jax.ShapeDtypeStruct((n, D_FF), BF16), jax.ShapeDtypeStruct((n, D_FF), BF16),
                 jax.ShapeDtypeStruct((n, D_MODEL), F32)]
    args = (up, up, x1, w_fc, b_fc.reshape(1, -1), w_down)
    if head is not None:
        in_specs += [full(1, D_MODEL), row(D_MODEL)]
        out_specs += [full(SG_ROWS, SG_W)]
        out_shape += [jax.ShapeDtypeStruct((SG_ROWS, SG_W), F32)]
        args += (head[0].reshape(1, -1), head[1])
    return _call(body, name=name, grid=(n // tm,), carry=carry, in_specs=in_specs, out_specs=out_specs,
                 out_shape=out_shape, scratch_shapes=[pltpu.VMEM((tm, D_MODEL), F32)], args=args)


def _ffn_backward(dx2, up, gc, w_fc, w_down, sheet, layer, *, tm, name, carry=None):
    n = up.shape[0]
    steps = n // tm
    hb = tm // HALO
    row = SG_LAYER * layer + ROW_FCONV

    def body(dx_ref, dxn_ref, up_ref, upn_ref, gc_ref, gcn_ref, wfc_ref, wd_ref, sg_in, dup_ref, sg_ref):
        i = pl.program_id(0)
        last = i == steps - 1
        _sheet_begin(i, sg_in, sg_ref, row, 4)

        dxe = jnp.concatenate([dx_ref[...], dxn_ref[...]], axis=0).astype(BF16)
        for c0 in range(0, D_FF, FFN_CHUNK):
            cols = slice(c0, c0 + FFN_CHUNK)
            vcols = slice(D_FF + c0, D_FF + c0 + FFN_CHUNK)
            dae = _dot_nt(dxe, wd_ref[cols, :])
            da, dan = dae[:tm], dae[tm:]
            gate = up_ref[:, cols]
            val = up_ref[:, vcols]
            gcv = gc_ref[:, cols]
            s = _sigmoid(gcv)
            dab = da.astype(BF16)
            dup_ref[:, vcols] = dab * (gcv * s)
            dgc = dab * val * (s * (1.0 + gcv * (1.0 - s)))
            gcn = gcn_ref[:, cols]
            sn = _sigmoid(gcn)
            dgcn = dan.astype(BF16) * upn_ref[:, vcols] * (sn * (1.0 + gcn * (1.0 - sn)))
            dgcn = jnp.where(last, jnp.zeros_like(dgcn), dgcn)
            up1 = _shift_up(dgc, 1, dgcn)
            up2 = _shift_up(dgc, 2, dgcn)
            w = wfc_ref[:, cols].astype(BF16)
            dup_ref[:, cols] = w[2:3] * dgc + w[1:2] * up1 + w[0:1] * up2
            sg_ref[row:row + 1, cols] += _column_sums(gate * up2)
            sg_ref[row + 1:row + 2, cols] += _column_sums(gate * up1)
            sg_ref[row + 2:row + 3, cols] += _column_sums(gate * dgc)
            sg_ref[row + 3:row + 4, cols] += _column_sums(dgc)

    nxt = lambda i: (jnp.minimum((i + 1) * hb, steps * hb - 1), 0)
    return _call(
        body, name=name, grid=(steps,), carry=carry,
        in_specs=[pl.BlockSpec((tm, D_MODEL), lambda i: (i, 0)),
                  pl.BlockSpec((HALO, D_MODEL), nxt),
                  pl.BlockSpec((tm, 2 * D_FF), lambda i: (i, 0)),
                  pl.BlockSpec((HALO, 2 * D_FF), nxt),
                  pl.BlockSpec((tm, D_FF), lambda i: (i, 0)),
                  pl.BlockSpec((HALO, D_FF), nxt),
                  pl.BlockSpec((3, D_FF), lambda i: (0, 0)),
                  pl.BlockSpec((D_FF, D_MODEL), lambda i: (0, 0)), _sheet_spec()],
        out_specs=[pl.BlockSpec((tm, 2 * D_FF), lambda i: (i, 0)), _sheet_spec()],
        out_shape=[jax.ShapeDtypeStruct((n, 2 * D_FF), BF16), jax.ShapeDtypeStruct((SG_ROWS, SG_W), F32)],
        args=(dx2, dx2, up, up, gc, gc, w_fc, w_down, sheet))


def _matmul_norm_backward(dz, w_t, x, g, dres, sheet, row, *, tm, name, carry=None):
    n, c = dz.shape
    d = x.shape[1]
    ch = MATMUL_CHUNK

    def body(dz_ref, wt_ref, x_ref, g_ref, dres_ref, *rest):
        i = pl.program_id(0)
        if sheet is None:
            dx_ref, sg_ref = rest

            @pl.when(i == 0)
            def _():
                sg_ref[...] = jnp.zeros_like(sg_ref)
        else:
            sg_in, dx_ref, sg_ref = rest
            _sheet_begin(i, sg_in, sg_ref, row, 1)

        dh = _dot(dz_ref[:, 0:ch], wt_ref[0:ch, :])
        for c0 in range(ch, c, ch):
            dh += _dot(dz_ref[:, c0:c0 + ch], wt_ref[c0:c0 + ch, :])
        xv = x_ref[...]
        r = lax.rsqrt(jnp.mean(xv * xv, axis=-1, keepdims=True) + RMS_EPS)
        xh = xv * r
        sg_ref[row:row + 1, 0:d] += jnp.sum(dh * xh, axis=0, keepdims=True)
        dxh = dh * g_ref[...]
        dx_ref[...] = dres_ref[...] + r * (dxh - xh * jnp.mean(dxh * xh, axis=-1, keepdims=True))

    in_specs = [pl.BlockSpec((tm, c), lambda i: (i, 0)),
                pl.BlockSpec((c, d), lambda i: (0, 0)),
                pl.BlockSpec((tm, d), lambda i: (i, 0)),
                pl.BlockSpec((1, d), lambda i: (0, 0)),
                pl.BlockSpec((tm, d), lambda i: (i, 0))]
    args = (dz, w_t, x, g.reshape(1, d), dres)
    if sheet is None:
        small_spec, small_shape = pl.BlockSpec((8, d), lambda i: (0, 0)), jax.ShapeDtypeStruct((8, d), F32)
    else:
        in_specs, args = in_specs + [_sheet_spec()], args + (sheet,)
        small_spec, small_shape = _sheet_spec(), jax.ShapeDtypeStruct((SG_ROWS, SG_W), F32)
    return _call(
        body, name=name, grid=(n // tm,), carry=carry, in_specs=in_specs,
        out_specs=[pl.BlockSpec((tm, d), lambda i: (i, 0)), small_spec],
        out_shape=[jax.ShapeDtypeStruct((n, d), F32), small_shape], args=args)


def _mix_backward(dx1, z, conv, pa, pb, b_gate, ln_g, ln_b, w_s, w_s_t, b_s, w_sc, w_out, wb, sheet, layer, *, tm, name,
                  carry=None):
    n = z.shape[0]
    steps = n // tm
    hb = tm // HALO
    base = SG_LAYER * layer
    r_bg, r_lng, r_lnb, r_sc = base + ROW_BGATE, base + ROW_LN_G, base + ROW_LN_B, base + ROW_SCONV

    def body(dx_ref, dxn_ref, z_ref, zn_ref, cv_ref, pa_ref, pb_ref, bg_ref, lng_ref, lnb_ref, ws_ref, wst_ref,
             bs_ref, wsc_ref, wo_ref, wb_ref, sg_in,
             dz_ref, dpa_ref, dpb_ref, dws_ref, dbs_ref, sg_ref, f_scr, dvn_scr):
        i = pl.program_id(0)
        last = i == steps - 1
        _sheet_begin(i, sg_in, sg_ref, r_bg, ROW_NORM2 - ROW_BGATE)

        @pl.when(i == 0)
        def _():
            dws_ref[...] = jnp.zeros_like(dws_ref)
            dbs_ref[...] = jnp.zeros_like(dbs_ref)

        dxe = jnp.concatenate([dx_ref[...], dxn_ref[...]], axis=0).astype(BF16)
        dmge = _dot_nt(dxe, wo_ref[...])
        dmg, dmgn = dmge[:tm].astype(BF16), dmge[tm:].astype(BF16)

        pa_v = pa_ref[...]
        pb_v = pb_ref[...]
        bg = bg_ref[...].astype(BF16)
        sa = _sigmoid(z_ref[:, OFF_GA:OFF_GA + D_MODEL] + bg[:, 0:D_MODEL])
        sb = _sigmoid(z_ref[:, OFF_GB:OFF_GB + D_MODEL] + bg[:, D_MODEL:2 * D_MODEL])
        dpa = dmg * sa
        dpb = dmg * sb
        dga = dmg * pa_v * sa * (1.0 - sa)
        dgb = dmg * pb_v * sb * (1.0 - sb)
        dpa_ref[...] = dpa
        dpb_ref[...] = dpb
        dz_ref[:, OFF_GA:OFF_GA + D_MODEL] = dga
        dz_ref[:, OFF_GB:OFF_GB + D_MODEL] = dgb
        sg_ref[r_bg:r_bg + 1, 0:D_MODEL] += _column_sums(dga)
        sg_ref[r_bg:r_bg + 1, D_MODEL:2 * D_MODEL] += _column_sums(dgb)

        dya = _dot_nt(dpa, wb_ref[0]).astype(BF16)
        u = z_ref[:, OFF_U:OFF_U + D_A]
        v = z_ref[:, OFF_V:OFF_V + D_A].astype(F32)
        ln_g = lng_ref[...]
        gu, tu, tv, xh, rstd, vn, f = _gmlp_forward(u, v, ln_g, lnb_ref[...], ws_ref, bs_ref, f_scr)
        dgu = dya * f
        df_bf = dya * gu
        dz_ref[:, OFF_U:OFF_U + D_A] = dgu * _gelu_grad(u, tu)
        mask = _spatial_mask(False)
        mask_t = _spatial_mask(True)
        wmt = [jnp.where(mask_t, wst_ref[h], 0.0).astype(BF16) for h in range(N_HEADS)]
        for b in range(tm // GMLP_BLOCK):
            rows = slice(b * GMLP_BLOCK, (b + 1) * GMLP_BLOCK)
            for h in range(N_HEADS):
                cols = slice(h * HEAD, (h + 1) * HEAD)
                dfb = df_bf[rows, cols]
                dvn_scr[rows, cols] = _dot(wmt[h], dfb)
                dws_ref[h] += jnp.where(mask, _dot_nt(dfb, vn[rows, cols]), 0.0)
                dbs_ref[h] += jnp.sum(dfb.astype(F32), axis=1, keepdims=True)
        dvn = dvn_scr[...]
        sg_ref[r_lng:r_lng + 1, 0:D_A] += jnp.sum(dvn * xh, axis=0, keepdims=True)
        sg_ref[r_lnb:r_lnb + 1, 0:D_A] += jnp.sum(dvn, axis=0, keepdims=True)
        dxh = dvn * ln_g
        dgv = rstd * (dxh - jnp.mean(dxh, axis=-1, keepdims=True) - xh * jnp.mean(dxh * xh, axis=-1, keepdims=True))
        dz_ref[:, OFF_V:OFF_V + D_A] = (dgv * _gelu_grad(v, tv)).astype(BF16)

        sbn = _sigmoid(zn_ref[:, OFF_GB:OFF_GB + D_MODEL] + bg[:, D_MODEL:2 * D_MODEL])
        dpbe = jnp.concatenate([dpb, dmgn * sbn], axis=0)
        dybe = _dot_nt(dpbe, wb_ref[1])
        dyb, dybn = dybe[:tm].astype(BF16), dybe[tm:].astype(BF16)
        bgv = z_ref[:, OFF_BG:OFF_BG + D_B]
        cg = z_ref[:, OFF_CG:OFF_CG + D_B]
        hbv = z_ref[:, OFF_HB:OFF_HB + D_B]
        q = cg * hbv
        dz_ref[:, OFF_BG:OFF_BG + D_B] = dyb * cv_ref[...]
        dconv = dyb * bgv
        dconvn = dybn * zn_ref[:, OFF_BG:OFF_BG + D_B]
        dconvn = jnp.where(last, jnp.zeros_like(dconvn), dconvn)
        up1 = _shift_up(dconv, 1, dconvn)
        up2 = _shift_up(dconv, 2, dconvn)
        sg_ref[r_sc:r_sc + 1, 0:D_B] += _column_sums(q * up2)
        sg_ref[r_sc + 1:r_sc + 2, 0:D_B] += _column_sums(q * up1)
        sg_ref[r_sc + 2:r_sc + 3, 0:D_B] += _column_sums(q * dconv)
        w = wsc_ref[...].astype(BF16)
        dq = w[2:3] * dconv + w[1:2] * up1 + w[0:1] * up2
        dz_ref[:, OFF_CG:OFF_CG + D_B] = dq * hbv
        dz_ref[:, OFF_HB:OFF_HB + D_B] = dq * cg

    row = lambda w: pl.BlockSpec((tm, w), lambda i: (i, 0))
    full = lambda *s: pl.BlockSpec(s, lambda i: (0,) * len(s))
    nxt = lambda i: (jnp.minimum((i + 1) * hb, steps * hb - 1), 0)
    return _call(
        body, name=name, grid=(steps,), carry=carry,
        in_specs=[row(D_MODEL), pl.BlockSpec((HALO, D_MODEL), nxt),
                  row(D_IN), pl.BlockSpec((HALO, D_IN), nxt),
                  row(D_B), row(D_MODEL), row(D_MODEL),
                  full(1, 2 * D_MODEL), full(1, D_A), full(1, D_A),
                  full(N_HEADS, GMLP_BLOCK, GMLP_BLOCK), full(N_HEADS, GMLP_BLOCK, GMLP_BLOCK),
                  full(N_HEADS, GMLP_BLOCK, 1), full(3, D_B),
                  full(D_MODEL, D_MODEL), full(2, D_A, D_MODEL), _sheet_spec()],
        out_specs=[row(D_IN), row(D_MODEL), row(D_MODEL), full(N_HEADS, GMLP_BLOCK, GMLP_BLOCK),
                   full(N_HEADS, GMLP_BLOCK, 1), _sheet_spec()],
        out_shape=[jax.ShapeDtypeStruct((n, D_IN), BF16), jax.ShapeDtypeStruct((n, D_MODEL), BF16),
                   jax.ShapeDtypeStruct((n, D_MODEL), BF16),
                   jax.ShapeDtypeStruct((N_HEADS, GMLP_BLOCK, GMLP_BLOCK), F32),
                   jax.ShapeDtypeStruct((N_HEADS, GMLP_BLOCK, 1), F32), jax.ShapeDtypeStruct((SG_ROWS, SG_W), F32)],
        scratch_shapes=[pltpu.VMEM((tm, D_A), BF16), pltpu.VMEM((tm, D_A), F32)],
        args=(dx1, dx1, z, z, conv, pa, pb, b_gate.reshape(1, -1), ln_g.reshape(1, -1), ln_b.reshape(1, -1), w_s, w_s_t,
              b_s.reshape(N_HEADS, GMLP_BLOCK, 1), w_sc, w_out, wb, sheet))


def _matmul_tn(a, b, *, t1, tn, name, carry=None, pieces=1):
    n, k1 = a.shape
    k2 = b.shape[1]
    steps = n // tn
    w = k2 // pieces

    def body(a_ref, b_ref, *rest):
        o_refs, acc = rest[:pieces], rest[pieces]
        s = pl.program_id(1)

        @pl.when(s == 0)
        def _():
            acc[...] = jnp.zeros_like(acc)

        acc[...] += lax.dot_general(a_ref[...].astype(BF16), b_ref[...].astype(BF16), TN, preferred_element_type=F32)

        @pl.when(s == steps - 1)
        def _():
            for c, o_ref in enumerate(o_refs):
                o_ref[...] = acc[:, c * w:(c + 1) * w].astype(BF16)

    outs, carried = _call(
        body, name=name, grid=(k1 // t1, steps), carry=carry,
        in_specs=[pl.BlockSpec((tn, t1), lambda i, s: (s, i)),
                  pl.BlockSpec((tn, k2), lambda i, s: (s, 0))],
        out_specs=[pl.BlockSpec((t1, w), lambda i, s: (i, 0))] * pieces,
        out_shape=[jax.ShapeDtypeStruct((k1, w), BF16)] * pieces,
        scratch_shapes=[pltpu.VMEM((t1, k2), F32)],
        args=(a, b))
    return (outs[0] if pieces == 1 else list(outs)), carried


def _adamw_math(w, g, m, v):
    m = ADAM_B1 * m + (1.0 - ADAM_B1) * g
    v = ADAM_B2 * v + (1.0 - ADAM_B2) * (g * g)
    m_hat = m / (1.0 - ADAM_B1 ** ADAM_STEP)
    v_hat = v / (1.0 - ADAM_B2 ** ADAM_STEP)
    delta = -ADAM_LR * (m_hat / (jnp.sqrt(v_hat) + ADAM_EPS) + ADAM_WD * w)
    return delta, m, v


def _sum_parts(recvs, *, tr, name):
    _, r, c = recvs[0].shape

    def body(*refs):
        recv_refs, g_ref = refs[:DEPTH], refs[DEPTH]
        layer = pl.program_id(0)
        for l in range(DEPTH):
            @pl.when(layer == l)
            def _(l=l):
                g = recv_refs[l][0].astype(F32)
                for s in range(1, N_DEV):
                    g = g + recv_refs[l][s].astype(F32)
                g_ref[0] = g

    outs, _ = _call(
        body, name=name, grid=(DEPTH, r // tr),
        in_specs=[pl.BlockSpec((N_DEV, tr, c), lambda l, i: (0, i, 0))] * DEPTH,
        out_specs=[pl.BlockSpec((1, tr, c), lambda l, i: (l, i, 0))],
        out_shape=[jax.ShapeDtypeStruct((DEPTH, r, c), F32)],
        args=tuple(recvs))
    return outs[0]


def _adamw(w, g, m, v, *, tr, name):
    r, c = w.shape

    def body(w_ref, g_ref, m_ref, v_ref, d_ref, nm_ref, nv_ref):
        delta, nm, nv = _adamw_math(w_ref[...], g_ref[...], m_ref[...], v_ref[...])
        d_ref[...] = delta
        nm_ref[...] = nm
        nv_ref[...] = nv

    spec = pl.BlockSpec((tr, c), lambda i: (i, 0))
    outs, _ = _call(body, name=name, grid=(r // tr,), in_specs=[spec] * 4, out_specs=[spec] * 3,
                    out_shape=[jax.ShapeDtypeStruct((r, c), F32)] * 3, args=(w, g, m, v))
    return outs


def _sum_adamw(recvs, w, m, v, *, tr, name):
    _, r, c = w.shape
    blocks = len(recvs[0])
    flat = [piece for layer in recvs for piece in layer]

    def body(*refs):
        recv_refs = refs[:len(flat)]
        w_ref, m_ref, v_ref, g_ref, d_ref, nm_ref, nv_ref = refs[len(flat):]
        layer = pl.program_id(0)
        for l in range(DEPTH):
            @pl.when(layer == l)
            def _(l=l):
                cols = []
                for piece in recv_refs[l * blocks:(l + 1) * blocks]:
                    part = piece[0].astype(F32)
                    for s in range(1, N_DEV):
                        part = part + piece[s].astype(F32)
                    cols.append(part)
                g = cols[0] if blocks == 1 else jnp.concatenate(cols, axis=-1)
                delta, nm, nv = _adamw_math(w_ref[0], g, m_ref[0], v_ref[0])
                g_ref[0] = g
                d_ref[0] = delta
                nm_ref[0] = nm
                nv_ref[0] = nv

    spec = pl.BlockSpec((1, tr, c), lambda l, i: (l, i, 0))
    outs, _ = _call(
        body, name=name, grid=(DEPTH, r // tr),
        in_specs=[pl.BlockSpec((N_DEV, tr, c // blocks), lambda l, i: (0, i, 0))] * len(flat) + [spec] * 3,
        out_specs=[spec] * 4, out_shape=[jax.ShapeDtypeStruct((DEPTH, r, c), F32)] * 4,
        args=tuple(flat) + (w, m, v))
    return outs


def _adamw_small(sheet, extra, params, *, name):
    sheet_rows = dict(norm1_g=ROW_NORM1, b_gate=ROW_BGATE, gmlp_ln_g=ROW_LN_G, gmlp_ln_b=ROW_LN_B, norm2_g=ROW_NORM2,
                      b_ffn_conv=ROW_BFCONV)
    names = list(params)
    extra_names = list(extra)

    def body(*refs):
        sg_ref, refs = refs[0], refs[1:]
        extra_refs, refs = dict(zip(extra_names, refs[:len(extra_names)])), refs[len(extra_names):]
        ins, outs = refs[:3 * len(names)], refs[3 * len(names):]
        for j, key in enumerate(names):
            w_ref, m_ref, v_ref = ins[3 * j:3 * j + 3]
            g_ref, d_ref, nm_ref, nv_ref = outs[4 * j:4 * j + 4]
            if key in extra_refs:
                g_ref[...] = extra_refs[key][...]
            elif key == "final_g":
                g_ref[...] = sg_ref[ROW_FINAL:ROW_FINAL + 1, 0:D_MODEL]
            else:
                width = w_ref.shape[-1]
                for l in range(DEPTH):
                    row = SG_LAYER * l + sheet_rows[key]
                    g_ref[l:l + 1, :] = sg_ref[row:row + 1, 0:width]
            delta, nm, nv = _adamw_math(w_ref[...], g_ref[...], m_ref[...], v_ref[...])
            d_ref[...] = delta
            nm_ref[...] = nm
            nv_ref[...] = nv

    args = [sheet] + [extra[k] for k in extra_names] + [t for k in names for t in params[k]]
    vmem = pl.BlockSpec(memory_space=pltpu.VMEM)
    outs = pl.pallas_call(
        body, name=name, in_specs=[vmem] * len(args), out_specs=[vmem] * (4 * len(names)),
        out_shape=[jax.ShapeDtypeStruct(params[k][0].shape, F32) for k in names for _ in range(4)],
    )(*args)
    return {k: tuple(outs[4 * j:4 * j + 4]) for j, k in enumerate(names)}


def _rows(gathered):
    return gathered.reshape(N_DEV * gathered.shape[1], gathered.shape[2])


def _parts(full):
    return full.reshape(N_DEV, full.shape[0] // N_DEV, full.shape[1])


def kernel(x, norm1_g, w_in, b_gate, gmlp_ln_g, gmlp_ln_b, w_spatial, b_spatial, w_shortconv, w_branch, w_out, norm2_g, w_ffn_up, w_ffn_conv, b_ffn_conv, w_ffn_down, final_g, loss_target, m_norm1_g, m_w_in, m_b_gate, m_gmlp_ln_g, m_gmlp_ln_b, m_w_spatial, m_b_spatial, m_w_shortconv, m_w_branch, m_w_out, m_norm2_g, m_w_ffn_up, m_w_ffn_conv, m_b_ffn_conv, m_w_ffn_down, m_final_g, v_norm1_g, v_w_in, v_b_gate, v_gmlp_ln_g, v_gmlp_ln_b, v_w_spatial, v_b_spatial, v_w_shortconv, v_w_branch, v_w_out, v_norm2_g, v_w_ffn_up, v_w_ffn_conv, v_b_ffn_conv, v_w_ffn_down, v_final_g):
    n = x.shape[1]
    tm_in, tm, tn = 1024, 512, 2048
    x0 = x.reshape(n, D_MODEL)
    target = loss_target.reshape(n, D_MODEL)
    my_idx = 4 * lax.axis_index("x") + 2 * lax.axis_index("y") + lax.axis_index("c")
    sc_w, fc_w = D_B // N_DEV, D_FF // N_DEV

    sh_in = [w_in[l].T.astype(BF16) for l in range(DEPTH)]
    sh_up = [w_ffn_up[l].T.astype(BF16) for l in range(DEPTH)]
    sh_br = [w_branch[l].astype(BF16) for l in range(DEPTH)]
    sh_out = [w_out[l].astype(BF16) for l in range(DEPTH)]
    sh_down = [w_ffn_down[l].astype(BF16) for l in range(DEPTH)]
    taps = jnp.concatenate([w_shortconv, w_ffn_conv], axis=-1)

    def branch_weights(g):
        return g.transpose(1, 2, 0, 3).reshape(2, D_A, D_MODEL)

    g_in0, g_taps = _gather_now([sh_in[0], taps], name="gather_first")
    w_sc = [g_taps[:, l, :, :sc_w].transpose(1, 0, 2).reshape(3, D_B) for l in range(DEPTH)]
    w_fc = [g_taps[:, l, :, sc_w:].transpose(1, 0, 2).reshape(3, D_FF) for l in range(DEPTH)]
    w_s_t = [w_spatial[l].transpose(0, 2, 1) for l in range(DEPTH)]
    weights = [dict(), dict()]
    weights[0]["in_t"] = _rows(g_in0)
    saved = []
    xc = x0
    for l in range(DEPTH):
        p = weights[l]
        carry = _Gather([sh_br[0], sh_out[0]], relay_early=True) if l == 0 else _Gather([sh_up[1]])
        (h, z), got = _norm_matmul(xc, norm1_g[l], p["in_t"], tm=tm_in, name=f"fwd_in_{l}", carry=carry)
        if l == 0:
            p["wb"], p["out"] = branch_weights(got[0]), _rows(got[1])
        else:
            p["up_t"] = _rows(got[0])
        carry = _Gather([sh_up[0]]) if l == 0 else None
        (ya, yb, conv, pa, pb, mg, x1), got = _mix_forward(
            z, xc, b_gate[l], gmlp_ln_g[l], gmlp_ln_b[l], w_spatial[l], b_spatial[l], w_sc[l], p["wb"], p["out"],
            tm=tm, name=f"fwd_mix_{l}", carry=carry)
        if l == 0:
            p["up_t"] = _rows(got[0])
        (h2, up), got = _norm_matmul(x1, norm2_g[l], p["up_t"], tm=tm, name=f"fwd_up_{l}", carry=_Gather([sh_down[l]], relay_early=True))
        p["down"] = _rows(got[0])
        carry = _Gather([sh_br[1], sh_out[1], sh_in[1]]) if l == 0 else None
        head = (final_g, target) if l == DEPTH - 1 else None
        outs, got = _ffn_forward(up, x1, w_fc[l], b_ffn_conv[l], p["down"], tm=tm, name=f"fwd_ffn_{l}", carry=carry, head=head)
        if l == 0:
            weights[1]["wb"], weights[1]["out"], weights[1]["in_t"] = branch_weights(got[0]), _rows(got[1]), _rows(got[2])
        gc, a = outs[0], outs[1]
        saved.append(dict(x=xc, h=h, z=z, ya=ya, yb=yb, conv=conv, pa=pa, pb=pb, mg=mg, x1=x1, h2=h2, up=up, gc=gc, a=a))
        xc = outs[2]
    dx, sheet = outs[2], outs[3]

    recv = [dict(), dict()]
    small_dws, small_dbs = [None] * DEPTH, [None] * DEPTH
    pending_in = None
    for l in reversed(range(DEPTH)):
        p, s = weights[l], saved[l]
        carry = _Exchange([pending_in]) if pending_in is not None else None
        (dup, sheet), got = _ffn_backward(dx, s["up"], s["gc"], w_fc[l], p["down"], sheet, l, tm=tm, name=f"bwd_ffn_{l}",
                                          carry=carry)
        if got is not None:
            recv[l + 1]["in_t"] = got[0]
        dw_down, _ = _matmul_tn(s["a"], dx, t1=D_FF // 2, tn=tn, name=f"dw_down_{l}")
        dw_up_t, got = _matmul_tn(dup, s["h2"], t1=2 * D_FF // 4, tn=tn, name=f"dw_up_{l}", pieces=2,
                                  carry=_Exchange([_parts(dw_down)]))
        recv[l]["down"] = got[0]
        (dx1, sheet), got_left = _matmul_norm_backward(
            dup, p["up_t"], s["x1"], norm2_g[l], dx, sheet, SG_LAYER * l + ROW_NORM2, tm=tm, name=f"bwd_up_{l}",
            carry=_Exchange([_parts(dw_up_t[0])]))
        dw_out, _ = _matmul_tn(s["mg"], dx1, t1=D_MODEL, tn=tn, name=f"dw_out_{l}")
        (dz, dpa, dpb, small_dws[l], small_dbs[l], sheet), got_right = _mix_backward(
            dx1, s["z"], s["conv"], s["pa"], s["pb"], b_gate[l], gmlp_ln_g[l], gmlp_ln_b[l], w_spatial[l], w_s_t[l],
            b_spatial[l], w_sc[l], p["out"], p["wb"], sheet, l, tm=tm, name=f"bwd_mix_{l}",
            carry=_Exchange([_parts(dw_up_t[1]), _parts(dw_out)]))
        recv[l]["up_t"], recv[l]["out"] = [got_left[0], got_right[0]], got_right[1]
        dw_bra_t, _ = _matmul_tn(dpa, s["ya"], t1=D_MODEL, tn=tn, name=f"dw_branch_a_{l}")
        dw_brb_t, _ = _matmul_tn(dpb, s["yb"], t1=D_MODEL, tn=tn, name=f"dw_branch_b_{l}")
        carry = _Exchange([_parts(dw_bra_t), _parts(dw_brb_t)])
        if l == 0:
            dbs = jnp.stack([t.reshape(N_HEADS, GMLP_BLOCK) for t in small_dbs]).reshape(DEPTH * N_HEADS, GMLP_BLOCK)
            carry = _Both(carry, _Gather([sheet, small_dws[0], small_dws[1], dbs]))
        dw_in_t, got = _matmul_tn(dz, s["h"], t1=D_IN // 4, tn=tn, name=f"dw_in_{l}", carry=carry)
        recv[l]["bra_t"], recv[l]["brb_t"] = got[:2]
        if l == 0:
            gathered_small = got[2:]
            (dx0, dg1_first), got = _matmul_norm_backward(dz, p["in_t"], s["x"], norm1_g[l], dx1, None, 0, tm=tm,
                                                         name=f"bwd_in_{l}", carry=_Exchange([_parts(dw_in_t)]))
            recv[0]["in_t"] = got[0]
        else:
            (dx0, sheet), _ = _matmul_norm_backward(dz, p["in_t"], s["x"], norm1_g[l], dx1, sheet,
                                                    SG_LAYER * l + ROW_NORM1, tm=tm, name=f"bwd_in_{l}")
            pending_in = _parts(dw_in_t)
        dx = dx0
    grad_x = dx.reshape(x.shape)

    results = {}
    both = lambda key: [recv[l][key] for l in range(DEPTH)]
    blocks = lambda key: [r if isinstance(r, list) else [r] for r in both(key)]
    swap = lambda t: t.transpose(0, 2, 1)
    for key, slab, (w, m, v), tr in [("w_in", "in_t", (w_in, m_w_in, v_w_in), 192),
                                     ("w_ffn_up", "up_t", (w_ffn_up, m_w_ffn_up, v_w_ffn_up), 176)]:
        outs = _sum_adamw(blocks(slab), swap(w), swap(m), swap(v), tr=tr, name=f"adamw_{key}")
        results[key] = tuple(swap(o) for o in outs)
    g_bra = _sum_parts(both("bra_t"), tr=128, name="sum_w_branch_a").transpose(0, 2, 1)
    g_brb = _sum_parts(both("brb_t"), tr=128, name="sum_w_branch_b").transpose(0, 2, 1)
    g_br = jnp.stack([g_bra, g_brb], axis=1)
    flat = lambda t: t.reshape(-1, t.shape[-1])
    outs = _adamw(flat(w_branch), flat(g_br), flat(m_w_branch), flat(v_w_branch), tr=512, name="adamw_w_branch")
    results["w_branch"] = (g_br,) + tuple(o.reshape(w_branch.shape) for o in outs)
    results["w_out"] = tuple(_sum_adamw(blocks("out"), w_out, m_w_out, v_w_out, tr=128, name="adamw_w_out"))
    results["w_ffn_down"] = tuple(_sum_adamw(blocks("down"), w_ffn_down, m_w_ffn_down, v_w_ffn_down, tr=176,
                                             name="adamw_w_ffn_down"))

    sheet, dws0, dws1, dbs = _sum_gathered(gathered_small, name="sum_small_grads")
    (dg1_first,) = _all_reduce_small([dg1_first], name="all_reduce_last_gain")
    sheet = sheet.at[ROW_NORM1, :D_MODEL].set(dg1_first[0])
    loss = sheet[ROW_LOSS, 0]
    taps = lambda row, width: jnp.stack([sheet[SG_LAYER * l + row:SG_LAYER * l + row + 3, :width] for l in range(DEPTH)])
    extra = dict(w_spatial=jnp.stack([dws0, dws1]), b_spatial=dbs.reshape(DEPTH, N_HEADS, GMLP_BLOCK),
                 w_shortconv=lax.dynamic_slice_in_dim(taps(ROW_SCONV, D_B), my_idx * sc_w, sc_w, axis=2),
                 w_ffn_conv=lax.dynamic_slice_in_dim(taps(ROW_FCONV, D_FF), my_idx * fc_w, fc_w, axis=2))
    small_w = dict(norm1_g=(norm1_g, m_norm1_g, v_norm1_g), b_gate=(b_gate, m_b_gate, v_b_gate),
                   gmlp_ln_g=(gmlp_ln_g, m_gmlp_ln_g, v_gmlp_ln_g), gmlp_ln_b=(gmlp_ln_b, m_gmlp_ln_b, v_gmlp_ln_b),
                   w_spatial=(w_spatial, m_w_spatial, v_w_spatial), b_spatial=(b_spatial, m_b_spatial, v_b_spatial),
                   w_shortconv=(w_shortconv, m_w_shortconv, v_w_shortconv), norm2_g=(norm2_g, m_norm2_g, v_norm2_g),
                   w_ffn_conv=(w_ffn_conv, m_w_ffn_conv, v_w_ffn_conv), b_ffn_conv=(b_ffn_conv, m_b_ffn_conv, v_b_ffn_conv),
                   final_g=tuple(t.reshape(1, D_MODEL) for t in (final_g, m_final_g, v_final_g)))
    results.update(_adamw_small(sheet, extra, small_w, name="adamw_small"))
    results["final_g"] = tuple(t.reshape(D_MODEL) for t in results["final_g"])

    names = ["norm1_g", "w_in", "b_gate", "gmlp_ln_g", "gmlp_ln_b", "w_spatial", "b_spatial", "w_shortconv", "w_branch",
             "w_out", "norm2_g", "w_ffn_up", "w_ffn_conv", "b_ffn_conv", "w_ffn_down", "final_g"]
    return (loss, grad_x, *[results[k][0] for k in names], *[results[k][1] for k in names],
            *[results[k][2] for k in names], *[results[k][3] for k in names])
```

```python
import math

import jax
import jax.numpy as jnp
from jax import lax
from jax.experimental import pallas as pl
from jax.experimental.pallas import tpu as pltpu

F32 = jnp.float32
BF16 = jnp.bfloat16

N_DEV = 8
DEPTH = 2
D_MODEL = 1024
D_A = 512
D_B = 512
D_FF = 2816
D_IN = 4608
N_HEADS = 4
HEAD = 128
GMLP_BLOCK = 128
CAUSAL_CHUNK = 64
OFF_U, OFF_V, OFF_BG, OFF_CG, OFF_HB, OFF_GA, OFF_GB = 0, 512, 1024, 1536, 2048, 2560, 3584
RMS_EPS = 1e-6
LN_EPS = 1e-5
ADAM_LR, ADAM_B1, ADAM_B2, ADAM_EPS, ADAM_WD, ADAM_STEP = 0.001, 0.9, 0.999, 1e-08, 0.01, 10

SUBLANES, LANES = 8, 128
MATMUL_CHUNK = 512
HALO = 16
FFN_CHUNK = 256
SG_ROWS, SG_W, SG_LAYER = 40, D_FF, 16
ROW_NORM1, ROW_BGATE, ROW_LN_G, ROW_LN_B, ROW_SCONV, ROW_NORM2, ROW_FCONV, ROW_BFCONV = 0, 1, 2, 3, 4, 7, 8, 11
ROW_FINAL, ROW_LOSS = 32, 33
V7X_VMEM_BYTES = 64 << 20
VMEM_LIMIT = V7X_VMEM_BYTES - (8 << 20)
MESH = pl.DeviceIdType.MESH
GELU_C0 = 0.7978845608028654
GELU_C1 = 0.044715
NT = (((1,), (1,)), ((), ()))
TN = (((0,), (0,)), ((), ()))


def _dot(a, b):
    return jnp.dot(a, b, preferred_element_type=F32)


def _dot_nt(a, b):
    return lax.dot_general(a, b, NT, preferred_element_type=F32)


def _sigmoid(x):
    return 1.0 / (1.0 + jnp.exp(-x))


def _gelu_tanh(x):
    return jnp.tanh(GELU_C0 * (x + GELU_C1 * x * x * x))


def _gelu_grad(x, t):
    return 0.5 * (1.0 + t) + 0.5 * x * (1.0 - t * t) * GELU_C0 * (1.0 + 3.0 * GELU_C1 * x * x)


def _sublane_tile(dtype):
    return SUBLANES * (4 // jnp.dtype(dtype).itemsize)


def _shift_down(a, k, prev):
    p = prev.shape[0]
    r = pltpu.roll(a, k, 0)
    sub = _sublane_tile(a.dtype)
    head = r[0:sub]
    rid = lax.broadcasted_iota(jnp.int32, head.shape, 0)
    for j in range(k):
        head = jnp.where(rid == j, prev[p - k + j:p - k + j + 1, :], head)
    return jnp.concatenate([head, r[sub:]], axis=0)


def _shift_up(a, k, nxt):
    t = a.shape[0]
    r = pltpu.roll(a, t - k, 0)
    sub = _sublane_tile(a.dtype)
    tail = r[t - sub:t]
    rid = lax.broadcasted_iota(jnp.int32, tail.shape, 0)
    for j in range(k):
        tail = jnp.where(rid == sub - k + j, nxt[j:j + 1, :], tail)
    return jnp.concatenate([r[0:t - sub], tail], axis=0)


def _column_sums(p):
    if p.dtype.itemsize < 4:
        t = p.shape[0]
        p = p[:t // 2] + p[t // 2:]
        p = p[:t // 4] + p[t // 4:]
    return jnp.sum(p.astype(F32), axis=0, keepdims=True)


def _sheet_begin(step, sheet_in, sheet_out, first_row, rows):
    @pl.when(step == 0)
    def _():
        sheet_out[...] = sheet_in[...]
        sheet_out[first_row:first_row + rows, :] = jnp.zeros((rows, SG_W), F32)


def _sheet_spec():
    return pl.BlockSpec((SG_ROWS, SG_W), lambda i: (0, 0))


def _spatial_mask(transposed):
    ri = lax.broadcasted_iota(jnp.int32, (GMLP_BLOCK, GMLP_BLOCK), 0) // CAUSAL_CHUNK
    ci = lax.broadcasted_iota(jnp.int32, (GMLP_BLOCK, GMLP_BLOCK), 1) // CAUSAL_CHUNK
    return (ri <= ci) if transposed else (ci <= ri)


def _gmlp_forward(u, v, ln_g, ln_b, ws_ref, bs_ref, f_scr):
    tm = u.shape[0]
    tu = _gelu_tanh(u)
    tv = _gelu_tanh(v)
    gu = 0.5 * u * (1.0 + tu)
    gv = 0.5 * v * (1.0 + tv)
    mu = jnp.mean(gv, axis=-1, keepdims=True)
    cen = gv - mu
    rstd = lax.rsqrt(jnp.mean(cen * cen, axis=-1, keepdims=True) + LN_EPS)
    xh = cen * rstd
    vn = (xh * ln_g + ln_b).astype(BF16)
    mask = _spatial_mask(False)
    wm = [jnp.where(mask, ws_ref[h], 0.0).astype(BF16) for h in range(N_HEADS)]
    for b in range(tm // GMLP_BLOCK):
        rows = slice(b * GMLP_BLOCK, (b + 1) * GMLP_BLOCK)
        for h in range(N_HEADS):
            cols = slice(h * HEAD, (h + 1) * HEAD)
            f_scr[rows, cols] = (_dot(wm[h], vn[rows, cols]) + bs_ref[h]).astype(f_scr.dtype)
    return gu, tu, tv, xh, rstd, vn, f_scr[...]


def _position():
    return lax.axis_index("x"), lax.axis_index("y"), lax.axis_index("c")


def _handshake(peers):
    barrier = pltpu.get_barrier_semaphore()
    for peer in peers:
        pl.semaphore_signal(barrier, inc=1, device_id=peer, device_id_type=MESH)
    pl.semaphore_wait(barrier, len(peers))


class _Gather:
    collective_id = 1

    def __init__(self, arrays):
        self.arrays = list(arrays)
        self.out_shape = [jax.ShapeDtypeStruct((N_DEV,) + a.shape, a.dtype) for a in self.arrays]
        self.base = 0

    def barrier(self):
        x, y, c = _position()
        _handshake([(x, y, 1 - c), (1 - x, y, c), (x, 1 - y, c), (1 - x, 1 - y, c)])

    def _plan(self, ins, outs, sems):
        send_sems, recv_sems, local_sems = sems
        x, y, c = _position()
        me, sibling = (x, y, c), (x, y, 1 - c)
        chips = [(1 - x, y), (x, 1 - y), (1 - x, 1 - y)]

        def slot(a, p):
            return outs[a].at[4 * p[0] + 2 * p[1] + p[2]]

        def copy(a, k, block, to, src=None):
            return pltpu.make_async_remote_copy(
                src_ref=slot(a, block) if src is None else src, dst_ref=slot(a, block),
                send_sem=send_sems.at[self.base + a, k], recv_sem=recv_sems.at[self.base + a, k],
                device_id=to, device_id_type=MESH)

        n = len(self.arrays)

        def mine():
            return [pltpu.make_async_copy(ins[a], slot(a, me), local_sems.at[self.base + a]) for a in range(n)]

        def first():
            out = []
            for a in range(n):
                out.append(copy(a, 0, me, sibling, src=ins[a]))
                out += [copy(a, 1 + j, me, (*chip, c), src=ins[a]) for j, chip in enumerate(chips)]
            return out

        def arrivals():
            return [copy(a, 1 + j, (*chip, c), me) for j, chip in enumerate(chips) for a in range(n)]

        def relays():
            return [copy(a, 4 + j, (*chip, c), sibling) for j, chip in enumerate(chips) for a in range(n)]

        def from_sibling():
            out = [copy(a, 0, sibling, me) for a in range(n)]
            return out + [copy(a, 4 + j, (*chip, 1 - c), me) for j, chip in enumerate(chips) for a in range(n)]

        return mine, first, arrivals, relays, from_sibling

    def start(self, ins, outs, sems):
        mine, first, _, _, _ = self._plan(ins, outs, sems)
        for cp in mine() + first():
            cp.start()

    def relay(self, ins, outs, sems):
        _, _, arrivals, relays, _ = self._plan(ins, outs, sems)
        for arrived, onward in zip(arrivals(), relays()):
            arrived.wait_recv()
            onward.start()

    def finish(self, ins, outs, sems):
        mine, first, _, relays, from_sibling = self._plan(ins, outs, sems)
        for cp in from_sibling():
            cp.wait_recv()
        for cp in first() + relays():
            cp.wait_send()
        for cp in mine():
            cp.wait()


class _Exchange:
    collective_id = 0

    def __init__(self, arrays):
        self.arrays = list(arrays)
        self.out_shape = [jax.ShapeDtypeStruct(a.shape, a.dtype) for a in self.arrays]
        self.base = 0

    def barrier(self):
        x, y, c = _position()
        _handshake([(x ^ dx, y ^ dy, c ^ dc) for dx in (0, 1) for dy in (0, 1) for dc in (0, 1) if dx or dy or dc])

    def _plan(self, ins, outs, sems):
        send_sems, recv_sems, local_sems = sems
        x, y, c = _position()
        my_idx = 4 * x + 2 * y + c
        n = len(self.arrays)
        offsets = [(dx, dy, dc) for dx in (0, 1) for dy in (0, 1) for dc in (0, 1) if (dx, dy, dc) != (0, 0, 0)]

        def mine():
            return [pltpu.make_async_copy(ins[a].at[my_idx], outs[a].at[my_idx], local_sems.at[self.base + a])
                    for a in range(n)]

        def remote(arriving):
            out = []
            for k, (dx, dy, dc) in enumerate(offsets):
                px, py, pc = x ^ dx, y ^ dy, c ^ dc
                p_idx = 4 * px + 2 * py + pc
                for a in range(n):
                    out.append(pltpu.make_async_remote_copy(
                        src_ref=ins[a].at[p_idx], dst_ref=outs[a].at[p_idx if arriving else my_idx],
                        send_sem=send_sems.at[self.base + a, k], recv_sem=recv_sems.at[self.base + a, k],
                        device_id=(px, py, pc), device_id_type=MESH))
            return out

        return mine, remote

    def start(self, ins, outs, sems):
        mine, remote = self._plan(ins, outs, sems)
        for cp in mine() + remote(False):
            cp.start()

    def relay(self, ins, outs, sems):
        pass

    def finish(self, ins, outs, sems):
        mine, remote = self._plan(ins, outs, sems)
        for cp in remote(True):
            cp.wait_recv()
        for cp in remote(False):
            cp.wait_send()
        for cp in mine():
            cp.wait()


class _Both:
    def __init__(self, *carries):
        self.carries = carries
        self.arrays = [a for c in carries for a in c.arrays]
        self.out_shape = [s for c in carries for s in c.out_shape]
        first = 0
        for c in carries:
            c.base = first
            first += len(c.arrays)
        self.collective_id = min(c.collective_id for c in carries)

    def barrier(self):
        min(self.carries, key=lambda c: c.collective_id).barrier()

    def _each(self, method, ins, outs, sems):
        for c in self.carries:
            rows = slice(c.base, c.base + len(c.arrays))
            getattr(c, method)(ins[rows], outs[rows], sems)

    def start(self, ins, outs, sems):
        self._each("start", ins, outs, sems)

    def relay(self, ins, outs, sems):
        self._each("relay", ins, outs, sems)

    def finish(self, ins, outs, sems):
        self._each("finish", ins, outs, sems)


def _call(body, *, name, grid, in_specs, out_specs, out_shape, args, scratch_shapes=(), carry=None):
    n_in, n_out, n_scr = len(in_specs), len(out_specs), len(scratch_shapes)
    params = pltpu.CompilerParams(dimension_semantics=("arbitrary",) * len(grid), vmem_limit_bytes=VMEM_LIMIT)
    if carry is None:
        outs = pl.pallas_call(body, name=name, grid=grid, in_specs=in_specs, out_specs=out_specs, out_shape=out_shape,
                              scratch_shapes=list(scratch_shapes), compiler_params=params)(*args)
        return outs, None
    m = len(carry.arrays)
    total = math.prod(grid)

    def wrapped(*refs):
        ins, refs = refs[:n_in], refs[n_in:]
        c_ins, refs = refs[:m], refs[m:]
        outs, refs = refs[:n_out], refs[n_out:]
        c_outs, refs = refs[:m], refs[m:]
        scr, sems = refs[:n_scr], refs[n_scr:]
        flat = pl.program_id(0)
        for d in range(1, len(grid)):
            flat = flat * grid[d] + pl.program_id(d)

        @pl.when(flat == 0)
        def _():
            carry.barrier()
            carry.start(c_ins, c_outs, sems)

        body(*ins, *outs, *scr)

        @pl.when(flat == total - 2)
        def _():
            carry.relay(c_ins, c_outs, sems)

        @pl.when(flat == total - 1)
        def _():
            carry.finish(c_ins, c_outs, sems)

    any_spec = pl.BlockSpec(memory_space=pl.ANY)
    sem_shapes = [pltpu.SemaphoreType.DMA((m, 7)), pltpu.SemaphoreType.DMA((m, 7)), pltpu.SemaphoreType.DMA((m,))]
    params = pltpu.CompilerParams(dimension_semantics=("arbitrary",) * len(grid), vmem_limit_bytes=VMEM_LIMIT,
                                  collective_id=carry.collective_id)
    outs = pl.pallas_call(
        wrapped, name=name, grid=grid,
        in_specs=list(in_specs) + [any_spec] * m, out_specs=list(out_specs) + [any_spec] * m,
        out_shape=list(out_shape) + carry.out_shape,
        scratch_shapes=list(scratch_shapes) + sem_shapes, compiler_params=params)(*args, *carry.arrays)
    return outs[:n_out], outs[n_out:]


def _gather_now(arrays, *, name):
    carry = _Gather(arrays)
    m = len(arrays)

    def body(*refs):
        ins, outs, sems = refs[:m], refs[m:2 * m], refs[2 * m:]
        carry.barrier()
        carry.start(ins, outs, sems)
        carry.relay(ins, outs, sems)
        carry.finish(ins, outs, sems)

    any_spec = pl.BlockSpec(memory_space=pl.ANY)
    return pl.pallas_call(
        body, name=name, in_specs=[any_spec] * m, out_specs=[any_spec] * m, out_shape=carry.out_shape,
        scratch_shapes=[pltpu.SemaphoreType.DMA((m, 7)), pltpu.SemaphoreType.DMA((m, 7)),
                        pltpu.SemaphoreType.DMA((m,))],
        compiler_params=pltpu.CompilerParams(collective_id=carry.collective_id),
    )(*arrays)


def _all_reduce_small(arrs, *, name):
    n = len(arrs)

    def body(*refs):
        ins, outs, bufs = refs[:n], refs[n:2 * n], refs[2 * n:3 * n]
        send_sems, recv_sems = refs[3 * n:]
        x, y, c = _position()
        me, sibling = (x, y, c), (x, y, 1 - c)
        chips = [(1 - x, y), (x, 1 - y), (1 - x, 1 - y)]
        _handshake([sibling] + [(*chip, c) for chip in chips])

        def copy(a, k, block, to, src=None):
            slot = bufs[a].at[4 * block[0] + 2 * block[1] + block[2]]
            return pltpu.make_async_remote_copy(
                src_ref=slot if src is None else src, dst_ref=slot,
                send_sem=send_sems.at[a, k], recv_sem=recv_sems.at[a, k], device_id=to, device_id_type=MESH)

        first = []
        for a in range(n):
            first.append(copy(a, 0, me, sibling, src=ins[a]))
            first += [copy(a, 1 + j, me, (*chip, c), src=ins[a]) for j, chip in enumerate(chips)]
        for cp in first:
            cp.start()
        passed = []
        for j, chip in enumerate(chips):
            for a in range(n):
                copy(a, 1 + j, (*chip, c), me).wait_recv()
                cp = copy(a, 4 + j, (*chip, c), sibling)
                cp.start()
                passed.append(cp)
        for a in range(n):
            copy(a, 0, sibling, me).wait_recv()
            for j, chip in enumerate(chips):
                copy(a, 4 + j, (*chip, 1 - c), me).wait_recv()
        for cp in first + passed:
            cp.wait_send()
        my_idx = 4 * x + 2 * y + c
        for a in range(n):
            acc = jnp.zeros(ins[a].shape, F32)
            for s in range(N_DEV):
                acc = acc + jnp.where(my_idx == s, ins[a][...], bufs[a][s])
            outs[a][...] = acc

    vmem = pl.BlockSpec(memory_space=pltpu.VMEM)
    return pl.pallas_call(
        body, name=name, in_specs=[vmem] * n, out_specs=[vmem] * n,
        out_shape=[jax.ShapeDtypeStruct(a.shape, F32) for a in arrs],
        scratch_shapes=[pltpu.VMEM((N_DEV,) + a.shape, F32) for a in arrs]
        + [pltpu.SemaphoreType.DMA((n, 7)), pltpu.SemaphoreType.DMA((n, 7))],
        compiler_params=pltpu.CompilerParams(vmem_limit_bytes=VMEM_LIMIT, collective_id=_Gather.collective_id),
    )(*arrs)


def _sum_gathered(arrs, *, name):
    n = len(arrs)

    def body(*refs):
        for in_ref, out_ref in zip(refs[:n], refs[n:]):
            acc = in_ref[0]
            for s in range(1, N_DEV):
                acc = acc + in_ref[s]
            out_ref[...] = acc

    vmem = pl.BlockSpec(memory_space=pltpu.VMEM)
    return pl.pallas_call(
        body, name=name, in_specs=[vmem] * n, out_specs=[vmem] * n,
        out_shape=[jax.ShapeDtypeStruct(a.shape[1:], F32) for a in arrs],
        compiler_params=pltpu.CompilerParams(vmem_limit_bytes=VMEM_LIMIT),
    )(*arrs)


def _norm_matmul(x, g, w_t, *, tm, name, carry=None):
    n, d = x.shape
    c = w_t.shape[0]
    ch = MATMUL_CHUNK

    def body(x_ref, g_ref, wt_ref, h_ref, z_ref):
        xv = x_ref[...]
        r = lax.rsqrt(jnp.mean(xv * xv, axis=-1, keepdims=True) + RMS_EPS)
        h = (xv * r * g_ref[...]).astype(BF16)
        h_ref[...] = h
        for c0 in range(0, c, ch):
            z_ref[:, c0:c0 + ch] = _dot_nt(h, wt_ref[c0:c0 + ch, :]).astype(BF16)

    return _call(
        body, name=name, grid=(n // tm,), carry=carry,
        in_specs=[pl.BlockSpec((tm, d), lambda i: (i, 0)),
                  pl.BlockSpec((1, d), lambda i: (0, 0)),
                  pl.BlockSpec((c, d), lambda i: (0, 0))],
        out_specs=[pl.BlockSpec((tm, d), lambda i: (i, 0)),
                   pl.BlockSpec((tm, c), lambda i: (i, 0))],
        out_shape=[jax.ShapeDtypeStruct((n, d), BF16), jax.ShapeDtypeStruct((n, c), BF16)],
        args=(x, g.reshape(1, d), w_t))


def _mix_forward(z, x, b_gate, ln_g, ln_b, w_s, b_s, w_sc, wb, w_out, *, tm, name, carry=None):
    n = z.shape[0]
    hb = tm // HALO

    def body(z_ref, zp_ref, x_ref, bg_ref, lng_ref, lnb_ref, ws_ref, bs_ref, wsc_ref, wb_ref, wo_ref,
             ya_ref, yb_ref, cv_ref, pa_ref, pb_ref, mg_ref, x1_ref, f_scr):
        i = pl.program_id(0)
        u = z_ref[:, OFF_U:OFF_U + D_A]
        v = z_ref[:, OFF_V:OFF_V + D_A].astype(F32)
        gu, _, _, _, _, _, f = _gmlp_forward(u, v, lng_ref[...], lnb_ref[...], ws_ref, bs_ref, f_scr)
        ya = gu * f
        ya_ref[...] = ya

        q = z_ref[:, OFF_CG:OFF_CG + D_B] * z_ref[:, OFF_HB:OFF_HB + D_B]
        qp = zp_ref[:, OFF_CG:OFF_CG + D_B] * zp_ref[:, OFF_HB:OFF_HB + D_B]
        qp = jnp.where(i > 0, qp, jnp.zeros_like(qp))
        w = wsc_ref[...].astype(BF16)
        conv = w[0:1] * _shift_down(q, 2, qp) + w[1:2] * _shift_down(q, 1, qp) + w[2:3] * q
        cv_ref[...] = conv
        yb = z_ref[:, OFF_BG:OFF_BG + D_B] * conv
        yb_ref[...] = yb

        pa = _dot(ya, wb_ref[0]).astype(BF16)
        pb = _dot(yb, wb_ref[1]).astype(BF16)
        pa_ref[...] = pa
        pb_ref[...] = pb
        bg = bg_ref[...].astype(BF16)
        sa = _sigmoid(z_ref[:, OFF_GA:OFF_GA + D_MODEL] + bg[:, 0:D_MODEL])
        sb = _sigmoid(z_ref[:, OFF_GB:OFF_GB + D_MODEL] + bg[:, D_MODEL:2 * D_MODEL])
        mg = sa * pa + sb * pb
        mg_ref[...] = mg
        x1_ref[...] = x_ref[...] + _dot(mg, wo_ref[...])

    row = lambda w: pl.BlockSpec((tm, w), lambda i: (i, 0))
    full = lambda *s: pl.BlockSpec(s, lambda i: (0,) * len(s))
    bf = lambda w: jax.ShapeDtypeStruct((n, w), BF16)
    return _call(
        body, name=name, grid=(n // tm,), carry=carry,
        in_specs=[row(D_IN),
                  pl.BlockSpec((HALO, D_IN), lambda i: (jnp.maximum(i * hb - 1, 0), 0)),
                  row(D_MODEL), full(1, 2 * D_MODEL), full(1, D_A), full(1, D_A),
                  full(N_HEADS, GMLP_BLOCK, GMLP_BLOCK), full(N_HEADS, GMLP_BLOCK, 1), full(3, D_B),
                  full(2, D_A, D_MODEL), full(D_MODEL, D_MODEL)],
        out_specs=[row(D_A), row(D_B), row(D_B), row(D_MODEL), row(D_MODEL), row(D_MODEL), row(D_MODEL)],
        out_shape=[bf(D_A), bf(D_B), bf(D_B), bf(D_MODEL), bf(D_MODEL), bf(D_MODEL),
                   jax.ShapeDtypeStruct((n, D_MODEL), F32)],
        scratch_shapes=[pltpu.VMEM((tm, D_A), BF16)],
        args=(z, z, x, b_gate.reshape(1, -1), ln_g.reshape(1, -1), ln_b.reshape(1, -1), w_s,
              b_s.reshape(N_HEADS, GMLP_BLOCK, 1), w_sc, wb, w_out))


def _loss_tile(xv, gv, tv):
    d = xv.shape[-1]
    r = lax.rsqrt(jnp.mean(xv * xv, axis=-1, keepdims=True) + RMS_EPS)
    xh = xv * r
    e = xh * gv - tv
    per_row = jnp.sum(e * e, axis=-1, keepdims=True) * (0.5 / d)
    dy = e * (1.0 / d)
    dxh = dy * gv
    dx = r * (dxh - xh * jnp.mean(dxh * xh, axis=-1, keepdims=True))
    return dx, jnp.sum(per_row, axis=0, keepdims=True), jnp.sum(dy * xh, axis=0, keepdims=True)


def _ffn_forward(up, x1, w_fc, b_fc, w_down, *, tm, name, carry=None, head=None):
    n = up.shape[0]
    hb = tm // HALO
    n_in = 6 if head is None else 8

    def body(*refs):
        up_ref, upp_ref, x1_ref, wfc_ref, bfc_ref, wd_ref = refs[:6]
        gc_ref, a_ref, out_ref = refs[n_in:n_in + 3]
        acc = refs[-1]
        i = pl.program_id(0)
        acc[...] = x1_ref[...]
        for c0 in range(0, D_FF, FFN_CHUNK):
            cols = slice(c0, c0 + FFN_CHUNK)
            gate = up_ref[:, cols]
            val = up_ref[:, D_FF + c0:D_FF + c0 + FFN_CHUNK]
            gp = upp_ref[:, cols]
            gp = jnp.where(i > 0, gp, jnp.zeros_like(gp))
            w = wfc_ref[:, cols].astype(BF16)
            gc = (w[0:1] * _shift_down(gate, 2, gp) + w[1:2] * _shift_down(gate, 1, gp) + w[2:3] * gate
                  + bfc_ref[:, cols].astype(BF16))
            gc_ref[:, cols] = gc
            a = gc * _sigmoid(gc) * val
            a_ref[:, cols] = a
            acc[...] += _dot(a, wd_ref[cols, :])
        if head is None:
            out_ref[...] = acc[...]
        else:
            g_ref, t_ref = refs[6:8]
            sg_ref = refs[n_in + 3]

            @pl.when(i == 0)
            def _():
                sg_ref[...] = jnp.zeros_like(sg_ref)

            dx, loss, dg = _loss_tile(acc[...], g_ref[...], t_ref[...])
            out_ref[...] = dx
            sg_ref[ROW_LOSS:ROW_LOSS + 1, 0:LANES] += jnp.broadcast_to(loss, (1, LANES))
            sg_ref[ROW_FINAL:ROW_FINAL + 1, 0:D_MODEL] += dg

    row = lambda w: pl.BlockSpec((tm, w), lambda i: (i, 0))
    full = lambda r, c: pl.BlockSpec((r, c), lambda i: (0, 0))
    in_specs = [row(2 * D_FF), pl.BlockSpec((HALO, D_FF), lambda i: (jnp.maximum(i * hb - 1, 0), 0)), row(D_MODEL),
                full(3, D_FF), full(1, D_FF), full(D_FF, D_MODEL)]
    out_specs = [row(D_FF), row(D_FF), row(D_MODEL)]
    out_shape = [jax.ShapeDtypeStruct((n, D_FF), BF16), jax.ShapeDtypeStruct((n, D_FF), BF16),
                 jax.ShapeDtypeStruct((n, D_MODEL), F32)]
    args = (up, up, x1, w_fc, b_fc.reshape(1, -1), w_down)
    if head is not None:
        in_specs += [full(1, D_MODEL), row(D_MODEL)]
        out_specs += [full(SG_ROWS, SG_W)]
        out_shape += [jax.ShapeDtypeStruct((SG_ROWS, SG_W), F32)]
        args += (head[0].reshape(1, -1), head[1])
    return _call(body, name=name, grid=(n // tm,), carry=carry, in_specs=in_specs, out_specs=out_specs,
                 out_shape=out_shape, scratch_shapes=[pltpu.VMEM((tm, D_MODEL), F32)], args=args)


def _ffn_backward(dx2, up, gc, w_fc, w_down, sheet, layer, *, tm, name, carry=None):
    n = up.shape[0]
    steps = n // tm
    hb = tm // HALO
    row = SG_LAYER * layer + ROW_FCONV

    def body(dx_ref, dxn_ref, up_ref, upn_ref, gc_ref, gcn_ref, wfc_ref, wd_ref, sg_in, dup_ref, sg_ref):
        i = pl.program_id(0)
        last = i == steps - 1
        _sheet_begin(i, sg_in, sg_ref, row, 4)

        dxe = jnp.concatenate([dx_ref[...], dxn_ref[...]], axis=0).astype(BF16)
        for c0 in range(0, D_FF, FFN_CHUNK):
            cols = slice(c0, c0 + FFN_CHUNK)
            vcols = slice(D_FF + c0, D_FF + c0 + FFN_CHUNK)
            dae = _dot_nt(dxe, wd_ref[cols, :])
            da, dan = dae[:tm], dae[tm:]
            gate = up_ref[:, cols]
            val = up_ref[:, vcols]
            gcv = gc_ref[:, cols]
            s = _sigmoid(gcv)
            dab = da.astype(BF16)
            dup_ref[:, vcols] = dab * (gcv * s)
            dgc = dab * val * (s * (1.0 + gcv * (1.0 - s)))
            gcn = gcn_ref[:, cols]
            sn = _sigmoid(gcn)
            dgcn = dan.astype(BF16) * upn_ref[:, vcols] * (sn * (1.0 + gcn * (1.0 - sn)))
            dgcn = jnp.where(last, jnp.zeros_like(dgcn), dgcn)
            up1 = _shift_up(dgc, 1, dgcn)
            up2 = _shift_up(dgc, 2, dgcn)
            w = wfc_ref[:, cols].astype(BF16)
            dup_ref[:, cols] = w[2:3] * dgc + w[1:2] * up1 + w[0:1] * up2
            sg_ref[row:row + 1, cols] += _column_sums(gate * up2)
            sg_ref[row + 1:row + 2, cols] += _column_sums(gate * up1)
            sg_ref[row + 2:row + 3, cols] += _column_sums(gate * dgc)
            sg_ref[row + 3:row + 4, cols] += _column_sums(dgc)

    nxt = lambda i: (jnp.minimum((i + 1) * hb, steps * hb - 1), 0)
    return _call(
        body, name=name, grid=(steps,), carry=carry,
        in_specs=[pl.BlockSpec((tm, D_MODEL), lambda i: (i, 0)),
                  pl.BlockSpec((HALO, D_MODEL), nxt),
                  pl.BlockSpec((tm, 2 * D_FF), lambda i: (i, 0)),
                  pl.BlockSpec((HALO, 2 * D_FF), nxt),
                  pl.BlockSpec((tm, D_FF), lambda i: (i, 0)),
                  pl.BlockSpec((HALO, D_FF), nxt),
                  pl.BlockSpec((3, D_FF), lambda i: (0, 0)),
                  pl.BlockSpec((D_FF, D_MODEL), lambda i: (0, 0)), _sheet_spec()],
        out_specs=[pl.BlockSpec((tm, 2 * D_FF), lambda i: (i, 0)), _sheet_spec()],
        out_shape=[jax.ShapeDtypeStruct((n, 2 * D_FF), BF16), jax.ShapeDtypeStruct((SG_ROWS, SG_W), F32)],
        args=(dx2, dx2, up, up, gc, gc, w_fc, w_down, sheet))


def _matmul_norm_backward(dz, w_t, x, g, dres, sheet, row, *, tm, name, carry=None):
    n, c = dz.shape
    d = x.shape[1]
    ch = MATMUL_CHUNK

    def body(dz_ref, wt_ref, x_ref, g_ref, dres_ref, *rest):
        i = pl.program_id(0)
        if sheet is None:
            dx_ref, sg_ref = rest

            @pl.when(i == 0)
            def _():
                sg_ref[...] = jnp.zeros_like(sg_ref)
        else:
            sg_in, dx_ref, sg_ref = rest
            _sheet_begin(i, sg_in, sg_ref, row, 1)

        dh = _dot(dz_ref[:, 0:ch], wt_ref[0:ch, :])
        for c0 in range(ch, c, ch):
            dh += _dot(dz_ref[:, c0:c0 + ch], wt_ref[c0:c0 + ch, :])
        xv = x_ref[...]
        r = lax.rsqrt(jnp.mean(xv * xv, axis=-1, keepdims=True) + RMS_EPS)
        xh = xv * r
        sg_ref[row:row + 1, 0:d] += jnp.sum(dh * xh, axis=0, keepdims=True)
        dxh = dh * g_ref[...]
        dx_ref[...] = dres_ref[...] + r * (dxh - xh * jnp.mean(dxh * xh, axis=-1, keepdims=True))

    in_specs = [pl.BlockSpec((tm, c), lambda i: (i, 0)),
                pl.BlockSpec((c, d), lambda i: (0, 0)),
                pl.BlockSpec((tm, d), lambda i: (i, 0)),
                pl.BlockSpec((1, d), lambda i: (0, 0)),
                pl.BlockSpec((tm, d), lambda i: (i, 0))]
    args = (dz, w_t, x, g.reshape(1, d), dres)
    if sheet is None:
        small_spec, small_shape = pl.BlockSpec((8, d), lambda i: (0, 0)), jax.ShapeDtypeStruct((8, d), F32)
    else:
        in_specs, args = in_specs + [_sheet_spec()], args + (sheet,)
        small_spec, small_shape = _sheet_spec(), jax.ShapeDtypeStruct((SG_ROWS, SG_W), F32)
    return _call(
        body, name=name, grid=(n // tm,), carry=carry, in_specs=in_specs,
        out_specs=[pl.BlockSpec((tm, d), lambda i: (i, 0)), small_spec],
        out_shape=[jax.ShapeDtypeStruct((n, d), F32), small_shape], args=args)


def _mix_backward(dx1, z, conv, pa, pb, b_gate, ln_g, ln_b, w_s, w_s_t, b_s, w_sc, w_out, wb, sheet, layer, *, tm, name,
                  carry=None):
    n = z.shape[0]
    steps = n // tm
    hb = tm // HALO
    base = SG_LAYER * layer
    r_bg, r_lng, r_lnb, r_sc = base + ROW_BGATE, base + ROW_LN_G, base + ROW_LN_B, base + ROW_SCONV

    def body(dx_ref, dxn_ref, z_ref, zn_ref, cv_ref, pa_ref, pb_ref, bg_ref, lng_ref, lnb_ref, ws_ref, wst_ref,
             bs_ref, wsc_ref, wo_ref, wb_ref, sg_in,
             dz_ref, dpa_ref, dpb_ref, dws_ref, dbs_ref, sg_ref, f_scr, dvn_scr):
        i = pl.program_id(0)
        last = i == steps - 1
        _sheet_begin(i, sg_in, sg_ref, r_bg, ROW_NORM2 - ROW_BGATE)

        @pl.when(i == 0)
        def _():
            dws_ref[...] = jnp.zeros_like(dws_ref)
            dbs_ref[...] = jnp.zeros_like(dbs_ref)

        dxe = jnp.concatenate([dx_ref[...], dxn_ref[...]], axis=0).astype(BF16)
        dmge = _dot_nt(dxe, wo_ref[...])
        dmg, dmgn = dmge[:tm].astype(BF16), dmge[tm:].astype(BF16)

        pa_v = pa_ref[...]
        pb_v = pb_ref[...]
        bg = bg_ref[...].astype(BF16)
        sa = _sigmoid(z_ref[:, OFF_GA:OFF_GA + D_MODEL] + bg[:, 0:D_MODEL])
        sb = _sigmoid(z_ref[:, OFF_GB:OFF_GB + D_MODEL] + bg[:, D_MODEL:2 * D_MODEL])
        dpa = dmg * sa
        dpb = dmg * sb
        dga = dmg * pa_v * sa * (1.0 - sa)
        dgb = dmg * pb_v * sb * (1.0 - sb)
        dpa_ref[...] = dpa
        dpb_ref[...] = dpb
        dz_ref[:, OFF_GA:OFF_GA + D_MODEL] = dga
        dz_ref[:, OFF_GB:OFF_GB + D_MODEL] = dgb
        sg_ref[r_bg:r_bg + 1, 0:D_MODEL] += _column_sums(dga)
        sg_ref[r_bg:r_bg + 1, D_MODEL:2 * D_MODEL] += _column_sums(dgb)

        dya = _dot_nt(dpa, wb_ref[0]).astype(BF16)
        u = z_ref[:, OFF_U:OFF_U + D_A]
        v = z_ref[:, OFF_V:OFF_V + D_A].astype(F32)
        ln_g = lng_ref[...]
        gu, tu, tv, xh, rstd, vn, f = _gmlp_forward(u, v, ln_g, lnb_ref[...], ws_ref, bs_ref, f_scr)
        dgu = dya * f
        df_bf = dya * gu
        dz_ref[:, OFF_U:OFF_U + D_A] = dgu * _gelu_grad(u, tu)
        mask = _spatial_mask(False)
        mask_t = _spatial_mask(True)
        wmt = [jnp.where(mask_t, wst_ref[h], 0.0).astype(BF16) for h in range(N_HEADS)]
        for b in range(tm // GMLP_BLOCK):
            rows = slice(b * GMLP_BLOCK, (b + 1) * GMLP_BLOCK)
            for h in range(N_HEADS):
                cols = slice(h * HEAD, (h + 1) * HEAD)
                dfb = df_bf[rows, cols]
                dvn_scr[rows, cols] = _dot(wmt[h], dfb)
                dws_ref[h] += jnp.where(mask, _dot_nt(dfb, vn[rows, cols]), 0.0)
                dbs_ref[h] += jnp.sum(dfb.astype(F32), axis=1, keepdims=True)
        dvn = dvn_scr[...]
        sg_ref[r_lng:r_lng + 1, 0:D_A] += jnp.sum(dvn * xh, axis=0, keepdims=True)
        sg_ref[r_lnb:r_lnb + 1, 0:D_A] += jnp.sum(dvn, axis=0, keepdims=True)
        dxh = dvn * ln_g
        dgv = rstd * (dxh - jnp.mean(dxh, axis=-1, keepdims=True) - xh * jnp.mean(dxh * xh, axis=-1, keepdims=True))
        dz_ref[:, OFF_V:OFF_V + D_A] = (dgv * _gelu_grad(v, tv)).astype(BF16)

        sbn = _sigmoid(zn_ref[:, OFF_GB:OFF_GB + D_MODEL] + bg[:, D_MODEL:2 * D_MODEL])
        dpbe = jnp.concatenate([dpb, dmgn * sbn], axis=0)
        dybe = _dot_nt(dpbe, wb_ref[1])
        dyb, dybn = dybe[:tm].astype(BF16), dybe[tm:].astype(BF16)
        bgv = z_ref[:, OFF_BG:OFF_BG + D_B]
        cg = z_ref[:, OFF_CG:OFF_CG + D_B]
        hbv = z_ref[:, OFF_HB:OFF_HB + D_B]
        q = cg * hbv
        dz_ref[:, OFF_BG:OFF_BG + D_B] = dyb * cv_ref[...]
        dconv = dyb * bgv
        dconvn = dybn * zn_ref[:, OFF_BG:OFF_BG + D_B]
        dconvn = jnp.where(last, jnp.zeros_like(dconvn), dconvn)
        up1 = _shift_up(dconv, 1, dconvn)
        up2 = _shift_up(dconv, 2, dconvn)
        sg_ref[r_sc:r_sc + 1, 0:D_B] += _column_sums(q * up2)
        sg_ref[r_sc + 1:r_sc + 2, 0:D_B] += _column_sums(q * up1)
        sg_ref[r_sc + 2:r_sc + 3, 0:D_B] += _column_sums(q * dconv)
        w = wsc_ref[...].astype(BF16)
        dq = w[2:3] * dconv + w[1:2] * up1 + w[0:1] * up2
        dz_ref[:, OFF_CG:OFF_CG + D_B] = dq * hbv
        dz_ref[:, OFF_HB:OFF_HB + D_B] = dq * cg

    row = lambda w: pl.BlockSpec((tm, w), lambda i: (i, 0))
    full = lambda *s: pl.BlockSpec(s, lambda i: (0,) * len(s))
    nxt = lambda i: (jnp.minimum((i + 1) * hb, steps * hb - 1), 0)
    return _call(
        body, name=name, grid=(steps,), carry=carry,
        in_specs=[row(D_MODEL), pl.BlockSpec((HALO, D_MODEL), nxt),
                  row(D_IN), pl.BlockSpec((HALO, D_IN), nxt),
                  row(D_B), row(D_MODEL), row(D_MODEL),
                  full(1, 2 * D_MODEL), full(1, D_A), full(1, D_A),
                  full(N_HEADS, GMLP_BLOCK, GMLP_BLOCK), full(N_HEADS, GMLP_BLOCK, GMLP_BLOCK),
                  full(N_HEADS, GMLP_BLOCK, 1), full(3, D_B),
                  full(D_MODEL, D_MODEL), full(2, D_A, D_MODEL), _sheet_spec()],
        out_specs=[row(D_IN), row(D_MODEL), row(D_MODEL), full(N_HEADS, GMLP_BLOCK, GMLP_BLOCK),
                   full(N_HEADS, GMLP_BLOCK, 1), _sheet_spec()],
        out_shape=[jax.ShapeDtypeStruct((n, D_IN), BF16), jax.ShapeDtypeStruct((n, D_MODEL), BF16),
                   jax.ShapeDtypeStruct((n, D_MODEL), BF16),
                   jax.ShapeDtypeStruct((N_HEADS, GMLP_BLOCK, GMLP_BLOCK), F32),
                   jax.ShapeDtypeStruct((N_HEADS, GMLP_BLOCK, 1), F32), jax.ShapeDtypeStruct((SG_ROWS, SG_W), F32)],
        scratch_shapes=[pltpu.VMEM((tm, D_A), BF16), pltpu.VMEM((tm, D_A), F32)],
        args=(dx1, dx1, z, z, conv, pa, pb, b_gate.reshape(1, -1), ln_g.reshape(1, -1), ln_b.reshape(1, -1), w_s, w_s_t,
              b_s.reshape(N_HEADS, GMLP_BLOCK, 1), w_sc, w_out, wb, sheet))


def _matmul_tn(a, b, *, t1, tn, name, carry=None, pieces=1):
    n, k1 = a.shape
    k2 = b.shape[1]
    steps = n // tn
    w = k2 // pieces

    def body(a_ref, b_ref, *rest):
        o_refs, acc = rest[:pieces], rest[pieces]
        s = pl.program_id(1)

        @pl.when(s == 0)
        def _():
            acc[...] = jnp.zeros_like(acc)

        acc[...] += lax.dot_general(a_ref[...].astype(BF16), b_ref[...].astype(BF16), TN, preferred_element_type=F32)

        @pl.when(s == steps - 1)
        def _():
            for c, o_ref in enumerate(o_refs):
                o_ref[...] = acc[:, c * w:(c + 1) * w].astype(BF16)

    outs, carried = _call(
        body, name=name, grid=(k1 // t1, steps), carry=carry,
        in_specs=[pl.BlockSpec((tn, t1), lambda i, s: (s, i)),
                  pl.BlockSpec((tn, k2), lambda i, s: (s, 0))],
        out_specs=[pl.BlockSpec((t1, w), lambda i, s: (i, 0))] * pieces,
        out_shape=[jax.ShapeDtypeStruct((k1, w), BF16)] * pieces,
        scratch_shapes=[pltpu.VMEM((t1, k2), F32)],
        args=(a, b))
    return (outs[0] if pieces == 1 else list(outs)), carried


def _adamw_math(w, g, m, v):
    m = ADAM_B1 * m + (1.0 - ADAM_B1) * g
    v = ADAM_B2 * v + (1.0 - ADAM_B2) * (g * g)
    m_hat = m / (1.0 - ADAM_B1 ** ADAM_STEP)
    v_hat = v / (1.0 - ADAM_B2 ** ADAM_STEP)
    delta = -ADAM_LR * (m_hat / (jnp.sqrt(v_hat) + ADAM_EPS) + ADAM_WD * w)
    return delta, m, v


def _sum_parts(recvs, *, tr, name):
    _, r, c = recvs[0].shape

    def body(*refs):
        recv_refs, g_ref = refs[:DEPTH], refs[DEPTH]
        layer = pl.program_id(0)
        for l in range(DEPTH):
            @pl.when(layer == l)
            def _(l=l):
                g = recv_refs[l][0].astype(F32)
                for s in range(1, N_DEV):
                    g = g + recv_refs[l][s].astype(F32)
                g_ref[0] = g

    outs, _ = _call(
        body, name=name, grid=(DEPTH, r // tr),
        in_specs=[pl.BlockSpec((N_DEV, tr, c), lambda l, i: (0, i, 0))] * DEPTH,
        out_specs=[pl.BlockSpec((1, tr, c), lambda l, i: (l, i, 0))],
        out_shape=[jax.ShapeDtypeStruct((DEPTH, r, c), F32)],
        args=tuple(recvs))
    return outs[0]


def _adamw(w, g, m, v, *, tr, name):
    r, c = w.shape

    def body(w_ref, g_ref, m_ref, v_ref, d_ref, nm_ref, nv_ref):
        delta, nm, nv = _adamw_math(w_ref[...], g_ref[...], m_ref[...], v_ref[...])
        d_ref[...] = delta
        nm_ref[...] = nm
        nv_ref[...] = nv

    spec = pl.BlockSpec((tr, c), lambda i: (i, 0))
    outs, _ = _call(body, name=name, grid=(r // tr,), in_specs=[spec] * 4, out_specs=[spec] * 3,
                    out_shape=[jax.ShapeDtypeStruct((r, c), F32)] * 3, args=(w, g, m, v))
    return outs


def _sum_adamw(recvs, w, m, v, *, tr, name):
    _, r, c = w.shape
    blocks = len(recvs[0])
    flat = [piece for layer in recvs for piece in layer]

    def body(*refs):
        recv_refs = refs[:len(flat)]
        w_ref, m_ref, v_ref, g_ref, d_ref, nm_ref, nv_ref = refs[len(flat):]
        layer = pl.program_id(0)
        for l in range(DEPTH):
            @pl.when(layer == l)
            def _(l=l):
                cols = []
                for piece in recv_refs[l * blocks:(l + 1) * blocks]:
                    part = piece[0].astype(F32)
                    for s in range(1, N_DEV):
                        part = part + piece[s].astype(F32)
                    cols.append(part)
                g = cols[0] if blocks == 1 else jnp.concatenate(cols, axis=-1)
                delta, nm, nv = _adamw_math(w_ref[0], g, m_ref[0], v_ref[0])
                g_ref[0] = g
                d_ref[0] = delta
                nm_ref[0] = nm
                nv_ref[0] = nv

    spec = pl.BlockSpec((1, tr, c), lambda l, i: (l, i, 0))
    outs, _ = _call(
        body, name=name, grid=(DEPTH, r // tr),
        in_specs=[pl.BlockSpec((N_DEV, tr, c // blocks), lambda l, i: (0, i, 0))] * len(flat) + [spec] * 3,
        out_specs=[spec] * 4, out_shape=[jax.ShapeDtypeStruct((DEPTH, r, c), F32)] * 4,
        args=tuple(flat) + (w, m, v))
    return outs


def _adamw_small(sheet, extra, params, *, name):
    sheet_rows = dict(norm1_g=ROW_NORM1, b_gate=ROW_BGATE, gmlp_ln_g=ROW_LN_G, gmlp_ln_b=ROW_LN_B, norm2_g=ROW_NORM2,
                      b_ffn_conv=ROW_BFCONV)
    names = list(params)
    extra_names = list(extra)

    def body(*refs):
        sg_ref, refs = refs[0], refs[1:]
        extra_refs, refs = dict(zip(extra_names, refs[:len(extra_names)])), refs[len(extra_names):]
        ins, outs = refs[:3 * len(names)], refs[3 * len(names):]
        for j, key in enumerate(names):
            w_ref, m_ref, v_ref = ins[3 * j:3 * j + 3]
            g_ref, d_ref, nm_ref, nv_ref = outs[4 * j:4 * j + 4]
            if key in extra_refs:
                g_ref[...] = extra_refs[key][...]
            elif key == "final_g":
                g_ref[...] = sg_ref[ROW_FINAL:ROW_FINAL + 1, 0:D_MODEL]
            else:
                width = w_ref.shape[-1]
                for l in range(DEPTH):
                    row = SG_LAYER * l + sheet_rows[key]
                    g_ref[l:l + 1, :] = sg_ref[row:row + 1, 0:width]
            delta, nm, nv = _adamw_math(w_ref[...], g_ref[...], m_ref[...], v_ref[...])
            d_ref[...] = delta
            nm_ref[...] = nm
            nv_ref[...] = nv

    args = [sheet] + [extra[k] for k in extra_names] + [t for k in names for t in params[k]]
    vmem = pl.BlockSpec(memory_space=pltpu.VMEM)
    outs = pl.pallas_call(
        body, name=name, in_specs=[vmem] * len(args), out_specs=[vmem] * (4 * len(names)),
        out_shape=[jax.ShapeDtypeStruct(params[k][0].shape, F32) for k in names for _ in range(4)],
    )(*args)
    return {k: tuple(outs[4 * j:4 * j + 4]) for j, k in enumerate(names)}


def _rows(gathered):
    return gathered.reshape(N_DEV * gathered.shape[1], gathered.shape[2])


def _parts(full):
    return full.reshape(N_DEV, full.shape[0] // N_DEV, full.shape[1])


def kernel(x, norm1_g, w_in, b_gate, gmlp_ln_g, gmlp_ln_b, w_spatial, b_spatial, w_shortconv, w_branch, w_out, norm2_g, w_ffn_up, w_ffn_conv, b_ffn_conv, w_ffn_down, final_g, loss_target, m_norm1_g, m_w_in, m_b_gate, m_gmlp_ln_g, m_gmlp_ln_b, m_w_spatial, m_b_spatial, m_w_shortconv, m_w_branch, m_w_out, m_norm2_g, m_w_ffn_up, m_w_ffn_conv, m_b_ffn_conv, m_w_ffn_down, m_final_g, v_norm1_g, v_w_in, v_b_gate, v_gmlp_ln_g, v_gmlp_ln_b, v_w_spatial, v_b_spatial, v_w_shortconv, v_w_branch, v_w_out, v_norm2_g, v_w_ffn_up, v_w_ffn_conv, v_b_ffn_conv, v_w_ffn_down, v_final_g):
    n = x.shape[1]
    tm_in, tm, tn = 1024, 512, 2048
    x0 = x.reshape(n, D_MODEL)
    target = loss_target.reshape(n, D_MODEL)
    my_idx = 4 * lax.axis_index("x") + 2 * lax.axis_index("y") + lax.axis_index("c")
    sc_w, fc_w = D_B // N_DEV, D_FF // N_DEV

    sh_in = [w_in[l].T.astype(BF16) for l in range(DEPTH)]
    sh_up = [w_ffn_up[l].T.astype(BF16) for l in range(DEPTH)]
    sh_br = [w_branch[l].astype(BF16) for l in range(DEPTH)]
    sh_out = [w_out[l].astype(BF16) for l in range(DEPTH)]
    sh_down = [w_ffn_down[l].astype(BF16) for l in range(DEPTH)]
    taps = jnp.concatenate([w_shortconv, w_ffn_conv], axis=-1)

    def branch_weights(g):
        return g.transpose(1, 2, 0, 3).reshape(2, D_A, D_MODEL)

    g_in0, g_taps = _gather_now([sh_in[0], taps], name="gather_first")
    w_sc = [g_taps[:, l, :, :sc_w].transpose(1, 0, 2).reshape(3, D_B) for l in range(DEPTH)]
    w_fc = [g_taps[:, l, :, sc_w:].transpose(1, 0, 2).reshape(3, D_FF) for l in range(DEPTH)]
    w_s_t = [w_spatial[l].transpose(0, 2, 1) for l in range(DEPTH)]
    weights = [dict(), dict()]
    weights[0]["in_t"] = _rows(g_in0)
    saved = []
    xc = x0
    for l in range(DEPTH):
        p = weights[l]
        carry = _Gather([sh_br[0], sh_out[0]] if l == 0 else [sh_up[1]])
        (h, z), got = _norm_matmul(xc, norm1_g[l], p["in_t"], tm=tm_in, name=f"fwd_in_{l}", carry=carry)
        if l == 0:
            p["wb"], p["out"] = branch_weights(got[0]), _rows(got[1])
        else:
            p["up_t"] = _rows(got[0])
        carry = _Gather([sh_up[0]]) if l == 0 else None
        (ya, yb, conv, pa, pb, mg, x1), got = _mix_forward(
            z, xc, b_gate[l], gmlp_ln_g[l], gmlp_ln_b[l], w_spatial[l], b_spatial[l], w_sc[l], p["wb"], p["out"],
            tm=tm, name=f"fwd_mix_{l}", carry=carry)
        if l == 0:
            p["up_t"] = _rows(got[0])
        (h2, up), got = _norm_matmul(x1, norm2_g[l], p["up_t"], tm=tm, name=f"fwd_up_{l}", carry=_Gather([sh_down[l]]))
        p["down"] = _rows(got[0])
        carry = _Gather([sh_br[1], sh_out[1], sh_in[1]]) if l == 0 else None
        head = (final_g, target) if l == DEPTH - 1 else None
        outs, got = _ffn_forward(up, x1, w_fc[l], b_ffn_conv[l], p["down"], tm=tm, name=f"fwd_ffn_{l}", carry=carry, head=head)
        if l == 0:
            weights[1]["wb"], weights[1]["out"], weights[1]["in_t"] = branch_weights(got[0]), _rows(got[1]), _rows(got[2])
        gc, a = outs[0], outs[1]
        saved.append(dict(x=xc, h=h, z=z, ya=ya, yb=yb, conv=conv, pa=pa, pb=pb, mg=mg, x1=x1, h2=h2, up=up, gc=gc, a=a))
        xc = outs[2]
    dx, sheet = outs[2], outs[3]

    recv = [dict(), dict()]
    small_dws, small_dbs = [None] * DEPTH, [None] * DEPTH
    pending_in = None
    for l in reversed(range(DEPTH)):
        p, s = weights[l], saved[l]
        carry = _Exchange([pending_in]) if pending_in is not None else None
        (dup, sheet), got = _ffn_backward(dx, s["up"], s["gc"], w_fc[l], p["down"], sheet, l, tm=tm, name=f"bwd_ffn_{l}",
                                          carry=carry)
        if got is not None:
            recv[l + 1]["in_t"] = got[0]
        dw_down, _ = _matmul_tn(s["a"], dx, t1=D_FF // 2, tn=tn, name=f"dw_down_{l}")
        dw_up_t, got = _matmul_tn(dup, s["h2"], t1=2 * D_FF // 4, tn=tn, name=f"dw_up_{l}", pieces=2,
                                  carry=_Exchange([_parts(dw_down)]))
        recv[l]["down"] = got[0]
        (dx1, sheet), got_left = _matmul_norm_backward(
            dup, p["up_t"], s["x1"], norm2_g[l], dx, sheet, SG_LAYER * l + ROW_NORM2, tm=tm, name=f"bwd_up_{l}",
            carry=_Exchange([_parts(dw_up_t[0])]))
        dw_out, _ = _matmul_tn(s["mg"], dx1, t1=D_MODEL, tn=tn, name=f"dw_out_{l}")
        (dz, dpa, dpb, small_dws[l], small_dbs[l], sheet), got_right = _mix_backward(
            dx1, s["z"], s["conv"], s["pa"], s["pb"], b_gate[l], gmlp_ln_g[l], gmlp_ln_b[l], w_spatial[l], w_s_t[l],
            b_spatial[l], w_sc[l], p["out"], p["wb"], sheet, l, tm=tm, name=f"bwd_mix_{l}",
            carry=_Exchange([_parts(dw_up_t[1]), _parts(dw_out)]))
        recv[l]["up_t"], recv[l]["out"] = [got_left[0], got_right[0]], got_right[1]
        dw_bra_t, _ = _matmul_tn(dpa, s["ya"], t1=D_MODEL, tn=tn, name=f"dw_branch_a_{l}")
        dw_brb_t, _ = _matmul_tn(dpb, s["yb"], t1=D_MODEL, tn=tn, name=f"dw_branch_b_{l}")
        carry = _Exchange([_parts(dw_bra_t), _parts(dw_brb_t)])
        if l == 0:
            dbs = jnp.stack([t.reshape(N_HEADS, GMLP_BLOCK) for t in small_dbs]).reshape(DEPTH * N_HEADS, GMLP_BLOCK)
            carry = _Both(carry, _Gather([sheet, small_dws[0], small_dws[1], dbs]))
        dw_in_t, got = _matmul_tn(dz, s["h"], t1=D_IN // 4, tn=tn, name=f"dw_in_{l}", carry=carry)
        recv[l]["bra_t"], recv[l]["brb_t"] = got[:2]
        if l == 0:
            gathered_small = got[2:]
            (dx0, dg1_first), got = _matmul_norm_backward(dz, p["in_t"], s["x"], norm1_g[l], dx1, None, 0, tm=tm,
                                                         name=f"bwd_in_{l}", carry=_Exchange([_parts(dw_in_t)]))
            recv[0]["in_t"] = got[0]
        else:
            (dx0, sheet), _ = _matmul_norm_backward(dz, p["in_t"], s["x"], norm1_g[l], dx1, sheet,
                                                    SG_LAYER * l + ROW_NORM1, tm=tm, name=f"bwd_in_{l}")
            pending_in = _parts(dw_in_t)
        dx = dx0
    grad_x = dx.reshape(x.shape)

    results = {}
    both = lambda key: [recv[l][key] for l in range(DEPTH)]
    blocks = lambda key: [r if isinstance(r, list) else [r] for r in both(key)]
    swap = lambda t: t.transpose(0, 2, 1)
    for key, slab, (w, m, v), tr in [("w_in", "in_t", (w_in, m_w_in, v_w_in), 192),
                                     ("w_ffn_up", "up_t", (w_ffn_up, m_w_ffn_up, v_w_ffn_up), 176)]:
        outs = _sum_adamw(blocks(slab), swap(w), swap(m), swap(v), tr=tr, name=f"adamw_{key}")
        results[key] = tuple(swap(o) for o in outs)
    g_bra = _sum_parts(both("bra_t"), tr=128, name="sum_w_branch_a").transpose(0, 2, 1)
    g_brb = _sum_parts(both("brb_t"), tr=128, name="sum_w_branch_b").transpose(0, 2, 1)
    g_br = jnp.stack([g_bra, g_brb], axis=1)
    flat = lambda t: t.reshape(-1, t.shape[-1])
    outs = _adamw(flat(w_branch), flat(g_br), flat(m_w_branch), flat(v_w_branch), tr=512, name="adamw_w_branch")
    results["w_branch"] = (g_br,) + tuple(o.reshape(w_branch.shape) for o in outs)
    results["w_out"] = tuple(_sum_adamw(blocks("out"), w_out, m_w_out, v_w_out, tr=128, name="adamw_w_out"))
    results["w_ffn_down"] = tuple(_sum_adamw(blocks("down"), w_ffn_down, m_w_ffn_down, v_w_ffn_down, tr=176,
                                             name="adamw_w_ffn_down"))

    sheet, dws0, dws1, dbs = _sum_gathered(gathered_small, name="sum_small_grads")
    (dg1_first,) = _all_reduce_small([dg1_first], name="all_reduce_last_gain")
    sheet = sheet.at[ROW_NORM1, :D_MODEL].set(dg1_first[0])
    loss = sheet[ROW_LOSS, 0]
    taps = lambda row, width: jnp.stack([sheet[SG_LAYER * l + row:SG_LAYER * l + row + 3, :width] for l in range(DEPTH)])
    extra = dict(w_spatial=jnp.stack([dws0, dws1]), b_spatial=dbs.reshape(DEPTH, N_HEADS, GMLP_BLOCK),
                 w_shortconv=lax.dynamic_slice_in_dim(taps(ROW_SCONV, D_B), my_idx * sc_w, sc_w, axis=2),
                 w_ffn_conv=lax.dynamic_slice_in_dim(taps(ROW_FCONV, D_FF), my_idx * fc_w, fc_w, axis=2))
    small_w = dict(norm1_g=(norm1_g, m_norm1_g, v_norm1_g), b_gate=(b_gate, m_b_gate, v_b_gate),
                   gmlp_ln_g=(gmlp_ln_g, m_gmlp_ln_g, v_gmlp_ln_g), gmlp_ln_b=(gmlp_ln_b, m_gmlp_ln_b, v_gmlp_ln_b),
                   w_spatial=(w_spatial, m_w_spatial, v_w_spatial), b_spatial=(b_spatial, m_b_spatial, v_b_spatial),
                   w_shortconv=(w_shortconv, m_w_shortconv, v_w_shortconv), norm2_g=(norm2_g, m_norm2_g, v_norm2_g),
                   w_ffn_conv=(w_ffn_conv, m_w_ffn_conv, v_w_ffn_conv), b_ffn_conv=(b_ffn_conv, m_b_ffn_conv, v_b_ffn_conv),
                   final_g=tuple(t.reshape(1, D_MODEL) for t in (final_g, m_final_g, v_final_g)))
    results.update(_adamw_small(sheet, extra, small_w, name="adamw_small"))
    results["final_g"] = tuple(t.reshape(D_MODEL) for t in results["final_g"])

    names = ["norm1_g", "w_in", "b_gate", "gmlp_ln_g", "gmlp_ln_b", "w_spatial", "b_spatial", "w_shortconv", "w_branch",
             "w_out", "norm2_g", "w_ffn_up", "w_ffn_conv", "b_ffn_conv", "w_ffn_down", "final_g"]
    return (loss, grad_x, *[results[k][0] for k in names], *[results[k][1] for k in names],
            *[results[k][2] for k in names], *[results[k][3] for k in names])
```

```python
import math

import jax
import jax.numpy as jnp
from jax import lax
from jax.experimental import pallas as pl
from jax.experimental.pallas import tpu as pltpu

F32 = jnp.float32
BF16 = jnp.bfloat16

N_DEV = 8
DEPTH = 2
D_MODEL = 1024
D_A = 512
D_B = 512
D_FF = 2816
D_IN = 4608
N_HEADS = 4
HEAD = 128
GMLP_BLOCK = 128
CAUSAL_CHUNK = 64
OFF_U, OFF_V, OFF_BG, OFF_CG, OFF_HB, OFF_GA, OFF_GB = 0, 512, 1024, 1536, 2048, 2560, 3584
RMS_EPS = 1e-6
LN_EPS = 1e-5
ADAM_LR, ADAM_B1, ADAM_B2, ADAM_EPS, ADAM_WD, ADAM_STEP = 0.001, 0.9, 0.999, 1e-08, 0.01, 10

SUBLANES, LANES = 8, 128
MATMUL_CHUNK = 512
HALO = 16
FFN_CHUNK = 256
SG_ROWS, SG_W, SG_LAYER = 40, D_FF, 16
ROW_NORM1, ROW_BGATE, ROW_LN_G, ROW_LN_B, ROW_SCONV, ROW_NORM2, ROW_FCONV, ROW_BFCONV = 0, 1, 2, 3, 4, 7, 8, 11
ROW_FINAL, ROW_LOSS = 32, 33
V7X_VMEM_BYTES = 64 << 20
VMEM_LIMIT = V7X_VMEM_BYTES - (8 << 20)
MESH = pl.DeviceIdType.MESH
GELU_C0 = 0.7978845608028654
GELU_C1 = 0.044715
NT = (((1,), (1,)), ((), ()))
TN = (((0,), (0,)), ((), ()))


def _dot(a, b):
    return jnp.dot(a, b, preferred_element_type=F32)


def _dot_nt(a, b):
    return lax.dot_general(a, b, NT, preferred_element_type=F32)


def _sigmoid(x):
    return 1.0 / (1.0 + jnp.exp(-x))


def _gelu_tanh(x):
    return jnp.tanh(GELU_C0 * (x + GELU_C1 * x * x * x))


def _gelu_grad(x, t):
    return 0.5 * (1.0 + t) + 0.5 * x * (1.0 - t * t) * GELU_C0 * (1.0 + 3.0 * GELU_C1 * x * x)


def _sublane_tile(dtype):
    return SUBLANES * (4 // jnp.dtype(dtype).itemsize)


def _shift_down(a, k, prev):
    p = prev.shape[0]
    r = pltpu.roll(a, k, 0)
    sub = _sublane_tile(a.dtype)
    head = r[0:sub]
    rid = lax.broadcasted_iota(jnp.int32, head.shape, 0)
    for j in range(k):
        head = jnp.where(rid == j, prev[p - k + j:p - k + j + 1, :], head)
    return jnp.concatenate([head, r[sub:]], axis=0)


def _shift_up(a, k, nxt):
    t = a.shape[0]
    r = pltpu.roll(a, t - k, 0)
    sub = _sublane_tile(a.dtype)
    tail = r[t - sub:t]
    rid = lax.broadcasted_iota(jnp.int32, tail.shape, 0)
    for j in range(k):
        tail = jnp.where(rid == sub - k + j, nxt[j:j + 1, :], tail)
    return jnp.concatenate([r[0:t - sub], tail], axis=0)


def _column_sums(p):
    if p.dtype.itemsize < 4:
        t = p.shape[0]
        p = p[:t // 2] + p[t // 2:]
        p = p[:t // 4] + p[t // 4:]
    return jnp.sum(p.astype(F32), axis=0, keepdims=True)


def _sheet_begin(step, sheet_in, sheet_out, first_row, rows):
    @pl.when(step == 0)
    def _():
        sheet_out[...] = sheet_in[...]
        sheet_out[first_row:first_row + rows, :] = jnp.zeros((rows, SG_W), F32)


def _sheet_spec():
    return pl.BlockSpec((SG_ROWS, SG_W), lambda i: (0, 0))


def _spatial_mask(transposed):
    ri = lax.broadcasted_iota(jnp.int32, (GMLP_BLOCK, GMLP_BLOCK), 0) // CAUSAL_CHUNK
    ci = lax.broadcasted_iota(jnp.int32, (GMLP_BLOCK, GMLP_BLOCK), 1) // CAUSAL_CHUNK
    return (ri <= ci) if transposed else (ci <= ri)


def _gmlp_forward(u, v, ln_g, ln_b, ws_ref, bs_ref, f_scr):
    tm = u.shape[0]
    tu = _gelu_tanh(u)
    tv = _gelu_tanh(v)
    gu = 0.5 * u * (1.0 + tu)
    gv = 0.5 * v * (1.0 + tv)
    mu = jnp.mean(gv, axis=-1, keepdims=True)
    cen = gv - mu
    rstd = lax.rsqrt(jnp.mean(cen * cen, axis=-1, keepdims=True) + LN_EPS)
    xh = cen * rstd
    vn = (xh * ln_g + ln_b).astype(BF16)
    mask = _spatial_mask(False)
    wm = [jnp.where(mask, ws_ref[h], 0.0).astype(BF16) for h in range(N_HEADS)]
    for b in range(tm // GMLP_BLOCK):
        rows = slice(b * GMLP_BLOCK, (b + 1) * GMLP_BLOCK)
        for h in range(N_HEADS):
            cols = slice(h * HEAD, (h + 1) * HEAD)
            f_scr[rows, cols] = (_dot(wm[h], vn[rows, cols]) + bs_ref[h]).astype(f_scr.dtype)
    return gu, tu, tv, xh, rstd, vn, f_scr[...]


def _position():
    return lax.axis_index("x"), lax.axis_index("y"), lax.axis_index("c")


def _handshake(peers):
    barrier = pltpu.get_barrier_semaphore()
    for peer in peers:
        pl.semaphore_signal(barrier, inc=1, device_id=peer, device_id_type=MESH)
    pl.semaphore_wait(barrier, len(peers))


class _Gather:
    collective_id = 1

    def __init__(self, arrays):
        self.arrays = list(arrays)
        self.out_shape = [jax.ShapeDtypeStruct((N_DEV,) + a.shape, a.dtype) for a in self.arrays]
        self.base = 0

    def barrier(self):
        x, y, c = _position()
        _handshake([(x, y, 1 - c), (1 - x, y, c), (x, 1 - y, c), (1 - x, 1 - y, c)])

    def _plan(self, ins, outs, sems):
        send_sems, recv_sems, local_sems = sems
        x, y, c = _position()
        me, sibling = (x, y, c), (x, y, 1 - c)
        chips = [(1 - x, y), (x, 1 - y), (1 - x, 1 - y)]

        def slot(a, p):
            return outs[a].at[4 * p[0] + 2 * p[1] + p[2]]

        def copy(a, k, block, to, src=None):
            return pltpu.make_async_remote_copy(
                src_ref=slot(a, block) if src is None else src, dst_ref=slot(a, block),
                send_sem=send_sems.at[self.base + a, k], recv_sem=recv_sems.at[self.base + a, k],
                device_id=to, device_id_type=MESH)

        n = len(self.arrays)

        def mine():
            return [pltpu.make_async_copy(ins[a], slot(a, me), local_sems.at[self.base + a]) for a in range(n)]

        def first():
            out = []
            for a in range(n):
                out.append(copy(a, 0, me, sibling, src=ins[a]))
                out += [copy(a, 1 + j, me, (*chip, c), src=ins[a]) for j, chip in enumerate(chips)]
            return out

        def arrivals():
            return [copy(a, 1 + j, (*chip, c), me) for j, chip in enumerate(chips) for a in range(n)]

        def relays():
            return [copy(a, 4 + j, (*chip, c), sibling) for j, chip in enumerate(chips) for a in range(n)]

        def from_sibling():
            out = [copy(a, 0, sibling, me) for a in range(n)]
            return out + [copy(a, 4 + j, (*chip, 1 - c), me) for j, chip in enumerate(chips) for a in range(n)]

        return mine, first, arrivals, relays, from_sibling

    def start(self, ins, outs, sems):
        mine, first, _, _, _ = self._plan(ins, outs, sems)
        for cp in mine() + first():
            cp.start()

    def relay(self, ins, outs, sems):
        _, _, arrivals, relays, _ = self._plan(ins, outs, sems)
        for arrived, onward in zip(arrivals(), relays()):
            arrived.wait_recv()
            onward.start()

    def finish(self, ins, outs, sems):
        mine, first, _, relays, from_sibling = self._plan(ins, outs, sems)
        for cp in from_sibling():
            cp.wait_recv()
        for cp in first() + relays():
            cp.wait_send()
        for cp in mine():
            cp.wait()


class _Exchange:
    collective_id = 0

    def __init__(self, arrays):
        self.arrays = list(arrays)
        self.out_shape = [jax.ShapeDtypeStruct(a.shape, a.dtype) for a in self.arrays]
        self.base = 0

    def barrier(self):
        x, y, c = _position()
        _handshake([(x ^ dx, y ^ dy, c ^ dc) for dx in (0, 1) for dy in (0, 1) for dc in (0, 1) if dx or dy or dc])

    def _plan(self, ins, outs, sems):
        send_sems, recv_sems, local_sems = sems
        x, y, c = _position()
        my_idx = 4 * x + 2 * y + c
        n = len(self.arrays)
        offsets = [(dx, dy, dc) for dx in (0, 1) for dy in (0, 1) for dc in (0, 1) if (dx, dy, dc) != (0, 0, 0)]

        def mine():
            return [pltpu.make_async_copy(ins[a].at[my_idx], outs[a].at[my_idx], local_sems.at[self.base + a])
                    for a in range(n)]

        def remote(arriving):
            out = []
            for k, (dx, dy, dc) in enumerate(offsets):
                px, py, pc = x ^ dx, y ^ dy, c ^ dc
                p_idx = 4 * px + 2 * py + pc
                for a in range(n):
                    out.append(pltpu.make_async_remote_copy(
                        src_ref=ins[a].at[p_idx], dst_ref=outs[a].at[p_idx if arriving else my_idx],
                        send_sem=send_sems.at[self.base + a, k], recv_sem=recv_sems.at[self.base + a, k],
                        device_id=(px, py, pc), device_id_type=MESH))
            return out

        return mine, remote

    def start(self, ins, outs, sems):
        mine, remote = self._plan(ins, outs, sems)
        for cp in mine() + remote(False):
            cp.start()

    def relay(self, ins, outs, sems):
        pass

    def finish(self, ins, outs, sems):
        mine, remote = self._plan(ins, outs, sems)
        for cp in remote(True):
            cp.wait_recv()
        for cp in remote(False):
            cp.wait_send()
        for cp in mine():
            cp.wait()


def _exchange_start(parts, *, name):
    hbm = pl.BlockSpec(memory_space=pltpu.HBM)
    sem = pl.BlockSpec(memory_space=pltpu.SEMAPHORE)
    offsets = [(dx, dy, dc) for dx in (0, 1) for dy in (0, 1) for dc in (0, 1) if dx or dy or dc]

    def body(parts_ref, land_ref, send_sems, recv_sems, parts_thru, land_thru, token):
        x, y, c = _position()
        my_idx = 4 * x + 2 * y + c
        _handshake([(x ^ dx, y ^ dy, c ^ dc) for dx, dy, dc in offsets])
        for k, (dx, dy, dc) in enumerate(offsets):
            px, py, pc = x ^ dx, y ^ dy, c ^ dc
            pltpu.make_async_remote_copy(
                src_ref=parts_ref.at[4 * px + 2 * py + pc], dst_ref=land_ref.at[my_idx],
                send_sem=send_sems.at[k], recv_sem=recv_sems.at[k],
                device_id=(px, py, pc), device_id_type=MESH).start()
        token[...] = jnp.zeros_like(token)

    return pl.pallas_call(
        body, name=name,
        out_shape=(pltpu.SemaphoreType.DMA((7,)), pltpu.SemaphoreType.DMA((7,)), pltpu.HBM(parts.shape, parts.dtype),
                   pltpu.HBM(parts.shape, parts.dtype), jax.ShapeDtypeStruct((SUBLANES, LANES), F32)),
        in_specs=(hbm, hbm), out_specs=(sem, sem, hbm, hbm, pl.BlockSpec(memory_space=pltpu.VMEM)),
        input_output_aliases={0: 2, 1: 3},
        compiler_params=pltpu.CompilerParams(has_side_effects=pltpu.SideEffectType.DATAFLOW_SIDE_EFFECTING,
                                             collective_id=2),
    )(pltpu.with_memory_space_constraint(parts, pltpu.HBM), pltpu.with_memory_space_constraint(parts, pltpu.HBM))


def _exchange_wait(send_sems, recv_sems, parts_thru, land_thru, after, *, name):
    hbm = pl.BlockSpec(memory_space=pltpu.HBM)
    sem = pl.BlockSpec(memory_space=pltpu.SEMAPHORE)
    offsets = [(dx, dy, dc) for dx in (0, 1) for dy in (0, 1) for dc in (0, 1) if dx or dy or dc]

    def body(parts_ref, land_ref, send_sems, recv_sems, after_ref, parts_dead, got_ref):
        x, y, c = _position()
        for k, (dx, dy, dc) in enumerate(offsets):
            px, py, pc = x ^ dx, y ^ dy, c ^ dc
            p_idx = 4 * px + 2 * py + pc
            copy = pltpu.make_async_remote_copy(
                src_ref=parts_ref.at[p_idx], dst_ref=land_ref.at[p_idx], send_sem=send_sems.at[k],
                recv_sem=recv_sems.at[k], device_id=(px, py, pc), device_id_type=MESH)
            copy.wait_send()
            copy.wait_recv()

    return pl.pallas_call(
        body, name=name,
        out_shape=(pltpu.HBM(parts_thru.shape, parts_thru.dtype), pltpu.HBM(land_thru.shape, land_thru.dtype)),
        in_specs=(hbm, hbm, sem, sem, pl.BlockSpec(memory_space=pl.ANY)), out_specs=(hbm, hbm),
        input_output_aliases={0: 0, 1: 1},
        compiler_params=pltpu.CompilerParams(has_side_effects=pltpu.SideEffectType.DATAFLOW_SIDE_EFFECTING),
    )(parts_thru, land_thru, send_sems, recv_sems, after)[1]


class _Both:
    def __init__(self, *carries):
        self.carries = carries
        self.arrays = [a for c in carries for a in c.arrays]
        self.out_shape = [s for c in carries for s in c.out_shape]
        first = 0
        for c in carries:
            c.base = first
            first += len(c.arrays)
        self.collective_id = min(c.collective_id for c in carries)

    def barrier(self):
        min(self.carries, key=lambda c: c.collective_id).barrier()

    def _each(self, method, ins, outs, sems):
        for c in self.carries:
            rows = slice(c.base, c.base + len(c.arrays))
            getattr(c, method)(ins[rows], outs[rows], sems)

    def start(self, ins, outs, sems):
        self._each("start", ins, outs, sems)

    def relay(self, ins, outs, sems):
        self._each("relay", ins, outs, sems)

    def finish(self, ins, outs, sems):
        self._each("finish", ins, outs, sems)


def _call(body, *, name, grid, in_specs, out_specs, out_shape, args, scratch_shapes=(), carry=None):
    n_in, n_out, n_scr = len(in_specs), len(out_specs), len(scratch_shapes)
    params = pltpu.CompilerParams(dimension_semantics=("arbitrary",) * len(grid), vmem_limit_bytes=VMEM_LIMIT)
    if carry is None:
        outs = pl.pallas_call(body, name=name, grid=grid, in_specs=in_specs, out_specs=out_specs, out_shape=out_shape,
                              scratch_shapes=list(scratch_shapes), compiler_params=params)(*args)
        return outs, None
    m = len(carry.arrays)
    total = math.prod(grid)

    def wrapped(*refs):
        ins, refs = refs[:n_in], refs[n_in:]
        c_ins, refs = refs[:m], refs[m:]
        outs, refs = refs[:n_out], refs[n_out:]
        c_outs, refs = refs[:m], refs[m:]
        scr, sems = refs[:n_scr], refs[n_scr:]
        flat = pl.program_id(0)
        for d in range(1, len(grid)):
            flat = flat * grid[d] + pl.program_id(d)

        @pl.when(flat == 0)
        def _():
            carry.barrier()
            carry.start(c_ins, c_outs, sems)

        body(*ins, *outs, *scr)

        @pl.when(flat == total - 2)
        def _():
            carry.relay(c_ins, c_outs, sems)

        @pl.when(flat == total - 1)
        def _():
            carry.finish(c_ins, c_outs, sems)

    any_spec = pl.BlockSpec(memory_space=pl.ANY)
    sem_shapes = [pltpu.SemaphoreType.DMA((m, 7)), pltpu.SemaphoreType.DMA((m, 7)), pltpu.SemaphoreType.DMA((m,))]
    params = pltpu.CompilerParams(dimension_semantics=("arbitrary",) * len(grid), vmem_limit_bytes=VMEM_LIMIT,
                                  collective_id=carry.collective_id)
    outs = pl.pallas_call(
        wrapped, name=name, grid=grid,
        in_specs=list(in_specs) + [any_spec] * m, out_specs=list(out_specs) + [any_spec] * m,
        out_shape=list(out_shape) + carry.out_shape,
        scratch_shapes=list(scratch_shapes) + sem_shapes, compiler_params=params)(*args, *carry.arrays)
    return outs[:n_out], outs[n_out:]


def _gather_now(arrays, *, name):
    carry = _Gather(arrays)
    m = len(arrays)

    def body(*refs):
        ins, outs, sems = refs[:m], refs[m:2 * m], refs[2 * m:]
        carry.barrier()
        carry.start(ins, outs, sems)
        carry.relay(ins, outs, sems)
        carry.finish(ins, outs, sems)

    any_spec = pl.BlockSpec(memory_space=pl.ANY)
    return pl.pallas_call(
        body, name=name, in_specs=[any_spec] * m, out_specs=[any_spec] * m, out_shape=carry.out_shape,
        scratch_shapes=[pltpu.SemaphoreType.DMA((m, 7)), pltpu.SemaphoreType.DMA((m, 7)),
                        pltpu.SemaphoreType.DMA((m,))],
        compiler_params=pltpu.CompilerParams(collective_id=carry.collective_id),
    )(*arrays)


def _all_reduce_small(arrs, *, name):
    n = len(arrs)

    def body(*refs):
        ins, outs, bufs = refs[:n], refs[n:2 * n], refs[2 * n:3 * n]
        send_sems, recv_sems = refs[3 * n:]
        x, y, c = _position()
        me, sibling = (x, y, c), (x, y, 1 - c)
        chips = [(1 - x, y), (x, 1 - y), (1 - x, 1 - y)]
        _handshake([sibling] + [(*chip, c) for chip in chips])

        def copy(a, k, block, to, src=None):
            slot = bufs[a].at[4 * block[0] + 2 * block[1] + block[2]]
            return pltpu.make_async_remote_copy(
                src_ref=slot if src is None else src, dst_ref=slot,
                send_sem=send_sems.at[a, k], recv_sem=recv_sems.at[a, k], device_id=to, device_id_type=MESH)

        first = []
        for a in range(n):
            first.append(copy(a, 0, me, sibling, src=ins[a]))
            first += [copy(a, 1 + j, me, (*chip, c), src=ins[a]) for j, chip in enumerate(chips)]
        for cp in first:
            cp.start()
        passed = []
        for j, chip in enumerate(chips):
            for a in range(n):
                copy(a, 1 + j, (*chip, c), me).wait_recv()
                cp = copy(a, 4 + j, (*chip, c), sibling)
                cp.start()
                passed.append(cp)
        for a in range(n):
            copy(a, 0, sibling, me).wait_recv()
            for j, chip in enumerate(chips):
                copy(a, 4 + j, (*chip, 1 - c), me).wait_recv()
        for cp in first + passed:
            cp.wait_send()
        my_idx = 4 * x + 2 * y + c
        for a in range(n):
            acc = jnp.zeros(ins[a].shape, F32)
            for s in range(N_DEV):
                acc = acc + jnp.where(my_idx == s, ins[a][...], bufs[a][s])
            outs[a][...] = acc

    vmem = pl.BlockSpec(memory_space=pltpu.VMEM)
    return pl.pallas_call(
        body, name=name, in_specs=[vmem] * n, out_specs=[vmem] * n,
        out_shape=[jax.ShapeDtypeStruct(a.shape, F32) for a in arrs],
        scratch_shapes=[pltpu.VMEM((N_DEV,) + a.shape, F32) for a in arrs]
        + [pltpu.SemaphoreType.DMA((n, 7)), pltpu.SemaphoreType.DMA((n, 7))],
        compiler_params=pltpu.CompilerParams(vmem_limit_bytes=VMEM_LIMIT, collective_id=_Gather.collective_id),
    )(*arrs)


def _sum_gathered(arrs, *, name):
    n = len(arrs)

    def body(*refs):
        for in_ref, out_ref in zip(refs[:n], refs[n:]):
            acc = in_ref[0]
            for s in range(1, N_DEV):
                acc = acc + in_ref[s]
            out_ref[...] = acc

    vmem = pl.BlockSpec(memory_space=pltpu.VMEM)
    return pl.pallas_call(
        body, name=name, in_specs=[vmem] * n, out_specs=[vmem] * n,
        out_shape=[jax.ShapeDtypeStruct(a.shape[1:], F32) for a in arrs],
        compiler_params=pltpu.CompilerParams(vmem_limit_bytes=VMEM_LIMIT),
    )(*arrs)


def _norm_matmul(x, g, w_t, *, tm, name, carry=None):
    n, d = x.shape
    c = w_t.shape[0]
    ch = MATMUL_CHUNK

    def body(x_ref, g_ref, wt_ref, h_ref, z_ref):
        xv = x_ref[...]
        r = lax.rsqrt(jnp.mean(xv * xv, axis=-1, keepdims=True) + RMS_EPS)
        h = (xv * r * g_ref[...]).astype(BF16)
        h_ref[...] = h
        for c0 in range(0, c, ch):
            z_ref[:, c0:c0 + ch] = _dot_nt(h, wt_ref[c0:c0 + ch, :]).astype(BF16)

    return _call(
        body, name=name, grid=(n // tm,), carry=carry,
        in_specs=[pl.BlockSpec((tm, d), lambda i: (i, 0)),
                  pl.BlockSpec((1, d), lambda i: (0, 0)),
                  pl.BlockSpec((c, d), lambda i: (0, 0))],
        out_specs=[pl.BlockSpec((tm, d), lambda i: (i, 0)),
                   pl.BlockSpec((tm, c), lambda i: (i, 0))],
        out_shape=[jax.ShapeDtypeStruct((n, d), BF16), jax.ShapeDtypeStruct((n, c), BF16)],
        args=(x, g.reshape(1, d), w_t))


def _mix_forward(z, x, b_gate, ln_g, ln_b, w_s, b_s, w_sc, wb, w_out, *, tm, name, carry=None):
    n = z.shape[0]
    hb = tm // HALO

    def body(z_ref, zp_ref, x_ref, bg_ref, lng_ref, lnb_ref, ws_ref, bs_ref, wsc_ref, wb_ref, wo_ref,
             ya_ref, yb_ref, cv_ref, pa_ref, pb_ref, mg_ref, x1_ref, f_scr):
        i = pl.program_id(0)
        u = z_ref[:, OFF_U:OFF_U + D_A]
        v = z_ref[:, OFF_V:OFF_V + D_A].astype(F32)
        gu, _, _, _, _, _, f = _gmlp_forward(u, v, lng_ref[...], lnb_ref[...], ws_ref, bs_ref, f_scr)
        ya = gu * f
        ya_ref[...] = ya

        q = z_ref[:, OFF_CG:OFF_CG + D_B] * z_ref[:, OFF_HB:OFF_HB + D_B]
        qp = zp_ref[:, OFF_CG:OFF_CG + D_B] * zp_ref[:, OFF_HB:OFF_HB + D_B]
        qp = jnp.where(i > 0, qp, jnp.zeros_like(qp))
        w = wsc_ref[...].astype(BF16)
        conv = w[0:1] * _shift_down(q, 2, qp) + w[1:2] * _shift_down(q, 1, qp) + w[2:3] * q
        cv_ref[...] = conv
        yb = z_ref[:, OFF_BG:OFF_BG + D_B] * conv
        yb_ref[...] = yb

        pa = _dot(ya, wb_ref[0]).astype(BF16)
        pb = _dot(yb, wb_ref[1]).astype(BF16)
        pa_ref[...] = pa
        pb_ref[...] = pb
        bg = bg_ref[...].astype(BF16)
        sa = _sigmoid(z_ref[:, OFF_GA:OFF_GA + D_MODEL] + bg[:, 0:D_MODEL])
        sb = _sigmoid(z_ref[:, OFF_GB:OFF_GB + D_MODEL] + bg[:, D_MODEL:2 * D_MODEL])
        mg = sa * pa + sb * pb
        mg_ref[...] = mg
        x1_ref[...] = x_ref[...] + _dot(mg, wo_ref[...])

    row = lambda w: pl.BlockSpec((tm, w), lambda i: (i, 0))
    full = lambda *s: pl.BlockSpec(s, lambda i: (0,) * len(s))
    bf = lambda w: jax.ShapeDtypeStruct((n, w), BF16)
    return _call(
        body, name=name, grid=(n // tm,), carry=carry,
        in_specs=[row(D_IN),
                  pl.BlockSpec((HALO, D_IN), lambda i: (jnp.maximum(i * hb - 1, 0), 0)),
                  row(D_MODEL), full(1, 2 * D_MODEL), full(1, D_A), full(1, D_A),
                  full(N_HEADS, GMLP_BLOCK, GMLP_BLOCK), full(N_HEADS, GMLP_BLOCK, 1), full(3, D_B),
                  full(2, D_A, D_MODEL), full(D_MODEL, D_MODEL)],
        out_specs=[row(D_A), row(D_B), row(D_B), row(D_MODEL), row(D_MODEL), row(D_MODEL), row(D_MODEL)],
        out_shape=[bf(D_A), bf(D_B), bf(D_B), bf(D_MODEL), bf(D_MODEL), bf(D_MODEL),
                   jax.ShapeDtypeStruct((n, D_MODEL), F32)],
        scratch_shapes=[pltpu.VMEM((tm, D_A), BF16)],
        args=(z, z, x, b_gate.reshape(1, -1), ln_g.reshape(1, -1), ln_b.reshape(1, -1), w_s,
              b_s.reshape(N_HEADS, GMLP_BLOCK, 1), w_sc, wb, w_out))


def _loss_tile(xv, gv, tv):
    d = xv.shape[-1]
    r = lax.rsqrt(jnp.mean(xv * xv, axis=-1, keepdims=True) + RMS_EPS)
    xh = xv * r
    e = xh * gv - tv
    per_row = jnp.sum(e * e, axis=-1, keepdims=True) * (0.5 / d)
    dy = e * (1.0 / d)
    dxh = dy * gv
    dx = r * (dxh - xh * jnp.mean(dxh * xh, axis=-1, keepdims=True))
    return dx, jnp.sum(per_row, axis=0, keepdims=True), jnp.sum(dy * xh, axis=0, keepdims=True)


def _ffn_forward(up, x1, w_fc, b_fc, w_down, *, tm, name, carry=None, head=None):
    n = up.shape[0]
    hb = tm // HALO
    n_in = 6 if head is None else 8

    def body(*refs):
        up_ref, upp_ref, x1_ref, wfc_ref, bfc_ref, wd_ref = refs[:6]
        gc_ref, a_ref, out_ref = refs[n_in:n_in + 3]
        acc = refs[-1]
        i = pl.program_id(0)
        acc[...] = x1_ref[...]
        for c0 in range(0, D_FF, FFN_CHUNK):
            cols = slice(c0, c0 + FFN_CHUNK)
            gate = up_ref[:, cols]
            val = up_ref[:, D_FF + c0:D_FF + c0 + FFN_CHUNK]
            gp = upp_ref[:, cols]
            gp = jnp.where(i > 0, gp, jnp.zeros_like(gp))
            w = wfc_ref[:, cols].astype(BF16)
            gc = (w[0:1] * _shift_down(gate, 2, gp) + w[1:2] * _shift_down(gate, 1, gp) + w[2:3] * gate
                  + bfc_ref[:, cols].astype(BF16))
            gc_ref[:, cols] = gc
            a = gc * _sigmoid(gc) * val
            a_ref[:, cols] = a
            acc[...] += _dot(a, wd_ref[cols, :])
        if head is None:
            out_ref[...] = acc[...]
        else:
            g_ref, t_ref = refs[6:8]
            sg_ref = refs[n_in + 3]

            @pl.when(i == 0)
            def _():
                sg_ref[...] = jnp.zeros_like(sg_ref)

            dx, loss, dg = _loss_tile(acc[...], g_ref[...], t_ref[...])
            out_ref[...] = dx
            sg_ref[ROW_LOSS:ROW_LOSS + 1, 0:LANES] += jnp.broadcast_to(loss, (1, LANES))
            sg_ref[ROW_FINAL:ROW_FINAL + 1, 0:D_MODEL] += dg

    row = lambda w: pl.BlockSpec((tm, w), lambda i: (i, 0))
    full = lambda r, c: pl.BlockSpec((r, c), lambda i: (0, 0))
    in_specs = [row(2 * D_FF), pl.BlockSpec((HALO, D_FF), lambda i: (jnp.maximum(i * hb - 1, 0), 0)), row(D_MODEL),
                full(3, D_FF), full(1, D_FF), full(D_FF, D_MODEL)]
    out_specs = [row(D_FF), row(D_FF), row(D_MODEL)]
    out_shape = [jax.ShapeDtypeStruct((n, D_FF), BF16), jax.ShapeDtypeStruct((n, D_FF), BF16),
                 jax.ShapeDtypeStruct((n, D_MODEL), F32)]
    args = (up, up, x1, w_fc, b_fc.reshape(1, -1), w_down)
    if head is not None:
        in_specs += [full(1, D_MODEL), row(D_MODEL)]
        out_specs += [full(SG_ROWS, SG_W)]
        out_shape += [jax.ShapeDtypeStruct((SG_ROWS, SG_W), F32)]
        args += (head[0].reshape(1, -1), head[1])
    return _call(body, name=name, grid=(n // tm,), carry=carry, in_specs=in_specs, out_specs=out_specs,
                 out_shape=out_shape, scratch_shapes=[pltpu.VMEM((tm, D_MODEL), F32)], args=args)


def _ffn_backward(dx2, up, gc, w_fc, w_down, sheet, layer, *, tm, name, carry=None):
    n = up.shape[0]
    steps = n // tm
    hb = tm // HALO
    row = SG_LAYER * layer + ROW_FCONV

    def body(dx_ref, dxn_ref, up_ref, upn_ref, gc_ref, gcn_ref, wfc_ref, wd_ref, sg_in, dup_ref, sg_ref):
        i = pl.program_id(0)
        last = i == steps - 1
        _sheet_begin(i, sg_in, sg_ref, row, 4)

        dxe = jnp.concatenate([dx_ref[...], dxn_ref[...]], axis=0).astype(BF16)
        for c0 in range(0, D_FF, FFN_CHUNK):
            cols = slice(c0, c0 + FFN_CHUNK)
            vcols = slice(D_FF + c0, D_FF + c0 + FFN_CHUNK)
            dae = _dot_nt(dxe, wd_ref[cols, :])
            da, dan = dae[:tm], dae[tm:]
            gate = up_ref[:, cols]
            val = up_ref[:, vcols]
            gcv = gc_ref[:, cols]
            s = _sigmoid(gcv)
            dab = da.astype(BF16)
            dup_ref[:, vcols] = dab * (gcv * s)
            dgc = dab * val * (s * (1.0 + gcv * (1.0 - s)))
            gcn = gcn_ref[:, cols]
            sn = _sigmoid(gcn)
            dgcn = dan.astype(BF16) * upn_ref[:, vcols] * (sn * (1.0 + gcn * (1.0 - sn)))
            dgcn = jnp.where(last, jnp.zeros_like(dgcn), dgcn)
            up1 = _shift_up(dgc, 1, dgcn)
            up2 = _shift_up(dgc, 2, dgcn)
            w = wfc_ref[:, cols].astype(BF16)
            dup_ref[:, cols] = w[2:3] * dgc + w[1:2] * up1 + w[0:1] * up2
            sg_ref[row:row + 1, cols] += _column_sums(gate * up2)
            sg_ref[row + 1:row + 2, cols] += _column_sums(gate * up1)
            sg_ref[row + 2:row + 3, cols] += _column_sums(gate * dgc)
            sg_ref[row + 3:row + 4, cols] += _column_sums(dgc)

    nxt = lambda i: (jnp.minimum((i + 1) * hb, steps * hb - 1), 0)
    return _call(
        body, name=name, grid=(steps,), carry=carry,
        in_specs=[pl.BlockSpec((tm, D_MODEL), lambda i: (i, 0)),
                  pl.BlockSpec((HALO, D_MODEL), nxt),
                  pl.BlockSpec((tm, 2 * D_FF), lambda i: (i, 0)),
                  pl.BlockSpec((HALO, 2 * D_FF), nxt),
                  pl.BlockSpec((tm, D_FF), lambda i: (i, 0)),
                  pl.BlockSpec((HALO, D_FF), nxt),
                  pl.BlockSpec((3, D_FF), lambda i: (0, 0)),
                  pl.BlockSpec((D_FF, D_MODEL), lambda i: (0, 0)), _sheet_spec()],
        out_specs=[pl.BlockSpec((tm, 2 * D_FF), lambda i: (i, 0)), _sheet_spec()],
        out_shape=[jax.ShapeDtypeStruct((n, 2 * D_FF), BF16), jax.ShapeDtypeStruct((SG_ROWS, SG_W), F32)],
        args=(dx2, dx2, up, up, gc, gc, w_fc, w_down, sheet))


def _matmul_norm_backward(dz, w_t, x, g, dres, sheet, row, *, tm, name, carry=None):
    n, c = dz.shape
    d = x.shape[1]
    ch = MATMUL_CHUNK

    def body(dz_ref, wt_ref, x_ref, g_ref, dres_ref, *rest):
        i = pl.program_id(0)
        if sheet is None:
            dx_ref, sg_ref = rest

            @pl.when(i == 0)
            def _():
                sg_ref[...] = jnp.zeros_like(sg_ref)
        else:
            sg_in, dx_ref, sg_ref = rest
            _sheet_begin(i, sg_in, sg_ref, row, 1)

        dh = _dot(dz_ref[:, 0:ch], wt_ref[0:ch, :])
        for c0 in range(ch, c, ch):
            dh += _dot(dz_ref[:, c0:c0 + ch], wt_ref[c0:c0 + ch, :])
        xv = x_ref[...]
        r = lax.rsqrt(jnp.mean(xv * xv, axis=-1, keepdims=True) + RMS_EPS)
        xh = xv * r
        sg_ref[row:row + 1, 0:d] += jnp.sum(dh * xh, axis=0, keepdims=True)
        dxh = dh * g_ref[...]
        dx_ref[...] = dres_ref[...] + r * (dxh - xh * jnp.mean(dxh * xh, axis=-1, keepdims=True))

    in_specs = [pl.BlockSpec((tm, c), lambda i: (i, 0)),
                pl.BlockSpec((c, d), lambda i: (0, 0)),
                pl.BlockSpec((tm, d), lambda i: (i, 0)),
                pl.BlockSpec((1, d), lambda i: (0, 0)),
                pl.BlockSpec((tm, d), lambda i: (i, 0))]
    args = (dz, w_t, x, g.reshape(1, d), dres)
    if sheet is None:
        small_spec, small_shape = pl.BlockSpec((8, d), lambda i: (0, 0)), jax.ShapeDtypeStruct((8, d), F32)
    else:
        in_specs, args = in_specs + [_sheet_spec()], args + (sheet,)
        small_spec, small_shape = _sheet_spec(), jax.ShapeDtypeStruct((SG_ROWS, SG_W), F32)
    return _call(
        body, name=name, grid=(n // tm,), carry=carry, in_specs=in_specs,
        out_specs=[pl.BlockSpec((tm, d), lambda i: (i, 0)), small_spec],
        out_shape=[jax.ShapeDtypeStruct((n, d), F32), small_shape], args=args)


def _mix_backward(dx1, z, conv, pa, pb, b_gate, ln_g, ln_b, w_s, w_s_t, b_s, w_sc, w_out, wb, sheet, layer, *, tm, name,
                  carry=None):
    n = z.shape[0]
    steps = n // tm
    hb = tm // HALO
    base = SG_LAYER * layer
    r_bg, r_lng, r_lnb, r_sc = base + ROW_BGATE, base + ROW_LN_G, base + ROW_LN_B, base + ROW_SCONV

    def body(dx_ref, dxn_ref, z_ref, zn_ref, cv_ref, pa_ref, pb_ref, bg_ref, lng_ref, lnb_ref, ws_ref, wst_ref,
             bs_ref, wsc_ref, wo_ref, wb_ref, sg_in,
             dz_ref, dpa_ref, dpb_ref, dws_ref, dbs_ref, sg_ref, f_scr, dvn_scr):
        i = pl.program_id(0)
        last = i == steps - 1
        _sheet_begin(i, sg_in, sg_ref, r_bg, ROW_NORM2 - ROW_BGATE)

        @pl.when(i == 0)
        def _():
            dws_ref[...] = jnp.zeros_like(dws_ref)
            dbs_ref[...] = jnp.zeros_like(dbs_ref)

        dxe = jnp.concatenate([dx_ref[...], dxn_ref[...]], axis=0).astype(BF16)
        dmge = _dot_nt(dxe, wo_ref[...])
        dmg, dmgn = dmge[:tm].astype(BF16), dmge[tm:].astype(BF16)

        pa_v = pa_ref[...]
        pb_v = pb_ref[...]
        bg = bg_ref[...].astype(BF16)
        sa = _sigmoid(z_ref[:, OFF_GA:OFF_GA + D_MODEL] + bg[:, 0:D_MODEL])
        sb = _sigmoid(z_ref[:, OFF_GB:OFF_GB + D_MODEL] + bg[:, D_MODEL:2 * D_MODEL])
        dpa = dmg * sa
        dpb = dmg * sb
        dga = dmg * pa_v * sa * (1.0 - sa)
        dgb = dmg * pb_v * sb * (1.0 - sb)
        dpa_ref[...] = dpa
        dpb_ref[...] = dpb
        dz_ref[:, OFF_GA:OFF_GA + D_MODEL] = dga
        dz_ref[:, OFF_GB:OFF_GB + D_MODEL] = dgb
        sg_ref[r_bg:r_bg + 1, 0:D_MODEL] += _column_sums(dga)
        sg_ref[r_bg:r_bg + 1, D_MODEL:2 * D_MODEL] += _column_sums(dgb)

        dya = _dot_nt(dpa, wb_ref[0]).astype(BF16)
        u = z_ref[:, OFF_U:OFF_U + D_A]
        v = z_ref[:, OFF_V:OFF_V + D_A].astype(F32)
        ln_g = lng_ref[...]
        gu, tu, tv, xh, rstd, vn, f = _gmlp_forward(u, v, ln_g, lnb_ref[...], ws_ref, bs_ref, f_scr)
        dgu = dya * f
        df_bf = dya * gu
        dz_ref[:, OFF_U:OFF_U + D_A] = dgu * _gelu_grad(u, tu)
        mask = _spatial_mask(False)
        mask_t = _spatial_mask(True)
        wmt = [jnp.where(mask_t, wst_ref[h], 0.0).astype(BF16) for h in range(N_HEADS)]
        for b in range(tm // GMLP_BLOCK):
            rows = slice(b * GMLP_BLOCK, (b + 1) * GMLP_BLOCK)
            for h in range(N_HEADS):
                cols = slice(h * HEAD, (h + 1) * HEAD)
                dfb = df_bf[rows, cols]
                dvn_scr[rows, cols] = _dot(wmt[h], dfb)
                dws_ref[h] += jnp.where(mask, _dot_nt(dfb, vn[rows, cols]), 0.0)
                dbs_ref[h] += jnp.sum(dfb.astype(F32), axis=1, keepdims=True)
        dvn = dvn_scr[...]
        sg_ref[r_lng:r_lng + 1, 0:D_A] += jnp.sum(dvn * xh, axis=0, keepdims=True)
        sg_ref[r_lnb:r_lnb + 1, 0:D_A] += jnp.sum(dvn, axis=0, keepdims=True)
        dxh = dvn * ln_g
        dgv = rstd * (dxh - jnp.mean(dxh, axis=-1, keepdims=True) - xh * jnp.mean(dxh * xh, axis=-1, keepdims=True))
        dz_ref[:, OFF_V:OFF_V + D_A] = (dgv * _gelu_grad(v, tv)).astype(BF16)

        sbn = _sigmoid(zn_ref[:, OFF_GB:OFF_GB + D_MODEL] + bg[:, D_MODEL:2 * D_MODEL])
        dpbe = jnp.concatenate([dpb, dmgn * sbn], axis=0)
        dybe = _dot_nt(dpbe, wb_ref[1])
        dyb, dybn = dybe[:tm].astype(BF16), dybe[tm:].astype(BF16)
        bgv = z_ref[:, OFF_BG:OFF_BG + D_B]
        cg = z_ref[:, OFF_CG:OFF_CG + D_B]
        hbv = z_ref[:, OFF_HB:OFF_HB + D_B]
        q = cg * hbv
        dz_ref[:, OFF_BG:OFF_BG + D_B] = dyb * cv_ref[...]
        dconv = dyb * bgv
        dconvn = dybn * zn_ref[:, OFF_BG:OFF_BG + D_B]
        dconvn = jnp.where(last, jnp.zeros_like(dconvn), dconvn)
        up1 = _shift_up(dconv, 1, dconvn)
        up2 = _shift_up(dconv, 2, dconvn)
        sg_ref[r_sc:r_sc + 1, 0:D_B] += _column_sums(q * up2)
        sg_ref[r_sc + 1:r_sc + 2, 0:D_B] += _column_sums(q * up1)
        sg_ref[r_sc + 2:r_sc + 3, 0:D_B] += _column_sums(q * dconv)
        w = wsc_ref[...].astype(BF16)
        dq = w[2:3] * dconv + w[1:2] * up1 + w[0:1] * up2
        dz_ref[:, OFF_CG:OFF_CG + D_B] = dq * hbv
        dz_ref[:, OFF_HB:OFF_HB + D_B] = dq * cg

    row = lambda w: pl.BlockSpec((tm, w), lambda i: (i, 0))
    full = lambda *s: pl.BlockSpec(s, lambda i: (0,) * len(s))
    nxt = lambda i: (jnp.minimum((i + 1) * hb, steps * hb - 1), 0)
    return _call(
        body, name=name, grid=(steps,), carry=carry,
        in_specs=[row(D_MODEL), pl.BlockSpec((HALO, D_MODEL), nxt),
                  row(D_IN), pl.BlockSpec((HALO, D_IN), nxt),
                  row(D_B), row(D_MODEL), row(D_MODEL),
                  full(1, 2 * D_MODEL), full(1, D_A), full(1, D_A),
                  full(N_HEADS, GMLP_BLOCK, GMLP_BLOCK), full(N_HEADS, GMLP_BLOCK, GMLP_BLOCK),
                  full(N_HEADS, GMLP_BLOCK, 1), full(3, D_B),
                  full(D_MODEL, D_MODEL), full(2, D_A, D_MODEL), _sheet_spec()],
        out_specs=[row(D_IN), row(D_MODEL), row(D_MODEL), full(N_HEADS, GMLP_BLOCK, GMLP_BLOCK),
                   full(N_HEADS, GMLP_BLOCK, 1), _sheet_spec()],
        out_shape=[jax.ShapeDtypeStruct((n, D_IN), BF16), jax.ShapeDtypeStruct((n, D_MODEL), BF16),
                   jax.ShapeDtypeStruct((n, D_MODEL), BF16),
                   jax.ShapeDtypeStruct((N_HEADS, GMLP_BLOCK, GMLP_BLOCK), F32),
                   jax.ShapeDtypeStruct((N_HEADS, GMLP_BLOCK, 1), F32), jax.ShapeDtypeStruct((SG_ROWS, SG_W), F32)],
        scratch_shapes=[pltpu.VMEM((tm, D_A), BF16), pltpu.VMEM((tm, D_A), F32)],
        args=(dx1, dx1, z, z, conv, pa, pb, b_gate.reshape(1, -1), ln_g.reshape(1, -1), ln_b.reshape(1, -1), w_s, w_s_t,
              b_s.reshape(N_HEADS, GMLP_BLOCK, 1), w_sc, w_out, wb, sheet))


def _matmul_tn(a, b, *, t1, tn, name, carry=None, pieces=1):
    n, k1 = a.shape
    k2 = b.shape[1]
    steps = n // tn
    w = k2 // pieces

    def body(a_ref, b_ref, *rest):
        o_refs, acc = rest[:pieces], rest[pieces]
        s = pl.program_id(1)

        @pl.when(s == 0)
        def _():
            acc[...] = jnp.zeros_like(acc)

        acc[...] += lax.dot_general(a_ref[...].astype(BF16), b_ref[...].astype(BF16), TN, preferred_element_type=F32)

        @pl.when(s == steps - 1)
        def _():
            for c, o_ref in enumerate(o_refs):
                o_ref[...] = acc[:, c * w:(c + 1) * w].astype(BF16)

    outs, carried = _call(
        body, name=name, grid=(k1 // t1, steps), carry=carry,
        in_specs=[pl.BlockSpec((tn, t1), lambda i, s: (s, i)),
                  pl.BlockSpec((tn, k2), lambda i, s: (s, 0))],
        out_specs=[pl.BlockSpec((t1, w), lambda i, s: (i, 0))] * pieces,
        out_shape=[jax.ShapeDtypeStruct((k1, w), BF16)] * pieces,
        scratch_shapes=[pltpu.VMEM((t1, k2), F32)],
        args=(a, b))
    return (outs[0] if pieces == 1 else list(outs)), carried


def _adamw_math(w, g, m, v):
    m = ADAM_B1 * m + (1.0 - ADAM_B1) * g
    v = ADAM_B2 * v + (1.0 - ADAM_B2) * (g * g)
    m_hat = m / (1.0 - ADAM_B1 ** ADAM_STEP)
    v_hat = v / (1.0 - ADAM_B2 ** ADAM_STEP)
    delta = -ADAM_LR * (m_hat / (jnp.sqrt(v_hat) + ADAM_EPS) + ADAM_WD * w)
    return delta, m, v


def _sum_parts(recvs, *, tr, name):
    _, r, c = recvs[0].shape

    def body(*refs):
        recv_refs, g_ref = refs[:DEPTH], refs[DEPTH]
        layer = pl.program_id(0)
        for l in range(DEPTH):
            @pl.when(layer == l)
            def _(l=l):
                g = recv_refs[l][0].astype(F32)
                for s in range(1, N_DEV):
                    g = g + recv_refs[l][s].astype(F32)
                g_ref[0] = g

    outs, _ = _call(
        body, name=name, grid=(DEPTH, r // tr),
        in_specs=[pl.BlockSpec((N_DEV, tr, c), lambda l, i: (0, i, 0))] * DEPTH,
        out_specs=[pl.BlockSpec((1, tr, c), lambda l, i: (l, i, 0))],
        out_shape=[jax.ShapeDtypeStruct((DEPTH, r, c), F32)],
        args=tuple(recvs))
    return outs[0]


def _adamw(w, g, m, v, *, tr, name):
    r, c = w.shape

    def body(w_ref, g_ref, m_ref, v_ref, d_ref, nm_ref, nv_ref):
        delta, nm, nv = _adamw_math(w_ref[...], g_ref[...], m_ref[...], v_ref[...])
        d_ref[...] = delta
        nm_ref[...] = nm
        nv_ref[...] = nv

    spec = pl.BlockSpec((tr, c), lambda i: (i, 0))
    outs, _ = _call(body, name=name, grid=(r // tr,), in_specs=[spec] * 4, out_specs=[spec] * 3,
                    out_shape=[jax.ShapeDtypeStruct((r, c), F32)] * 3, args=(w, g, m, v))
    return outs


def _sum_adamw(recvs, w, m, v, *, tr, name):
    _, r, c = w.shape
    blocks = len(recvs[0])
    flat = [piece for layer in recvs for piece in layer]

    def body(*refs):
        recv_refs = refs[:len(flat)]
        w_ref, m_ref, v_ref, g_ref, d_ref, nm_ref, nv_ref = refs[len(flat):]
        layer = pl.program_id(0)
        for l in range(DEPTH):
            @pl.when(layer == l)
            def _(l=l):
                cols = []
                for piece in recv_refs[l * blocks:(l + 1) * blocks]:
                    part = piece[0].astype(F32)
                    for s in range(1, N_DEV):
                        part = part + piece[s].astype(F32)
                    cols.append(part)
                g = cols[0] if blocks == 1 else jnp.concatenate(cols, axis=-1)
                delta, nm, nv = _adamw_math(w_ref[0], g, m_ref[0], v_ref[0])
                g_ref[0] = g
                d_ref[0] = delta
                nm_ref[0] = nm
                nv_ref[0] = nv

    spec = pl.BlockSpec((1, tr, c), lambda l, i: (l, i, 0))
    outs, _ = _call(
        body, name=name, grid=(DEPTH, r // tr),
        in_specs=[pl.BlockSpec((N_DEV, tr, c // blocks), lambda l, i: (0, i, 0))] * len(flat) + [spec] * 3,
        out_specs=[spec] * 4, out_shape=[jax.ShapeDtypeStruct((DEPTH, r, c), F32)] * 4,
        args=tuple(flat) + (w, m, v))
    return outs


def _adamw_small(sheet, extra, params, *, name):
    sheet_rows = dict(norm1_g=ROW_NORM1, b_gate=ROW_BGATE, gmlp_ln_g=ROW_LN_G, gmlp_ln_b=ROW_LN_B, norm2_g=ROW_NORM2,
                      b_ffn_conv=ROW_BFCONV)
    names = list(params)
    extra_names = list(extra)

    def body(*refs):
        sg_ref, refs = refs[0], refs[1:]
        extra_refs, refs = dict(zip(extra_names, refs[:len(extra_names)])), refs[len(extra_names):]
        ins, outs = refs[:3 * len(names)], refs[3 * len(names):]
        for j, key in enumerate(names):
            w_ref, m_ref, v_ref = ins[3 * j:3 * j + 3]
            g_ref, d_ref, nm_ref, nv_ref = outs[4 * j:4 * j + 4]
            if key in extra_refs:
                g_ref[...] = extra_refs[key][...]
            elif key == "final_g":
                g_ref[...] = sg_ref[ROW_FINAL:ROW_FINAL + 1, 0:D_MODEL]
            else:
                width = w_ref.shape[-1]
                for l in range(DEPTH):
                    row = SG_LAYER * l + sheet_rows[key]
                    g_ref[l:l + 1, :] = sg_ref[row:row + 1, 0:width]
            delta, nm, nv = _adamw_math(w_ref[...], g_ref[...], m_ref[...], v_ref[...])
            d_ref[...] = delta
            nm_ref[...] = nm
            nv_ref[...] = nv

    args = [sheet] + [extra[k] for k in extra_names] + [t for k in names for t in params[k]]
    vmem = pl.BlockSpec(memory_space=pltpu.VMEM)
    outs = pl.pallas_call(
        body, name=name, in_specs=[vmem] * len(args), out_specs=[vmem] * (4 * len(names)),
        out_shape=[jax.ShapeDtypeStruct(params[k][0].shape, F32) for k in names for _ in range(4)],
    )(*args)
    return {k: tuple(outs[4 * j:4 * j + 4]) for j, k in enumerate(names)}


def _rows(gathered):
    return gathered.reshape(N_DEV * gathered.shape[1], gathered.shape[2])


def _parts(full):
    return full.reshape(N_DEV, full.shape[0] // N_DEV, full.shape[1])


def kernel(x, norm1_g, w_in, b_gate, gmlp_ln_g, gmlp_ln_b, w_spatial, b_spatial, w_shortconv, w_branch, w_out, norm2_g, w_ffn_up, w_ffn_conv, b_ffn_conv, w_ffn_down, final_g, loss_target, m_norm1_g, m_w_in, m_b_gate, m_gmlp_ln_g, m_gmlp_ln_b, m_w_spatial, m_b_spatial, m_w_shortconv, m_w_branch, m_w_out, m_norm2_g, m_w_ffn_up, m_w_ffn_conv, m_b_ffn_conv, m_w_ffn_down, m_final_g, v_norm1_g, v_w_in, v_b_gate, v_gmlp_ln_g, v_gmlp_ln_b, v_w_spatial, v_b_spatial, v_w_shortconv, v_w_branch, v_w_out, v_norm2_g, v_w_ffn_up, v_w_ffn_conv, v_b_ffn_conv, v_w_ffn_down, v_final_g):
    n = x.shape[1]
    tm_in, tm, tn = 1024, 512, 2048
    x0 = x.reshape(n, D_MODEL)
    target = loss_target.reshape(n, D_MODEL)
    my_idx = 4 * lax.axis_index("x") + 2 * lax.axis_index("y") + lax.axis_index("c")
    sc_w, fc_w = D_B // N_DEV, D_FF // N_DEV

    sh_in = [w_in[l].T.astype(BF16) for l in range(DEPTH)]
    sh_up = [w_ffn_up[l].T.astype(BF16) for l in range(DEPTH)]
    sh_br = [w_branch[l].astype(BF16) for l in range(DEPTH)]
    sh_out = [w_out[l].astype(BF16) for l in range(DEPTH)]
    sh_down = [w_ffn_down[l].astype(BF16) for l in range(DEPTH)]
    taps = jnp.concatenate([w_shortconv, w_ffn_conv], axis=-1)

    def branch_weights(g):
        return g.transpose(1, 2, 0, 3).reshape(2, D_A, D_MODEL)

    g_in0, g_taps = _gather_now([sh_in[0], taps], name="gather_first")
    w_sc = [g_taps[:, l, :, :sc_w].transpose(1, 0, 2).reshape(3, D_B) for l in range(DEPTH)]
    w_fc = [g_taps[:, l, :, sc_w:].transpose(1, 0, 2).reshape(3, D_FF) for l in range(DEPTH)]
    w_s_t = [w_spatial[l].transpose(0, 2, 1) for l in range(DEPTH)]
    weights = [dict(), dict()]
    weights[0]["in_t"] = _rows(g_in0)
    saved = []
    xc = x0
    for l in range(DEPTH):
        p = weights[l]
        carry = _Gather([sh_br[0], sh_out[0]] if l == 0 else [sh_up[1]])
        (h, z), got = _norm_matmul(xc, norm1_g[l], p["in_t"], tm=tm_in, name=f"fwd_in_{l}", carry=carry)
        if l == 0:
            p["wb"], p["out"] = branch_weights(got[0]), _rows(got[1])
        else:
            p["up_t"] = _rows(got[0])
        carry = _Gather([sh_up[0]]) if l == 0 else None
        (ya, yb, conv, pa, pb, mg, x1), got = _mix_forward(
            z, xc, b_gate[l], gmlp_ln_g[l], gmlp_ln_b[l], w_spatial[l], b_spatial[l], w_sc[l], p["wb"], p["out"],
            tm=tm, name=f"fwd_mix_{l}", carry=carry)
        if l == 0:
            p["up_t"] = _rows(got[0])
        (h2, up), got = _norm_matmul(x1, norm2_g[l], p["up_t"], tm=tm, name=f"fwd_up_{l}", carry=_Gather([sh_down[l]]))
        p["down"] = _rows(got[0])
        carry = _Gather([sh_br[1], sh_out[1], sh_in[1]]) if l == 0 else None
        head = (final_g, target) if l == DEPTH - 1 else None
        outs, got = _ffn_forward(up, x1, w_fc[l], b_ffn_conv[l], p["down"], tm=tm, name=f"fwd_ffn_{l}", carry=carry, head=head)
        if l == 0:
            weights[1]["wb"], weights[1]["out"], weights[1]["in_t"] = branch_weights(got[0]), _rows(got[1]), _rows(got[2])
        gc, a = outs[0], outs[1]
        saved.append(dict(x=xc, h=h, z=z, ya=ya, yb=yb, conv=conv, pa=pa, pb=pb, mg=mg, x1=x1, h2=h2, up=up, gc=gc, a=a))
        xc = outs[2]
    dx, sheet = outs[2], outs[3]

    recv = [dict(), dict()]
    small_dws, small_dbs = [None] * DEPTH, [None] * DEPTH
    pending_in = None
    for l in reversed(range(DEPTH)):
        p, s = weights[l], saved[l]
        carry = _Exchange([pending_in]) if pending_in is not None else None
        (dup, sheet), got = _ffn_backward(dx, s["up"], s["gc"], w_fc[l], p["down"], sheet, l, tm=tm, name=f"bwd_ffn_{l}",
                                          carry=carry)
        if got is not None:
            recv[l + 1]["in_t"] = got[0]
        dw_down, _ = _matmul_tn(s["a"], dx, t1=D_FF // 2, tn=tn, name=f"dw_down_{l}")
        dw_up_t, got = _matmul_tn(dup, s["h2"], t1=2 * D_FF // 4, tn=tn, name=f"dw_up_{l}", pieces=2,
                                  carry=_Exchange([_parts(dw_down)]))
        recv[l]["down"] = got[0]
        (dx1, sheet), got_left = _matmul_norm_backward(
            dup, p["up_t"], s["x1"], norm2_g[l], dx, sheet, SG_LAYER * l + ROW_NORM2, tm=tm, name=f"bwd_up_{l}",
            carry=_Exchange([_parts(dw_up_t[0])]))
        dw_out, _ = _matmul_tn(s["mg"], dx1, t1=D_MODEL, tn=tn, name=f"dw_out_{l}")
        (dz, dpa, dpb, small_dws[l], small_dbs[l], sheet), got_right = _mix_backward(
            dx1, s["z"], s["conv"], s["pa"], s["pb"], b_gate[l], gmlp_ln_g[l], gmlp_ln_b[l], w_spatial[l], w_s_t[l],
            b_spatial[l], w_sc[l], p["out"], p["wb"], sheet, l, tm=tm, name=f"bwd_mix_{l}",
            carry=_Exchange([_parts(dw_up_t[1]), _parts(dw_out)]))
        recv[l]["up_t"], recv[l]["out"] = [got_left[0], got_right[0]], got_right[1]
        dw_bra_t, _ = _matmul_tn(dpa, s["ya"], t1=D_MODEL, tn=tn, name=f"dw_branch_a_{l}")
        dw_brb_t, _ = _matmul_tn(dpb, s["yb"], t1=D_MODEL, tn=tn, name=f"dw_branch_b_{l}")
        carry = _Exchange([_parts(dw_bra_t), _parts(dw_brb_t)])
        if l == 0:
            dbs = jnp.stack([t.reshape(N_HEADS, GMLP_BLOCK) for t in small_dbs]).reshape(DEPTH * N_HEADS, GMLP_BLOCK)
            carry = _Both(carry, _Gather([sheet, small_dws[0], small_dws[1], dbs]))
        dw_in_t, got = _matmul_tn(dz, s["h"], t1=D_IN // 4, tn=tn, name=f"dw_in_{l}", carry=carry)
        recv[l]["bra_t"], recv[l]["brb_t"] = got[:2]
        if l == 0:
            gathered_small = got[2:]
            send_sems, recv_sems, parts_thru, land_thru, token = _exchange_start(_parts(dw_in_t), name="exchange_w_in_0_start")
            (dx0, dg1_first), _ = _matmul_norm_backward(dz, p["in_t"], s["x"], norm1_g[l] + token[0, 0], dx1, None, 0,
                                                        tm=tm, name=f"bwd_in_{l}")
            recv[0]["in_t"] = _exchange_wait(send_sems, recv_sems, parts_thru, land_thru, dg1_first,
                                             name="exchange_w_in_0_wait")
        else:
            (dx0, sheet), _ = _matmul_norm_backward(dz, p["in_t"], s["x"], norm1_g[l], dx1, sheet,
                                                    SG_LAYER * l + ROW_NORM1, tm=tm, name=f"bwd_in_{l}")
            pending_in = _parts(dw_in_t)
        dx = dx0
    grad_x = dx.reshape(x.shape)

    results = {}
    both = lambda key: [recv[l][key] for l in range(DEPTH)]
    blocks = lambda key: [r if isinstance(r, list) else [r] for r in both(key)]
    swap = lambda t: t.transpose(0, 2, 1)
    for key, slab, (w, m, v), tr in [("w_in", "in_t", (w_in, m_w_in, v_w_in), 192),
                                     ("w_ffn_up", "up_t", (w_ffn_up, m_w_ffn_up, v_w_ffn_up), 176)]:
        outs = _sum_adamw(blocks(slab), swap(w), swap(m), swap(v), tr=tr, name=f"adamw_{key}")
        results[key] = tuple(swap(o) for o in outs)
    g_bra = _sum_parts(both("bra_t"), tr=128, name="sum_w_branch_a").transpose(0, 2, 1)
    g_brb = _sum_parts(both("brb_t"), tr=128, name="sum_w_branch_b").transpose(0, 2, 1)
    g_br = jnp.stack([g_bra, g_brb], axis=1)
    flat = lambda t: t.reshape(-1, t.shape[-1])
    outs = _adamw(flat(w_branch), flat(g_br), flat(m_w_branch), flat(v_w_branch), tr=512, name="adamw_w_branch")
    results["w_branch"] = (g_br,) + tuple(o.reshape(w_branch.shape) for o in outs)
    results["w_out"] = tuple(_sum_adamw(blocks("out"), w_out, m_w_out, v_w_out, tr=128, name="adamw_w_out"))
    results["w_ffn_down"] = tuple(_sum_adamw(blocks("down"), w_ffn_down, m_w_ffn_down, v_w_ffn_down, tr=176,
                                             name="adamw_w_ffn_down"))

    sheet, dws0, dws1, dbs = _sum_gathered(gathered_small, name="sum_small_grads")
    (dg1_first,) = _all_reduce_small([dg1_first], name="all_reduce_last_gain")
    sheet = sheet.at[ROW_NORM1, :D_MODEL].set(dg1_first[0])
    loss = sheet[ROW_LOSS, 0]
    taps = lambda row, width: jnp.stack([sheet[SG_LAYER * l + row:SG_LAYER * l + row + 3, :width] for l in range(DEPTH)])
    extra = dict(w_spatial=jnp.stack([dws0, dws1]), b_spatial=dbs.reshape(DEPTH, N_HEADS, GMLP_BLOCK),
                 w_shortconv=lax.dynamic_slice_in_dim(taps(ROW_SCONV, D_B), my_idx * sc_w, sc_w, axis=2),
                 w_ffn_conv=lax.dynamic_slice_in_dim(taps(ROW_FCONV, D_FF), my_idx * fc_w, fc_w, axis=2))
    small_w = dict(norm1_g=(norm1_g, m_norm1_g, v_norm1_g), b_gate=(b_gate, m_b_gate, v_b_gate),
                   gmlp_ln_g=(gmlp_ln_g, m_gmlp_ln_g, v_gmlp_ln_g), gmlp_ln_b=(gmlp_ln_b, m_gmlp_ln_b, v_gmlp_ln_b),
                   w_spatial=(w_spatial, m_w_spatial, v_w_spatial), b_spatial=(b_spatial, m_b_spatial, v_b_spatial),
                   w_shortconv=(w_shortconv, m_w_shortconv, v_w_shortconv), norm2_g=(norm2_g, m_norm2_g, v_norm2_g),
                   w_ffn_conv=(w_ffn_conv, m_w_ffn_conv, v_w_ffn_conv), b_ffn_conv=(b_ffn_conv, m_b_ffn_conv, v_b_ffn_conv),
                   final_g=tuple(t.reshape(1, D_MODEL) for t in (final_g, m_final_g, v_final_g)))
    results.update(_adamw_small(sheet, extra, small_w, name="adamw_small"))
    results["final_g"] = tuple(t.reshape(D_MODEL) for t in results["final_g"])

    names = ["norm1_g", "w_in", "b_gate", "gmlp_ln_g", "gmlp_ln_b", "w_spatial", "b_spatial", "w_shortconv", "w_branch",
             "w_out", "norm2_g", "w_ffn_up", "w_ffn_conv", "b_ffn_conv", "w_ffn_down", "final_g"]
    return (loss, grad_x, *[results[k][0] for k in names], *[results[k][1] for k in names],
            *[results[k][2] for k in names], *[results[k][3] for k in names])
```

```python
import math

import jax
import jax.numpy as jnp
from jax import lax
from jax.experimental import pallas as pl
from jax.experimental.pallas import tpu as pltpu

F32 = jnp.float32
BF16 = jnp.bfloat16

N_DEV = 8
DEPTH = 2
D_MODEL = 1024
D_A = 512
D_B = 512
D_FF = 2816
D_IN = 4608
N_HEADS = 4
HEAD = 128
GMLP_BLOCK = 128
CAUSAL_CHUNK = 64
OFF_U, OFF_V, OFF_BG, OFF_CG, OFF_HB, OFF_GA, OFF_GB = 0, 512, 1024, 1536, 2048, 2560, 3584
RMS_EPS = 1e-6
LN_EPS = 1e-5
ADAM_LR, ADAM_B1, ADAM_B2, ADAM_EPS, ADAM_WD, ADAM_STEP = 0.001, 0.9, 0.999, 1e-08, 0.01, 10

SUBLANES, LANES = 8, 128
MATMUL_CHUNK = 512
HALO = 16
FFN_CHUNK = 256
SG_ROWS, SG_W, SG_LAYER = 40, D_FF, 16
ROW_NORM1, ROW_BGATE, ROW_LN_G, ROW_LN_B, ROW_SCONV, ROW_NORM2, ROW_FCONV, ROW_BFCONV = 0, 1, 2, 3, 4, 7, 8, 11
ROW_FINAL, ROW_LOSS = 32, 33
V7X_VMEM_BYTES = 64 << 20
VMEM_LIMIT = V7X_VMEM_BYTES - (8 << 20)
MESH = pl.DeviceIdType.MESH
GELU_C0 = 0.7978845608028654
GELU_C1 = 0.044715
NT = (((1,), (1,)), ((), ()))
TN = (((0,), (0,)), ((), ()))


def _dot(a, b):
    return jnp.dot(a, b, preferred_element_type=F32)


def _dot_nt(a, b):
    return lax.dot_general(a, b, NT, preferred_element_type=F32)


def _sigmoid(x):
    return 1.0 / (1.0 + jnp.exp(-x))


def _gelu_tanh(x):
    return jnp.tanh(GELU_C0 * (x + GELU_C1 * x * x * x))


def _gelu_grad(x, t):
    return 0.5 * (1.0 + t) + 0.5 * x * (1.0 - t * t) * GELU_C0 * (1.0 + 3.0 * GELU_C1 * x * x)


def _sublane_tile(dtype):
    return SUBLANES * (4 // jnp.dtype(dtype).itemsize)


def _shift_down(a, k, prev):
    p = prev.shape[0]
    r = pltpu.roll(a, k, 0)
    sub = _sublane_tile(a.dtype)
    head = r[0:sub]
    rid = lax.broadcasted_iota(jnp.int32, head.shape, 0)
    for j in range(k):
        head = jnp.where(rid == j, prev[p - k + j:p - k + j + 1, :], head)
    return jnp.concatenate([head, r[sub:]], axis=0)


def _shift_up(a, k, nxt):
    t = a.shape[0]
    r = pltpu.roll(a, t - k, 0)
    sub = _sublane_tile(a.dtype)
    tail = r[t - sub:t]
    rid = lax.broadcasted_iota(jnp.int32, tail.shape, 0)
    for j in range(k):
        tail = jnp.where(rid == sub - k + j, nxt[j:j + 1, :], tail)
    return jnp.concatenate([r[0:t - sub], tail], axis=0)


def _column_sums(p):
    if p.dtype.itemsize < 4:
        t = p.shape[0]
        p = p[:t // 2] + p[t // 2:]
        p = p[:t // 4] + p[t // 4:]
    return jnp.sum(p.astype(F32), axis=0, keepdims=True)


def _sheet_begin(step, sheet_in, sheet_out, first_row, rows):
    @pl.when(step == 0)
    def _():
        sheet_out[...] = sheet_in[...]
        sheet_out[first_row:first_row + rows, :] = jnp.zeros((rows, SG_W), F32)


def _sheet_spec():
    return pl.BlockSpec((SG_ROWS, SG_W), lambda i: (0, 0))


def _spatial_mask(transposed):
    ri = lax.broadcasted_iota(jnp.int32, (GMLP_BLOCK, GMLP_BLOCK), 0) // CAUSAL_CHUNK
    ci = lax.broadcasted_iota(jnp.int32, (GMLP_BLOCK, GMLP_BLOCK), 1) // CAUSAL_CHUNK
    return (ri <= ci) if transposed else (ci <= ri)


def _gmlp_forward(u, v, ln_g, ln_b, ws_ref, bs_ref, f_scr):
    tm = u.shape[0]
    tu = _gelu_tanh(u)
    tv = _gelu_tanh(v)
    gu = 0.5 * u * (1.0 + tu)
    gv = 0.5 * v * (1.0 + tv)
    mu = jnp.mean(gv, axis=-1, keepdims=True)
    cen = gv - mu
    rstd = lax.rsqrt(jnp.mean(cen * cen, axis=-1, keepdims=True) + LN_EPS)
    xh = cen * rstd
    vn = (xh * ln_g + ln_b).astype(BF16)
    mask = _spatial_mask(False)
    wm = [jnp.where(mask, ws_ref[h], 0.0).astype(BF16) for h in range(N_HEADS)]
    for b in range(tm // GMLP_BLOCK):
        rows = slice(b * GMLP_BLOCK, (b + 1) * GMLP_BLOCK)
        for h in range(N_HEADS):
            cols = slice(h * HEAD, (h + 1) * HEAD)
            f_scr[rows, cols] = (_dot(wm[h], vn[rows, cols]) + bs_ref[h]).astype(f_scr.dtype)
    return gu, tu, tv, xh, rstd, vn, f_scr[...]


def _position():
    return lax.axis_index("x"), lax.axis_index("y"), lax.axis_index("c")


def _handshake(peers):
    barrier = pltpu.get_barrier_semaphore()
    for peer in peers:
        pl.semaphore_signal(barrier, inc=1, device_id=peer, device_id_type=MESH)
    pl.semaphore_wait(barrier, len(peers))


class _Gather:
    collective_id = 1

    def __init__(self, arrays):
        self.arrays = list(arrays)
        self.out_shape = [jax.ShapeDtypeStruct((N_DEV,) + a.shape, a.dtype) for a in self.arrays]
        self.base = 0

    def barrier(self):
        x, y, c = _position()
        _handshake([(x, y, 1 - c), (1 - x, y, c), (x, 1 - y, c), (1 - x, 1 - y, c)])

    def _plan(self, ins, outs, sems):
        send_sems, recv_sems, local_sems = sems
        x, y, c = _position()
        me, sibling = (x, y, c), (x, y, 1 - c)
        chips = [(1 - x, y), (x, 1 - y), (1 - x, 1 - y)]

        def slot(a, p):
            return outs[a].at[4 * p[0] + 2 * p[1] + p[2]]

        def copy(a, k, block, to, src=None):
            return pltpu.make_async_remote_copy(
                src_ref=slot(a, block) if src is None else src, dst_ref=slot(a, block),
                send_sem=send_sems.at[self.base + a, k], recv_sem=recv_sems.at[self.base + a, k],
                device_id=to, device_id_type=MESH)

        n = len(self.arrays)

        def mine():
            return [pltpu.make_async_copy(ins[a], slot(a, me), local_sems.at[self.base + a]) for a in range(n)]

        def first():
            out = []
            for a in range(n):
                out.append(copy(a, 0, me, sibling, src=ins[a]))
                out += [copy(a, 1 + j, me, (*chip, c), src=ins[a]) for j, chip in enumerate(chips)]
            return out

        def arrivals():
            return [copy(a, 1 + j, (*chip, c), me) for j, chip in enumerate(chips) for a in range(n)]

        def relays():
            return [copy(a, 4 + j, (*chip, c), sibling) for j, chip in enumerate(chips) for a in range(n)]

        def from_sibling():
            out = [copy(a, 0, sibling, me) for a in range(n)]
            return out + [copy(a, 4 + j, (*chip, 1 - c), me) for j, chip in enumerate(chips) for a in range(n)]

        return mine, first, arrivals, relays, from_sibling

    def start(self, ins, outs, sems):
        mine, first, _, _, _ = self._plan(ins, outs, sems)
        for cp in mine() + first():
            cp.start()

    def relay(self, ins, outs, sems):
        _, _, arrivals, relays, _ = self._plan(ins, outs, sems)
        for arrived, onward in zip(arrivals(), relays()):
            arrived.wait_recv()
            onward.start()

    def finish(self, ins, outs, sems):
        mine, first, _, relays, from_sibling = self._plan(ins, outs, sems)
        for cp in from_sibling():
            cp.wait_recv()
        for cp in first() + relays():
            cp.wait_send()
        for cp in mine():
            cp.wait()


class _Exchange:
    collective_id = 0

    def __init__(self, arrays):
        self.arrays = list(arrays)
        self.out_shape = [jax.ShapeDtypeStruct(a.shape, a.dtype) for a in self.arrays]
        self.base = 0

    def barrier(self):
        x, y, c = _position()
        _handshake([(x ^ dx, y ^ dy, c ^ dc) for dx in (0, 1) for dy in (0, 1) for dc in (0, 1) if dx or dy or dc])

    def _plan(self, ins, outs, sems):
        send_sems, recv_sems, local_sems = sems
        x, y, c = _position()
        my_idx = 4 * x + 2 * y + c
        n = len(self.arrays)
        offsets = [(dx, dy, dc) for dx in (0, 1) for dy in (0, 1) for dc in (0, 1) if (dx, dy, dc) != (0, 0, 0)]

        def mine():
            return [pltpu.make_async_copy(ins[a].at[my_idx], outs[a].at[my_idx], local_sems.at[self.base + a])
                    for a in range(n)]

        def remote(arriving):
            out = []
            for k, (dx, dy, dc) in enumerate(offsets):
                px, py, pc = x ^ dx, y ^ dy, c ^ dc
                p_idx = 4 * px + 2 * py + pc
                for a in range(n):
                    out.append(pltpu.make_async_remote_copy(
                        src_ref=ins[a].at[p_idx], dst_ref=outs[a].at[p_idx if arriving else my_idx],
                        send_sem=send_sems.at[self.base + a, k], recv_sem=recv_sems.at[self.base + a, k],
                        device_id=(px, py, pc), device_id_type=MESH))
            return out

        return mine, remote

    def start(self, ins, outs, sems):
        mine, remote = self._plan(ins, outs, sems)
        for cp in mine() + remote(False):
            cp.start()

    def relay(self, ins, outs, sems):
        pass

    def finish(self, ins, outs, sems):
        mine, remote = self._plan(ins, outs, sems)
        for cp in remote(True):
            cp.wait_recv()
        for cp in remote(False):
            cp.wait_send()
        for cp in mine():
            cp.wait()


def _exchange_start(parts, *, name):
    hbm = pl.BlockSpec(memory_space=pltpu.HBM)
    sem = pl.BlockSpec(memory_space=pltpu.SEMAPHORE)
    offsets = [(dx, dy, dc) for dx in (0, 1) for dy in (0, 1) for dc in (0, 1) if dx or dy or dc]

    def body(parts_ref, send_sems, recv_sems, parts_thru, land_ref, token, own_sem):
        x, y, c = _position()
        my_idx = 4 * x + 2 * y + c
        own = pltpu.make_async_copy(parts_ref.at[my_idx], land_ref.at[my_idx], own_sem)
        own.start()
        _handshake([(x ^ dx, y ^ dy, c ^ dc) for dx, dy, dc in offsets])
        for k, (dx, dy, dc) in enumerate(offsets):
            px, py, pc = x ^ dx, y ^ dy, c ^ dc
            pltpu.make_async_remote_copy(
                src_ref=parts_ref.at[4 * px + 2 * py + pc], dst_ref=land_ref.at[my_idx],
                send_sem=send_sems.at[k], recv_sem=recv_sems.at[k],
                device_id=(px, py, pc), device_id_type=MESH).start()
        own.wait()
        token[...] = jnp.zeros_like(token)

    return pl.pallas_call(
        body, name=name,
        out_shape=(pltpu.SemaphoreType.DMA((7,)), pltpu.SemaphoreType.DMA((7,)), pltpu.HBM(parts.shape, parts.dtype),
                   pltpu.HBM(parts.shape, parts.dtype), jax.ShapeDtypeStruct((SUBLANES, LANES), F32)),
        in_specs=(hbm,), out_specs=(sem, sem, hbm, hbm, pl.BlockSpec(memory_space=pltpu.VMEM)),
        input_output_aliases={0: 2}, scratch_shapes=[pltpu.SemaphoreType.DMA(())],
        compiler_params=pltpu.CompilerParams(has_side_effects=pltpu.SideEffectType.DATAFLOW_SIDE_EFFECTING,
                                             collective_id=2),
    )(pltpu.with_memory_space_constraint(parts, pltpu.HBM))


def _exchange_wait(send_sems, recv_sems, parts_thru, land_thru, after, *, name):
    hbm = pl.BlockSpec(memory_space=pltpu.HBM)
    sem = pl.BlockSpec(memory_space=pltpu.SEMAPHORE)
    offsets = [(dx, dy, dc) for dx in (0, 1) for dy in (0, 1) for dc in (0, 1) if dx or dy or dc]

    def body(parts_ref, land_ref, send_sems, recv_sems, after_ref, parts_dead, got_ref):
        x, y, c = _position()
        for k, (dx, dy, dc) in enumerate(offsets):
            px, py, pc = x ^ dx, y ^ dy, c ^ dc
            p_idx = 4 * px + 2 * py + pc
            copy = pltpu.make_async_remote_copy(
                src_ref=parts_ref.at[p_idx], dst_ref=land_ref.at[p_idx], send_sem=send_sems.at[k],
                recv_sem=recv_sems.at[k], device_id=(px, py, pc), device_id_type=MESH)
            copy.wait_send()
            copy.wait_recv()

    return pl.pallas_call(
        body, name=name,
        out_shape=(pltpu.HBM(parts_thru.shape, parts_thru.dtype), pltpu.HBM(land_thru.shape, land_thru.dtype)),
        in_specs=(hbm, hbm, sem, sem, pl.BlockSpec(memory_space=pl.ANY)), out_specs=(hbm, hbm),
        input_output_aliases={0: 0, 1: 1},
        compiler_params=pltpu.CompilerParams(has_side_effects=pltpu.SideEffectType.DATAFLOW_SIDE_EFFECTING),
    )(parts_thru, land_thru, send_sems, recv_sems, after)[1]


class _Both:
    def __init__(self, *carries):
        self.carries = carries
        self.arrays = [a for c in carries for a in c.arrays]
        self.out_shape = [s for c in carries for s in c.out_shape]
        first = 0
        for c in carries:
            c.base = first
            first += len(c.arrays)
        self.collective_id = min(c.collective_id for c in carries)

    def barrier(self):
        min(self.carries, key=lambda c: c.collective_id).barrier()

    def _each(self, method, ins, outs, sems):
        for c in self.carries:
            rows = slice(c.base, c.base + len(c.arrays))
            getattr(c, method)(ins[rows], outs[rows], sems)

    def start(self, ins, outs, sems):
        self._each("start", ins, outs, sems)

    def relay(self, ins, outs, sems):
        self._each("relay", ins, outs, sems)

    def finish(self, ins, outs, sems):
        self._each("finish", ins, outs, sems)


def _call(body, *, name, grid, in_specs, out_specs, out_shape, args, scratch_shapes=(), carry=None):
    n_in, n_out, n_scr = len(in_specs), len(out_specs), len(scratch_shapes)
    params = pltpu.CompilerParams(dimension_semantics=("arbitrary",) * len(grid), vmem_limit_bytes=VMEM_LIMIT)
    if carry is None:
        outs = pl.pallas_call(body, name=name, grid=grid, in_specs=in_specs, out_specs=out_specs, out_shape=out_shape,
                              scratch_shapes=list(scratch_shapes), compiler_params=params)(*args)
        return outs, None
    m = len(carry.arrays)
    total = math.prod(grid)

    def wrapped(*refs):
        ins, refs = refs[:n_in], refs[n_in:]
        c_ins, refs = refs[:m], refs[m:]
        outs, refs = refs[:n_out], refs[n_out:]
        c_outs, refs = refs[:m], refs[m:]
        scr, sems = refs[:n_scr], refs[n_scr:]
        flat = pl.program_id(0)
        for d in range(1, len(grid)):
            flat = flat * grid[d] + pl.program_id(d)

        @pl.when(flat == 0)
        def _():
            carry.barrier()
            carry.start(c_ins, c_outs, sems)

        body(*ins, *outs, *scr)

        @pl.when(flat == total - 2)
        def _():
            carry.relay(c_ins, c_outs, sems)

        @pl.when(flat == total - 1)
        def _():
            carry.finish(c_ins, c_outs, sems)

    any_spec = pl.BlockSpec(memory_space=pl.ANY)
    sem_shapes = [pltpu.SemaphoreType.DMA((m, 7)), pltpu.SemaphoreType.DMA((m, 7)), pltpu.SemaphoreType.DMA((m,))]
    params = pltpu.CompilerParams(dimension_semantics=("arbitrary",) * len(grid), vmem_limit_bytes=VMEM_LIMIT,
                                  collective_id=carry.collective_id)
    outs = pl.pallas_call(
        wrapped, name=name, grid=grid,
        in_specs=list(in_specs) + [any_spec] * m, out_specs=list(out_specs) + [any_spec] * m,
        out_shape=list(out_shape) + carry.out_shape,
        scratch_shapes=list(scratch_shapes) + sem_shapes, compiler_params=params)(*args, *carry.arrays)
    return outs[:n_out], outs[n_out:]


def _gather_now(arrays, *, name):
    carry = _Gather(arrays)
    m = len(arrays)

    def body(*refs):
        ins, outs, sems = refs[:m], refs[m:2 * m], refs[2 * m:]
        carry.barrier()
        carry.start(ins, outs, sems)
        carry.relay(ins, outs, sems)
        carry.finish(ins, outs, sems)

    any_spec = pl.BlockSpec(memory_space=pl.ANY)
    return pl.pallas_call(
        body, name=name, in_specs=[any_spec] * m, out_specs=[any_spec] * m, out_shape=carry.out_shape,
        scratch_shapes=[pltpu.SemaphoreType.DMA((m, 7)), pltpu.SemaphoreType.DMA((m, 7)),
                        pltpu.SemaphoreType.DMA((m,))],
        compiler_params=pltpu.CompilerParams(collective_id=carry.collective_id),
    )(*arrays)


def _all_reduce_small(arrs, *, name):
    n = len(arrs)

    def body(*refs):
        ins, outs, bufs = refs[:n], refs[n:2 * n], refs[2 * n:3 * n]
        send_sems, recv_sems = refs[3 * n:]
        x, y, c = _position()
        me, sibling = (x, y, c), (x, y, 1 - c)
        chips = [(1 - x, y), (x, 1 - y), (1 - x, 1 - y)]
        _handshake([sibling] + [(*chip, c) for chip in chips])

        def copy(a, k, block, to, src=None):
            slot = bufs[a].at[4 * block[0] + 2 * block[1] + block[2]]
            return pltpu.make_async_remote_copy(
                src_ref=slot if src is None else src, dst_ref=slot,
                send_sem=send_sems.at[a, k], recv_sem=recv_sems.at[a, k], device_id=to, device_id_type=MESH)

        first = []
        for a in range(n):
            first.append(copy(a, 0, me, sibling, src=ins[a]))
            first += [copy(a, 1 + j, me, (*chip, c), src=ins[a]) for j, chip in enumerate(chips)]
        for cp in first:
            cp.start()
        passed = []
        for j, chip in enumerate(chips):
            for a in range(n):
                copy(a, 1 + j, (*chip, c), me).wait_recv()
                cp = copy(a, 4 + j, (*chip, c), sibling)
                cp.start()
                passed.append(cp)
        for a in range(n):
            copy(a, 0, sibling, me).wait_recv()
            for j, chip in enumerate(chips):
                copy(a, 4 + j, (*chip, 1 - c), me).wait_recv()
        for cp in first + passed:
            cp.wait_send()
        my_idx = 4 * x + 2 * y + c
        for a in range(n):
            acc = jnp.zeros(ins[a].shape, F32)
            for s in range(N_DEV):
                acc = acc + jnp.where(my_idx == s, ins[a][...], bufs[a][s])
            outs[a][...] = acc

    vmem = pl.BlockSpec(memory_space=pltpu.VMEM)
    return pl.pallas_call(
        body, name=name, in_specs=[vmem] * n, out_specs=[vmem] * n,
        out_shape=[jax.ShapeDtypeStruct(a.shape, F32) for a in arrs],
        scratch_shapes=[pltpu.VMEM((N_DEV,) + a.shape, F32) for a in arrs]
        + [pltpu.SemaphoreType.DMA((n, 7)), pltpu.SemaphoreType.DMA((n, 7))],
        compiler_params=pltpu.CompilerParams(vmem_limit_bytes=VMEM_LIMIT, collective_id=_Gather.collective_id),
    )(*arrs)


def _sum_gathered(arrs, *, name):
    n = len(arrs)

    def body(*refs):
        for in_ref, out_ref in zip(refs[:n], refs[n:]):
            acc = in_ref[0]
            for s in range(1, N_DEV):
                acc = acc + in_ref[s]
            out_ref[...] = acc

    vmem = pl.BlockSpec(memory_space=pltpu.VMEM)
    return pl.pallas_call(
        body, name=name, in_specs=[vmem] * n, out_specs=[vmem] * n,
        out_shape=[jax.ShapeDtypeStruct(a.shape[1:], F32) for a in arrs],
        compiler_params=pltpu.CompilerParams(vmem_limit_bytes=VMEM_LIMIT),
    )(*arrs)


def _norm_matmul(x, g, w_t, *, tm, name, carry=None):
    n, d = x.shape
    c = w_t.shape[0]
    ch = MATMUL_CHUNK

    def body(x_ref, g_ref, wt_ref, h_ref, z_ref):
        xv = x_ref[...]
        r = lax.rsqrt(jnp.mean(xv * xv, axis=-1, keepdims=True) + RMS_EPS)
        h = (xv * r * g_ref[...]).astype(BF16)
        h_ref[...] = h
        for c0 in range(0, c, ch):
            z_ref[:, c0:c0 + ch] = _dot_nt(h, wt_ref[c0:c0 + ch, :]).astype(BF16)

    return _call(
        body, name=name, grid=(n // tm,), carry=carry,
        in_specs=[pl.BlockSpec((tm, d), lambda i: (i, 0)),
                  pl.BlockSpec((1, d), lambda i: (0, 0)),
                  pl.BlockSpec((c, d), lambda i: (0, 0))],
        out_specs=[pl.BlockSpec((tm, d), lambda i: (i, 0)),
                   pl.BlockSpec((tm, c), lambda i: (i, 0))],
        out_shape=[jax.ShapeDtypeStruct((n, d), BF16), jax.ShapeDtypeStruct((n, c), BF16)],
        args=(x, g.reshape(1, d), w_t))


def _mix_forward(z, x, b_gate, ln_g, ln_b, w_s, b_s, w_sc, wb, w_out, *, tm, name, carry=None):
    n = z.shape[0]
    hb = tm // HALO

    def body(z_ref, zp_ref, x_ref, bg_ref, lng_ref, lnb_ref, ws_ref, bs_ref, wsc_ref, wb_ref, wo_ref,
             ya_ref, yb_ref, cv_ref, pa_ref, pb_ref, mg_ref, x1_ref, f_scr):
        i = pl.program_id(0)
        u = z_ref[:, OFF_U:OFF_U + D_A]
        v = z_ref[:, OFF_V:OFF_V + D_A].astype(F32)
        gu, _, _, _, _, _, f = _gmlp_forward(u, v, lng_ref[...], lnb_ref[...], ws_ref, bs_ref, f_scr)
        ya = gu * f
        ya_ref[...] = ya

        q = z_ref[:, OFF_CG:OFF_CG + D_B] * z_ref[:, OFF_HB:OFF_HB + D_B]
        qp = zp_ref[:, OFF_CG:OFF_CG + D_B] * zp_ref[:, OFF_HB:OFF_HB + D_B]
        qp = jnp.where(i > 0, qp, jnp.zeros_like(qp))
        w = wsc_ref[...].astype(BF16)
        conv = w[0:1] * _shift_down(q, 2, qp) + w[1:2] * _shift_down(q, 1, qp) + w[2:3] * q
        cv_ref[...] = conv
        yb = z_ref[:, OFF_BG:OFF_BG + D_B] * conv
        yb_ref[...] = yb

        pa = _dot(ya, wb_ref[0]).astype(BF16)
        pb = _dot(yb, wb_ref[1]).astype(BF16)
        pa_ref[...] = pa
        pb_ref[...] = pb
        bg = bg_ref[...].astype(BF16)
        sa = _sigmoid(z_ref[:, OFF_GA:OFF_GA + D_MODEL] + bg[:, 0:D_MODEL])
        sb = _sigmoid(z_ref[:, OFF_GB:OFF_GB + D_MODEL] + bg[:, D_MODEL:2 * D_MODEL])
        mg = sa * pa + sb * pb
        mg_ref[...] = mg
        x1_ref[...] = x_ref[...] + _dot(mg, wo_ref[...])

    row = lambda w: pl.BlockSpec((tm, w), lambda i: (i, 0))
    full = lambda *s: pl.BlockSpec(s, lambda i: (0,) * len(s))
    bf = lambda w: jax.ShapeDtypeStruct((n, w), BF16)
    return _call(
        body, name=name, grid=(n // tm,), carry=carry,
        in_specs=[row(D_IN),
                  pl.BlockSpec((HALO, D_IN), lambda i: (jnp.maximum(i * hb - 1, 0), 0)),
                  row(D_MODEL), full(1, 2 * D_MODEL), full(1, D_A), full(1, D_A),
                  full(N_HEADS, GMLP_BLOCK, GMLP_BLOCK), full(N_HEADS, GMLP_BLOCK, 1), full(3, D_B),
                  full(2, D_A, D_MODEL), full(D_MODEL, D_MODEL)],
        out_specs=[row(D_A), row(D_B), row(D_B), row(D_MODEL), row(D_MODEL), row(D_MODEL), row(D_MODEL)],
        out_shape=[bf(D_A), bf(D_B), bf(D_B), bf(D_MODEL), bf(D_MODEL), bf(D_MODEL),
                   jax.ShapeDtypeStruct((n, D_MODEL), F32)],
        scratch_shapes=[pltpu.VMEM((tm, D_A), BF16)],
        args=(z, z, x, b_gate.reshape(1, -1), ln_g.reshape(1, -1), ln_b.reshape(1, -1), w_s,
              b_s.reshape(N_HEADS, GMLP_BLOCK, 1), w_sc, wb, w_out))


def _loss_tile(xv, gv, tv):
    d = xv.shape[-1]
    r = lax.rsqrt(jnp.mean(xv * xv, axis=-1, keepdims=True) + RMS_EPS)
    xh = xv * r
    e = xh * gv - tv
    per_row = jnp.sum(e * e, axis=-1, keepdims=True) * (0.5 / d)
    dy = e * (1.0 / d)
    dxh = dy * gv
    dx = r * (dxh - xh * jnp.mean(dxh * xh, axis=-1, keepdims=True))
    return dx, jnp.sum(per_row, axis=0, keepdims=True), jnp.sum(dy * xh, axis=0, keepdims=True)


def _ffn_forward(up, x1, w_fc, b_fc, w_down, *, tm, name, carry=None, head=None):
    n = up.shape[0]
    hb = tm // HALO
    n_in = 6 if head is None else 8

    def body(*refs):
        up_ref, upp_ref, x1_ref, wfc_ref, bfc_ref, wd_ref = refs[:6]
        gc_ref, a_ref, out_ref = refs[n_in:n_in + 3]
        acc = refs[-1]
        i = pl.program_id(0)
        acc[...] = x1_ref[...]
        for c0 in range(0, D_FF, FFN_CHUNK):
            cols = slice(c0, c0 + FFN_CHUNK)
            gate = up_ref[:, cols]
            val = up_ref[:, D_FF + c0:D_FF + c0 + FFN_CHUNK]
            gp = upp_ref[:, cols]
            gp = jnp.where(i > 0, gp, jnp.zeros_like(gp))
            w = wfc_ref[:, cols].astype(BF16)
            gc = (w[0:1] * _shift_down(gate, 2, gp) + w[1:2] * _shift_down(gate, 1, gp) + w[2:3] * gate
                  + bfc_ref[:, cols].astype(BF16))
            gc_ref[:, cols] = gc
            a = gc * _sigmoid(gc) * val
            a_ref[:, cols] = a
            acc[...] += _dot(a, wd_ref[cols, :])
        if head is None:
            out_ref[...] = acc[...]
        else:
            g_ref, t_ref = refs[6:8]
            sg_ref = refs[n_in + 3]

            @pl.when(i == 0)
            def _():
                sg_ref[...] = jnp.zeros_like(sg_ref)

            dx, loss, dg = _loss_tile(acc[...], g_ref[...], t_ref[...])
            out_ref[...] = dx
            sg_ref[ROW_LOSS:ROW_LOSS + 1, 0:LANES] += jnp.broadcast_to(loss, (1, LANES))
            sg_ref[ROW_FINAL:ROW_FINAL + 1, 0:D_MODEL] += dg

    row = lambda w: pl.BlockSpec((tm, w), lambda i: (i, 0))
    full = lambda r, c: pl.BlockSpec((r, c), lambda i: (0, 0))
    in_specs = [row(2 * D_FF), pl.BlockSpec((HALO, D_FF), lambda i: (jnp.maximum(i * hb - 1, 0), 0)), row(D_MODEL),
                full(3, D_FF), full(1, D_FF), full(D_FF, D_MODEL)]
    out_specs = [row(D_FF), row(D_FF), row(D_MODEL)]
    out_shape = [jax.ShapeDtypeStruct((n, D_FF), BF16), jax.ShapeDtypeStruct((n, D_FF), BF16),
                 jax.ShapeDtypeStruct((n, D_MODEL), F32)]
    args = (up, up, x1, w_fc, b_fc.reshape(1, -1), w_down)
    if head is not None:
        in_specs += [full(1, D_MODEL), row(D_MODEL)]
        out_specs += [full(SG_ROWS, SG_W)]
        out_shape += [jax.ShapeDtypeStruct((SG_ROWS, SG_W), F32)]
        args += (head[0].reshape(1, -1), head[1])
    return _call(body, name=name, grid=(n // tm,), carry=carry, in_specs=in_specs, out_specs=out_specs,
                 out_shape=out_shape, scratch_shapes=[pltpu.VMEM((tm, D_MODEL), F32)], args=args)


def _ffn_backward(dx2, up, gc, w_fc, w_down, sheet, layer, *, tm, name, carry=None):
    n = up.shape[0]
    steps = n // tm
    hb = tm // HALO
    row = SG_LAYER * layer + ROW_FCONV

    def body(dx_ref, dxn_ref, up_ref, upn_ref, gc_ref, gcn_ref, wfc_ref, wd_ref, sg_in, dup_ref, sg_ref):
        i = pl.program_id(0)
        last = i == steps - 1
        _sheet_begin(i, sg_in, sg_ref, row, 4)

        dxe = jnp.concatenate([dx_ref[...], dxn_ref[...]], axis=0).astype(BF16)
        for c0 in range(0, D_FF, FFN_CHUNK):
            cols = slice(c0, c0 + FFN_CHUNK)
            vcols = slice(D_FF + c0, D_FF + c0 + FFN_CHUNK)
            dae = _dot_nt(dxe, wd_ref[cols, :])
            da, dan = dae[:tm], dae[tm:]
            gate = up_ref[:, cols]
            val = up_ref[:, vcols]
            gcv = gc_ref[:, cols]
            s = _sigmoid(gcv)
            dab = da.astype(BF16)
            dup_ref[:, vcols] = dab * (gcv * s)
            dgc = dab * val * (s * (1.0 + gcv * (1.0 - s)))
            gcn = gcn_ref[:, cols]
            sn = _sigmoid(gcn)
            dgcn = dan.astype(BF16) * upn_ref[:, vcols] * (sn * (1.0 + gcn * (1.0 - sn)))
            dgcn = jnp.where(last, jnp.zeros_like(dgcn), dgcn)
            up1 = _shift_up(dgc, 1, dgcn)
            up2 = _shift_up(dgc, 2, dgcn)
            w = wfc_ref[:, cols].astype(BF16)
            dup_ref[:, cols] = w[2:3] * dgc + w[1:2] * up1 + w[0:1] * up2
            sg_ref[row:row + 1, cols] += _column_sums(gate * up2)
            sg_ref[row + 1:row + 2, cols] += _column_sums(gate * up1)
            sg_ref[row + 2:row + 3, cols] += _column_sums(gate * dgc)
            sg_ref[row + 3:row + 4, cols] += _column_sums(dgc)

    nxt = lambda i: (jnp.minimum((i + 1) * hb, steps * hb - 1), 0)
    return _call(
        body, name=name, grid=(steps,), carry=carry,
        in_specs=[pl.BlockSpec((tm, D_MODEL), lambda i: (i, 0)),
                  pl.BlockSpec((HALO, D_MODEL), nxt),
                  pl.BlockSpec((tm, 2 * D_FF), lambda i: (i, 0)),
                  pl.BlockSpec((HALO, 2 * D_FF), nxt),
                  pl.BlockSpec((tm, D_FF), lambda i: (i, 0)),
                  pl.BlockSpec((HALO, D_FF), nxt),
                  pl.BlockSpec((3, D_FF), lambda i: (0, 0)),
                  pl.BlockSpec((D_FF, D_MODEL), lambda i: (0, 0)), _sheet_spec()],
        out_specs=[pl.BlockSpec((tm, 2 * D_FF), lambda i: (i, 0)), _sheet_spec()],
        out_shape=[jax.ShapeDtypeStruct((n, 2 * D_FF), BF16), jax.ShapeDtypeStruct((SG_ROWS, SG_W), F32)],
        args=(dx2, dx2, up, up, gc, gc, w_fc, w_down, sheet))


def _matmul_norm_backward(dz, w_t, x, g, dres, sheet, row, *, tm, name, carry=None):
    n, c = dz.shape
    d = x.shape[1]
    ch = MATMUL_CHUNK

    def body(dz_ref, wt_ref, x_ref, g_ref, dres_ref, *rest):
        i = pl.program_id(0)
        if sheet is None:
            dx_ref, sg_ref = rest

            @pl.when(i == 0)
            def _():
                sg_ref[...] = jnp.zeros_like(sg_ref)
        else:
            sg_in, dx_ref, sg_ref = rest
            _sheet_begin(i, sg_in, sg_ref, row, 1)

        dh = _dot(dz_ref[:, 0:ch], wt_ref[0:ch, :])
        for c0 in range(ch, c, ch):
            dh += _dot(dz_ref[:, c0:c0 + ch], wt_ref[c0:c0 + ch, :])
        xv = x_ref[...]
        r = lax.rsqrt(jnp.mean(xv * xv, axis=-1, keepdims=True) + RMS_EPS)
        xh = xv * r
        sg_ref[row:row + 1, 0:d] += jnp.sum(dh * xh, axis=0, keepdims=True)
        dxh = dh * g_ref[...]
        dx_ref[...] = dres_ref[...] + r * (dxh - xh * jnp.mean(dxh * xh, axis=-1, keepdims=True))

    in_specs = [pl.BlockSpec((tm, c), lambda i: (i, 0)),
                pl.BlockSpec((c, d), lambda i: (0, 0)),
                pl.BlockSpec((tm, d), lambda i: (i, 0)),
                pl.BlockSpec((1, d), lambda i: (0, 0)),
                pl.BlockSpec((tm, d), lambda i: (i, 0))]
    args = (dz, w_t, x, g.reshape(1, d), dres)
    if sheet is None:
        small_spec, small_shape = pl.BlockSpec((8, d), lambda i: (0, 0)), jax.ShapeDtypeStruct((8, d), F32)
    else:
        in_specs, args = in_specs + [_sheet_spec()], args + (sheet,)
        small_spec, small_shape = _sheet_spec(), jax.ShapeDtypeStruct((SG_ROWS, SG_W), F32)
    return _call(
        body, name=name, grid=(n // tm,), carry=carry, in_specs=in_specs,
        out_specs=[pl.BlockSpec((tm, d), lambda i: (i, 0)), small_spec],
        out_shape=[jax.ShapeDtypeStruct((n, d), F32), small_shape], args=args)


def _mix_backward(dx1, z, conv, pa, pb, b_gate, ln_g, ln_b, w_s, w_s_t, b_s, w_sc, w_out, wb, sheet, layer, *, tm, name,
                  carry=None):
    n = z.shape[0]
    steps = n // tm
    hb = tm // HALO
    base = SG_LAYER * layer
    r_bg, r_lng, r_lnb, r_sc = base + ROW_BGATE, base + ROW_LN_G, base + ROW_LN_B, base + ROW_SCONV

    def body(dx_ref, dxn_ref, z_ref, zn_ref, cv_ref, pa_ref, pb_ref, bg_ref, lng_ref, lnb_ref, ws_ref, wst_ref,
             bs_ref, wsc_ref, wo_ref, wb_ref, sg_in,
             dz_ref, dpa_ref, dpb_ref, dws_ref, dbs_ref, sg_ref, f_scr, dvn_scr):
        i = pl.program_id(0)
        last = i == steps - 1
        _sheet_begin(i, sg_in, sg_ref, r_bg, ROW_NORM2 - ROW_BGATE)

        @pl.when(i == 0)
        def _():
            dws_ref[...] = jnp.zeros_like(dws_ref)
            dbs_ref[...] = jnp.zeros_like(dbs_ref)

        dxe = jnp.concatenate([dx_ref[...], dxn_ref[...]], axis=0).astype(BF16)
        dmge = _dot_nt(dxe, wo_ref[...])
        dmg, dmgn = dmge[:tm].astype(BF16), dmge[tm:].astype(BF16)

        pa_v = pa_ref[...]
        pb_v = pb_ref[...]
        bg = bg_ref[...].astype(BF16)
        sa = _sigmoid(z_ref[:, OFF_GA:OFF_GA + D_MODEL] + bg[:, 0:D_MODEL])
        sb = _sigmoid(z_ref[:, OFF_GB:OFF_GB + D_MODEL] + bg[:, D_MODEL:2 * D_MODEL])
        dpa = dmg * sa
        dpb = dmg * sb
        dga = dmg * pa_v * sa * (1.0 - sa)
        dgb = dmg * pb_v * sb * (1.0 - sb)
        dpa_ref[...] = dpa
        dpb_ref[...] = dpb
        dz_ref[:, OFF_GA:OFF_GA + D_MODEL] = dga
        dz_ref[:, OFF_GB:OFF_GB + D_MODEL] = dgb
        sg_ref[r_bg:r_bg + 1, 0:D_MODEL] += _column_sums(dga)
        sg_ref[r_bg:r_bg + 1, D_MODEL:2 * D_MODEL] += _column_sums(dgb)

        dya = _dot_nt(dpa, wb_ref[0]).astype(BF16)
        u = z_ref[:, OFF_U:OFF_U + D_A]
        v = z_ref[:, OFF_V:OFF_V + D_A].astype(F32)
        ln_g = lng_ref[...]
        gu, tu, tv, xh, rstd, vn, f = _gmlp_forward(u, v, ln_g, lnb_ref[...], ws_ref, bs_ref, f_scr)
        dgu = dya * f
        df_bf = dya * gu
        dz_ref[:, OFF_U:OFF_U + D_A] = dgu * _gelu_grad(u, tu)
        mask = _spatial_mask(False)
        mask_t = _spatial_mask(True)
        wmt = [jnp.where(mask_t, wst_ref[h], 0.0).astype(BF16) for h in range(N_HEADS)]
        for b in range(tm // GMLP_BLOCK):
            rows = slice(b * GMLP_BLOCK, (b + 1) * GMLP_BLOCK)
            for h in range(N_HEADS):
                cols = slice(h * HEAD, (h + 1) * HEAD)
                dfb = df_bf[rows, cols]
                dvn_scr[rows, cols] = _dot(wmt[h], dfb)
                dws_ref[h] += jnp.where(mask, _dot_nt(dfb, vn[rows, cols]), 0.0)
                dbs_ref[h] += jnp.sum(dfb.astype(F32), axis=1, keepdims=True)
        dvn = dvn_scr[...]
        sg_ref[r_lng:r_lng + 1, 0:D_A] += jnp.sum(dvn * xh, axis=0, keepdims=True)
        sg_ref[r_lnb:r_lnb + 1, 0:D_A] += jnp.sum(dvn, axis=0, keepdims=True)
        dxh = dvn * ln_g
        dgv = rstd * (dxh - jnp.mean(dxh, axis=-1, keepdims=True) - xh * jnp.mean(dxh * xh, axis=-1, keepdims=True))
        dz_ref[:, OFF_V:OFF_V + D_A] = (dgv * _gelu_grad(v, tv)).astype(BF16)

        sbn = _sigmoid(zn_ref[:, OFF_GB:OFF_GB + D_MODEL] + bg[:, D_MODEL:2 * D_MODEL])
        dpbe = jnp.concatenate([dpb, dmgn * sbn], axis=0)
        dybe = _dot_nt(dpbe, wb_ref[1])
        dyb, dybn = dybe[:tm].astype(BF16), dybe[tm:].astype(BF16)
        bgv = z_ref[:, OFF_BG:OFF_BG + D_B]
        cg = z_ref[:, OFF_CG:OFF_CG + D_B]
        hbv = z_ref[:, OFF_HB:OFF_HB + D_B]
        q = cg * hbv
        dz_ref[:, OFF_BG:OFF_BG + D_B] = dyb * cv_ref[...]
        dconv = dyb * bgv
        dconvn = dybn * zn_ref[:, OFF_BG:OFF_BG + D_B]
        dconvn = jnp.where(last, jnp.zeros_like(dconvn), dconvn)
        up1 = _shift_up(dconv, 1, dconvn)
        up2 = _shift_up(dconv, 2, dconvn)
        sg_ref[r_sc:r_sc + 1, 0:D_B] += _column_sums(q * up2)
        sg_ref[r_sc + 1:r_sc + 2, 0:D_B] += _column_sums(q * up1)
        sg_ref[r_sc + 2:r_sc + 3, 0:D_B] += _column_sums(q * dconv)
        w = wsc_ref[...].astype(BF16)
        dq = w[2:3] * dconv + w[1:2] * up1 + w[0:1] * up2
        dz_ref[:, OFF_CG:OFF_CG + D_B] = dq * hbv
        dz_ref[:, OFF_HB:OFF_HB + D_B] = dq * cg

    row = lambda w: pl.BlockSpec((tm, w), lambda i: (i, 0))
    full = lambda *s: pl.BlockSpec(s, lambda i: (0,) * len(s))
    nxt = lambda i: (jnp.minimum((i + 1) * hb, steps * hb - 1), 0)
    return _call(
        body, name=name, grid=(steps,), carry=carry,
        in_specs=[row(D_MODEL), pl.BlockSpec((HALO, D_MODEL), nxt),
                  row(D_IN), pl.BlockSpec((HALO, D_IN), nxt),
                  row(D_B), row(D_MODEL), row(D_MODEL),
                  full(1, 2 * D_MODEL), full(1, D_A), full(1, D_A),
                  full(N_HEADS, GMLP_BLOCK, GMLP_BLOCK), full(N_HEADS, GMLP_BLOCK, GMLP_BLOCK),
                  full(N_HEADS, GMLP_BLOCK, 1), full(3, D_B),
                  full(D_MODEL, D_MODEL), full(2, D_A, D_MODEL), _sheet_spec()],
        out_specs=[row(D_IN), row(D_MODEL), row(D_MODEL), full(N_HEADS, GMLP_BLOCK, GMLP_BLOCK),
                   full(N_HEADS, GMLP_BLOCK, 1), _sheet_spec()],
        out_shape=[jax.ShapeDtypeStruct((n, D_IN), BF16), jax.ShapeDtypeStruct((n, D_MODEL), BF16),
                   jax.ShapeDtypeStruct((n, D_MODEL), BF16),
                   jax.ShapeDtypeStruct((N_HEADS, GMLP_BLOCK, GMLP_BLOCK), F32),
                   jax.ShapeDtypeStruct((N_HEADS, GMLP_BLOCK, 1), F32), jax.ShapeDtypeStruct((SG_ROWS, SG_W), F32)],
        scratch_shapes=[pltpu.VMEM((tm, D_A), BF16), pltpu.VMEM((tm, D_A), F32)],
        args=(dx1, dx1, z, z, conv, pa, pb, b_gate.reshape(1, -1), ln_g.reshape(1, -1), ln_b.reshape(1, -1), w_s, w_s_t,
              b_s.reshape(N_HEADS, GMLP_BLOCK, 1), w_sc, w_out, wb, sheet))


def _matmul_tn(a, b, *, t1, tn, name, carry=None, pieces=1):
    n, k1 = a.shape
    k2 = b.shape[1]
    steps = n // tn
    w = k2 // pieces

    def body(a_ref, b_ref, *rest):
        o_refs, acc = rest[:pieces], rest[pieces]
        s = pl.program_id(1)

        @pl.when(s == 0)
        def _():
            acc[...] = jnp.zeros_like(acc)

        acc[...] += lax.dot_general(a_ref[...].astype(BF16), b_ref[...].astype(BF16), TN, preferred_element_type=F32)

        @pl.when(s == steps - 1)
        def _():
            for c, o_ref in enumerate(o_refs):
                o_ref[...] = acc[:, c * w:(c + 1) * w].astype(BF16)

    outs, carried = _call(
        body, name=name, grid=(k1 // t1, steps), carry=carry,
        in_specs=[pl.BlockSpec((tn, t1), lambda i, s: (s, i)),
                  pl.BlockSpec((tn, k2), lambda i, s: (s, 0))],
        out_specs=[pl.BlockSpec((t1, w), lambda i, s: (i, 0))] * pieces,
        out_shape=[jax.ShapeDtypeStruct((k1, w), BF16)] * pieces,
        scratch_shapes=[pltpu.VMEM((t1, k2), F32)],
        args=(a, b))
    return (outs[0] if pieces == 1 else list(outs)), carried


def _adamw_math(w, g, m, v):
    m = ADAM_B1 * m + (1.0 - ADAM_B1) * g
    v = ADAM_B2 * v + (1.0 - ADAM_B2) * (g * g)
    m_hat = m / (1.0 - ADAM_B1 ** ADAM_STEP)
    v_hat = v / (1.0 - ADAM_B2 ** ADAM_STEP)
    delta = -ADAM_LR * (m_hat / (jnp.sqrt(v_hat) + ADAM_EPS) + ADAM_WD * w)
    return delta, m, v


def _sum_parts(recvs, *, tr, name):
    _, r, c = recvs[0].shape

    def body(*refs):
        recv_refs, g_ref = refs[:DEPTH], refs[DEPTH]
        layer = pl.program_id(0)
        for l in range(DEPTH):
            @pl.when(layer == l)
            def _(l=l):
                g = recv_refs[l][0].astype(F32)
                for s in range(1, N_DEV):
                    g = g + recv_refs[l][s].astype(F32)
                g_ref[0] = g

    outs, _ = _call(
        body, name=name, grid=(DEPTH, r // tr),
        in_specs=[pl.BlockSpec((N_DEV, tr, c), lambda l, i: (0, i, 0))] * DEPTH,
        out_specs=[pl.BlockSpec((1, tr, c), lambda l, i: (l, i, 0))],
        out_shape=[jax.ShapeDtypeStruct((DEPTH, r, c), F32)],
        args=tuple(recvs))
    return outs[0]


def _adamw(w, g, m, v, *, tr, name):
    r, c = w.shape

    def body(w_ref, g_ref, m_ref, v_ref, d_ref, nm_ref, nv_ref):
        delta, nm, nv = _adamw_math(w_ref[...], g_ref[...], m_ref[...], v_ref[...])
        d_ref[...] = delta
        nm_ref[...] = nm
        nv_ref[...] = nv

    spec = pl.BlockSpec((tr, c), lambda i: (i, 0))
    outs, _ = _call(body, name=name, grid=(r // tr,), in_specs=[spec] * 4, out_specs=[spec] * 3,
                    out_shape=[jax.ShapeDtypeStruct((r, c), F32)] * 3, args=(w, g, m, v))
    return outs


def _sum_adamw(recvs, w, m, v, *, tr, name):
    _, r, c = w.shape
    blocks = len(recvs[0])
    flat = [piece for layer in recvs for piece in layer]

    def body(*refs):
        recv_refs = refs[:len(flat)]
        w_ref, m_ref, v_ref, g_ref, d_ref, nm_ref, nv_ref = refs[len(flat):]
        layer = pl.program_id(0)
        for l in range(DEPTH):
            @pl.when(layer == l)
            def _(l=l):
                cols = []
                for piece in recv_refs[l * blocks:(l + 1) * blocks]:
                    part = piece[0].astype(F32)
                    for s in range(1, N_DEV):
                        part = part + piece[s].astype(F32)
                    cols.append(part)
                g = cols[0] if blocks == 1 else jnp.concatenate(cols, axis=-1)
                delta, nm, nv = _adamw_math(w_ref[0], g, m_ref[0], v_ref[0])
                g_ref[0] = g
                d_ref[0] = delta
                nm_ref[0] = nm
                nv_ref[0] = nv

    spec = pl.BlockSpec((1, tr, c), lambda l, i: (l, i, 0))
    outs, _ = _call(
        body, name=name, grid=(DEPTH, r // tr),
        in_specs=[pl.BlockSpec((N_DEV, tr, c // blocks), lambda l, i: (0, i, 0))] * len(flat) + [spec] * 3,
        out_specs=[spec] * 4, out_shape=[jax.ShapeDtypeStruct((DEPTH, r, c), F32)] * 4,
        args=tuple(flat) + (w, m, v))
    return outs


def _adamw_small(sheet, extra, params, *, name):
    sheet_rows = dict(norm1_g=ROW_NORM1, b_gate=ROW_BGATE, gmlp_ln_g=ROW_LN_G, gmlp_ln_b=ROW_LN_B, norm2_g=ROW_NORM2,
                      b_ffn_conv=ROW_BFCONV)
    names = list(params)
    extra_names = list(extra)

    def body(*refs):
        sg_ref, refs = refs[0], refs[1:]
        extra_refs, refs = dict(zip(extra_names, refs[:len(extra_names)])), refs[len(extra_names):]
        ins, outs = refs[:3 * len(names)], refs[3 * len(names):]
        for j, key in enumerate(names):
            w_ref, m_ref, v_ref = ins[3 * j:3 * j + 3]
            g_ref, d_ref, nm_ref, nv_ref = outs[4 * j:4 * j + 4]
            if key in extra_refs:
                g_ref[...] = extra_refs[key][...]
            elif key == "final_g":
                g_ref[...] = sg_ref[ROW_FINAL:ROW_FINAL + 1, 0:D_MODEL]
            else:
                width = w_ref.shape[-1]
                for l in range(DEPTH):
                    row = SG_LAYER * l + sheet_rows[key]
                    g_ref[l:l + 1, :] = sg_ref[row:row + 1, 0:width]
            delta, nm, nv = _adamw_math(w_ref[...], g_ref[...], m_ref[...], v_ref[...])
            d_ref[...] = delta
            nm_ref[...] = nm
            nv_ref[...] = nv

    args = [sheet] + [extra[k] for k in extra_names] + [t for k in names for t in params[k]]
    vmem = pl.BlockSpec(memory_space=pltpu.VMEM)
    outs = pl.pallas_call(
        body, name=name, in_specs=[vmem] * len(args), out_specs=[vmem] * (4 * len(names)),
        out_shape=[jax.ShapeDtypeStruct(params[k][0].shape, F32) for k in names for _ in range(4)],
    )(*args)
    return {k: tuple(outs[4 * j:4 * j + 4]) for j, k in enumerate(names)}


def _rows(gathered):
    return gathered.reshape(N_DEV * gathered.shape[1], gathered.shape[2])


def _parts(full):
    return full.reshape(N_DEV, full.shape[0] // N_DEV, full.shape[1])


def kernel(x, norm1_g, w_in, b_gate, gmlp_ln_g, gmlp_ln_b, w_spatial, b_spatial, w_shortconv, w_branch, w_out, norm2_g, w_ffn_up, w_ffn_conv, b_ffn_conv, w_ffn_down, final_g, loss_target, m_norm1_g, m_w_in, m_b_gate, m_gmlp_ln_g, m_gmlp_ln_b, m_w_spatial, m_b_spatial, m_w_shortconv, m_w_branch, m_w_out, m_norm2_g, m_w_ffn_up, m_w_ffn_conv, m_b_ffn_conv, m_w_ffn_down, m_final_g, v_norm1_g, v_w_in, v_b_gate, v_gmlp_ln_g, v_gmlp_ln_b, v_w_spatial, v_b_spatial, v_w_shortconv, v_w_branch, v_w_out, v_norm2_g, v_w_ffn_up, v_w_ffn_conv, v_b_ffn_conv, v_w_ffn_down, v_final_g):
    n = x.shape[1]
    tm_in, tm, tn = 1024, 512, 2048
    x0 = x.reshape(n, D_MODEL)
    target = loss_target.reshape(n, D_MODEL)
    my_idx = 4 * lax.axis_index("x") + 2 * lax.axis_index("y") + lax.axis_index("c")
    sc_w, fc_w = D_B // N_DEV, D_FF // N_DEV

    sh_in = [w_in[l].T.astype(BF16) for l in range(DEPTH)]
    sh_up = [w_ffn_up[l].T.astype(BF16) for l in range(DEPTH)]
    sh_br = [w_branch[l].astype(BF16) for l in range(DEPTH)]
    sh_out = [w_out[l].astype(BF16) for l in range(DEPTH)]
    sh_down = [w_ffn_down[l].astype(BF16) for l in range(DEPTH)]
    taps = jnp.concatenate([w_shortconv, w_ffn_conv], axis=-1)

    def branch_weights(g):
        return g.transpose(1, 2, 0, 3).reshape(2, D_A, D_MODEL)

    g_in0, g_taps = _gather_now([sh_in[0], taps], name="gather_first")
    w_sc = [g_taps[:, l, :, :sc_w].transpose(1, 0, 2).reshape(3, D_B) for l in range(DEPTH)]
    w_fc = [g_taps[:, l, :, sc_w:].transpose(1, 0, 2).reshape(3, D_FF) for l in range(DEPTH)]
    w_s_t = [w_spatial[l].transpose(0, 2, 1) for l in range(DEPTH)]
    weights = [dict(), dict()]
    weights[0]["in_t"] = _rows(g_in0)
    saved = []
    xc = x0
    for l in range(DEPTH):
        p = weights[l]
        carry = _Gather([sh_br[0], sh_out[0]] if l == 0 else [sh_up[1]])
        (h, z), got = _norm_matmul(xc, norm1_g[l], p["in_t"], tm=tm_in, name=f"fwd_in_{l}", carry=carry)
        if l == 0:
            p["wb"], p["out"] = branch_weights(got[0]), _rows(got[1])
        else:
            p["up_t"] = _rows(got[0])
        carry = _Gather([sh_up[0]]) if l == 0 else None
        (ya, yb, conv, pa, pb, mg, x1), got = _mix_forward(
            z, xc, b_gate[l], gmlp_ln_g[l], gmlp_ln_b[l], w_spatial[l], b_spatial[l], w_sc[l], p["wb"], p["out"],
            tm=tm, name=f"fwd_mix_{l}", carry=carry)
        if l == 0:
            p["up_t"] = _rows(got[0])
        (h2, up), got = _norm_matmul(x1, norm2_g[l], p["up_t"], tm=tm, name=f"fwd_up_{l}", carry=_Gather([sh_down[l]]))
        p["down"] = _rows(got[0])
        carry = _Gather([sh_br[1], sh_out[1], sh_in[1]]) if l == 0 else None
        head = (final_g, target) if l == DEPTH - 1 else None
        outs, got = _ffn_forward(up, x1, w_fc[l], b_ffn_conv[l], p["down"], tm=tm, name=f"fwd_ffn_{l}", carry=carry, head=head)
        if l == 0:
            weights[1]["wb"], weights[1]["out"], weights[1]["in_t"] = branch_weights(got[0]), _rows(got[1]), _rows(got[2])
        gc, a = outs[0], outs[1]
        saved.append(dict(x=xc, h=h, z=z, ya=ya, yb=yb, conv=conv, pa=pa, pb=pb, mg=mg, x1=x1, h2=h2, up=up, gc=gc, a=a))
        xc = outs[2]
    dx, sheet = outs[2], outs[3]

    recv = [dict(), dict()]
    small_dws, small_dbs = [None] * DEPTH, [None] * DEPTH
    pending_in = None
    for l in reversed(range(DEPTH)):
        p, s = weights[l], saved[l]
        carry = _Exchange([pending_in]) if pending_in is not None else None
        (dup, sheet), got = _ffn_backward(dx, s["up"], s["gc"], w_fc[l], p["down"], sheet, l, tm=tm, name=f"bwd_ffn_{l}",
                                          carry=carry)
        if got is not None:
            recv[l + 1]["in_t"] = got[0]
        dw_down, _ = _matmul_tn(s["a"], dx, t1=D_FF // 2, tn=tn, name=f"dw_down_{l}")
        dw_up_t, got = _matmul_tn(dup, s["h2"], t1=2 * D_FF // 4, tn=tn, name=f"dw_up_{l}", pieces=2,
                                  carry=_Exchange([_parts(dw_down)]))
        recv[l]["down"] = got[0]
        (dx1, sheet), got_left = _matmul_norm_backward(
            dup, p["up_t"], s["x1"], norm2_g[l], dx, sheet, SG_LAYER * l + ROW_NORM2, tm=tm, name=f"bwd_up_{l}",
            carry=_Exchange([_parts(dw_up_t[0])]))
        dw_out, _ = _matmul_tn(s["mg"], dx1, t1=D_MODEL, tn=tn, name=f"dw_out_{l}")
        (dz, dpa, dpb, small_dws[l], small_dbs[l], sheet), got_right = _mix_backward(
            dx1, s["z"], s["conv"], s["pa"], s["pb"], b_gate[l], gmlp_ln_g[l], gmlp_ln_b[l], w_spatial[l], w_s_t[l],
            b_spatial[l], w_sc[l], p["out"], p["wb"], sheet, l, tm=tm, name=f"bwd_mix_{l}",
            carry=_Exchange([_parts(dw_up_t[1]), _parts(dw_out)]))
        recv[l]["up_t"], recv[l]["out"] = [got_left[0], got_right[0]], got_right[1]
        dw_bra_t, _ = _matmul_tn(dpa, s["ya"], t1=D_MODEL, tn=tn, name=f"dw_branch_a_{l}")
        dw_brb_t, _ = _matmul_tn(dpb, s["yb"], t1=D_MODEL, tn=tn, name=f"dw_branch_b_{l}")
        carry = _Exchange([_parts(dw_bra_t), _parts(dw_brb_t)])
        if l == 0:
            dbs = jnp.stack([t.reshape(N_HEADS, GMLP_BLOCK) for t in small_dbs]).reshape(DEPTH * N_HEADS, GMLP_BLOCK)
            carry = _Both(carry, _Gather([sheet, small_dws[0], small_dws[1], dbs]))
        dw_in_t, got = _matmul_tn(dz, s["h"], t1=D_IN // 4, tn=tn, name=f"dw_in_{l}", carry=carry)
        recv[l]["bra_t"], recv[l]["brb_t"] = got[:2]
        if l == 0:
            gathered_small = got[2:]
            send_sems, recv_sems, parts_thru, land_thru, token = _exchange_start(_parts(dw_in_t), name="exchange_w_in_0_start")
            (dx0, dg1_first), _ = _matmul_norm_backward(dz, p["in_t"], s["x"], norm1_g[l] + token[0, 0], dx1, None, 0,
                                                        tm=tm, name=f"bwd_in_{l}")
            recv[0]["in_t"] = _exchange_wait(send_sems, recv_sems, parts_thru, land_thru, dg1_first,
                                             name="exchange_w_in_0_wait")
        else:
            (dx0, sheet), _ = _matmul_norm_backward(dz, p["in_t"], s["x"], norm1_g[l], dx1, sheet,
                                                    SG_LAYER * l + ROW_NORM1, tm=tm, name=f"bwd_in_{l}")
            pending_in = _parts(dw_in_t)
        dx = dx0
    grad_x = dx.reshape(x.shape)

    results = {}
    both = lambda key: [recv[l][key] for l in range(DEPTH)]
    blocks = lambda key: [r if isinstance(r, list) else [r] for r in both(key)]
    swap = lambda t: t.transpose(0, 2, 1)
    for key, slab, (w, m, v), tr in [("w_in", "in_t", (w_in, m_w_in, v_w_in), 192),
                                     ("w_ffn_up", "up_t", (w_ffn_up, m_w_ffn_up, v_w_ffn_up), 176)]:
        outs = _sum_adamw(blocks(slab), swap(w), swap(m), swap(v), tr=tr, name=f"adamw_{key}")
        results[key] = tuple(swap(o) for o in outs)
    g_bra = _sum_parts(both("bra_t"), tr=128, name="sum_w_branch_a").transpose(0, 2, 1)
    g_brb = _sum_parts(both("brb_t"), tr=128, name="sum_w_branch_b").transpose(0, 2, 1)
    g_br = jnp.stack([g_bra, g_brb], axis=1)
    flat = lambda t: t.reshape(-1, t.shape[-1])
    outs = _adamw(flat(w_branch), flat(g_br), flat(m_w_branch), flat(v_w_branch), tr=512, name="adamw_w_branch")
    results["w_branch"] = (g_br,) + tuple(o.reshape(w_branch.shape) for o in outs)
    results["w_out"] = tuple(_sum_adamw(blocks("out"), w_out, m_w_out, v_w_out, tr=128, name="adamw_w_out"))
    results["w_ffn_down"] = tuple(_sum_adamw(blocks("down"), w_ffn_down, m_w_ffn_down, v_w_ffn_down, tr=176,
                                             name="adamw_w_ffn_down"))

    sheet, dws0, dws1, dbs = _sum_gathered(gathered_small, name="sum_small_grads")
    (dg1_first,) = _all_reduce_small([dg1_first], name="all_reduce_last_gain")
    sheet = sheet.at[ROW_NORM1, :D_MODEL].set(dg1_first[0])
    loss = sheet[ROW_LOSS, 0]
    swap_taps = lambda t: t.transpose(1, 0, 2)
    taps = lambda row, width: jnp.stack([sheet[SG_LAYER * l + row:SG_LAYER * l + row + 3, :width] for l in range(DEPTH)], axis=1)
    extra = dict(w_spatial=jnp.stack([dws0, dws1]), b_spatial=dbs.reshape(DEPTH, N_HEADS, GMLP_BLOCK),
                 w_shortconv=lax.dynamic_slice_in_dim(taps(ROW_SCONV, D_B), my_idx * sc_w, sc_w, axis=2),
                 w_ffn_conv=lax.dynamic_slice_in_dim(taps(ROW_FCONV, D_FF), my_idx * fc_w, fc_w, axis=2))
    small_w = dict(norm1_g=(norm1_g, m_norm1_g, v_norm1_g), b_gate=(b_gate, m_b_gate, v_b_gate),
                   gmlp_ln_g=(gmlp_ln_g, m_gmlp_ln_g, v_gmlp_ln_g), gmlp_ln_b=(gmlp_ln_b, m_gmlp_ln_b, v_gmlp_ln_b),
                   w_spatial=(w_spatial, m_w_spatial, v_w_spatial), b_spatial=(b_spatial, m_b_spatial, v_b_spatial),
                   w_shortconv=tuple(swap_taps(t) for t in (w_shortconv, m_w_shortconv, v_w_shortconv)), norm2_g=(norm2_g, m_norm2_g, v_norm2_g),
                   w_ffn_conv=tuple(swap_taps(t) for t in (w_ffn_conv, m_w_ffn_conv, v_w_ffn_conv)), b_ffn_conv=(b_ffn_conv, m_b_ffn_conv, v_b_ffn_conv),
                   final_g=tuple(t.reshape(1, D_MODEL) for t in (final_g, m_final_g, v_final_g)))
    results.update(_adamw_small(sheet, extra, small_w, name="adamw_small"))
    results["final_g"] = tuple(t.reshape(D_MODEL) for t in results["final_g"])
    for key in ("w_shortconv", "w_ffn_conv"):
        results[key] = tuple(swap_taps(t) for t in results[key])

    names = ["norm1_g", "w_in", "b_gate", "gmlp_ln_g", "gmlp_ln_b", "w_spatial", "b_spatial", "w_shortconv", "w_branch",
             "w_out", "norm2_g", "w_ffn_up", "w_ffn_conv", "b_ffn_conv", "w_ffn_down", "final_g"]
    return (loss, grad_x, *[results[k][0] for k in names], *[results[k][1] for k in names],
            *[results[k][2] for k in names], *[results[k][3] for k in names])
```

```python
import math

import jax
import jax.numpy as jnp
from jax import lax
from jax.experimental import pallas as pl
from jax.experimental.pallas import tpu as pltpu

F32 = jnp.float32
BF16 = jnp.bfloat16

N_DEV = 8
DEPTH = 2
D_MODEL = 1024
D_A = 512
D_B = 512
D_FF = 2816
D_IN = 4608
N_HEADS = 4
HEAD = 128
GMLP_BLOCK = 128
CAUSAL_CHUNK = 64
OFF_U, OFF_V, OFF_BG, OFF_CG, OFF_HB, OFF_GA, OFF_GB = 0, 512, 1024, 1536, 2048, 2560, 3584
RMS_EPS = 1e-6
LN_EPS = 1e-5
ADAM_LR, ADAM_B1, ADAM_B2, ADAM_EPS, ADAM_WD, ADAM_STEP = 0.001, 0.9, 0.999, 1e-08, 0.01, 10

SUBLANES, LANES = 8, 128
MATMUL_CHUNK = 512
HALO = 16
FFN_CHUNK = 256
SG_ROWS, SG_W, SG_LAYER = 40, D_FF, 16
ROW_NORM1, ROW_BGATE, ROW_LN_G, ROW_LN_B, ROW_SCONV, ROW_NORM2, ROW_FCONV, ROW_BFCONV = 0, 1, 2, 3, 4, 7, 8, 11
ROW_FINAL, ROW_LOSS = 32, 33
V7X_VMEM_BYTES = 64 << 20
VMEM_LIMIT = V7X_VMEM_BYTES - (8 << 20)
MESH = pl.DeviceIdType.MESH
GELU_C0 = 0.7978845608028654
GELU_C1 = 0.044715
NT = (((1,), (1,)), ((), ()))
TN = (((0,), (0,)), ((), ()))


def _dot(a, b):
    return jnp.dot(a, b, preferred_element_type=F32)


def _dot_nt(a, b):
    return lax.dot_general(a, b, NT, preferred_element_type=F32)


def _sigmoid(x):
    return 1.0 / (1.0 + jnp.exp(-x))


def _gelu_tanh(x):
    return jnp.tanh(GELU_C0 * (x + GELU_C1 * x * x * x))


def _gelu_grad(x, t):
    return 0.5 * (1.0 + t) + 0.5 * x * (1.0 - t * t) * GELU_C0 * (1.0 + 3.0 * GELU_C1 * x * x)


def _sublane_tile(dtype):
    return SUBLANES * (4 // jnp.dtype(dtype).itemsize)


def _shift_down(a, k, prev):
    p = prev.shape[0]
    r = pltpu.roll(a, k, 0)
    sub = _sublane_tile(a.dtype)
    head = r[0:sub]
    rid = lax.broadcasted_iota(jnp.int32, head.shape, 0)
    for j in range(k):
        head = jnp.where(rid == j, prev[p - k + j:p - k + j + 1, :], head)
    return jnp.concatenate([head, r[sub:]], axis=0)


def _shift_up(a, k, nxt):
    t = a.shape[0]
    r = pltpu.roll(a, t - k, 0)
    sub = _sublane_tile(a.dtype)
    tail = r[t - sub:t]
    rid = lax.broadcasted_iota(jnp.int32, tail.shape, 0)
    for j in range(k):
        tail = jnp.where(rid == sub - k + j, nxt[j:j + 1, :], tail)
    return jnp.concatenate([r[0:t - sub], tail], axis=0)


def _column_sums(p):
    if p.dtype.itemsize < 4:
        t = p.shape[0]
        p = p[:t // 2] + p[t // 2:]
        p = p[:t // 4] + p[t // 4:]
    return jnp.sum(p.astype(F32), axis=0, keepdims=True)


def _sheet_begin(step, sheet_in, sheet_out, first_row, rows):
    @pl.when(step == 0)
    def _():
        sheet_out[...] = sheet_in[...]
        sheet_out[first_row:first_row + rows, :] = jnp.zeros((rows, SG_W), F32)


def _sheet_spec():
    return pl.BlockSpec((SG_ROWS, SG_W), lambda i: (0, 0))


def _spatial_mask(transposed):
    ri = lax.broadcasted_iota(jnp.int32, (GMLP_BLOCK, GMLP_BLOCK), 0) // CAUSAL_CHUNK
    ci = lax.broadcasted_iota(jnp.int32, (GMLP_BLOCK, GMLP_BLOCK), 1) // CAUSAL_CHUNK
    return (ri <= ci) if transposed else (ci <= ri)


def _gmlp_forward(u, v, ln_g, ln_b, ws_ref, bs_ref, f_scr):
    tm = u.shape[0]
    tu = _gelu_tanh(u)
    tv = _gelu_tanh(v)
    gu = 0.5 * u * (1.0 + tu)
    gv = 0.5 * v * (1.0 + tv)
    mu = jnp.mean(gv, axis=-1, keepdims=True)
    cen = gv - mu
    rstd = lax.rsqrt(jnp.mean(cen * cen, axis=-1, keepdims=True) + LN_EPS)
    xh = cen * rstd
    vn = (xh * ln_g + ln_b).astype(BF16)
    mask = _spatial_mask(False)
    wm = [jnp.where(mask, ws_ref[h], 0.0).astype(BF16) for h in range(N_HEADS)]
    for b in range(tm // GMLP_BLOCK):
        rows = slice(b * GMLP_BLOCK, (b + 1) * GMLP_BLOCK)
        for h in range(N_HEADS):
            cols = slice(h * HEAD, (h + 1) * HEAD)
            f_scr[rows, cols] = (_dot(wm[h], vn[rows, cols]) + bs_ref[h]).astype(f_scr.dtype)
    return gu, tu, tv, xh, rstd, vn, f_scr[...]


def _position():
    return lax.axis_index("x"), lax.axis_index("y"), lax.axis_index("c")


def _handshake(peers):
    barrier = pltpu.get_barrier_semaphore()
    for peer in peers:
        pl.semaphore_signal(barrier, inc=1, device_id=peer, device_id_type=MESH)
    pl.semaphore_wait(barrier, len(peers))


class _Gather:
    collective_id = 1

    def __init__(self, arrays):
        self.arrays = list(arrays)
        self.out_shape = [jax.ShapeDtypeStruct((N_DEV,) + a.shape, a.dtype) for a in self.arrays]
        self.base = 0

    def barrier(self):
        x, y, c = _position()
        _handshake([(x, y, 1 - c), (1 - x, y, c), (x, 1 - y, c), (1 - x, 1 - y, c)])

    def _plan(self, ins, outs, sems):
        send_sems, recv_sems, local_sems = sems
        x, y, c = _position()
        me, sibling = (x, y, c), (x, y, 1 - c)
        chips = [(1 - x, y), (x, 1 - y), (1 - x, 1 - y)]

        def slot(a, p):
            return outs[a].at[4 * p[0] + 2 * p[1] + p[2]]

        def copy(a, k, block, to, src=None):
            return pltpu.make_async_remote_copy(
                src_ref=slot(a, block) if src is None else src, dst_ref=slot(a, block),
                send_sem=send_sems.at[self.base + a, k], recv_sem=recv_sems.at[self.base + a, k],
                device_id=to, device_id_type=MESH)

        n = len(self.arrays)

        def mine():
            return [pltpu.make_async_copy(ins[a], slot(a, me), local_sems.at[self.base + a]) for a in range(n)]

        def first():
            out = []
            for a in range(n):
                out.append(copy(a, 0, me, sibling, src=ins[a]))
                out += [copy(a, 1 + j, me, (*chip, c), src=ins[a]) for j, chip in enumerate(chips)]
            return out

        def arrivals():
            return [copy(a, 1 + j, (*chip, c), me) for j, chip in enumerate(chips) for a in range(n)]

        def relays():
            return [copy(a, 4 + j, (*chip, c), sibling) for j, chip in enumerate(chips) for a in range(n)]

        def from_sibling():
            out = [copy(a, 0, sibling, me) for a in range(n)]
            return out + [copy(a, 4 + j, (*chip, 1 - c), me) for j, chip in enumerate(chips) for a in range(n)]

        return mine, first, arrivals, relays, from_sibling

    def start(self, ins, outs, sems):
        mine, first, _, _, _ = self._plan(ins, outs, sems)
        for cp in mine() + first():
            cp.start()

    def relay(self, ins, outs, sems):
        _, _, arrivals, relays, _ = self._plan(ins, outs, sems)
        for arrived, onward in zip(arrivals(), relays()):
            arrived.wait_recv()
            onward.start()

    def finish(self, ins, outs, sems):
        mine, first, _, relays, from_sibling = self._plan(ins, outs, sems)
        for cp in from_sibling():
            cp.wait_recv()
        for cp in first() + relays():
            cp.wait_send()
        for cp in mine():
            cp.wait()


class _Exchange:
    collective_id = 0

    def __init__(self, arrays):
        self.arrays = list(arrays)
        self.out_shape = [jax.ShapeDtypeStruct(a.shape, a.dtype) for a in self.arrays]
        self.base = 0

    def barrier(self):
        x, y, c = _position()
        _handshake([(x ^ dx, y ^ dy, c ^ dc) for dx in (0, 1) for dy in (0, 1) for dc in (0, 1) if dx or dy or dc])

    def _plan(self, ins, outs, sems):
        send_sems, recv_sems, local_sems = sems
        x, y, c = _position()
        my_idx = 4 * x + 2 * y + c
        n = len(self.arrays)
        offsets = [(dx, dy, dc) for dx in (0, 1) for dy in (0, 1) for dc in (0, 1) if (dx, dy, dc) != (0, 0, 0)]

        def mine():
            return [pltpu.make_async_copy(ins[a].at[my_idx], outs[a].at[my_idx], local_sems.at[self.base + a])
                    for a in range(n)]

        def remote(arriving):
            out = []
            for k, (dx, dy, dc) in enumerate(offsets):
                px, py, pc = x ^ dx, y ^ dy, c ^ dc
                p_idx = 4 * px + 2 * py + pc
                for a in range(n):
                    out.append(pltpu.make_async_remote_copy(
                        src_ref=ins[a].at[p_idx], dst_ref=outs[a].at[p_idx if arriving else my_idx],
                        send_sem=send_sems.at[self.base + a, k], recv_sem=recv_sems.at[self.base + a, k],
                        device_id=(px, py, pc), device_id_type=MESH))
            return out

        return mine, remote

    def start(self, ins, outs, sems):
        mine, remote = self._plan(ins, outs, sems)
        for cp in mine() + remote(False):
            cp.start()

    def relay(self, ins, outs, sems):
        pass

    def finish(self, ins, outs, sems):
        mine, remote = self._plan(ins, outs, sems)
        for cp in remote(True):
            cp.wait_recv()
        for cp in remote(False):
            cp.wait_send()
        for cp in mine():
            cp.wait()


def _exchange_start(parts, *, name):
    hbm = pl.BlockSpec(memory_space=pltpu.HBM)
    sem = pl.BlockSpec(memory_space=pltpu.SEMAPHORE)
    offsets = [(dx, dy, dc) for dx in (0, 1) for dy in (0, 1) for dc in (0, 1) if dx or dy or dc]

    def body(parts_ref, send_sems, recv_sems, parts_thru, land_ref, token):
        x, y, c = _position()
        my_idx = 4 * x + 2 * y + c
        pltpu.make_async_copy(parts_ref.at[my_idx], land_ref.at[my_idx], send_sems.at[7]).start()
        _handshake([(x ^ dx, y ^ dy, c ^ dc) for dx, dy, dc in offsets])
        for k, (dx, dy, dc) in enumerate(offsets):
            px, py, pc = x ^ dx, y ^ dy, c ^ dc
            pltpu.make_async_remote_copy(
                src_ref=parts_ref.at[4 * px + 2 * py + pc], dst_ref=land_ref.at[my_idx],
                send_sem=send_sems.at[k], recv_sem=recv_sems.at[k],
                device_id=(px, py, pc), device_id_type=MESH).start()
        token[...] = jnp.zeros_like(token)

    return pl.pallas_call(
        body, name=name,
        out_shape=(pltpu.SemaphoreType.DMA((8,)), pltpu.SemaphoreType.DMA((7,)), pltpu.HBM(parts.shape, parts.dtype),
                   pltpu.HBM(parts.shape, parts.dtype), jax.ShapeDtypeStruct((SUBLANES, LANES), F32)),
        in_specs=(hbm,), out_specs=(sem, sem, hbm, hbm, pl.BlockSpec(memory_space=pltpu.VMEM)),
        input_output_aliases={0: 2},
        compiler_params=pltpu.CompilerParams(has_side_effects=pltpu.SideEffectType.DATAFLOW_SIDE_EFFECTING,
                                             collective_id=2),
    )(pltpu.with_memory_space_constraint(parts, pltpu.HBM))


def _exchange_wait(send_sems, recv_sems, parts_thru, land_thru, after, *, name):
    hbm = pl.BlockSpec(memory_space=pltpu.HBM)
    sem = pl.BlockSpec(memory_space=pltpu.SEMAPHORE)
    offsets = [(dx, dy, dc) for dx in (0, 1) for dy in (0, 1) for dc in (0, 1) if dx or dy or dc]

    def body(parts_ref, land_ref, send_sems, recv_sems, after_ref, parts_dead, got_ref):
        x, y, c = _position()
        my_idx = 4 * x + 2 * y + c
        pltpu.make_async_copy(parts_ref.at[my_idx], land_ref.at[my_idx], send_sems.at[7]).wait()
        for k, (dx, dy, dc) in enumerate(offsets):
            px, py, pc = x ^ dx, y ^ dy, c ^ dc
            p_idx = 4 * px + 2 * py + pc
            copy = pltpu.make_async_remote_copy(
                src_ref=parts_ref.at[p_idx], dst_ref=land_ref.at[p_idx], send_sem=send_sems.at[k],
                recv_sem=recv_sems.at[k], device_id=(px, py, pc), device_id_type=MESH)
            copy.wait_send()
            copy.wait_recv()

    return pl.pallas_call(
        body, name=name,
        out_shape=(pltpu.HBM(parts_thru.shape, parts_thru.dtype), pltpu.HBM(land_thru.shape, land_thru.dtype)),
        in_specs=(hbm, hbm, sem, sem, pl.BlockSpec(memory_space=pl.ANY)), out_specs=(hbm, hbm),
        input_output_aliases={0: 0, 1: 1},
        compiler_params=pltpu.CompilerParams(has_side_effects=pltpu.SideEffectType.DATAFLOW_SIDE_EFFECTING),
    )(parts_thru, land_thru, send_sems, recv_sems, after)[1]


class _Both:
    def __init__(self, *carries):
        self.carries = carries
        self.arrays = [a for c in carries for a in c.arrays]
        self.out_shape = [s for c in carries for s in c.out_shape]
        first = 0
        for c in carries:
            c.base = first
            first += len(c.arrays)
        self.collective_id = min(c.collective_id for c in carries)

    def barrier(self):
        min(self.carries, key=lambda c: c.collective_id).barrier()

    def _each(self, method, ins, outs, sems):
        for c in self.carries:
            rows = slice(c.base, c.base + len(c.arrays))
            getattr(c, method)(ins[rows], outs[rows], sems)

    def start(self, ins, outs, sems):
        self._each("start", ins, outs, sems)

    def relay(self, ins, outs, sems):
        self._each("relay", ins, outs, sems)

    def finish(self, ins, outs, sems):
        self._each("finish", ins, outs, sems)


def _call(body, *, name, grid, in_specs, out_specs, out_shape, args, scratch_shapes=(), carry=None):
    n_in, n_out, n_scr = len(in_specs), len(out_specs), len(scratch_shapes)
    params = pltpu.CompilerParams(dimension_semantics=("arbitrary",) * len(grid), vmem_limit_bytes=VMEM_LIMIT)
    if carry is None:
        outs = pl.pallas_call(body, name=name, grid=grid, in_specs=in_specs, out_specs=out_specs, out_shape=out_shape,
                              scratch_shapes=list(scratch_shapes), compiler_params=params)(*args)
        return outs, None
    m = len(carry.arrays)
    total = math.prod(grid)

    def wrapped(*refs):
        ins, refs = refs[:n_in], refs[n_in:]
        c_ins, refs = refs[:m], refs[m:]
        outs, refs = refs[:n_out], refs[n_out:]
        c_outs, refs = refs[:m], refs[m:]
        scr, sems = refs[:n_scr], refs[n_scr:]
        flat = pl.program_id(0)
        for d in range(1, len(grid)):
            flat = flat * grid[d] + pl.program_id(d)

        @pl.when(flat == 0)
        def _():
            carry.barrier()
            carry.start(c_ins, c_outs, sems)

        body(*ins, *outs, *scr)

        @pl.when(flat == total - 2)
        def _():
            carry.relay(c_ins, c_outs, sems)

        @pl.when(flat == total - 1)
        def _():
            carry.finish(c_ins, c_outs, sems)

    any_spec = pl.BlockSpec(memory_space=pl.ANY)
    sem_shapes = [pltpu.SemaphoreType.DMA((m, 7)), pltpu.SemaphoreType.DMA((m, 7)), pltpu.SemaphoreType.DMA((m,))]
    params = pltpu.CompilerParams(dimension_semantics=("arbitrary",) * len(grid), vmem_limit_bytes=VMEM_LIMIT,
                                  collective_id=carry.collective_id)
    outs = pl.pallas_call(
        wrapped, name=name, grid=grid,
        in_specs=list(in_specs) + [any_spec] * m, out_specs=list(out_specs) + [any_spec] * m,
        out_shape=list(out_shape) + carry.out_shape,
        scratch_shapes=list(scratch_shapes) + sem_shapes, compiler_params=params)(*args, *carry.arrays)
    return outs[:n_out], outs[n_out:]


def _gather_now(arrays, *, name):
    carry = _Gather(arrays)
    m = len(arrays)

    def body(*refs):
        ins, outs, sems = refs[:m], refs[m:2 * m], refs[2 * m:]
        carry.barrier()
        carry.start(ins, outs, sems)
        carry.relay(ins, outs, sems)
        carry.finish(ins, outs, sems)

    any_spec = pl.BlockSpec(memory_space=pl.ANY)
    return pl.pallas_call(
        body, name=name, in_specs=[any_spec] * m, out_specs=[any_spec] * m, out_shape=carry.out_shape,
        scratch_shapes=[pltpu.SemaphoreType.DMA((m, 7)), pltpu.SemaphoreType.DMA((m, 7)),
                        pltpu.SemaphoreType.DMA((m,))],
        compiler_params=pltpu.CompilerParams(collective_id=carry.collective_id),
    )(*arrays)


def _all_reduce_small(arrs, *, name):
    n = len(arrs)

    def body(*refs):
        ins, outs, bufs = refs[:n], refs[n:2 * n], refs[2 * n:3 * n]
        send_sems, recv_sems = refs[3 * n:]
        x, y, c = _position()
        me, sibling = (x, y, c), (x, y, 1 - c)
        chips = [(1 - x, y), (x, 1 - y), (1 - x, 1 - y)]
        _handshake([sibling] + [(*chip, c) for chip in chips])

        def copy(a, k, block, to, src=None):
            slot = bufs[a].at[4 * block[0] + 2 * block[1] + block[2]]
            return pltpu.make_async_remote_copy(
                src_ref=slot if src is None else src, dst_ref=slot,
                send_sem=send_sems.at[a, k], recv_sem=recv_sems.at[a, k], device_id=to, device_id_type=MESH)

        first = []
        for a in range(n):
            first.append(copy(a, 0, me, sibling, src=ins[a]))
            first += [copy(a, 1 + j, me, (*chip, c), src=ins[a]) for j, chip in enumerate(chips)]
        for cp in first:
            cp.start()
        passed = []
        for j, chip in enumerate(chips):
            for a in range(n):
                copy(a, 1 + j, (*chip, c), me).wait_recv()
                cp = copy(a, 4 + j, (*chip, c), sibling)
                cp.start()
                passed.append(cp)
        for a in range(n):
            copy(a, 0, sibling, me).wait_recv()
            for j, chip in enumerate(chips):
                copy(a, 4 + j, (*chip, 1 - c), me).wait_recv()
        for cp in first + passed:
            cp.wait_send()
        my_idx = 4 * x + 2 * y + c
        for a in range(n):
            acc = jnp.zeros(ins[a].shape, F32)
            for s in range(N_DEV):
                acc = acc + jnp.where(my_idx == s, ins[a][...], bufs[a][s])
            outs[a][...] = acc

    vmem = pl.BlockSpec(memory_space=pltpu.VMEM)
    return pl.pallas_call(
        body, name=name, in_specs=[vmem] * n, out_specs=[vmem] * n,
        out_shape=[jax.ShapeDtypeStruct(a.shape, F32) for a in arrs],
        scratch_shapes=[pltpu.VMEM((N_DEV,) + a.shape, F32) for a in arrs]
        + [pltpu.SemaphoreType.DMA((n, 7)), pltpu.SemaphoreType.DMA((n, 7))],
        compiler_params=pltpu.CompilerParams(vmem_limit_bytes=VMEM_LIMIT, collective_id=_Gather.collective_id),
    )(*arrs)


def _sum_gathered(arrs, *, name):
    n = len(arrs)

    def body(*refs):
        for in_ref, out_ref in zip(refs[:n], refs[n:]):
            acc = in_ref[0]
            for s in range(1, N_DEV):
                acc = acc + in_ref[s]
            out_ref[...] = acc

    vmem = pl.BlockSpec(memory_space=pltpu.VMEM)
    return pl.pallas_call(
        body, name=name, in_specs=[vmem] * n, out_specs=[vmem] * n,
        out_shape=[jax.ShapeDtypeStruct(a.shape[1:], F32) for a in arrs],
        compiler_params=pltpu.CompilerParams(vmem_limit_bytes=VMEM_LIMIT),
    )(*arrs)


def _norm_matmul(x, g, w_t, *, tm, name, carry=None):
    n, d = x.shape
    c = w_t.shape[0]
    ch = MATMUL_CHUNK

    def body(x_ref, g_ref, wt_ref, h_ref, z_ref):
        xv = x_ref[...]
        r = lax.rsqrt(jnp.mean(xv * xv, axis=-1, keepdims=True) + RMS_EPS)
        h = (xv * r * g_ref[...]).astype(BF16)
        h_ref[...] = h
        for c0 in range(0, c, ch):
            z_ref[:, c0:c0 + ch] = _dot_nt(h, wt_ref[c0:c0 + ch, :]).astype(BF16)

    return _call(
        body, name=name, grid=(n // tm,), carry=carry,
        in_specs=[pl.BlockSpec((tm, d), lambda i: (i, 0)),
                  pl.BlockSpec((1, d), lambda i: (0, 0)),
                  pl.BlockSpec((c, d), lambda i: (0, 0))],
        out_specs=[pl.BlockSpec((tm, d), lambda i: (i, 0)),
                   pl.BlockSpec((tm, c), lambda i: (i, 0))],
        out_shape=[jax.ShapeDtypeStruct((n, d), BF16), jax.ShapeDtypeStruct((n, c), BF16)],
        args=(x, g.reshape(1, d), w_t))


def _mix_forward(z, x, b_gate, ln_g, ln_b, w_s, b_s, w_sc, wb, w_out, *, tm, name, carry=None):
    n = z.shape[0]
    hb = tm // HALO

    def body(z_ref, zp_ref, x_ref, bg_ref, lng_ref, lnb_ref, ws_ref, bs_ref, wsc_ref, wb_ref, wo_ref,
             ya_ref, yb_ref, cv_ref, pa_ref, pb_ref, mg_ref, x1_ref, f_scr):
        i = pl.program_id(0)
        u = z_ref[:, OFF_U:OFF_U + D_A]
        v = z_ref[:, OFF_V:OFF_V + D_A].astype(F32)
        gu, _, _, _, _, _, f = _gmlp_forward(u, v, lng_ref[...], lnb_ref[...], ws_ref, bs_ref, f_scr)
        ya = gu * f
        ya_ref[...] = ya

        q = z_ref[:, OFF_CG:OFF_CG + D_B] * z_ref[:, OFF_HB:OFF_HB + D_B]
        qp = zp_ref[:, OFF_CG:OFF_CG + D_B] * zp_ref[:, OFF_HB:OFF_HB + D_B]
        qp = jnp.where(i > 0, qp, jnp.zeros_like(qp))
        w = wsc_ref[...].astype(BF16)
        conv = w[0:1] * _shift_down(q, 2, qp) + w[1:2] * _shift_down(q, 1, qp) + w[2:3] * q
        cv_ref[...] = conv
        yb = z_ref[:, OFF_BG:OFF_BG + D_B] * conv
        yb_ref[...] = yb

        pa = _dot(ya, wb_ref[0]).astype(BF16)
        pb = _dot(yb, wb_ref[1]).astype(BF16)
        pa_ref[...] = pa
        pb_ref[...] = pb
        bg = bg_ref[...].astype(BF16)
        sa = _sigmoid(z_ref[:, OFF_GA:OFF_GA + D_MODEL] + bg[:, 0:D_MODEL])
        sb = _sigmoid(z_ref[:, OFF_GB:OFF_GB + D_MODEL] + bg[:, D_MODEL:2 * D_MODEL])
        mg = sa * pa + sb * pb
        mg_ref[...] = mg
        x1_ref[...] = x_ref[...] + _dot(mg, wo_ref[...])

    row = lambda w: pl.BlockSpec((tm, w), lambda i: (i, 0))
    full = lambda *s: pl.BlockSpec(s, lambda i: (0,) * len(s))
    bf = lambda w: jax.ShapeDtypeStruct((n, w), BF16)
    return _call(
        body, name=name, grid=(n // tm,), carry=carry,
        in_specs=[row(D_IN),
                  pl.BlockSpec((HALO, D_IN), lambda i: (jnp.maximum(i * hb - 1, 0), 0)),
                  row(D_MODEL), full(1, 2 * D_MODEL), full(1, D_A), full(1, D_A),
                  full(N_HEADS, GMLP_BLOCK, GMLP_BLOCK), full(N_HEADS, GMLP_BLOCK, 1), full(3, D_B),
                  full(2, D_A, D_MODEL), full(D_MODEL, D_MODEL)],
        out_specs=[row(D_A), row(D_B), row(D_B), row(D_MODEL), row(D_MODEL), row(D_MODEL), row(D_MODEL)],
        out_shape=[bf(D_A), bf(D_B), bf(D_B), bf(D_MODEL), bf(D_MODEL), bf(D_MODEL),
                   jax.ShapeDtypeStruct((n, D_MODEL), F32)],
        scratch_shapes=[pltpu.VMEM((tm, D_A), BF16)],
        args=(z, z, x, b_gate.reshape(1, -1), ln_g.reshape(1, -1), ln_b.reshape(1, -1), w_s,
              b_s.reshape(N_HEADS, GMLP_BLOCK, 1), w_sc, wb, w_out))


def _loss_tile(xv, gv, tv):
    d = xv.shape[-1]
    r = lax.rsqrt(jnp.mean(xv * xv, axis=-1, keepdims=True) + RMS_EPS)
    xh = xv * r
    e = xh * gv - tv
    per_row = jnp.sum(e * e, axis=-1, keepdims=True) * (0.5 / d)
    dy = e * (1.0 / d)
    dxh = dy * gv
    dx = r * (dxh - xh * jnp.mean(dxh * xh, axis=-1, keepdims=True))
    return dx, jnp.sum(per_row, axis=0, keepdims=True), jnp.sum(dy * xh, axis=0, keepdims=True)


def _ffn_forward(up, x1, w_fc, b_fc, w_down, *, tm, name, carry=None, head=None):
    n = up.shape[0]
    hb = tm // HALO
    n_in = 6 if head is None else 8

    def body(*refs):
        up_ref, upp_ref, x1_ref, wfc_ref, bfc_ref, wd_ref = refs[:6]
        gc_ref, a_ref, out_ref = refs[n_in:n_in + 3]
        acc = refs[-1]
        i = pl.program_id(0)
        acc[...] = x1_ref[...]
        for c0 in range(0, D_FF, FFN_CHUNK):
            cols = slice(c0, c0 + FFN_CHUNK)
            gate = up_ref[:, cols]
            val = up_ref[:, D_FF + c0:D_FF + c0 + FFN_CHUNK]
            gp = upp_ref[:, cols]
            gp = jnp.where(i > 0, gp, jnp.zeros_like(gp))
            w = wfc_ref[:, cols].astype(BF16)
            gc = (w[0:1] * _shift_down(gate, 2, gp) + w[1:2] * _shift_down(gate, 1, gp) + w[2:3] * gate
                  + bfc_ref[:, cols].astype(BF16))
            gc_ref[:, cols] = gc
            a = gc * _sigmoid(gc) * val
            a_ref[:, cols] = a
            acc[...] += _dot(a, wd_ref[cols, :])
        if head is None:
            out_ref[...] = acc[...]
        else:
            g_ref, t_ref = refs[6:8]
            sg_ref = refs[n_in + 3]

            @pl.when(i == 0)
            def _():
                sg_ref[...] = jnp.zeros_like(sg_ref)

            dx, loss, dg = _loss_tile(acc[...], g_ref[...], t_ref[...])
            out_ref[...] = dx
            sg_ref[ROW_LOSS:ROW_LOSS + 1, 0:LANES] += jnp.broadcast_to(loss, (1, LANES))
            sg_ref[ROW_FINAL:ROW_FINAL + 1, 0:D_MODEL] += dg

    row = lambda w: pl.BlockSpec((tm, w), lambda i: (i, 0))
    full = lambda r, c: pl.BlockSpec((r, c), lambda i: (0, 0))
    in_specs = [row(2 * D_FF), pl.BlockSpec((HALO, D_FF), lambda i: (jnp.maximum(i * hb - 1, 0), 0)), row(D_MODEL),
                full(3, D_FF), full(1, D_FF), full(D_FF, D_MODEL)]
    out_specs = [row(D_FF), row(D_FF), row(D_MODEL)]
    out_shape = [jax.ShapeDtypeStruct((n, D_FF), BF16), jax.ShapeDtypeStruct((n, D_FF), BF16),
                 jax.ShapeDtypeStruct((n, D_MODEL), F32)]
    args = (up, up, x1, w_fc, b_fc.reshape(1, -1), w_down)
    if head is not None:
        in_specs += [full(1, D_MODEL), row(D_MODEL)]
        out_specs += [full(SG_ROWS, SG_W)]
        out_shape += [jax.ShapeDtypeStruct((SG_ROWS, SG_W), F32)]
        args += (head[0].reshape(1, -1), head[1])
    return _call(body, name=name, grid=(n // tm,), carry=carry, in_specs=in_specs, out_specs=out_specs,
                 out_shape=out_shape, scratch_shapes=[pltpu.VMEM((tm, D_MODEL), F32)], args=args)


def _ffn_backward(dx2, up, gc, w_fc, w_down, sheet, layer, *, tm, name, carry=None):
    n = up.shape[0]
    steps = n // tm
    hb = tm // HALO
    row = SG_LAYER * layer + ROW_FCONV

    def body(dx_ref, dxn_ref, up_ref, upn_ref, gc_ref, gcn_ref, wfc_ref, wd_ref, sg_in, dup_ref, sg_ref):
        i = pl.program_id(0)
        last = i == steps - 1
        _sheet_begin(i, sg_in, sg_ref, row, 4)

        dxe = jnp.concatenate([dx_ref[...], dxn_ref[...]], axis=0).astype(BF16)
        for c0 in range(0, D_FF, FFN_CHUNK):
            cols = slice(c0, c0 + FFN_CHUNK)
            vcols = slice(D_FF + c0, D_FF + c0 + FFN_CHUNK)
            dae = _dot_nt(dxe, wd_ref[cols, :])
            da, dan = dae[:tm], dae[tm:]
            gate = up_ref[:, cols]
            val = up_ref[:, vcols]
            gcv = gc_ref[:, cols]
            s = _sigmoid(gcv)
            dab = da.astype(BF16)
            dup_ref[:, vcols] = dab * (gcv * s)
            dgc = dab * val * (s * (1.0 + gcv * (1.0 - s)))
            gcn = gcn_ref[:, cols]
            sn = _sigmoid(gcn)
            dgcn = dan.astype(BF16) * upn_ref[:, vcols] * (sn * (1.0 + gcn * (1.0 - sn)))
            dgcn = jnp.where(last, jnp.zeros_like(dgcn), dgcn)
            up1 = _shift_up(dgc, 1, dgcn)
            up2 = _shift_up(dgc, 2, dgcn)
            w = wfc_ref[:, cols].astype(BF16)
            dup_ref[:, cols] = w[2:3] * dgc + w[1:2] * up1 + w[0:1] * up2
            sg_ref[row:row + 1, cols] += _column_sums(gate * up2)
            sg_ref[row + 1:row + 2, cols] += _column_sums(gate * up1)
            sg_ref[row + 2:row + 3, cols] += _column_sums(gate * dgc)
            sg_ref[row + 3:row + 4, cols] += _column_sums(dgc)

    nxt = lambda i: (jnp.minimum((i + 1) * hb, steps * hb - 1), 0)
    return _call(
        body, name=name, grid=(steps,), carry=carry,
        in_specs=[pl.BlockSpec((tm, D_MODEL), lambda i: (i, 0)),
                  pl.BlockSpec((HALO, D_MODEL), nxt),
                  pl.BlockSpec((tm, 2 * D_FF), lambda i: (i, 0)),
                  pl.BlockSpec((HALO, 2 * D_FF), nxt),
                  pl.BlockSpec((tm, D_FF), lambda i: (i, 0)),
                  pl.BlockSpec((HALO, D_FF), nxt),
                  pl.BlockSpec((3, D_FF), lambda i: (0, 0)),
                  pl.BlockSpec((D_FF, D_MODEL), lambda i: (0, 0)), _sheet_spec()],
        out_specs=[pl.BlockSpec((tm, 2 * D_FF), lambda i: (i, 0)), _sheet_spec()],
        out_shape=[jax.ShapeDtypeStruct((n, 2 * D_FF), BF16), jax.ShapeDtypeStruct((SG_ROWS, SG_W), F32)],
        args=(dx2, dx2, up, up, gc, gc, w_fc, w_down, sheet))


def _matmul_norm_backward(dz, w_t, x, g, dres, sheet, row, *, tm, name, carry=None):
    n, c = dz.shape
    d = x.shape[1]
    ch = MATMUL_CHUNK

    def body(dz_ref, wt_ref, x_ref, g_ref, dres_ref, *rest):
        i = pl.program_id(0)
        if sheet is None:
            dx_ref, sg_ref = rest

            @pl.when(i == 0)
            def _():
                sg_ref[...] = jnp.zeros_like(sg_ref)
        else:
            sg_in, dx_ref, sg_ref = rest
            _sheet_begin(i, sg_in, sg_ref, row, 1)

        dh = _dot(dz_ref[:, 0:ch], wt_ref[0:ch, :])
        for c0 in range(ch, c, ch):
            dh += _dot(dz_ref[:, c0:c0 + ch], wt_ref[c0:c0 + ch, :])
        xv = x_ref[...]
        r = lax.rsqrt(jnp.mean(xv * xv, axis=-1, keepdims=True) + RMS_EPS)
        xh = xv * r
        sg_ref[row:row + 1, 0:d] += jnp.sum(dh * xh, axis=0, keepdims=True)
        dxh = dh * g_ref[...]
        dx_ref[...] = dres_ref[...] + r * (dxh - xh * jnp.mean(dxh * xh, axis=-1, keepdims=True))

    in_specs = [pl.BlockSpec((tm, c), lambda i: (i, 0)),
                pl.BlockSpec((c, d), lambda i: (0, 0)),
                pl.BlockSpec((tm, d), lambda i: (i, 0)),
                pl.BlockSpec((1, d), lambda i: (0, 0)),
                pl.BlockSpec((tm, d), lambda i: (i, 0))]
    args = (dz, w_t, x, g.reshape(1, d), dres)
    if sheet is None:
        small_spec, small_shape = pl.BlockSpec((8, d), lambda i: (0, 0)), jax.ShapeDtypeStruct((8, d), F32)
    else:
        in_specs, args = in_specs + [_sheet_spec()], args + (sheet,)
        small_spec, small_shape = _sheet_spec(), jax.ShapeDtypeStruct((SG_ROWS, SG_W), F32)
    return _call(
        body, name=name, grid=(n // tm,), carry=carry, in_specs=in_specs,
        out_specs=[pl.BlockSpec((tm, d), lambda i: (i, 0)), small_spec],
        out_shape=[jax.ShapeDtypeStruct((n, d), F32), small_shape], args=args)


def _mix_backward(dx1, z, conv, pa, pb, b_gate, ln_g, ln_b, w_s, w_s_t, b_s, w_sc, w_out, wb, sheet, layer, *, tm, name,
                  carry=None):
    n = z.shape[0]
    steps = n // tm
    hb = tm // HALO
    base = SG_LAYER * layer
    r_bg, r_lng, r_lnb, r_sc = base + ROW_BGATE, base + ROW_LN_G, base + ROW_LN_B, base + ROW_SCONV

    def body(dx_ref, dxn_ref, z_ref, zn_ref, cv_ref, pa_ref, pb_ref, bg_ref, lng_ref, lnb_ref, ws_ref, wst_ref,
             bs_ref, wsc_ref, wo_ref, wb_ref, sg_in,
             dz_ref, dpa_ref, dpb_ref, dws_ref, dbs_ref, sg_ref, f_scr, dvn_scr):
        i = pl.program_id(0)
        last = i == steps - 1
        _sheet_begin(i, sg_in, sg_ref, r_bg, ROW_NORM2 - ROW_BGATE)

        @pl.when(i == 0)
        def _():
            dws_ref[...] = jnp.zeros_like(dws_ref)
            dbs_ref[...] = jnp.zeros_like(dbs_ref)

        dxe = jnp.concatenate([dx_ref[...], dxn_ref[...]], axis=0).astype(BF16)
        dmge = _dot_nt(dxe, wo_ref[...])
        dmg, dmgn = dmge[:tm].astype(BF16), dmge[tm:].astype(BF16)

        pa_v = pa_ref[...]
        pb_v = pb_ref[...]
        bg = bg_ref[...].astype(BF16)
        sa = _sigmoid(z_ref[:, OFF_GA:OFF_GA + D_MODEL] + bg[:, 0:D_MODEL])
        sb = _sigmoid(z_ref[:, OFF_GB:OFF_GB + D_MODEL] + bg[:, D_MODEL:2 * D_MODEL])
        dpa = dmg * sa
        dpb = dmg * sb
        dga = dmg * pa_v * sa * (1.0 - sa)
        dgb = dmg * pb_v * sb * (1.0 - sb)
        dpa_ref[...] = dpa
        dpb_ref[...] = dpb
        dz_ref[:, OFF_GA:OFF_GA + D_MODEL] = dga
        dz_ref[:, OFF_GB:OFF_GB + D_MODEL] = dgb
        sg_ref[r_bg:r_bg + 1, 0:D_MODEL] += _column_sums(dga)
        sg_ref[r_bg:r_bg + 1, D_MODEL:2 * D_MODEL] += _column_sums(dgb)

        dya = _dot_nt(dpa, wb_ref[0]).astype(BF16)
        u = z_ref[:, OFF_U:OFF_U + D_A]
        v = z_ref[:, OFF_V:OFF_V + D_A].astype(F32)
        ln_g = lng_ref[...]
        gu, tu, tv, xh, rstd, vn, f = _gmlp_forward(u, v, ln_g, lnb_ref[...], ws_ref, bs_ref, f_scr)
        dgu = dya * f
        df_bf = dya * gu
        dz_ref[:, OFF_U:OFF_U + D_A] = dgu * _gelu_grad(u, tu)
        mask = _spatial_mask(False)
        mask_t = _spatial_mask(True)
        wmt = [jnp.where(mask_t, wst_ref[h], 0.0).astype(BF16) for h in range(N_HEADS)]
        for b in range(tm // GMLP_BLOCK):
            rows = slice(b * GMLP_BLOCK, (b + 1) * GMLP_BLOCK)
            for h in range(N_HEADS):
                cols = slice(h * HEAD, (h + 1) * HEAD)
                dfb = df_bf[rows, cols]
                dvn_scr[rows, cols] = _dot(wmt[h], dfb)
                dws_ref[h] += jnp.where(mask, _dot_nt(dfb, vn[rows, cols]), 0.0)
                dbs_ref[h] += jnp.sum(dfb.astype(F32), axis=1, keepdims=True)
        dvn = dvn_scr[...]
        sg_ref[r_lng:r_lng + 1, 0:D_A] += jnp.sum(dvn * xh, axis=0, keepdims=True)
        sg_ref[r_lnb:r_lnb + 1, 0:D_A] += jnp.sum(dvn, axis=0, keepdims=True)
        dxh = dvn * ln_g
        dgv = rstd * (dxh - jnp.mean(dxh, axis=-1, keepdims=True) - xh * jnp.mean(dxh * xh, axis=-1, keepdims=True))
        dz_ref[:, OFF_V:OFF_V + D_A] = (dgv * _gelu_grad(v, tv)).astype(BF16)

        sbn = _sigmoid(zn_ref[:, OFF_GB:OFF_GB + D_MODEL] + bg[:, D_MODEL:2 * D_MODEL])
        dpbe = jnp.concatenate([dpb, dmgn * sbn], axis=0)
        dybe = _dot_nt(dpbe, wb_ref[1])
        dyb, dybn = dybe[:tm].astype(BF16), dybe[tm:].astype(BF16)
        bgv = z_ref[:, OFF_BG:OFF_BG + D_B]
        cg = z_ref[:, OFF_CG:OFF_CG + D_B]
        hbv = z_ref[:, OFF_HB:OFF_HB + D_B]
        q = cg * hbv
        dz_ref[:, OFF_BG:OFF_BG + D_B] = dyb * cv_ref[...]
        dconv = dyb * bgv
        dconvn = dybn * zn_ref[:, OFF_BG:OFF_BG + D_B]
        dconvn = jnp.where(last, jnp.zeros_like(dconvn), dconvn)
        up1 = _shift_up(dconv, 1, dconvn)
        up2 = _shift_up(dconv, 2, dconvn)
        sg_ref[r_sc:r_sc + 1, 0:D_B] += _column_sums(q * up2)
        sg_ref[r_sc + 1:r_sc + 2, 0:D_B] += _column_sums(q * up1)
        sg_ref[r_sc + 2:r_sc + 3, 0:D_B] += _column_sums(q * dconv)
        w = wsc_ref[...].astype(BF16)
        dq = w[2:3] * dconv + w[1:2] * up1 + w[0:1] * up2
        dz_ref[:, OFF_CG:OFF_CG + D_B] = dq * hbv
        dz_ref[:, OFF_HB:OFF_HB + D_B] = dq * cg

    row = lambda w: pl.BlockSpec((tm, w), lambda i: (i, 0))
    full = lambda *s: pl.BlockSpec(s, lambda i: (0,) * len(s))
    nxt = lambda i: (jnp.minimum((i + 1) * hb, steps * hb - 1), 0)
    return _call(
        body, name=name, grid=(steps,), carry=carry,
        in_specs=[row(D_MODEL), pl.BlockSpec((HALO, D_MODEL), nxt),
                  row(D_IN), pl.BlockSpec((HALO, D_IN), nxt),
                  row(D_B), row(D_MODEL), row(D_MODEL),
                  full(1, 2 * D_MODEL), full(1, D_A), full(1, D_A),
                  full(N_HEADS, GMLP_BLOCK, GMLP_BLOCK), full(N_HEADS, GMLP_BLOCK, GMLP_BLOCK),
                  full(N_HEADS, GMLP_BLOCK, 1), full(3, D_B),
                  full(D_MODEL, D_MODEL), full(2, D_A, D_MODEL), _sheet_spec()],
        out_specs=[row(D_IN), row(D_MODEL), row(D_MODEL), full(N_HEADS, GMLP_BLOCK, GMLP_BLOCK),
                   full(N_HEADS, GMLP_BLOCK, 1), _sheet_spec()],
        out_shape=[jax.ShapeDtypeStruct((n, D_IN), BF16), jax.ShapeDtypeStruct((n, D_MODEL), BF16),
                   jax.ShapeDtypeStruct((n, D_MODEL), BF16),
                   jax.ShapeDtypeStruct((N_HEADS, GMLP_BLOCK, GMLP_BLOCK), F32),
                   jax.ShapeDtypeStruct((N_HEADS, GMLP_BLOCK, 1), F32), jax.ShapeDtypeStruct((SG_ROWS, SG_W), F32)],
        scratch_shapes=[pltpu.VMEM((tm, D_A), BF16), pltpu.VMEM((tm, D_A), F32)],
        args=(dx1, dx1, z, z, conv, pa, pb, b_gate.reshape(1, -1), ln_g.reshape(1, -1), ln_b.reshape(1, -1), w_s, w_s_t,
              b_s.reshape(N_HEADS, GMLP_BLOCK, 1), w_sc, w_out, wb, sheet))


def _matmul_tn(a, b, *, t1, tn, name, carry=None, pieces=1):
    n, k1 = a.shape
    k2 = b.shape[1]
    steps = n // tn
    w = k2 // pieces

    def body(a_ref, b_ref, *rest):
        o_refs, acc = rest[:pieces], rest[pieces]
        s = pl.program_id(1)

        @pl.when(s == 0)
        def _():
            acc[...] = jnp.zeros_like(acc)

        acc[...] += lax.dot_general(a_ref[...].astype(BF16), b_ref[...].astype(BF16), TN, preferred_element_type=F32)

        @pl.when(s == steps - 1)
        def _():
            for c, o_ref in enumerate(o_refs):
                o_ref[...] = acc[:, c * w:(c + 1) * w].astype(BF16)

    outs, carried = _call(
        body, name=name, grid=(k1 // t1, steps), carry=carry,
        in_specs=[pl.BlockSpec((tn, t1), lambda i, s: (s, i)),
                  pl.BlockSpec((tn, k2), lambda i, s: (s, 0))],
        out_specs=[pl.BlockSpec((t1, w), lambda i, s: (i, 0))] * pieces,
        out_shape=[jax.ShapeDtypeStruct((k1, w), BF16)] * pieces,
        scratch_shapes=[pltpu.VMEM((t1, k2), F32)],
        args=(a, b))
    return (outs[0] if pieces == 1 else list(outs)), carried


def _adamw_math(w, g, m, v):
    m = ADAM_B1 * m + (1.0 - ADAM_B1) * g
    v = ADAM_B2 * v + (1.0 - ADAM_B2) * (g * g)
    m_hat = m / (1.0 - ADAM_B1 ** ADAM_STEP)
    v_hat = v / (1.0 - ADAM_B2 ** ADAM_STEP)
    delta = -ADAM_LR * (m_hat / (jnp.sqrt(v_hat) + ADAM_EPS) + ADAM_WD * w)
    return delta, m, v


def _sum_parts(recvs, *, tr, name):
    _, r, c = recvs[0].shape

    def body(*refs):
        recv_refs, g_ref = refs[:DEPTH], refs[DEPTH]
        layer = pl.program_id(0)
        for l in range(DEPTH):
            @pl.when(layer == l)
            def _(l=l):
                g = recv_refs[l][0].astype(F32)
                for s in range(1, N_DEV):
                    g = g + recv_refs[l][s].astype(F32)
                g_ref[0] = g

    outs, _ = _call(
        body, name=name, grid=(DEPTH, r // tr),
        in_specs=[pl.BlockSpec((N_DEV, tr, c), lambda l, i: (0, i, 0))] * DEPTH,
        out_specs=[pl.BlockSpec((1, tr, c), lambda l, i: (l, i, 0))],
        out_shape=[jax.ShapeDtypeStruct((DEPTH, r, c), F32)],
        args=tuple(recvs))
    return outs[0]


def _adamw(w, g, m, v, *, tr, name):
    r, c = w.shape

    def body(w_ref, g_ref, m_ref, v_ref, d_ref, nm_ref, nv_ref):
        delta, nm, nv = _adamw_math(w_ref[...], g_ref[...], m_ref[...], v_ref[...])
        d_ref[...] = delta
        nm_ref[...] = nm
        nv_ref[...] = nv

    spec = pl.BlockSpec((tr, c), lambda i: (i, 0))
    outs, _ = _call(body, name=name, grid=(r // tr,), in_specs=[spec] * 4, out_specs=[spec] * 3,
                    out_shape=[jax.ShapeDtypeStruct((r, c), F32)] * 3, args=(w, g, m, v))
    return outs


def _sum_adamw(recvs, w, m, v, *, tr, name):
    _, r, c = w.shape
    blocks = len(recvs[0])
    flat = [piece for layer in recvs for piece in layer]

    def body(*refs):
        recv_refs = refs[:len(flat)]
        w_ref, m_ref, v_ref, g_ref, d_ref, nm_ref, nv_ref = refs[len(flat):]
        layer = pl.program_id(0)
        for l in range(DEPTH):
            @pl.when(layer == l)
            def _(l=l):
                cols = []
                for piece in recv_refs[l * blocks:(l + 1) * blocks]:
                    part = piece[0].astype(F32)
                    for s in range(1, N_DEV):
                        part = part + piece[s].astype(F32)
                    cols.append(part)
                g = cols[0] if blocks == 1 else jnp.concatenate(cols, axis=-1)
                delta, nm, nv = _adamw_math(w_ref[0], g, m_ref[0], v_ref[0])
                g_ref[0] = g
                d_ref[0] = delta
                nm_ref[0] = nm
                nv_ref[0] = nv

    spec = pl.BlockSpec((1, tr, c), lambda l, i: (l, i, 0))
    outs, _ = _call(
        body, name=name, grid=(DEPTH, r // tr),
        in_specs=[pl.BlockSpec((N_DEV, tr, c // blocks), lambda l, i: (0, i, 0))] * len(flat) + [spec] * 3,
        out_specs=[spec] * 4, out_shape=[jax.ShapeDtypeStruct((DEPTH, r, c), F32)] * 4,
        args=tuple(flat) + (w, m, v))
    return outs


def _adamw_small(sheet, extra, params, *, name):
    sheet_rows = dict(norm1_g=ROW_NORM1, b_gate=ROW_BGATE, gmlp_ln_g=ROW_LN_G, gmlp_ln_b=ROW_LN_B, norm2_g=ROW_NORM2,
                      b_ffn_conv=ROW_BFCONV)
    names = list(params)
    extra_names = list(extra)

    def body(*refs):
        sg_ref, refs = refs[0], refs[1:]
        extra_refs, refs = dict(zip(extra_names, refs[:len(extra_names)])), refs[len(extra_names):]
        ins, outs = refs[:3 * len(names)], refs[3 * len(names):]
        for j, key in enumerate(names):
            w_ref, m_ref, v_ref = ins[3 * j:3 * j + 3]
            g_ref, d_ref, nm_ref, nv_ref = outs[4 * j:4 * j + 4]
            if key in extra_refs:
                g_ref[...] = extra_refs[key][...]
            elif key == "final_g":
                g_ref[...] = sg_ref[ROW_FINAL:ROW_FINAL + 1, 0:D_MODEL]
            else:
                width = w_ref.shape[-1]
                for l in range(DEPTH):
                    row = SG_LAYER * l + sheet_rows[key]
                    g_ref[l:l + 1, :] = sg_ref[row:row + 1, 0:width]
            delta, nm, nv = _adamw_math(w_ref[...], g_ref[...], m_ref[...], v_ref[...])
            d_ref[...] = delta
            nm_ref[...] = nm
            nv_ref[...] = nv

    args = [sheet] + [extra[k] for k in extra_names] + [t for k in names for t in params[k]]
    vmem = pl.BlockSpec(memory_space=pltpu.VMEM)
    outs = pl.pallas_call(
        body, name=name, in_specs=[vmem] * len(args), out_specs=[vmem] * (4 * len(names)),
        out_shape=[jax.ShapeDtypeStruct(params[k][0].shape, F32) for k in names for _ in range(4)],
    )(*args)
    return {k: tuple(outs[4 * j:4 * j + 4]) for j, k in enumerate(names)}


def _rows(gathered):
    return gathered.reshape(N_DEV * gathered.shape[1], gathered.shape[2])


def _parts(full):
    return full.reshape(N_DEV, full.shape[0] // N_DEV, full.shape[1])


def kernel(x, norm1_g, w_in, b_gate, gmlp_ln_g, gmlp_ln_b, w_spatial, b_spatial, w_shortconv, w_branch, w_out, norm2_g, w_ffn_up, w_ffn_conv, b_ffn_conv, w_ffn_down, final_g, loss_target, m_norm1_g, m_w_in, m_b_gate, m_gmlp_ln_g, m_gmlp_ln_b, m_w_spatial, m_b_spatial, m_w_shortconv, m_w_branch, m_w_out, m_norm2_g, m_w_ffn_up, m_w_ffn_conv, m_b_ffn_conv, m_w_ffn_down, m_final_g, v_norm1_g, v_w_in, v_b_gate, v_gmlp_ln_g, v_gmlp_ln_b, v_w_spatial, v_b_spatial, v_w_shortconv, v_w_branch, v_w_out, v_norm2_g, v_w_ffn_up, v_w_ffn_conv, v_b_ffn_conv, v_w_ffn_down, v_final_g):
    n = x.shape[1]
    tm_in, tm, tn = 1024, 512, 2048
    x0 = x.reshape(n, D_MODEL)
    target = loss_target.reshape(n, D_MODEL)
    my_idx = 4 * lax.axis_index("x") + 2 * lax.axis_index("y") + lax.axis_index("c")
    sc_w, fc_w = D_B // N_DEV, D_FF // N_DEV

    sh_in = [w_in[l].T.astype(BF16) for l in range(DEPTH)]
    sh_up = [w_ffn_up[l].T.astype(BF16) for l in range(DEPTH)]
    sh_br = [w_branch[l].astype(BF16) for l in range(DEPTH)]
    sh_out = [w_out[l].astype(BF16) for l in range(DEPTH)]
    sh_down = [w_ffn_down[l].astype(BF16) for l in range(DEPTH)]
    taps = jnp.concatenate([w_shortconv, w_ffn_conv], axis=-1)

    def branch_weights(g):
        return g.transpose(1, 2, 0, 3).reshape(2, D_A, D_MODEL)

    g_in0, g_taps = _gather_now([sh_in[0], taps], name="gather_first")
    w_sc = [g_taps[:, l, :, :sc_w].transpose(1, 0, 2).reshape(3, D_B) for l in range(DEPTH)]
    w_fc = [g_taps[:, l, :, sc_w:].transpose(1, 0, 2).reshape(3, D_FF) for l in range(DEPTH)]
    w_s_t = [w_spatial[l].transpose(0, 2, 1) for l in range(DEPTH)]
    weights = [dict(), dict()]
    weights[0]["in_t"] = _rows(g_in0)
    saved = []
    xc = x0
    for l in range(DEPTH):
        p = weights[l]
        carry = _Gather([sh_br[0], sh_out[0]] if l == 0 else [sh_up[1]])
        (h, z), got = _norm_matmul(xc, norm1_g[l], p["in_t"], tm=tm_in, name=f"fwd_in_{l}", carry=carry)
        if l == 0:
            p["wb"], p["out"] = branch_weights(got[0]), _rows(got[1])
        else:
            p["up_t"] = _rows(got[0])
        carry = _Gather([sh_up[0]]) if l == 0 else None
        (ya, yb, conv, pa, pb, mg, x1), got = _mix_forward(
            z, xc, b_gate[l], gmlp_ln_g[l], gmlp_ln_b[l], w_spatial[l], b_spatial[l], w_sc[l], p["wb"], p["out"],
            tm=tm, name=f"fwd_mix_{l}", carry=carry)
        if l == 0:
            p["up_t"] = _rows(got[0])
        (h2, up), got = _norm_matmul(x1, norm2_g[l], p["up_t"], tm=tm, name=f"fwd_up_{l}", carry=_Gather([sh_down[l]]))
        p["down"] = _rows(got[0])
        carry = _Gather([sh_br[1], sh_out[1], sh_in[1]]) if l == 0 else None
        head = (final_g, target) if l == DEPTH - 1 else None
        outs, got = _ffn_forward(up, x1, w_fc[l], b_ffn_conv[l], p["down"], tm=tm, name=f"fwd_ffn_{l}", carry=carry, head=head)
        if l == 0:
            weights[1]["wb"], weights[1]["out"], weights[1]["in_t"] = branch_weights(got[0]), _rows(got[1]), _rows(got[2])
        gc, a = outs[0], outs[1]
        saved.append(dict(x=xc, h=h, z=z, ya=ya, yb=yb, conv=conv, pa=pa, pb=pb, mg=mg, x1=x1, h2=h2, up=up, gc=gc, a=a))
        xc = outs[2]
    dx, sheet = outs[2], outs[3]

    recv = [dict(), dict()]
    small_dws, small_dbs = [None] * DEPTH, [None] * DEPTH
    pending_in = None
    for l in reversed(range(DEPTH)):
        p, s = weights[l], saved[l]
        carry = _Exchange([pending_in]) if pending_in is not None else None
        (dup, sheet), got = _ffn_backward(dx, s["up"], s["gc"], w_fc[l], p["down"], sheet, l, tm=tm, name=f"bwd_ffn_{l}",
                                          carry=carry)
        if got is not None:
            recv[l + 1]["in_t"] = got[0]
        dw_down, _ = _matmul_tn(s["a"], dx, t1=D_FF // 2, tn=tn, name=f"dw_down_{l}")
        dw_up_t, got = _matmul_tn(dup, s["h2"], t1=2 * D_FF // 4, tn=tn, name=f"dw_up_{l}", pieces=2,
                                  carry=_Exchange([_parts(dw_down)]))
        recv[l]["down"] = got[0]
        (dx1, sheet), got_left = _matmul_norm_backward(
            dup, p["up_t"], s["x1"], norm2_g[l], dx, sheet, SG_LAYER * l + ROW_NORM2, tm=tm, name=f"bwd_up_{l}",
            carry=_Exchange([_parts(dw_up_t[0])]))
        dw_out, _ = _matmul_tn(s["mg"], dx1, t1=D_MODEL, tn=tn, name=f"dw_out_{l}")
        (dz, dpa, dpb, small_dws[l], small_dbs[l], sheet), got_right = _mix_backward(
            dx1, s["z"], s["conv"], s["pa"], s["pb"], b_gate[l], gmlp_ln_g[l], gmlp_ln_b[l], w_spatial[l], w_s_t[l],
            b_spatial[l], w_sc[l], p["out"], p["wb"], sheet, l, tm=tm, name=f"bwd_mix_{l}",
            carry=_Exchange([_parts(dw_up_t[1]), _parts(dw_out)]))
        recv[l]["up_t"], recv[l]["out"] = [got_left[0], got_right[0]], got_right[1]
        dw_bra_t, _ = _matmul_tn(dpa, s["ya"], t1=D_MODEL, tn=tn, name=f"dw_branch_a_{l}")
        dw_brb_t, _ = _matmul_tn(dpb, s["yb"], t1=D_MODEL, tn=tn, name=f"dw_branch_b_{l}")
        carry = _Exchange([_parts(dw_bra_t), _parts(dw_brb_t)])
        if l == 0:
            dbs = jnp.stack([t.reshape(N_HEADS, GMLP_BLOCK) for t in small_dbs]).reshape(DEPTH * N_HEADS, GMLP_BLOCK)
            carry = _Both(carry, _Gather([sheet, small_dws[0], small_dws[1], dbs]))
        dw_in_t, got = _matmul_tn(dz, s["h"], t1=D_IN // 4, tn=tn, name=f"dw_in_{l}", carry=carry)
        recv[l]["bra_t"], recv[l]["brb_t"] = got[:2]
        if l == 0:
            gathered_small = got[2:]
            send_sems, recv_sems, parts_thru, land_thru, token = _exchange_start(_parts(dw_in_t), name="exchange_w_in_0_start")
            (dx0, dg1_first), _ = _matmul_norm_backward(dz, p["in_t"], s["x"], norm1_g[l] + token[0, 0], dx1, None, 0,
                                                        tm=tm, name=f"bwd_in_{l}")
            recv[0]["in_t"] = _exchange_wait(send_sems, recv_sems, parts_thru, land_thru, dg1_first,
                                             name="exchange_w_in_0_wait")
        else:
            (dx0, sheet), _ = _matmul_norm_backward(dz, p["in_t"], s["x"], norm1_g[l], dx1, sheet,
                                                    SG_LAYER * l + ROW_NORM1, tm=tm, name=f"bwd_in_{l}")
            pending_in = _parts(dw_in_t)
        dx = dx0
    grad_x = dx.reshape(x.shape)

    results = {}
    both = lambda key: [recv[l][key] for l in range(DEPTH)]
    blocks = lambda key: [r if isinstance(r, list) else [r] for r in both(key)]
    swap = lambda t: t.transpose(0, 2, 1)
    for key, slab, (w, m, v), tr in [("w_in", "in_t", (w_in, m_w_in, v_w_in), 192),
                                     ("w_ffn_up", "up_t", (w_ffn_up, m_w_ffn_up, v_w_ffn_up), 176)]:
        outs = _sum_adamw(blocks(slab), swap(w), swap(m), swap(v), tr=tr, name=f"adamw_{key}")
        results[key] = tuple(swap(o) for o in outs)
    g_bra = _sum_parts(both("bra_t"), tr=128, name="sum_w_branch_a").transpose(0, 2, 1)
    g_brb = _sum_parts(both("brb_t"), tr=128, name="sum_w_branch_b").transpose(0, 2, 1)
    g_br = jnp.stack([g_bra, g_brb], axis=1)
    flat = lambda t: t.reshape(-1, t.shape[-1])
    outs = _adamw(flat(w_branch), flat(g_br), flat(m_w_branch), flat(v_w_branch), tr=512, name="adamw_w_branch")
    results["w_branch"] = (g_br,) + tuple(o.reshape(w_branch.shape) for o in outs)
    results["w_out"] = tuple(_sum_adamw(blocks("out"), w_out, m_w_out, v_w_out, tr=128, name="adamw_w_out"))
    results["w_ffn_down"] = tuple(_sum_adamw(blocks("down"), w_ffn_down, m_w_ffn_down, v_w_ffn_down, tr=176,
                                             name="adamw_w_ffn_down"))

    sheet, dws0, dws1, dbs = _sum_gathered(gathered_small, name="sum_small_grads")
    (dg1_first,) = _all_reduce_small([dg1_first], name="all_reduce_last_gain")
    sheet = sheet.at[ROW_NORM1, :D_MODEL].set(dg1_first[0])
    loss = sheet[ROW_LOSS, 0]
    swap_taps = lambda t: t.transpose(1, 0, 2)
    taps = lambda row, width: jnp.stack([sheet[SG_LAYER * l + row:SG_LAYER * l + row + 3, :width] for l in range(DEPTH)], axis=1)
    extra = dict(w_spatial=jnp.stack([dws0, dws1]), b_spatial=dbs.reshape(DEPTH, N_HEADS, GMLP_BLOCK),
                 w_shortconv=lax.dynamic_slice_in_dim(taps(ROW_SCONV, D_B), my_idx * sc_w, sc_w, axis=2),
                 w_ffn_conv=lax.dynamic_slice_in_dim(taps(ROW_FCONV, D_FF), my_idx * fc_w, fc_w, axis=2))
    small_w = dict(norm1_g=(norm1_g, m_norm1_g, v_norm1_g), b_gate=(b_gate, m_b_gate, v_b_gate),
                   gmlp_ln_g=(gmlp_ln_g, m_gmlp_ln_g, v_gmlp_ln_g), gmlp_ln_b=(gmlp_ln_b, m_gmlp_ln_b, v_gmlp_ln_b),
                   w_spatial=(w_spatial, m_w_spatial, v_w_spatial), b_spatial=(b_spatial, m_b_spatial, v_b_spatial),
                   w_shortconv=tuple(swap_taps(t) for t in (w_shortconv, m_w_shortconv, v_w_shortconv)), norm2_g=(norm2_g, m_norm2_g, v_norm2_g),
                   w_ffn_conv=tuple(swap_taps(t) for t in (w_ffn_conv, m_w_ffn_conv, v_w_ffn_conv)), b_ffn_conv=(b_ffn_conv, m_b_ffn_conv, v_b_ffn_conv),
                   final_g=tuple(t.reshape(1, D_MODEL) for t in (final_g, m_final_g, v_final_g)))
    results.update(_adamw_small(sheet, extra, small_w, name="adamw_small"))
    results["final_g"] = tuple(t.reshape(D_MODEL) for t in results["final_g"])
    for key in ("w_shortconv", "w_ffn_conv"):
        results[key] = tuple(swap_taps(t) for t in results[key])

    names = ["norm1_g", "w_in", "b_gate", "gmlp_ln_g", "gmlp_ln_b", "w_spatial", "b_spatial", "w_shortconv", "w_branch",
             "w_out", "norm2_g", "w_ffn_up", "w_ffn_conv", "b_ffn_conv", "w_ffn_down", "final_g"]
    return (loss, grad_x, *[results[k][0] for k in names], *[results[k][1] for k in names],
            *[results[k][2] for k in names], *[results[k][3] for k in names])
```

```python
import math

import jax
import jax.numpy as jnp
from jax import lax
from jax.experimental import pallas as pl
from jax.experimental.pallas import tpu as pltpu

F32 = jnp.float32
BF16 = jnp.bfloat16

N_DEV = 8
DEPTH = 2
D_MODEL = 1024
D_A = 512
D_B = 512
D_FF = 2816
D_IN = 4608
N_HEADS = 4
HEAD = 128
GMLP_BLOCK = 128
CAUSAL_CHUNK = 64
OFF_U, OFF_V, OFF_BG, OFF_CG, OFF_HB, OFF_GA, OFF_GB = 0, 512, 1024, 1536, 2048, 2560, 3584
RMS_EPS = 1e-6
LN_EPS = 1e-5
ADAM_LR, ADAM_B1, ADAM_B2, ADAM_EPS, ADAM_WD, ADAM_STEP = 0.001, 0.9, 0.999, 1e-08, 0.01, 10

SUBLANES, LANES = 8, 128
MATMUL_CHUNK = 512
HALO = 16
FFN_CHUNK = 256
SG_ROWS, SG_W, SG_LAYER = 40, D_FF, 16
ROW_NORM1, ROW_BGATE, ROW_LN_G, ROW_LN_B, ROW_SCONV, ROW_NORM2, ROW_FCONV, ROW_BFCONV = 0, 1, 2, 3, 4, 7, 8, 11
ROW_FINAL, ROW_LOSS = 32, 33
V7X_VMEM_BYTES = 64 << 20
VMEM_LIMIT = V7X_VMEM_BYTES - (8 << 20)
MESH = pl.DeviceIdType.MESH
GELU_C0 = 0.7978845608028654
GELU_C1 = 0.044715
NT = (((1,), (1,)), ((), ()))
TN = (((0,), (0,)), ((), ()))


def _dot(a, b):
    return jnp.dot(a, b, preferred_element_type=F32)


def _dot_nt(a, b):
    return lax.dot_general(a, b, NT, preferred_element_type=F32)


def _sigmoid(x):
    return 1.0 / (1.0 + jnp.exp(-x))


def _gelu_tanh(x):
    return jnp.tanh(GELU_C0 * (x + GELU_C1 * x * x * x))


def _gelu_grad(x, t):
    return 0.5 * (1.0 + t) + 0.5 * x * (1.0 - t * t) * GELU_C0 * (1.0 + 3.0 * GELU_C1 * x * x)


def _sublane_tile(dtype):
    return SUBLANES * (4 // jnp.dtype(dtype).itemsize)


def _shift_down(a, k, prev):
    p = prev.shape[0]
    r = pltpu.roll(a, k, 0)
    sub = _sublane_tile(a.dtype)
    head = r[0:sub]
    rid = lax.broadcasted_iota(jnp.int32, head.shape, 0)
    for j in range(k):
        head = jnp.where(rid == j, prev[p - k + j:p - k + j + 1, :], head)
    return jnp.concatenate([head, r[sub:]], axis=0)


def _shift_up(a, k, nxt):
    t = a.shape[0]
    r = pltpu.roll(a, t - k, 0)
    sub = _sublane_tile(a.dtype)
    tail = r[t - sub:t]
    rid = lax.broadcasted_iota(jnp.int32, tail.shape, 0)
    for j in range(k):
        tail = jnp.where(rid == sub - k + j, nxt[j:j + 1, :], tail)
    return jnp.concatenate([r[0:t - sub], tail], axis=0)


def _column_sums(p):
    if p.dtype.itemsize < 4:
        t = p.shape[0]
        p = p[:t // 2] + p[t // 2:]
        p = p[:t // 4] + p[t // 4:]
    return jnp.sum(p.astype(F32), axis=0, keepdims=True)


def _sheet_begin(step, sheet_in, sheet_out, first_row, rows):
    @pl.when(step == 0)
    def _():
        sheet_out[...] = sheet_in[...]
        sheet_out[first_row:first_row + rows, :] = jnp.zeros((rows, SG_W), F32)


def _sheet_spec():
    return pl.BlockSpec((SG_ROWS, SG_W), lambda i: (0, 0))


def _spatial_mask(transposed):
    ri = lax.broadcasted_iota(jnp.int32, (GMLP_BLOCK, GMLP_BLOCK), 0) // CAUSAL_CHUNK
    ci = lax.broadcasted_iota(jnp.int32, (GMLP_BLOCK, GMLP_BLOCK), 1) // CAUSAL_CHUNK
    return (ri <= ci) if transposed else (ci <= ri)


def _gmlp_forward(u, v, ln_g, ln_b, ws_ref, bs_ref, f_scr):
    tm = u.shape[0]
    tu = _gelu_tanh(u)
    tv = _gelu_tanh(v)
    gu = 0.5 * u * (1.0 + tu)
    gv = 0.5 * v * (1.0 + tv)
    mu = jnp.mean(gv, axis=-1, keepdims=True)
    cen = gv - mu
    rstd = lax.rsqrt(jnp.mean(cen * cen, axis=-1, keepdims=True) + LN_EPS)
    xh = cen * rstd
    vn = (xh * ln_g + ln_b).astype(BF16)
    mask = _spatial_mask(False)
    wm = [jnp.where(mask, ws_ref[h], 0.0).astype(BF16) for h in range(N_HEADS)]
    for b in range(tm // GMLP_BLOCK):
        rows = slice(b * GMLP_BLOCK, (b + 1) * GMLP_BLOCK)
        for h in range(N_HEADS):
            cols = slice(h * HEAD, (h + 1) * HEAD)
            f_scr[rows, cols] = (_dot(wm[h], vn[rows, cols]) + bs_ref[h]).astype(f_scr.dtype)
    return gu, tu, tv, xh, rstd, vn, f_scr[...]


def _position():
    return lax.axis_index("x"), lax.axis_index("y"), lax.axis_index("c")


def _handshake(peers):
    barrier = pltpu.get_barrier_semaphore()
    for peer in peers:
        pl.semaphore_signal(barrier, inc=1, device_id=peer, device_id_type=MESH)
    pl.semaphore_wait(barrier, len(peers))


class _Gather:
    collective_id = 1

    def __init__(self, arrays):
        self.arrays = list(arrays)
        self.out_shape = [jax.ShapeDtypeStruct((N_DEV,) + a.shape, a.dtype) for a in self.arrays]
        self.base = 0

    def barrier(self):
        x, y, c = _position()
        _handshake([(x, y, 1 - c), (1 - x, y, c), (x, 1 - y, c), (1 - x, 1 - y, c)])

    def _plan(self, ins, outs, sems):
        send_sems, recv_sems, local_sems = sems
        x, y, c = _position()
        me, sibling = (x, y, c), (x, y, 1 - c)
        chips = [(1 - x, y), (x, 1 - y), (1 - x, 1 - y)]

        def slot(a, p):
            return outs[a].at[4 * p[0] + 2 * p[1] + p[2]]

        def copy(a, k, block, to, src=None):
            return pltpu.make_async_remote_copy(
                src_ref=slot(a, block) if src is None else src, dst_ref=slot(a, block),
                send_sem=send_sems.at[self.base + a, k], recv_sem=recv_sems.at[self.base + a, k],
                device_id=to, device_id_type=MESH)

        n = len(self.arrays)

        def mine():
            return [pltpu.make_async_copy(ins[a], slot(a, me), local_sems.at[self.base + a]) for a in range(n)]

        def first():
            out = []
            for a in range(n):
                out.append(copy(a, 0, me, sibling, src=ins[a]))
                out += [copy(a, 1 + j, me, (*chip, c), src=ins[a]) for j, chip in enumerate(chips)]
            return out

        def arrivals():
            return [copy(a, 1 + j, (*chip, c), me) for j, chip in enumerate(chips) for a in range(n)]

        def relays():
            return [copy(a, 4 + j, (*chip, c), sibling) for j, chip in enumerate(chips) for a in range(n)]

        def from_sibling():
            out = [copy(a, 0, sibling, me) for a in range(n)]
            return out + [copy(a, 4 + j, (*chip, 1 - c), me) for j, chip in enumerate(chips) for a in range(n)]

        return mine, first, arrivals, relays, from_sibling

    def start(self, ins, outs, sems):
        mine, first, _, _, _ = self._plan(ins, outs, sems)
        for cp in mine() + first():
            cp.start()

    def relay(self, ins, outs, sems):
        _, _, arrivals, relays, _ = self._plan(ins, outs, sems)
        for arrived, onward in zip(arrivals(), relays()):
            arrived.wait_recv()
            onward.start()

    def finish(self, ins, outs, sems):
        mine, first, _, relays, from_sibling = self._plan(ins, outs, sems)
        for cp in from_sibling():
            cp.wait_recv()
        for cp in first() + relays():
            cp.wait_send()
        for cp in mine():
            cp.wait()


class _Exchange:
    collective_id = 0

    def __init__(self, arrays):
        self.arrays = list(arrays)
        self.out_shape = [jax.ShapeDtypeStruct(a.shape, a.dtype) for a in self.arrays]
        self.base = 0

    def barrier(self):
        x, y, c = _position()
        _handshake([(x ^ dx, y ^ dy, c ^ dc) for dx in (0, 1) for dy in (0, 1) for dc in (0, 1) if dx or dy or dc])

    def _plan(self, ins, outs, sems):
        send_sems, recv_sems, local_sems = sems
        x, y, c = _position()
        my_idx = 4 * x + 2 * y + c
        n = len(self.arrays)
        offsets = [(dx, dy, dc) for dx in (0, 1) for dy in (0, 1) for dc in (0, 1) if (dx, dy, dc) != (0, 0, 0)]

        def mine():
            return [pltpu.make_async_copy(ins[a].at[my_idx], outs[a].at[my_idx], local_sems.at[self.base + a])
                    for a in range(n)]

        def remote(arriving):
            out = []
            for k, (dx, dy, dc) in enumerate(offsets):
                px, py, pc = x ^ dx, y ^ dy, c ^ dc
                p_idx = 4 * px + 2 * py + pc
                for a in range(n):
                    out.append(pltpu.make_async_remote_copy(
                        src_ref=ins[a].at[p_idx], dst_ref=outs[a].at[p_idx if arriving else my_idx],
                        send_sem=send_sems.at[self.base + a, k], recv_sem=recv_sems.at[self.base + a, k],
                        device_id=(px, py, pc), device_id_type=MESH))
            return out

        return mine, remote

    def start(self, ins, outs, sems):
        mine, remote = self._plan(ins, outs, sems)
        for cp in mine() + remote(False):
            cp.start()

    def relay(self, ins, outs, sems):
        pass

    def finish(self, ins, outs, sems):
        mine, remote = self._plan(ins, outs, sems)
        for cp in remote(True):
            cp.wait_recv()
        for cp in remote(False):
            cp.wait_send()
        for cp in mine():
            cp.wait()


def _exchange_start(parts, thru, *, name):
    hbm = pl.BlockSpec(memory_space=pltpu.HBM)
    sem = pl.BlockSpec(memory_space=pltpu.SEMAPHORE)
    offsets = [(dx, dy, dc) for dx in (0, 1) for dy in (0, 1) for dc in (0, 1) if dx or dy or dc]

    def body(parts_ref, thru_ref, send_sems, recv_sems, parts_thru, land_ref, thru_out):
        x, y, c = _position()
        my_idx = 4 * x + 2 * y + c
        pltpu.make_async_copy(parts_ref.at[my_idx], land_ref.at[my_idx], send_sems.at[7]).start()
        _handshake([(x ^ dx, y ^ dy, c ^ dc) for dx, dy, dc in offsets])
        for k, (dx, dy, dc) in enumerate(offsets):
            px, py, pc = x ^ dx, y ^ dy, c ^ dc
            pltpu.make_async_remote_copy(
                src_ref=parts_ref.at[4 * px + 2 * py + pc], dst_ref=land_ref.at[my_idx],
                send_sem=send_sems.at[k], recv_sem=recv_sems.at[k],
                device_id=(px, py, pc), device_id_type=MESH).start()
        thru_out[...] = thru_ref[...]

    return pl.pallas_call(
        body, name=name,
        out_shape=(pltpu.SemaphoreType.DMA((8,)), pltpu.SemaphoreType.DMA((7,)), pltpu.HBM(parts.shape, parts.dtype),
                   pltpu.HBM(parts.shape, parts.dtype), jax.ShapeDtypeStruct(thru.shape, thru.dtype)),
        in_specs=(hbm, pl.BlockSpec(memory_space=pltpu.VMEM)),
        out_specs=(sem, sem, hbm, hbm, pl.BlockSpec(memory_space=pltpu.VMEM)),
        input_output_aliases={0: 2},
        compiler_params=pltpu.CompilerParams(has_side_effects=pltpu.SideEffectType.DATAFLOW_SIDE_EFFECTING,
                                             collective_id=2),
    )(pltpu.with_memory_space_constraint(parts, pltpu.HBM), thru)


def _exchange_wait(send_sems, recv_sems, parts_thru, land_thru, after, *, name):
    hbm = pl.BlockSpec(memory_space=pltpu.HBM)
    sem = pl.BlockSpec(memory_space=pltpu.SEMAPHORE)
    offsets = [(dx, dy, dc) for dx in (0, 1) for dy in (0, 1) for dc in (0, 1) if dx or dy or dc]

    def body(parts_ref, land_ref, send_sems, recv_sems, after_ref, parts_dead, got_ref):
        x, y, c = _position()
        my_idx = 4 * x + 2 * y + c
        pltpu.make_async_copy(parts_ref.at[my_idx], land_ref.at[my_idx], send_sems.at[7]).wait()
        for k, (dx, dy, dc) in enumerate(offsets):
            px, py, pc = x ^ dx, y ^ dy, c ^ dc
            p_idx = 4 * px + 2 * py + pc
            copy = pltpu.make_async_remote_copy(
                src_ref=parts_ref.at[p_idx], dst_ref=land_ref.at[p_idx], send_sem=send_sems.at[k],
                recv_sem=recv_sems.at[k], device_id=(px, py, pc), device_id_type=MESH)
            copy.wait_send()
            copy.wait_recv()

    return pl.pallas_call(
        body, name=name,
        out_shape=(pltpu.HBM(parts_thru.shape, parts_thru.dtype), pltpu.HBM(land_thru.shape, land_thru.dtype)),
        in_specs=(hbm, hbm, sem, sem, pl.BlockSpec(memory_space=pl.ANY)), out_specs=(hbm, hbm),
        input_output_aliases={0: 0, 1: 1},
        compiler_params=pltpu.CompilerParams(has_side_effects=pltpu.SideEffectType.DATAFLOW_SIDE_EFFECTING),
    )(parts_thru, land_thru, send_sems, recv_sems, after)[1]


class _Both:
    def __init__(self, *carries):
        self.carries = carries
        self.arrays = [a for c in carries for a in c.arrays]
        self.out_shape = [s for c in carries for s in c.out_shape]
        first = 0
        for c in carries:
            c.base = first
            first += len(c.arrays)
        self.collective_id = min(c.collective_id for c in carries)

    def barrier(self):
        min(self.carries, key=lambda c: c.collective_id).barrier()

    def _each(self, method, ins, outs, sems):
        for c in self.carries:
            rows = slice(c.base, c.base + len(c.arrays))
            getattr(c, method)(ins[rows], outs[rows], sems)

    def start(self, ins, outs, sems):
        self._each("start", ins, outs, sems)

    def relay(self, ins, outs, sems):
        self._each("relay", ins, outs, sems)

    def finish(self, ins, outs, sems):
        self._each("finish", ins, outs, sems)


def _call(body, *, name, grid, in_specs, out_specs, out_shape, args, scratch_shapes=(), carry=None):
    n_in, n_out, n_scr = len(in_specs), len(out_specs), len(scratch_shapes)
    params = pltpu.CompilerParams(dimension_semantics=("arbitrary",) * len(grid), vmem_limit_bytes=VMEM_LIMIT)
    if carry is None:
        outs = pl.pallas_call(body, name=name, grid=grid, in_specs=in_specs, out_specs=out_specs, out_shape=out_shape,
                              scratch_shapes=list(scratch_shapes), compiler_params=params)(*args)
        return outs, None
    m = len(carry.arrays)
    total = math.prod(grid)

    def wrapped(*refs):
        ins, refs = refs[:n_in], refs[n_in:]
        c_ins, refs = refs[:m], refs[m:]
        outs, refs = refs[:n_out], refs[n_out:]
        c_outs, refs = refs[:m], refs[m:]
        scr, sems = refs[:n_scr], refs[n_scr:]
        flat = pl.program_id(0)
        for d in range(1, len(grid)):
            flat = flat * grid[d] + pl.program_id(d)

        @pl.when(flat == 0)
        def _():
            carry.barrier()
            carry.start(c_ins, c_outs, sems)

        body(*ins, *outs, *scr)

        @pl.when(flat == total - 2)
        def _():
            carry.relay(c_ins, c_outs, sems)

        @pl.when(flat == total - 1)
        def _():
            carry.finish(c_ins, c_outs, sems)

    any_spec = pl.BlockSpec(memory_space=pl.ANY)
    sem_shapes = [pltpu.SemaphoreType.DMA((m, 7)), pltpu.SemaphoreType.DMA((m, 7)), pltpu.SemaphoreType.DMA((m,))]
    params = pltpu.CompilerParams(dimension_semantics=("arbitrary",) * len(grid), vmem_limit_bytes=VMEM_LIMIT,
                                  collective_id=carry.collective_id)
    outs = pl.pallas_call(
        wrapped, name=name, grid=grid,
        in_specs=list(in_specs) + [any_spec] * m, out_specs=list(out_specs) + [any_spec] * m,
        out_shape=list(out_shape) + carry.out_shape,
        scratch_shapes=list(scratch_shapes) + sem_shapes, compiler_params=params)(*args, *carry.arrays)
    return outs[:n_out], outs[n_out:]


def _gather_now(arrays, *, name):
    carry = _Gather(arrays)
    m = len(arrays)

    def body(*refs):
        ins, outs, sems = refs[:m], refs[m:2 * m], refs[2 * m:]
        carry.barrier()
        carry.start(ins, outs, sems)
        carry.relay(ins, outs, sems)
        carry.finish(ins, outs, sems)

    any_spec = pl.BlockSpec(memory_space=pl.ANY)
    return pl.pallas_call(
        body, name=name, in_specs=[any_spec] * m, out_specs=[any_spec] * m, out_shape=carry.out_shape,
        scratch_shapes=[pltpu.SemaphoreType.DMA((m, 7)), pltpu.SemaphoreType.DMA((m, 7)),
                        pltpu.SemaphoreType.DMA((m,))],
        compiler_params=pltpu.CompilerParams(collective_id=carry.collective_id),
    )(*arrays)


def _all_reduce_small(arrs, *, name):
    n = len(arrs)

    def body(*refs):
        ins, outs, bufs = refs[:n], refs[n:2 * n], refs[2 * n:3 * n]
        send_sems, recv_sems = refs[3 * n:]
        x, y, c = _position()
        me, sibling = (x, y, c), (x, y, 1 - c)
        chips = [(1 - x, y), (x, 1 - y), (1 - x, 1 - y)]
        _handshake([sibling] + [(*chip, c) for chip in chips])

        def copy(a, k, block, to, src=None):
            slot = bufs[a].at[4 * block[0] + 2 * block[1] + block[2]]
            return pltpu.make_async_remote_copy(
                src_ref=slot if src is None else src, dst_ref=slot,
                send_sem=send_sems.at[a, k], recv_sem=recv_sems.at[a, k], device_id=to, device_id_type=MESH)

        first = []
        for a in range(n):
            first.append(copy(a, 0, me, sibling, src=ins[a]))
            first += [copy(a, 1 + j, me, (*chip, c), src=ins[a]) for j, chip in enumerate(chips)]
        for cp in first:
            cp.start()
        passed = []
        for j, chip in enumerate(chips):
            for a in range(n):
                copy(a, 1 + j, (*chip, c), me).wait_recv()
                cp = copy(a, 4 + j, (*chip, c), sibling)
                cp.start()
                passed.append(cp)
        for a in range(n):
            copy(a, 0, sibling, me).wait_recv()
            for j, chip in enumerate(chips):
                copy(a, 4 + j, (*chip, 1 - c), me).wait_recv()
        for cp in first + passed:
            cp.wait_send()
        my_idx = 4 * x + 2 * y + c
        for a in range(n):
            acc = jnp.zeros(ins[a].shape, F32)
            for s in range(N_DEV):
                acc = acc + jnp.where(my_idx == s, ins[a][...], bufs[a][s])
            outs[a][...] = acc

    vmem = pl.BlockSpec(memory_space=pltpu.VMEM)
    return pl.pallas_call(
        body, name=name, in_specs=[vmem] * n, out_specs=[vmem] * n,
        out_shape=[jax.ShapeDtypeStruct(a.shape, F32) for a in arrs],
        scratch_shapes=[pltpu.VMEM((N_DEV,) + a.shape, F32) for a in arrs]
        + [pltpu.SemaphoreType.DMA((n, 7)), pltpu.SemaphoreType.DMA((n, 7))],
        compiler_params=pltpu.CompilerParams(vmem_limit_bytes=VMEM_LIMIT, collective_id=_Gather.collective_id),
    )(*arrs)


def _sum_gathered(arrs, *, name):
    n = len(arrs)

    def body(*refs):
        for in_ref, out_ref in zip(refs[:n], refs[n:]):
            acc = in_ref[0]
            for s in range(1, N_DEV):
                acc = acc + in_ref[s]
            out_ref[...] = acc

    vmem = pl.BlockSpec(memory_space=pltpu.VMEM)
    return pl.pallas_call(
        body, name=name, in_specs=[vmem] * n, out_specs=[vmem] * n,
        out_shape=[jax.ShapeDtypeStruct(a.shape[1:], F32) for a in arrs],
        compiler_params=pltpu.CompilerParams(vmem_limit_bytes=VMEM_LIMIT),
    )(*arrs)


def _norm_matmul(x, g, w_t, *, tm, name, carry=None):
    n, d = x.shape
    c = w_t.shape[0]
    ch = MATMUL_CHUNK

    def body(x_ref, g_ref, wt_ref, h_ref, z_ref):
        xv = x_ref[...]
        r = lax.rsqrt(jnp.mean(xv * xv, axis=-1, keepdims=True) + RMS_EPS)
        h = (xv * r * g_ref[...]).astype(BF16)
        h_ref[...] = h
        for c0 in range(0, c, ch):
            z_ref[:, c0:c0 + ch] = _dot_nt(h, wt_ref[c0:c0 + ch, :]).astype(BF16)

    return _call(
        body, name=name, grid=(n // tm,), carry=carry,
        in_specs=[pl.BlockSpec((tm, d), lambda i: (i, 0)),
                  pl.BlockSpec((1, d), lambda i: (0, 0)),
                  pl.BlockSpec((c, d), lambda i: (0, 0))],
        out_specs=[pl.BlockSpec((tm, d), lambda i: (i, 0)),
                   pl.BlockSpec((tm, c), lambda i: (i, 0))],
        out_shape=[jax.ShapeDtypeStruct((n, d), BF16), jax.ShapeDtypeStruct((n, c), BF16)],
        args=(x, g.reshape(1, d), w_t))


def _mix_forward(z, x, b_gate, ln_g, ln_b, w_s, b_s, w_sc, wb, w_out, *, tm, name, carry=None):
    n = z.shape[0]
    hb = tm // HALO

    def body(z_ref, zp_ref, x_ref, bg_ref, lng_ref, lnb_ref, ws_ref, bs_ref, wsc_ref, wb_ref, wo_ref,
             ya_ref, yb_ref, cv_ref, pa_ref, pb_ref, mg_ref, x1_ref, f_scr):
        i = pl.program_id(0)
        u = z_ref[:, OFF_U:OFF_U + D_A]
        v = z_ref[:, OFF_V:OFF_V + D_A].astype(F32)
        gu, _, _, _, _, _, f = _gmlp_forward(u, v, lng_ref[...], lnb_ref[...], ws_ref, bs_ref, f_scr)
        ya = gu * f
        ya_ref[...] = ya

        q = z_ref[:, OFF_CG:OFF_CG + D_B] * z_ref[:, OFF_HB:OFF_HB + D_B]
        qp = zp_ref[:, OFF_CG:OFF_CG + D_B] * zp_ref[:, OFF_HB:OFF_HB + D_B]
        qp = jnp.where(i > 0, qp, jnp.zeros_like(qp))
        w = wsc_ref[...].astype(BF16)
        conv = w[0:1] * _shift_down(q, 2, qp) + w[1:2] * _shift_down(q, 1, qp) + w[2:3] * q
        cv_ref[...] = conv
        yb = z_ref[:, OFF_BG:OFF_BG + D_B] * conv
        yb_ref[...] = yb

        pa = _dot(ya, wb_ref[0]).astype(BF16)
        pb = _dot(yb, wb_ref[1]).astype(BF16)
        pa_ref[...] = pa
        pb_ref[...] = pb
        bg = bg_ref[...].astype(BF16)
        sa = _sigmoid(z_ref[:, OFF_GA:OFF_GA + D_MODEL] + bg[:, 0:D_MODEL])
        sb = _sigmoid(z_ref[:, OFF_GB:OFF_GB + D_MODEL] + bg[:, D_MODEL:2 * D_MODEL])
        mg = sa * pa + sb * pb
        mg_ref[...] = mg
        x1_ref[...] = x_ref[...] + _dot(mg, wo_ref[...])

    row = lambda w: pl.BlockSpec((tm, w), lambda i: (i, 0))
    full = lambda *s: pl.BlockSpec(s, lambda i: (0,) * len(s))
    bf = lambda w: jax.ShapeDtypeStruct((n, w), BF16)
    return _call(
        body, name=name, grid=(n // tm,), carry=carry,
        in_specs=[row(D_IN),
                  pl.BlockSpec((HALO, D_IN), lambda i: (jnp.maximum(i * hb - 1, 0), 0)),
                  row(D_MODEL), full(1, 2 * D_MODEL), full(1, D_A), full(1, D_A),
                  full(N_HEADS, GMLP_BLOCK, GMLP_BLOCK), full(N_HEADS, GMLP_BLOCK, 1), full(3, D_B),
                  full(2, D_A, D_MODEL), full(D_MODEL, D_MODEL)],
        out_specs=[row(D_A), row(D_B), row(D_B), row(D_MODEL), row(D_MODEL), row(D_MODEL), row(D_MODEL)],
        out_shape=[bf(D_A), bf(D_B), bf(D_B), bf(D_MODEL), bf(D_MODEL), bf(D_MODEL),
                   jax.ShapeDtypeStruct((n, D_MODEL), F32)],
        scratch_shapes=[pltpu.VMEM((tm, D_A), BF16)],
        args=(z, z, x, b_gate.reshape(1, -1), ln_g.reshape(1, -1), ln_b.reshape(1, -1), w_s,
              b_s.reshape(N_HEADS, GMLP_BLOCK, 1), w_sc, wb, w_out))


def _loss_tile(xv, gv, tv):
    d = xv.shape[-1]
    r = lax.rsqrt(jnp.mean(xv * xv, axis=-1, keepdims=True) + RMS_EPS)
    xh = xv * r
    e = xh * gv - tv
    per_row = jnp.sum(e * e, axis=-1, keepdims=True) * (0.5 / d)
    dy = e * (1.0 / d)
    dxh = dy * gv
    dx = r * (dxh - xh * jnp.mean(dxh * xh, axis=-1, keepdims=True))
    return dx, jnp.sum(per_row, axis=0, keepdims=True), jnp.sum(dy * xh, axis=0, keepdims=True)


def _ffn_forward(up, x1, w_fc, b_fc, w_down, *, tm, name, carry=None, head=None):
    n = up.shape[0]
    hb = tm // HALO
    n_in = 6 if head is None else 8

    def body(*refs):
        up_ref, upp_ref, x1_ref, wfc_ref, bfc_ref, wd_ref = refs[:6]
        gc_ref, a_ref, out_ref = refs[n_in:n_in + 3]
        acc = refs[-1]
        i = pl.program_id(0)
        acc[...] = x1_ref[...]
        for c0 in range(0, D_FF, FFN_CHUNK):
            cols = slice(c0, c0 + FFN_CHUNK)
            gate = up_ref[:, cols]
            val = up_ref[:, D_FF + c0:D_FF + c0 + FFN_CHUNK]
            gp = upp_ref[:, cols]
            gp = jnp.where(i > 0, gp, jnp.zeros_like(gp))
            w = wfc_ref[:, cols].astype(BF16)
            gc = (w[0:1] * _shift_down(gate, 2, gp) + w[1:2] * _shift_down(gate, 1, gp) + w[2:3] * gate
                  + bfc_ref[:, cols].astype(BF16))
            gc_ref[:, cols] = gc
            a = gc * _sigmoid(gc) * val
            a_ref[:, cols] = a
            acc[...] += _dot(a, wd_ref[cols, :])
        if head is None:
            out_ref[...] = acc[...]
        else:
            g_ref, t_ref = refs[6:8]
            sg_ref = refs[n_in + 3]

            @pl.when(i == 0)
            def _():
                sg_ref[...] = jnp.zeros_like(sg_ref)

            dx, loss, dg = _loss_tile(acc[...], g_ref[...], t_ref[...])
            out_ref[...] = dx
            sg_ref[ROW_LOSS:ROW_LOSS + 1, 0:LANES] += jnp.broadcast_to(loss, (1, LANES))
            sg_ref[ROW_FINAL:ROW_FINAL + 1, 0:D_MODEL] += dg

    row = lambda w: pl.BlockSpec((tm, w), lambda i: (i, 0))
    full = lambda r, c: pl.BlockSpec((r, c), lambda i: (0, 0))
    in_specs = [row(2 * D_FF), pl.BlockSpec((HALO, D_FF), lambda i: (jnp.maximum(i * hb - 1, 0), 0)), row(D_MODEL),
                full(3, D_FF), full(1, D_FF), full(D_FF, D_MODEL)]
    out_specs = [row(D_FF), row(D_FF), row(D_MODEL)]
    out_shape = [jax.ShapeDtypeStruct((n, D_FF), BF16), jax.ShapeDtypeStruct((n, D_FF), BF16),
                 jax.ShapeDtypeStruct((n, D_MODEL), F32)]
    args = (up, up, x1, w_fc, b_fc.reshape(1, -1), w_down)
    if head is not None:
        in_specs += [full(1, D_MODEL), row(D_MODEL)]
        out_specs += [full(SG_ROWS, SG_W)]
        out_shape += [jax.ShapeDtypeStruct((SG_ROWS, SG_W), F32)]
        args += (head[0].reshape(1, -1), head[1])
    return _call(body, name=name, grid=(n // tm,), carry=carry, in_specs=in_specs, out_specs=out_specs,
                 out_shape=out_shape, scratch_shapes=[pltpu.VMEM((tm, D_MODEL), F32)], args=args)


def _ffn_backward(dx2, up, gc, w_fc, w_down, sheet, layer, *, tm, name, carry=None):
    n = up.shape[0]
    steps = n // tm
    hb = tm // HALO
    row = SG_LAYER * layer + ROW_FCONV

    def body(dx_ref, dxn_ref, up_ref, upn_ref, gc_ref, gcn_ref, wfc_ref, wd_ref, sg_in, dup_ref, sg_ref):
        i = pl.program_id(0)
        last = i == steps - 1
        _sheet_begin(i, sg_in, sg_ref, row, 4)

        dxe = jnp.concatenate([dx_ref[...], dxn_ref[...]], axis=0).astype(BF16)
        for c0 in range(0, D_FF, FFN_CHUNK):
            cols = slice(c0, c0 + FFN_CHUNK)
            vcols = slice(D_FF + c0, D_FF + c0 + FFN_CHUNK)
            dae = _dot_nt(dxe, wd_ref[cols, :])
            da, dan = dae[:tm], dae[tm:]
            gate = up_ref[:, cols]
            val = up_ref[:, vcols]
            gcv = gc_ref[:, cols]
            s = _sigmoid(gcv)
            dab = da.astype(BF16)
            dup_ref[:, vcols] = dab * (gcv * s)
            dgc = dab * val * (s * (1.0 + gcv * (1.0 - s)))
            gcn = gcn_ref[:, cols]
            sn = _sigmoid(gcn)
            dgcn = dan.astype(BF16) * upn_ref[:, vcols] * (sn * (1.0 + gcn * (1.0 - sn)))
            dgcn = jnp.where(last, jnp.zeros_like(dgcn), dgcn)
            up1 = _shift_up(dgc, 1, dgcn)
            up2 = _shift_up(dgc, 2, dgcn)
            w = wfc_ref[:, cols].astype(BF16)
            dup_ref[:, cols] = w[2:3] * dgc + w[1:2] * up1 + w[0:1] * up2
            sg_ref[row:row + 1, cols] += _column_sums(gate * up2)
            sg_ref[row + 1:row + 2, cols] += _column_sums(gate * up1)
            sg_ref[row + 2:row + 3, cols] += _column_sums(gate * dgc)
            sg_ref[row + 3:row + 4, cols] += _column_sums(dgc)

    nxt = lambda i: (jnp.minimum((i + 1) * hb, steps * hb - 1), 0)
    return _call(
        body, name=name, grid=(steps,), carry=carry,
        in_specs=[pl.BlockSpec((tm, D_MODEL), lambda i: (i, 0)),
                  pl.BlockSpec((HALO, D_MODEL), nxt),
                  pl.BlockSpec((tm, 2 * D_FF), lambda i: (i, 0)),
                  pl.BlockSpec((HALO, 2 * D_FF), nxt),
                  pl.BlockSpec((tm, D_FF), lambda i: (i, 0)),
                  pl.BlockSpec((HALO, D_FF), nxt),
                  pl.BlockSpec((3, D_FF), lambda i: (0, 0)),
                  pl.BlockSpec((D_FF, D_MODEL), lambda i: (0, 0)), _sheet_spec()],
        out_specs=[pl.BlockSpec((tm, 2 * D_FF), lambda i: (i, 0)), _sheet_spec()],
        out_shape=[jax.ShapeDtypeStruct((n, 2 * D_FF), BF16), jax.ShapeDtypeStruct((SG_ROWS, SG_W), F32)],
        args=(dx2, dx2, up, up, gc, gc, w_fc, w_down, sheet))


def _matmul_norm_backward(dz, w_t, x, g, dres, sheet, row, *, tm, name, carry=None):
    n, c = dz.shape
    d = x.shape[1]
    ch = MATMUL_CHUNK

    def body(dz_ref, wt_ref, x_ref, g_ref, dres_ref, *rest):
        i = pl.program_id(0)
        if sheet is None:
            dx_ref, sg_ref = rest

            @pl.when(i == 0)
            def _():
                sg_ref[...] = jnp.zeros_like(sg_ref)
        else:
            sg_in, dx_ref, sg_ref = rest
            _sheet_begin(i, sg_in, sg_ref, row, 1)

        dh = _dot(dz_ref[:, 0:ch], wt_ref[0:ch, :])
        for c0 in range(ch, c, ch):
            dh += _dot(dz_ref[:, c0:c0 + ch], wt_ref[c0:c0 + ch, :])
        xv = x_ref[...]
        r = lax.rsqrt(jnp.mean(xv * xv, axis=-1, keepdims=True) + RMS_EPS)
        xh = xv * r
        sg_ref[row:row + 1, 0:d] += jnp.sum(dh * xh, axis=0, keepdims=True)
        dxh = dh * g_ref[...]
        dx_ref[...] = dres_ref[...] + r * (dxh - xh * jnp.mean(dxh * xh, axis=-1, keepdims=True))

    in_specs = [pl.BlockSpec((tm, c), lambda i: (i, 0)),
                pl.BlockSpec((c, d), lambda i: (0, 0)),
                pl.BlockSpec((tm, d), lambda i: (i, 0)),
                pl.BlockSpec((1, d), lambda i: (0, 0)),
                pl.BlockSpec((tm, d), lambda i: (i, 0))]
    args = (dz, w_t, x, g.reshape(1, d), dres)
    if sheet is None:
        small_spec, small_shape = pl.BlockSpec((8, d), lambda i: (0, 0)), jax.ShapeDtypeStruct((8, d), F32)
    else:
        in_specs, args = in_specs + [_sheet_spec()], args + (sheet,)
        small_spec, small_shape = _sheet_spec(), jax.ShapeDtypeStruct((SG_ROWS, SG_W), F32)
    return _call(
        body, name=name, grid=(n // tm,), carry=carry, in_specs=in_specs,
        out_specs=[pl.BlockSpec((tm, d), lambda i: (i, 0)), small_spec],
        out_shape=[jax.ShapeDtypeStruct((n, d), F32), small_shape], args=args)


def _mix_backward(dx1, z, conv, pa, pb, b_gate, ln_g, ln_b, w_s, w_s_t, b_s, w_sc, w_out, wb, sheet, layer, *, tm, name,
                  carry=None):
    n = z.shape[0]
    steps = n // tm
    hb = tm // HALO
    base = SG_LAYER * layer
    r_bg, r_lng, r_lnb, r_sc = base + ROW_BGATE, base + ROW_LN_G, base + ROW_LN_B, base + ROW_SCONV

    def body(dx_ref, dxn_ref, z_ref, zn_ref, cv_ref, pa_ref, pb_ref, bg_ref, lng_ref, lnb_ref, ws_ref, wst_ref,
             bs_ref, wsc_ref, wo_ref, wb_ref, sg_in,
             dz_ref, dpa_ref, dpb_ref, dws_ref, dbs_ref, sg_ref, f_scr, dvn_scr):
        i = pl.program_id(0)
        last = i == steps - 1
        _sheet_begin(i, sg_in, sg_ref, r_bg, ROW_NORM2 - ROW_BGATE)

        @pl.when(i == 0)
        def _():
            dws_ref[...] = jnp.zeros_like(dws_ref)
            dbs_ref[...] = jnp.zeros_like(dbs_ref)

        dxe = jnp.concatenate([dx_ref[...], dxn_ref[...]], axis=0).astype(BF16)
        dmge = _dot_nt(dxe, wo_ref[...])
        dmg, dmgn = dmge[:tm].astype(BF16), dmge[tm:].astype(BF16)

        pa_v = pa_ref[...]
        pb_v = pb_ref[...]
        bg = bg_ref[...].astype(BF16)
        sa = _sigmoid(z_ref[:, OFF_GA:OFF_GA + D_MODEL] + bg[:, 0:D_MODEL])
        sb = _sigmoid(z_ref[:, OFF_GB:OFF_GB + D_MODEL] + bg[:, D_MODEL:2 * D_MODEL])
        dpa = dmg * sa
        dpb = dmg * sb
        dga = dmg * pa_v * sa * (1.0 - sa)
        dgb = dmg * pb_v * sb * (1.0 - sb)
        dpa_ref[...] = dpa
        dpb_ref[...] = dpb
        dz_ref[:, OFF_GA:OFF_GA + D_MODEL] = dga
        dz_ref[:, OFF_GB:OFF_GB + D_MODEL] = dgb
        sg_ref[r_bg:r_bg + 1, 0:D_MODEL] += _column_sums(dga)
        sg_ref[r_bg:r_bg + 1, D_MODEL:2 * D_MODEL] += _column_sums(dgb)

        dya = _dot_nt(dpa, wb_ref[0]).astype(BF16)
        u = z_ref[:, OFF_U:OFF_U + D_A]
        v = z_ref[:, OFF_V:OFF_V + D_A].astype(F32)
        ln_g = lng_ref[...]
        gu, tu, tv, xh, rstd, vn, f = _gmlp_forward(u, v, ln_g, lnb_ref[...], ws_ref, bs_ref, f_scr)
        dgu = dya * f
        df_bf = dya * gu
        dz_ref[:, OFF_U:OFF_U + D_A] = dgu * _gelu_grad(u, tu)
        mask = _spatial_mask(False)
        mask_t = _spatial_mask(True)
        wmt = [jnp.where(mask_t, wst_ref[h], 0.0).astype(BF16) for h in range(N_HEADS)]
        for b in range(tm // GMLP_BLOCK):
            rows = slice(b * GMLP_BLOCK, (b + 1) * GMLP_BLOCK)
            for h in range(N_HEADS):
                cols = slice(h * HEAD, (h + 1) * HEAD)
                dfb = df_bf[rows, cols]
                dvn_scr[rows, cols] = _dot(wmt[h], dfb)
                dws_ref[h] += jnp.where(mask, _dot_nt(dfb, vn[rows, cols]), 0.0)
                dbs_ref[h] += jnp.sum(dfb.astype(F32), axis=1, keepdims=True)
        dvn = dvn_scr[...]
        sg_ref[r_lng:r_lng + 1, 0:D_A] += jnp.sum(dvn * xh, axis=0, keepdims=True)
        sg_ref[r_lnb:r_lnb + 1, 0:D_A] += jnp.sum(dvn, axis=0, keepdims=True)
        dxh = dvn * ln_g
        dgv = rstd * (dxh - jnp.mean(dxh, axis=-1, keepdims=True) - xh * jnp.mean(dxh * xh, axis=-1, keepdims=True))
        dz_ref[:, OFF_V:OFF_V + D_A] = (dgv * _gelu_grad(v, tv)).astype(BF16)

        sbn = _sigmoid(zn_ref[:, OFF_GB:OFF_GB + D_MODEL] + bg[:, D_MODEL:2 * D_MODEL])
        dpbe = jnp.concatenate([dpb, dmgn * sbn], axis=0)
        dybe = _dot_nt(dpbe, wb_ref[1])
        dyb, dybn = dybe[:tm].astype(BF16), dybe[tm:].astype(BF16)
        bgv = z_ref[:, OFF_BG:OFF_BG + D_B]
        cg = z_ref[:, OFF_CG:OFF_CG + D_B]
        hbv = z_ref[:, OFF_HB:OFF_HB + D_B]
        q = cg * hbv
        dz_ref[:, OFF_BG:OFF_BG + D_B] = dyb * cv_ref[...]
        dconv = dyb * bgv
        dconvn = dybn * zn_ref[:, OFF_BG:OFF_BG + D_B]
        dconvn = jnp.where(last, jnp.zeros_like(dconvn), dconvn)
        up1 = _shift_up(dconv, 1, dconvn)
        up2 = _shift_up(dconv, 2, dconvn)
        sg_ref[r_sc:r_sc + 1, 0:D_B] += _column_sums(q * up2)
        sg_ref[r_sc + 1:r_sc + 2, 0:D_B] += _column_sums(q * up1)
        sg_ref[r_sc + 2:r_sc + 3, 0:D_B] += _column_sums(q * dconv)
        w = wsc_ref[...].astype(BF16)
        dq = w[2:3] * dconv + w[1:2] * up1 + w[0:1] * up2
        dz_ref[:, OFF_CG:OFF_CG + D_B] = dq * hbv
        dz_ref[:, OFF_HB:OFF_HB + D_B] = dq * cg

    row = lambda w: pl.BlockSpec((tm, w), lambda i: (i, 0))
    full = lambda *s: pl.BlockSpec(s, lambda i: (0,) * len(s))
    nxt = lambda i: (jnp.minimum((i + 1) * hb, steps * hb - 1), 0)
    return _call(
        body, name=name, grid=(steps,), carry=carry,
        in_specs=[row(D_MODEL), pl.BlockSpec((HALO, D_MODEL), nxt),
                  row(D_IN), pl.BlockSpec((HALO, D_IN), nxt),
                  row(D_B), row(D_MODEL), row(D_MODEL),
                  full(1, 2 * D_MODEL), full(1, D_A), full(1, D_A),
                  full(N_HEADS, GMLP_BLOCK, GMLP_BLOCK), full(N_HEADS, GMLP_BLOCK, GMLP_BLOCK),
                  full(N_HEADS, GMLP_BLOCK, 1), full(3, D_B),
                  full(D_MODEL, D_MODEL), full(2, D_A, D_MODEL), _sheet_spec()],
        out_specs=[row(D_IN), row(D_MODEL), row(D_MODEL), full(N_HEADS, GMLP_BLOCK, GMLP_BLOCK),
                   full(N_HEADS, GMLP_BLOCK, 1), _sheet_spec()],
        out_shape=[jax.ShapeDtypeStruct((n, D_IN), BF16), jax.ShapeDtypeStruct((n, D_MODEL), BF16),
                   jax.ShapeDtypeStruct((n, D_MODEL), BF16),
                   jax.ShapeDtypeStruct((N_HEADS, GMLP_BLOCK, GMLP_BLOCK), F32),
                   jax.ShapeDtypeStruct((N_HEADS, GMLP_BLOCK, 1), F32), jax.ShapeDtypeStruct((SG_ROWS, SG_W), F32)],
        scratch_shapes=[pltpu.VMEM((tm, D_A), BF16), pltpu.VMEM((tm, D_A), F32)],
        args=(dx1, dx1, z, z, conv, pa, pb, b_gate.reshape(1, -1), ln_g.reshape(1, -1), ln_b.reshape(1, -1), w_s, w_s_t,
              b_s.reshape(N_HEADS, GMLP_BLOCK, 1), w_sc, w_out, wb, sheet))


def _matmul_tn(a, b, *, t1, tn, name, carry=None, pieces=1):
    n, k1 = a.shape
    k2 = b.shape[1]
    steps = n // tn
    w = k2 // pieces

    def body(a_ref, b_ref, *rest):
        o_refs, acc = rest[:pieces], rest[pieces]
        s = pl.program_id(1)

        @pl.when(s == 0)
        def _():
            acc[...] = jnp.zeros_like(acc)

        acc[...] += lax.dot_general(a_ref[...].astype(BF16), b_ref[...].astype(BF16), TN, preferred_element_type=F32)

        @pl.when(s == steps - 1)
        def _():
            for c, o_ref in enumerate(o_refs):
                o_ref[...] = acc[:, c * w:(c + 1) * w].astype(BF16)

    outs, carried = _call(
        body, name=name, grid=(k1 // t1, steps), carry=carry,
        in_specs=[pl.BlockSpec((tn, t1), lambda i, s: (s, i)),
                  pl.BlockSpec((tn, k2), lambda i, s: (s, 0))],
        out_specs=[pl.BlockSpec((t1, w), lambda i, s: (i, 0))] * pieces,
        out_shape=[jax.ShapeDtypeStruct((k1, w), BF16)] * pieces,
        scratch_shapes=[pltpu.VMEM((t1, k2), F32)],
        args=(a, b))
    return (outs[0] if pieces == 1 else list(outs)), carried


def _adamw_math(w, g, m, v):
    m = ADAM_B1 * m + (1.0 - ADAM_B1) * g
    v = ADAM_B2 * v + (1.0 - ADAM_B2) * (g * g)
    m_hat = m / (1.0 - ADAM_B1 ** ADAM_STEP)
    v_hat = v / (1.0 - ADAM_B2 ** ADAM_STEP)
    delta = -ADAM_LR * (m_hat / (jnp.sqrt(v_hat) + ADAM_EPS) + ADAM_WD * w)
    return delta, m, v


def _sum_parts(recvs, *, tr, name):
    _, r, c = recvs[0].shape

    def body(*refs):
        recv_refs, g_ref = refs[:DEPTH], refs[DEPTH]
        layer = pl.program_id(0)
        for l in range(DEPTH):
            @pl.when(layer == l)
            def _(l=l):
                g = recv_refs[l][0].astype(F32)
                for s in range(1, N_DEV):
                    g = g + recv_refs[l][s].astype(F32)
                g_ref[0] = g

    outs, _ = _call(
        body, name=name, grid=(DEPTH, r // tr),
        in_specs=[pl.BlockSpec((N_DEV, tr, c), lambda l, i: (0, i, 0))] * DEPTH,
        out_specs=[pl.BlockSpec((1, tr, c), lambda l, i: (l, i, 0))],
        out_shape=[jax.ShapeDtypeStruct((DEPTH, r, c), F32)],
        args=tuple(recvs))
    return outs[0]


def _adamw(w, g, m, v, *, tr, name):
    r, c = w.shape

    def body(w_ref, g_ref, m_ref, v_ref, d_ref, nm_ref, nv_ref):
        delta, nm, nv = _adamw_math(w_ref[...], g_ref[...], m_ref[...], v_ref[...])
        d_ref[...] = delta
        nm_ref[...] = nm
        nv_ref[...] = nv

    spec = pl.BlockSpec((tr, c), lambda i: (i, 0))
    outs, _ = _call(body, name=name, grid=(r // tr,), in_specs=[spec] * 4, out_specs=[spec] * 3,
                    out_shape=[jax.ShapeDtypeStruct((r, c), F32)] * 3, args=(w, g, m, v))
    return outs


def _sum_adamw(recvs, w, m, v, *, tr, name, carry=None):
    _, r, c = w.shape
    blocks = len(recvs[0])
    flat = [piece for layer in recvs for piece in layer]

    def body(*refs):
        recv_refs = refs[:len(flat)]
        w_ref, m_ref, v_ref, g_ref, d_ref, nm_ref, nv_ref = refs[len(flat):]
        layer = pl.program_id(0)
        for l in range(DEPTH):
            @pl.when(layer == l)
            def _(l=l):
                cols = []
                for piece in recv_refs[l * blocks:(l + 1) * blocks]:
                    part = piece[0].astype(F32)
                    for s in range(1, N_DEV):
                        part = part + piece[s].astype(F32)
                    cols.append(part)
                g = cols[0] if blocks == 1 else jnp.concatenate(cols, axis=-1)
                delta, nm, nv = _adamw_math(w_ref[0], g, m_ref[0], v_ref[0])
                g_ref[0] = g
                d_ref[0] = delta
                nm_ref[0] = nm
                nv_ref[0] = nv

    spec = pl.BlockSpec((1, tr, c), lambda l, i: (l, i, 0))
    return _call(
        body, name=name, grid=(DEPTH, r // tr),
        in_specs=[pl.BlockSpec((N_DEV, tr, c // blocks), lambda l, i: (0, i, 0))] * len(flat) + [spec] * 3,
        out_specs=[spec] * 4, out_shape=[jax.ShapeDtypeStruct((DEPTH, r, c), F32)] * 4,
        args=tuple(flat) + (w, m, v), carry=carry)


def _adamw_small(sheet, gain0, extra, params, *, name):
    sheet_rows = dict(norm1_g=ROW_NORM1, b_gate=ROW_BGATE, gmlp_ln_g=ROW_LN_G, gmlp_ln_b=ROW_LN_B, norm2_g=ROW_NORM2,
                      b_ffn_conv=ROW_BFCONV)
    names = list(params)
    extra_names = list(extra)

    def body(*refs):
        sg_ref, gain0_ref, refs = refs[0], refs[1], refs[2:]
        extra_refs, refs = dict(zip(extra_names, refs[:len(extra_names)])), refs[len(extra_names):]
        ins, outs, loss_ref = refs[:3 * len(names)], refs[3 * len(names):-1], refs[-1]
        loss_ref[...] = sg_ref[ROW_LOSS:ROW_LOSS + 1, 0:1]
        for j, key in enumerate(names):
            w_ref, m_ref, v_ref = ins[3 * j:3 * j + 3]
            g_ref, d_ref, nm_ref, nv_ref = outs[4 * j:4 * j + 4]
            if key in extra_refs:
                g_ref[...] = extra_refs[key][...]
            elif key == "final_g":
                g_ref[...] = sg_ref[ROW_FINAL:ROW_FINAL + 1, 0:D_MODEL]
            else:
                width = w_ref.shape[-1]
                for l in range(DEPTH):
                    row = SG_LAYER * l + sheet_rows[key]
                    if key == "norm1_g" and l == 0:
                        g_ref[0:1, :] = gain0_ref[0:1, :]
                    else:
                        g_ref[l:l + 1, :] = sg_ref[row:row + 1, 0:width]
            delta, nm, nv = _adamw_math(w_ref[...], g_ref[...], m_ref[...], v_ref[...])
            d_ref[...] = delta
            nm_ref[...] = nm
            nv_ref[...] = nv

    args = [sheet, gain0] + [extra[k] for k in extra_names] + [t for k in names for t in params[k]]
    vmem = pl.BlockSpec(memory_space=pltpu.VMEM)
    outs = pl.pallas_call(
        body, name=name, in_specs=[vmem] * len(args), out_specs=[vmem] * (4 * len(names) + 1),
        out_shape=[jax.ShapeDtypeStruct(params[k][0].shape, F32) for k in names for _ in range(4)]
        + [jax.ShapeDtypeStruct((1, 1), F32)],
    )(*args)
    return {k: tuple(outs[4 * j:4 * j + 4]) for j, k in enumerate(names)}, outs[-1]


def _rows(gathered):
    return gathered.reshape(N_DEV * gathered.shape[1], gathered.shape[2])


def _parts(full):
    return full.reshape(N_DEV, full.shape[0] // N_DEV, full.shape[1])


def kernel(x, norm1_g, w_in, b_gate, gmlp_ln_g, gmlp_ln_b, w_spatial, b_spatial, w_shortconv, w_branch, w_out, norm2_g, w_ffn_up, w_ffn_conv, b_ffn_conv, w_ffn_down, final_g, loss_target, m_norm1_g, m_w_in, m_b_gate, m_gmlp_ln_g, m_gmlp_ln_b, m_w_spatial, m_b_spatial, m_w_shortconv, m_w_branch, m_w_out, m_norm2_g, m_w_ffn_up, m_w_ffn_conv, m_b_ffn_conv, m_w_ffn_down, m_final_g, v_norm1_g, v_w_in, v_b_gate, v_gmlp_ln_g, v_gmlp_ln_b, v_w_spatial, v_b_spatial, v_w_shortconv, v_w_branch, v_w_out, v_norm2_g, v_w_ffn_up, v_w_ffn_conv, v_b_ffn_conv, v_w_ffn_down, v_final_g):
    n = x.shape[1]
    tm_in, tm, tn = 1024, 512, 2048
    x0 = x.reshape(n, D_MODEL)
    target = loss_target.reshape(n, D_MODEL)
    my_idx = 4 * lax.axis_index("x") + 2 * lax.axis_index("y") + lax.axis_index("c")
    sc_w, fc_w = D_B // N_DEV, D_FF // N_DEV

    sh_in = [w_in[l].T.astype(BF16) for l in range(DEPTH)]
    sh_up = [w_ffn_up[l].T.astype(BF16) for l in range(DEPTH)]
    sh_br = [w_branch[l].astype(BF16) for l in range(DEPTH)]
    sh_out = [w_out[l].astype(BF16) for l in range(DEPTH)]
    sh_down = [w_ffn_down[l].astype(BF16) for l in range(DEPTH)]
    taps = jnp.concatenate([w_shortconv, w_ffn_conv], axis=-1)

    def branch_weights(g):
        return g.transpose(1, 2, 0, 3).reshape(2, D_A, D_MODEL)

    g_in0, g_taps = _gather_now([sh_in[0], taps], name="gather_first")
    w_sc = [g_taps[:, l, :, :sc_w].transpose(1, 0, 2).reshape(3, D_B) for l in range(DEPTH)]
    w_fc = [g_taps[:, l, :, sc_w:].transpose(1, 0, 2).reshape(3, D_FF) for l in range(DEPTH)]
    w_s_t = [w_spatial[l].transpose(0, 2, 1) for l in range(DEPTH)]
    weights = [dict(), dict()]
    weights[0]["in_t"] = _rows(g_in0)
    saved = []
    xc = x0
    for l in range(DEPTH):
        p = weights[l]
        carry = _Gather([sh_br[0], sh_out[0]] if l == 0 else [sh_up[1]])
        (h, z), got = _norm_matmul(xc, norm1_g[l], p["in_t"], tm=tm_in, name=f"fwd_in_{l}", carry=carry)
        if l == 0:
            p["wb"], p["out"] = branch_weights(got[0]), _rows(got[1])
        else:
            p["up_t"] = _rows(got[0])
        carry = _Gather([sh_up[0]]) if l == 0 else None
        (ya, yb, conv, pa, pb, mg, x1), got = _mix_forward(
            z, xc, b_gate[l], gmlp_ln_g[l], gmlp_ln_b[l], w_spatial[l], b_spatial[l], w_sc[l], p["wb"], p["out"],
            tm=tm, name=f"fwd_mix_{l}", carry=carry)
        if l == 0:
            p["up_t"] = _rows(got[0])
        (h2, up), got = _norm_matmul(x1, norm2_g[l], p["up_t"], tm=tm, name=f"fwd_up_{l}", carry=_Gather([sh_down[l]]))
        p["down"] = _rows(got[0])
        carry = _Gather([sh_br[1], sh_out[1], sh_in[1]]) if l == 0 else None
        head = (final_g, target) if l == DEPTH - 1 else None
        outs, got = _ffn_forward(up, x1, w_fc[l], b_ffn_conv[l], p["down"], tm=tm, name=f"fwd_ffn_{l}", carry=carry, head=head)
        if l == 0:
            weights[1]["wb"], weights[1]["out"], weights[1]["in_t"] = branch_weights(got[0]), _rows(got[1]), _rows(got[2])
        gc, a = outs[0], outs[1]
        saved.append(dict(x=xc, h=h, z=z, ya=ya, yb=yb, conv=conv, pa=pa, pb=pb, mg=mg, x1=x1, h2=h2, up=up, gc=gc, a=a))
        xc = outs[2]
    dx, sheet = outs[2], outs[3]

    recv = [dict(), dict()]
    small_dws, small_dbs = [None] * DEPTH, [None] * DEPTH
    pending_in = None
    for l in reversed(range(DEPTH)):
        p, s = weights[l], saved[l]
        carry = _Exchange([pending_in]) if pending_in is not None else None
        (dup, sheet), got = _ffn_backward(dx, s["up"], s["gc"], w_fc[l], p["down"], sheet, l, tm=tm, name=f"bwd_ffn_{l}",
                                          carry=carry)
        if got is not None:
            recv[l + 1]["in_t"] = got[0]
        dw_down, _ = _matmul_tn(s["a"], dx, t1=D_FF // 2, tn=tn, name=f"dw_down_{l}")
        dw_up_t, got = _matmul_tn(dup, s["h2"], t1=2 * D_FF // 4, tn=tn, name=f"dw_up_{l}", pieces=2,
                                  carry=_Exchange([_parts(dw_down)]))
        recv[l]["down"] = got[0]
        (dx1, sheet), got_left = _matmul_norm_backward(
            dup, p["up_t"], s["x1"], norm2_g[l], dx, sheet, SG_LAYER * l + ROW_NORM2, tm=tm, name=f"bwd_up_{l}",
            carry=_Exchange([_parts(dw_up_t[0])]))
        dw_out, _ = _matmul_tn(s["mg"], dx1, t1=D_MODEL, tn=tn, name=f"dw_out_{l}")
        (dz, dpa, dpb, small_dws[l], small_dbs[l], sheet), got_right = _mix_backward(
            dx1, s["z"], s["conv"], s["pa"], s["pb"], b_gate[l], gmlp_ln_g[l], gmlp_ln_b[l], w_spatial[l], w_s_t[l],
            b_spatial[l], w_sc[l], p["out"], p["wb"], sheet, l, tm=tm, name=f"bwd_mix_{l}",
            carry=_Exchange([_parts(dw_up_t[1]), _parts(dw_out)]))
        recv[l]["up_t"], recv[l]["out"] = [got_left[0], got_right[0]], got_right[1]
        dw_bra_t, _ = _matmul_tn(dpa, s["ya"], t1=D_MODEL, tn=tn, name=f"dw_branch_a_{l}")
        dw_brb_t, _ = _matmul_tn(dpb, s["yb"], t1=D_MODEL, tn=tn, name=f"dw_branch_b_{l}")
        carry = _Exchange([_parts(dw_bra_t), _parts(dw_brb_t)])
        if l == 0:
            dbs = jnp.stack([t.reshape(N_HEADS, GMLP_BLOCK) for t in small_dbs]).reshape(DEPTH * N_HEADS, GMLP_BLOCK)
            carry = _Both(carry, _Gather([sheet, small_dws[0], small_dws[1], dbs]))
        dw_in_t, got = _matmul_tn(dz, s["h"], t1=D_IN // 4, tn=tn, name=f"dw_in_{l}", carry=carry)
        recv[l]["bra_t"], recv[l]["brb_t"] = got[:2]
        if l == 0:
            gathered_small = got[2:]
            send_sems, recv_sems, parts_thru, land_thru, gain = _exchange_start(
                _parts(dw_in_t), norm1_g[l].reshape(1, D_MODEL), name="exchange_w_in_0_start")
            (dx0, dg1_first), _ = _matmul_norm_backward(dz, p["in_t"], s["x"], gain.reshape(D_MODEL), dx1, None, 0,
                                                        tm=tm, name=f"bwd_in_{l}")
            recv[0]["in_t"] = _exchange_wait(send_sems, recv_sems, parts_thru, land_thru, dg1_first,
                                             name="exchange_w_in_0_wait")
        else:
            (dx0, sheet), _ = _matmul_norm_backward(dz, p["in_t"], s["x"], norm1_g[l], dx1, sheet,
                                                    SG_LAYER * l + ROW_NORM1, tm=tm, name=f"bwd_in_{l}")
            pending_in = _parts(dw_in_t)
        dx = dx0
    grad_x = dx.reshape(x.shape)

    results = {}
    both = lambda key: [recv[l][key] for l in range(DEPTH)]
    blocks = lambda key: [r if isinstance(r, list) else [r] for r in both(key)]
    swap = lambda t: t.transpose(0, 2, 1)
    for key, slab, (w, m, v), tr in [("w_in", "in_t", (w_in, m_w_in, v_w_in), 192),
                                     ("w_ffn_up", "up_t", (w_ffn_up, m_w_ffn_up, v_w_ffn_up), 176)]:
        carry = _Gather([dg1_first]) if key == "w_in" else None
        outs, got = _sum_adamw(blocks(slab), swap(w), swap(m), swap(v), tr=tr, name=f"adamw_{key}", carry=carry)
        if got is not None:
            gathered_small = list(gathered_small) + [got[0]]
        results[key] = tuple(swap(o) for o in outs)
    g_bra = _sum_parts(both("bra_t"), tr=128, name="sum_w_branch_a").transpose(0, 2, 1)
    g_brb = _sum_parts(both("brb_t"), tr=128, name="sum_w_branch_b").transpose(0, 2, 1)
    g_br = jnp.stack([g_bra, g_brb], axis=1)
    flat = lambda t: t.reshape(-1, t.shape[-1])
    outs = _adamw(flat(w_branch), flat(g_br), flat(m_w_branch), flat(v_w_branch), tr=512, name="adamw_w_branch")
    results["w_branch"] = (g_br,) + tuple(o.reshape(w_branch.shape) for o in outs)
    results["w_out"] = tuple(_sum_adamw(blocks("out"), w_out, m_w_out, v_w_out, tr=128, name="adamw_w_out")[0])
    results["w_ffn_down"] = tuple(_sum_adamw(blocks("down"), w_ffn_down, m_w_ffn_down, v_w_ffn_down, tr=176,
                                             name="adamw_w_ffn_down")[0])

    sheet, dws0, dws1, dbs, gain0 = _sum_gathered(gathered_small, name="sum_small_grads")
    swap_taps = lambda t: t.transpose(1, 0, 2)
    taps = lambda row, width: jnp.stack([sheet[SG_LAYER * l + row:SG_LAYER * l + row + 3, :width] for l in range(DEPTH)], axis=1)
    extra = dict(w_spatial=jnp.stack([dws0, dws1]), b_spatial=dbs.reshape(DEPTH, N_HEADS, GMLP_BLOCK),
                 w_shortconv=lax.dynamic_slice_in_dim(taps(ROW_SCONV, D_B), my_idx * sc_w, sc_w, axis=2),
                 w_ffn_conv=lax.dynamic_slice_in_dim(taps(ROW_FCONV, D_FF), my_idx * fc_w, fc_w, axis=2))
    small_w = dict(norm1_g=(norm1_g, m_norm1_g, v_norm1_g), b_gate=(b_gate, m_b_gate, v_b_gate),
                   gmlp_ln_g=(gmlp_ln_g, m_gmlp_ln_g, v_gmlp_ln_g), gmlp_ln_b=(gmlp_ln_b, m_gmlp_ln_b, v_gmlp_ln_b),
                   w_spatial=(w_spatial, m_w_spatial, v_w_spatial), b_spatial=(b_spatial, m_b_spatial, v_b_spatial),
                   w_shortconv=tuple(swap_taps(t) for t in (w_shortconv, m_w_shortconv, v_w_shortconv)), norm2_g=(norm2_g, m_norm2_g, v_norm2_g),
                   w_ffn_conv=tuple(swap_taps(t) for t in (w_ffn_conv, m_w_ffn_conv, v_w_ffn_conv)), b_ffn_conv=(b_ffn_conv, m_b_ffn_conv, v_b_ffn_conv),
                   final_g=tuple(t.reshape(1, D_MODEL) for t in (final_g, m_final_g, v_final_g)))
    small, loss = _adamw_small(sheet, gain0, extra, small_w, name="adamw_small")
    loss = loss.reshape(())
    results.update(small)
    results["final_g"] = tuple(t.reshape(D_MODEL) for t in results["final_g"])
    for key in ("w_shortconv", "w_ffn_conv"):
        results[key] = tuple(swap_taps(t) for t in results[key])

    names = ["norm1_g", "w_in", "b_gate", "gmlp_ln_g", "gmlp_ln_b", "w_spatial", "b_spatial", "w_shortconv", "w_branch",
             "w_out", "norm2_g", "w_ffn_up", "w_ffn_conv", "b_ffn_conv", "w_ffn_down", "final_g"]
    return (loss, grad_x, *[results[k][0] for k in names], *[results[k][1] for k in names],
            *[results[k][2] for k in names], *[results[k][3] for k in names])
```

```python
import math

import jax
import jax.numpy as jnp
from jax import lax
from jax.experimental import pallas as pl
from jax.experimental.pallas import tpu as pltpu

F32 = jnp.float32
BF16 = jnp.bfloat16

N_DEV = 8
DEPTH = 2
D_MODEL = 1024
D_A = 512
D_B = 512
D_FF = 2816
D_IN = 4608
N_HEADS = 4
HEAD = 128
GMLP_BLOCK = 128
CAUSAL_CHUNK = 64
OFF_U, OFF_V, OFF_BG, OFF_CG, OFF_HB, OFF_GA, OFF_GB = 0, 512, 1024, 1536, 2048, 2560, 3584
RMS_EPS = 1e-6
LN_EPS = 1e-5
ADAM_LR, ADAM_B1, ADAM_B2, ADAM_EPS, ADAM_WD, ADAM_STEP = 0.001, 0.9, 0.999, 1e-08, 0.01, 10

SUBLANES, LANES = 8, 128
MATMUL_CHUNK = 512
HALO = 16
FFN_CHUNK = 256
SG_ROWS, SG_W, SG_LAYER = 40, D_FF, 16
ROW_NORM1, ROW_BGATE, ROW_LN_G, ROW_LN_B, ROW_SCONV, ROW_NORM2, ROW_FCONV, ROW_BFCONV = 0, 1, 2, 3, 4, 7, 8, 11
ROW_FINAL, ROW_LOSS = 32, 33
V7X_VMEM_BYTES = 64 << 20
VMEM_LIMIT = V7X_VMEM_BYTES - (8 << 20)
MESH = pl.DeviceIdType.MESH
GELU_C0 = 0.7978845608028654
GELU_C1 = 0.044715
NT = (((1,), (1,)), ((), ()))
TN = (((0,), (0,)), ((), ()))


def _dot(a, b):
    return jnp.dot(a, b, preferred_element_type=F32)


def _dot_nt(a, b):
    return lax.dot_general(a, b, NT, preferred_element_type=F32)


def _sigmoid(x):
    return 1.0 / (1.0 + jnp.exp(-x))


def _gelu_tanh(x):
    return jnp.tanh(GELU_C0 * (x + GELU_C1 * x * x * x))


def _gelu_grad(x, t):
    return 0.5 * (1.0 + t) + 0.5 * x * (1.0 - t * t) * GELU_C0 * (1.0 + 3.0 * GELU_C1 * x * x)


def _sublane_tile(dtype):
    return SUBLANES * (4 // jnp.dtype(dtype).itemsize)


def _shift_down(a, k, prev):
    p = prev.shape[0]
    r = pltpu.roll(a, k, 0)
    sub = _sublane_tile(a.dtype)
    head = r[0:sub]
    rid = lax.broadcasted_iota(jnp.int32, head.shape, 0)
    for j in range(k):
        head = jnp.where(rid == j, prev[p - k + j:p - k + j + 1, :], head)
    return jnp.concatenate([head, r[sub:]], axis=0)


def _shift_up(a, k, nxt):
    t = a.shape[0]
    r = pltpu.roll(a, t - k, 0)
    sub = _sublane_tile(a.dtype)
    tail = r[t - sub:t]
    rid = lax.broadcasted_iota(jnp.int32, tail.shape, 0)
    for j in range(k):
        tail = jnp.where(rid == sub - k + j, nxt[j:j + 1, :], tail)
    return jnp.concatenate([r[0:t - sub], tail], axis=0)


def _column_sums(p):
    if p.dtype.itemsize < 4:
        t = p.shape[0]
        p = p[:t // 2] + p[t // 2:]
        p = p[:t // 4] + p[t // 4:]
    return jnp.sum(p.astype(F32), axis=0, keepdims=True)


def _sheet_begin(step, sheet_in, sheet_out, first_row, rows):
    @pl.when(step == 0)
    def _():
        sheet_out[...] = sheet_in[...]
        sheet_out[first_row:first_row + rows, :] = jnp.zeros((rows, SG_W), F32)


def _sheet_spec():
    return pl.BlockSpec((SG_ROWS, SG_W), lambda i: (0, 0))


def _spatial_mask(transposed):
    ri = lax.broadcasted_iota(jnp.int32, (GMLP_BLOCK, GMLP_BLOCK), 0) // CAUSAL_CHUNK
    ci = lax.broadcasted_iota(jnp.int32, (GMLP_BLOCK, GMLP_BLOCK), 1) // CAUSAL_CHUNK
    return (ri <= ci) if transposed else (ci <= ri)


def _gmlp_forward(u, v, ln_g, ln_b, ws_ref, bs_ref, f_scr):
    tm = u.shape[0]
    tu = _gelu_tanh(u)
    tv = _gelu_tanh(v)
    gu = 0.5 * u * (1.0 + tu)
    gv = 0.5 * v * (1.0 + tv)
    mu = jnp.mean(gv, axis=-1, keepdims=True)
    cen = gv - mu
    rstd = lax.rsqrt(jnp.mean(cen * cen, axis=-1, keepdims=True) + LN_EPS)
    xh = cen * rstd
    vn = (xh * ln_g + ln_b).astype(BF16)
    mask = _spatial_mask(False)
    wm = [jnp.where(mask, ws_ref[h], 0.0).astype(BF16) for h in range(N_HEADS)]
    for b in range(tm // GMLP_BLOCK):
        rows = slice(b * GMLP_BLOCK, (b + 1) * GMLP_BLOCK)
        for h in range(N_HEADS):
            cols = slice(h * HEAD, (h + 1) * HEAD)
            f_scr[rows, cols] = (_dot(wm[h], vn[rows, cols]) + bs_ref[h]).astype(f_scr.dtype)
    return gu, tu, tv, xh, rstd, vn, f_scr[...]


def _position():
    return lax.axis_index("x"), lax.axis_index("y"), lax.axis_index("c")


def _handshake(peers):
    barrier = pltpu.get_barrier_semaphore()
    for peer in peers:
        pl.semaphore_signal(barrier, inc=1, device_id=peer, device_id_type=MESH)
    pl.semaphore_wait(barrier, len(peers))


class _Gather:
    collective_id = 1

    def __init__(self, arrays):
        self.arrays = list(arrays)
        self.out_shape = [jax.ShapeDtypeStruct((N_DEV,) + a.shape, a.dtype) for a in self.arrays]
        self.base = 0

    def barrier(self):
        x, y, c = _position()
        _handshake([(x, y, 1 - c), (1 - x, y, c), (x, 1 - y, c), (1 - x, 1 - y, c)])

    def _plan(self, ins, outs, sems):
        send_sems, recv_sems, local_sems = sems
        x, y, c = _position()
        me, sibling = (x, y, c), (x, y, 1 - c)
        chips = [(1 - x, y), (x, 1 - y), (1 - x, 1 - y)]

        def slot(a, p):
            return outs[a].at[4 * p[0] + 2 * p[1] + p[2]]

        def copy(a, k, block, to, src=None):
            return pltpu.make_async_remote_copy(
                src_ref=slot(a, block) if src is None else src, dst_ref=slot(a, block),
                send_sem=send_sems.at[self.base + a, k], recv_sem=recv_sems.at[self.base + a, k],
                device_id=to, device_id_type=MESH)

        n = len(self.arrays)

        def mine():
            return [pltpu.make_async_copy(ins[a], slot(a, me), local_sems.at[self.base + a]) for a in range(n)]

        def first():
            out = []
            for a in range(n):
                out.append(copy(a, 0, me, sibling, src=ins[a]))
                out += [copy(a, 1 + j, me, (*chip, c), src=ins[a]) for j, chip in enumerate(chips)]
            return out

        def arrivals():
            return [copy(a, 1 + j, (*chip, c), me) for j, chip in enumerate(chips) for a in range(n)]

        def relays():
            return [copy(a, 4 + j, (*chip, c), sibling) for j, chip in enumerate(chips) for a in range(n)]

        def from_sibling():
            out = [copy(a, 0, sibling, me) for a in range(n)]
            return out + [copy(a, 4 + j, (*chip, 1 - c), me) for j, chip in enumerate(chips) for a in range(n)]

        return mine, first, arrivals, relays, from_sibling

    def start(self, ins, outs, sems):
        mine, first, _, _, _ = self._plan(ins, outs, sems)
        for cp in mine() + first():
            cp.start()

    def relay(self, ins, outs, sems):
        _, _, arrivals, relays, _ = self._plan(ins, outs, sems)
        for arrived, onward in zip(arrivals(), relays()):
            arrived.wait_recv()
            onward.start()

    def finish(self, ins, outs, sems):
        mine, first, _, relays, from_sibling = self._plan(ins, outs, sems)
        for cp in from_sibling():
            cp.wait_recv()
        for cp in first() + relays():
            cp.wait_send()
        for cp in mine():
            cp.wait()


class _Exchange:
    collective_id = 0

    def __init__(self, arrays):
        self.arrays = list(arrays)
        self.out_shape = [jax.ShapeDtypeStruct(a.shape, a.dtype) for a in self.arrays]
        self.base = 0

    def barrier(self):
        x, y, c = _position()
        _handshake([(x ^ dx, y ^ dy, c ^ dc) for dx in (0, 1) for dy in (0, 1) for dc in (0, 1) if dx or dy or dc])

    def _plan(self, ins, outs, sems):
        send_sems, recv_sems, local_sems = sems
        x, y, c = _position()
        my_idx = 4 * x + 2 * y + c
        n = len(self.arrays)
        offsets = [(dx, dy, dc) for dx in (0, 1) for dy in (0, 1) for dc in (0, 1) if (dx, dy, dc) != (0, 0, 0)]

        def mine():
            return [pltpu.make_async_copy(ins[a].at[my_idx], outs[a].at[my_idx], local_sems.at[self.base + a])
                    for a in range(n)]

        def remote(arriving):
            out = []
            for k, (dx, dy, dc) in enumerate(offsets):
                px, py, pc = x ^ dx, y ^ dy, c ^ dc
                p_idx = 4 * px + 2 * py + pc
                for a in range(n):
                    out.append(pltpu.make_async_remote_copy(
                        src_ref=ins[a].at[p_idx], dst_ref=outs[a].at[p_idx if arriving else my_idx],
                        send_sem=send_sems.at[self.base + a, k], recv_sem=recv_sems.at[self.base + a, k],
                        device_id=(px, py, pc), device_id_type=MESH))
            return out

        return mine, remote

    def start(self, ins, outs, sems):
        mine, remote = self._plan(ins, outs, sems)
        for cp in mine() + remote(False):
            cp.start()

    def relay(self, ins, outs, sems):
        pass

    def finish(self, ins, outs, sems):
        mine, remote = self._plan(ins, outs, sems)
        for cp in remote(True):
            cp.wait_recv()
        for cp in remote(False):
            cp.wait_send()
        for cp in mine():
            cp.wait()


def _exchange_start(parts, thru, *, name):
    hbm = pl.BlockSpec(memory_space=pltpu.HBM)
    sem = pl.BlockSpec(memory_space=pltpu.SEMAPHORE)
    offsets = [(dx, dy, dc) for dx in (0, 1) for dy in (0, 1) for dc in (0, 1) if dx or dy or dc]

    def body(parts_ref, thru_ref, send_sems, recv_sems, parts_thru, land_ref, thru_out):
        x, y, c = _position()
        my_idx = 4 * x + 2 * y + c
        pltpu.make_async_copy(parts_ref.at[my_idx], land_ref.at[my_idx], send_sems.at[7]).start()
        _handshake([(x ^ dx, y ^ dy, c ^ dc) for dx, dy, dc in offsets])
        for k, (dx, dy, dc) in enumerate(offsets):
            px, py, pc = x ^ dx, y ^ dy, c ^ dc
            pltpu.make_async_remote_copy(
                src_ref=parts_ref.at[4 * px + 2 * py + pc], dst_ref=land_ref.at[my_idx],
                send_sem=send_sems.at[k], recv_sem=recv_sems.at[k],
                device_id=(px, py, pc), device_id_type=MESH).start()
        thru_out[...] = thru_ref[...]

    return pl.pallas_call(
        body, name=name,
        out_shape=(pltpu.SemaphoreType.DMA((8,)), pltpu.SemaphoreType.DMA((7,)), pltpu.HBM(parts.shape, parts.dtype),
                   pltpu.HBM(parts.shape, parts.dtype), jax.ShapeDtypeStruct(thru.shape, thru.dtype)),
        in_specs=(hbm, pl.BlockSpec(memory_space=pltpu.VMEM)),
        out_specs=(sem, sem, hbm, hbm, pl.BlockSpec(memory_space=pltpu.VMEM)),
        input_output_aliases={0: 2},
        compiler_params=pltpu.CompilerParams(has_side_effects=pltpu.SideEffectType.DATAFLOW_SIDE_EFFECTING,
                                             collective_id=2),
    )(pltpu.with_memory_space_constraint(parts, pltpu.HBM), thru)


def _exchange_wait(send_sems, recv_sems, parts_thru, land_thru, after, *, name):
    hbm = pl.BlockSpec(memory_space=pltpu.HBM)
    sem = pl.BlockSpec(memory_space=pltpu.SEMAPHORE)
    offsets = [(dx, dy, dc) for dx in (0, 1) for dy in (0, 1) for dc in (0, 1) if dx or dy or dc]

    def body(parts_ref, land_ref, send_sems, recv_sems, after_ref, parts_dead, got_ref):
        x, y, c = _position()
        my_idx = 4 * x + 2 * y + c
        pltpu.make_async_copy(parts_ref.at[my_idx], land_ref.at[my_idx], send_sems.at[7]).wait()
        for k, (dx, dy, dc) in enumerate(offsets):
            px, py, pc = x ^ dx, y ^ dy, c ^ dc
            p_idx = 4 * px + 2 * py + pc
            copy = pltpu.make_async_remote_copy(
                src_ref=parts_ref.at[p_idx], dst_ref=land_ref.at[p_idx], send_sem=send_sems.at[k],
                recv_sem=recv_sems.at[k], device_id=(px, py, pc), device_id_type=MESH)
            copy.wait_send()
            copy.wait_recv()

    return pl.pallas_call(
        body, name=name,
        out_shape=(pltpu.HBM(parts_thru.shape, parts_thru.dtype), pltpu.HBM(land_thru.shape, land_thru.dtype)),
        in_specs=(hbm, hbm, sem, sem, pl.BlockSpec(memory_space=pl.ANY)), out_specs=(hbm, hbm),
        input_output_aliases={0: 0, 1: 1},
        compiler_params=pltpu.CompilerParams(has_side_effects=pltpu.SideEffectType.DATAFLOW_SIDE_EFFECTING),
    )(parts_thru, land_thru, send_sems, recv_sems, after)[1]


class _Both:
    def __init__(self, *carries):
        self.carries = carries
        self.arrays = [a for c in carries for a in c.arrays]
        self.out_shape = [s for c in carries for s in c.out_shape]
        first = 0
        for c in carries:
            c.base = first
            first += len(c.arrays)
        self.collective_id = min(c.collective_id for c in carries)

    def barrier(self):
        min(self.carries, key=lambda c: c.collective_id).barrier()

    def _each(self, method, ins, outs, sems):
        for c in self.carries:
            rows = slice(c.base, c.base + len(c.arrays))
            getattr(c, method)(ins[rows], outs[rows], sems)

    def start(self, ins, outs, sems):
        self._each("start", ins, outs, sems)

    def relay(self, ins, outs, sems):
        self._each("relay", ins, outs, sems)

    def finish(self, ins, outs, sems):
        self._each("finish", ins, outs, sems)


def _call(body, *, name, grid, in_specs, out_specs, out_shape, args, scratch_shapes=(), carry=None):
    n_in, n_out, n_scr = len(in_specs), len(out_specs), len(scratch_shapes)
    params = pltpu.CompilerParams(dimension_semantics=("arbitrary",) * len(grid), vmem_limit_bytes=VMEM_LIMIT)
    if carry is None:
        outs = pl.pallas_call(body, name=name, grid=grid, in_specs=in_specs, out_specs=out_specs, out_shape=out_shape,
                              scratch_shapes=list(scratch_shapes), compiler_params=params)(*args)
        return outs, None
    m = len(carry.arrays)
    total = math.prod(grid)

    def wrapped(*refs):
        ins, refs = refs[:n_in], refs[n_in:]
        c_ins, refs = refs[:m], refs[m:]
        outs, refs = refs[:n_out], refs[n_out:]
        c_outs, refs = refs[:m], refs[m:]
        scr, sems = refs[:n_scr], refs[n_scr:]
        flat = pl.program_id(0)
        for d in range(1, len(grid)):
            flat = flat * grid[d] + pl.program_id(d)

        @pl.when(flat == 0)
        def _():
            carry.barrier()
            carry.start(c_ins, c_outs, sems)

        body(*ins, *outs, *scr)

        @pl.when(flat == total - 2)
        def _():
            carry.relay(c_ins, c_outs, sems)

        @pl.when(flat == total - 1)
        def _():
            carry.finish(c_ins, c_outs, sems)

    any_spec = pl.BlockSpec(memory_space=pl.ANY)
    sem_shapes = [pltpu.SemaphoreType.DMA((m, 7)), pltpu.SemaphoreType.DMA((m, 7)), pltpu.SemaphoreType.DMA((m,))]
    params = pltpu.CompilerParams(dimension_semantics=("arbitrary",) * len(grid), vmem_limit_bytes=VMEM_LIMIT,
                                  collective_id=carry.collective_id)
    outs = pl.pallas_call(
        wrapped, name=name, grid=grid,
        in_specs=list(in_specs) + [any_spec] * m, out_specs=list(out_specs) + [any_spec] * m,
        out_shape=list(out_shape) + carry.out_shape,
        scratch_shapes=list(scratch_shapes) + sem_shapes, compiler_params=params)(*args, *carry.arrays)
    return outs[:n_out], outs[n_out:]


def _gather_now(arrays, *, name):
    carry = _Gather(arrays)
    m = len(arrays)

    def body(*refs):
        ins, outs, sems = refs[:m], refs[m:2 * m], refs[2 * m:]
        carry.barrier()
        carry.start(ins, outs, sems)
        carry.relay(ins, outs, sems)
        carry.finish(ins, outs, sems)

    any_spec = pl.BlockSpec(memory_space=pl.ANY)
    return pl.pallas_call(
        body, name=name, in_specs=[any_spec] * m, out_specs=[any_spec] * m, out_shape=carry.out_shape,
        scratch_shapes=[pltpu.SemaphoreType.DMA((m, 7)), pltpu.SemaphoreType.DMA((m, 7)),
                        pltpu.SemaphoreType.DMA((m,))],
        compiler_params=pltpu.CompilerParams(collective_id=carry.collective_id),
    )(*arrays)


def _sum_gathered(arrs, late, *, name):
    n = len(arrs)

    def body(*refs):
        late_ref, ins, late_out, outs = refs[0], refs[1:1 + n], refs[1 + n], refs[2 + n:2 + 2 * n]
        buf, send_sems, recv_sems = refs[2 + 2 * n:]
        x, y, c = _position()
        me, sibling = (x, y, c), (x, y, 1 - c)
        chips = [(1 - x, y), (x, 1 - y), (1 - x, 1 - y)]
        _handshake([sibling] + [(*chip, c) for chip in chips])

        def copy(k, block, to, src=None):
            slot = buf.at[4 * block[0] + 2 * block[1] + block[2]]
            return pltpu.make_async_remote_copy(
                src_ref=slot if src is None else src, dst_ref=slot,
                send_sem=send_sems.at[k], recv_sem=recv_sems.at[k], device_id=to, device_id_type=MESH)

        first = [copy(0, me, sibling, src=late_ref)]
        first += [copy(1 + j, me, (*chip, c), src=late_ref) for j, chip in enumerate(chips)]
        for cp in first:
            cp.start()
        for in_ref, out_ref in zip(ins, outs):
            acc = in_ref[0]
            for s in range(1, N_DEV):
                acc = acc + in_ref[s]
            out_ref[...] = acc
        passed = []
        for j, chip in enumerate(chips):
            copy(1 + j, (*chip, c), me).wait_recv()
            cp = copy(4 + j, (*chip, c), sibling)
            cp.start()
            passed.append(cp)
        copy(0, sibling, me).wait_recv()
        for j, chip in enumerate(chips):
            copy(4 + j, (*chip, 1 - c), me).wait_recv()
        for cp in first + passed:
            cp.wait_send()
        my_idx = 4 * x + 2 * y + c
        acc = jnp.zeros(late_ref.shape, F32)
        for s in range(N_DEV):
            acc = acc + jnp.where(my_idx == s, late_ref[...], buf[s])
        late_out[...] = acc

    vmem = pl.BlockSpec(memory_space=pltpu.VMEM)
    outs = pl.pallas_call(
        body, name=name, in_specs=[vmem] * (n + 1), out_specs=[vmem] * (n + 1),
        out_shape=[jax.ShapeDtypeStruct(late.shape, F32)] + [jax.ShapeDtypeStruct(a.shape[1:], F32) for a in arrs],
        scratch_shapes=[pltpu.VMEM((N_DEV,) + late.shape, F32), pltpu.SemaphoreType.DMA((7,)),
                        pltpu.SemaphoreType.DMA((7,))],
        compiler_params=pltpu.CompilerParams(vmem_limit_bytes=VMEM_LIMIT, collective_id=_Gather.collective_id),
    )(late, *arrs)
    return list(outs[1:]) + [outs[0]]


def _norm_matmul(x, g, w_t, *, tm, name, carry=None):
    n, d = x.shape
    c = w_t.shape[0]
    ch = MATMUL_CHUNK

    def body(x_ref, g_ref, wt_ref, h_ref, z_ref):
        xv = x_ref[...]
        r = lax.rsqrt(jnp.mean(xv * xv, axis=-1, keepdims=True) + RMS_EPS)
        h = (xv * r * g_ref[...]).astype(BF16)
        h_ref[...] = h
        for c0 in range(0, c, ch):
            z_ref[:, c0:c0 + ch] = _dot_nt(h, wt_ref[c0:c0 + ch, :]).astype(BF16)

    return _call(
        body, name=name, grid=(n // tm,), carry=carry,
        in_specs=[pl.BlockSpec((tm, d), lambda i: (i, 0)),
                  pl.BlockSpec((1, d), lambda i: (0, 0)),
                  pl.BlockSpec((c, d), lambda i: (0, 0))],
        out_specs=[pl.BlockSpec((tm, d), lambda i: (i, 0)),
                   pl.BlockSpec((tm, c), lambda i: (i, 0))],
        out_shape=[jax.ShapeDtypeStruct((n, d), BF16), jax.ShapeDtypeStruct((n, c), BF16)],
        args=(x, g.reshape(1, d), w_t))


def _mix_forward(z, x, b_gate, ln_g, ln_b, w_s, b_s, w_sc, wb, w_out, *, tm, name, carry=None):
    n = z.shape[0]
    hb = tm // HALO

    def body(z_ref, zp_ref, x_ref, bg_ref, lng_ref, lnb_ref, ws_ref, bs_ref, wsc_ref, wb_ref, wo_ref,
             ya_ref, yb_ref, cv_ref, pa_ref, pb_ref, mg_ref, x1_ref, f_scr):
        i = pl.program_id(0)
        u = z_ref[:, OFF_U:OFF_U + D_A]
        v = z_ref[:, OFF_V:OFF_V + D_A].astype(F32)
        gu, _, _, _, _, _, f = _gmlp_forward(u, v, lng_ref[...], lnb_ref[...], ws_ref, bs_ref, f_scr)
        ya = gu * f
        ya_ref[...] = ya

        q = z_ref[:, OFF_CG:OFF_CG + D_B] * z_ref[:, OFF_HB:OFF_HB + D_B]
        qp = zp_ref[:, OFF_CG:OFF_CG + D_B] * zp_ref[:, OFF_HB:OFF_HB + D_B]
        qp = jnp.where(i > 0, qp, jnp.zeros_like(qp))
        w = wsc_ref[...].astype(BF16)
        conv = w[0:1] * _shift_down(q, 2, qp) + w[1:2] * _shift_down(q, 1, qp) + w[2:3] * q
        cv_ref[...] = conv
        yb = z_ref[:, OFF_BG:OFF_BG + D_B] * conv
        yb_ref[...] = yb

        pa = _dot(ya, wb_ref[0]).astype(BF16)
        pb = _dot(yb, wb_ref[1]).astype(BF16)
        pa_ref[...] = pa
        pb_ref[...] = pb
        bg = bg_ref[...].astype(BF16)
        sa = _sigmoid(z_ref[:, OFF_GA:OFF_GA + D_MODEL] + bg[:, 0:D_MODEL])
        sb = _sigmoid(z_ref[:, OFF_GB:OFF_GB + D_MODEL] + bg[:, D_MODEL:2 * D_MODEL])
        mg = sa * pa + sb * pb
        mg_ref[...] = mg
        x1_ref[...] = x_ref[...] + _dot(mg, wo_ref[...])

    row = lambda w: pl.BlockSpec((tm, w), lambda i: (i, 0))
    full = lambda *s: pl.BlockSpec(s, lambda i: (0,) * len(s))
    bf = lambda w: jax.ShapeDtypeStruct((n, w), BF16)
    return _call(
        body, name=name, grid=(n // tm,), carry=carry,
        in_specs=[row(D_IN),
                  pl.BlockSpec((HALO, D_IN), lambda i: (jnp.maximum(i * hb - 1, 0), 0)),
                  row(D_MODEL), full(1, 2 * D_MODEL), full(1, D_A), full(1, D_A),
                  full(N_HEADS, GMLP_BLOCK, GMLP_BLOCK), full(N_HEADS, GMLP_BLOCK, 1), full(3, D_B),
                  full(2, D_A, D_MODEL), full(D_MODEL, D_MODEL)],
        out_specs=[row(D_A), row(D_B), row(D_B), row(D_MODEL), row(D_MODEL), row(D_MODEL), row(D_MODEL)],
        out_shape=[bf(D_A), bf(D_B), bf(D_B), bf(D_MODEL), bf(D_MODEL), bf(D_MODEL),
                   jax.ShapeDtypeStruct((n, D_MODEL), F32)],
        scratch_shapes=[pltpu.VMEM((tm, D_A), BF16)],
        args=(z, z, x, b_gate.reshape(1, -1), ln_g.reshape(1, -1), ln_b.reshape(1, -1), w_s,
              b_s.reshape(N_HEADS, GMLP_BLOCK, 1), w_sc, wb, w_out))


def _loss_tile(xv, gv, tv):
    d = xv.shape[-1]
    r = lax.rsqrt(jnp.mean(xv * xv, axis=-1, keepdims=True) + RMS_EPS)
    xh = xv * r
    e = xh * gv - tv
    per_row = jnp.sum(e * e, axis=-1, keepdims=True) * (0.5 / d)
    dy = e * (1.0 / d)
    dxh = dy * gv
    dx = r * (dxh - xh * jnp.mean(dxh * xh, axis=-1, keepdims=True))
    return dx, jnp.sum(per_row, axis=0, keepdims=True), jnp.sum(dy * xh, axis=0, keepdims=True)


def _ffn_forward(up, x1, w_fc, b_fc, w_down, *, tm, name, carry=None, head=None):
    n = up.shape[0]
    hb = tm // HALO
    n_in = 6 if head is None else 8

    def body(*refs):
        up_ref, upp_ref, x1_ref, wfc_ref, bfc_ref, wd_ref = refs[:6]
        gc_ref, a_ref, out_ref = refs[n_in:n_in + 3]
        acc = refs[-1]
        i = pl.program_id(0)
        acc[...] = x1_ref[...]
        for c0 in range(0, D_FF, FFN_CHUNK):
            cols = slice(c0, c0 + FFN_CHUNK)
            gate = up_ref[:, cols]
            val = up_ref[:, D_FF + c0:D_FF + c0 + FFN_CHUNK]
            gp = upp_ref[:, cols]
            gp = jnp.where(i > 0, gp, jnp.zeros_like(gp))
            w = wfc_ref[:, cols].astype(BF16)
            gc = (w[0:1] * _shift_down(gate, 2, gp) + w[1:2] * _shift_down(gate, 1, gp) + w[2:3] * gate
                  + bfc_ref[:, cols].astype(BF16))
            gc_ref[:, cols] = gc
            a = gc * _sigmoid(gc) * val
            a_ref[:, cols] = a
            acc[...] += _dot(a, wd_ref[cols, :])
        if head is None:
            out_ref[...] = acc[...]
        else:
            g_ref, t_ref = refs[6:8]
            sg_ref = refs[n_in + 3]

            @pl.when(i == 0)
            def _():
                sg_ref[...] = jnp.zeros_like(sg_ref)

            dx, loss, dg = _loss_tile(acc[...], g_ref[...], t_ref[...])
            out_ref[...] = dx
            sg_ref[ROW_LOSS:ROW_LOSS + 1, 0:LANES] += jnp.broadcast_to(loss, (1, LANES))
            sg_ref[ROW_FINAL:ROW_FINAL + 1, 0:D_MODEL] += dg

    row = lambda w: pl.BlockSpec((tm, w), lambda i: (i, 0))
    full = lambda r, c: pl.BlockSpec((r, c), lambda i: (0, 0))
    in_specs = [row(2 * D_FF), pl.BlockSpec((HALO, D_FF), lambda i: (jnp.maximum(i * hb - 1, 0), 0)), row(D_MODEL),
                full(3, D_FF), full(1, D_FF), full(D_FF, D_MODEL)]
    out_specs = [row(D_FF), row(D_FF), row(D_MODEL)]
    out_shape = [jax.ShapeDtypeStruct((n, D_FF), BF16), jax.ShapeDtypeStruct((n, D_FF), BF16),
                 jax.ShapeDtypeStruct((n, D_MODEL), F32)]
    args = (up, up, x1, w_fc, b_fc.reshape(1, -1), w_down)
    if head is not None:
        in_specs += [full(1, D_MODEL), row(D_MODEL)]
        out_specs += [full(SG_ROWS, SG_W)]
        out_shape += [jax.ShapeDtypeStruct((SG_ROWS, SG_W), F32)]
        args += (head[0].reshape(1, -1), head[1])
    return _call(body, name=name, grid=(n // tm,), carry=carry, in_specs=in_specs, out_specs=out_specs,
                 out_shape=out_shape, scratch_shapes=[pltpu.VMEM((tm, D_MODEL), F32)], args=args)


def _ffn_backward(dx2, up, gc, w_fc, w_down, sheet, layer, *, tm, name, carry=None):
    n = up.shape[0]
    steps = n // tm
    hb = tm // HALO
    row = SG_LAYER * layer + ROW_FCONV

    def body(dx_ref, dxn_ref, up_ref, upn_ref, gc_ref, gcn_ref, wfc_ref, wd_ref, sg_in, dup_ref, sg_ref):
        i = pl.program_id(0)
        last = i == steps - 1
        _sheet_begin(i, sg_in, sg_ref, row, 4)

        dxe = jnp.concatenate([dx_ref[...], dxn_ref[...]], axis=0).astype(BF16)
        for c0 in range(0, D_FF, FFN_CHUNK):
            cols = slice(c0, c0 + FFN_CHUNK)
            vcols = slice(D_FF + c0, D_FF + c0 + FFN_CHUNK)
            dae = _dot_nt(dxe, wd_ref[cols, :])
            da, dan = dae[:tm], dae[tm:]
            gate = up_ref[:, cols]
            val = up_ref[:, vcols]
            gcv = gc_ref[:, cols]
            s = _sigmoid(gcv)
            dab = da.astype(BF16)
            dup_ref[:, vcols] = dab * (gcv * s)
            dgc = dab * val * (s * (1.0 + gcv * (1.0 - s)))
            gcn = gcn_ref[:, cols]
            sn = _sigmoid(gcn)
            dgcn = dan.astype(BF16) * upn_ref[:, vcols] * (sn * (1.0 + gcn * (1.0 - sn)))
            dgcn = jnp.where(last, jnp.zeros_like(dgcn), dgcn)
            up1 = _shift_up(dgc, 1, dgcn)
            up2 = _shift_up(dgc, 2, dgcn)
            w = wfc_ref[:, cols].astype(BF16)
            dup_ref[:, cols] = w[2:3] * dgc + w[1:2] * up1 + w[0:1] * up2
            sg_ref[row:row + 1, cols] += _column_sums(gate * up2)
            sg_ref[row + 1:row + 2, cols] += _column_sums(gate * up1)
            sg_ref[row + 2:row + 3, cols] += _column_sums(gate * dgc)
            sg_ref[row + 3:row + 4, cols] += _column_sums(dgc)

    nxt = lambda i: (jnp.minimum((i + 1) * hb, steps * hb - 1), 0)
    return _call(
        body, name=name, grid=(steps,), carry=carry,
        in_specs=[pl.BlockSpec((tm, D_MODEL), lambda i: (i, 0)),
                  pl.BlockSpec((HALO, D_MODEL), nxt),
                  pl.BlockSpec((tm, 2 * D_FF), lambda i: (i, 0)),
                  pl.BlockSpec((HALO, 2 * D_FF), nxt),
                  pl.BlockSpec((tm, D_FF), lambda i: (i, 0)),
                  pl.BlockSpec((HALO, D_FF), nxt),
                  pl.BlockSpec((3, D_FF), lambda i: (0, 0)),
                  pl.BlockSpec((D_FF, D_MODEL), lambda i: (0, 0)), _sheet_spec()],
        out_specs=[pl.BlockSpec((tm, 2 * D_FF), lambda i: (i, 0)), _sheet_spec()],
        out_shape=[jax.ShapeDtypeStruct((n, 2 * D_FF), BF16), jax.ShapeDtypeStruct((SG_ROWS, SG_W), F32)],
        args=(dx2, dx2, up, up, gc, gc, w_fc, w_down, sheet))


def _matmul_norm_backward(dz, w_t, x, g, dres, sheet, row, *, tm, name, carry=None):
    n, c = dz.shape
    d = x.shape[1]
    ch = MATMUL_CHUNK

    def body(dz_ref, wt_ref, x_ref, g_ref, dres_ref, *rest):
        i = pl.program_id(0)
        if sheet is None:
            dx_ref, sg_ref = rest

            @pl.when(i == 0)
            def _():
                sg_ref[...] = jnp.zeros_like(sg_ref)
        else:
            sg_in, dx_ref, sg_ref = rest
            _sheet_begin(i, sg_in, sg_ref, row, 1)

        dh = _dot(dz_ref[:, 0:ch], wt_ref[0:ch, :])
        for c0 in range(ch, c, ch):
            dh += _dot(dz_ref[:, c0:c0 + ch], wt_ref[c0:c0 + ch, :])
        xv = x_ref[...]
        r = lax.rsqrt(jnp.mean(xv * xv, axis=-1, keepdims=True) + RMS_EPS)
        xh = xv * r
        sg_ref[row:row + 1, 0:d] += jnp.sum(dh * xh, axis=0, keepdims=True)
        dxh = dh * g_ref[...]
        dx_ref[...] = dres_ref[...] + r * (dxh - xh * jnp.mean(dxh * xh, axis=-1, keepdims=True))

    in_specs = [pl.BlockSpec((tm, c), lambda i: (i, 0)),
                pl.BlockSpec((c, d), lambda i: (0, 0)),
                pl.BlockSpec((tm, d), lambda i: (i, 0)),
                pl.BlockSpec((1, d), lambda i: (0, 0)),
                pl.BlockSpec((tm, d), lambda i: (i, 0))]
    args = (dz, w_t, x, g.reshape(1, d), dres)
    if sheet is None:
        small_spec, small_shape = pl.BlockSpec((8, d), lambda i: (0, 0)), jax.ShapeDtypeStruct((8, d), F32)
    else:
        in_specs, args = in_specs + [_sheet_spec()], args + (sheet,)
        small_spec, small_shape = _sheet_spec(), jax.ShapeDtypeStruct((SG_ROWS, SG_W), F32)
    return _call(
        body, name=name, grid=(n // tm,), carry=carry, in_specs=in_specs,
        out_specs=[pl.BlockSpec((tm, d), lambda i: (i, 0)), small_spec],
        out_shape=[jax.ShapeDtypeStruct((n, d), F32), small_shape], args=args)


def _mix_backward(dx1, z, conv, pa, pb, b_gate, ln_g, ln_b, w_s, w_s_t, b_s, w_sc, w_out, wb, sheet, layer, *, tm, name,
                  carry=None):
    n = z.shape[0]
    steps = n // tm
    hb = tm // HALO
    base = SG_LAYER * layer
    r_bg, r_lng, r_lnb, r_sc = base + ROW_BGATE, base + ROW_LN_G, base + ROW_LN_B, base + ROW_SCONV

    def body(dx_ref, dxn_ref, z_ref, zn_ref, cv_ref, pa_ref, pb_ref, bg_ref, lng_ref, lnb_ref, ws_ref, wst_ref,
             bs_ref, wsc_ref, wo_ref, wb_ref, sg_in,
             dz_ref, dpa_ref, dpb_ref, dws_ref, dbs_ref, sg_ref, f_scr, dvn_scr):
        i = pl.program_id(0)
        last = i == steps - 1
        _sheet_begin(i, sg_in, sg_ref, r_bg, ROW_NORM2 - ROW_BGATE)

        @pl.when(i == 0)
        def _():
            dws_ref[...] = jnp.zeros_like(dws_ref)
            dbs_ref[...] = jnp.zeros_like(dbs_ref)

        dxe = jnp.concatenate([dx_ref[...], dxn_ref[...]], axis=0).astype(BF16)
        dmge = _dot_nt(dxe, wo_ref[...])
        dmg, dmgn = dmge[:tm].astype(BF16), dmge[tm:].astype(BF16)

        pa_v = pa_ref[...]
        pb_v = pb_ref[...]
        bg = bg_ref[...].astype(BF16)
        sa = _sigmoid(z_ref[:, OFF_GA:OFF_GA + D_MODEL] + bg[:, 0:D_MODEL])
        sb = _sigmoid(z_ref[:, OFF_GB:OFF_GB + D_MODEL] + bg[:, D_MODEL:2 * D_MODEL])
        dpa = dmg * sa
        dpb = dmg * sb
        dga = dmg * pa_v * sa * (1.0 - sa)
        dgb = dmg * pb_v * sb * (1.0 - sb)
        dpa_ref[...] = dpa
        dpb_ref[...] = dpb
        dz_ref[:, OFF_GA:OFF_GA + D_MODEL] = dga
        dz_ref[:, OFF_GB:OFF_GB + D_MODEL] = dgb
        sg_ref[r_bg:r_bg + 1, 0:D_MODEL] += _column_sums(dga)
        sg_ref[r_bg:r_bg + 1, D_MODEL:2 * D_MODEL] += _column_sums(dgb)

        dya = _dot_nt(dpa, wb_ref[0]).astype(BF16)
        u = z_ref[:, OFF_U:OFF_U + D_A]
        v = z_ref[:, OFF_V:OFF_V + D_A].astype(F32)
        ln_g = lng_ref[...]
        gu, tu, tv, xh, rstd, vn, f = _gmlp_forward(u, v, ln_g, lnb_ref[...], ws_ref, bs_ref, f_scr)
        dgu = dya * f
        df_bf = dya * gu
        dz_ref[:, OFF_U:OFF_U + D_A] = dgu * _gelu_grad(u, tu)
        mask = _spatial_mask(False)
        mask_t = _spatial_mask(True)
        wmt = [jnp.where(mask_t, wst_ref[h], 0.0).astype(BF16) for h in range(N_HEADS)]
        for b in range(tm // GMLP_BLOCK):
            rows = slice(b * GMLP_BLOCK, (b + 1) * GMLP_BLOCK)
            for h in range(N_HEADS):
                cols = slice(h * HEAD, (h + 1) * HEAD)
                dfb = df_bf[rows, cols]
                dvn_scr[rows, cols] = _dot(wmt[h], dfb)
                dws_ref[h] += jnp.where(mask, _dot_nt(dfb, vn[rows, cols]), 0.0)
                dbs_ref[h] += jnp.sum(dfb.astype(F32), axis=1, keepdims=True)
        dvn = dvn_scr[...]
        sg_ref[r_lng:r_lng + 1, 0:D_A] += jnp.sum(dvn * xh, axis=0, keepdims=True)
        sg_ref[r_lnb:r_lnb + 1, 0:D_A] += jnp.sum(dvn, axis=0, keepdims=True)
        dxh = dvn * ln_g
        dgv = rstd * (dxh - jnp.mean(dxh, axis=-1, keepdims=True) - xh * jnp.mean(dxh * xh, axis=-1, keepdims=True))
        dz_ref[:, OFF_V:OFF_V + D_A] = (dgv * _gelu_grad(v, tv)).astype(BF16)

        sbn = _sigmoid(zn_ref[:, OFF_GB:OFF_GB + D_MODEL] + bg[:, D_MODEL:2 * D_MODEL])
        dpbe = jnp.concatenate([dpb, dmgn * sbn], axis=0)
        dybe = _dot_nt(dpbe, wb_ref[1])
        dyb, dybn = dybe[:tm].astype(BF16), dybe[tm:].astype(BF16)
        bgv = z_ref[:, OFF_BG:OFF_BG + D_B]
        cg = z_ref[:, OFF_CG:OFF_CG + D_B]
        hbv = z_ref[:, OFF_HB:OFF_HB + D_B]
        q = cg * hbv
        dz_ref[:, OFF_BG:OFF_BG + D_B] = dyb * cv_ref[...]
        dconv = dyb * bgv
        dconvn = dybn * zn_ref[:, OFF_BG:OFF_BG + D_B]
        dconvn = jnp.where(last, jnp.zeros_like(dconvn), dconvn)
        up1 = _shift_up(dconv, 1, dconvn)
        up2 = _shift_up(dconv, 2, dconvn)
        sg_ref[r_sc:r_sc + 1, 0:D_B] += _column_sums(q * up2)
        sg_ref[r_sc + 1:r_sc + 2, 0:D_B] += _column_sums(q * up1)
        sg_ref[r_sc + 2:r_sc + 3, 0:D_B] += _column_sums(q * dconv)
        w = wsc_ref[...].astype(BF16)
        dq = w[2:3] * dconv + w[1:2] * up1 + w[0:1] * up2
        dz_ref[:, OFF_CG:OFF_CG + D_B] = dq * hbv
        dz_ref[:, OFF_HB:OFF_HB + D_B] = dq * cg

    row = lambda w: pl.BlockSpec((tm, w), lambda i: (i, 0))
    full = lambda *s: pl.BlockSpec(s, lambda i: (0,) * len(s))
    nxt = lambda i: (jnp.minimum((i + 1) * hb, steps * hb - 1), 0)
    return _call(
        body, name=name, grid=(steps,), carry=carry,
        in_specs=[row(D_MODEL), pl.BlockSpec((HALO, D_MODEL), nxt),
                  row(D_IN), pl.BlockSpec((HALO, D_IN), nxt),
                  row(D_B), row(D_MODEL), row(D_MODEL),
                  full(1, 2 * D_MODEL), full(1, D_A), full(1, D_A),
                  full(N_HEADS, GMLP_BLOCK, GMLP_BLOCK), full(N_HEADS, GMLP_BLOCK, GMLP_BLOCK),
                  full(N_HEADS, GMLP_BLOCK, 1), full(3, D_B),
                  full(D_MODEL, D_MODEL), full(2, D_A, D_MODEL), _sheet_spec()],
        out_specs=[row(D_IN), row(D_MODEL), row(D_MODEL), full(N_HEADS, GMLP_BLOCK, GMLP_BLOCK),
                   full(N_HEADS, GMLP_BLOCK, 1), _sheet_spec()],
        out_shape=[jax.ShapeDtypeStruct((n, D_IN), BF16), jax.ShapeDtypeStruct((n, D_MODEL), BF16),
                   jax.ShapeDtypeStruct((n, D_MODEL), BF16),
                   jax.ShapeDtypeStruct((N_HEADS, GMLP_BLOCK, GMLP_BLOCK), F32),
                   jax.ShapeDtypeStruct((N_HEADS, GMLP_BLOCK, 1), F32), jax.ShapeDtypeStruct((SG_ROWS, SG_W), F32)],
        scratch_shapes=[pltpu.VMEM((tm, D_A), BF16), pltpu.VMEM((tm, D_A), F32)],
        args=(dx1, dx1, z, z, conv, pa, pb, b_gate.reshape(1, -1), ln_g.reshape(1, -1), ln_b.reshape(1, -1), w_s, w_s_t,
              b_s.reshape(N_HEADS, GMLP_BLOCK, 1), w_sc, w_out, wb, sheet))


def _matmul_tn(a, b, *, t1, tn, name, carry=None, pieces=1):
    n, k1 = a.shape
    k2 = b.shape[1]
    steps = n // tn
    w = k2 // pieces

    def body(a_ref, b_ref, *rest):
        o_refs, acc = rest[:pieces], rest[pieces]
        s = pl.program_id(1)

        @pl.when(s == 0)
        def _():
            acc[...] = jnp.zeros_like(acc)

        acc[...] += lax.dot_general(a_ref[...].astype(BF16), b_ref[...].astype(BF16), TN, preferred_element_type=F32)

        @pl.when(s == steps - 1)
        def _():
            for c, o_ref in enumerate(o_refs):
                o_ref[...] = acc[:, c * w:(c + 1) * w].astype(BF16)

    outs, carried = _call(
        body, name=name, grid=(k1 // t1, steps), carry=carry,
        in_specs=[pl.BlockSpec((tn, t1), lambda i, s: (s, i)),
                  pl.BlockSpec((tn, k2), lambda i, s: (s, 0))],
        out_specs=[pl.BlockSpec((t1, w), lambda i, s: (i, 0))] * pieces,
        out_shape=[jax.ShapeDtypeStruct((k1, w), BF16)] * pieces,
        scratch_shapes=[pltpu.VMEM((t1, k2), F32)],
        args=(a, b))
    return (outs[0] if pieces == 1 else list(outs)), carried


def _adamw_math(w, g, m, v):
    m = ADAM_B1 * m + (1.0 - ADAM_B1) * g
    v = ADAM_B2 * v + (1.0 - ADAM_B2) * (g * g)
    m_hat = m / (1.0 - ADAM_B1 ** ADAM_STEP)
    v_hat = v / (1.0 - ADAM_B2 ** ADAM_STEP)
    delta = -ADAM_LR * (m_hat / (jnp.sqrt(v_hat) + ADAM_EPS) + ADAM_WD * w)
    return delta, m, v


def _sum_parts(recvs, *, tr, name):
    _, r, c = recvs[0].shape

    def body(*refs):
        recv_refs, g_ref = refs[:DEPTH], refs[DEPTH]
        layer = pl.program_id(0)
        for l in range(DEPTH):
            @pl.when(layer == l)
            def _(l=l):
                g = recv_refs[l][0].astype(F32)
                for s in range(1, N_DEV):
                    g = g + recv_refs[l][s].astype(F32)
                g_ref[0] = g

    outs, _ = _call(
        body, name=name, grid=(DEPTH, r // tr),
        in_specs=[pl.BlockSpec((N_DEV, tr, c), lambda l, i: (0, i, 0))] * DEPTH,
        out_specs=[pl.BlockSpec((1, tr, c), lambda l, i: (l, i, 0))],
        out_shape=[jax.ShapeDtypeStruct((DEPTH, r, c), F32)],
        args=tuple(recvs))
    return outs[0]


def _adamw(w, g, m, v, *, tr, name):
    r, c = w.shape

    def body(w_ref, g_ref, m_ref, v_ref, d_ref, nm_ref, nv_ref):
        delta, nm, nv = _adamw_math(w_ref[...], g_ref[...], m_ref[...], v_ref[...])
        d_ref[...] = delta
        nm_ref[...] = nm
        nv_ref[...] = nv

    spec = pl.BlockSpec((tr, c), lambda i: (i, 0))
    outs, _ = _call(body, name=name, grid=(r // tr,), in_specs=[spec] * 4, out_specs=[spec] * 3,
                    out_shape=[jax.ShapeDtypeStruct((r, c), F32)] * 3, args=(w, g, m, v))
    return outs


def _sum_adamw(recvs, w, m, v, *, tr, name, carry=None):
    _, r, c = w.shape
    blocks = len(recvs[0])
    flat = [piece for layer in recvs for piece in layer]

    def body(*refs):
        recv_refs = refs[:len(flat)]
        w_ref, m_ref, v_ref, g_ref, d_ref, nm_ref, nv_ref = refs[len(flat):]
        layer = pl.program_id(0)
        for l in range(DEPTH):
            @pl.when(layer == l)
            def _(l=l):
                cols = []
                for piece in recv_refs[l * blocks:(l + 1) * blocks]:
                    part = piece[0].astype(F32)
                    for s in range(1, N_DEV):
                        part = part + piece[s].astype(F32)
                    cols.append(part)
                g = cols[0] if blocks == 1 else jnp.concatenate(cols, axis=-1)
                delta, nm, nv = _adamw_math(w_ref[0], g, m_ref[0], v_ref[0])
                g_ref[0] = g
                d_ref[0] = delta
                nm_ref[0] = nm
                nv_ref[0] = nv

    spec = pl.BlockSpec((1, tr, c), lambda l, i: (l, i, 0))
    return _call(
        body, name=name, grid=(DEPTH, r // tr),
        in_specs=[pl.BlockSpec((N_DEV, tr, c // blocks), lambda l, i: (0, i, 0))] * len(flat) + [spec] * 3,
        out_specs=[spec] * 4, out_shape=[jax.ShapeDtypeStruct((DEPTH, r, c), F32)] * 4,
        args=tuple(flat) + (w, m, v), carry=carry)


def _adamw_small(sheet, gain0, extra, params, *, name):
    sheet_rows = dict(norm1_g=ROW_NORM1, b_gate=ROW_BGATE, gmlp_ln_g=ROW_LN_G, gmlp_ln_b=ROW_LN_B, norm2_g=ROW_NORM2,
                      b_ffn_conv=ROW_BFCONV)
    names = list(params)
    extra_names = list(extra)

    def body(*refs):
        sg_ref, gain0_ref, refs = refs[0], refs[1], refs[2:]
        extra_refs, refs = dict(zip(extra_names, refs[:len(extra_names)])), refs[len(extra_names):]
        ins, outs, loss_ref = refs[:3 * len(names)], refs[3 * len(names):-1], refs[-1]
        loss_ref[...] = sg_ref[ROW_LOSS:ROW_LOSS + 1, 0:1]
        for j, key in enumerate(names):
            w_ref, m_ref, v_ref = ins[3 * j:3 * j + 3]
            g_ref, d_ref, nm_ref, nv_ref = outs[4 * j:4 * j + 4]
            if key in extra_refs:
                g_ref[...] = extra_refs[key][...]
            elif key == "final_g":
                g_ref[...] = sg_ref[ROW_FINAL:ROW_FINAL + 1, 0:D_MODEL]
            else:
                width = w_ref.shape[-1]
                for l in range(DEPTH):
                    row = SG_LAYER * l + sheet_rows[key]
                    if key == "norm1_g" and l == 0:
                        g_ref[0:1, :] = gain0_ref[0:1, :]
                    else:
                        g_ref[l:l + 1, :] = sg_ref[row:row + 1, 0:width]
            delta, nm, nv = _adamw_math(w_ref[...], g_ref[...], m_ref[...], v_ref[...])
            d_ref[...] = delta
            nm_ref[...] = nm
            nv_ref[...] = nv

    args = [sheet, gain0] + [extra[k] for k in extra_names] + [t for k in names for t in params[k]]
    vmem = pl.BlockSpec(memory_space=pltpu.VMEM)
    outs = pl.pallas_call(
        body, name=name, in_specs=[vmem] * len(args), out_specs=[vmem] * (4 * len(names) + 1),
        out_shape=[jax.ShapeDtypeStruct(params[k][0].shape, F32) for k in names for _ in range(4)]
        + [jax.ShapeDtypeStruct((1, 1), F32)],
    )(*args)
    return {k: tuple(outs[4 * j:4 * j + 4]) for j, k in enumerate(names)}, outs[-1]


def _rows(gathered):
    return gathered.reshape(N_DEV * gathered.shape[1], gathered.shape[2])


def _parts(full):
    return full.reshape(N_DEV, full.shape[0] // N_DEV, full.shape[1])


def kernel(x, norm1_g, w_in, b_gate, gmlp_ln_g, gmlp_ln_b, w_spatial, b_spatial, w_shortconv, w_branch, w_out, norm2_g, w_ffn_up, w_ffn_conv, b_ffn_conv, w_ffn_down, final_g, loss_target, m_norm1_g, m_w_in, m_b_gate, m_gmlp_ln_g, m_gmlp_ln_b, m_w_spatial, m_b_spatial, m_w_shortconv, m_w_branch, m_w_out, m_norm2_g, m_w_ffn_up, m_w_ffn_conv, m_b_ffn_conv, m_w_ffn_down, m_final_g, v_norm1_g, v_w_in, v_b_gate, v_gmlp_ln_g, v_gmlp_ln_b, v_w_spatial, v_b_spatial, v_w_shortconv, v_w_branch, v_w_out, v_norm2_g, v_w_ffn_up, v_w_ffn_conv, v_b_ffn_conv, v_w_ffn_down, v_final_g):
    n = x.shape[1]
    tm_in, tm, tn = 1024, 512, 2048
    x0 = x.reshape(n, D_MODEL)
    target = loss_target.reshape(n, D_MODEL)
    my_idx = 4 * lax.axis_index("x") + 2 * lax.axis_index("y") + lax.axis_index("c")
    sc_w, fc_w = D_B // N_DEV, D_FF // N_DEV

    sh_in = [w_in[l].T.astype(BF16) for l in range(DEPTH)]
    sh_up = [w_ffn_up[l].T.astype(BF16) for l in range(DEPTH)]
    sh_br = [w_branch[l].astype(BF16) for l in range(DEPTH)]
    sh_out = [w_out[l].astype(BF16) for l in range(DEPTH)]
    sh_down = [w_ffn_down[l].astype(BF16) for l in range(DEPTH)]
    taps = jnp.concatenate([w_shortconv, w_ffn_conv], axis=-1)

    def branch_weights(g):
        return g.transpose(1, 2, 0, 3).reshape(2, D_A, D_MODEL)

    g_in0, g_taps = _gather_now([sh_in[0], taps], name="gather_first")
    w_sc = [g_taps[:, l, :, :sc_w].transpose(1, 0, 2).reshape(3, D_B) for l in range(DEPTH)]
    w_fc = [g_taps[:, l, :, sc_w:].transpose(1, 0, 2).reshape(3, D_FF) for l in range(DEPTH)]
    w_s_t = [w_spatial[l].transpose(0, 2, 1) for l in range(DEPTH)]
    weights = [dict(), dict()]
    weights[0]["in_t"] = _rows(g_in0)
    saved = []
    xc = x0
    for l in range(DEPTH):
        p = weights[l]
        carry = _Gather([sh_br[0], sh_out[0]] if l == 0 else [sh_up[1]])
        (h, z), got = _norm_matmul(xc, norm1_g[l], p["in_t"], tm=tm_in, name=f"fwd_in_{l}", carry=carry)
        if l == 0:
            p["wb"], p["out"] = branch_weights(got[0]), _rows(got[1])
        else:
            p["up_t"] = _rows(got[0])
        carry = _Gather([sh_up[0]]) if l == 0 else None
        (ya, yb, conv, pa, pb, mg, x1), got = _mix_forward(
            z, xc, b_gate[l], gmlp_ln_g[l], gmlp_ln_b[l], w_spatial[l], b_spatial[l], w_sc[l], p["wb"], p["out"],
            tm=tm, name=f"fwd_mix_{l}", carry=carry)
        if l == 0:
            p["up_t"] = _rows(got[0])
        (h2, up), got = _norm_matmul(x1, norm2_g[l], p["up_t"], tm=tm, name=f"fwd_up_{l}", carry=_Gather([sh_down[l]]))
        p["down"] = _rows(got[0])
        carry = _Gather([sh_br[1], sh_out[1], sh_in[1]]) if l == 0 else None
        head = (final_g, target) if l == DEPTH - 1 else None
        outs, got = _ffn_forward(up, x1, w_fc[l], b_ffn_conv[l], p["down"], tm=tm, name=f"fwd_ffn_{l}", carry=carry, head=head)
        if l == 0:
            weights[1]["wb"], weights[1]["out"], weights[1]["in_t"] = branch_weights(got[0]), _rows(got[1]), _rows(got[2])
        gc, a = outs[0], outs[1]
        saved.append(dict(x=xc, h=h, z=z, ya=ya, yb=yb, conv=conv, pa=pa, pb=pb, mg=mg, x1=x1, h2=h2, up=up, gc=gc, a=a))
        xc = outs[2]
    dx, sheet = outs[2], outs[3]

    recv = [dict(), dict()]
    small_dws, small_dbs = [None] * DEPTH, [None] * DEPTH
    pending_in = None
    for l in reversed(range(DEPTH)):
        p, s = weights[l], saved[l]
        carry = _Exchange([pending_in]) if pending_in is not None else None
        (dup, sheet), got = _ffn_backward(dx, s["up"], s["gc"], w_fc[l], p["down"], sheet, l, tm=tm, name=f"bwd_ffn_{l}",
                                          carry=carry)
        if got is not None:
            recv[l + 1]["in_t"] = got[0]
        dw_down, _ = _matmul_tn(s["a"], dx, t1=D_FF // 2, tn=tn, name=f"dw_down_{l}")
        dw_up_t, got = _matmul_tn(dup, s["h2"], t1=2 * D_FF // 4, tn=tn, name=f"dw_up_{l}", pieces=2,
                                  carry=_Exchange([_parts(dw_down)]))
        recv[l]["down"] = got[0]
        (dx1, sheet), got_left = _matmul_norm_backward(
            dup, p["up_t"], s["x1"], norm2_g[l], dx, sheet, SG_LAYER * l + ROW_NORM2, tm=tm, name=f"bwd_up_{l}",
            carry=_Exchange([_parts(dw_up_t[0])]))
        dw_out, _ = _matmul_tn(s["mg"], dx1, t1=D_MODEL, tn=tn, name=f"dw_out_{l}")
        (dz, dpa, dpb, small_dws[l], small_dbs[l], sheet), got_right = _mix_backward(
            dx1, s["z"], s["conv"], s["pa"], s["pb"], b_gate[l], gmlp_ln_g[l], gmlp_ln_b[l], w_spatial[l], w_s_t[l],
            b_spatial[l], w_sc[l], p["out"], p["wb"], sheet, l, tm=tm, name=f"bwd_mix_{l}",
            carry=_Exchange([_parts(dw_up_t[1]), _parts(dw_out)]))
        recv[l]["up_t"], recv[l]["out"] = [got_left[0], got_right[0]], got_right[1]
        dw_bra_t, _ = _matmul_tn(dpa, s["ya"], t1=D_MODEL, tn=tn, name=f"dw_branch_a_{l}")
        dw_brb_t, _ = _matmul_tn(dpb, s["yb"], t1=D_MODEL, tn=tn, name=f"dw_branch_b_{l}")
        carry = _Exchange([_parts(dw_bra_t), _parts(dw_brb_t)])
        if l == 0:
            dbs = jnp.stack([t.reshape(N_HEADS, GMLP_BLOCK) for t in small_dbs]).reshape(DEPTH * N_HEADS, GMLP_BLOCK)
            carry = _Both(carry, _Gather([sheet, small_dws[0], small_dws[1], dbs]))
        dw_in_t, got = _matmul_tn(dz, s["h"], t1=D_IN // 4, tn=tn, name=f"dw_in_{l}", carry=carry)
        recv[l]["bra_t"], recv[l]["brb_t"] = got[:2]
        if l == 0:
            gathered_small = got[2:]
            send_sems, recv_sems, parts_thru, land_thru, gain = _exchange_start(
                _parts(dw_in_t), norm1_g[l].reshape(1, D_MODEL), name="exchange_w_in_0_start")
            (dx0, dg1_first), _ = _matmul_norm_backward(dz, p["in_t"], s["x"], gain.reshape(D_MODEL), dx1, None, 0,
                                                        tm=tm, name=f"bwd_in_{l}")
            recv[0]["in_t"] = _exchange_wait(send_sems, recv_sems, parts_thru, land_thru, dg1_first,
                                             name="exchange_w_in_0_wait")
        else:
            (dx0, sheet), _ = _matmul_norm_backward(dz, p["in_t"], s["x"], norm1_g[l], dx1, sheet,
                                                    SG_LAYER * l + ROW_NORM1, tm=tm, name=f"bwd_in_{l}")
            pending_in = _parts(dw_in_t)
        dx = dx0
    grad_x = dx.reshape(x.shape)

    results = {}
    both = lambda key: [recv[l][key] for l in range(DEPTH)]
    blocks = lambda key: [r if isinstance(r, list) else [r] for r in both(key)]
    swap = lambda t: t.transpose(0, 2, 1)
    for key, slab, (w, m, v), tr in [("w_in", "in_t", (w_in, m_w_in, v_w_in), 192),
                                     ("w_ffn_up", "up_t", (w_ffn_up, m_w_ffn_up, v_w_ffn_up), 176)]:
        outs, _ = _sum_adamw(blocks(slab), swap(w), swap(m), swap(v), tr=tr, name=f"adamw_{key}")
        results[key] = tuple(swap(o) for o in outs)
    g_bra = _sum_parts(both("bra_t"), tr=128, name="sum_w_branch_a").transpose(0, 2, 1)
    g_brb = _sum_parts(both("brb_t"), tr=128, name="sum_w_branch_b").transpose(0, 2, 1)
    g_br = jnp.stack([g_bra, g_brb], axis=1)
    flat = lambda t: t.reshape(-1, t.shape[-1])
    outs = _adamw(flat(w_branch), flat(g_br), flat(m_w_branch), flat(v_w_branch), tr=512, name="adamw_w_branch")
    results["w_branch"] = (g_br,) + tuple(o.reshape(w_branch.shape) for o in outs)
    results["w_out"] = tuple(_sum_adamw(blocks("out"), w_out, m_w_out, v_w_out, tr=128, name="adamw_w_out")[0])
    results["w_ffn_down"] = tuple(_sum_adamw(blocks("down"), w_ffn_down, m_w_ffn_down, v_w_ffn_down, tr=176,
                                             name="adamw_w_ffn_down")[0])

    sheet, dws0, dws1, dbs, gain0 = _sum_gathered(gathered_small, dg1_first, name="sum_small_grads")
    swap_taps = lambda t: t.transpose(1, 0, 2)
    taps = lambda row, width: jnp.stack([sheet[SG_LAYER * l + row:SG_LAYER * l + row + 3, :width] for l in range(DEPTH)], axis=1)
    extra = dict(w_spatial=jnp.stack([dws0, dws1]), b_spatial=dbs.reshape(DEPTH, N_HEADS, GMLP_BLOCK),
                 w_shortconv=lax.dynamic_slice_in_dim(taps(ROW_SCONV, D_B), my_idx * sc_w, sc_w, axis=2),
                 w_ffn_conv=lax.dynamic_slice_in_dim(taps(ROW_FCONV, D_FF), my_idx * fc_w, fc_w, axis=2))
    small_w = dict(norm1_g=(norm1_g, m_norm1_g, v_norm1_g), b_gate=(b_gate, m_b_gate, v_b_gate),
                   gmlp_ln_g=(gmlp_ln_g, m_gmlp_ln_g, v_gmlp_ln_g), gmlp_ln_b=(gmlp_ln_b, m_gmlp_ln_b, v_gmlp_ln_b),
                   w_spatial=(w_spatial, m_w_spatial, v_w_spatial), b_spatial=(b_spatial, m_b_spatial, v_b_spatial),
                   w_shortconv=tuple(swap_taps(t) for t in (w_shortconv, m_w_shortconv, v_w_shortconv)), norm2_g=(norm2_g, m_norm2_g, v_norm2_g),
                   w_ffn_conv=tuple(swap_taps(t) for t in (w_ffn_conv, m_w_ffn_conv, v_w_ffn_conv)), b_ffn_conv=(b_ffn_conv, m_b_ffn_conv, v_b_ffn_conv),
                   final_g=tuple(t.reshape(1, D_MODEL) for t in (final_g, m_final_g, v_final_g)))
    small, loss = _adamw_small(sheet, gain0, extra, small_w, name="adamw_small")
    loss = loss.reshape(())
    results.update(small)
    results["final_g"] = tuple(t.reshape(D_MODEL) for t in results["final_g"])
    for key in ("w_shortconv", "w_ffn_conv"):
        results[key] = tuple(swap_taps(t) for t in results[key])

    names = ["norm1_g", "w_in", "b_gate", "gmlp_ln_g", "gmlp_ln_b", "w_spatial", "b_spatial", "w_shortconv", "w_branch",
             "w_out", "norm2_g", "w_ffn_up", "w_ffn_conv", "b_ffn_conv", "w_ffn_down", "final_g"]
    return (loss, grad_x, *[results[k][0] for k in names], *[results[k][1] for k in names],
            *[results[k][2] for k in names], *[results[k][3] for k in names])
```

```python
import math

import jax
import jax.numpy as jnp
from jax import lax
from jax.experimental import pallas as pl
from jax.experimental.pallas import tpu as pltpu

F32 = jnp.float32
BF16 = jnp.bfloat16

N_DEV = 8
DEPTH = 2
D_MODEL = 1024
D_A = 512
D_B = 512
D_FF = 2816
D_IN = 4608
N_HEADS = 4
HEAD = 128
GMLP_BLOCK = 128
CAUSAL_CHUNK = 64
OFF_U, OFF_V, OFF_BG, OFF_CG, OFF_HB, OFF_GA, OFF_GB = 0, 512, 1024, 1536, 2048, 2560, 3584
RMS_EPS = 1e-6
LN_EPS = 1e-5
ADAM_LR, ADAM_B1, ADAM_B2, ADAM_EPS, ADAM_WD, ADAM_STEP = 0.001, 0.9, 0.999, 1e-08, 0.01, 10

SUBLANES, LANES = 8, 128
MATMUL_CHUNK = 512
HALO = 16
FFN_CHUNK = 256
SG_ROWS, SG_W, SG_LAYER = 40, D_FF, 16
ROW_NORM1, ROW_BGATE, ROW_LN_G, ROW_LN_B, ROW_SCONV, ROW_NORM2, ROW_FCONV, ROW_BFCONV = 0, 1, 2, 3, 4, 7, 8, 11
ROW_FINAL, ROW_LOSS = 32, 33
V7X_VMEM_BYTES = 64 << 20
VMEM_LIMIT = V7X_VMEM_BYTES - (8 << 20)
MESH = pl.DeviceIdType.MESH
GELU_C0 = 0.7978845608028654
GELU_C1 = 0.044715
NT = (((1,), (1,)), ((), ()))
TN = (((0,), (0,)), ((), ()))


def _dot(a, b):
    return jnp.dot(a, b, preferred_element_type=F32)


def _dot_nt(a, b):
    return lax.dot_general(a, b, NT, preferred_element_type=F32)


def _sigmoid(x):
    return 1.0 / (1.0 + jnp.exp(-x))


def _gelu_tanh(x):
    return jnp.tanh(GELU_C0 * (x + GELU_C1 * x * x * x))


def _gelu_grad(x, t):
    return 0.5 * (1.0 + t) + 0.5 * x * (1.0 - t * t) * GELU_C0 * (1.0 + 3.0 * GELU_C1 * x * x)


def _sublane_tile(dtype):
    return SUBLANES * (4 // jnp.dtype(dtype).itemsize)


def _shift_down(a, k, prev):
    p = prev.shape[0]
    r = pltpu.roll(a, k, 0)
    sub = _sublane_tile(a.dtype)
    head = r[0:sub]
    rid = lax.broadcasted_iota(jnp.int32, head.shape, 0)
    for j in range(k):
        head = jnp.where(rid == j, prev[p - k + j:p - k + j + 1, :], head)
    return jnp.concatenate([head, r[sub:]], axis=0)


def _shift_up(a, k, nxt):
    t = a.shape[0]
    r = pltpu.roll(a, t - k, 0)
    sub = _sublane_tile(a.dtype)
    tail = r[t - sub:t]
    rid = lax.broadcasted_iota(jnp.int32, tail.shape, 0)
    for j in range(k):
        tail = jnp.where(rid == sub - k + j, nxt[j:j + 1, :], tail)
    return jnp.concatenate([r[0:t - sub], tail], axis=0)


def _column_sums(p):
    if p.dtype.itemsize < 4:
        t = p.shape[0]
        p = p[:t // 2] + p[t // 2:]
        p = p[:t // 4] + p[t // 4:]
    return jnp.sum(p.astype(F32), axis=0, keepdims=True)


def _sheet_begin(step, sheet_in, sheet_out, first_row, rows):
    @pl.when(step == 0)
    def _():
        sheet_out[...] = sheet_in[...]
        sheet_out[first_row:first_row + rows, :] = jnp.zeros((rows, SG_W), F32)


def _sheet_spec():
    return pl.BlockSpec((SG_ROWS, SG_W), lambda i: (0, 0))


def _spatial_mask(transposed):
    ri = lax.broadcasted_iota(jnp.int32, (GMLP_BLOCK, GMLP_BLOCK), 0) // CAUSAL_CHUNK
    ci = lax.broadcasted_iota(jnp.int32, (GMLP_BLOCK, GMLP_BLOCK), 1) // CAUSAL_CHUNK
    return (ri <= ci) if transposed else (ci <= ri)


def _gmlp_forward(u, v, ln_g, ln_b, ws_ref, bs_ref, f_scr):
    tm = u.shape[0]
    tu = _gelu_tanh(u)
    tv = _gelu_tanh(v)
    gu = 0.5 * u * (1.0 + tu)
    gv = 0.5 * v * (1.0 + tv)
    mu = jnp.mean(gv, axis=-1, keepdims=True)
    cen = gv - mu
    rstd = lax.rsqrt(jnp.mean(cen * cen, axis=-1, keepdims=True) + LN_EPS)
    xh = cen * rstd
    vn = (xh * ln_g + ln_b).astype(BF16)
    mask = _spatial_mask(False)
    wm = [jnp.where(mask, ws_ref[h], 0.0).astype(BF16) for h in range(N_HEADS)]
    for b in range(tm // GMLP_BLOCK):
        rows = slice(b * GMLP_BLOCK, (b + 1) * GMLP_BLOCK)
        for h in range(N_HEADS):
            cols = slice(h * HEAD, (h + 1) * HEAD)
            f_scr[rows, cols] = (_dot(wm[h], vn[rows, cols]) + bs_ref[h]).astype(f_scr.dtype)
    return gu, tu, tv, xh, rstd, vn, f_scr[...]


def _position():
    return lax.axis_index("x"), lax.axis_index("y"), lax.axis_index("c")


def _handshake(peers):
    barrier = pltpu.get_barrier_semaphore()
    for peer in peers:
        pl.semaphore_signal(barrier, inc=1, device_id=peer, device_id_type=MESH)
    pl.semaphore_wait(barrier, len(peers))


class _Gather:
    collective_id = 1

    def __init__(self, arrays):
        self.arrays = list(arrays)
        self.out_shape = [jax.ShapeDtypeStruct((N_DEV,) + a.shape, a.dtype) for a in self.arrays]
        self.base = 0

    def barrier(self):
        x, y, c = _position()
        _handshake([(x, y, 1 - c), (1 - x, y, c), (x, 1 - y, c), (1 - x, 1 - y, c)])

    def _plan(self, ins, outs, sems):
        send_sems, recv_sems, local_sems = sems
        x, y, c = _position()
        me, sibling = (x, y, c), (x, y, 1 - c)
        chips = [(1 - x, y), (x, 1 - y), (1 - x, 1 - y)]

        def slot(a, p):
            return outs[a].at[4 * p[0] + 2 * p[1] + p[2]]

        def copy(a, k, block, to, src=None):
            return pltpu.make_async_remote_copy(
                src_ref=slot(a, block) if src is None else src, dst_ref=slot(a, block),
                send_sem=send_sems.at[self.base + a, k], recv_sem=recv_sems.at[self.base + a, k],
                device_id=to, device_id_type=MESH)

        n = len(self.arrays)

        def mine():
            return [pltpu.make_async_copy(ins[a], slot(a, me), local_sems.at[self.base + a]) for a in range(n)]

        def first():
            out = []
            for a in range(n):
                out.append(copy(a, 0, me, sibling, src=ins[a]))
                out += [copy(a, 1 + j, me, (*chip, c), src=ins[a]) for j, chip in enumerate(chips)]
            return out

        def arrivals():
            return [copy(a, 1 + j, (*chip, c), me) for j, chip in enumerate(chips) for a in range(n)]

        def relays():
            return [copy(a, 4 + j, (*chip, c), sibling) for j, chip in enumerate(chips) for a in range(n)]

        def from_sibling():
            out = [copy(a, 0, sibling, me) for a in range(n)]
            return out + [copy(a, 4 + j, (*chip, 1 - c), me) for j, chip in enumerate(chips) for a in range(n)]

        return mine, first, arrivals, relays, from_sibling

    def start(self, ins, outs, sems):
        mine, first, _, _, _ = self._plan(ins, outs, sems)
        for cp in mine() + first():
            cp.start()

    def relay(self, ins, outs, sems):
        _, _, arrivals, relays, _ = self._plan(ins, outs, sems)
        for arrived, onward in zip(arrivals(), relays()):
            arrived.wait_recv()
            onward.start()

    def finish(self, ins, outs, sems):
        mine, first, _, relays, from_sibling = self._plan(ins, outs, sems)
        for cp in from_sibling():
            cp.wait_recv()
        for cp in first() + relays():
            cp.wait_send()
        for cp in mine():
            cp.wait()


class _Exchange:
    collective_id = 0

    def __init__(self, arrays):
        self.arrays = list(arrays)
        self.out_shape = [jax.ShapeDtypeStruct(a.shape, a.dtype) for a in self.arrays]
        self.base = 0

    def barrier(self):
        x, y, c = _position()
        _handshake([(x ^ dx, y ^ dy, c ^ dc) for dx in (0, 1) for dy in (0, 1) for dc in (0, 1) if dx or dy or dc])

    def _plan(self, ins, outs, sems):
        send_sems, recv_sems, local_sems = sems
        x, y, c = _position()
        my_idx = 4 * x + 2 * y + c
        n = len(self.arrays)
        offsets = [(dx, dy, dc) for dx in (0, 1) for dy in (0, 1) for dc in (0, 1) if (dx, dy, dc) != (0, 0, 0)]

        def mine():
            return [pltpu.make_async_copy(ins[a].at[my_idx], outs[a].at[my_idx], local_sems.at[self.base + a])
                    for a in range(n)]

        def remote(arriving):
            out = []
            for k, (dx, dy, dc) in enumerate(offsets):
                px, py, pc = x ^ dx, y ^ dy, c ^ dc
                p_idx = 4 * px + 2 * py + pc
                for a in range(n):
                    out.append(pltpu.make_async_remote_copy(
                        src_ref=ins[a].at[p_idx], dst_ref=outs[a].at[p_idx if arriving else my_idx],
                        send_sem=send_sems.at[self.base + a, k], recv_sem=recv_sems.at[self.base + a, k],
                        device_id=(px, py, pc), device_id_type=MESH))
            return out

        return mine, remote

    def start(self, ins, outs, sems):
        mine, remote = self._plan(ins, outs, sems)
        for cp in mine() + remote(False):
            cp.start()

    def relay(self, ins, outs, sems):
        pass

    def finish(self, ins, outs, sems):
        mine, remote = self._plan(ins, outs, sems)
        for cp in remote(True):
            cp.wait_recv()
        for cp in remote(False):
            cp.wait_send()
        for cp in mine():
            cp.wait()


def _exchange_start(parts, thru, *, name):
    hbm = pl.BlockSpec(memory_space=pltpu.HBM)
    sem = pl.BlockSpec(memory_space=pltpu.SEMAPHORE)
    offsets = [(dx, dy, dc) for dx in (0, 1) for dy in (0, 1) for dc in (0, 1) if dx or dy or dc]

    def body(parts_ref, thru_ref, send_sems, recv_sems, parts_thru, land_ref, thru_out):
        x, y, c = _position()
        my_idx = 4 * x + 2 * y + c
        pltpu.make_async_copy(parts_ref.at[my_idx], land_ref.at[my_idx], send_sems.at[7]).start()
        _handshake([(x ^ dx, y ^ dy, c ^ dc) for dx, dy, dc in offsets])
        for k, (dx, dy, dc) in enumerate(offsets):
            px, py, pc = x ^ dx, y ^ dy, c ^ dc
            pltpu.make_async_remote_copy(
                src_ref=parts_ref.at[4 * px + 2 * py + pc], dst_ref=land_ref.at[my_idx],
                send_sem=send_sems.at[k], recv_sem=recv_sems.at[k],
                device_id=(px, py, pc), device_id_type=MESH).start()
        thru_out[...] = thru_ref[...]

    return pl.pallas_call(
        body, name=name,
        out_shape=(pltpu.SemaphoreType.DMA((8,)), pltpu.SemaphoreType.DMA((7,)), pltpu.HBM(parts.shape, parts.dtype),
                   pltpu.HBM(parts.shape, parts.dtype), jax.ShapeDtypeStruct(thru.shape, thru.dtype)),
        in_specs=(hbm, pl.BlockSpec(memory_space=pltpu.VMEM)),
        out_specs=(sem, sem, hbm, hbm, pl.BlockSpec(memory_space=pltpu.VMEM)),
        input_output_aliases={0: 2},
        compiler_params=pltpu.CompilerParams(has_side_effects=pltpu.SideEffectType.DATAFLOW_SIDE_EFFECTING,
                                             collective_id=2),
    )(pltpu.with_memory_space_constraint(parts, pltpu.HBM), thru)


def _exchange_wait(send_sems, recv_sems, parts_thru, land_thru, after, *, name):
    hbm = pl.BlockSpec(memory_space=pltpu.HBM)
    sem = pl.BlockSpec(memory_space=pltpu.SEMAPHORE)
    offsets = [(dx, dy, dc) for dx in (0, 1) for dy in (0, 1) for dc in (0, 1) if dx or dy or dc]

    def body(parts_ref, land_ref, send_sems, recv_sems, after_ref, parts_dead, got_ref):
        x, y, c = _position()
        my_idx = 4 * x + 2 * y + c
        pltpu.make_async_copy(parts_ref.at[my_idx], land_ref.at[my_idx], send_sems.at[7]).wait()
        for k, (dx, dy, dc) in enumerate(offsets):
            px, py, pc = x ^ dx, y ^ dy, c ^ dc
            p_idx = 4 * px + 2 * py + pc
            copy = pltpu.make_async_remote_copy(
                src_ref=parts_ref.at[p_idx], dst_ref=land_ref.at[p_idx], send_sem=send_sems.at[k],
                recv_sem=recv_sems.at[k], device_id=(px, py, pc), device_id_type=MESH)
            copy.wait_send()
            copy.wait_recv()

    return pl.pallas_call(
        body, name=name,
        out_shape=(pltpu.HBM(parts_thru.shape, parts_thru.dtype), pltpu.HBM(land_thru.shape, land_thru.dtype)),
        in_specs=(hbm, hbm, sem, sem, pl.BlockSpec(memory_space=pl.ANY)), out_specs=(hbm, hbm),
        input_output_aliases={0: 0, 1: 1},
        compiler_params=pltpu.CompilerParams(has_side_effects=pltpu.SideEffectType.DATAFLOW_SIDE_EFFECTING),
    )(parts_thru, land_thru, send_sems, recv_sems, after)[1]


class _Both:
    def __init__(self, *carries):
        self.carries = carries
        self.arrays = [a for c in carries for a in c.arrays]
        self.out_shape = [s for c in carries for s in c.out_shape]
        first = 0
        for c in carries:
            c.base = first
            first += len(c.arrays)
        self.collective_id = min(c.collective_id for c in carries)

    def barrier(self):
        min(self.carries, key=lambda c: c.collective_id).barrier()

    def _each(self, method, ins, outs, sems):
        for c in self.carries:
            rows = slice(c.base, c.base + len(c.arrays))
            getattr(c, method)(ins[rows], outs[rows], sems)

    def start(self, ins, outs, sems):
        self._each("start", ins, outs, sems)

    def relay(self, ins, outs, sems):
        self._each("relay", ins, outs, sems)

    def finish(self, ins, outs, sems):
        self._each("finish", ins, outs, sems)


def _call(body, *, name, grid, in_specs, out_specs, out_shape, args, scratch_shapes=(), carry=None):
    n_in, n_out, n_scr = len(in_specs), len(out_specs), len(scratch_shapes)
    params = pltpu.CompilerParams(dimension_semantics=("arbitrary",) * len(grid), vmem_limit_bytes=VMEM_LIMIT)
    if carry is None:
        outs = pl.pallas_call(body, name=name, grid=grid, in_specs=in_specs, out_specs=out_specs, out_shape=out_shape,
                              scratch_shapes=list(scratch_shapes), compiler_params=params)(*args)
        return outs, None
    m = len(carry.arrays)
    total = math.prod(grid)

    def wrapped(*refs):
        ins, refs = refs[:n_in], refs[n_in:]
        c_ins, refs = refs[:m], refs[m:]
        outs, refs = refs[:n_out], refs[n_out:]
        c_outs, refs = refs[:m], refs[m:]
        scr, sems = refs[:n_scr], refs[n_scr:]
        flat = pl.program_id(0)
        for d in range(1, len(grid)):
            flat = flat * grid[d] + pl.program_id(d)

        @pl.when(flat == 0)
        def _():
            carry.barrier()
            carry.start(c_ins, c_outs, sems)

        body(*ins, *outs, *scr)

        @pl.when(flat == total - 2)
        def _():
            carry.relay(c_ins, c_outs, sems)

        @pl.when(flat == total - 1)
        def _():
            carry.finish(c_ins, c_outs, sems)

    any_spec = pl.BlockSpec(memory_space=pl.ANY)
    sem_shapes = [pltpu.SemaphoreType.DMA((m, 7)), pltpu.SemaphoreType.DMA((m, 7)), pltpu.SemaphoreType.DMA((m,))]
    params = pltpu.CompilerParams(dimension_semantics=("arbitrary",) * len(grid), vmem_limit_bytes=VMEM_LIMIT,
                                  collective_id=carry.collective_id)
    outs = pl.pallas_call(
        wrapped, name=name, grid=grid,
        in_specs=list(in_specs) + [any_spec] * m, out_specs=list(out_specs) + [any_spec] * m,
        out_shape=list(out_shape) + carry.out_shape,
        scratch_shapes=list(scratch_shapes) + sem_shapes, compiler_params=params)(*args, *carry.arrays)
    return outs[:n_out], outs[n_out:]


def _gather_now(arrays, *, name):
    carry = _Gather(arrays)
    m = len(arrays)

    def body(*refs):
        ins, outs, sems = refs[:m], refs[m:2 * m], refs[2 * m:]
        carry.barrier()
        carry.start(ins, outs, sems)
        carry.relay(ins, outs, sems)
        carry.finish(ins, outs, sems)

    any_spec = pl.BlockSpec(memory_space=pl.ANY)
    return pl.pallas_call(
        body, name=name, in_specs=[any_spec] * m, out_specs=[any_spec] * m, out_shape=carry.out_shape,
        scratch_shapes=[pltpu.SemaphoreType.DMA((m, 7)), pltpu.SemaphoreType.DMA((m, 7)),
                        pltpu.SemaphoreType.DMA((m,))],
        compiler_params=pltpu.CompilerParams(collective_id=carry.collective_id),
    )(*arrays)


def _sum_gathered(arrs, late, *, name):
    n = len(arrs)

    def body(*refs):
        late_ref, ins, late_out, outs = refs[0], refs[1:1 + n], refs[1 + n], refs[2 + n:2 + 2 * n]
        buf, send_sems, recv_sems = refs[2 + 2 * n:]
        x, y, c = _position()
        me, sibling = (x, y, c), (x, y, 1 - c)
        chips = [(1 - x, y), (x, 1 - y), (1 - x, 1 - y)]
        _handshake([sibling] + [(*chip, c) for chip in chips])

        def copy(k, block, to, src=None):
            slot = buf.at[4 * block[0] + 2 * block[1] + block[2]]
            return pltpu.make_async_remote_copy(
                src_ref=slot if src is None else src, dst_ref=slot,
                send_sem=send_sems.at[k], recv_sem=recv_sems.at[k], device_id=to, device_id_type=MESH)

        first = [copy(0, me, sibling, src=late_ref)]
        first += [copy(1 + j, me, (*chip, c), src=late_ref) for j, chip in enumerate(chips)]
        for cp in first:
            cp.start()
        for in_ref, out_ref in zip(ins, outs):
            acc = in_ref[0]
            for s in range(1, N_DEV):
                acc = acc + in_ref[s]
            out_ref[...] = acc
        passed = []
        for j, chip in enumerate(chips):
            copy(1 + j, (*chip, c), me).wait_recv()
            cp = copy(4 + j, (*chip, c), sibling)
            cp.start()
            passed.append(cp)
        copy(0, sibling, me).wait_recv()
        for j, chip in enumerate(chips):
            copy(4 + j, (*chip, 1 - c), me).wait_recv()
        for cp in first + passed:
            cp.wait_send()
        my_idx = 4 * x + 2 * y + c
        acc = jnp.zeros(late_ref.shape, F32)
        for s in range(N_DEV):
            acc = acc + jnp.where(my_idx == s, late_ref[...], buf[s])
        late_out[...] = acc

    vmem = pl.BlockSpec(memory_space=pltpu.VMEM)
    outs = pl.pallas_call(
        body, name=name, in_specs=[vmem] * (n + 1), out_specs=[vmem] * (n + 1),
        out_shape=[jax.ShapeDtypeStruct(late.shape, F32)] + [jax.ShapeDtypeStruct(a.shape[1:], F32) for a in arrs],
        scratch_shapes=[pltpu.VMEM((N_DEV,) + late.shape, F32), pltpu.SemaphoreType.DMA((7,)),
                        pltpu.SemaphoreType.DMA((7,))],
        compiler_params=pltpu.CompilerParams(vmem_limit_bytes=VMEM_LIMIT, collective_id=_Gather.collective_id),
    )(late, *arrs)
    return list(outs[1:]) + [outs[0]]


def _norm_matmul(x, g, w_t, *, tm, name, carry=None):
    n, d = x.shape
    c = w_t.shape[0]
    ch = MATMUL_CHUNK

    def body(x_ref, g_ref, wt_ref, h_ref, z_ref):
        xv = x_ref[...]
        r = lax.rsqrt(jnp.mean(xv * xv, axis=-1, keepdims=True) + RMS_EPS)
        h = (xv * r * g_ref[...]).astype(BF16)
        h_ref[...] = h
        for c0 in range(0, c, ch):
            z_ref[:, c0:c0 + ch] = _dot_nt(h, wt_ref[c0:c0 + ch, :]).astype(BF16)

    return _call(
        body, name=name, grid=(n // tm,), carry=carry,
        in_specs=[pl.BlockSpec((tm, d), lambda i: (i, 0)),
                  pl.BlockSpec((1, d), lambda i: (0, 0)),
                  pl.BlockSpec((c, d), lambda i: (0, 0))],
        out_specs=[pl.BlockSpec((tm, d), lambda i: (i, 0)),
                   pl.BlockSpec((tm, c), lambda i: (i, 0))],
        out_shape=[jax.ShapeDtypeStruct((n, d), BF16), jax.ShapeDtypeStruct((n, c), BF16)],
        args=(x, g.reshape(1, d), w_t))


def _mix_forward(z, x, b_gate, ln_g, ln_b, w_s, b_s, w_sc, wb, w_out, *, tm, name, carry=None):
    n = z.shape[0]
    hb = tm // HALO

    def body(z_ref, zp_ref, x_ref, bg_ref, lng_ref, lnb_ref, ws_ref, bs_ref, wsc_ref, wb_ref, wo_ref,
             ya_ref, yb_ref, cv_ref, pa_ref, pb_ref, mg_ref, x1_ref, f_scr):
        i = pl.program_id(0)
        u = z_ref[:, OFF_U:OFF_U + D_A]
        v = z_ref[:, OFF_V:OFF_V + D_A].astype(F32)
        gu, _, _, _, _, _, f = _gmlp_forward(u, v, lng_ref[...], lnb_ref[...], ws_ref, bs_ref, f_scr)
        ya = gu * f
        ya_ref[...] = ya

        q = z_ref[:, OFF_CG:OFF_CG + D_B] * z_ref[:, OFF_HB:OFF_HB + D_B]
        qp = zp_ref[:, OFF_CG:OFF_CG + D_B] * zp_ref[:, OFF_HB:OFF_HB + D_B]
        qp = jnp.where(i > 0, qp, jnp.zeros_like(qp))
        w = wsc_ref[...].astype(BF16)
        conv = w[0:1] * _shift_down(q, 2, qp) + w[1:2] * _shift_down(q, 1, qp) + w[2:3] * q
        cv_ref[...] = conv
        yb = z_ref[:, OFF_BG:OFF_BG + D_B] * conv
        yb_ref[...] = yb

        pa = _dot(ya, wb_ref[0]).astype(BF16)
        pb = _dot(yb, wb_ref[1]).astype(BF16)
        pa_ref[...] = pa
        pb_ref[...] = pb
        bg = bg_ref[...].astype(BF16)
        sa = _sigmoid(z_ref[:, OFF_GA:OFF_GA + D_MODEL] + bg[:, 0:D_MODEL])
        sb = _sigmoid(z_ref[:, OFF_GB:OFF_GB + D_MODEL] + bg[:, D_MODEL:2 * D_MODEL])
        mg = sa * pa + sb * pb
        mg_ref[...] = mg
        x1_ref[...] = x_ref[...] + _dot(mg, wo_ref[...])

    row = lambda w: pl.BlockSpec((tm, w), lambda i: (i, 0))
    full = lambda *s: pl.BlockSpec(s, lambda i: (0,) * len(s))
    bf = lambda w: jax.ShapeDtypeStruct((n, w), BF16)
    return _call(
        body, name=name, grid=(n // tm,), carry=carry,
        in_specs=[row(D_IN),
                  pl.BlockSpec((HALO, D_IN), lambda i: (jnp.maximum(i * hb - 1, 0), 0)),
                  row(D_MODEL), full(1, 2 * D_MODEL), full(1, D_A), full(1, D_A),
                  full(N_HEADS, GMLP_BLOCK, GMLP_BLOCK), full(N_HEADS, GMLP_BLOCK, 1), full(3, D_B),
                  full(2, D_A, D_MODEL), full(D_MODEL, D_MODEL)],
        out_specs=[row(D_A), row(D_B), row(D_B), row(D_MODEL), row(D_MODEL), row(D_MODEL), row(D_MODEL)],
        out_shape=[bf(D_A), bf(D_B), bf(D_B), bf(D_MODEL), bf(D_MODEL), bf(D_MODEL),
                   jax.ShapeDtypeStruct((n, D_MODEL), F32)],
        scratch_shapes=[pltpu.VMEM((tm, D_A), BF16)],
        args=(z, z, x, b_gate.reshape(1, -1), ln_g.reshape(1, -1), ln_b.reshape(1, -1), w_s,
              b_s.reshape(N_HEADS, GMLP_BLOCK, 1), w_sc, wb, w_out))


def _loss_tile(xv, gv, tv):
    d = xv.shape[-1]
    r = lax.rsqrt(jnp.mean(xv * xv, axis=-1, keepdims=True) + RMS_EPS)
    xh = xv * r
    e = xh * gv - tv
    per_row = jnp.sum(e * e, axis=-1, keepdims=True) * (0.5 / d)
    dy = e * (1.0 / d)
    dxh = dy * gv
    dx = r * (dxh - xh * jnp.mean(dxh * xh, axis=-1, keepdims=True))
    return dx, jnp.sum(per_row, axis=0, keepdims=True), jnp.sum(dy * xh, axis=0, keepdims=True)


def _ffn_forward(up, x1, w_fc, b_fc, w_down, *, tm, name, carry=None, head=None):
    n = up.shape[0]
    hb = tm // HALO
    n_in = 6 if head is None else 8

    def body(*refs):
        up_ref, upp_ref, x1_ref, wfc_ref, bfc_ref, wd_ref = refs[:6]
        gc_ref, a_ref, out_ref = refs[n_in:n_in + 3]
        acc = refs[-1]
        i = pl.program_id(0)
        acc[...] = x1_ref[...]
        for c0 in range(0, D_FF, FFN_CHUNK):
            cols = slice(c0, c0 + FFN_CHUNK)
            gate = up_ref[:, cols]
            val = up_ref[:, D_FF + c0:D_FF + c0 + FFN_CHUNK]
            gp = upp_ref[:, cols]
            gp = jnp.where(i > 0, gp, jnp.zeros_like(gp))
            w = wfc_ref[:, cols].astype(BF16)
            gc = (w[0:1] * _shift_down(gate, 2, gp) + w[1:2] * _shift_down(gate, 1, gp) + w[2:3] * gate
                  + bfc_ref[:, cols].astype(BF16))
            gc_ref[:, cols] = gc
            a = gc * _sigmoid(gc) * val
            a_ref[:, cols] = a
            acc[...] += _dot(a, wd_ref[cols, :])
        if head is None:
            out_ref[...] = acc[...]
        else:
            g_ref, t_ref = refs[6:8]
            sg_ref = refs[n_in + 3]

            @pl.when(i == 0)
            def _():
                sg_ref[...] = jnp.zeros_like(sg_ref)

            dx, loss, dg = _loss_tile(acc[...], g_ref[...], t_ref[...])
            out_ref[...] = dx
            sg_ref[ROW_LOSS:ROW_LOSS + 1, 0:LANES] += jnp.broadcast_to(loss, (1, LANES))
            sg_ref[ROW_FINAL:ROW_FINAL + 1, 0:D_MODEL] += dg

    row = lambda w: pl.BlockSpec((tm, w), lambda i: (i, 0))
    full = lambda r, c: pl.BlockSpec((r, c), lambda i: (0, 0))
    in_specs = [row(2 * D_FF), pl.BlockSpec((HALO, D_FF), lambda i: (jnp.maximum(i * hb - 1, 0), 0)), row(D_MODEL),
                full(3, D_FF), full(1, D_FF), full(D_FF, D_MODEL)]
    out_specs = [row(D_FF), row(D_FF), row(D_MODEL)]
    out_shape = [jax.ShapeDtypeStruct((n, D_FF), BF16), jax.ShapeDtypeStruct((n, D_FF), BF16),
                 jax.ShapeDtypeStruct((n, D_MODEL), F32)]
    args = (up, up, x1, w_fc, b_fc.reshape(1, -1), w_down)
    if head is not None:
        in_specs += [full(1, D_MODEL), row(D_MODEL)]
        out_specs += [full(SG_ROWS, SG_W)]
        out_shape += [jax.ShapeDtypeStruct((SG_ROWS, SG_W), F32)]
        args += (head[0].reshape(1, -1), head[1])
    return _call(body, name=name, grid=(n // tm,), carry=carry, in_specs=in_specs, out_specs=out_specs,
                 out_shape=out_shape, scratch_shapes=[pltpu.VMEM((tm, D_MODEL), F32)], args=args)


def _ffn_backward(dx2, up, gc, w_fc, w_down, sheet, layer, *, tm, name, carry=None):
    n = up.shape[0]
    steps = n // tm
    hb = tm // HALO
    row = SG_LAYER * layer + ROW_FCONV

    def body(dx_ref, dxn_ref, up_ref, upn_ref, gc_ref, gcn_ref, wfc_ref, wd_ref, sg_in, dup_ref, sg_ref):
        i = pl.program_id(0)
        last = i == steps - 1
        _sheet_begin(i, sg_in, sg_ref, row, 4)

        dxe = jnp.concatenate([dx_ref[...], dxn_ref[...]], axis=0).astype(BF16)
        for c0 in range(0, D_FF, FFN_CHUNK):
            cols = slice(c0, c0 + FFN_CHUNK)
            vcols = slice(D_FF + c0, D_FF + c0 + FFN_CHUNK)
            dae = _dot_nt(dxe, wd_ref[cols, :])
            da, dan = dae[:tm], dae[tm:]
            gate = up_ref[:, cols]
            val = up_ref[:, vcols]
            gcv = gc_ref[:, cols]
            s = _sigmoid(gcv)
            dab = da.astype(BF16)
            dup_ref[:, vcols] = dab * (gcv * s)
            dgc = dab * val * (s * (1.0 + gcv * (1.0 - s)))
            gcn = gcn_ref[:, cols]
            sn = _sigmoid(gcn)
            dgcn = dan.astype(BF16) * upn_ref[:, vcols] * (sn * (1.0 + gcn * (1.0 - sn)))
            dgcn = jnp.where(last, jnp.zeros_like(dgcn), dgcn)
            up1 = _shift_up(dgc, 1, dgcn)
            up2 = _shift_up(dgc, 2, dgcn)
            w = wfc_ref[:, cols].astype(BF16)
            dup_ref[:, cols] = w[2:3] * dgc + w[1:2] * up1 + w[0:1] * up2
            sg_ref[row:row + 1, cols] += _column_sums(gate * up2)
            sg_ref[row + 1:row + 2, cols] += _column_sums(gate * up1)
            sg_ref[row + 2:row + 3, cols] += _column_sums(gate * dgc)
            sg_ref[row + 3:row + 4, cols] += _column_sums(dgc)

    nxt = lambda i: (jnp.minimum((i + 1) * hb, steps * hb - 1), 0)
    return _call(
        body, name=name, grid=(steps,), carry=carry,
        in_specs=[pl.BlockSpec((tm, D_MODEL), lambda i: (i, 0)),
                  pl.BlockSpec((HALO, D_MODEL), nxt),
                  pl.BlockSpec((tm, 2 * D_FF), lambda i: (i, 0)),
                  pl.BlockSpec((HALO, 2 * D_FF), nxt),
                  pl.BlockSpec((tm, D_FF), lambda i: (i, 0)),
                  pl.BlockSpec((HALO, D_FF), nxt),
                  pl.BlockSpec((3, D_FF), lambda i: (0, 0)),
                  pl.BlockSpec((D_FF, D_MODEL), lambda i: (0, 0)), _sheet_spec()],
        out_specs=[pl.BlockSpec((tm, 2 * D_FF), lambda i: (i, 0)), _sheet_spec()],
        out_shape=[jax.ShapeDtypeStruct((n, 2 * D_FF), BF16), jax.ShapeDtypeStruct((SG_ROWS, SG_W), F32)],
        args=(dx2, dx2, up, up, gc, gc, w_fc, w_down, sheet))


def _matmul_norm_backward(dz, w_t, x, g, dres, sheet, row, *, tm, name, carry=None):
    n, c = dz.shape
    d = x.shape[1]
    ch = MATMUL_CHUNK

    def body(dz_ref, wt_ref, x_ref, g_ref, dres_ref, *rest):
        i = pl.program_id(0)
        if sheet is None:
            dx_ref, sg_ref = rest

            @pl.when(i == 0)
            def _():
                sg_ref[...] = jnp.zeros_like(sg_ref)
        else:
            sg_in, dx_ref, sg_ref = rest
            _sheet_begin(i, sg_in, sg_ref, row, 1)

        dh = _dot(dz_ref[:, 0:ch], wt_ref[0:ch, :])
        for c0 in range(ch, c, ch):
            dh += _dot(dz_ref[:, c0:c0 + ch], wt_ref[c0:c0 + ch, :])
        xv = x_ref[...]
        r = lax.rsqrt(jnp.mean(xv * xv, axis=-1, keepdims=True) + RMS_EPS)
        xh = xv * r
        sg_ref[row:row + 1, 0:d] += jnp.sum(dh * xh, axis=0, keepdims=True)
        dxh = dh * g_ref[...]
        dx_ref[...] = dres_ref[...] + r * (dxh - xh * jnp.mean(dxh * xh, axis=-1, keepdims=True))

    in_specs = [pl.BlockSpec((tm, c), lambda i: (i, 0)),
                pl.BlockSpec((c, d), lambda i: (0, 0)),
                pl.BlockSpec((tm, d), lambda i: (i, 0)),
                pl.BlockSpec((1, d), lambda i: (0, 0)),
                pl.BlockSpec((tm, d), lambda i: (i, 0))]
    args = (dz, w_t, x, g.reshape(1, d), dres)
    if sheet is None:
        small_spec, small_shape = pl.BlockSpec((8, d), lambda i: (0, 0)), jax.ShapeDtypeStruct((8, d), F32)
    else:
        in_specs, args = in_specs + [_sheet_spec()], args + (sheet,)
        small_spec, small_shape = _sheet_spec(), jax.ShapeDtypeStruct((SG_ROWS, SG_W), F32)
    return _call(
        body, name=name, grid=(n // tm,), carry=carry, in_specs=in_specs,
        out_specs=[pl.BlockSpec((tm, d), lambda i: (i, 0)), small_spec],
        out_shape=[jax.ShapeDtypeStruct((n, d), F32), small_shape], args=args)


def _mix_backward(dx1, z, conv, pa, pb, b_gate, ln_g, ln_b, w_s, w_s_t, b_s, w_sc, w_out, wb, sheet, layer, *, tm, name,
                  carry=None):
    n = z.shape[0]
    steps = n // tm
    hb = tm // HALO
    base = SG_LAYER * layer
    r_bg, r_lng, r_lnb, r_sc = base + ROW_BGATE, base + ROW_LN_G, base + ROW_LN_B, base + ROW_SCONV

    def body(dx_ref, dxn_ref, z_ref, zn_ref, cv_ref, pa_ref, pb_ref, bg_ref, lng_ref, lnb_ref, ws_ref, wst_ref,
             bs_ref, wsc_ref, wo_ref, wb_ref, sg_in,
             dz_ref, dpa_ref, dpb_ref, dws_ref, dbs_ref, sg_ref, f_scr, dvn_scr):
        i = pl.program_id(0)
        last = i == steps - 1
        _sheet_begin(i, sg_in, sg_ref, r_bg, ROW_NORM2 - ROW_BGATE)

        @pl.when(i == 0)
        def _():
            dws_ref[...] = jnp.zeros_like(dws_ref)
            dbs_ref[...] = jnp.zeros_like(dbs_ref)

        dxe = jnp.concatenate([dx_ref[...], dxn_ref[...]], axis=0).astype(BF16)
        dmge = _dot_nt(dxe, wo_ref[...])
        dmg, dmgn = dmge[:tm].astype(BF16), dmge[tm:].astype(BF16)

        pa_v = pa_ref[...]
        pb_v = pb_ref[...]
        bg = bg_ref[...].astype(BF16)
        sa = _sigmoid(z_ref[:, OFF_GA:OFF_GA + D_MODEL] + bg[:, 0:D_MODEL])
        sb = _sigmoid(z_ref[:, OFF_GB:OFF_GB + D_MODEL] + bg[:, D_MODEL:2 * D_MODEL])
        dpa = dmg * sa
        dpb = dmg * sb
        dga = dmg * pa_v * sa * (1.0 - sa)
        dgb = dmg * pb_v * sb * (1.0 - sb)
        dpa_ref[...] = dpa
        dpb_ref[...] = dpb
        dz_ref[:, OFF_GA:OFF_GA + D_MODEL] = dga
        dz_ref[:, OFF_GB:OFF_GB + D_MODEL] = dgb
        sg_ref[r_bg:r_bg + 1, 0:D_MODEL] += _column_sums(dga)
        sg_ref[r_bg:r_bg + 1, D_MODEL:2 * D_MODEL] += _column_sums(dgb)

        dya = _dot_nt(dpa, wb_ref[0]).astype(BF16)
        u = z_ref[:, OFF_U:OFF_U + D_A]
        v = z_ref[:, OFF_V:OFF_V + D_A].astype(F32)
        ln_g = lng_ref[...]
        gu, tu, tv, xh, rstd, vn, f = _gmlp_forward(u, v, ln_g, lnb_ref[...], ws_ref, bs_ref, f_scr)
        dgu = dya * f
        df_bf = dya * gu
        dz_ref[:, OFF_U:OFF_U + D_A] = dgu * _gelu_grad(u, tu)
        mask = _spatial_mask(False)
        mask_t = _spatial_mask(True)
        wmt = [jnp.where(mask_t, wst_ref[h], 0.0).astype(BF16) for h in range(N_HEADS)]
        for b in range(tm // GMLP_BLOCK):
            rows = slice(b * GMLP_BLOCK, (b + 1) * GMLP_BLOCK)
            for h in range(N_HEADS):
                cols = slice(h * HEAD, (h + 1) * HEAD)
                dfb = df_bf[rows, cols]
                dvn_scr[rows, cols] = _dot(wmt[h], dfb)
                dws_ref[h] += jnp.where(mask, _dot_nt(dfb, vn[rows, cols]), 0.0)
                dbs_ref[h] += jnp.sum(dfb.astype(F32), axis=1, keepdims=True)
        dvn = dvn_scr[...]
        sg_ref[r_lng:r_lng + 1, 0:D_A] += jnp.sum(dvn * xh, axis=0, keepdims=True)
        sg_ref[r_lnb:r_lnb + 1, 0:D_A] += jnp.sum(dvn, axis=0, keepdims=True)
        dxh = dvn * ln_g
        dgv = rstd * (dxh - jnp.mean(dxh, axis=-1, keepdims=True) - xh * jnp.mean(dxh * xh, axis=-1, keepdims=True))
        dz_ref[:, OFF_V:OFF_V + D_A] = (dgv * _gelu_grad(v, tv)).astype(BF16)

        sbn = _sigmoid(zn_ref[:, OFF_GB:OFF_GB + D_MODEL] + bg[:, D_MODEL:2 * D_MODEL])
        dpbe = jnp.concatenate([dpb, dmgn * sbn], axis=0)
        dybe = _dot_nt(dpbe, wb_ref[1])
        dyb, dybn = dybe[:tm].astype(BF16), dybe[tm:].astype(BF16)
        bgv = z_ref[:, OFF_BG:OFF_BG + D_B]
        cg = z_ref[:, OFF_CG:OFF_CG + D_B]
        hbv = z_ref[:, OFF_HB:OFF_HB + D_B]
        q = cg * hbv
        dz_ref[:, OFF_BG:OFF_BG + D_B] = dyb * cv_ref[...]
        dconv = dyb * bgv
        dconvn = dybn * zn_ref[:, OFF_BG:OFF_BG + D_B]
        dconvn = jnp.where(last, jnp.zeros_like(dconvn), dconvn)
        up1 = _shift_up(dconv, 1, dconvn)
        up2 = _shift_up(dconv, 2, dconvn)
        sg_ref[r_sc:r_sc + 1, 0:D_B] += _column_sums(q * up2)
        sg_ref[r_sc + 1:r_sc + 2, 0:D_B] += _column_sums(q * up1)
        sg_ref[r_sc + 2:r_sc + 3, 0:D_B] += _column_sums(q * dconv)
        w = wsc_ref[...].astype(BF16)
        dq = w[2:3] * dconv + w[1:2] * up1 + w[0:1] * up2
        dz_ref[:, OFF_CG:OFF_CG + D_B] = dq * hbv
        dz_ref[:, OFF_HB:OFF_HB + D_B] = dq * cg

    row = lambda w: pl.BlockSpec((tm, w), lambda i: (i, 0))
    full = lambda *s: pl.BlockSpec(s, lambda i: (0,) * len(s))
    nxt = lambda i: (jnp.minimum((i + 1) * hb, steps * hb - 1), 0)
    return _call(
        body, name=name, grid=(steps,), carry=carry,
        in_specs=[row(D_MODEL), pl.BlockSpec((HALO, D_MODEL), nxt),
                  row(D_IN), pl.BlockSpec((HALO, D_IN), nxt),
                  row(D_B), row(D_MODEL), row(D_MODEL),
                  full(1, 2 * D_MODEL), full(1, D_A), full(1, D_A),
                  full(N_HEADS, GMLP_BLOCK, GMLP_BLOCK), full(N_HEADS, GMLP_BLOCK, GMLP_BLOCK),
                  full(N_HEADS, GMLP_BLOCK, 1), full(3, D_B),
                  full(D_MODEL, D_MODEL), full(2, D_A, D_MODEL), _sheet_spec()],
        out_specs=[row(D_IN), row(D_MODEL), row(D_MODEL), full(N_HEADS, GMLP_BLOCK, GMLP_BLOCK),
                   full(N_HEADS, GMLP_BLOCK, 1), _sheet_spec()],
        out_shape=[jax.ShapeDtypeStruct((n, D_IN), BF16), jax.ShapeDtypeStruct((n, D_MODEL), BF16),
                   jax.ShapeDtypeStruct((n, D_MODEL), BF16),
                   jax.ShapeDtypeStruct((N_HEADS, GMLP_BLOCK, GMLP_BLOCK), F32),
                   jax.ShapeDtypeStruct((N_HEADS, GMLP_BLOCK, 1), F32), jax.ShapeDtypeStruct((SG_ROWS, SG_W), F32)],
        scratch_shapes=[pltpu.VMEM((tm, D_A), BF16), pltpu.VMEM((tm, D_A), F32)],
        args=(dx1, dx1, z, z, conv, pa, pb, b_gate.reshape(1, -1), ln_g.reshape(1, -1), ln_b.reshape(1, -1), w_s, w_s_t,
              b_s.reshape(N_HEADS, GMLP_BLOCK, 1), w_sc, w_out, wb, sheet))


def _matmul_tn(a, b, *, t1, tn, name, carry=None, pieces=1):
    n, k1 = a.shape
    k2 = b.shape[1]
    steps = n // tn
    w = k2 // pieces

    def body(a_ref, b_ref, *rest):
        o_refs, acc = rest[:pieces], rest[pieces]
        s = pl.program_id(1)

        @pl.when(s == 0)
        def _():
            acc[...] = jnp.zeros_like(acc)

        acc[...] += lax.dot_general(a_ref[...].astype(BF16), b_ref[...].astype(BF16), TN, preferred_element_type=F32)

        @pl.when(s == steps - 1)
        def _():
            for c, o_ref in enumerate(o_refs):
                o_ref[...] = acc[:, c * w:(c + 1) * w].astype(BF16)

    outs, carried = _call(
        body, name=name, grid=(k1 // t1, steps), carry=carry,
        in_specs=[pl.BlockSpec((tn, t1), lambda i, s: (s, i)),
                  pl.BlockSpec((tn, k2), lambda i, s: (s, 0))],
        out_specs=[pl.BlockSpec((t1, w), lambda i, s: (i, 0))] * pieces,
        out_shape=[jax.ShapeDtypeStruct((k1, w), BF16)] * pieces,
        scratch_shapes=[pltpu.VMEM((t1, k2), F32)],
        args=(a, b))
    return (outs[0] if pieces == 1 else list(outs)), carried


def _adamw_math(w, g, m, v):
    m = ADAM_B1 * m + (1.0 - ADAM_B1) * g
    v = ADAM_B2 * v + (1.0 - ADAM_B2) * (g * g)
    m_hat = m / (1.0 - ADAM_B1 ** ADAM_STEP)
    v_hat = v / (1.0 - ADAM_B2 ** ADAM_STEP)
    delta = -ADAM_LR * (m_hat / (jnp.sqrt(v_hat) + ADAM_EPS) + ADAM_WD * w)
    return delta, m, v


def _sum_parts(recvs, *, tr, name):
    _, r, c = recvs[0].shape
    k = len(recvs)

    def body(*refs):
        recv_refs, g_ref = refs[:k], refs[k]
        layer = pl.program_id(0)
        for l in range(k):
            @pl.when(layer == l)
            def _(l=l):
                g = recv_refs[l][0].astype(F32)
                for s in range(1, N_DEV):
                    g = g + recv_refs[l][s].astype(F32)
                g_ref[0] = g

    outs, _ = _call(
        body, name=name, grid=(k, r // tr),
        in_specs=[pl.BlockSpec((N_DEV, tr, c), lambda l, i: (0, i, 0))] * k,
        out_specs=[pl.BlockSpec((1, tr, c), lambda l, i: (l, i, 0))],
        out_shape=[jax.ShapeDtypeStruct((k, r, c), F32)],
        args=tuple(recvs))
    return outs[0]


def _adamw(w, g, m, v, *, tr, name):
    r, c = w.shape

    def body(w_ref, g_ref, m_ref, v_ref, d_ref, nm_ref, nv_ref):
        delta, nm, nv = _adamw_math(w_ref[...], g_ref[...], m_ref[...], v_ref[...])
        d_ref[...] = delta
        nm_ref[...] = nm
        nv_ref[...] = nv

    spec = pl.BlockSpec((tr, c), lambda i: (i, 0))
    outs, _ = _call(body, name=name, grid=(r // tr,), in_specs=[spec] * 4, out_specs=[spec] * 3,
                    out_shape=[jax.ShapeDtypeStruct((r, c), F32)] * 3, args=(w, g, m, v))
    return outs


def _sum_adamw(recvs, w, m, v, *, tr, name, carry=None):
    _, r, c = w.shape
    blocks = len(recvs[0])
    flat = [piece for layer in recvs for piece in layer]

    def body(*refs):
        recv_refs = refs[:len(flat)]
        w_ref, m_ref, v_ref, g_ref, d_ref, nm_ref, nv_ref = refs[len(flat):]
        layer = pl.program_id(0)
        for l in range(DEPTH):
            @pl.when(layer == l)
            def _(l=l):
                cols = []
                for piece in recv_refs[l * blocks:(l + 1) * blocks]:
                    part = piece[0].astype(F32)
                    for s in range(1, N_DEV):
                        part = part + piece[s].astype(F32)
                    cols.append(part)
                g = cols[0] if blocks == 1 else jnp.concatenate(cols, axis=-1)
                delta, nm, nv = _adamw_math(w_ref[0], g, m_ref[0], v_ref[0])
                g_ref[0] = g
                d_ref[0] = delta
                nm_ref[0] = nm
                nv_ref[0] = nv

    spec = pl.BlockSpec((1, tr, c), lambda l, i: (l, i, 0))
    return _call(
        body, name=name, grid=(DEPTH, r // tr),
        in_specs=[pl.BlockSpec((N_DEV, tr, c // blocks), lambda l, i: (0, i, 0))] * len(flat) + [spec] * 3,
        out_specs=[spec] * 4, out_shape=[jax.ShapeDtypeStruct((DEPTH, r, c), F32)] * 4,
        args=tuple(flat) + (w, m, v), carry=carry)


def _adamw_small(sheet, gain0, extra, params, *, name):
    sheet_rows = dict(norm1_g=ROW_NORM1, b_gate=ROW_BGATE, gmlp_ln_g=ROW_LN_G, gmlp_ln_b=ROW_LN_B, norm2_g=ROW_NORM2,
                      b_ffn_conv=ROW_BFCONV)
    names = list(params)
    extra_names = list(extra)

    def body(*refs):
        sg_ref, gain0_ref, refs = refs[0], refs[1], refs[2:]
        extra_refs, refs = dict(zip(extra_names, refs[:len(extra_names)])), refs[len(extra_names):]
        ins, outs, loss_ref = refs[:3 * len(names)], refs[3 * len(names):-1], refs[-1]
        loss_ref[...] = sg_ref[ROW_LOSS:ROW_LOSS + 1, 0:1]
        for j, key in enumerate(names):
            w_ref, m_ref, v_ref = ins[3 * j:3 * j + 3]
            g_ref, d_ref, nm_ref, nv_ref = outs[4 * j:4 * j + 4]
            if key in extra_refs:
                g_ref[...] = extra_refs[key][...]
            elif key == "final_g":
                g_ref[...] = sg_ref[ROW_FINAL:ROW_FINAL + 1, 0:D_MODEL]
            else:
                width = w_ref.shape[-1]
                for l in range(DEPTH):
                    row = SG_LAYER * l + sheet_rows[key]
                    if key == "norm1_g" and l == 0:
                        g_ref[0:1, :] = gain0_ref[0:1, :]
                    else:
                        g_ref[l:l + 1, :] = sg_ref[row:row + 1, 0:width]
            delta, nm, nv = _adamw_math(w_ref[...], g_ref[...], m_ref[...], v_ref[...])
            d_ref[...] = delta
            nm_ref[...] = nm
            nv_ref[...] = nv

    args = [sheet, gain0] + [extra[k] for k in extra_names] + [t for k in names for t in params[k]]
    vmem = pl.BlockSpec(memory_space=pltpu.VMEM)
    outs = pl.pallas_call(
        body, name=name, in_specs=[vmem] * len(args), out_specs=[vmem] * (4 * len(names) + 1),
        out_shape=[jax.ShapeDtypeStruct(params[k][0].shape, F32) for k in names for _ in range(4)]
        + [jax.ShapeDtypeStruct((1, 1), F32)],
    )(*args)
    return {k: tuple(outs[4 * j:4 * j + 4]) for j, k in enumerate(names)}, outs[-1]


def _rows(gathered):
    return gathered.reshape(N_DEV * gathered.shape[1], gathered.shape[2])


def _parts(full):
    return full.reshape(N_DEV, full.shape[0] // N_DEV, full.shape[1])


def kernel(x, norm1_g, w_in, b_gate, gmlp_ln_g, gmlp_ln_b, w_spatial, b_spatial, w_shortconv, w_branch, w_out, norm2_g, w_ffn_up, w_ffn_conv, b_ffn_conv, w_ffn_down, final_g, loss_target, m_norm1_g, m_w_in, m_b_gate, m_gmlp_ln_g, m_gmlp_ln_b, m_w_spatial, m_b_spatial, m_w_shortconv, m_w_branch, m_w_out, m_norm2_g, m_w_ffn_up, m_w_ffn_conv, m_b_ffn_conv, m_w_ffn_down, m_final_g, v_norm1_g, v_w_in, v_b_gate, v_gmlp_ln_g, v_gmlp_ln_b, v_w_spatial, v_b_spatial, v_w_shortconv, v_w_branch, v_w_out, v_norm2_g, v_w_ffn_up, v_w_ffn_conv, v_b_ffn_conv, v_w_ffn_down, v_final_g):
    n = x.shape[1]
    tm_in, tm, tn = 1024, 512, 2048
    x0 = x.reshape(n, D_MODEL)
    target = loss_target.reshape(n, D_MODEL)
    my_idx = 4 * lax.axis_index("x") + 2 * lax.axis_index("y") + lax.axis_index("c")
    sc_w, fc_w = D_B // N_DEV, D_FF // N_DEV

    sh_in = [w_in[l].T.astype(BF16) for l in range(DEPTH)]
    sh_up = [w_ffn_up[l].T.astype(BF16) for l in range(DEPTH)]
    sh_br = [w_branch[l].astype(BF16) for l in range(DEPTH)]
    sh_out = [w_out[l].astype(BF16) for l in range(DEPTH)]
    sh_down = [w_ffn_down[l].astype(BF16) for l in range(DEPTH)]
    taps = jnp.concatenate([w_shortconv, w_ffn_conv], axis=-1)

    def branch_weights(g):
        return g.transpose(1, 2, 0, 3).reshape(2, D_A, D_MODEL)

    g_in0, g_taps = _gather_now([sh_in[0], taps], name="gather_first")
    w_sc = [g_taps[:, l, :, :sc_w].transpose(1, 0, 2).reshape(3, D_B) for l in range(DEPTH)]
    w_fc = [g_taps[:, l, :, sc_w:].transpose(1, 0, 2).reshape(3, D_FF) for l in range(DEPTH)]
    w_s_t = [w_spatial[l].transpose(0, 2, 1) for l in range(DEPTH)]
    weights = [dict(), dict()]
    weights[0]["in_t"] = _rows(g_in0)
    saved = []
    xc = x0
    for l in range(DEPTH):
        p = weights[l]
        carry = _Gather([sh_br[0], sh_out[0]] if l == 0 else [sh_up[1]])
        (h, z), got = _norm_matmul(xc, norm1_g[l], p["in_t"], tm=tm_in, name=f"fwd_in_{l}", carry=carry)
        if l == 0:
            p["wb"], p["out"] = branch_weights(got[0]), _rows(got[1])
        else:
            p["up_t"] = _rows(got[0])
        carry = _Gather([sh_up[0]]) if l == 0 else None
        (ya, yb, conv, pa, pb, mg, x1), got = _mix_forward(
            z, xc, b_gate[l], gmlp_ln_g[l], gmlp_ln_b[l], w_spatial[l], b_spatial[l], w_sc[l], p["wb"], p["out"],
            tm=tm, name=f"fwd_mix_{l}", carry=carry)
        if l == 0:
            p["up_t"] = _rows(got[0])
        (h2, up), got = _norm_matmul(x1, norm2_g[l], p["up_t"], tm=tm, name=f"fwd_up_{l}", carry=_Gather([sh_down[l]]))
        p["down"] = _rows(got[0])
        carry = _Gather([sh_br[1], sh_out[1], sh_in[1]]) if l == 0 else None
        head = (final_g, target) if l == DEPTH - 1 else None
        outs, got = _ffn_forward(up, x1, w_fc[l], b_ffn_conv[l], p["down"], tm=tm, name=f"fwd_ffn_{l}", carry=carry, head=head)
        if l == 0:
            weights[1]["wb"], weights[1]["out"], weights[1]["in_t"] = branch_weights(got[0]), _rows(got[1]), _rows(got[2])
        gc, a = outs[0], outs[1]
        saved.append(dict(x=xc, h=h, z=z, ya=ya, yb=yb, conv=conv, pa=pa, pb=pb, mg=mg, x1=x1, h2=h2, up=up, gc=gc, a=a))
        xc = outs[2]
    dx, sheet = outs[2], outs[3]

    recv = [dict(), dict()]
    small_dws, small_dbs = [None] * DEPTH, [None] * DEPTH
    pending_in = None
    for l in reversed(range(DEPTH)):
        p, s = weights[l], saved[l]
        carry = _Exchange([pending_in]) if pending_in is not None else None
        (dup, sheet), got = _ffn_backward(dx, s["up"], s["gc"], w_fc[l], p["down"], sheet, l, tm=tm, name=f"bwd_ffn_{l}",
                                          carry=carry)
        if got is not None:
            recv[l + 1]["in_t"] = got[0]
        dw_down, _ = _matmul_tn(s["a"], dx, t1=D_FF // 2, tn=tn, name=f"dw_down_{l}")
        dw_up_t, got = _matmul_tn(dup, s["h2"], t1=2 * D_FF // 4, tn=tn, name=f"dw_up_{l}", pieces=2,
                                  carry=_Exchange([_parts(dw_down)]))
        recv[l]["down"] = got[0]
        (dx1, sheet), got_left = _matmul_norm_backward(
            dup, p["up_t"], s["x1"], norm2_g[l], dx, sheet, SG_LAYER * l + ROW_NORM2, tm=tm, name=f"bwd_up_{l}",
            carry=_Exchange([_parts(dw_up_t[0])]))
        dw_out, _ = _matmul_tn(s["mg"], dx1, t1=D_MODEL, tn=tn, name=f"dw_out_{l}")
        (dz, dpa, dpb, small_dws[l], small_dbs[l], sheet), got_right = _mix_backward(
            dx1, s["z"], s["conv"], s["pa"], s["pb"], b_gate[l], gmlp_ln_g[l], gmlp_ln_b[l], w_spatial[l], w_s_t[l],
            b_spatial[l], w_sc[l], p["out"], p["wb"], sheet, l, tm=tm, name=f"bwd_mix_{l}",
            carry=_Exchange([_parts(dw_up_t[1]), _parts(dw_out)]))
        recv[l]["up_t"], recv[l]["out"] = [got_left[0], got_right[0]], got_right[1]
        dw_bra_t, _ = _matmul_tn(dpa, s["ya"], t1=D_MODEL, tn=tn, name=f"dw_branch_a_{l}")
        dw_brb_t, _ = _matmul_tn(dpb, s["yb"], t1=D_MODEL, tn=tn, name=f"dw_branch_b_{l}")
        carry = _Exchange([_parts(dw_bra_t), _parts(dw_brb_t)])
        if l == 0:
            dbs = jnp.stack([t.reshape(N_HEADS, GMLP_BLOCK) for t in small_dbs]).reshape(DEPTH * N_HEADS, GMLP_BLOCK)
            carry = _Both(carry, _Gather([sheet, small_dws[0], small_dws[1], dbs]))
        dw_in_t, got = _matmul_tn(dz, s["h"], t1=D_IN // 4, tn=tn, name=f"dw_in_{l}", carry=carry)
        recv[l]["bra_t"], recv[l]["brb_t"] = got[:2]
        if l == 0:
            gathered_small = got[2:]
            send_sems, recv_sems, parts_thru, land_thru, gain = _exchange_start(
                _parts(dw_in_t), norm1_g[l].reshape(1, D_MODEL), name="exchange_w_in_0_start")
            (dx0, dg1_first), _ = _matmul_norm_backward(dz, p["in_t"], s["x"], gain.reshape(D_MODEL), dx1, None, 0,
                                                        tm=tm, name=f"bwd_in_{l}")
            recv[0]["in_t"] = _exchange_wait(send_sems, recv_sems, parts_thru, land_thru, dg1_first,
                                             name="exchange_w_in_0_wait")
        else:
            (dx0, sheet), _ = _matmul_norm_backward(dz, p["in_t"], s["x"], norm1_g[l], dx1, sheet,
                                                    SG_LAYER * l + ROW_NORM1, tm=tm, name=f"bwd_in_{l}")
            pending_in = _parts(dw_in_t)
        dx = dx0
    grad_x = dx.reshape(x.shape)

    results = {}
    both = lambda key: [recv[l][key] for l in range(DEPTH)]
    blocks = lambda key: [r if isinstance(r, list) else [r] for r in both(key)]
    swap = lambda t: t.transpose(0, 2, 1)
    for key, slab, (w, m, v), tr in [("w_in", "in_t", (w_in, m_w_in, v_w_in), 192),
                                     ("w_ffn_up", "up_t", (w_ffn_up, m_w_ffn_up, v_w_ffn_up), 176)]:
        outs, _ = _sum_adamw(blocks(slab), swap(w), swap(m), swap(v), tr=tr, name=f"adamw_{key}")
        results[key] = tuple(swap(o) for o in outs)
    g_br_t = _sum_parts([recv[l][key] for l in range(DEPTH) for key in ("bra_t", "brb_t")], tr=128, name="sum_w_branch")
    g_br = g_br_t.reshape(DEPTH, 2, D_MODEL // N_DEV, D_A).transpose(0, 1, 3, 2)
    flat = lambda t: t.reshape(-1, t.shape[-1])
    outs = _adamw(flat(w_branch), flat(g_br), flat(m_w_branch), flat(v_w_branch), tr=512, name="adamw_w_branch")
    results["w_branch"] = (g_br,) + tuple(o.reshape(w_branch.shape) for o in outs)
    results["w_out"] = tuple(_sum_adamw(blocks("out"), w_out, m_w_out, v_w_out, tr=128, name="adamw_w_out")[0])
    results["w_ffn_down"] = tuple(_sum_adamw(blocks("down"), w_ffn_down, m_w_ffn_down, v_w_ffn_down, tr=176,
                                             name="adamw_w_ffn_down")[0])

    sheet, dws0, dws1, dbs, gain0 = _sum_gathered(gathered_small, dg1_first, name="sum_small_grads")
    swap_taps = lambda t: t.transpose(1, 0, 2)
    taps = lambda row, width: jnp.stack([sheet[SG_LAYER * l + row:SG_LAYER * l + row + 3, :width] for l in range(DEPTH)], axis=1)
    extra = dict(w_spatial=jnp.stack([dws0, dws1]), b_spatial=dbs.reshape(DEPTH, N_HEADS, GMLP_BLOCK),
                 w_shortconv=lax.dynamic_slice_in_dim(taps(ROW_SCONV, D_B), my_idx * sc_w, sc_w, axis=2),
                 w_ffn_conv=lax.dynamic_slice_in_dim(taps(ROW_FCONV, D_FF), my_idx * fc_w, fc_w, axis=2))
    small_w = dict(norm1_g=(norm1_g, m_norm1_g, v_norm1_g), b_gate=(b_gate, m_b_gate, v_b_gate),
                   gmlp_ln_g=(gmlp_ln_g, m_gmlp_ln_g, v_gmlp_ln_g), gmlp_ln_b=(gmlp_ln_b, m_gmlp_ln_b, v_gmlp_ln_b),
                   w_spatial=(w_spatial, m_w_spatial, v_w_spatial), b_spatial=(b_spatial, m_b_spatial, v_b_spatial),
                   w_shortconv=tuple(swap_taps(t) for t in (w_shortconv, m_w_shortconv, v_w_shortconv)), norm2_g=(norm2_g, m_norm2_g, v_norm2_g),
                   w_ffn_conv=tuple(swap_taps(t) for t in (w_ffn_conv, m_w_ffn_conv, v_w_ffn_conv)), b_ffn_conv=(b_ffn_conv, m_b_ffn_conv, v_b_ffn_conv),
                   final_g=tuple(t.reshape(1, D_MODEL) for t in (final_g, m_final_g, v_final_g)))
    small, loss = _adamw_small(sheet, gain0, extra, small_w, name="adamw_small")
    loss = loss.reshape(())
    results.update(small)
    results["final_g"] = tuple(t.reshape(D_MODEL) for t in results["final_g"])
    for key in ("w_shortconv", "w_ffn_conv"):
        results[key] = tuple(swap_taps(t) for t in results[key])

    names = ["norm1_g", "w_in", "b_gate", "gmlp_ln_g", "gmlp_ln_b", "w_spatial", "b_spatial", "w_shortconv", "w_branch",
             "w_out", "norm2_g", "w_ffn_up", "w_ffn_conv", "b_ffn_conv", "w_ffn_down", "final_g"]
    return (loss, grad_x, *[results[k][0] for k in names], *[results[k][1] for k in names],
            *[results[k][2] for k in names], *[results[k][3] for k in names])
```

```python
import math

import jax
import jax.numpy as jnp
from jax import lax
from jax.experimental import pallas as pl
from jax.experimental.pallas import tpu as pltpu

F32 = jnp.float32
BF16 = jnp.bfloat16

N_DEV = 8
DEPTH = 2
D_MODEL = 1024
D_A = 512
D_B = 512
D_FF = 2816
D_IN = 4608
N_HEADS = 4
HEAD = 128
GMLP_BLOCK = 128
CAUSAL_CHUNK = 64
OFF_U, OFF_V, OFF_BG, OFF_CG, OFF_HB, OFF_GA, OFF_GB = 0, 512, 1024, 1536, 2048, 2560, 3584
RMS_EPS = 1e-6
LN_EPS = 1e-5
ADAM_LR, ADAM_B1, ADAM_B2, ADAM_EPS, ADAM_WD, ADAM_STEP = 0.001, 0.9, 0.999, 1e-08, 0.01, 10

SUBLANES, LANES = 8, 128
MATMUL_CHUNK = 512
HALO = 16
FFN_CHUNK = 256
SG_ROWS, SG_W, SG_LAYER = 40, D_FF, 16
ROW_NORM1, ROW_BGATE, ROW_LN_G, ROW_LN_B, ROW_SCONV, ROW_NORM2, ROW_FCONV, ROW_BFCONV = 0, 1, 2, 3, 4, 7, 8, 11
ROW_FINAL, ROW_LOSS = 32, 33
V7X_VMEM_BYTES = 64 << 20
VMEM_LIMIT = V7X_VMEM_BYTES - (8 << 20)
MESH = pl.DeviceIdType.MESH
GELU_C0 = 0.7978845608028654
GELU_C1 = 0.044715
NT = (((1,), (1,)), ((), ()))
TN = (((0,), (0,)), ((), ()))


def _dot(a, b):
    return jnp.dot(a, b, preferred_element_type=F32)


def _dot_nt(a, b):
    return lax.dot_general(a, b, NT, preferred_element_type=F32)


def _sigmoid(x):
    return 1.0 / (1.0 + jnp.exp(-x))


def _gelu_tanh(x):
    return jnp.tanh(GELU_C0 * (x + GELU_C1 * x * x * x))


def _gelu_grad(x, t):
    return 0.5 * (1.0 + t) + 0.5 * x * (1.0 - t * t) * GELU_C0 * (1.0 + 3.0 * GELU_C1 * x * x)


def _sublane_tile(dtype):
    return SUBLANES * (4 // jnp.dtype(dtype).itemsize)


def _shift_down(a, k, prev):
    p = prev.shape[0]
    r = pltpu.roll(a, k, 0)
    sub = _sublane_tile(a.dtype)
    head = r[0:sub]
    rid = lax.broadcasted_iota(jnp.int32, head.shape, 0)
    for j in range(k):
        head = jnp.where(rid == j, prev[p - k + j:p - k + j + 1, :], head)
    return jnp.concatenate([head, r[sub:]], axis=0)


def _shift_up(a, k, nxt):
    t = a.shape[0]
    r = pltpu.roll(a, t - k, 0)
    sub = _sublane_tile(a.dtype)
    tail = r[t - sub:t]
    rid = lax.broadcasted_iota(jnp.int32, tail.shape, 0)
    for j in range(k):
        tail = jnp.where(rid == sub - k + j, nxt[j:j + 1, :], tail)
    return jnp.concatenate([r[0:t - sub], tail], axis=0)


def _column_sums(p):
    if p.dtype.itemsize < 4:
        t = p.shape[0]
        p = p[:t // 2] + p[t // 2:]
        p = p[:t // 4] + p[t // 4:]
    return jnp.sum(p.astype(F32), axis=0, keepdims=True)


def _sheet_begin(step, sheet_in, sheet_out, first_row, rows):
    @pl.when(step == 0)
    def _():
        sheet_out[...] = sheet_in[...]
        sheet_out[first_row:first_row + rows, :] = jnp.zeros((rows, SG_W), F32)


def _sheet_spec():
    return pl.BlockSpec((SG_ROWS, SG_W), lambda i: (0, 0))


def _spatial_mask(transposed):
    ri = lax.broadcasted_iota(jnp.int32, (GMLP_BLOCK, GMLP_BLOCK), 0) // CAUSAL_CHUNK
    ci = lax.broadcasted_iota(jnp.int32, (GMLP_BLOCK, GMLP_BLOCK), 1) // CAUSAL_CHUNK
    return (ri <= ci) if transposed else (ci <= ri)


def _gmlp_forward(u, v, ln_g, ln_b, ws_ref, bs_ref, f_scr):
    tm = u.shape[0]
    tu = _gelu_tanh(u)
    tv = _gelu_tanh(v)
    gu = 0.5 * u * (1.0 + tu)
    gv = 0.5 * v * (1.0 + tv)
    mu = jnp.mean(gv, axis=-1, keepdims=True)
    cen = gv - mu
    rstd = lax.rsqrt(jnp.mean(cen * cen, axis=-1, keepdims=True) + LN_EPS)
    xh = cen * rstd
    vn = (xh * ln_g + ln_b).astype(BF16)
    mask = _spatial_mask(False)
    wm = [jnp.where(mask, ws_ref[h], 0.0).astype(BF16) for h in range(N_HEADS)]
    for b in range(tm // GMLP_BLOCK):
        rows = slice(b * GMLP_BLOCK, (b + 1) * GMLP_BLOCK)
        for h in range(N_HEADS):
            cols = slice(h * HEAD, (h + 1) * HEAD)
            f_scr[rows, cols] = (_dot(wm[h], vn[rows, cols]) + bs_ref[h]).astype(f_scr.dtype)
    return gu, tu, tv, xh, rstd, vn, f_scr[...]


def _position():
    return lax.axis_index("x"), lax.axis_index("y"), lax.axis_index("c")


def _handshake(peers):
    barrier = pltpu.get_barrier_semaphore()
    for peer in peers:
        pl.semaphore_signal(barrier, inc=1, device_id=peer, device_id_type=MESH)
    pl.semaphore_wait(barrier, len(peers))


class _Gather:
    collective_id = 1

    def __init__(self, arrays):
        self.arrays = list(arrays)
        self.out_shape = [jax.ShapeDtypeStruct((N_DEV,) + a.shape, a.dtype) for a in self.arrays]
        self.base = 0

    def barrier(self):
        x, y, c = _position()
        _handshake([(x, y, 1 - c), (1 - x, y, c), (x, 1 - y, c), (1 - x, 1 - y, c)])

    def _plan(self, ins, outs, sems):
        send_sems, recv_sems, local_sems = sems
        x, y, c = _position()
        me, sibling = (x, y, c), (x, y, 1 - c)
        chips = [(1 - x, y), (x, 1 - y), (1 - x, 1 - y)]

        def slot(a, p):
            return outs[a].at[4 * p[0] + 2 * p[1] + p[2]]

        def copy(a, k, block, to, src=None):
            return pltpu.make_async_remote_copy(
                src_ref=slot(a, block) if src is None else src, dst_ref=slot(a, block),
                send_sem=send_sems.at[self.base + a, k], recv_sem=recv_sems.at[self.base + a, k],
                device_id=to, device_id_type=MESH)

        n = len(self.arrays)

        def mine():
            return [pltpu.make_async_copy(ins[a], slot(a, me), local_sems.at[self.base + a]) for a in range(n)]

        def first():
            out = []
            for a in range(n):
                out.append(copy(a, 0, me, sibling, src=ins[a]))
                out += [copy(a, 1 + j, me, (*chip, c), src=ins[a]) for j, chip in enumerate(chips)]
            return out

        def arrivals():
            return [copy(a, 1 + j, (*chip, c), me) for j, chip in enumerate(chips) for a in range(n)]

        def relays():
            return [copy(a, 4 + j, (*chip, c), sibling) for j, chip in enumerate(chips) for a in range(n)]

        def from_sibling():
            out = [copy(a, 0, sibling, me) for a in range(n)]
            return out + [copy(a, 4 + j, (*chip, 1 - c), me) for j, chip in enumerate(chips) for a in range(n)]

        return mine, first, arrivals, relays, from_sibling

    def start(self, ins, outs, sems):
        mine, first, _, _, _ = self._plan(ins, outs, sems)
        for cp in mine() + first():
            cp.start()

    def relay(self, ins, outs, sems):
        _, _, arrivals, relays, _ = self._plan(ins, outs, sems)
        for arrived, onward in zip(arrivals(), relays()):
            arrived.wait_recv()
            onward.start()

    def finish(self, ins, outs, sems):
        mine, first, _, relays, from_sibling = self._plan(ins, outs, sems)
        for cp in from_sibling():
            cp.wait_recv()
        for cp in first() + relays():
            cp.wait_send()
        for cp in mine():
            cp.wait()


class _Exchange:
    collective_id = 0

    def __init__(self, arrays):
        self.arrays = list(arrays)
        self.out_shape = [jax.ShapeDtypeStruct(a.shape, a.dtype) for a in self.arrays]
        self.base = 0

    def barrier(self):
        x, y, c = _position()
        _handshake([(x ^ dx, y ^ dy, c ^ dc) for dx in (0, 1) for dy in (0, 1) for dc in (0, 1) if dx or dy or dc])

    def _plan(self, ins, outs, sems):
        send_sems, recv_sems, local_sems = sems
        x, y, c = _position()
        my_idx = 4 * x + 2 * y + c
        n = len(self.arrays)
        offsets = [(dx, dy, dc) for dx in (0, 1) for dy in (0, 1) for dc in (0, 1) if (dx, dy, dc) != (0, 0, 0)]

        def mine():
            return [pltpu.make_async_copy(ins[a].at[my_idx], outs[a].at[my_idx], local_sems.at[self.base + a])
                    for a in range(n)]

        def remote(arriving):
            out = []
            for k, (dx, dy, dc) in enumerate(offsets):
                px, py, pc = x ^ dx, y ^ dy, c ^ dc
                p_idx = 4 * px + 2 * py + pc
                for a in range(n):
                    out.append(pltpu.make_async_remote_copy(
                        src_ref=ins[a].at[p_idx], dst_ref=outs[a].at[p_idx if arriving else my_idx],
                        send_sem=send_sems.at[self.base + a, k], recv_sem=recv_sems.at[self.base + a, k],
                        device_id=(px, py, pc), device_id_type=MESH))
            return out

        return mine, remote

    def start(self, ins, outs, sems):
        mine, remote = self._plan(ins, outs, sems)
        for cp in mine() + remote(False):
            cp.start()

    def relay(self, ins, outs, sems):
        pass

    def finish(self, ins, outs, sems):
        mine, remote = self._plan(ins, outs, sems)
        for cp in remote(True):
            cp.wait_recv()
        for cp in remote(False):
            cp.wait_send()
        for cp in mine():
            cp.wait()


def _exchange_start(parts, thru, *, name):
    hbm = pl.BlockSpec(memory_space=pltpu.HBM)
    sem = pl.BlockSpec(memory_space=pltpu.SEMAPHORE)
    offsets = [(dx, dy, dc) for dx in (0, 1) for dy in (0, 1) for dc in (0, 1) if dx or dy or dc]

    def body(parts_ref, thru_ref, send_sems, recv_sems, parts_thru, land_ref, thru_out):
        x, y, c = _position()
        my_idx = 4 * x + 2 * y + c
        pltpu.make_async_copy(parts_ref.at[my_idx], land_ref.at[my_idx], send_sems.at[7]).start()
        _handshake([(x ^ dx, y ^ dy, c ^ dc) for dx, dy, dc in offsets])
        for k, (dx, dy, dc) in enumerate(offsets):
            px, py, pc = x ^ dx, y ^ dy, c ^ dc
            pltpu.make_async_remote_copy(
                src_ref=parts_ref.at[4 * px + 2 * py + pc], dst_ref=land_ref.at[my_idx],
                send_sem=send_sems.at[k], recv_sem=recv_sems.at[k],
                device_id=(px, py, pc), device_id_type=MESH).start()
        thru_out[...] = thru_ref[...]

    return pl.pallas_call(
        body, name=name,
        out_shape=(pltpu.SemaphoreType.DMA((8,)), pltpu.SemaphoreType.DMA((7,)), pltpu.HBM(parts.shape, parts.dtype),
                   pltpu.HBM(parts.shape, parts.dtype), jax.ShapeDtypeStruct(thru.shape, thru.dtype)),
        in_specs=(hbm, pl.BlockSpec(memory_space=pltpu.VMEM)),
        out_specs=(sem, sem, hbm, hbm, pl.BlockSpec(memory_space=pltpu.VMEM)),
        input_output_aliases={0: 2},
        compiler_params=pltpu.CompilerParams(has_side_effects=pltpu.SideEffectType.DATAFLOW_SIDE_EFFECTING,
                                             collective_id=2),
    )(pltpu.with_memory_space_constraint(parts, pltpu.HBM), thru)


def _exchange_wait(send_sems, recv_sems, parts_thru, land_thru, after, *, name):
    hbm = pl.BlockSpec(memory_space=pltpu.HBM)
    sem = pl.BlockSpec(memory_space=pltpu.SEMAPHORE)
    offsets = [(dx, dy, dc) for dx in (0, 1) for dy in (0, 1) for dc in (0, 1) if dx or dy or dc]

    def body(parts_ref, land_ref, send_sems, recv_sems, after_ref, parts_dead, got_ref):
        x, y, c = _position()
        my_idx = 4 * x + 2 * y + c
        pltpu.make_async_copy(parts_ref.at[my_idx], land_ref.at[my_idx], send_sems.at[7]).wait()
        for k, (dx, dy, dc) in enumerate(offsets):
            px, py, pc = x ^ dx, y ^ dy, c ^ dc
            p_idx = 4 * px + 2 * py + pc
            copy = pltpu.make_async_remote_copy(
                src_ref=parts_ref.at[p_idx], dst_ref=land_ref.at[p_idx], send_sem=send_sems.at[k],
                recv_sem=recv_sems.at[k], device_id=(px, py, pc), device_id_type=MESH)
            copy.wait_send()
            copy.wait_recv()

    return pl.pallas_call(
        body, name=name,
        out_shape=(pltpu.HBM(parts_thru.shape, parts_thru.dtype), pltpu.HBM(land_thru.shape, land_thru.dtype)),
        in_specs=(hbm, hbm, sem, sem, pl.BlockSpec(memory_space=pl.ANY)), out_specs=(hbm, hbm),
        input_output_aliases={0: 0, 1: 1},
        compiler_params=pltpu.CompilerParams(has_side_effects=pltpu.SideEffectType.DATAFLOW_SIDE_EFFECTING),
    )(parts_thru, land_thru, send_sems, recv_sems, after)[1]


class _Both:
    def __init__(self, *carries):
        self.carries = carries
        self.arrays = [a for c in carries for a in c.arrays]
        self.out_shape = [s for c in carries for s in c.out_shape]
        first = 0
        for c in carries:
            c.base = first
            first += len(c.arrays)
        self.collective_id = min(c.collective_id for c in carries)

    def barrier(self):
        min(self.carries, key=lambda c: c.collective_id).barrier()

    def _each(self, method, ins, outs, sems):
        for c in self.carries:
            rows = slice(c.base, c.base + len(c.arrays))
            getattr(c, method)(ins[rows], outs[rows], sems)

    def start(self, ins, outs, sems):
        self._each("start", ins, outs, sems)

    def relay(self, ins, outs, sems):
        self._each("relay", ins, outs, sems)

    def finish(self, ins, outs, sems):
        self._each("finish", ins, outs, sems)


def _call(body, *, name, grid, in_specs, out_specs, out_shape, args, scratch_shapes=(), carry=None):
    n_in, n_out, n_scr = len(in_specs), len(out_specs), len(scratch_shapes)
    params = pltpu.CompilerParams(dimension_semantics=("arbitrary",) * len(grid), vmem_limit_bytes=VMEM_LIMIT)
    if carry is None:
        outs = pl.pallas_call(body, name=name, grid=grid, in_specs=in_specs, out_specs=out_specs, out_shape=out_shape,
                              scratch_shapes=list(scratch_shapes), compiler_params=params)(*args)
        return outs, None
    m = len(carry.arrays)
    total = math.prod(grid)

    def wrapped(*refs):
        ins, refs = refs[:n_in], refs[n_in:]
        c_ins, refs = refs[:m], refs[m:]
        outs, refs = refs[:n_out], refs[n_out:]
        c_outs, refs = refs[:m], refs[m:]
        scr, sems = refs[:n_scr], refs[n_scr:]
        flat = pl.program_id(0)
        for d in range(1, len(grid)):
            flat = flat * grid[d] + pl.program_id(d)

        @pl.when(flat == 0)
        def _():
            carry.barrier()
            carry.start(c_ins, c_outs, sems)

        body(*ins, *outs, *scr)

        @pl.when(flat == max(total - 3, 0))
        def _():
            carry.relay(c_ins, c_outs, sems)

        @pl.when(flat == total - 1)
        def _():
            carry.finish(c_ins, c_outs, sems)

    any_spec = pl.BlockSpec(memory_space=pl.ANY)
    sem_shapes = [pltpu.SemaphoreType.DMA((m, 7)), pltpu.SemaphoreType.DMA((m, 7)), pltpu.SemaphoreType.DMA((m,))]
    params = pltpu.CompilerParams(dimension_semantics=("arbitrary",) * len(grid), vmem_limit_bytes=VMEM_LIMIT,
                                  collective_id=carry.collective_id)
    outs = pl.pallas_call(
        wrapped, name=name, grid=grid,
        in_specs=list(in_specs) + [any_spec] * m, out_specs=list(out_specs) + [any_spec] * m,
        out_shape=list(out_shape) + carry.out_shape,
        scratch_shapes=list(scratch_shapes) + sem_shapes, compiler_params=params)(*args, *carry.arrays)
    return outs[:n_out], outs[n_out:]


def _gather_now(arrays, *, name):
    carry = _Gather(arrays)
    m = len(arrays)

    def body(*refs):
        ins, outs, sems = refs[:m], refs[m:2 * m], refs[2 * m:]
        carry.barrier()
        carry.start(ins, outs, sems)
        carry.relay(ins, outs, sems)
        carry.finish(ins, outs, sems)

    any_spec = pl.BlockSpec(memory_space=pl.ANY)
    return pl.pallas_call(
        body, name=name, in_specs=[any_spec] * m, out_specs=[any_spec] * m, out_shape=carry.out_shape,
        scratch_shapes=[pltpu.SemaphoreType.DMA((m, 7)), pltpu.SemaphoreType.DMA((m, 7)),
                        pltpu.SemaphoreType.DMA((m,))],
        compiler_params=pltpu.CompilerParams(collective_id=carry.collective_id),
    )(*arrays)


def _sum_gathered(arrs, late, *, name):
    n = len(arrs)

    def body(*refs):
        late_ref, ins, late_out, outs = refs[0], refs[1:1 + n], refs[1 + n], refs[2 + n:2 + 2 * n]
        buf, send_sems, recv_sems = refs[2 + 2 * n:]
        x, y, c = _position()
        me, sibling = (x, y, c), (x, y, 1 - c)
        chips = [(1 - x, y), (x, 1 - y), (1 - x, 1 - y)]
        _handshake([sibling] + [(*chip, c) for chip in chips])

        def copy(k, block, to, src=None):
            slot = buf.at[4 * block[0] + 2 * block[1] + block[2]]
            return pltpu.make_async_remote_copy(
                src_ref=slot if src is None else src, dst_ref=slot,
                send_sem=send_sems.at[k], recv_sem=recv_sems.at[k], device_id=to, device_id_type=MESH)

        first = [copy(0, me, sibling, src=late_ref)]
        first += [copy(1 + j, me, (*chip, c), src=late_ref) for j, chip in enumerate(chips)]
        for cp in first:
            cp.start()
        for in_ref, out_ref in zip(ins, outs):
            acc = in_ref[0]
            for s in range(1, N_DEV):
                acc = acc + in_ref[s]
            out_ref[...] = acc
        passed = []
        for j, chip in enumerate(chips):
            copy(1 + j, (*chip, c), me).wait_recv()
            cp = copy(4 + j, (*chip, c), sibling)
            cp.start()
            passed.append(cp)
        copy(0, sibling, me).wait_recv()
        for j, chip in enumerate(chips):
            copy(4 + j, (*chip, 1 - c), me).wait_recv()
        for cp in first + passed:
            cp.wait_send()
        my_idx = 4 * x + 2 * y + c
        acc = jnp.zeros(late_ref.shape, F32)
        for s in range(N_DEV):
            acc = acc + jnp.where(my_idx == s, late_ref[...], buf[s])
        late_out[...] = acc

    vmem = pl.BlockSpec(memory_space=pltpu.VMEM)
    outs = pl.pallas_call(
        body, name=name, in_specs=[vmem] * (n + 1), out_specs=[vmem] * (n + 1),
        out_shape=[jax.ShapeDtypeStruct(late.shape, F32)] + [jax.ShapeDtypeStruct(a.shape[1:], F32) for a in arrs],
        scratch_shapes=[pltpu.VMEM((N_DEV,) + late.shape, F32), pltpu.SemaphoreType.DMA((7,)),
                        pltpu.SemaphoreType.DMA((7,))],
        compiler_params=pltpu.CompilerParams(vmem_limit_bytes=VMEM_LIMIT, collective_id=_Gather.collective_id),
    )(late, *arrs)
    return list(outs[1:]) + [outs[0]]


def _norm_matmul(x, g, w_t, *, tm, name, carry=None):
    n, d = x.shape
    c = w_t.shape[0]
    ch = MATMUL_CHUNK

    def body(x_ref, g_ref, wt_ref, h_ref, z_ref):
        xv = x_ref[...]
        r = lax.rsqrt(jnp.mean(xv * xv, axis=-1, keepdims=True) + RMS_EPS)
        h = (xv * r * g_ref[...]).astype(BF16)
        h_ref[...] = h
        for c0 in range(0, c, ch):
            z_ref[:, c0:c0 + ch] = _dot_nt(h, wt_ref[c0:c0 + ch, :]).astype(BF16)

    return _call(
        body, name=name, grid=(n // tm,), carry=carry,
        in_specs=[pl.BlockSpec((tm, d), lambda i: (i, 0)),
                  pl.BlockSpec((1, d), lambda i: (0, 0)),
                  pl.BlockSpec((c, d), lambda i: (0, 0))],
        out_specs=[pl.BlockSpec((tm, d), lambda i: (i, 0)),
                   pl.BlockSpec((tm, c), lambda i: (i, 0))],
        out_shape=[jax.ShapeDtypeStruct((n, d), BF16), jax.ShapeDtypeStruct((n, c), BF16)],
        args=(x, g.reshape(1, d), w_t))


def _mix_forward(z, x, b_gate, ln_g, ln_b, w_s, b_s, w_sc, wb, w_out, *, tm, name, carry=None):
    n = z.shape[0]
    hb = tm // HALO

    def body(z_ref, zp_ref, x_ref, bg_ref, lng_ref, lnb_ref, ws_ref, bs_ref, wsc_ref, wb_ref, wo_ref,
             ya_ref, yb_ref, cv_ref, pa_ref, pb_ref, mg_ref, x1_ref, f_scr):
        i = pl.program_id(0)
        u = z_ref[:, OFF_U:OFF_U + D_A]
        v = z_ref[:, OFF_V:OFF_V + D_A].astype(F32)
        gu, _, _, _, _, _, f = _gmlp_forward(u, v, lng_ref[...], lnb_ref[...], ws_ref, bs_ref, f_scr)
        ya = gu * f
        ya_ref[...] = ya

        q = z_ref[:, OFF_CG:OFF_CG + D_B] * z_ref[:, OFF_HB:OFF_HB + D_B]
        qp = zp_ref[:, OFF_CG:OFF_CG + D_B] * zp_ref[:, OFF_HB:OFF_HB + D_B]
        qp = jnp.where(i > 0, qp, jnp.zeros_like(qp))
        w = wsc_ref[...].astype(BF16)
        conv = w[0:1] * _shift_down(q, 2, qp) + w[1:2] * _shift_down(q, 1, qp) + w[2:3] * q
        cv_ref[...] = conv
        yb = z_ref[:, OFF_BG:OFF_BG + D_B] * conv
        yb_ref[...] = yb

        pa = _dot(ya, wb_ref[0]).astype(BF16)
        pb = _dot(yb, wb_ref[1]).astype(BF16)
        pa_ref[...] = pa
        pb_ref[...] = pb
        bg = bg_ref[...].astype(BF16)
        sa = _sigmoid(z_ref[:, OFF_GA:OFF_GA + D_MODEL] + bg[:, 0:D_MODEL])
        sb = _sigmoid(z_ref[:, OFF_GB:OFF_GB + D_MODEL] + bg[:, D_MODEL:2 * D_MODEL])
        mg = sa * pa + sb * pb
        mg_ref[...] = mg
        x1_ref[...] = x_ref[...] + _dot(mg, wo_ref[...])

    row = lambda w: pl.BlockSpec((tm, w), lambda i: (i, 0))
    full = lambda *s: pl.BlockSpec(s, lambda i: (0,) * len(s))
    bf = lambda w: jax.ShapeDtypeStruct((n, w), BF16)
    return _call(
        body, name=name, grid=(n // tm,), carry=carry,
        in_specs=[row(D_IN),
                  pl.BlockSpec((HALO, D_IN), lambda i: (jnp.maximum(i * hb - 1, 0), 0)),
                  row(D_MODEL), full(1, 2 * D_MODEL), full(1, D_A), full(1, D_A),
                  full(N_HEADS, GMLP_BLOCK, GMLP_BLOCK), full(N_HEADS, GMLP_BLOCK, 1), full(3, D_B),
                  full(2, D_A, D_MODEL), full(D_MODEL, D_MODEL)],
        out_specs=[row(D_A), row(D_B), row(D_B), row(D_MODEL), row(D_MODEL), row(D_MODEL), row(D_MODEL)],
        out_shape=[bf(D_A), bf(D_B), bf(D_B), bf(D_MODEL), bf(D_MODEL), bf(D_MODEL),
                   jax.ShapeDtypeStruct((n, D_MODEL), F32)],
        scratch_shapes=[pltpu.VMEM((tm, D_A), BF16)],
        args=(z, z, x, b_gate.reshape(1, -1), ln_g.reshape(1, -1), ln_b.reshape(1, -1), w_s,
              b_s.reshape(N_HEADS, GMLP_BLOCK, 1), w_sc, wb, w_out))


def _loss_tile(xv, gv, tv):
    d = xv.shape[-1]
    r = lax.rsqrt(jnp.mean(xv * xv, axis=-1, keepdims=True) + RMS_EPS)
    xh = xv * r
    e = xh * gv - tv
    per_row = jnp.sum(e * e, axis=-1, keepdims=True) * (0.5 / d)
    dy = e * (1.0 / d)
    dxh = dy * gv
    dx = r * (dxh - xh * jnp.mean(dxh * xh, axis=-1, keepdims=True))
    return dx, jnp.sum(per_row, axis=0, keepdims=True), jnp.sum(dy * xh, axis=0, keepdims=True)


def _ffn_forward(up, x1, w_fc, b_fc, w_down, *, tm, name, carry=None, head=None):
    n = up.shape[0]
    hb = tm // HALO
    n_in = 6 if head is None else 8

    def body(*refs):
        up_ref, upp_ref, x1_ref, wfc_ref, bfc_ref, wd_ref = refs[:6]
        gc_ref, a_ref, out_ref = refs[n_in:n_in + 3]
        acc = refs[-1]
        i = pl.program_id(0)
        acc[...] = x1_ref[...]
        for c0 in range(0, D_FF, FFN_CHUNK):
            cols = slice(c0, c0 + FFN_CHUNK)
            gate = up_ref[:, cols]
            val = up_ref[:, D_FF + c0:D_FF + c0 + FFN_CHUNK]
            gp = upp_ref[:, cols]
            gp = jnp.where(i > 0, gp, jnp.zeros_like(gp))
            w = wfc_ref[:, cols].astype(BF16)
            gc = (w[0:1] * _shift_down(gate, 2, gp) + w[1:2] * _shift_down(gate, 1, gp) + w[2:3] * gate
                  + bfc_ref[:, cols].astype(BF16))
            gc_ref[:, cols] = gc
            a = gc * _sigmoid(gc) * val
            a_ref[:, cols] = a
            acc[...] += _dot(a, wd_ref[cols, :])
        if head is None:
            out_ref[...] = acc[...]
        else:
            g_ref, t_ref = refs[6:8]
            sg_ref = refs[n_in + 3]

            @pl.when(i == 0)
            def _():
                sg_ref[...] = jnp.zeros_like(sg_ref)

            dx, loss, dg = _loss_tile(acc[...], g_ref[...], t_ref[...])
            out_ref[...] = dx
            sg_ref[ROW_LOSS:ROW_LOSS + 1, 0:LANES] += jnp.broadcast_to(loss, (1, LANES))
            sg_ref[ROW_FINAL:ROW_FINAL + 1, 0:D_MODEL] += dg

    row = lambda w: pl.BlockSpec((tm, w), lambda i: (i, 0))
    full = lambda r, c: pl.BlockSpec((r, c), lambda i: (0, 0))
    in_specs = [row(2 * D_FF), pl.BlockSpec((HALO, D_FF), lambda i: (jnp.maximum(i * hb - 1, 0), 0)), row(D_MODEL),
                full(3, D_FF), full(1, D_FF), full(D_FF, D_MODEL)]
    out_specs = [row(D_FF), row(D_FF), row(D_MODEL)]
    out_shape = [jax.ShapeDtypeStruct((n, D_FF), BF16), jax.ShapeDtypeStruct((n, D_FF), BF16),
                 jax.ShapeDtypeStruct((n, D_MODEL), F32)]
    args = (up, up, x1, w_fc, b_fc.reshape(1, -1), w_down)
    if head is not None:
        in_specs += [full(1, D_MODEL), row(D_MODEL)]
        out_specs += [full(SG_ROWS, SG_W)]
        out_shape += [jax.ShapeDtypeStruct((SG_ROWS, SG_W), F32)]
        args += (head[0].reshape(1, -1), head[1])
    return _call(body, name=name, grid=(n // tm,), carry=carry, in_specs=in_specs, out_specs=out_specs,
                 out_shape=out_shape, scratch_shapes=[pltpu.VMEM((tm, D_MODEL), F32)], args=args)


def _ffn_backward(dx2, up, gc, w_fc, w_down, sheet, layer, *, tm, name, carry=None):
    n = up.shape[0]
    steps = n // tm
    hb = tm // HALO
    row = SG_LAYER * layer + ROW_FCONV

    def body(dx_ref, dxn_ref, up_ref, upn_ref, gc_ref, gcn_ref, wfc_ref, wd_ref, sg_in, dup_ref, sg_ref):
        i = pl.program_id(0)
        last = i == steps - 1
        _sheet_begin(i, sg_in, sg_ref, row, 4)

        dxe = jnp.concatenate([dx_ref[...], dxn_ref[...]], axis=0).astype(BF16)
        for c0 in range(0, D_FF, FFN_CHUNK):
            cols = slice(c0, c0 + FFN_CHUNK)
            vcols = slice(D_FF + c0, D_FF + c0 + FFN_CHUNK)
            dae = _dot_nt(dxe, wd_ref[cols, :])
            da, dan = dae[:tm], dae[tm:]
            gate = up_ref[:, cols]
            val = up_ref[:, vcols]
            gcv = gc_ref[:, cols]
            s = _sigmoid(gcv)
            dab = da.astype(BF16)
            dup_ref[:, vcols] = dab * (gcv * s)
            dgc = dab * val * (s * (1.0 + gcv * (1.0 - s)))
            gcn = gcn_ref[:, cols]
            sn = _sigmoid(gcn)
            dgcn = dan.astype(BF16) * upn_ref[:, vcols] * (sn * (1.0 + gcn * (1.0 - sn)))
            dgcn = jnp.where(last, jnp.zeros_like(dgcn), dgcn)
            up1 = _shift_up(dgc, 1, dgcn)
            up2 = _shift_up(dgc, 2, dgcn)
            w = wfc_ref[:, cols].astype(BF16)
            dup_ref[:, cols] = w[2:3] * dgc + w[1:2] * up1 + w[0:1] * up2
            sg_ref[row:row + 1, cols] += _column_sums(gate * up2)
            sg_ref[row + 1:row + 2, cols] += _column_sums(gate * up1)
            sg_ref[row + 2:row + 3, cols] += _column_sums(gate * dgc)
            sg_ref[row + 3:row + 4, cols] += _column_sums(dgc)

    nxt = lambda i: (jnp.minimum((i + 1) * hb, steps * hb - 1), 0)
    return _call(
        body, name=name, grid=(steps,), carry=carry,
        in_specs=[pl.BlockSpec((tm, D_MODEL), lambda i: (i, 0)),
                  pl.BlockSpec((HALO, D_MODEL), nxt),
                  pl.BlockSpec((tm, 2 * D_FF), lambda i: (i, 0)),
                  pl.BlockSpec((HALO, 2 * D_FF), nxt),
                  pl.BlockSpec((tm, D_FF), lambda i: (i, 0)),
                  pl.BlockSpec((HALO, D_FF), nxt),
                  pl.BlockSpec((3, D_FF), lambda i: (0, 0)),
                  pl.BlockSpec((D_FF, D_MODEL), lambda i: (0, 0)), _sheet_spec()],
        out_specs=[pl.BlockSpec((tm, 2 * D_FF), lambda i: (i, 0)), _sheet_spec()],
        out_shape=[jax.ShapeDtypeStruct((n, 2 * D_FF), BF16), jax.ShapeDtypeStruct((SG_ROWS, SG_W), F32)],
        args=(dx2, dx2, up, up, gc, gc, w_fc, w_down, sheet))


def _matmul_norm_backward(dz, w_t, x, g, dres, sheet, row, *, tm, name, carry=None):
    n, c = dz.shape
    d = x.shape[1]
    ch = MATMUL_CHUNK

    def body(dz_ref, wt_ref, x_ref, g_ref, dres_ref, *rest):
        i = pl.program_id(0)
        if sheet is None:
            dx_ref, sg_ref = rest

            @pl.when(i == 0)
            def _():
                sg_ref[...] = jnp.zeros_like(sg_ref)
        else:
            sg_in, dx_ref, sg_ref = rest
            _sheet_begin(i, sg_in, sg_ref, row, 1)

        dh = _dot(dz_ref[:, 0:ch], wt_ref[0:ch, :])
        for c0 in range(ch, c, ch):
            dh += _dot(dz_ref[:, c0:c0 + ch], wt_ref[c0:c0 + ch, :])
        xv = x_ref[...]
        r = lax.rsqrt(jnp.mean(xv * xv, axis=-1, keepdims=True) + RMS_EPS)
        xh = xv * r
        sg_ref[row:row + 1, 0:d] += jnp.sum(dh * xh, axis=0, keepdims=True)
        dxh = dh * g_ref[...]
        dx_ref[...] = dres_ref[...] + r * (dxh - xh * jnp.mean(dxh * xh, axis=-1, keepdims=True))

    in_specs = [pl.BlockSpec((tm, c), lambda i: (i, 0)),
                pl.BlockSpec((c, d), lambda i: (0, 0)),
                pl.BlockSpec((tm, d), lambda i: (i, 0)),
                pl.BlockSpec((1, d), lambda i: (0, 0)),
                pl.BlockSpec((tm, d), lambda i: (i, 0))]
    args = (dz, w_t, x, g.reshape(1, d), dres)
    if sheet is None:
        small_spec, small_shape = pl.BlockSpec((8, d), lambda i: (0, 0)), jax.ShapeDtypeStruct((8, d), F32)
    else:
        in_specs, args = in_specs + [_sheet_spec()], args + (sheet,)
        small_spec, small_shape = _sheet_spec(), jax.ShapeDtypeStruct((SG_ROWS, SG_W), F32)
    return _call(
        body, name=name, grid=(n // tm,), carry=carry, in_specs=in_specs,
        out_specs=[pl.BlockSpec((tm, d), lambda i: (i, 0)), small_spec],
        out_shape=[jax.ShapeDtypeStruct((n, d), F32), small_shape], args=args)


def _mix_backward(dx1, z, conv, pa, pb, b_gate, ln_g, ln_b, w_s, w_s_t, b_s, w_sc, w_out, wb, sheet, layer, *, tm, name,
                  carry=None):
    n = z.shape[0]
    steps = n // tm
    hb = tm // HALO
    base = SG_LAYER * layer
    r_bg, r_lng, r_lnb, r_sc = base + ROW_BGATE, base + ROW_LN_G, base + ROW_LN_B, base + ROW_SCONV

    def body(dx_ref, dxn_ref, z_ref, zn_ref, cv_ref, pa_ref, pb_ref, bg_ref, lng_ref, lnb_ref, ws_ref, wst_ref,
             bs_ref, wsc_ref, wo_ref, wb_ref, sg_in,
             dz_ref, dpa_ref, dpb_ref, dws_ref, dbs_ref, sg_ref, f_scr, dvn_scr):
        i = pl.program_id(0)
        last = i == steps - 1
        _sheet_begin(i, sg_in, sg_ref, r_bg, ROW_NORM2 - ROW_BGATE)

        @pl.when(i == 0)
        def _():
            dws_ref[...] = jnp.zeros_like(dws_ref)
            dbs_ref[...] = jnp.zeros_like(dbs_ref)

        dxe = jnp.concatenate([dx_ref[...], dxn_ref[...]], axis=0).astype(BF16)
        dmge = _dot_nt(dxe, wo_ref[...])
        dmg, dmgn = dmge[:tm].astype(BF16), dmge[tm:].astype(BF16)

        pa_v = pa_ref[...]
        pb_v = pb_ref[...]
        bg = bg_ref[...].astype(BF16)
        sa = _sigmoid(z_ref[:, OFF_GA:OFF_GA + D_MODEL] + bg[:, 0:D_MODEL])
        sb = _sigmoid(z_ref[:, OFF_GB:OFF_GB + D_MODEL] + bg[:, D_MODEL:2 * D_MODEL])
        dpa = dmg * sa
        dpb = dmg * sb
        dga = dmg * pa_v * sa * (1.0 - sa)
        dgb = dmg * pb_v * sb * (1.0 - sb)
        dpa_ref[...] = dpa
        dpb_ref[...] = dpb
        dz_ref[:, OFF_GA:OFF_GA + D_MODEL] = dga
        dz_ref[:, OFF_GB:OFF_GB + D_MODEL] = dgb
        sg_ref[r_bg:r_bg + 1, 0:D_MODEL] += _column_sums(dga)
        sg_ref[r_bg:r_bg + 1, D_MODEL:2 * D_MODEL] += _column_sums(dgb)

        dya = _dot_nt(dpa, wb_ref[0]).astype(BF16)
        u = z_ref[:, OFF_U:OFF_U + D_A]
        v = z_ref[:, OFF_V:OFF_V + D_A].astype(F32)
        ln_g = lng_ref[...]
        gu, tu, tv, xh, rstd, vn, f = _gmlp_forward(u, v, ln_g, lnb_ref[...], ws_ref, bs_ref, f_scr)
        dgu = dya * f
        df_bf = dya * gu
        dz_ref[:, OFF_U:OFF_U + D_A] = dgu * _gelu_grad(u, tu)
        mask = _spatial_mask(False)
        mask_t = _spatial_mask(True)
        wmt = [jnp.where(mask_t, wst_ref[h], 0.0).astype(BF16) for h in range(N_HEADS)]
        for b in range(tm // GMLP_BLOCK):
            rows = slice(b * GMLP_BLOCK, (b + 1) * GMLP_BLOCK)
            for h in range(N_HEADS):
                cols = slice(h * HEAD, (h + 1) * HEAD)
                dfb = df_bf[rows, cols]
                dvn_scr[rows, cols] = _dot(wmt[h], dfb)
                dws_ref[h] += jnp.where(mask, _dot_nt(dfb, vn[rows, cols]), 0.0)
                dbs_ref[h] += jnp.sum(dfb.astype(F32), axis=1, keepdims=True)
        dvn = dvn_scr[...]
        sg_ref[r_lng:r_lng + 1, 0:D_A] += jnp.sum(dvn * xh, axis=0, keepdims=True)
        sg_ref[r_lnb:r_lnb + 1, 0:D_A] += jnp.sum(dvn, axis=0, keepdims=True)
        dxh = dvn * ln_g
        dgv = rstd * (dxh - jnp.mean(dxh, axis=-1, keepdims=True) - xh * jnp.mean(dxh * xh, axis=-1, keepdims=True))
        dz_ref[:, OFF_V:OFF_V + D_A] = (dgv * _gelu_grad(v, tv)).astype(BF16)

        sbn = _sigmoid(zn_ref[:, OFF_GB:OFF_GB + D_MODEL] + bg[:, D_MODEL:2 * D_MODEL])
        dpbe = jnp.concatenate([dpb, dmgn * sbn], axis=0)
        dybe = _dot_nt(dpbe, wb_ref[1])
        dyb, dybn = dybe[:tm].astype(BF16), dybe[tm:].astype(BF16)
        bgv = z_ref[:, OFF_BG:OFF_BG + D_B]
        cg = z_ref[:, OFF_CG:OFF_CG + D_B]
        hbv = z_ref[:, OFF_HB:OFF_HB + D_B]
        q = cg * hbv
        dz_ref[:, OFF_BG:OFF_BG + D_B] = dyb * cv_ref[...]
        dconv = dyb * bgv
        dconvn = dybn * zn_ref[:, OFF_BG:OFF_BG + D_B]
        dconvn = jnp.where(last, jnp.zeros_like(dconvn), dconvn)
        up1 = _shift_up(dconv, 1, dconvn)
        up2 = _shift_up(dconv, 2, dconvn)
        sg_ref[r_sc:r_sc + 1, 0:D_B] += _column_sums(q * up2)
        sg_ref[r_sc + 1:r_sc + 2, 0:D_B] += _column_sums(q * up1)
        sg_ref[r_sc + 2:r_sc + 3, 0:D_B] += _column_sums(q * dconv)
        w = wsc_ref[...].astype(BF16)
        dq = w[2:3] * dconv + w[1:2] * up1 + w[0:1] * up2
        dz_ref[:, OFF_CG:OFF_CG + D_B] = dq * hbv
        dz_ref[:, OFF_HB:OFF_HB + D_B] = dq * cg

    row = lambda w: pl.BlockSpec((tm, w), lambda i: (i, 0))
    full = lambda *s: pl.BlockSpec(s, lambda i: (0,) * len(s))
    nxt = lambda i: (jnp.minimum((i + 1) * hb, steps * hb - 1), 0)
    return _call(
        body, name=name, grid=(steps,), carry=carry,
        in_specs=[row(D_MODEL), pl.BlockSpec((HALO, D_MODEL), nxt),
                  row(D_IN), pl.BlockSpec((HALO, D_IN), nxt),
                  row(D_B), row(D_MODEL), row(D_MODEL),
                  full(1, 2 * D_MODEL), full(1, D_A), full(1, D_A),
                  full(N_HEADS, GMLP_BLOCK, GMLP_BLOCK), full(N_HEADS, GMLP_BLOCK, GMLP_BLOCK),
                  full(N_HEADS, GMLP_BLOCK, 1), full(3, D_B),
                  full(D_MODEL, D_MODEL), full(2, D_A, D_MODEL), _sheet_spec()],
        out_specs=[row(D_IN), row(D_MODEL), row(D_MODEL), full(N_HEADS, GMLP_BLOCK, GMLP_BLOCK),
                   full(N_HEADS, GMLP_BLOCK, 1), _sheet_spec()],
        out_shape=[jax.ShapeDtypeStruct((n, D_IN), BF16), jax.ShapeDtypeStruct((n, D_MODEL), BF16),
                   jax.ShapeDtypeStruct((n, D_MODEL), BF16),
                   jax.ShapeDtypeStruct((N_HEADS, GMLP_BLOCK, GMLP_BLOCK), F32),
                   jax.ShapeDtypeStruct((N_HEADS, GMLP_BLOCK, 1), F32), jax.ShapeDtypeStruct((SG_ROWS, SG_W), F32)],
        scratch_shapes=[pltpu.VMEM((tm, D_A), BF16), pltpu.VMEM((tm, D_A), F32)],
        args=(dx1, dx1, z, z, conv, pa, pb, b_gate.reshape(1, -1), ln_g.reshape(1, -1), ln_b.reshape(1, -1), w_s, w_s_t,
              b_s.reshape(N_HEADS, GMLP_BLOCK, 1), w_sc, w_out, wb, sheet))


def _matmul_tn(a, b, *, t1, tn, name, carry=None, pieces=1):
    n, k1 = a.shape
    k2 = b.shape[1]
    steps = n // tn
    w = k2 // pieces

    def body(a_ref, b_ref, *rest):
        o_refs, acc = rest[:pieces], rest[pieces]
        s = pl.program_id(1)

        @pl.when(s == 0)
        def _():
            acc[...] = jnp.zeros_like(acc)

        acc[...] += lax.dot_general(a_ref[...].astype(BF16), b_ref[...].astype(BF16), TN, preferred_element_type=F32)

        @pl.when(s == steps - 1)
        def _():
            for c, o_ref in enumerate(o_refs):
                o_ref[...] = acc[:, c * w:(c + 1) * w].astype(BF16)

    outs, carried = _call(
        body, name=name, grid=(k1 // t1, steps), carry=carry,
        in_specs=[pl.BlockSpec((tn, t1), lambda i, s: (s, i)),
                  pl.BlockSpec((tn, k2), lambda i, s: (s, 0))],
        out_specs=[pl.BlockSpec((t1, w), lambda i, s: (i, 0))] * pieces,
        out_shape=[jax.ShapeDtypeStruct((k1, w), BF16)] * pieces,
        scratch_shapes=[pltpu.VMEM((t1, k2), F32)],
        args=(a, b))
    return (outs[0] if pieces == 1 else list(outs)), carried


def _adamw_math(w, g, m, v):
    m = ADAM_B1 * m + (1.0 - ADAM_B1) * g
    v = ADAM_B2 * v + (1.0 - ADAM_B2) * (g * g)
    m_hat = m / (1.0 - ADAM_B1 ** ADAM_STEP)
    v_hat = v / (1.0 - ADAM_B2 ** ADAM_STEP)
    delta = -ADAM_LR * (m_hat / (jnp.sqrt(v_hat) + ADAM_EPS) + ADAM_WD * w)
    return delta, m, v


def _sum_parts(recvs, *, tr, name):
    _, r, c = recvs[0].shape

    def body(*refs):
        recv_refs, g_ref = refs[:DEPTH], refs[DEPTH]
        layer = pl.program_id(0)
        for l in range(DEPTH):
            @pl.when(layer == l)
            def _(l=l):
                g = recv_refs[l][0].astype(F32)
                for s in range(1, N_DEV):
                    g = g + recv_refs[l][s].astype(F32)
                g_ref[0] = g

    outs, _ = _call(
        body, name=name, grid=(DEPTH, r // tr),
        in_specs=[pl.BlockSpec((N_DEV, tr, c), lambda l, i: (0, i, 0))] * DEPTH,
        out_specs=[pl.BlockSpec((1, tr, c), lambda l, i: (l, i, 0))],
        out_shape=[jax.ShapeDtypeStruct((DEPTH, r, c), F32)],
        args=tuple(recvs))
    return outs[0]


def _adamw(w, g, m, v, *, tr, name):
    r, c = w.shape

    def body(w_ref, g_ref, m_ref, v_ref, d_ref, nm_ref, nv_ref):
        delta, nm, nv = _adamw_math(w_ref[...], g_ref[...], m_ref[...], v_ref[...])
        d_ref[...] = delta
        nm_ref[...] = nm
        nv_ref[...] = nv

    spec = pl.BlockSpec((tr, c), lambda i: (i, 0))
    outs, _ = _call(body, name=name, grid=(r // tr,), in_specs=[spec] * 4, out_specs=[spec] * 3,
                    out_shape=[jax.ShapeDtypeStruct((r, c), F32)] * 3, args=(w, g, m, v))
    return outs


def _sum_adamw(recvs, w, m, v, *, tr, name, carry=None):
    _, r, c = w.shape
    blocks = len(recvs[0])
    flat = [piece for layer in recvs for piece in layer]

    def body(*refs):
        recv_refs = refs[:len(flat)]
        w_ref, m_ref, v_ref, g_ref, d_ref, nm_ref, nv_ref = refs[len(flat):]
        layer = pl.program_id(0)
        for l in range(DEPTH):
            @pl.when(layer == l)
            def _(l=l):
                cols = []
                for piece in recv_refs[l * blocks:(l + 1) * blocks]:
                    part = piece[0].astype(F32)
                    for s in range(1, N_DEV):
                        part = part + piece[s].astype(F32)
                    cols.append(part)
                g = cols[0] if blocks == 1 else jnp.concatenate(cols, axis=-1)
                delta, nm, nv = _adamw_math(w_ref[0], g, m_ref[0], v_ref[0])
                g_ref[0] = g
                d_ref[0] = delta
                nm_ref[0] = nm
                nv_ref[0] = nv

    spec = pl.BlockSpec((1, tr, c), lambda l, i: (l, i, 0))
    return _call(
        body, name=name, grid=(DEPTH, r // tr),
        in_specs=[pl.BlockSpec((N_DEV, tr, c // blocks), lambda l, i: (0, i, 0))] * len(flat) + [spec] * 3,
        out_specs=[spec] * 4, out_shape=[jax.ShapeDtypeStruct((DEPTH, r, c), F32)] * 4,
        args=tuple(flat) + (w, m, v), carry=carry)


def _adamw_small(sheet, gain0, extra, params, *, name):
    sheet_rows = dict(norm1_g=ROW_NORM1, b_gate=ROW_BGATE, gmlp_ln_g=ROW_LN_G, gmlp_ln_b=ROW_LN_B, norm2_g=ROW_NORM2,
                      b_ffn_conv=ROW_BFCONV)
    names = list(params)
    extra_names = list(extra)

    def body(*refs):
        sg_ref, gain0_ref, refs = refs[0], refs[1], refs[2:]
        extra_refs, refs = dict(zip(extra_names, refs[:len(extra_names)])), refs[len(extra_names):]
        ins, outs, loss_ref = refs[:3 * len(names)], refs[3 * len(names):-1], refs[-1]
        loss_ref[...] = sg_ref[ROW_LOSS:ROW_LOSS + 1, 0:1]
        for j, key in enumerate(names):
            w_ref, m_ref, v_ref = ins[3 * j:3 * j + 3]
            g_ref, d_ref, nm_ref, nv_ref = outs[4 * j:4 * j + 4]
            if key in extra_refs:
                g_ref[...] = extra_refs[key][...]
            elif key == "final_g":
                g_ref[...] = sg_ref[ROW_FINAL:ROW_FINAL + 1, 0:D_MODEL]
            else:
                width = w_ref.shape[-1]
                for l in range(DEPTH):
                    row = SG_LAYER * l + sheet_rows[key]
                    if key == "norm1_g" and l == 0:
                        g_ref[0:1, :] = gain0_ref[0:1, :]
                    else:
                        g_ref[l:l + 1, :] = sg_ref[row:row + 1, 0:width]
            delta, nm, nv = _adamw_math(w_ref[...], g_ref[...], m_ref[...], v_ref[...])
            d_ref[...] = delta
            nm_ref[...] = nm
            nv_ref[...] = nv

    args = [sheet, gain0] + [extra[k] for k in extra_names] + [t for k in names for t in params[k]]
    vmem = pl.BlockSpec(memory_space=pltpu.VMEM)
    outs = pl.pallas_call(
        body, name=name, in_specs=[vmem] * len(args), out_specs=[vmem] * (4 * len(names) + 1),
        out_shape=[jax.ShapeDtypeStruct(params[k][0].shape, F32) for k in names for _ in range(4)]
        + [jax.ShapeDtypeStruct((1, 1), F32)],
    )(*args)
    return {k: tuple(outs[4 * j:4 * j + 4]) for j, k in enumerate(names)}, outs[-1]


def _rows(gathered):
    return gathered.reshape(N_DEV * gathered.shape[1], gathered.shape[2])


def _parts(full):
    return full.reshape(N_DEV, full.shape[0] // N_DEV, full.shape[1])


def kernel(x, norm1_g, w_in, b_gate, gmlp_ln_g, gmlp_ln_b, w_spatial, b_spatial, w_shortconv, w_branch, w_out, norm2_g, w_ffn_up, w_ffn_conv, b_ffn_conv, w_ffn_down, final_g, loss_target, m_norm1_g, m_w_in, m_b_gate, m_gmlp_ln_g, m_gmlp_ln_b, m_w_spatial, m_b_spatial, m_w_shortconv, m_w_branch, m_w_out, m_norm2_g, m_w_ffn_up, m_w_ffn_conv, m_b_ffn_conv, m_w_ffn_down, m_final_g, v_norm1_g, v_w_in, v_b_gate, v_gmlp_ln_g, v_gmlp_ln_b, v_w_spatial, v_b_spatial, v_w_shortconv, v_w_branch, v_w_out, v_norm2_g, v_w_ffn_up, v_w_ffn_conv, v_b_ffn_conv, v_w_ffn_down, v_final_g):
    n = x.shape[1]
    tm_in, tm, tn = 1024, 512, 2048
    x0 = x.reshape(n, D_MODEL)
    target = loss_target.reshape(n, D_MODEL)
    my_idx = 4 * lax.axis_index("x") + 2 * lax.axis_index("y") + lax.axis_index("c")
    sc_w, fc_w = D_B // N_DEV, D_FF // N_DEV

    sh_in = [w_in[l].T.astype(BF16) for l in range(DEPTH)]
    sh_up = [w_ffn_up[l].T.astype(BF16) for l in range(DEPTH)]
    sh_br = [w_branch[l].astype(BF16) for l in range(DEPTH)]
    sh_out = [w_out[l].astype(BF16) for l in range(DEPTH)]
    sh_down = [w_ffn_down[l].astype(BF16) for l in range(DEPTH)]
    taps = jnp.concatenate([w_shortconv, w_ffn_conv], axis=-1)

    def branch_weights(g):
        return g.transpose(1, 2, 0, 3).reshape(2, D_A, D_MODEL)

    g_in0, g_taps = _gather_now([sh_in[0], taps], name="gather_first")
    w_sc = [g_taps[:, l, :, :sc_w].transpose(1, 0, 2).reshape(3, D_B) for l in range(DEPTH)]
    w_fc = [g_taps[:, l, :, sc_w:].transpose(1, 0, 2).reshape(3, D_FF) for l in range(DEPTH)]
    w_s_t = [w_spatial[l].transpose(0, 2, 1) for l in range(DEPTH)]
    weights = [dict(), dict()]
    weights[0]["in_t"] = _rows(g_in0)
    saved = []
    xc = x0
    for l in range(DEPTH):
        p = weights[l]
        carry = _Gather([sh_br[0], sh_out[0]] if l == 0 else [sh_up[1]])
        (h, z), got = _norm_matmul(xc, norm1_g[l], p["in_t"], tm=tm_in, name=f"fwd_in_{l}", carry=carry)
        if l == 0:
            p["wb"], p["out"] = branch_weights(got[0]), _rows(got[1])
        else:
            p["up_t"] = _rows(got[0])
        carry = _Gather([sh_up[0]]) if l == 0 else None
        (ya, yb, conv, pa, pb, mg, x1), got = _mix_forward(
            z, xc, b_gate[l], gmlp_ln_g[l], gmlp_ln_b[l], w_spatial[l], b_spatial[l], w_sc[l], p["wb"], p["out"],
            tm=tm, name=f"fwd_mix_{l}", carry=carry)
        if l == 0:
            p["up_t"] = _rows(got[0])
        (h2, up), got = _norm_matmul(x1, norm2_g[l], p["up_t"], tm=tm, name=f"fwd_up_{l}", carry=_Gather([sh_down[l]]))
        p["down"] = _rows(got[0])
        carry = _Gather([sh_br[1], sh_out[1], sh_in[1]]) if l == 0 else None
        head = (final_g, target) if l == DEPTH - 1 else None
        outs, got = _ffn_forward(up, x1, w_fc[l], b_ffn_conv[l], p["down"], tm=tm, name=f"fwd_ffn_{l}", carry=carry, head=head)
        if l == 0:
            weights[1]["wb"], weights[1]["out"], weights[1]["in_t"] = branch_weights(got[0]), _rows(got[1]), _rows(got[2])
        gc, a = outs[0], outs[1]
        saved.append(dict(x=xc, h=h, z=z, ya=ya, yb=yb, conv=conv, pa=pa, pb=pb, mg=mg, x1=x1, h2=h2, up=up, gc=gc, a=a))
        xc = outs[2]
    dx, sheet = outs[2], outs[3]

    recv = [dict(), dict()]
    small_dws, small_dbs = [None] * DEPTH, [None] * DEPTH
    pending_in = None
    for l in reversed(range(DEPTH)):
        p, s = weights[l], saved[l]
        carry = _Exchange([pending_in]) if pending_in is not None else None
        (dup, sheet), got = _ffn_backward(dx, s["up"], s["gc"], w_fc[l], p["down"], sheet, l, tm=tm, name=f"bwd_ffn_{l}",
                                          carry=carry)
        if got is not None:
            recv[l + 1]["in_t"] = got[0]
        dw_down, _ = _matmul_tn(s["a"], dx, t1=D_FF // 2, tn=tn, name=f"dw_down_{l}")
        dw_up_t, got = _matmul_tn(dup, s["h2"], t1=2 * D_FF // 4, tn=tn, name=f"dw_up_{l}", pieces=2,
                                  carry=_Exchange([_parts(dw_down)]))
        recv[l]["down"] = got[0]
        (dx1, sheet), got_left = _matmul_norm_backward(
            dup, p["up_t"], s["x1"], norm2_g[l], dx, sheet, SG_LAYER * l + ROW_NORM2, tm=tm, name=f"bwd_up_{l}",
            carry=_Exchange([_parts(dw_up_t[0])]))
        dw_out, _ = _matmul_tn(s["mg"], dx1, t1=D_MODEL, tn=tn, name=f"dw_out_{l}")
        (dz, dpa, dpb, small_dws[l], small_dbs[l], sheet), got_right = _mix_backward(
            dx1, s["z"], s["conv"], s["pa"], s["pb"], b_gate[l], gmlp_ln_g[l], gmlp_ln_b[l], w_spatial[l], w_s_t[l],
            b_spatial[l], w_sc[l], p["out"], p["wb"], sheet, l, tm=tm, name=f"bwd_mix_{l}",
            carry=_Exchange([_parts(dw_up_t[1]), _parts(dw_out)]))
        recv[l]["up_t"], recv[l]["out"] = [got_left[0], got_right[0]], got_right[1]
        dw_bra_t, _ = _matmul_tn(dpa, s["ya"], t1=D_MODEL, tn=tn, name=f"dw_branch_a_{l}")
        dw_brb_t, _ = _matmul_tn(dpb, s["yb"], t1=D_MODEL, tn=tn, name=f"dw_branch_b_{l}")
        carry = _Exchange([_parts(dw_bra_t), _parts(dw_brb_t)])
        if l == 0:
            dbs = jnp.stack([t.reshape(N_HEADS, GMLP_BLOCK) for t in small_dbs]).reshape(DEPTH * N_HEADS, GMLP_BLOCK)
            carry = _Both(carry, _Gather([sheet, small_dws[0], small_dws[1], dbs]))
        dw_in_t, got = _matmul_tn(dz, s["h"], t1=D_IN // 4, tn=tn, name=f"dw_in_{l}", carry=carry)
        recv[l]["bra_t"], recv[l]["brb_t"] = got[:2]
        if l == 0:
            gathered_small = got[2:]
            send_sems, recv_sems, parts_thru, land_thru, gain = _exchange_start(
                _parts(dw_in_t), norm1_g[l].reshape(1, D_MODEL), name="exchange_w_in_0_start")
            (dx0, dg1_first), _ = _matmul_norm_backward(dz, p["in_t"], s["x"], gain.reshape(D_MODEL), dx1, None, 0,
                                                        tm=tm, name=f"bwd_in_{l}")
            recv[0]["in_t"] = _exchange_wait(send_sems, recv_sems, parts_thru, land_thru, dg1_first,
                                             name="exchange_w_in_0_wait")
        else:
            (dx0, sheet), _ = _matmul_norm_backward(dz, p["in_t"], s["x"], norm1_g[l], dx1, sheet,
                                                    SG_LAYER * l + ROW_NORM1, tm=tm, name=f"bwd_in_{l}")
            pending_in = _parts(dw_in_t)
        dx = dx0
    grad_x = dx.reshape(x.shape)

    results = {}
    both = lambda key: [recv[l][key] for l in range(DEPTH)]
    blocks = lambda key: [r if isinstance(r, list) else [r] for r in both(key)]
    swap = lambda t: t.transpose(0, 2, 1)
    for key, slab, (w, m, v), tr in [("w_in", "in_t", (w_in, m_w_in, v_w_in), 192),
                                     ("w_ffn_up", "up_t", (w_ffn_up, m_w_ffn_up, v_w_ffn_up), 176)]:
        outs, _ = _sum_adamw(blocks(slab), swap(w), swap(m), swap(v), tr=tr, name=f"adamw_{key}")
        results[key] = tuple(swap(o) for o in outs)
    g_bra = _sum_parts(both("bra_t"), tr=128, name="sum_w_branch_a").transpose(0, 2, 1)
    g_brb = _sum_parts(both("brb_t"), tr=128, name="sum_w_branch_b").transpose(0, 2, 1)
    g_br = jnp.stack([g_bra, g_brb], axis=1)
    flat = lambda t: t.reshape(-1, t.shape[-1])
    outs = _adamw(flat(w_branch), flat(g_br), flat(m_w_branch), flat(v_w_branch), tr=512, name="adamw_w_branch")
    results["w_branch"] = (g_br,) + tuple(o.reshape(w_branch.shape) for o in outs)
    results["w_out"] = tuple(_sum_adamw(blocks("out"), w_out, m_w_out, v_w_out, tr=128, name="adamw_w_out")[0])
    results["w_ffn_down"] = tuple(_sum_adamw(blocks("down"), w_ffn_down, m_w_ffn_down, v_w_ffn_down, tr=176,
                                             name="adamw_w_ffn_down")[0])

    sheet, dws0, dws1, dbs, gain0 = _sum_gathered(gathered_small, dg1_first, name="sum_small_grads")
    swap_taps = lambda t: t.transpose(1, 0, 2)
    taps = lambda row, width: jnp.stack([sheet[SG_LAYER * l + row:SG_LAYER * l + row + 3, :width] for l in range(DEPTH)], axis=1)
    extra = dict(w_spatial=jnp.stack([dws0, dws1]), b_spatial=dbs.reshape(DEPTH, N_HEADS, GMLP_BLOCK),
                 w_shortconv=lax.dynamic_slice_in_dim(taps(ROW_SCONV, D_B), my_idx * sc_w, sc_w, axis=2),
                 w_ffn_conv=lax.dynamic_slice_in_dim(taps(ROW_FCONV, D_FF), my_idx * fc_w, fc_w, axis=2))
    small_w = dict(norm1_g=(norm1_g, m_norm1_g, v_norm1_g), b_gate=(b_gate, m_b_gate, v_b_gate),
                   gmlp_ln_g=(gmlp_ln_g, m_gmlp_ln_g, v_gmlp_ln_g), gmlp_ln_b=(gmlp_ln_b, m_gmlp_ln_b, v_gmlp_ln_b),
                   w_spatial=(w_spatial, m_w_spatial, v_w_spatial), b_spatial=(b_spatial, m_b_spatial, v_b_spatial),
                   w_shortconv=tuple(swap_taps(t) for t in (w_shortconv, m_w_shortconv, v_w_shortconv)), norm2_g=(norm2_g, m_norm2_g, v_norm2_g),
                   w_ffn_conv=tuple(swap_taps(t) for t in (w_ffn_conv, m_w_ffn_conv, v_w_ffn_conv)), b_ffn_conv=(b_ffn_conv, m_b_ffn_conv, v_b_ffn_conv),
                   final_g=tuple(t.reshape(1, D_MODEL) for t in (final_g, m_final_g, v_final_g)))
    small, loss = _adamw_small(sheet, gain0, extra, small_w, name="adamw_small")
    loss = loss.reshape(())
    results.update(small)
    results["final_g"] = tuple(t.reshape(D_MODEL) for t in results["final_g"])
    for key in ("w_shortconv", "w_ffn_conv"):
        results[key] = tuple(swap_taps(t) for t in results[key])

    names = ["norm1_g", "w_in", "b_gate", "gmlp_ln_g", "gmlp_ln_b", "w_spatial", "b_spatial", "w_shortconv", "w_branch",
             "w_out", "norm2_g", "w_ffn_up", "w_ffn_conv", "b_ffn_conv", "w_ffn_down", "final_g"]
    return (loss, grad_x, *[results[k][0] for k in names], *[results[k][1] for k in names],
            *[results[k][2] for k in names], *[results[k][3] for k in names])
```

```python
import math

import jax
import jax.numpy as jnp
from jax import lax
from jax.experimental import pallas as pl
from jax.experimental.pallas import tpu as pltpu

F32 = jnp.float32
BF16 = jnp.bfloat16

N_DEV = 8
DEPTH = 2
D_MODEL = 1024
D_A = 512
D_B = 512
D_FF = 2816
D_IN = 4608
N_HEADS = 4
HEAD = 128
GMLP_BLOCK = 128
CAUSAL_CHUNK = 64
OFF_U, OFF_V, OFF_BG, OFF_CG, OFF_HB, OFF_GA, OFF_GB = 0, 512, 1024, 1536, 2048, 2560, 3584
RMS_EPS = 1e-6
LN_EPS = 1e-5
ADAM_LR, ADAM_B1, ADAM_B2, ADAM_EPS, ADAM_WD, ADAM_STEP = 0.001, 0.9, 0.999, 1e-08, 0.01, 10

SUBLANES, LANES = 8, 128
MATMUL_CHUNK = 512
HALO = 16
FFN_CHUNK = 256
SG_ROWS, SG_W, SG_LAYER = 40, D_FF, 16
ROW_NORM1, ROW_BGATE, ROW_LN_G, ROW_LN_B, ROW_SCONV, ROW_NORM2, ROW_FCONV, ROW_BFCONV = 0, 1, 2, 3, 4, 7, 8, 11
ROW_FINAL, ROW_LOSS = 32, 33
V7X_VMEM_BYTES = 64 << 20
VMEM_LIMIT = V7X_VMEM_BYTES - (8 << 20)
MESH = pl.DeviceIdType.MESH
GELU_C0 = 0.7978845608028654
GELU_C1 = 0.044715
NT = (((1,), (1,)), ((), ()))
TN = (((0,), (0,)), ((), ()))


def _dot(a, b):
    return jnp.dot(a, b, preferred_element_type=F32)


def _dot_nt(a, b):
    return lax.dot_general(a, b, NT, preferred_element_type=F32)


def _sigmoid(x):
    return 1.0 / (1.0 + jnp.exp(-x))


def _gelu_tanh(x):
    return jnp.tanh(GELU_C0 * (x + GELU_C1 * x * x * x))


def _gelu_grad(x, t):
    return 0.5 * (1.0 + t) + 0.5 * x * (1.0 - t * t) * GELU_C0 * (1.0 + 3.0 * GELU_C1 * x * x)


def _sublane_tile(dtype):
    return SUBLANES * (4 // jnp.dtype(dtype).itemsize)


def _shift_down(a, k, prev):
    p = prev.shape[0]
    r = pltpu.roll(a, k, 0)
    sub = _sublane_tile(a.dtype)
    head = r[0:sub]
    rid = lax.broadcasted_iota(jnp.int32, head.shape, 0)
    for j in range(k):
        head = jnp.where(rid == j, prev[p - k + j:p - k + j + 1, :], head)
    return jnp.concatenate([head, r[sub:]], axis=0)


def _shift_up(a, k, nxt):
    t = a.shape[0]
    r = pltpu.roll(a, t - k, 0)
    sub = _sublane_tile(a.dtype)
    tail = r[t - sub:t]
    rid = lax.broadcasted_iota(jnp.int32, tail.shape, 0)
    for j in range(k):
        tail = jnp.where(rid == sub - k + j, nxt[j:j + 1, :], tail)
    return jnp.concatenate([r[0:t - sub], tail], axis=0)


def _column_sums(p):
    if p.dtype.itemsize < 4:
        t = p.shape[0]
        p = p[:t // 2] + p[t // 2:]
        p = p[:t // 4] + p[t // 4:]
    return jnp.sum(p.astype(F32), axis=0, keepdims=True)


def _sheet_begin(step, sheet_in, sheet_out, first_row, rows):
    @pl.when(step == 0)
    def _():
        sheet_out[...] = sheet_in[...]
        sheet_out[first_row:first_row + rows, :] = jnp.zeros((rows, SG_W), F32)


def _sheet_spec():
    return pl.BlockSpec((SG_ROWS, SG_W), lambda i: (0, 0))


def _spatial_mask(transposed):
    ri = lax.broadcasted_iota(jnp.int32, (GMLP_BLOCK, GMLP_BLOCK), 0) // CAUSAL_CHUNK
    ci = lax.broadcasted_iota(jnp.int32, (GMLP_BLOCK, GMLP_BLOCK), 1) // CAUSAL_CHUNK
    return (ri <= ci) if transposed else (ci <= ri)


def _gmlp_forward(u, v, ln_g, ln_b, ws_ref, bs_ref, f_scr):
    tm = u.shape[0]
    tu = _gelu_tanh(u)
    tv = _gelu_tanh(v)
    gu = 0.5 * u * (1.0 + tu)
    gv = 0.5 * v * (1.0 + tv)
    mu = jnp.mean(gv, axis=-1, keepdims=True)
    cen = gv - mu
    rstd = lax.rsqrt(jnp.mean(cen * cen, axis=-1, keepdims=True) + LN_EPS)
    xh = cen * rstd
    vn = (xh * ln_g + ln_b).astype(BF16)
    mask = _spatial_mask(False)
    wm = [jnp.where(mask, ws_ref[h], 0.0).astype(BF16) for h in range(N_HEADS)]
    for b in range(tm // GMLP_BLOCK):
        rows = slice(b * GMLP_BLOCK, (b + 1) * GMLP_BLOCK)
        for h in range(N_HEADS):
            cols = slice(h * HEAD, (h + 1) * HEAD)
            f_scr[rows, cols] = (_dot(wm[h], vn[rows, cols]) + bs_ref[h]).astype(f_scr.dtype)
    return gu, tu, tv, xh, rstd, vn, f_scr[...]


def _position():
    return lax.axis_index("x"), lax.axis_index("y"), lax.axis_index("c")


def _handshake(peers):
    barrier = pltpu.get_barrier_semaphore()
    for peer in peers:
        pl.semaphore_signal(barrier, inc=1, device_id=peer, device_id_type=MESH)
    pl.semaphore_wait(barrier, len(peers))


class _Gather:
    collective_id = 1

    def __init__(self, arrays, relay_back=3):
        self.arrays = list(arrays)
        self.out_shape = [jax.ShapeDtypeStruct((N_DEV,) + a.shape, a.dtype) for a in self.arrays]
        self.base = 0
        self.relay_back = relay_back

    def barrier(self):
        x, y, c = _position()
        _handshake([(x, y, 1 - c), (1 - x, y, c), (x, 1 - y, c), (1 - x, 1 - y, c)])

    def _plan(self, ins, outs, sems):
        send_sems, recv_sems, local_sems = sems
        x, y, c = _position()
        me, sibling = (x, y, c), (x, y, 1 - c)
        chips = [(1 - x, y), (x, 1 - y), (1 - x, 1 - y)]

        def slot(a, p):
            return outs[a].at[4 * p[0] + 2 * p[1] + p[2]]

        def copy(a, k, block, to, src=None):
            return pltpu.make_async_remote_copy(
                src_ref=slot(a, block) if src is None else src, dst_ref=slot(a, block),
                send_sem=send_sems.at[self.base + a, k], recv_sem=recv_sems.at[self.base + a, k],
                device_id=to, device_id_type=MESH)

        n = len(self.arrays)

        def mine():
            return [pltpu.make_async_copy(ins[a], slot(a, me), local_sems.at[self.base + a]) for a in range(n)]

        def first():
            out = []
            for a in range(n):
                out.append(copy(a, 0, me, sibling, src=ins[a]))
                out += [copy(a, 1 + j, me, (*chip, c), src=ins[a]) for j, chip in enumerate(chips)]
            return out

        def arrivals():
            return [copy(a, 1 + j, (*chip, c), me) for j, chip in enumerate(chips) for a in range(n)]

        def relays():
            return [copy(a, 4 + j, (*chip, c), sibling) for j, chip in enumerate(chips) for a in range(n)]

        def from_sibling():
            out = [copy(a, 0, sibling, me) for a in range(n)]
            return out + [copy(a, 4 + j, (*chip, 1 - c), me) for j, chip in enumerate(chips) for a in range(n)]

        return mine, first, arrivals, relays, from_sibling

    def start(self, ins, outs, sems):
        mine, first, _, _, _ = self._plan(ins, outs, sems)
        for cp in mine() + first():
            cp.start()

    def relay(self, ins, outs, sems):
        _, _, arrivals, relays, _ = self._plan(ins, outs, sems)
        for arrived, onward in zip(arrivals(), relays()):
            arrived.wait_recv()
            onward.start()

    def finish(self, ins, outs, sems):
        mine, first, _, relays, from_sibling = self._plan(ins, outs, sems)
        for cp in from_sibling():
            cp.wait_recv()
        for cp in first() + relays():
            cp.wait_send()
        for cp in mine():
            cp.wait()


class _Exchange:
    collective_id = 0

    def __init__(self, arrays):
        self.arrays = list(arrays)
        self.out_shape = [jax.ShapeDtypeStruct(a.shape, a.dtype) for a in self.arrays]
        self.base = 0

    def barrier(self):
        x, y, c = _position()
        _handshake([(x ^ dx, y ^ dy, c ^ dc) for dx in (0, 1) for dy in (0, 1) for dc in (0, 1) if dx or dy or dc])

    def _plan(self, ins, outs, sems):
        send_sems, recv_sems, local_sems = sems
        x, y, c = _position()
        my_idx = 4 * x + 2 * y + c
        n = len(self.arrays)
        offsets = [(dx, dy, dc) for dx in (0, 1) for dy in (0, 1) for dc in (0, 1) if (dx, dy, dc) != (0, 0, 0)]

        def mine():
            return [pltpu.make_async_copy(ins[a].at[my_idx], outs[a].at[my_idx], local_sems.at[self.base + a])
                    for a in range(n)]

        def remote(arriving):
            out = []
            for k, (dx, dy, dc) in enumerate(offsets):
                px, py, pc = x ^ dx, y ^ dy, c ^ dc
                p_idx = 4 * px + 2 * py + pc
                for a in range(n):
                    out.append(pltpu.make_async_remote_copy(
                        src_ref=ins[a].at[p_idx], dst_ref=outs[a].at[p_idx if arriving else my_idx],
                        send_sem=send_sems.at[self.base + a, k], recv_sem=recv_sems.at[self.base + a, k],
                        device_id=(px, py, pc), device_id_type=MESH))
            return out

        return mine, remote

    def start(self, ins, outs, sems):
        mine, remote = self._plan(ins, outs, sems)
        for cp in mine() + remote(False):
            cp.start()

    def relay(self, ins, outs, sems):
        pass

    def finish(self, ins, outs, sems):
        mine, remote = self._plan(ins, outs, sems)
        for cp in remote(True):
            cp.wait_recv()
        for cp in remote(False):
            cp.wait_send()
        for cp in mine():
            cp.wait()


def _exchange_start(parts, thru, *, name):
    hbm = pl.BlockSpec(memory_space=pltpu.HBM)
    sem = pl.BlockSpec(memory_space=pltpu.SEMAPHORE)
    offsets = [(dx, dy, dc) for dx in (0, 1) for dy in (0, 1) for dc in (0, 1) if dx or dy or dc]

    def body(parts_ref, thru_ref, send_sems, recv_sems, parts_thru, land_ref, thru_out):
        x, y, c = _position()
        my_idx = 4 * x + 2 * y + c
        pltpu.make_async_copy(parts_ref.at[my_idx], land_ref.at[my_idx], send_sems.at[7]).start()
        _handshake([(x ^ dx, y ^ dy, c ^ dc) for dx, dy, dc in offsets])
        for k, (dx, dy, dc) in enumerate(offsets):
            px, py, pc = x ^ dx, y ^ dy, c ^ dc
            pltpu.make_async_remote_copy(
                src_ref=parts_ref.at[4 * px + 2 * py + pc], dst_ref=land_ref.at[my_idx],
                send_sem=send_sems.at[k], recv_sem=recv_sems.at[k],
                device_id=(px, py, pc), device_id_type=MESH).start()
        thru_out[...] = thru_ref[...]

    return pl.pallas_call(
        body, name=name,
        out_shape=(pltpu.SemaphoreType.DMA((8,)), pltpu.SemaphoreType.DMA((7,)), pltpu.HBM(parts.shape, parts.dtype),
                   pltpu.HBM(parts.shape, parts.dtype), jax.ShapeDtypeStruct(thru.shape, thru.dtype)),
        in_specs=(hbm, pl.BlockSpec(memory_space=pltpu.VMEM)),
        out_specs=(sem, sem, hbm, hbm, pl.BlockSpec(memory_space=pltpu.VMEM)),
        input_output_aliases={0: 2},
        compiler_params=pltpu.CompilerParams(has_side_effects=pltpu.SideEffectType.DATAFLOW_SIDE_EFFECTING,
                                             collective_id=2),
    )(pltpu.with_memory_space_constraint(parts, pltpu.HBM), thru)


def _exchange_wait(send_sems, recv_sems, parts_thru, land_thru, after, *, name):
    hbm = pl.BlockSpec(memory_space=pltpu.HBM)
    sem = pl.BlockSpec(memory_space=pltpu.SEMAPHORE)
    offsets = [(dx, dy, dc) for dx in (0, 1) for dy in (0, 1) for dc in (0, 1) if dx or dy or dc]

    def body(parts_ref, land_ref, send_sems, recv_sems, after_ref, parts_dead, got_ref):
        x, y, c = _position()
        my_idx = 4 * x + 2 * y + c
        pltpu.make_async_copy(parts_ref.at[my_idx], land_ref.at[my_idx], send_sems.at[7]).wait()
        for k, (dx, dy, dc) in enumerate(offsets):
            px, py, pc = x ^ dx, y ^ dy, c ^ dc
            p_idx = 4 * px + 2 * py + pc
            copy = pltpu.make_async_remote_copy(
                src_ref=parts_ref.at[p_idx], dst_ref=land_ref.at[p_idx], send_sem=send_sems.at[k],
                recv_sem=recv_sems.at[k], device_id=(px, py, pc), device_id_type=MESH)
            copy.wait_send()
            copy.wait_recv()

    return pl.pallas_call(
        body, name=name,
        out_shape=(pltpu.HBM(parts_thru.shape, parts_thru.dtype), pltpu.HBM(land_thru.shape, land_thru.dtype)),
        in_specs=(hbm, hbm, sem, sem, pl.BlockSpec(memory_space=pl.ANY)), out_specs=(hbm, hbm),
        input_output_aliases={0: 0, 1: 1},
        compiler_params=pltpu.CompilerParams(has_side_effects=pltpu.SideEffectType.DATAFLOW_SIDE_EFFECTING),
    )(parts_thru, land_thru, send_sems, recv_sems, after)[1]


class _Both:
    def __init__(self, *carries):
        self.carries = carries
        self.arrays = [a for c in carries for a in c.arrays]
        self.out_shape = [s for c in carries for s in c.out_shape]
        first = 0
        for c in carries:
            c.base = first
            first += len(c.arrays)
        self.collective_id = min(c.collective_id for c in carries)

    def barrier(self):
        min(self.carries, key=lambda c: c.collective_id).barrier()

    def _each(self, method, ins, outs, sems):
        for c in self.carries:
            rows = slice(c.base, c.base + len(c.arrays))
            getattr(c, method)(ins[rows], outs[rows], sems)

    def start(self, ins, outs, sems):
        self._each("start", ins, outs, sems)

    def relay(self, ins, outs, sems):
        self._each("relay", ins, outs, sems)

    def finish(self, ins, outs, sems):
        self._each("finish", ins, outs, sems)


def _call(body, *, name, grid, in_specs, out_specs, out_shape, args, scratch_shapes=(), carry=None):
    n_in, n_out, n_scr = len(in_specs), len(out_specs), len(scratch_shapes)
    params = pltpu.CompilerParams(dimension_semantics=("arbitrary",) * len(grid), vmem_limit_bytes=VMEM_LIMIT)
    if carry is None:
        outs = pl.pallas_call(body, name=name, grid=grid, in_specs=in_specs, out_specs=out_specs, out_shape=out_shape,
                              scratch_shapes=list(scratch_shapes), compiler_params=params)(*args)
        return outs, None
    m = len(carry.arrays)
    total = math.prod(grid)

    def wrapped(*refs):
        ins, refs = refs[:n_in], refs[n_in:]
        c_ins, refs = refs[:m], refs[m:]
        outs, refs = refs[:n_out], refs[n_out:]
        c_outs, refs = refs[:m], refs[m:]
        scr, sems = refs[:n_scr], refs[n_scr:]
        flat = pl.program_id(0)
        for d in range(1, len(grid)):
            flat = flat * grid[d] + pl.program_id(d)

        @pl.when(flat == 0)
        def _():
            carry.barrier()
            carry.start(c_ins, c_outs, sems)

        body(*ins, *outs, *scr)

        @pl.when(flat == max(total - getattr(carry, "relay_back", 3), 0))
        def _():
            carry.relay(c_ins, c_outs, sems)

        @pl.when(flat == total - 1)
        def _():
            carry.finish(c_ins, c_outs, sems)

    any_spec = pl.BlockSpec(memory_space=pl.ANY)
    sem_shapes = [pltpu.SemaphoreType.DMA((m, 7)), pltpu.SemaphoreType.DMA((m, 7)), pltpu.SemaphoreType.DMA((m,))]
    params = pltpu.CompilerParams(dimension_semantics=("arbitrary",) * len(grid), vmem_limit_bytes=VMEM_LIMIT,
                                  collective_id=carry.collective_id)
    outs = pl.pallas_call(
        wrapped, name=name, grid=grid,
        in_specs=list(in_specs) + [any_spec] * m, out_specs=list(out_specs) + [any_spec] * m,
        out_shape=list(out_shape) + carry.out_shape,
        scratch_shapes=list(scratch_shapes) + sem_shapes, compiler_params=params)(*args, *carry.arrays)
    return outs[:n_out], outs[n_out:]


def _gather_now(arrays, *, name):
    carry = _Gather(arrays)
    m = len(arrays)

    def body(*refs):
        ins, outs, sems = refs[:m], refs[m:2 * m], refs[2 * m:]
        carry.barrier()
        carry.start(ins, outs, sems)
        carry.relay(ins, outs, sems)
        carry.finish(ins, outs, sems)

    any_spec = pl.BlockSpec(memory_space=pl.ANY)
    return pl.pallas_call(
        body, name=name, in_specs=[any_spec] * m, out_specs=[any_spec] * m, out_shape=carry.out_shape,
        scratch_shapes=[pltpu.SemaphoreType.DMA((m, 7)), pltpu.SemaphoreType.DMA((m, 7)),
                        pltpu.SemaphoreType.DMA((m,))],
        compiler_params=pltpu.CompilerParams(collective_id=carry.collective_id),
    )(*arrays)


def _sum_gathered(arrs, late, *, name):
    n = len(arrs)

    def body(*refs):
        late_ref, ins, late_out, outs = refs[0], refs[1:1 + n], refs[1 + n], refs[2 + n:2 + 2 * n]
        buf, send_sems, recv_sems = refs[2 + 2 * n:]
        x, y, c = _position()
        me, sibling = (x, y, c), (x, y, 1 - c)
        chips = [(1 - x, y), (x, 1 - y), (1 - x, 1 - y)]
        _handshake([sibling] + [(*chip, c) for chip in chips])

        def copy(k, block, to, src=None):
            slot = buf.at[4 * block[0] + 2 * block[1] + block[2]]
            return pltpu.make_async_remote_copy(
                src_ref=slot if src is None else src, dst_ref=slot,
                send_sem=send_sems.at[k], recv_sem=recv_sems.at[k], device_id=to, device_id_type=MESH)

        first = [copy(0, me, sibling, src=late_ref)]
        first += [copy(1 + j, me, (*chip, c), src=late_ref) for j, chip in enumerate(chips)]
        for cp in first:
            cp.start()
        for in_ref, out_ref in zip(ins, outs):
            acc = in_ref[0]
            for s in range(1, N_DEV):
                acc = acc + in_ref[s]
            out_ref[...] = acc
        passed = []
        for j, chip in enumerate(chips):
            copy(1 + j, (*chip, c), me).wait_recv()
            cp = copy(4 + j, (*chip, c), sibling)
            cp.start()
            passed.append(cp)
        copy(0, sibling, me).wait_recv()
        for j, chip in enumerate(chips):
            copy(4 + j, (*chip, 1 - c), me).wait_recv()
        for cp in first + passed:
            cp.wait_send()
        my_idx = 4 * x + 2 * y + c
        acc = jnp.zeros(late_ref.shape, F32)
        for s in range(N_DEV):
            acc = acc + jnp.where(my_idx == s, late_ref[...], buf[s])
        late_out[...] = acc

    vmem = pl.BlockSpec(memory_space=pltpu.VMEM)
    outs = pl.pallas_call(
        body, name=name, in_specs=[vmem] * (n + 1), out_specs=[vmem] * (n + 1),
        out_shape=[jax.ShapeDtypeStruct(late.shape, F32)] + [jax.ShapeDtypeStruct(a.shape[1:], F32) for a in arrs],
        scratch_shapes=[pltpu.VMEM((N_DEV,) + late.shape, F32), pltpu.SemaphoreType.DMA((7,)),
                        pltpu.SemaphoreType.DMA((7,))],
        compiler_params=pltpu.CompilerParams(vmem_limit_bytes=VMEM_LIMIT, collective_id=_Gather.collective_id),
    )(late, *arrs)
    return list(outs[1:]) + [outs[0]]


def _norm_matmul(x, g, w_t, *, tm, name, carry=None):
    n, d = x.shape
    c = w_t.shape[0]
    ch = MATMUL_CHUNK

    def body(x_ref, g_ref, wt_ref, h_ref, z_ref):
        xv = x_ref[...]
        r = lax.rsqrt(jnp.mean(xv * xv, axis=-1, keepdims=True) + RMS_EPS)
        h = (xv * r * g_ref[...]).astype(BF16)
        h_ref[...] = h
        for c0 in range(0, c, ch):
            z_ref[:, c0:c0 + ch] = _dot_nt(h, wt_ref[c0:c0 + ch, :]).astype(BF16)

    return _call(
        body, name=name, grid=(n // tm,), carry=carry,
        in_specs=[pl.BlockSpec((tm, d), lambda i: (i, 0)),
                  pl.BlockSpec((1, d), lambda i: (0, 0)),
                  pl.BlockSpec((c, d), lambda i: (0, 0))],
        out_specs=[pl.BlockSpec((tm, d), lambda i: (i, 0)),
                   pl.BlockSpec((tm, c), lambda i: (i, 0))],
        out_shape=[jax.ShapeDtypeStruct((n, d), BF16), jax.ShapeDtypeStruct((n, c), BF16)],
        args=(x, g.reshape(1, d), w_t))


def _mix_forward(z, x, b_gate, ln_g, ln_b, w_s, b_s, w_sc, wb, w_out, *, tm, name, carry=None):
    n = z.shape[0]
    hb = tm // HALO

    def body(z_ref, zp_ref, x_ref, bg_ref, lng_ref, lnb_ref, ws_ref, bs_ref, wsc_ref, wb_ref, wo_ref,
             ya_ref, yb_ref, cv_ref, pa_ref, pb_ref, mg_ref, x1_ref, f_scr):
        i = pl.program_id(0)
        u = z_ref[:, OFF_U:OFF_U + D_A]
        v = z_ref[:, OFF_V:OFF_V + D_A].astype(F32)
        gu, _, _, _, _, _, f = _gmlp_forward(u, v, lng_ref[...], lnb_ref[...], ws_ref, bs_ref, f_scr)
        ya = gu * f
        ya_ref[...] = ya

        q = z_ref[:, OFF_CG:OFF_CG + D_B] * z_ref[:, OFF_HB:OFF_HB + D_B]
        qp = zp_ref[:, OFF_CG:OFF_CG + D_B] * zp_ref[:, OFF_HB:OFF_HB + D_B]
        qp = jnp.where(i > 0, qp, jnp.zeros_like(qp))
        w = wsc_ref[...].astype(BF16)
        conv = w[0:1] * _shift_down(q, 2, qp) + w[1:2] * _shift_down(q, 1, qp) + w[2:3] * q
        cv_ref[...] = conv
        yb = z_ref[:, OFF_BG:OFF_BG + D_B] * conv
        yb_ref[...] = yb

        pa = _dot(ya, wb_ref[0]).astype(BF16)
        pb = _dot(yb, wb_ref[1]).astype(BF16)
        pa_ref[...] = pa
        pb_ref[...] = pb
        bg = bg_ref[...].astype(BF16)
        sa = _sigmoid(z_ref[:, OFF_GA:OFF_GA + D_MODEL] + bg[:, 0:D_MODEL])
        sb = _sigmoid(z_ref[:, OFF_GB:OFF_GB + D_MODEL] + bg[:, D_MODEL:2 * D_MODEL])
        mg = sa * pa + sb * pb
        mg_ref[...] = mg
        x1_ref[...] = x_ref[...] + _dot(mg, wo_ref[...])

    row = lambda w: pl.BlockSpec((tm, w), lambda i: (i, 0))
    full = lambda *s: pl.BlockSpec(s, lambda i: (0,) * len(s))
    bf = lambda w: jax.ShapeDtypeStruct((n, w), BF16)
    return _call(
        body, name=name, grid=(n // tm,), carry=carry,
        in_specs=[row(D_IN),
                  pl.BlockSpec((HALO, D_IN), lambda i: (jnp.maximum(i * hb - 1, 0), 0)),
                  row(D_MODEL), full(1, 2 * D_MODEL), full(1, D_A), full(1, D_A),
                  full(N_HEADS, GMLP_BLOCK, GMLP_BLOCK), full(N_HEADS, GMLP_BLOCK, 1), full(3, D_B),
                  full(2, D_A, D_MODEL), full(D_MODEL, D_MODEL)],
        out_specs=[row(D_A), row(D_B), row(D_B), row(D_MODEL), row(D_MODEL), row(D_MODEL), row(D_MODEL)],
        out_shape=[bf(D_A), bf(D_B), bf(D_B), bf(D_MODEL), bf(D_MODEL), bf(D_MODEL),
                   jax.ShapeDtypeStruct((n, D_MODEL), F32)],
        scratch_shapes=[pltpu.VMEM((tm, D_A), BF16)],
        args=(z, z, x, b_gate.reshape(1, -1), ln_g.reshape(1, -1), ln_b.reshape(1, -1), w_s,
              b_s.reshape(N_HEADS, GMLP_BLOCK, 1), w_sc, wb, w_out))


def _loss_tile(xv, gv, tv):
    d = xv.shape[-1]
    r = lax.rsqrt(jnp.mean(xv * xv, axis=-1, keepdims=True) + RMS_EPS)
    xh = xv * r
    e = xh * gv - tv
    per_row = jnp.sum(e * e, axis=-1, keepdims=True) * (0.5 / d)
    dy = e * (1.0 / d)
    dxh = dy * gv
    dx = r * (dxh - xh * jnp.mean(dxh * xh, axis=-1, keepdims=True))
    return dx, jnp.sum(per_row, axis=0, keepdims=True), jnp.sum(dy * xh, axis=0, keepdims=True)


def _ffn_forward(up, x1, w_fc, b_fc, w_down, *, tm, name, carry=None, head=None):
    n = up.shape[0]
    hb = tm // HALO
    n_in = 6 if head is None else 8

    def body(*refs):
        up_ref, upp_ref, x1_ref, wfc_ref, bfc_ref, wd_ref = refs[:6]
        gc_ref, a_ref, out_ref = refs[n_in:n_in + 3]
        acc = refs[-1]
        i = pl.program_id(0)
        acc[...] = x1_ref[...]
        for c0 in range(0, D_FF, FFN_CHUNK):
            cols = slice(c0, c0 + FFN_CHUNK)
            gate = up_ref[:, cols]
            val = up_ref[:, D_FF + c0:D_FF + c0 + FFN_CHUNK]
            gp = upp_ref[:, cols]
            gp = jnp.where(i > 0, gp, jnp.zeros_like(gp))
            w = wfc_ref[:, cols].astype(BF16)
            gc = (w[0:1] * _shift_down(gate, 2, gp) + w[1:2] * _shift_down(gate, 1, gp) + w[2:3] * gate
                  + bfc_ref[:, cols].astype(BF16))
            gc_ref[:, cols] = gc
            a = gc * _sigmoid(gc) * val
            a_ref[:, cols] = a
            acc[...] += _dot(a, wd_ref[cols, :])
        if head is None:
            out_ref[...] = acc[...]
        else:
            g_ref, t_ref = refs[6:8]
            sg_ref = refs[n_in + 3]

            @pl.when(i == 0)
            def _():
                sg_ref[...] = jnp.zeros_like(sg_ref)

            dx, loss, dg = _loss_tile(acc[...], g_ref[...], t_ref[...])
            out_ref[...] = dx
            sg_ref[ROW_LOSS:ROW_LOSS + 1, 0:LANES] += jnp.broadcast_to(loss, (1, LANES))
            sg_ref[ROW_FINAL:ROW_FINAL + 1, 0:D_MODEL] += dg

    row = lambda w: pl.BlockSpec((tm, w), lambda i: (i, 0))
    full = lambda r, c: pl.BlockSpec((r, c), lambda i: (0, 0))
    in_specs = [row(2 * D_FF), pl.BlockSpec((HALO, D_FF), lambda i: (jnp.maximum(i * hb - 1, 0), 0)), row(D_MODEL),
                full(3, D_FF), full(1, D_FF), full(D_FF, D_MODEL)]
    out_specs = [row(D_FF), row(D_FF), row(D_MODEL)]
    out_shape = [jax.ShapeDtypeStruct((n, D_FF), BF16), jax.ShapeDtypeStruct((n, D_FF), BF16),
                 jax.ShapeDtypeStruct((n, D_MODEL), F32)]
    args = (up, up, x1, w_fc, b_fc.reshape(1, -1), w_down)
    if head is not None:
        in_specs += [full(1, D_MODEL), row(D_MODEL)]
        out_specs += [full(SG_ROWS, SG_W)]
        out_shape += [jax.ShapeDtypeStruct((SG_ROWS, SG_W), F32)]
        args += (head[0].reshape(1, -1), head[1])
    return _call(body, name=name, grid=(n // tm,), carry=carry, in_specs=in_specs, out_specs=out_specs,
                 out_shape=out_shape, scratch_shapes=[pltpu.VMEM((tm, D_MODEL), F32)], args=args)


def _ffn_backward(dx2, up, gc, w_fc, w_down, sheet, layer, *, tm, name, carry=None):
    n = up.shape[0]
    steps = n // tm
    hb = tm // HALO
    row = SG_LAYER * layer + ROW_FCONV

    def body(dx_ref, dxn_ref, up_ref, upn_ref, gc_ref, gcn_ref, wfc_ref, wd_ref, sg_in, dup_ref, sg_ref):
        i = pl.program_id(0)
        last = i == steps - 1
        _sheet_begin(i, sg_in, sg_ref, row, 4)

        dxe = jnp.concatenate([dx_ref[...], dxn_ref[...]], axis=0).astype(BF16)
        for c0 in range(0, D_FF, FFN_CHUNK):
            cols = slice(c0, c0 + FFN_CHUNK)
            vcols = slice(D_FF + c0, D_FF + c0 + FFN_CHUNK)
            dae = _dot_nt(dxe, wd_ref[cols, :])
            da, dan = dae[:tm], dae[tm:]
            gate = up_ref[:, cols]
            val = up_ref[:, vcols]
            gcv = gc_ref[:, cols]
            s = _sigmoid(gcv)
            dab = da.astype(BF16)
            dup_ref[:, vcols] = dab * (gcv * s)
            dgc = dab * val * (s * (1.0 + gcv * (1.0 - s)))
            gcn = gcn_ref[:, cols]
            sn = _sigmoid(gcn)
            dgcn = dan.astype(BF16) * upn_ref[:, vcols] * (sn * (1.0 + gcn * (1.0 - sn)))
            dgcn = jnp.where(last, jnp.zeros_like(dgcn), dgcn)
            up1 = _shift_up(dgc, 1, dgcn)
            up2 = _shift_up(dgc, 2, dgcn)
            w = wfc_ref[:, cols].astype(BF16)
            dup_ref[:, cols] = w[2:3] * dgc + w[1:2] * up1 + w[0:1] * up2
            sg_ref[row:row + 1, cols] += _column_sums(gate * up2)
            sg_ref[row + 1:row + 2, cols] += _column_sums(gate * up1)
            sg_ref[row + 2:row + 3, cols] += _column_sums(gate * dgc)
            sg_ref[row + 3:row + 4, cols] += _column_sums(dgc)

    nxt = lambda i: (jnp.minimum((i + 1) * hb, steps * hb - 1), 0)
    return _call(
        body, name=name, grid=(steps,), carry=carry,
        in_specs=[pl.BlockSpec((tm, D_MODEL), lambda i: (i, 0)),
                  pl.BlockSpec((HALO, D_MODEL), nxt),
                  pl.BlockSpec((tm, 2 * D_FF), lambda i: (i, 0)),
                  pl.BlockSpec((HALO, 2 * D_FF), nxt),
                  pl.BlockSpec((tm, D_FF), lambda i: (i, 0)),
                  pl.BlockSpec((HALO, D_FF), nxt),
                  pl.BlockSpec((3, D_FF), lambda i: (0, 0)),
                  pl.BlockSpec((D_FF, D_MODEL), lambda i: (0, 0)), _sheet_spec()],
        out_specs=[pl.BlockSpec((tm, 2 * D_FF), lambda i: (i, 0)), _sheet_spec()],
        out_shape=[jax.ShapeDtypeStruct((n, 2 * D_FF), BF16), jax.ShapeDtypeStruct((SG_ROWS, SG_W), F32)],
        args=(dx2, dx2, up, up, gc, gc, w_fc, w_down, sheet))


def _matmul_norm_backward(dz, w_t, x, g, dres, sheet, row, *, tm, name, carry=None):
    n, c = dz.shape
    d = x.shape[1]
    ch = MATMUL_CHUNK

    def body(dz_ref, wt_ref, x_ref, g_ref, dres_ref, *rest):
        i = pl.program_id(0)
        if sheet is None:
            dx_ref, sg_ref = rest

            @pl.when(i == 0)
            def _():
                sg_ref[...] = jnp.zeros_like(sg_ref)
        else:
            sg_in, dx_ref, sg_ref = rest
            _sheet_begin(i, sg_in, sg_ref, row, 1)

        dh = _dot(dz_ref[:, 0:ch], wt_ref[0:ch, :])
        for c0 in range(ch, c, ch):
            dh += _dot(dz_ref[:, c0:c0 + ch], wt_ref[c0:c0 + ch, :])
        xv = x_ref[...]
        r = lax.rsqrt(jnp.mean(xv * xv, axis=-1, keepdims=True) + RMS_EPS)
        xh = xv * r
        sg_ref[row:row + 1, 0:d] += jnp.sum(dh * xh, axis=0, keepdims=True)
        dxh = dh * g_ref[...]
        dx_ref[...] = dres_ref[...] + r * (dxh - xh * jnp.mean(dxh * xh, axis=-1, keepdims=True))

    in_specs = [pl.BlockSpec((tm, c), lambda i: (i, 0)),
                pl.BlockSpec((c, d), lambda i: (0, 0)),
                pl.BlockSpec((tm, d), lambda i: (i, 0)),
                pl.BlockSpec((1, d), lambda i: (0, 0)),
                pl.BlockSpec((tm, d), lambda i: (i, 0))]
    args = (dz, w_t, x, g.reshape(1, d), dres)
    if sheet is None:
        small_spec, small_shape = pl.BlockSpec((8, d), lambda i: (0, 0)), jax.ShapeDtypeStruct((8, d), F32)
    else:
        in_specs, args = in_specs + [_sheet_spec()], args + (sheet,)
        small_spec, small_shape = _sheet_spec(), jax.ShapeDtypeStruct((SG_ROWS, SG_W), F32)
    return _call(
        body, name=name, grid=(n // tm,), carry=carry, in_specs=in_specs,
        out_specs=[pl.BlockSpec((tm, d), lambda i: (i, 0)), small_spec],
        out_shape=[jax.ShapeDtypeStruct((n, d), F32), small_shape], args=args)


def _mix_backward(dx1, z, conv, pa, pb, b_gate, ln_g, ln_b, w_s, w_s_t, b_s, w_sc, w_out, wb, sheet, layer, *, tm, name,
                  carry=None):
    n = z.shape[0]
    steps = n // tm
    hb = tm // HALO
    base = SG_LAYER * layer
    r_bg, r_lng, r_lnb, r_sc = base + ROW_BGATE, base + ROW_LN_G, base + ROW_LN_B, base + ROW_SCONV

    def body(dx_ref, dxn_ref, z_ref, zn_ref, cv_ref, pa_ref, pb_ref, bg_ref, lng_ref, lnb_ref, ws_ref, wst_ref,
             bs_ref, wsc_ref, wo_ref, wb_ref, sg_in,
             dz_ref, dpa_ref, dpb_ref, dws_ref, dbs_ref, sg_ref, f_scr, dvn_scr):
        i = pl.program_id(0)
        last = i == steps - 1
        _sheet_begin(i, sg_in, sg_ref, r_bg, ROW_NORM2 - ROW_BGATE)

        @pl.when(i == 0)
        def _():
            dws_ref[...] = jnp.zeros_like(dws_ref)
            dbs_ref[...] = jnp.zeros_like(dbs_ref)

        dxe = jnp.concatenate([dx_ref[...], dxn_ref[...]], axis=0).astype(BF16)
        dmge = _dot_nt(dxe, wo_ref[...])
        dmg, dmgn = dmge[:tm].astype(BF16), dmge[tm:].astype(BF16)

        pa_v = pa_ref[...]
        pb_v = pb_ref[...]
        bg = bg_ref[...].astype(BF16)
        sa = _sigmoid(z_ref[:, OFF_GA:OFF_GA + D_MODEL] + bg[:, 0:D_MODEL])
        sb = _sigmoid(z_ref[:, OFF_GB:OFF_GB + D_MODEL] + bg[:, D_MODEL:2 * D_MODEL])
        dpa = dmg * sa
        dpb = dmg * sb
        dga = dmg * pa_v * sa * (1.0 - sa)
        dgb = dmg * pb_v * sb * (1.0 - sb)
        dpa_ref[...] = dpa
        dpb_ref[...] = dpb
        dz_ref[:, OFF_GA:OFF_GA + D_MODEL] = dga
        dz_ref[:, OFF_GB:OFF_GB + D_MODEL] = dgb
        sg_ref[r_bg:r_bg + 1, 0:D_MODEL] += _column_sums(dga)
        sg_ref[r_bg:r_bg + 1, D_MODEL:2 * D_MODEL] += _column_sums(dgb)

        dya = _dot_nt(dpa, wb_ref[0]).astype(BF16)
        u = z_ref[:, OFF_U:OFF_U + D_A]
        v = z_ref[:, OFF_V:OFF_V + D_A].astype(F32)
        ln_g = lng_ref[...]
        gu, tu, tv, xh, rstd, vn, f = _gmlp_forward(u, v, ln_g, lnb_ref[...], ws_ref, bs_ref, f_scr)
        dgu = dya * f
        df_bf = dya * gu
        dz_ref[:, OFF_U:OFF_U + D_A] = dgu * _gelu_grad(u, tu)
        mask = _spatial_mask(False)
        mask_t = _spatial_mask(True)
        wmt = [jnp.where(mask_t, wst_ref[h], 0.0).astype(BF16) for h in range(N_HEADS)]
        for b in range(tm // GMLP_BLOCK):
            rows = slice(b * GMLP_BLOCK, (b + 1) * GMLP_BLOCK)
            for h in range(N_HEADS):
                cols = slice(h * HEAD, (h + 1) * HEAD)
                dfb = df_bf[rows, cols]
                dvn_scr[rows, cols] = _dot(wmt[h], dfb)
                dws_ref[h] += jnp.where(mask, _dot_nt(dfb, vn[rows, cols]), 0.0)
                dbs_ref[h] += jnp.sum(dfb.astype(F32), axis=1, keepdims=True)
        dvn = dvn_scr[...]
        sg_ref[r_lng:r_lng + 1, 0:D_A] += jnp.sum(dvn * xh, axis=0, keepdims=True)
        sg_ref[r_lnb:r_lnb + 1, 0:D_A] += jnp.sum(dvn, axis=0, keepdims=True)
        dxh = dvn * ln_g
        dgv = rstd * (dxh - jnp.mean(dxh, axis=-1, keepdims=True) - xh * jnp.mean(dxh * xh, axis=-1, keepdims=True))
        dz_ref[:, OFF_V:OFF_V + D_A] = (dgv * _gelu_grad(v, tv)).astype(BF16)

        sbn = _sigmoid(zn_ref[:, OFF_GB:OFF_GB + D_MODEL] + bg[:, D_MODEL:2 * D_MODEL])
        dpbe = jnp.concatenate([dpb, dmgn * sbn], axis=0)
        dybe = _dot_nt(dpbe, wb_ref[1])
        dyb, dybn = dybe[:tm].astype(BF16), dybe[tm:].astype(BF16)
        bgv = z_ref[:, OFF_BG:OFF_BG + D_B]
        cg = z_ref[:, OFF_CG:OFF_CG + D_B]
        hbv = z_ref[:, OFF_HB:OFF_HB + D_B]
        q = cg * hbv
        dz_ref[:, OFF_BG:OFF_BG + D_B] = dyb * cv_ref[...]
        dconv = dyb * bgv
        dconvn = dybn * zn_ref[:, OFF_BG:OFF_BG + D_B]
        dconvn = jnp.where(last, jnp.zeros_like(dconvn), dconvn)
        up1 = _shift_up(dconv, 1, dconvn)
        up2 = _shift_up(dconv, 2, dconvn)
        sg_ref[r_sc:r_sc + 1, 0:D_B] += _column_sums(q * up2)
        sg_ref[r_sc + 1:r_sc + 2, 0:D_B] += _column_sums(q * up1)
        sg_ref[r_sc + 2:r_sc + 3, 0:D_B] += _column_sums(q * dconv)
        w = wsc_ref[...].astype(BF16)
        dq = w[2:3] * dconv + w[1:2] * up1 + w[0:1] * up2
        dz_ref[:, OFF_CG:OFF_CG + D_B] = dq * hbv
        dz_ref[:, OFF_HB:OFF_HB + D_B] = dq * cg

    row = lambda w: pl.BlockSpec((tm, w), lambda i: (i, 0))
    full = lambda *s: pl.BlockSpec(s, lambda i: (0,) * len(s))
    nxt = lambda i: (jnp.minimum((i + 1) * hb, steps * hb - 1), 0)
    return _call(
        body, name=name, grid=(steps,), carry=carry,
        in_specs=[row(D_MODEL), pl.BlockSpec((HALO, D_MODEL), nxt),
                  row(D_IN), pl.BlockSpec((HALO, D_IN), nxt),
                  row(D_B), row(D_MODEL), row(D_MODEL),
                  full(1, 2 * D_MODEL), full(1, D_A), full(1, D_A),
                  full(N_HEADS, GMLP_BLOCK, GMLP_BLOCK), full(N_HEADS, GMLP_BLOCK, GMLP_BLOCK),
                  full(N_HEADS, GMLP_BLOCK, 1), full(3, D_B),
                  full(D_MODEL, D_MODEL), full(2, D_A, D_MODEL), _sheet_spec()],
        out_specs=[row(D_IN), row(D_MODEL), row(D_MODEL), full(N_HEADS, GMLP_BLOCK, GMLP_BLOCK),
                   full(N_HEADS, GMLP_BLOCK, 1), _sheet_spec()],
        out_shape=[jax.ShapeDtypeStruct((n, D_IN), BF16), jax.ShapeDtypeStruct((n, D_MODEL), BF16),
                   jax.ShapeDtypeStruct((n, D_MODEL), BF16),
                   jax.ShapeDtypeStruct((N_HEADS, GMLP_BLOCK, GMLP_BLOCK), F32),
                   jax.ShapeDtypeStruct((N_HEADS, GMLP_BLOCK, 1), F32), jax.ShapeDtypeStruct((SG_ROWS, SG_W), F32)],
        scratch_shapes=[pltpu.VMEM((tm, D_A), BF16), pltpu.VMEM((tm, D_A), F32)],
        args=(dx1, dx1, z, z, conv, pa, pb, b_gate.reshape(1, -1), ln_g.reshape(1, -1), ln_b.reshape(1, -1), w_s, w_s_t,
              b_s.reshape(N_HEADS, GMLP_BLOCK, 1), w_sc, w_out, wb, sheet))


def _matmul_tn(a, b, *, t1, tn, name, carry=None, pieces=1):
    n, k1 = a.shape
    k2 = b.shape[1]
    steps = n // tn
    w = k2 // pieces

    def body(a_ref, b_ref, *rest):
        o_refs, acc = rest[:pieces], rest[pieces]
        s = pl.program_id(1)

        @pl.when(s == 0)
        def _():
            acc[...] = jnp.zeros_like(acc)

        acc[...] += lax.dot_general(a_ref[...].astype(BF16), b_ref[...].astype(BF16), TN, preferred_element_type=F32)

        @pl.when(s == steps - 1)
        def _():
            for c, o_ref in enumerate(o_refs):
                o_ref[...] = acc[:, c * w:(c + 1) * w].astype(BF16)

    outs, carried = _call(
        body, name=name, grid=(k1 // t1, steps), carry=carry,
        in_specs=[pl.BlockSpec((tn, t1), lambda i, s: (s, i)),
                  pl.BlockSpec((tn, k2), lambda i, s: (s, 0))],
        out_specs=[pl.BlockSpec((t1, w), lambda i, s: (i, 0))] * pieces,
        out_shape=[jax.ShapeDtypeStruct((k1, w), BF16)] * pieces,
        scratch_shapes=[pltpu.VMEM((t1, k2), F32)],
        args=(a, b))
    return (outs[0] if pieces == 1 else list(outs)), carried


def _adamw_math(w, g, m, v):
    m = ADAM_B1 * m + (1.0 - ADAM_B1) * g
    v = ADAM_B2 * v + (1.0 - ADAM_B2) * (g * g)
    m_hat = m / (1.0 - ADAM_B1 ** ADAM_STEP)
    v_hat = v / (1.0 - ADAM_B2 ** ADAM_STEP)
    delta = -ADAM_LR * (m_hat / (jnp.sqrt(v_hat) + ADAM_EPS) + ADAM_WD * w)
    return delta, m, v


def _sum_parts(recvs, *, tr, name):
    _, r, c = recvs[0].shape

    def body(*refs):
        recv_refs, g_ref = refs[:DEPTH], refs[DEPTH]
        layer = pl.program_id(0)
        for l in range(DEPTH):
            @pl.when(layer == l)
            def _(l=l):
                g = recv_refs[l][0].astype(F32)
                for s in range(1, N_DEV):
                    g = g + recv_refs[l][s].astype(F32)
                g_ref[0] = g

    outs, _ = _call(
        body, name=name, grid=(DEPTH, r // tr),
        in_specs=[pl.BlockSpec((N_DEV, tr, c), lambda l, i: (0, i, 0))] * DEPTH,
        out_specs=[pl.BlockSpec((1, tr, c), lambda l, i: (l, i, 0))],
        out_shape=[jax.ShapeDtypeStruct((DEPTH, r, c), F32)],
        args=tuple(recvs))
    return outs[0]


def _adamw(w, g, m, v, *, tr, name):
    r, c = w.shape

    def body(w_ref, g_ref, m_ref, v_ref, d_ref, nm_ref, nv_ref):
        delta, nm, nv = _adamw_math(w_ref[...], g_ref[...], m_ref[...], v_ref[...])
        d_ref[...] = delta
        nm_ref[...] = nm
        nv_ref[...] = nv

    spec = pl.BlockSpec((tr, c), lambda i: (i, 0))
    outs, _ = _call(body, name=name, grid=(r // tr,), in_specs=[spec] * 4, out_specs=[spec] * 3,
                    out_shape=[jax.ShapeDtypeStruct((r, c), F32)] * 3, args=(w, g, m, v))
    return outs


def _sum_adamw(recvs, w, m, v, *, tr, name, carry=None):
    _, r, c = w.shape
    blocks = len(recvs[0])
    flat = [piece for layer in recvs for piece in layer]

    def body(*refs):
        recv_refs = refs[:len(flat)]
        w_ref, m_ref, v_ref, g_ref, d_ref, nm_ref, nv_ref = refs[len(flat):]
        layer = pl.program_id(0)
        for l in range(DEPTH):
            @pl.when(layer == l)
            def _(l=l):
                cols = []
                for piece in recv_refs[l * blocks:(l + 1) * blocks]:
                    part = piece[0].astype(F32)
                    for s in range(1, N_DEV):
                        part = part + piece[s].astype(F32)
                    cols.append(part)
                g = cols[0] if blocks == 1 else jnp.concatenate(cols, axis=-1)
                delta, nm, nv = _adamw_math(w_ref[0], g, m_ref[0], v_ref[0])
                g_ref[0] = g
                d_ref[0] = delta
                nm_ref[0] = nm
                nv_ref[0] = nv

    spec = pl.BlockSpec((1, tr, c), lambda l, i: (l, i, 0))
    return _call(
        body, name=name, grid=(DEPTH, r // tr),
        in_specs=[pl.BlockSpec((N_DEV, tr, c // blocks), lambda l, i: (0, i, 0))] * len(flat) + [spec] * 3,
        out_specs=[spec] * 4, out_shape=[jax.ShapeDtypeStruct((DEPTH, r, c), F32)] * 4,
        args=tuple(flat) + (w, m, v), carry=carry)


def _adamw_small(sheet, gain0, extra, params, *, name):
    sheet_rows = dict(norm1_g=ROW_NORM1, b_gate=ROW_BGATE, gmlp_ln_g=ROW_LN_G, gmlp_ln_b=ROW_LN_B, norm2_g=ROW_NORM2,
                      b_ffn_conv=ROW_BFCONV)
    names = list(params)
    extra_names = list(extra)

    def body(*refs):
        sg_ref, gain0_ref, refs = refs[0], refs[1], refs[2:]
        extra_refs, refs = dict(zip(extra_names, refs[:len(extra_names)])), refs[len(extra_names):]
        ins, outs, loss_ref = refs[:3 * len(names)], refs[3 * len(names):-1], refs[-1]
        loss_ref[...] = sg_ref[ROW_LOSS:ROW_LOSS + 1, 0:1]
        for j, key in enumerate(names):
            w_ref, m_ref, v_ref = ins[3 * j:3 * j + 3]
            g_ref, d_ref, nm_ref, nv_ref = outs[4 * j:4 * j + 4]
            if key in extra_refs:
                g_ref[...] = extra_refs[key][...]
            elif key == "final_g":
                g_ref[...] = sg_ref[ROW_FINAL:ROW_FINAL + 1, 0:D_MODEL]
            else:
                width = w_ref.shape[-1]
                for l in range(DEPTH):
                    row = SG_LAYER * l + sheet_rows[key]
                    if key == "norm1_g" and l == 0:
                        g_ref[0:1, :] = gain0_ref[0:1, :]
                    else:
                        g_ref[l:l + 1, :] = sg_ref[row:row + 1, 0:width]
            delta, nm, nv = _adamw_math(w_ref[...], g_ref[...], m_ref[...], v_ref[...])
            d_ref[...] = delta
            nm_ref[...] = nm
            nv_ref[...] = nv

    args = [sheet, gain0] + [extra[k] for k in extra_names] + [t for k in names for t in params[k]]
    vmem = pl.BlockSpec(memory_space=pltpu.VMEM)
    outs = pl.pallas_call(
        body, name=name, in_specs=[vmem] * len(args), out_specs=[vmem] * (4 * len(names) + 1),
        out_shape=[jax.ShapeDtypeStruct(params[k][0].shape, F32) for k in names for _ in range(4)]
        + [jax.ShapeDtypeStruct((1, 1), F32)],
    )(*args)
    return {k: tuple(outs[4 * j:4 * j + 4]) for j, k in enumerate(names)}, outs[-1]


def _rows(gathered):
    return gathered.reshape(N_DEV * gathered.shape[1], gathered.shape[2])


def _parts(full):
    return full.reshape(N_DEV, full.shape[0] // N_DEV, full.shape[1])


def kernel(x, norm1_g, w_in, b_gate, gmlp_ln_g, gmlp_ln_b, w_spatial, b_spatial, w_shortconv, w_branch, w_out, norm2_g, w_ffn_up, w_ffn_conv, b_ffn_conv, w_ffn_down, final_g, loss_target, m_norm1_g, m_w_in, m_b_gate, m_gmlp_ln_g, m_gmlp_ln_b, m_w_spatial, m_b_spatial, m_w_shortconv, m_w_branch, m_w_out, m_norm2_g, m_w_ffn_up, m_w_ffn_conv, m_b_ffn_conv, m_w_ffn_down, m_final_g, v_norm1_g, v_w_in, v_b_gate, v_gmlp_ln_g, v_gmlp_ln_b, v_w_spatial, v_b_spatial, v_w_shortconv, v_w_branch, v_w_out, v_norm2_g, v_w_ffn_up, v_w_ffn_conv, v_b_ffn_conv, v_w_ffn_down, v_final_g):
    n = x.shape[1]
    tm_in, tm, tn = 1024, 512, 2048
    x0 = x.reshape(n, D_MODEL)
    target = loss_target.reshape(n, D_MODEL)
    my_idx = 4 * lax.axis_index("x") + 2 * lax.axis_index("y") + lax.axis_index("c")
    sc_w, fc_w = D_B // N_DEV, D_FF // N_DEV

    sh_in = [w_in[l].T.astype(BF16) for l in range(DEPTH)]
    sh_up = [w_ffn_up[l].T.astype(BF16) for l in range(DEPTH)]
    sh_br = [w_branch[l].astype(BF16) for l in range(DEPTH)]
    sh_out = [w_out[l].astype(BF16) for l in range(DEPTH)]
    sh_down = [w_ffn_down[l].astype(BF16) for l in range(DEPTH)]
    taps = jnp.concatenate([w_shortconv, w_ffn_conv], axis=-1)

    def branch_weights(g):
        return g.transpose(1, 2, 0, 3).reshape(2, D_A, D_MODEL)

    g_in0, g_taps = _gather_now([sh_in[0], taps], name="gather_first")
    w_sc = [g_taps[:, l, :, :sc_w].transpose(1, 0, 2).reshape(3, D_B) for l in range(DEPTH)]
    w_fc = [g_taps[:, l, :, sc_w:].transpose(1, 0, 2).reshape(3, D_FF) for l in range(DEPTH)]
    w_s_t = [w_spatial[l].transpose(0, 2, 1) for l in range(DEPTH)]
    weights = [dict(), dict()]
    weights[0]["in_t"] = _rows(g_in0)
    saved = []
    xc = x0
    for l in range(DEPTH):
        p = weights[l]
        carry = _Gather([sh_br[0], sh_out[0]]) if l == 0 else _Gather([sh_up[1]], relay_back=2)
        (h, z), got = _norm_matmul(xc, norm1_g[l], p["in_t"], tm=tm_in, name=f"fwd_in_{l}", carry=carry)
        if l == 0:
            p["wb"], p["out"] = branch_weights(got[0]), _rows(got[1])
        else:
            p["up_t"] = _rows(got[0])
        carry = _Gather([sh_up[0]]) if l == 0 else None
        (ya, yb, conv, pa, pb, mg, x1), got = _mix_forward(
            z, xc, b_gate[l], gmlp_ln_g[l], gmlp_ln_b[l], w_spatial[l], b_spatial[l], w_sc[l], p["wb"], p["out"],
            tm=tm, name=f"fwd_mix_{l}", carry=carry)
        if l == 0:
            p["up_t"] = _rows(got[0])
        (h2, up), got = _norm_matmul(x1, norm2_g[l], p["up_t"], tm=tm, name=f"fwd_up_{l}", carry=_Gather([sh_down[l]]))
        p["down"] = _rows(got[0])
        carry = _Gather([sh_br[1], sh_out[1], sh_in[1]]) if l == 0 else None
        head = (final_g, target) if l == DEPTH - 1 else None
        outs, got = _ffn_forward(up, x1, w_fc[l], b_ffn_conv[l], p["down"], tm=tm, name=f"fwd_ffn_{l}", carry=carry, head=head)
        if l == 0:
            weights[1]["wb"], weights[1]["out"], weights[1]["in_t"] = branch_weights(got[0]), _rows(got[1]), _rows(got[2])
        gc, a = outs[0], outs[1]
        saved.append(dict(x=xc, h=h, z=z, ya=ya, yb=yb, conv=conv, pa=pa, pb=pb, mg=mg, x1=x1, h2=h2, up=up, gc=gc, a=a))
        xc = outs[2]
    dx, sheet = outs[2], outs[3]

    recv = [dict(), dict()]
    small_dws, small_dbs = [None] * DEPTH, [None] * DEPTH
    pending_in = None
    for l in reversed(range(DEPTH)):
        p, s = weights[l], saved[l]
        carry = _Exchange([pending_in]) if pending_in is not None else None
        (dup, sheet), got = _ffn_backward(dx, s["up"], s["gc"], w_fc[l], p["down"], sheet, l, tm=tm, name=f"bwd_ffn_{l}",
                                          carry=carry)
        if got is not None:
            recv[l + 1]["in_t"] = got[0]
        dw_down, _ = _matmul_tn(s["a"], dx, t1=D_FF // 2, tn=tn, name=f"dw_down_{l}")
        dw_up_t, got = _matmul_tn(dup, s["h2"], t1=2 * D_FF // 4, tn=tn, name=f"dw_up_{l}", pieces=2,
                                  carry=_Exchange([_parts(dw_down)]))
        recv[l]["down"] = got[0]
        (dx1, sheet), got_left = _matmul_norm_backward(
            dup, p["up_t"], s["x1"], norm2_g[l], dx, sheet, SG_LAYER * l + ROW_NORM2, tm=tm, name=f"bwd_up_{l}",
            carry=_Exchange([_parts(dw_up_t[0])]))
        dw_out, _ = _matmul_tn(s["mg"], dx1, t1=D_MODEL, tn=tn, name=f"dw_out_{l}")
        (dz, dpa, dpb, small_dws[l], small_dbs[l], sheet), got_right = _mix_backward(
            dx1, s["z"], s["conv"], s["pa"], s["pb"], b_gate[l], gmlp_ln_g[l], gmlp_ln_b[l], w_spatial[l], w_s_t[l],
            b_spatial[l], w_sc[l], p["out"], p["wb"], sheet, l, tm=tm, name=f"bwd_mix_{l}",
            carry=_Exchange([_parts(dw_up_t[1]), _parts(dw_out)]))
        recv[l]["up_t"], recv[l]["out"] = [got_left[0], got_right[0]], got_right[1]
        dw_bra_t, _ = _matmul_tn(dpa, s["ya"], t1=D_MODEL, tn=tn, name=f"dw_branch_a_{l}")
        dw_brb_t, _ = _matmul_tn(dpb, s["yb"], t1=D_MODEL, tn=tn, name=f"dw_branch_b_{l}")
        carry = _Exchange([_parts(dw_bra_t), _parts(dw_brb_t)])
        if l == 0:
            dbs = jnp.stack([t.reshape(N_HEADS, GMLP_BLOCK) for t in small_dbs]).reshape(DEPTH * N_HEADS, GMLP_BLOCK)
            carry = _Both(carry, _Gather([sheet, small_dws[0], small_dws[1], dbs]))
        dw_in_t, got = _matmul_tn(dz, s["h"], t1=D_IN // 4, tn=tn, name=f"dw_in_{l}", carry=carry)
        recv[l]["bra_t"], recv[l]["brb_t"] = got[:2]
        if l == 0:
            gathered_small = got[2:]
            send_sems, recv_sems, parts_thru, land_thru, gain = _exchange_start(
                _parts(dw_in_t), norm1_g[l].reshape(1, D_MODEL), name="exchange_w_in_0_start")
            (dx0, dg1_first), _ = _matmul_norm_backward(dz, p["in_t"], s["x"], gain.reshape(D_MODEL), dx1, None, 0,
                                                        tm=tm, name=f"bwd_in_{l}")
            recv[0]["in_t"] = _exchange_wait(send_sems, recv_sems, parts_thru, land_thru, dg1_first,
                                             name="exchange_w_in_0_wait")
        else:
            (dx0, sheet), _ = _matmul_norm_backward(dz, p["in_t"], s["x"], norm1_g[l], dx1, sheet,
                                                    SG_LAYER * l + ROW_NORM1, tm=tm, name=f"bwd_in_{l}")
            pending_in = _parts(dw_in_t)
        dx = dx0
    grad_x = dx.reshape(x.shape)

    results = {}
    both = lambda key: [recv[l][key] for l in range(DEPTH)]
    blocks = lambda key: [r if isinstance(r, list) else [r] for r in both(key)]
    swap = lambda t: t.transpose(0, 2, 1)
    for key, slab, (w, m, v), tr in [("w_in", "in_t", (w_in, m_w_in, v_w_in), 192),
                                     ("w_ffn_up", "up_t", (w_ffn_up, m_w_ffn_up, v_w_ffn_up), 176)]:
        outs, _ = _sum_adamw(blocks(slab), swap(w), swap(m), swap(v), tr=tr, name=f"adamw_{key}")
        results[key] = tuple(swap(o) for o in outs)
    g_bra = _sum_parts(both("bra_t"), tr=128, name="sum_w_branch_a").transpose(0, 2, 1)
    g_brb = _sum_parts(both("brb_t"), tr=128, name="sum_w_branch_b").transpose(0, 2, 1)
    g_br = jnp.stack([g_bra, g_brb], axis=1)
    flat = lambda t: t.reshape(-1, t.shape[-1])
    outs = _adamw(flat(w_branch), flat(g_br), flat(m_w_branch), flat(v_w_branch), tr=512, name="adamw_w_branch")
    results["w_branch"] = (g_br,) + tuple(o.reshape(w_branch.shape) for o in outs)
    results["w_out"] = tuple(_sum_adamw(blocks("out"), w_out, m_w_out, v_w_out, tr=128, name="adamw_w_out")[0])
    results["w_ffn_down"] = tuple(_sum_adamw(blocks("down"), w_ffn_down, m_w_ffn_down, v_w_ffn_down, tr=176,
                                             name="adamw_w_ffn_down")[0])

    sheet, dws0, dws1, dbs, gain0 = _sum_gathered(gathered_small, dg1_first, name="sum_small_grads")
    swap_taps = lambda t: t.transpose(1, 0, 2)
    taps = lambda row, width: jnp.stack([sheet[SG_LAYER * l + row:SG_LAYER * l + row + 3, :width] for l in range(DEPTH)], axis=1)
    extra = dict(w_spatial=jnp.stack([dws0, dws1]), b_spatial=dbs.reshape(DEPTH, N_HEADS, GMLP_BLOCK),
                 w_shortconv=lax.dynamic_slice_in_dim(taps(ROW_SCONV, D_B), my_idx * sc_w, sc_w, axis=2),
                 w_ffn_conv=lax.dynamic_slice_in_dim(taps(ROW_FCONV, D_FF), my_idx * fc_w, fc_w, axis=2))
    small_w = dict(norm1_g=(norm1_g, m_norm1_g, v_norm1_g), b_gate=(b_gate, m_b_gate, v_b_gate),
                   gmlp_ln_g=(gmlp_ln_g, m_gmlp_ln_g, v_gmlp_ln_g), gmlp_ln_b=(gmlp_ln_b, m_gmlp_ln_b, v_gmlp_ln_b),
                   w_spatial=(w_spatial, m_w_spatial, v_w_spatial), b_spatial=(b_spatial, m_b_spatial, v_b_spatial),
                   w_shortconv=tuple(swap_taps(t) for t in (w_shortconv, m_w_shortconv, v_w_shortconv)), norm2_g=(norm2_g, m_norm2_g, v_norm2_g),
                   w_ffn_conv=tuple(swap_taps(t) for t in (w_ffn_conv, m_w_ffn_conv, v_w_ffn_conv)), b_ffn_conv=(b_ffn_conv, m_b_ffn_conv, v_b_ffn_conv),
                   final_g=tuple(t.reshape(1, D_MODEL) for t in (final_g, m_final_g, v_final_g)))
    small, loss = _adamw_small(sheet, gain0, extra, small_w, name="adamw_small")
    loss = loss.reshape(())
    results.update(small)
    results["final_g"] = tuple(t.reshape(D_MODEL) for t in results["final_g"])
    for key in ("w_shortconv", "w_ffn_conv"):
        results[key] = tuple(swap_taps(t) for t in results[key])

    names = ["norm1_g", "w_in", "b_gate", "gmlp_ln_g", "gmlp_ln_b", "w_spatial", "b_spatial", "w_shortconv", "w_branch",
             "w_out", "norm2_g", "w_ffn_up", "w_ffn_conv", "b_ffn_conv", "w_ffn_down", "final_g"]
    return (loss, grad_x, *[results[k][0] for k in names], *[results[k][1] for k in names],
            *[results[k][2] for k in names], *[results[k][3] for k in names])
```
